```python
import jax, jax.numpy as jnp
from jax import lax
import numpy as np

D_MODEL = 2048
BATCH = 8
SEQ = 4096
DEPTH = 2

N_MEM = 256
N_MIXERS = 2
N_A = (DEPTH + 1) // 2
N_B = DEPTH // 2
E_BRANCH = 2 * D_MODEL
E_CA = E_BRANCH // 4
E_MIX = E_BRANCH - E_CA
CA_HEADS = 4
CA_HEAD_DIM = E_CA // CA_HEADS
POOL_WINDOWS = (2, 4, 8, 16)
N_POOL_GROUPS = len(POOL_WINDOWS)
POOL_GROUP = E_MIX // N_POOL_GROUPS
HG_HEAD_DIM = 128
HG_HEADS = E_MIX // HG_HEAD_DIM
HG_CHUNK = 64
EPS = 1e-6

kernel_name = "interleaved_pool_hgrn2_memory_hybrid"


def rmsnorm(x, g):
    xf = x.astype(jnp.float32)
    y = xf * lax.rsqrt(jnp.mean(xf * xf, axis=-1, keepdims=True) + EPS)
    return (y * g.astype(jnp.float32)).astype(x.dtype)


def pool_mixer(u, w_grp, scale):
    B, S, _ = u.shape
    wmax = max(POOL_WINDOWS)
    uf = u.astype(jnp.float32)
    c = jnp.cumsum(uf, axis=1)
    c_pad = jnp.pad(c, ((0, 0), (wmax, 0), (0, 0)))
    pos = jnp.arange(S, dtype=jnp.float32)[:, None]
    pooled = []
    for j, w in enumerate(POOL_WINDOWS):
        lo, hi = j * POOL_GROUP, (j + 1) * POOL_GROUP
        win_sum = c[:, :, lo:hi] - c_pad[:, wmax - w: wmax - w + S, lo:hi]
        count = jnp.minimum(pos + 1.0, float(w))
        pooled.append(win_sum / count - uf[:, :, lo:hi])
    pooled = jnp.stack(pooled, axis=2).astype(u.dtype)
    y = jnp.einsum('bsgc,gcd->bsgd', pooled, w_grp)
    return y.reshape(B, S, E_MIX) * scale


def hgrn2_mixer(q_in, f_in, i_in, lb, norm_g):
    B, S, _ = q_in.shape
    N = S // HG_CHUNK
    f32 = jnp.float32

    def heads(t):
        return t.reshape(B, N, HG_CHUNK, HG_HEADS, HG_HEAD_DIM).transpose(0, 3, 1, 2, 4)

    q = heads(jax.nn.silu(q_in.astype(f32))) * (HG_HEAD_DIM ** -0.5)
    f = lb + (1.0 - lb) * jax.nn.sigmoid(f_in.astype(f32))
    k = heads(1.0 - f)
    b = jnp.cumsum(heads(jnp.log(f)), axis=3)
    v = heads(i_in.astype(f32))
    b_last = b[:, :, :, -1:, :]

    q_dec = q * jnp.exp(b)
    k_inv = k * jnp.exp(-b)
    k_to_end = k * jnp.exp(b_last - b)

    causal = jnp.tril(jnp.ones((HG_CHUNK, HG_CHUNK), dtype=bool))
    attn = jnp.einsum('bhnck,bhnsk->bhncs', q_dec, k_inv)
    attn = jnp.where(causal, attn, 0.0)
    o_intra = jnp.einsum('bhncs,bhnsv->bhncv', attn, v)

    def step(state, xs):
        qd, kte, vv, dec = xs
        o = jnp.einsum('bhck,bhkv->bhcv', qd, state)
        state = dec[..., None] * state + jnp.einsum('bhck,bhcv->bhkv', kte, vv)
        return state, o

    mv = lambda t: jnp.moveaxis(t, 2, 0)
    s0 = jnp.zeros((B, HG_HEADS, HG_HEAD_DIM, HG_HEAD_DIM), f32)
    _, o_inter = lax.scan(step, s0, (mv(q_dec), mv(k_to_end), mv(v), mv(jnp.exp(b_last[:, :, :, 0, :]))))
    o = o_intra + jnp.moveaxis(o_inter, 0, 2)
    o = o * lax.rsqrt(jnp.mean(o * o, axis=-1, keepdims=True) + EPS)
    o = o.transpose(0, 2, 3, 1, 4).reshape(B, S, E_MIX) * norm_g.astype(f32)
    return o.astype(q_in.dtype)


def memory_attention(q_in, mem_n, w_kv):
    B, S, _ = q_in.shape
    M = mem_n.shape[1]
    kv = jnp.einsum('bmd,de->bme', mem_n, w_kv)
    k, v = jnp.split(kv, 2, axis=-1)
    q = q_in.reshape(B, S, CA_HEADS, CA_HEAD_DIM)
    k = k.reshape(B, M, CA_HEADS, CA_HEAD_DIM)
    v = v.reshape(B, M, CA_HEADS, CA_HEAD_DIM)
    s = jnp.einsum('bshd,bmhd->bhsm', q, k).astype(jnp.float32) * (CA_HEAD_DIM ** -0.5)
    p = jax.nn.softmax(s, axis=-1).astype(v.dtype)
    o = jnp.einsum('bhsm,bmhd->bshd', p, v)
    return o.reshape(B, S, E_CA)


def _fwd_setup_inputs(seed: int = 0) -> dict:
    key = jax.random.key(seed)
    ks = jax.random.split(key, 14)
    f32 = jnp.float32
    nrm = lambda k, shape, s: jax.random.normal(k, shape, f32) * s
    d_in_pool = E_MIX + E_CA + E_BRANCH
    d_in_hg = 3 * E_MIX + E_CA + E_BRANCH
    return {
        "x": nrm(ks[0], (BATCH, SEQ, D_MODEL), 1.0),
        "mem": nrm(ks[1], (BATCH, N_MEM, D_MODEL), 1.0),
        "norm_g": 1.0 + nrm(ks[2], (DEPTH, D_MODEL), 0.02),
        "mem_norm_g": 1.0 + nrm(ks[3], (D_MODEL,), 0.02),
        "w_kv": nrm(ks[4], (DEPTH, D_MODEL, 2 * E_CA), D_MODEL ** -0.5),
        "w_out": nrm(ks[5], (DEPTH, E_BRANCH, D_MODEL), E_BRANCH ** -0.5),
        "pool_w_in": nrm(ks[6], (N_A, D_MODEL, d_in_pool), D_MODEL ** -0.5),
        "pool_w_grp": nrm(ks[7], (N_A, N_POOL_GROUPS, POOL_GROUP, POOL_GROUP), POOL_GROUP ** -0.5),
        "pool_scale": 1.0 + nrm(ks[8], (N_A, E_MIX), 0.1),
        "hgrn_w_in": nrm(ks[9], (N_B, D_MODEL, d_in_hg), D_MODEL ** -0.5),
        "hgrn_lb": nrm(ks[10], (DEPTH, E_MIX), 0.1),
        "hgrn_norm_g": 1.0 + nrm(ks[11], (N_B, E_MIX), 0.02),
        "final_g": 1.0 + nrm(ks[12], (D_MODEL,), 0.02),
    }


def _fwd_reference(x, mem, norm_g, mem_norm_g, w_kv, w_out, pool_w_in, pool_w_grp, pool_scale,
              hgrn_w_in, hgrn_lb, hgrn_norm_g, final_g):
    mem_n = rmsnorm(mem, mem_norm_g)
    sm = jax.nn.softmax(hgrn_lb.astype(jnp.float32), axis=0)
    lb_all = jnp.cumsum(sm, axis=0) - sm[0:1]
    for i in range(DEPTH):
        h = rmsnorm(x, norm_g[i])
        j = i // N_MIXERS
        if i % N_MIXERS == 0:
            proj = jnp.einsum('bsd,de->bse', h, pool_w_in[j])
            u, q_ca, gate = jnp.split(proj, [E_MIX, E_MIX + E_CA], axis=-1)
            mix = pool_mixer(u, pool_w_grp[j], pool_scale[j])
        else:
            proj = jnp.einsum('bsd,de->bse', h, hgrn_w_in[j])
            q_hg, f_hg, i_hg, q_ca, gate = jnp.split(
                proj, [E_MIX, 2 * E_MIX, 3 * E_MIX, 3 * E_MIX + E_CA], axis=-1)
            mix = hgrn2_mixer(q_hg, f_hg, i_hg, lb_all[i], hgrn_norm_g[j])
        ca = memory_attention(q_ca, mem_n, w_kv[i])
        branch = jnp.concatenate([mix, ca], axis=-1) * jax.nn.silu(gate)
        x = x + jnp.einsum('bse,ed->bsd', branch, w_out[i])
    return rmsnorm(x, final_g)


import jax as _jax
import jax.numpy as _jnp

TWIN_FORMAT = 'train_step'
FWD_PARAMS = ['x', 'mem', 'norm_g', 'mem_norm_g', 'w_kv', 'w_out', 'pool_w_in', 'pool_w_grp', 'pool_scale', 'hgrn_w_in', 'hgrn_lb', 'hgrn_norm_g', 'final_g']
TWIN_WEIGHTS = ['norm_g', 'mem_norm_g', 'w_kv', 'w_out', 'pool_w_in', 'pool_w_grp', 'pool_scale', 'hgrn_w_in', 'hgrn_lb', 'hgrn_norm_g', 'final_g']
TWIN_DIFF_INPUT = 'x'
TWIN_INPUTS = ['x', 'mem', 'norm_g', 'mem_norm_g', 'w_kv', 'w_out', 'pool_w_in', 'pool_w_grp', 'pool_scale', 'hgrn_w_in', 'hgrn_lb', 'hgrn_norm_g', 'final_g', 'loss_target', 'm_norm_g', 'm_mem_norm_g', 'm_w_kv', 'm_w_out', 'm_pool_w_in', 'm_pool_w_grp', 'm_pool_scale', 'm_hgrn_w_in', 'm_hgrn_lb', 'm_hgrn_norm_g', 'm_final_g', 'v_norm_g', 'v_mem_norm_g', 'v_w_kv', 'v_w_out', 'v_pool_w_in', 'v_pool_w_grp', 'v_pool_scale', 'v_hgrn_w_in', 'v_hgrn_lb', 'v_hgrn_norm_g', 'v_final_g']
TWIN_OUTPUTS = ['loss', 'grad_x', 'grad_norm_g', 'grad_mem_norm_g', 'grad_w_kv', 'grad_w_out', 'grad_pool_w_in', 'grad_pool_w_grp', 'grad_pool_scale', 'grad_hgrn_w_in', 'grad_hgrn_lb', 'grad_hgrn_norm_g', 'grad_final_g', 'delta_norm_g', 'delta_mem_norm_g', 'delta_w_kv', 'delta_w_out', 'delta_pool_w_in', 'delta_pool_w_grp', 'delta_pool_scale', 'delta_hgrn_w_in', 'delta_hgrn_lb', 'delta_hgrn_norm_g', 'delta_final_g', 'new_m_norm_g', 'new_m_mem_norm_g', 'new_m_w_kv', 'new_m_w_out', 'new_m_pool_w_in', 'new_m_pool_w_grp', 'new_m_pool_scale', 'new_m_hgrn_w_in', 'new_m_hgrn_lb', 'new_m_hgrn_norm_g', 'new_m_final_g', 'new_v_norm_g', 'new_v_mem_norm_g', 'new_v_w_kv', 'new_v_w_out', 'new_v_pool_w_in', 'new_v_pool_w_grp', 'new_v_pool_scale', 'new_v_hgrn_w_in', 'new_v_hgrn_lb', 'new_v_hgrn_norm_g', 'new_v_final_g']
TWIN_LEAF_KINDS = {'loss': 'loss', 'grad_x': 'grad_x', 'grad_norm_g': 'grad_w', 'grad_mem_norm_g': 'grad_w', 'grad_w_kv': 'grad_w', 'grad_w_out': 'grad_w', 'grad_pool_w_in': 'grad_w', 'grad_pool_w_grp': 'grad_w', 'grad_pool_scale': 'grad_w', 'grad_hgrn_w_in': 'grad_w', 'grad_hgrn_lb': 'grad_w', 'grad_hgrn_norm_g': 'grad_w', 'grad_final_g': 'grad_w', 'delta_norm_g': 'delta_w', 'delta_mem_norm_g': 'delta_w', 'delta_w_kv': 'delta_w', 'delta_w_out': 'delta_w', 'delta_pool_w_in': 'delta_w', 'delta_pool_w_grp': 'delta_w', 'delta_pool_scale': 'delta_w', 'delta_hgrn_w_in': 'delta_w', 'delta_hgrn_lb': 'delta_w', 'delta_hgrn_norm_g': 'delta_w', 'delta_final_g': 'delta_w', 'new_m_norm_g': 'new_m', 'new_m_mem_norm_g': 'new_m', 'new_m_w_kv': 'new_m', 'new_m_w_out': 'new_m', 'new_m_pool_w_in': 'new_m', 'new_m_pool_w_grp': 'new_m', 'new_m_pool_scale': 'new_m', 'new_m_hgrn_w_in': 'new_m', 'new_m_hgrn_lb': 'new_m', 'new_m_hgrn_norm_g': 'new_m', 'new_m_final_g': 'new_m', 'new_v_norm_g': 'new_v', 'new_v_mem_norm_g': 'new_v', 'new_v_w_kv': 'new_v', 'new_v_w_out': 'new_v', 'new_v_pool_w_in': 'new_v', 'new_v_pool_w_grp': 'new_v', 'new_v_pool_scale': 'new_v', 'new_v_hgrn_w_in': 'new_v', 'new_v_hgrn_lb': 'new_v', 'new_v_hgrn_norm_g': 'new_v', 'new_v_final_g': 'new_v'}


def _forward(args):
    return _fwd_reference(*[args[k] for k in FWD_PARAMS])


def _output_shape():
    def fwd():
        inp = _fwd_setup_inputs(0)
        return _fwd_reference(*[inp[k] for k in FWD_PARAMS])
    out = _jax.eval_shape(fwd)
    return out.shape, out.dtype

N_MICROBATCH = 1
ADAM_LR = 0.001
ADAM_B1 = 0.9
ADAM_B2 = 0.999
ADAM_EPS = 1e-08
ADAM_WD = 0.01
ADAM_STEP = 10
PER_EXAMPLE_BATCH_AXIS = {'x': 0, 'mem': 0, 'loss_target': 0}
SHARED_INPUTS = []
_WEIGHT_DTYPES = {'norm_g': _jnp.float32, 'mem_norm_g': _jnp.float32, 'w_kv': _jnp.float32, 'w_out': _jnp.float32, 'pool_w_in': _jnp.float32, 'pool_w_grp': _jnp.float32, 'pool_scale': _jnp.float32, 'hgrn_w_in': _jnp.float32, 'hgrn_lb': _jnp.float32, 'hgrn_norm_g': _jnp.float32, 'final_g': _jnp.float32}
MOMENT_SCALE = {'norm_g': 5.645133e-02, 'mem_norm_g': 5.227697e-03, 'w_kv': 3.568947e-03, 'w_out': 3.909267e-02, 'pool_w_in': 2.909107e-02, 'pool_w_grp': 3.295944e-02, 'pool_scale': 3.380864e-02, 'hgrn_w_in': 2.069346e-02, 'hgrn_lb': 2.900585e-03, 'hgrn_norm_g': 3.072261e-02, 'final_g': 1.599698e+01}


def _to_microbatches(a, axis):
    t = _jnp.moveaxis(a, axis, 0)
    t = t.reshape((N_MICROBATCH, t.shape[0] // N_MICROBATCH) + t.shape[1:])
    return _jnp.moveaxis(t, 1, axis + 1)


def setup_inputs(seed: int = 0) -> dict:
    inp = _fwd_setup_inputs(seed)
    key = _jax.random.fold_in(_jax.random.key(seed), 7919)
    shape, _ = _output_shape()
    out = dict(inp)
    out["loss_target"] = _jax.random.normal(_jax.random.fold_in(key, 0), shape, _jnp.float32)
    for i, name in enumerate(TWIN_WEIGHTS):
        w = inp[name].astype(_jnp.float32)
        if MOMENT_SCALE is None:
            s = _jnp.sqrt(_jnp.mean(_jnp.square(w)) + 1e-30)
        else:
            s = MOMENT_SCALE[name]
        km, kv = _jax.random.split(_jax.random.fold_in(key, i + 1))
        out[name] = w
        out["m_" + name] = s * _jax.random.normal(km, w.shape, _jnp.float32)
        out["v_" + name] = (s * s) * _jax.random.uniform(kv, w.shape, _jnp.float32, 0.5, 1.5)
    if N_MICROBATCH > 1:
        for name, axis in PER_EXAMPLE_BATCH_AXIS.items():
            out[name] = _to_microbatches(out[name], axis)
    return {'x': out['x'], 'mem': out['mem'], 'norm_g': out['norm_g'], 'mem_norm_g': out['mem_norm_g'], 'w_kv': out['w_kv'], 'w_out': out['w_out'], 'pool_w_in': out['pool_w_in'], 'pool_w_grp': out['pool_w_grp'], 'pool_scale': out['pool_scale'], 'hgrn_w_in': out['hgrn_w_in'], 'hgrn_lb': out['hgrn_lb'], 'hgrn_norm_g': out['hgrn_norm_g'], 'final_g': out['final_g'], 'loss_target': out['loss_target'], 'm_norm_g': out['m_norm_g'], 'm_mem_norm_g': out['m_mem_norm_g'], 'm_w_kv': out['m_w_kv'], 'm_w_out': out['m_w_out'], 'm_pool_w_in': out['m_pool_w_in'], 'm_pool_w_grp': out['m_pool_w_grp'], 'm_pool_scale': out['m_pool_scale'], 'm_hgrn_w_in': out['m_hgrn_w_in'], 'm_hgrn_lb': out['m_hgrn_lb'], 'm_hgrn_norm_g': out['m_hgrn_norm_g'], 'm_final_g': out['m_final_g'], 'v_norm_g': out['v_norm_g'], 'v_mem_norm_g': out['v_mem_norm_g'], 'v_w_kv': out['v_w_kv'], 'v_w_out': out['v_w_out'], 'v_pool_w_in': out['v_pool_w_in'], 'v_pool_w_grp': out['v_pool_w_grp'], 'v_pool_scale': out['v_pool_scale'], 'v_hgrn_w_in': out['v_hgrn_w_in'], 'v_hgrn_lb': out['v_hgrn_lb'], 'v_hgrn_norm_g': out['v_hgrn_norm_g'], 'v_final_g': out['v_final_g']}


def _loss(weights, diff, rest, loss_target):
    with _jax.named_scope("forward"):
        args = {**rest, TWIN_DIFF_INPUT: diff, **{k: w.astype(_WEIGHT_DTYPES[k]) for k, w in weights.items()}}
        y = _forward(args)
    with _jax.named_scope("loss_head"):
        err = _jnp.square(y.astype(_jnp.float32) - loss_target)
        return 0.5 * _jnp.sum(_jnp.mean(err, axis=-1)) if err.ndim else 0.5 * err


def _adamw(w, g, m, v):
    m = ADAM_B1 * m + (1.0 - ADAM_B1) * g
    v = ADAM_B2 * v + (1.0 - ADAM_B2) * _jnp.square(g)
    m_hat = m / (1.0 - ADAM_B1 ** ADAM_STEP)
    v_hat = v / (1.0 - ADAM_B2 ** ADAM_STEP)
    delta = -ADAM_LR * (m_hat / (_jnp.sqrt(v_hat) + ADAM_EPS) + ADAM_WD * w)
    return delta, m, v


def reference(x, mem, norm_g, mem_norm_g, w_kv, w_out, pool_w_in, pool_w_grp, pool_scale, hgrn_w_in, hgrn_lb, hgrn_norm_g, final_g, loss_target, m_norm_g, m_mem_norm_g, m_w_kv, m_w_out, m_pool_w_in, m_pool_w_grp, m_pool_scale, m_hgrn_w_in, m_hgrn_lb, m_hgrn_norm_g, m_final_g, v_norm_g, v_mem_norm_g, v_w_kv, v_w_out, v_pool_w_in, v_pool_w_grp, v_pool_scale, v_hgrn_w_in, v_hgrn_lb, v_hgrn_norm_g, v_final_g):
    given = dict(x=x, mem=mem, norm_g=norm_g, mem_norm_g=mem_norm_g, w_kv=w_kv, w_out=w_out, pool_w_in=pool_w_in, pool_w_grp=pool_w_grp, pool_scale=pool_scale, hgrn_w_in=hgrn_w_in, hgrn_lb=hgrn_lb, hgrn_norm_g=hgrn_norm_g, final_g=final_g, loss_target=loss_target, m_norm_g=m_norm_g, m_mem_norm_g=m_mem_norm_g, m_w_kv=m_w_kv, m_w_out=m_w_out, m_pool_w_in=m_pool_w_in, m_pool_w_grp=m_pool_w_grp, m_pool_scale=m_pool_scale, m_hgrn_w_in=m_hgrn_w_in, m_hgrn_lb=m_hgrn_lb, m_hgrn_norm_g=m_hgrn_norm_g, m_final_g=m_final_g, v_norm_g=v_norm_g, v_mem_norm_g=v_mem_norm_g, v_w_kv=v_w_kv, v_w_out=v_w_out, v_pool_w_in=v_pool_w_in, v_pool_w_grp=v_pool_w_grp, v_pool_scale=v_pool_scale, v_hgrn_w_in=v_hgrn_w_in, v_hgrn_lb=v_hgrn_lb, v_hgrn_norm_g=v_hgrn_norm_g, v_final_g=v_final_g)
    weights = {n: given[n] for n in TWIN_WEIGHTS}
    shared = {n: given[n] for n in SHARED_INPUTS}
    per_example = {n: given[n] for n in ['x', 'mem']}
    grad_fn = _jax.value_and_grad(_loss, argnums=(0, 1))

    def one_microbatch(ex, loss_target):
        ex = dict(ex)
        diff = ex.pop(TWIN_DIFF_INPUT)
        return grad_fn(weights, diff, {**shared, **ex}, loss_target)

    if N_MICROBATCH == 1:
        loss, (grad_w, grad_x) = one_microbatch(per_example, given["loss_target"])
    else:
        def body(carry, xs):
            loss_sum, grad_sum = carry
            l_k, (gw_k, gx_k) = one_microbatch(xs[0], xs[1])
            with _jax.named_scope("update"):
                return (loss_sum + l_k, _jax.tree.map(_jnp.add, grad_sum, gw_k)), gx_k

        init = (_jnp.zeros((), _jnp.float32), _jax.tree.map(_jnp.zeros_like, weights))
        (loss, grad_w), grad_x = _jax.lax.scan(body, init, (per_example, given["loss_target"]))
    with _jax.named_scope("update"):
        delta_w, new_m, new_v = {}, {}, {}
        for n in TWIN_WEIGHTS:
            delta_w[n], new_m[n], new_v[n] = _adamw(weights[n], grad_w[n], given["m_" + n], given["v_" + n])
    return (loss, grad_x, *[grad_w[n] for n in TWIN_WEIGHTS], *[delta_w[n] for n in TWIN_WEIGHTS],
            *[new_m[n] for n in TWIN_WEIGHTS], *[new_v[n] for n in TWIN_WEIGHTS])
```

```python
import functools

import jax
import jax.numpy as jnp
from jax import lax
from jax.experimental import pallas as pl
from jax.experimental.pallas import tpu as pltpu

F32 = jnp.float32
BF16 = jnp.bfloat16
MESH = pl.DeviceIdType.MESH
ANY = pl.BlockSpec(memory_space=pl.ANY)

EPS = 1e-6
HG_HEAD_DIM = 128
HG_CHUNK = 64
CA_HEADS = 4
N_POOL_GROUPS = 4
POOL_HALO = 128
ADAM_LR = 0.001
ADAM_B1 = 0.9
ADAM_B2 = 0.999
ADAM_EPS = 1e-08
ADAM_WD = 0.01
ADAM_STEP = 10
N_CHIPS = 4
N_DEV = 8
VMEM_LIMIT_BYTES = 56 * 1024 * 1024
SMALL_ROWS = 8


def _params(*sem):
    return pltpu.CompilerParams(dimension_semantics=sem, vmem_limit_bytes=VMEM_LIMIT_BYTES)


def _tile(n, pref):
    t = pref
    while n % t:
        t //= 2
    return t


def _sigmoid(x):
    return 1.0 / (1.0 + jnp.exp(-x))


def _matmul(name, a, b, *, grid, a_spec, b_spec, out_shape, out_spec, acc_shape, dims,
            add=None, add_spec=None, alias=None):
    nk = grid[2]
    has_add = add is not None
    has_alias = alias is not None

    def body(*refs):
        a_ref, b_ref = refs[0], refs[1]
        pos = 2
        add_ref = None
        if has_add:
            add_ref = refs[pos]
            pos += 1
        if has_alias:
            pos += 1
        o_ref, acc_ref = refs[pos], refs[pos + 1]
        k = pl.program_id(2)

        @pl.when(k == 0)
        def _():
            acc_ref[...] = jnp.zeros_like(acc_ref)

        acc_ref[...] += lax.dot_general(a_ref[...], b_ref[...], (dims, ((), ())), preferred_element_type=F32)

        @pl.when(k == nk - 1)
        def _():
            r = acc_ref[...]
            if has_add:
                r = r + add_ref[...].astype(F32)
            o_ref[...] = r.astype(o_ref.dtype)

    operands = [a, b]
    in_specs = [a_spec, b_spec]
    if has_add:
        operands.append(add)
        in_specs.append(add_spec)
    aliases = {}
    if has_alias:
        aliases = {len(operands): 0}
        operands.append(alias)
        in_specs.append(ANY)
    return pl.pallas_call(
        body, name=name, grid=grid, in_specs=in_specs, out_specs=out_spec, out_shape=out_shape,
        scratch_shapes=[pltpu.VMEM(acc_shape, F32)], input_output_aliases=aliases,
        compiler_params=_params("parallel", "parallel", "arbitrary"),
    )(*operands)


IJ = lambda i, j, k: (i, j)
IK = lambda i, j, k: (i, k)
KJ = lambda i, j, k: (k, j)
KI = lambda i, j, k: (k, i)
NN = ((1,), (0,))
NT = ((1,), (1,))
TN = ((0,), (0,))


def _rms_fwd(name, x, g):
    R, D = x.shape
    tr = _tile(R, 256)

    def body(x_ref, g_ref, o_ref):
        xf = x_ref[...]
        r = lax.rsqrt(jnp.mean(xf * xf, axis=-1, keepdims=True) + EPS)
        o_ref[...] = (xf * r * g_ref[...]).astype(o_ref.dtype)

    return pl.pallas_call(
        body, name=name, grid=(R // tr,),
        in_specs=[pl.BlockSpec((tr, D), lambda i: (i, 0)), pl.BlockSpec((1, D), lambda i: (0, 0))],
        out_specs=pl.BlockSpec((tr, D), lambda i: (i, 0)),
        out_shape=jax.ShapeDtypeStruct((R, D), BF16), compiler_params=_params("parallel"),
    )(x, g)


def _rms_bwd(name, dh, x, g, dres):
    R, D = x.shape
    tr = _tile(R, 256)

    def body(dh_ref, x_ref, g_ref, dres_ref, dx_ref, dxb_ref, dg_ref):
        xf = x_ref[...]
        r = lax.rsqrt(jnp.mean(xf * xf, axis=-1, keepdims=True) + EPS)
        xn = xf * r
        d = dh_ref[...]
        dyg = d * g_ref[...]
        dx = r * (dyg - xn * jnp.mean(dyg * xn, axis=-1, keepdims=True)) + dres_ref[...]
        dx_ref[...] = dx
        dxb_ref[...] = dx.astype(BF16)

        @pl.when(pl.program_id(0) == 0)
        def _():
            dg_ref[...] = jnp.zeros_like(dg_ref)

        dg_ref[...] += jnp.sum(d * xn, axis=0, keepdims=True)

    row = pl.BlockSpec((tr, D), lambda i: (i, 0))
    vec = pl.BlockSpec((1, D), lambda i: (0, 0))
    return pl.pallas_call(
        body, name=name, grid=(R // tr,), in_specs=[row, row, vec, row], out_specs=[row, row, vec],
        out_shape=[jax.ShapeDtypeStruct((R, D), F32), jax.ShapeDtypeStruct((R, D), BF16),
                   jax.ShapeDtypeStruct((1, D), F32)],
        compiler_params=_params("arbitrary"),
    )(dh, x, g, dres)


def _loss_head(x2, g, target):
    R, D = x2.shape
    tr = _tile(R, 256)

    def body(x_ref, g_ref, t_ref, dx_ref, dxb_ref, dg_ref, loss_ref):
        xf = x_ref[...]
        gg = g_ref[...]
        r = lax.rsqrt(jnp.mean(xf * xf, axis=-1, keepdims=True) + EPS)
        xn = xf * r
        e = xn * gg - t_ref[...]
        part = 0.5 * jnp.sum(jnp.mean(e * e, axis=-1, keepdims=True), axis=0, keepdims=True)
        dy = e * (1.0 / D)
        dyg = dy * gg
        dx = r * (dyg - xn * jnp.mean(dyg * xn, axis=-1, keepdims=True))
        dx_ref[...] = dx
        dxb_ref[...] = dx.astype(BF16)

        @pl.when(pl.program_id(0) == 0)
        def _():
            dg_ref[...] = jnp.zeros_like(dg_ref)
            loss_ref[...] = jnp.zeros_like(loss_ref)

        dg_ref[...] += jnp.sum(dy * xn, axis=0, keepdims=True)
        loss_ref[...] += jnp.broadcast_to(part, loss_ref.shape)

    row = pl.BlockSpec((tr, D), lambda i: (i, 0))
    vec = pl.BlockSpec((1, D), lambda i: (0, 0))
    return pl.pallas_call(
        body, name="loss_head", grid=(R // tr,), in_specs=[row, vec, row],
        out_specs=[row, row, vec, pl.BlockSpec((1, 128), lambda i: (0, 0))],
        out_shape=[jax.ShapeDtypeStruct((R, D), F32), jax.ShapeDtypeStruct((R, D), BF16),
                   jax.ShapeDtypeStruct((1, D), F32), jax.ShapeDtypeStruct((1, 128), F32)],
        compiler_params=_params("arbitrary"),
    )(x2, g, target)


def _pool_band(tr, reverse, w):
    r = lax.broadcasted_iota(jnp.int32, (tr, tr + POOL_HALO), 0)
    c = lax.broadcasted_iota(jnp.int32, (tr, tr + POOL_HALO), 1)
    if reverse:
        inside = (c >= r) & (c < r + w)
    else:
        cc = c - POOL_HALO
        inside = (cc <= r) & (cc > r - w)
    return jnp.where(inside, 1.0, 0.0).astype(BF16)


def _pool_fwd(proj, S, EMIX):
    PG = EMIX // N_POOL_GROUPS
    cb = _tile(PG, 256)
    tr = _tile(S, 256)
    per_group = PG // cb

    def body(u_ref, o_ref, ext):
        i = pl.program_id(1)
        w = jnp.left_shift(2, pl.program_id(0) // per_group)

        @pl.when(i == 0)
        def _():
            ext[0:POOL_HALO, :] = jnp.zeros((POOL_HALO, cb), BF16)

        u = u_ref[...]
        ext[POOL_HALO:, :] = u
        win = jnp.dot(_pool_band(tr, False, w), ext[...], preferred_element_type=F32)
        pos = i * tr + lax.broadcasted_iota(jnp.int32, (tr, 1), 0)
        cnt = jnp.minimum(pos + 1, w).astype(F32)
        o_ref[...] = (win / cnt - u.astype(F32)).astype(BF16)
        ext[0:POOL_HALO, :] = u[tr - POOL_HALO:, :]

    return pl.pallas_call(
        body, name="pool_fwd", grid=(EMIX // cb, S // tr),
        in_specs=[pl.BlockSpec((tr, cb), lambda j, i: (i, j))],
        out_specs=pl.BlockSpec((tr, cb), lambda j, i: (i, j)),
        out_shape=jax.ShapeDtypeStruct((S, EMIX), BF16),
        scratch_shapes=[pltpu.VMEM((tr + POOL_HALO, cb), BF16)],
        compiler_params=_params("parallel", "arbitrary"),
    )(proj)


def _pool_bwd(dpooled, dproj, S, EMIX):
    PG = EMIX // N_POOL_GROUPS
    cb = _tile(PG, 256)
    tr = _tile(S, 256)
    per_group = PG // cb
    nrt = S // tr

    def body(d_ref, _, o_ref, ext):
        step = pl.program_id(1)
        i = nrt - 1 - step
        w = jnp.left_shift(2, pl.program_id(0) // per_group)

        @pl.when(step == 0)
        def _():
            ext[tr:, :] = jnp.zeros((POOL_HALO, cb), BF16)

        d = d_ref[...]
        pos = i * tr + lax.broadcasted_iota(jnp.int32, (tr, 1), 0)
        cnt = jnp.minimum(pos + 1, w).astype(F32)
        z = (d / cnt).astype(BF16)
        ext[0:tr, :] = z
        win = jnp.dot(_pool_band(tr, True, w), ext[...], preferred_element_type=F32)
        o_ref[...] = (win - d).astype(BF16)
        ext[tr:, :] = z[0:POOL_HALO, :]

    return pl.pallas_call(
        body, name="pool_bwd", grid=(EMIX // cb, nrt),
        in_specs=[pl.BlockSpec((tr, cb), lambda j, s: (nrt - 1 - s, j)), ANY],
        out_specs=pl.BlockSpec((tr, cb), lambda j, s: (nrt - 1 - s, j)),
        out_shape=jax.ShapeDtypeStruct(dproj.shape, dproj.dtype),
        scratch_shapes=[pltpu.VMEM((tr + POOL_HALO, cb), BF16)],
        input_output_aliases={1: 0},
        compiler_params=_params("parallel", "arbitrary"),
    )(dpooled, dproj)


def _ca_fwd(name, proj, qblk, kv, premix, S, ECA, EMIX):
    M = kv.shape[0]
    hd = ECA // CA_HEADS
    ts = _tile(S, 512)
    scale = hd ** -0.5

    def body(q_ref, kv_ref, _, o_ref):
        for h in range(CA_HEADS):
            q = q_ref[:, h * hd:(h + 1) * hd]
            k = kv_ref[:, h * hd:(h + 1) * hd]
            v = kv_ref[:, ECA + h * hd:ECA + (h + 1) * hd]
            s = lax.dot_general(q, k, (NT, ((), ())), preferred_element_type=F32) * scale
            s = s - jnp.max(s, axis=-1, keepdims=True)
            p = jnp.exp(s)
            p = p / jnp.sum(p, axis=-1, keepdims=True)
            o = jnp.dot(p.astype(BF16), v, preferred_element_type=F32)
            o_ref[:, h * hd:(h + 1) * hd] = o.astype(BF16)

    return pl.pallas_call(
        body, name=name, grid=(S // ts,),
        in_specs=[pl.BlockSpec((ts, ECA), lambda i: (i, qblk)), pl.BlockSpec((M, 2 * ECA), lambda i: (0, 0)), ANY],
        out_specs=pl.BlockSpec((ts, ECA), lambda i: (i, EMIX // ECA)),
        out_shape=jax.ShapeDtypeStruct(premix.shape, premix.dtype),
        input_output_aliases={2: 0}, compiler_params=_params("parallel"),
    )(proj, kv, premix)


def _ca_bwd(name, dpremix, proj, qblk, kv, dbuf, dblk, S, ECA, EMIX):
    M = kv.shape[0]
    hd = ECA // CA_HEADS
    ts = _tile(S, 512)
    scale = hd ** -0.5

    def body(do_ref, q_ref, kv_ref, _, dq_ref, dkv_ref):
        @pl.when(pl.program_id(0) == 0)
        def _():
            dkv_ref[...] = jnp.zeros_like(dkv_ref)

        for h in range(CA_HEADS):
            lo, hi = h * hd, (h + 1) * hd
            q = q_ref[:, lo:hi]
            k = kv_ref[:, lo:hi]
            v = kv_ref[:, ECA + lo:ECA + hi]
            do = do_ref[:, lo:hi]
            s = lax.dot_general(q, k, (NT, ((), ())), preferred_element_type=F32) * scale
            s = s - jnp.max(s, axis=-1, keepdims=True)
            p = jnp.exp(s)
            p = p / jnp.sum(p, axis=-1, keepdims=True)
            pb = p.astype(BF16)
            dkv_ref[:, ECA + lo:ECA + hi] += lax.dot_general(pb, do, (TN, ((), ())), preferred_element_type=F32)
            dp = lax.dot_general(do, v, (NT, ((), ())), preferred_element_type=F32)
            ds = (p * (dp - jnp.sum(p * dp, axis=-1, keepdims=True)) * scale).astype(BF16)
            dq_ref[:, lo:hi] = jnp.dot(ds, k, preferred_element_type=F32).astype(BF16)
            dkv_ref[:, lo:hi] += lax.dot_general(ds, q, (TN, ((), ())), preferred_element_type=F32)

    return pl.pallas_call(
        body, name=name, grid=(S // ts,),
        in_specs=[pl.BlockSpec((ts, ECA), lambda i: (i, EMIX // ECA)), pl.BlockSpec((ts, ECA), lambda i: (i, qblk)),
                  pl.BlockSpec((M, 2 * ECA), lambda i: (0, 0)), ANY],
        out_specs=[pl.BlockSpec((ts, ECA), lambda i: (i, dblk)), pl.BlockSpec((M, 2 * ECA), lambda i: (0, 0))],
        out_shape=[jax.ShapeDtypeStruct(dbuf.shape, dbuf.dtype), jax.ShapeDtypeStruct((M, 2 * ECA), F32)],
        input_output_aliases={3: 0}, compiler_params=_params("arbitrary"),
    )(dpremix, proj, kv, dbuf)


def _gate_fwd(name, premix, proj, gblk, colscale, S, EB, ECA):
    ts = _tile(S, 512)

    def body(p_ref, g_ref, c_ref, o_ref):
        g = g_ref[...].astype(F32)
        o_ref[...] = (p_ref[...].astype(F32) * c_ref[...] * (g * _sigmoid(g))).astype(BF16)

    return pl.pallas_call(
        body, name=name, grid=(S // ts, EB // ECA),
        in_specs=[pl.BlockSpec((ts, ECA), lambda i, j: (i, j)), pl.BlockSpec((ts, ECA), lambda i, j: (i, gblk + j)),
                  pl.BlockSpec((1, ECA), lambda i, j: (0, j))],
        out_specs=pl.BlockSpec((ts, ECA), lambda i, j: (i, j)),
        out_shape=jax.ShapeDtypeStruct((S, EB), BF16), compiler_params=_params("parallel", "parallel"),
    )(premix, proj, colscale)


def _gate_bwd(name, dbranch, premix, proj, gblk, colscale, dshape, dblk, S, EB, ECA):
    ts = _tile(S, 512)

    def body(db_ref, p_ref, g_ref, c_ref, dp_ref, dg_ref, dc_ref):
        g = g_ref[...].astype(F32)
        sg = _sigmoid(g)
        si = g * sg
        c = c_ref[...]
        db = db_ref[...].astype(F32)
        t = db * p_ref[...].astype(F32)
        dp_ref[...] = (db * si * c).astype(BF16)
        dg_ref[...] = (t * c * (sg * (1.0 + g * (1.0 - sg)))).astype(BF16)

        @pl.when(pl.program_id(1) == 0)
        def _():
            dc_ref[...] = jnp.zeros_like(dc_ref)

        dc_ref[...] += jnp.sum(t * si, axis=0, keepdims=True)

    blk = pl.BlockSpec((ts, ECA), lambda j, i: (i, j))
    vec = pl.BlockSpec((1, ECA), lambda j, i: (0, j))
    return pl.pallas_call(
        body, name=name, grid=(EB // ECA, S // ts),
        in_specs=[blk, blk, pl.BlockSpec((ts, ECA), lambda j, i: (i, gblk + j)), vec],
        out_specs=[blk, pl.BlockSpec((ts, ECA), lambda j, i: (i, dblk + j)), vec],
        out_shape=[jax.ShapeDtypeStruct((S, EB), BF16), jax.ShapeDtypeStruct(dshape, BF16),
                   jax.ShapeDtypeStruct((1, EB), F32)],
        compiler_params=_params("parallel", "arbitrary"),
    )(dbranch, premix, proj, colscale)


def _hgrn_common(qin, fin, lbh):
    C = HG_CHUNK
    row = lax.broadcasted_iota(jnp.int32, (C, C), 0)
    col = lax.broadcasted_iota(jnp.int32, (C, C), 1)
    causal = row >= col
    sg = _sigmoid(fin)
    f = lbh + (1.0 - lbh) * sg
    k = 1.0 - f
    g = jnp.log(f)
    b = jnp.dot(jnp.where(causal, 1.0, 0.0), g, preferred_element_type=F32, precision=lax.Precision.HIGHEST)
    b_last = jnp.sum(g, axis=0, keepdims=True)
    eb = jnp.exp(b)
    einv = jnp.exp(-b)
    eend = jnp.exp(b_last - b)
    sq = _sigmoid(qin)
    qt = qin * sq * (HG_HEAD_DIM ** -0.5)
    a = qt * eb
    bm = k * einv
    e = k * eend
    d = jnp.exp(b_last)
    p = lax.dot_general(a.astype(BF16), bm.astype(BF16), (NT, ((), ())), preferred_element_type=F32)
    p = jnp.where(causal, p, 0.0)
    return dict(causal=causal, sg=sg, f=f, k=k, eb=eb, einv=einv, eend=eend, sq=sq, a=a, bm=bm, e=e, d=d, p=p)


def _hgrn_lb(lb_ref):
    l0 = lb_ref[0:1, :]
    l1 = lb_ref[1:2, :]
    mx = jnp.maximum(l0, l1)
    e0 = jnp.exp(l0 - mx)
    e1 = jnp.exp(l1 - mx)
    return e1 / (e0 + e1)


def _hgrn_fwd(proj, fgate, hgrn_lb, S, EMIX, EB):
    HD, C = HG_HEAD_DIM, HG_CHUNK
    HH = EMIX // HD
    hb = 2 if HH % 2 == 0 else 1
    W = hb * HD
    tr = _tile(S, 512)
    nch = tr // C

    def body(q_ref, f_ref, i_ref, lb_ref, o_ref, rstd_ref, st_ref, state):
        @pl.when(pl.program_id(1) == 0)
        def _():
            state[...] = jnp.zeros_like(state)

        lb = _hgrn_lb(lb_ref)

        def chunk(ci, carry):
            r0 = pl.multiple_of(ci * C, C)
            for h in range(hb):
                cs = slice(h * HD, (h + 1) * HD)
                qin = q_ref[pl.ds(r0, C), cs].astype(F32)
                fin = f_ref[pl.ds(r0, C), cs]
                v = i_ref[pl.ds(r0, C), cs]
                t = _hgrn_common(qin, fin, lb[:, cs])
                st = state[h]
                st_ref[h, ci] = st
                o = jnp.dot(t["p"].astype(BF16), v, preferred_element_type=F32)
                o = o + lax.dot_general(t["a"].astype(BF16), st.astype(BF16), (NT, ((), ())),
                                        preferred_element_type=F32)
                state[h] = st * t["d"] + lax.dot_general(v, t["e"].astype(BF16), (TN, ((), ())),
                                                         preferred_element_type=F32)
                rstd = lax.rsqrt(jnp.mean(o * o, axis=-1, keepdims=True) + EPS)
                o_ref[pl.ds(r0, C), cs] = (o * rstd).astype(BF16)
                rstd_ref[pl.ds(r0, C), cs] = jnp.broadcast_to(rstd, (C, HD))
            return carry

        lax.fori_loop(0, nch, chunk, 0)

    blk = lambda off: pl.BlockSpec((tr, W), lambda g, i: (i, off + g))
    return pl.pallas_call(
        body, name="hgrn_fwd", grid=(HH // hb, S // tr),
        in_specs=[blk(0), blk(0), blk(2 * EMIX // W), pl.BlockSpec((2, W), lambda g, i: (0, g))],
        out_specs=[blk(0), blk(0), pl.BlockSpec((hb, nch, HD, HD), lambda g, i: (g, i, 0, 0))],
        out_shape=[jax.ShapeDtypeStruct((S, EB), BF16), jax.ShapeDtypeStruct((S, EMIX), F32),
                   jax.ShapeDtypeStruct((HH, S // C, HD, HD), F32)],
        scratch_shapes=[pltpu.VMEM((hb, HD, HD), F32)],
        compiler_params=_params("parallel", "arbitrary"),
    )(proj, fgate, proj, hgrn_lb)


def _hgrn_bwd(dpremix, premix, rstd, states, proj, fgate, hgrn_lb, S, EMIX):
    HD, C = HG_HEAD_DIM, HG_CHUNK
    HH = EMIX // HD
    hb = 2 if HH % 2 == 0 else 1
    W = hb * HD
    tr = _tile(S, 512)
    nch = tr // C
    nrt = S // tr

    def body(do_ref, on_ref, rstd_ref, st_ref, q_ref, f_ref, i_ref, lb_ref, d_ref, dlb_ref, dstate):
        @pl.when(pl.program_id(1) == 0)
        def _():
            dstate[...] = jnp.zeros_like(dstate)
            dlb_ref[...] = jnp.zeros_like(dlb_ref)

        lb = _hgrn_lb(lb_ref)

        def chunk(step, carry):
            ci = nch - 1 - step
            r0 = pl.multiple_of(ci * C, C)
            for h in range(hb):
                cs = slice(h * HD, (h + 1) * HD)
                qin = q_ref[pl.ds(r0, C), cs].astype(F32)
                fin = f_ref[pl.ds(r0, C), cs]
                v = i_ref[pl.ds(r0, C), cs]
                lbh = lb[:, cs]
                t = _hgrn_common(qin, fin, lbh)
                a, bm, e, d, p = t["a"], t["bm"], t["e"], t["d"], t["p"]
                ab, bmb, eb16 = a.astype(BF16), bm.astype(BF16), e.astype(BF16)
                st = st_ref[h, ci]
                dst = dstate[h]
                on = on_ref[pl.ds(r0, C), cs].astype(F32)
                dn = do_ref[pl.ds(r0, C), cs].astype(F32)
                do = rstd_ref[pl.ds(r0, C), cs] * (dn - on * jnp.mean(dn * on, axis=-1, keepdims=True))
                dob = do.astype(BF16)
                dstb = dst.astype(BF16)
                dp = lax.dot_general(dob, v, (NT, ((), ())), preferred_element_type=F32)
                dp = jnp.where(t["causal"], dp, 0.0).astype(BF16)
                dv = lax.dot_general(p.astype(BF16), dob, (TN, ((), ())), preferred_element_type=F32)
                dv = dv + lax.dot_general(eb16, dstb, (NT, ((), ())), preferred_element_type=F32)
                da = jnp.dot(dp, bmb, preferred_element_type=F32)
                da = da + jnp.dot(dob, st.astype(BF16), preferred_element_type=F32)
                dbm = lax.dot_general(dp, ab, (TN, ((), ())), preferred_element_type=F32)
                de = jnp.dot(v, dstb, preferred_element_type=F32)
                dd = jnp.sum(dst * st, axis=0, keepdims=True)
                dstate[h] = dst * d + lax.dot_general(dob, ab, (TN, ((), ())), preferred_element_type=F32)
                dqt = da * t["eb"]
                dk = dbm * t["einv"] + de * t["eend"]
                dee = de * e
                db = da * a - dbm * bm - dee
                extra = jnp.sum(dee, axis=0, keepdims=True) + dd * d
                row = lax.broadcasted_iota(jnp.int32, (C, C), 0)
                col = lax.broadcasted_iota(jnp.int32, (C, C), 1)
                dg = jnp.dot(jnp.where(col >= row, 1.0, 0.0), db, preferred_element_type=F32,
                             precision=lax.Precision.HIGHEST) + extra
                df = dg / t["f"] - dk
                sg, sq = t["sg"], t["sq"]
                d_ref[0, pl.ds(r0, C), cs] = (dqt * (HD ** -0.5) * (sq * (1.0 + qin * (1.0 - sq)))).astype(BF16)
                d_ref[1, pl.ds(r0, C), cs] = (df * (1.0 - lbh) * sg * (1.0 - sg)).astype(BF16)
                d_ref[2, pl.ds(r0, C), cs] = dv.astype(BF16)
                dlb_ref[:, cs] += jnp.sum(df * (1.0 - sg), axis=0, keepdims=True)
            return carry

        lax.fori_loop(0, nch, chunk, 0)

    rev = lambda off: pl.BlockSpec((tr, W), lambda g, s: (nrt - 1 - s, off + g))
    return pl.pallas_call(
        body, name="hgrn_bwd", grid=(HH // hb, nrt),
        in_specs=[rev(0), rev(0), rev(0), pl.BlockSpec((hb, nch, HD, HD), lambda g, s: (g, nrt - 1 - s, 0, 0)),
                  rev(0), rev(0), rev(2 * EMIX // W), pl.BlockSpec((2, W), lambda g, s: (0, g))],
        out_specs=[pl.BlockSpec((3, tr, W), lambda g, s: (0, nrt - 1 - s, g)), pl.BlockSpec((1, W), lambda g, s: (0, g))],
        out_shape=[jax.ShapeDtypeStruct((3, S, EMIX), BF16), jax.ShapeDtypeStruct((1, EMIX), F32)],
        scratch_shapes=[pltpu.VMEM((hb, HD, HD), F32)],
        compiler_params=_params("parallel", "arbitrary"),
    )(dpremix, premix, rstd, states, proj, fgate, proj, hgrn_lb)


def _ew_tiles(R, C):
    return _tile(R, 256), _tile(C, 1024)


def _add_halves(name, a, b):
    _, R, C = a.shape
    tr, tc = _ew_tiles(R, C)

    def body(a_ref, b_ref, o_ref):
        o_ref[...] = (a_ref[...].astype(F32) + b_ref[...].astype(F32)).astype(BF16)

    blk = pl.BlockSpec((None, tr, tc), lambda s, i, j: (s, i, j))
    return pl.pallas_call(
        body, name=name, grid=(N_CHIPS, R // tr, C // tc), in_specs=[blk, blk], out_specs=blk,
        out_shape=jax.ShapeDtypeStruct(a.shape, BF16), compiler_params=_params("parallel", "parallel", "parallel"),
    )(a, b)


def _sum_slots(name, parts):
    _, R, C = parts.shape
    tr, tc = _ew_tiles(R, C)

    def body(p_ref, o_ref):
        acc = p_ref[0].astype(F32)
        for s in range(1, N_CHIPS):
            acc = acc + p_ref[s].astype(F32)
        o_ref[...] = acc

    return pl.pallas_call(
        body, name=name, grid=(R // tr, C // tc),
        in_specs=[pl.BlockSpec((N_CHIPS, tr, tc), lambda i, j: (0, i, j))],
        out_specs=pl.BlockSpec((tr, tc), lambda i, j: (i, j)),
        out_shape=jax.ShapeDtypeStruct((R, C), F32), compiler_params=_params("parallel", "parallel"),
    )(parts)


def _adamw(name, w, g, m, v):
    R, C = w.shape
    tr, tc = _ew_tiles(R, C)

    def body(w_ref, g_ref, m_ref, v_ref, d_ref, mo_ref, vo_ref):
        g = g_ref[...]
        mn = ADAM_B1 * m_ref[...] + (1.0 - ADAM_B1) * g
        vn = ADAM_B2 * v_ref[...] + (1.0 - ADAM_B2) * (g * g)
        m_hat = mn / (1.0 - ADAM_B1 ** ADAM_STEP)
        v_hat = vn / (1.0 - ADAM_B2 ** ADAM_STEP)
        d_ref[...] = -ADAM_LR * (m_hat / (jnp.sqrt(v_hat) + ADAM_EPS) + ADAM_WD * w_ref[...])
        mo_ref[...] = mn
        vo_ref[...] = vn

    blk = pl.BlockSpec((tr, tc), lambda i, j: (i, j))
    sds = jax.ShapeDtypeStruct((R, C), F32)
    return pl.pallas_call(
        body, name=name, grid=(R // tr, C // tc), in_specs=[blk] * 4, out_specs=[blk] * 3, out_shape=[sds] * 3,
        compiler_params=_params("parallel", "parallel"),
    )(w, g, m, v)


def _small_sum(gathered, lb_rows):
    _, RR, W = gathered.shape
    T = SMALL_ROWS
    LB_TILE = 4

    def body(g_ref, lb_ref, o_ref):
        acc = g_ref[0]
        for dev in range(1, N_DEV):
            acc = acc + g_ref[dev]
        o_ref[0:RR, :] = acc
        l0 = lb_ref[0:1, :]
        l1 = lb_ref[1:2, :]
        mx = jnp.maximum(l0, l1)
        e0 = jnp.exp(l0 - mx)
        e1 = jnp.exp(l1 - mx)
        lb = e1 / (e0 + e1)
        d1 = acc[LB_TILE * T:(LB_TILE + 1) * T, :] * (lb * (1.0 - lb))
        o_ref[RR:RR + T, :] = -d1
        o_ref[RR + T:RR + 2 * T, :] = d1

    vm = pl.BlockSpec(memory_space=pltpu.VMEM)
    return pl.pallas_call(
        body, name="small_sum", in_specs=[vm, vm], out_specs=vm,
        out_shape=jax.ShapeDtypeStruct((RR + 2 * T, W), F32),
    )(gathered, lb_rows)


def _place():
    return lax.axis_index("x"), lax.axis_index("y"), lax.axis_index("c")


def _other_chips(x, y):
    return [(1 - x, y), (x, 1 - y), (1 - x, 1 - y)]


def _allgather_weights(shards):
    nt = len(shards)

    def body(*refs):
        ins, outs = refs[:nt], refs[nt:2 * nt]
        send, recv, fsend, frecv, lsem = refs[2 * nt:]
        x, y, c = _place()
        me = 2 * x + y
        sib = (x, y, 1 - c)
        chips = _other_chips(x, y)

        def half(t, slot, hc):
            h = shards[t].shape[0] // 2
            return outs[t].at[slot, pl.ds(hc * h, h)]

        def copy(t, slot, hc, sems, j, to, src=None):
            return pltpu.make_async_remote_copy(
                src_ref=half(t, slot, hc) if src is None else src, dst_ref=half(t, slot, hc),
                send_sem=sems[0].at[t, j], recv_sem=sems[1].at[t, j], device_id=to, device_id_type=MESH)

        local, first, passed = [], [], []
        for t in range(nt):
            h = shards[t].shape[0] // 2
            cp = pltpu.make_async_copy(ins[t], outs[t].at[me], lsem.at[t])
            cp.start()
            local.append(cp)
            for j, (px, py) in enumerate(chips):
                cp = copy(t, me, c, (send, recv), j, (px, py, c), src=ins[t].at[pl.ds(c * h, h)])
                cp.start()
                first.append(cp)
        for t in range(nt):
            for j, (px, py) in enumerate(chips):
                slot = 2 * px + py
                copy(t, slot, c, (send, recv), j, (px, py, c)).wait_recv()
                cp = copy(t, slot, c, (fsend, frecv), j, sib)
                cp.start()
                passed.append(cp)
        for t in range(nt):
            for j, (px, py) in enumerate(chips):
                copy(t, 2 * px + py, 1 - c, (fsend, frecv), j, sib).wait_recv()
        for cp in first + passed:
            cp.wait_send()
        for cp in local:
            cp.wait()

    return pl.pallas_call(
        body, name="allgather_weights", in_specs=[ANY] * nt, out_specs=[ANY] * nt,
        out_shape=[jax.ShapeDtypeStruct((N_CHIPS,) + s.shape, s.dtype) for s in shards],
        scratch_shapes=[pltpu.SemaphoreType.DMA((nt, 3))] * 4 + [pltpu.SemaphoreType.DMA((nt,))],
    )(*shards)


def _exchange_halves(grads):
    nt = len(grads)

    def body(*refs):
        ins, owns, gots = refs[:nt], refs[nt:2 * nt], refs[2 * nt:3 * nt]
        send, recv, lsem = refs[3 * nt:]
        x, y, c = _place()
        sib = (x, y, 1 - c)
        copies = []
        for t in range(nt):
            h = grads[t].shape[1] // 2
            cp = pltpu.make_async_remote_copy(
                src_ref=ins[t].at[:, pl.ds((1 - c) * h, h)], dst_ref=gots[t], send_sem=send.at[t], recv_sem=recv.at[t],
                device_id=sib, device_id_type=MESH)
            cp.start()
            copies.append(cp)
            cp = pltpu.make_async_copy(ins[t].at[:, pl.ds(c * h, h)], owns[t], lsem.at[t])
            cp.start()
            copies.append(cp)
        for cp in copies:
            cp.wait()

    halves = [jax.ShapeDtypeStruct((N_CHIPS, g.shape[1] // 2, g.shape[2]), g.dtype) for g in grads]
    outs = pl.pallas_call(
        body, name="exchange_halves", in_specs=[ANY] * nt, out_specs=[ANY] * (2 * nt), out_shape=halves + halves,
        scratch_shapes=[pltpu.SemaphoreType.DMA((nt,))] * 3,
    )(*grads)
    return outs[:nt], outs[nt:]


def _scatter_partials(parts):
    nt = len(parts)

    def body(*refs):
        ins, outs = refs[:nt], refs[nt:2 * nt]
        send, recv, lsem = refs[2 * nt:]
        x, y, c = _place()
        me = 2 * x + y
        copies = []
        for t in range(nt):
            cp = pltpu.make_async_copy(ins[t].at[me], outs[t].at[me], lsem.at[t])
            cp.start()
            copies.append(cp)
            for j, (px, py) in enumerate(_other_chips(x, y)):
                cp = pltpu.make_async_remote_copy(
                    src_ref=ins[t].at[2 * px + py], dst_ref=outs[t].at[me], send_sem=send.at[t, j],
                    recv_sem=recv.at[t, j], device_id=(px, py, c), device_id_type=MESH)
                cp.start()
                copies.append(cp)
        for cp in copies:
            cp.wait()

    return pl.pallas_call(
        body, name="scatter_partials", in_specs=[ANY] * nt, out_specs=[ANY] * nt,
        out_shape=[jax.ShapeDtypeStruct(p.shape, p.dtype) for p in parts],
        scratch_shapes=[pltpu.SemaphoreType.DMA((nt, 3))] * 2 + [pltpu.SemaphoreType.DMA((nt,))],
    )(*parts)


def _share_halves(totals):
    nt = len(totals)

    def body(*refs):
        ins, outs = refs[:nt], refs[nt:2 * nt]
        send, recv, lsem = refs[2 * nt:]
        x, y, c = _place()
        copies = []
        for t in range(nt):
            cp = pltpu.make_async_copy(ins[t], outs[t].at[c], lsem.at[t])
            cp.start()
            copies.append(cp)
            cp = pltpu.make_async_remote_copy(
                src_ref=ins[t], dst_ref=outs[t].at[c], send_sem=send.at[t], recv_sem=recv.at[t],
                device_id=(x, y, 1 - c), device_id_type=MESH)
            cp.start()
            copies.append(cp)
        for cp in copies:
            cp.wait()

    return pl.pallas_call(
        body, name="share_halves", in_specs=[ANY] * nt, out_specs=[ANY] * nt,
        out_shape=[jax.ShapeDtypeStruct((2,) + t.shape, t.dtype) for t in totals],
        scratch_shapes=[pltpu.SemaphoreType.DMA((nt,))] * 3,
    )(*totals)


def _allgather_small(name, v):
    def body(v_ref, o_ref, send, recv, lsem):
        x, y, c = _place()
        me = 4 * x + 2 * y + c
        loc = pltpu.make_async_copy(v_ref, o_ref.at[me], lsem)
        loc.start()
        copies = []
        for k in range(1, N_DEV):
            px = 1 - x if k & 4 else x
            py = 1 - y if k & 2 else y
            pc = 1 - c if k & 1 else c
            cp = pltpu.make_async_remote_copy(
                src_ref=v_ref, dst_ref=o_ref.at[me], send_sem=send.at[k - 1], recv_sem=recv.at[k - 1],
                device_id=(px, py, pc), device_id_type=MESH)
            cp.start()
            copies.append(cp)
        for cp in copies:
            cp.wait()
        loc.wait()

    vm = pl.BlockSpec(memory_space=pltpu.VMEM)
    return pl.pallas_call(
        body, name=name, in_specs=[vm], out_specs=vm,
        out_shape=jax.ShapeDtypeStruct((N_DEV,) + v.shape, v.dtype),
        scratch_shapes=[pltpu.SemaphoreType.DMA((N_DEV - 1,))] * 2 + [pltpu.SemaphoreType.DMA],
    )(v)


def kernel(x, mem, norm_g, mem_norm_g, w_kv, w_out, pool_w_in, pool_w_grp, pool_scale, hgrn_w_in, hgrn_lb, hgrn_norm_g, final_g, loss_target, m_norm_g, m_mem_norm_g, m_w_kv, m_w_out, m_pool_w_in, m_pool_w_grp, m_pool_scale, m_hgrn_w_in, m_hgrn_lb, m_hgrn_norm_g, m_final_g, v_norm_g, v_mem_norm_g, v_w_kv, v_w_out, v_pool_w_in, v_pool_w_grp, v_pool_scale, v_hgrn_w_in, v_hgrn_lb, v_hgrn_norm_g, v_final_g):
    _, S, D = x.shape
    M = mem.shape[1]
    EB = 2 * D
    ECA = EB // 4
    EMIX = EB - ECA
    PG = EMIX // N_POOL_GROUPS
    NP0 = EMIX + ECA + EB
    NP1 = 3 * EMIX + ECA + EB
    SH0, SH1 = NP0 // N_CHIPS, NP1 // N_CHIPS
    DK, EK = D // N_CHIPS, EB // N_CHIPS
    TNP = 512 if all(v % 512 == 0 for v in (SH0, SH1, ECA, EMIX)) else 256
    TM = _tile(S, 1024)
    TK = _tile(S, 512)
    TD = _tile(D, 512)
    c0, c1 = SH0 // TNP, SH1 // TNP
    qt, et = EMIX // TNP, ECA // TNP
    chip = 2 * lax.axis_index("x") + lax.axis_index("y")

    xs, ms, tgt = x[0], mem[0], loss_target[0]

    flat = lambda w: w.reshape(-1, w.shape[-1])
    shard2d = [flat(w_kv), flat(w_out), flat(pool_w_in), flat(pool_w_grp), flat(hgrn_w_in)]
    g_kv, g_out, g_pin, g_grp, g_hin = _allgather_weights([s.astype(BF16) for s in shard2d])
    wkv = g_kv.reshape(N_CHIPS, 2, DK, 2 * ECA)
    wout = g_out.reshape(N_CHIPS, 2, EK, D)
    wpin = g_pin
    whin = g_hin
    wgrp = g_grp.reshape(N_CHIPS, N_POOL_GROUPS, PG // N_CHIPS, PG).transpose(1, 0, 2, 3).reshape(N_POOL_GROUPS, PG, PG)

    sds = jax.ShapeDtypeStruct
    tdk, tek = _tile(DK, 512), _tile(EK, 512)

    mem_n = _rms_fwd("rms_mem", ms, mem_norm_g.reshape(1, D))
    tkv = _tile(2 * ECA, 512)

    def kv_of(layer):
        return _matmul(
            f"kv{layer}", mem_n, wkv, grid=(1, 2 * ECA // tkv, D // tdk),
            a_spec=pl.BlockSpec((M, tdk), lambda i, j, k: (0, k)),
            b_spec=pl.BlockSpec((None, None, tdk, tkv), lambda i, j, k: (k // (DK // tdk), layer, k % (DK // tdk), j)),
            out_shape=sds((M, 2 * ECA), BF16), out_spec=pl.BlockSpec((M, tkv), lambda i, j, k: (0, j)),
            acc_shape=(M, tkv), dims=NN)

    kv = [kv_of(0), kv_of(1)]

    def out_proj(layer, branch, resid):
        return _matmul(
            f"out_proj{layer}", branch, wout, grid=(S // TM, D // TD, EB // tek),
            a_spec=pl.BlockSpec((TM, tek), lambda i, j, k: (i, k)),
            b_spec=pl.BlockSpec((None, None, tek, TD), lambda i, j, k: (k // (EK // tek), layer, k % (EK // tek), j)),
            out_shape=sds((S, D), F32), out_spec=pl.BlockSpec((TM, TD), lambda i, j, k: (i, j)),
            acc_shape=(TM, TD), dims=NN, add=resid, add_spec=pl.BlockSpec((TM, TD), lambda i, j, k: (i, j)))

    ones_ca = jnp.ones((1, ECA), F32)

    h0 = _rms_fwd("rms0", xs, norm_g[0:1])
    proj0 = _matmul(
        "proj0", h0, wpin, grid=(S // TM, NP0 // TNP, D // TD),
        a_spec=pl.BlockSpec((TM, TD), lambda i, j, k: (i, k)),
        b_spec=pl.BlockSpec((None, TD, TNP), lambda i, j, k: (j // c0, k, j % c0)),
        out_shape=sds((S, NP0), BF16), out_spec=pl.BlockSpec((TM, TNP), lambda i, j, k: (i, j)),
        acc_shape=(TM, TNP), dims=NN)
    pooled = _pool_fwd(proj0, S, EMIX)
    premix0 = _matmul(
        "pool_grp", pooled, wgrp, grid=(S // TM, N_POOL_GROUPS, 1),
        a_spec=pl.BlockSpec((TM, PG), lambda i, j, k: (i, j)),
        b_spec=pl.BlockSpec((None, PG, PG), lambda i, j, k: (j, 0, 0)),
        out_shape=sds((S, EB), BF16), out_spec=pl.BlockSpec((TM, PG), lambda i, j, k: (i, j)),
        acc_shape=(TM, PG), dims=NN)
    premix0 = _ca_fwd("ca_fwd0", proj0, EMIX // ECA, kv[0], premix0, S, ECA, EMIX)
    colscale0 = jnp.concatenate([pool_scale.reshape(1, EMIX), ones_ca], axis=1)
    gblk0 = (EMIX + ECA) // ECA
    branch0 = _gate_fwd("gate_fwd0", premix0, proj0, gblk0, colscale0, S, EB, ECA)
    x1 = out_proj(0, branch0, xs)

    h1 = _rms_fwd("rms1", x1, norm_g[1:2])

    def proj1_cols(name, ncols, col_of, out_cols, out_dtype, out_col_of):
        return _matmul(
            name, h1, whin, grid=(S // TM, ncols, D // TD),
            a_spec=pl.BlockSpec((TM, TD), lambda i, j, k: (i, k)),
            b_spec=pl.BlockSpec((None, TD, TNP), lambda i, j, k: (col_of(j) // c1, k, col_of(j) % c1)),
            out_shape=sds((S, out_cols), out_dtype), out_spec=pl.BlockSpec((TM, TNP), lambda i, j, k: (i, out_col_of(j))),
            acc_shape=(TM, TNP), dims=NN)

    skip_f = lambda j: jnp.where(j < qt, j, j + qt)
    proj1 = proj1_cols("proj1", NP1 // TNP - qt, skip_f, NP1, BF16, skip_f)
    fgate = proj1_cols("proj1_f", qt, lambda j: j + qt, EMIX, F32, lambda j: j)
    premix1, rstd1, states = _hgrn_fwd(proj1, fgate, hgrn_lb, S, EMIX, EB)
    premix1 = _ca_fwd("ca_fwd1", proj1, 3 * EMIX // ECA, kv[1], premix1, S, ECA, EMIX)
    norm_tiles = _allgather_small("allgather_norm_g", jnp.pad(hgrn_norm_g, ((0, SMALL_ROWS - 1), (0, 0))))
    hg_norm = norm_tiles[0::2, 0, :].reshape(1, EMIX)
    colscale1 = jnp.concatenate([hg_norm, ones_ca], axis=1)
    gblk1 = (3 * EMIX + ECA) // ECA
    branch1 = _gate_fwd("gate_fwd1", premix1, proj1, gblk1, colscale1, S, EB, ECA)
    x2 = out_proj(1, branch1, x1)

    dx2, dx2b, d_final_g, loss_part = _loss_head(x2, final_g.reshape(1, D), tgt)

    def out_proj_bwd(layer, dxb, branch, buf):
        dbranch = _matmul(
            f"dbranch{layer}", dxb, wout, grid=(S // TM, EB // tek, D // TD), a_spec=pl.BlockSpec((TM, TD), IK),
            b_spec=pl.BlockSpec((None, None, tek, TD), lambda i, j, k: (j // (EK // tek), layer, j % (EK // tek), k)),
            out_shape=sds((S, EB), BF16), out_spec=pl.BlockSpec((TM, tek), IJ), acc_shape=(TM, tek), dims=NT)
        dw = _matmul(
            f"dwout{layer}", branch, dxb, grid=(EB // tek, D // TD, S // TK), a_spec=pl.BlockSpec((TK, tek), KI),
            b_spec=pl.BlockSpec((TK, TD), KJ), out_shape=sds((N_CHIPS, 2, EK, D), BF16),
            out_spec=pl.BlockSpec((None, None, tek, TD), lambda i, j, k: (i // (EK // tek), layer, i % (EK // tek), j)),
            acc_shape=(tek, TD), dims=TN, alias=buf)
        return dbranch, dw

    def kv_bwd(layer, dkv, buf, dmem_add):
        dkvb = dkv.astype(BF16)
        dmem = _matmul(
            f"dmem{layer}", dkvb, wkv, grid=(1, D // tdk, 2 * ECA // tkv),
            a_spec=pl.BlockSpec((M, tkv), lambda i, j, k: (0, k)),
            b_spec=pl.BlockSpec((None, None, tdk, tkv), lambda i, j, k: (j // (DK // tdk), layer, j % (DK // tdk), k)),
            out_shape=sds((M, D), F32), out_spec=pl.BlockSpec((M, tdk), lambda i, j, k: (0, j)), acc_shape=(M, tdk),
            dims=NT, add=dmem_add, add_spec=pl.BlockSpec((M, tdk), lambda i, j, k: (0, j)))
        dw = _matmul(
            f"dwkv{layer}", mem_n, dkvb, grid=(D // tdk, 2 * ECA // tkv, 1),
            a_spec=pl.BlockSpec((M, tdk), lambda i, j, k: (0, i)), b_spec=pl.BlockSpec((M, tkv), lambda i, j, k: (0, j)),
            out_shape=sds((N_CHIPS, 2, DK, 2 * ECA), BF16),
            out_spec=pl.BlockSpec((None, None, tdk, tkv), lambda i, j, k: (i // (DK // tdk), layer, i % (DK // tdk), j)),
            acc_shape=(tdk, tkv), dims=TN, alias=buf)
        return dmem, dw

    dbranch1, gw_out = out_proj_bwd(1, dx2b, branch1, None)
    dpremix1, drest1, dcol1 = _gate_bwd("gate_bwd1", dbranch1, premix1, proj1, gblk1, colscale1, (S, ECA + EB), 1,
                                        S, EB, ECA)
    drest1, dkv1 = _ca_bwd("ca_bwd1", dpremix1, proj1, 3 * EMIX // ECA, kv[1], drest1, 0, S, ECA, EMIX)
    dqfi, dlb = _hgrn_bwd(dpremix1, premix1, rstd1, states, proj1, fgate, hgrn_lb, S, EMIX)
    nq, nr = 3 * qt, (ECA + EB) // TNP
    dh1 = _matmul(
        "dh1_qfi", dqfi, whin, grid=(S // TM, D // TD, nq),
        a_spec=pl.BlockSpec((None, TM, TNP), lambda i, j, k: (k // qt, i, k % qt)),
        b_spec=pl.BlockSpec((None, TD, TNP), lambda i, j, k: (k // c1, j, k % c1)),
        out_shape=sds((S, D), F32), out_spec=pl.BlockSpec((TM, TD), IJ), acc_shape=(TM, TD), dims=NT)
    dh1 = _matmul(
        "dh1_rest", drest1, whin, grid=(S // TM, D // TD, nr), a_spec=pl.BlockSpec((TM, TNP), IK),
        b_spec=pl.BlockSpec((None, TD, TNP), lambda i, j, k: ((k + nq) // c1, j, (k + nq) % c1)),
        out_shape=sds((S, D), F32), out_spec=pl.BlockSpec((TM, TD), IJ), acc_shape=(TM, TD), dims=NT,
        add=dh1, add_spec=pl.BlockSpec((TM, TD), IJ))
    gw_hin = _matmul(
        "dwhin_qfi", h1, dqfi, grid=(D // TD, nq, S // TK), a_spec=pl.BlockSpec((TK, TD), KI),
        b_spec=pl.BlockSpec((None, TK, TNP), lambda i, j, k: (j // qt, k, j % qt)),
        out_shape=sds((N_CHIPS, D, SH1), BF16), out_spec=pl.BlockSpec((None, TD, TNP), lambda i, j, k: (j // c1, i, j % c1)),
        acc_shape=(TD, TNP), dims=TN)
    gw_hin = _matmul(
        "dwhin_rest", h1, drest1, grid=(D // TD, nr, S // TK), a_spec=pl.BlockSpec((TK, TD), KI),
        b_spec=pl.BlockSpec((TK, TNP), KJ), out_shape=sds((N_CHIPS, D, SH1), BF16),
        out_spec=pl.BlockSpec((None, TD, TNP), lambda i, j, k: ((j + nq) // c1, i, (j + nq) % c1)),
        acc_shape=(TD, TNP), dims=TN, alias=gw_hin)
    dx1, dx1b, d_ng1 = _rms_bwd("rms_bwd1", dh1, x1, norm_g[1:2], dx2)
    dmem, gw_kv = kv_bwd(1, dkv1, None, None)

    dbranch0, gw_out = out_proj_bwd(0, dx1b, branch0, gw_out)
    dpremix0, dproj0, dcol0 = _gate_bwd("gate_bwd0", dbranch0, premix0, proj0, gblk0, colscale0, (S, NP0), gblk0,
                                        S, EB, ECA)
    dproj0, dkv0 = _ca_bwd("ca_bwd0", dpremix0, proj0, EMIX // ECA, kv[0], dproj0, EMIX // ECA, S, ECA, EMIX)
    dpooled = _matmul(
        "dpooled", dpremix0, wgrp, grid=(S // TM, N_POOL_GROUPS, 1), a_spec=pl.BlockSpec((TM, PG), IJ),
        b_spec=pl.BlockSpec((None, PG, PG), lambda i, j, k: (j, 0, 0)),
        out_shape=sds((S, EMIX), F32), out_spec=pl.BlockSpec((TM, PG), IJ), acc_shape=(TM, PG), dims=NT)
    dwgrp = _matmul(
        "dwgrp", pooled, dpremix0, grid=(N_POOL_GROUPS, 1, S // TK), a_spec=pl.BlockSpec((TK, PG), KI),
        b_spec=pl.BlockSpec((TK, PG), KI), out_shape=sds((N_POOL_GROUPS, PG, PG), F32),
        out_spec=pl.BlockSpec((None, PG, PG), lambda i, j, k: (i, 0, 0)), acc_shape=(PG, PG), dims=TN)
    dproj0 = _pool_bwd(dpooled, dproj0, S, EMIX)
    dh0 = _matmul(
        "dh0", dproj0, wpin, grid=(S // TM, D // TD, NP0 // TNP), a_spec=pl.BlockSpec((TM, TNP), IK),
        b_spec=pl.BlockSpec((None, TD, TNP), lambda i, j, k: (k // c0, j, k % c0)),
        out_shape=sds((S, D), F32), out_spec=pl.BlockSpec((TM, TD), IJ), acc_shape=(TM, TD), dims=NT)
    gw_pin = _matmul(
        "dwpin", h0, dproj0, grid=(D // TD, NP0 // TNP, S // TK), a_spec=pl.BlockSpec((TK, TD), KI),
        b_spec=pl.BlockSpec((TK, TNP), KJ), out_shape=sds((N_CHIPS, D, SH0), BF16),
        out_spec=pl.BlockSpec((None, TD, TNP), lambda i, j, k: (j // c0, i, j % c0)), acc_shape=(TD, TNP), dims=TN)
    grad_x, _, d_ng0 = _rms_bwd("rms_bwd0", dh0, xs, norm_g[0:1], dx1)
    dmem, gw_kv = kv_bwd(0, dkv0, gw_kv, dmem)
    _, _, d_mng = _rms_bwd("rms_bwd_mem", dmem, ms, mem_norm_g.reshape(1, D), jnp.zeros_like(ms))

    gw_grp = dwgrp.reshape(N_POOL_GROUPS, N_CHIPS, PG // N_CHIPS, PG).transpose(1, 0, 2, 3).reshape(N_CHIPS, PG, PG)
    stacks = [gw_kv.reshape(N_CHIPS, 2 * DK, 2 * ECA), gw_out.reshape(N_CHIPS, 2 * EK, D), gw_pin,
              gw_grp.astype(BF16), gw_hin]
    own, got = _exchange_halves(stacks)
    parts = [_add_halves(f"add_halves{t}", a, b) for t, (a, b) in enumerate(zip(own, got))]
    landed = _scatter_partials(parts)
    totals = [_sum_slots(f"sum_slots{t}", p) for t, p in enumerate(landed)]
    big_g = [f.reshape(-1, f.shape[-1]) for f in _share_halves(totals)]
    big_names = ["w_kv", "w_out", "pool_w_in", "pool_w_grp", "hgrn_w_in"]
    big_m = [flat(a) for a in (m_w_kv, m_w_out, m_pool_w_in, m_pool_w_grp, m_hgrn_w_in)]
    big_v = [flat(a) for a in (v_w_kv, v_w_out, v_pool_w_in, v_pool_w_grp, v_hgrn_w_in)]
    big_shapes = [w_kv.shape, w_out.shape, pool_w_in.shape, pool_w_grp.shape, hgrn_w_in.shape]
    grads, deltas, new_m, new_v = {}, {}, {}, {}
    for t, n in enumerate(big_names):
        d, mn, vn = _adamw(f"adamw_{n}", shard2d[t], big_g[t], big_m[t], big_v[t])
        grads[n], deltas[n] = big_g[t].reshape(big_shapes[t]), d.reshape(big_shapes[t])
        new_m[n], new_v[n] = mn.reshape(big_shapes[t]), vn.reshape(big_shapes[t])

    Wd = EMIX
    tile_of = lambda v: jnp.pad(v, ((0, SMALL_ROWS - v.shape[0]), (0, Wd - v.shape[1])))
    partial = jnp.concatenate([tile_of(v) for v in (d_ng0, d_ng1, d_mng, dcol0[:, :EMIX], dlb, dcol1[:, :EMIX],
                                                    d_final_g, loss_part)], axis=0)
    summed = _small_sum(_allgather_small("allgather_grads", partial), tile_of(hgrn_lb))
    T = SMALL_ROWS
    tl = lambda i: summed[i * T:(i + 1) * T]
    nshard = EMIX // N_CHIPS
    g_hg_norm = lax.dynamic_slice_in_dim(summed[5 * T:5 * T + 1], chip * nshard, nshard, axis=1)
    small_names = ["norm_g0", "norm_g1", "mem_norm_g", "pool_scale", "hgrn_lb0", "hgrn_lb1", "hgrn_norm_g", "final_g"]
    small_w = [norm_g[0:1], norm_g[1:2], mem_norm_g.reshape(1, D), pool_scale, hgrn_lb[0:1], hgrn_lb[1:2], hgrn_norm_g,
               final_g.reshape(1, D)]
    small_m = [m_norm_g[0:1], m_norm_g[1:2], m_mem_norm_g.reshape(1, D), m_pool_scale, m_hgrn_lb[0:1], m_hgrn_lb[1:2],
               m_hgrn_norm_g, m_final_g.reshape(1, D)]
    small_v = [v_norm_g[0:1], v_norm_g[1:2], v_mem_norm_g.reshape(1, D), v_pool_scale, v_hgrn_lb[0:1], v_hgrn_lb[1:2],
               v_hgrn_norm_g, v_final_g.reshape(1, D)]
    pack = lambda vs: jnp.concatenate([tile_of(v) for v in vs], axis=0)
    g_pack = jnp.concatenate([tl(0), tl(1), tl(2), tl(3), tl(8), tl(9), tile_of(g_hg_norm), tl(6)], axis=0)
    d_pack, m_pack, v_pack = _adamw("adamw_small", pack(small_w), g_pack, pack(small_m), pack(small_v))
    widths = [v.shape[1] for v in small_w]
    rows = lambda p: {n: p[i * T, :widths[i]] for i, n in enumerate(small_names)}

    def assemble(r, out):
        out["norm_g"] = jnp.stack([r["norm_g0"], r["norm_g1"]])
        out["mem_norm_g"] = r["mem_norm_g"]
        out["pool_scale"] = r["pool_scale"].reshape(1, EMIX)
        out["hgrn_lb"] = jnp.stack([r["hgrn_lb0"], r["hgrn_lb1"]])
        out["hgrn_norm_g"] = r["hgrn_norm_g"].reshape(1, nshard)
        out["final_g"] = r["final_g"]

    assemble(rows(g_pack), grads)
    assemble(rows(d_pack), deltas)
    assemble(rows(m_pack), new_m)
    assemble(rows(v_pack), new_v)
    loss = summed[7 * T, 0]

    order = ["norm_g", "mem_norm_g", "w_kv", "w_out", "pool_w_in", "pool_w_grp", "pool_scale", "hgrn_w_in", "hgrn_lb",
             "hgrn_norm_g", "final_g"]
    return (loss, grad_x.reshape(1, S, D), *[grads[n] for n in order], *[deltas[n] for n in order],
            *[new_m[n] for n in order], *[new_v[n] for n in order])
```

```python
import functools

import jax
import jax.numpy as jnp
from jax import lax
from jax.experimental import pallas as pl
from jax.experimental.pallas import tpu as pltpu

F32 = jnp.float32
BF16 = jnp.bfloat16
MESH = pl.DeviceIdType.MESH
ANY = pl.BlockSpec(memory_space=pl.ANY)

EPS = 1e-6
HG_HEAD_DIM = 128
HG_CHUNK = 64
CA_HEADS = 4
N_POOL_GROUPS = 4
POOL_HALO = 128
ADAM_LR = 0.001
ADAM_B1 = 0.9
ADAM_B2 = 0.999
ADAM_EPS = 1e-08
ADAM_WD = 0.01
ADAM_STEP = 10
N_CHIPS = 4
N_DEV = 8
VMEM_LIMIT_BYTES = 56 * 1024 * 1024
SMALL_ROWS = 8
STREAM_CHUNK_BYTES = 2 * 1024 * 1024
STREAM_SLOTS = 2


def _params(*sem):
    return pltpu.CompilerParams(dimension_semantics=sem, vmem_limit_bytes=VMEM_LIMIT_BYTES)


def _tile(n, pref):
    t = pref
    while n % t:
        t //= 2
    return t


def _sigmoid(x):
    return 1.0 / (1.0 + jnp.exp(-x))


def _matmul(name, a, b, *, grid, a_spec, b_spec, out_shape, out_spec, acc_shape, dims,
            add=None, add_spec=None, alias=None):
    nk = grid[2]
    has_add = add is not None
    has_alias = alias is not None

    def body(*refs):
        a_ref, b_ref = refs[0], refs[1]
        pos = 2
        add_ref = None
        if has_add:
            add_ref = refs[pos]
            pos += 1
        if has_alias:
            pos += 1
        o_ref = refs[pos]
        prod = lax.dot_general(a_ref[...], b_ref[...], (dims, ((), ())), preferred_element_type=F32)

        def finish(r):
            if has_add:
                r = r + add_ref[...].astype(F32)
            o_ref[...] = r.astype(o_ref.dtype)

        if nk == 1:
            finish(prod)
            return
        acc_ref = refs[pos + 1]
        k = pl.program_id(2)

        @pl.when(k == 0)
        def _():
            acc_ref[...] = prod

        @pl.when(k > 0)
        def _():
            acc_ref[...] += prod

        @pl.when(k == nk - 1)
        def _():
            finish(acc_ref[...])

    operands = [a, b]
    in_specs = [a_spec, b_spec]
    if has_add:
        operands.append(add)
        in_specs.append(add_spec)
    aliases = {}
    if has_alias:
        aliases = {len(operands): 0}
        operands.append(alias)
        in_specs.append(ANY)
    return pl.pallas_call(
        body, name=name, grid=grid, in_specs=in_specs, out_specs=out_spec, out_shape=out_shape,
        scratch_shapes=[pltpu.VMEM(acc_shape, F32)] if nk > 1 else [], input_output_aliases=aliases,
        compiler_params=_params("parallel", "parallel", "arbitrary"),
    )(*operands)


IJ = lambda i, j, k: (i, j)
IK = lambda i, j, k: (i, k)
KJ = lambda i, j, k: (k, j)
KI = lambda i, j, k: (k, i)
NN = ((1,), (0,))
NT = ((1,), (1,))
TN = ((0,), (0,))


def _rms_fwd(name, x, g):
    R, D = x.shape
    tr = _tile(R, 256)

    def body(x_ref, g_ref, o_ref):
        xf = x_ref[...]
        r = lax.rsqrt(jnp.mean(xf * xf, axis=-1, keepdims=True) + EPS)
        o_ref[...] = (xf * r * g_ref[...]).astype(o_ref.dtype)

    return pl.pallas_call(
        body, name=name, grid=(R // tr,),
        in_specs=[pl.BlockSpec((tr, D), lambda i: (i, 0)), pl.BlockSpec((1, D), lambda i: (0, 0))],
        out_specs=pl.BlockSpec((tr, D), lambda i: (i, 0)),
        out_shape=jax.ShapeDtypeStruct((R, D), BF16), compiler_params=_params("parallel"),
    )(x, g)


def _rms_bwd(name, dh, x, g, dres):
    R, D = x.shape
    tr = _tile(R, 256)

    def body(dh_ref, x_ref, g_ref, dres_ref, dx_ref, dxb_ref, dg_ref):
        xf = x_ref[...]
        r = lax.rsqrt(jnp.mean(xf * xf, axis=-1, keepdims=True) + EPS)
        xn = xf * r
        d = dh_ref[...]
        dyg = d * g_ref[...]
        dx = r * (dyg - xn * jnp.mean(dyg * xn, axis=-1, keepdims=True)) + dres_ref[...]
        dx_ref[...] = dx
        dxb_ref[...] = dx.astype(BF16)

        @pl.when(pl.program_id(0) == 0)
        def _():
            dg_ref[...] = jnp.zeros_like(dg_ref)

        dg_ref[...] += jnp.sum(d * xn, axis=0, keepdims=True)

    row = pl.BlockSpec((tr, D), lambda i: (i, 0))
    vec = pl.BlockSpec((1, D), lambda i: (0, 0))
    return pl.pallas_call(
        body, name=name, grid=(R // tr,), in_specs=[row, row, vec, row], out_specs=[row, row, vec],
        out_shape=[jax.ShapeDtypeStruct((R, D), F32), jax.ShapeDtypeStruct((R, D), BF16),
                   jax.ShapeDtypeStruct((1, D), F32)],
        compiler_params=_params("arbitrary"),
    )(dh, x, g, dres)


def _loss_head(x2, g, target):
    R, D = x2.shape
    tr = _tile(R, 256)

    def body(x_ref, g_ref, t_ref, dx_ref, dxb_ref, dg_ref, loss_ref):
        xf = x_ref[...]
        gg = g_ref[...]
        r = lax.rsqrt(jnp.mean(xf * xf, axis=-1, keepdims=True) + EPS)
        xn = xf * r
        e = xn * gg - t_ref[...]
        part = 0.5 * jnp.sum(jnp.mean(e * e, axis=-1, keepdims=True), axis=0, keepdims=True)
        dy = e * (1.0 / D)
        dyg = dy * gg
        dx = r * (dyg - xn * jnp.mean(dyg * xn, axis=-1, keepdims=True))
        dx_ref[...] = dx
        dxb_ref[...] = dx.astype(BF16)

        @pl.when(pl.program_id(0) == 0)
        def _():
            dg_ref[...] = jnp.zeros_like(dg_ref)
            loss_ref[...] = jnp.zeros_like(loss_ref)

        dg_ref[...] += jnp.sum(dy * xn, axis=0, keepdims=True)
        loss_ref[...] += jnp.broadcast_to(part, loss_ref.shape)

    row = pl.BlockSpec((tr, D), lambda i: (i, 0))
    vec = pl.BlockSpec((1, D), lambda i: (0, 0))
    return pl.pallas_call(
        body, name="loss_head", grid=(R // tr,), in_specs=[row, vec, row],
        out_specs=[row, row, vec, pl.BlockSpec((1, 128), lambda i: (0, 0))],
        out_shape=[jax.ShapeDtypeStruct((R, D), F32), jax.ShapeDtypeStruct((R, D), BF16),
                   jax.ShapeDtypeStruct((1, D), F32), jax.ShapeDtypeStruct((1, 128), F32)],
        compiler_params=_params("arbitrary"),
    )(x2, g, target)


def _pool_band(tr, reverse, w):
    r = lax.broadcasted_iota(jnp.int32, (tr, tr + POOL_HALO), 0)
    c = lax.broadcasted_iota(jnp.int32, (tr, tr + POOL_HALO), 1)
    if reverse:
        inside = (c >= r) & (c < r + w)
    else:
        cc = c - POOL_HALO
        inside = (cc <= r) & (cc > r - w)
    return jnp.where(inside, 1.0, 0.0).astype(BF16)


def _pool_fwd(proj, S, EMIX):
    PG = EMIX // N_POOL_GROUPS
    cb = _tile(PG, 256)
    tr = _tile(S, 256)
    per_group = PG // cb

    def body(u_ref, o_ref, ext):
        i = pl.program_id(1)
        w = jnp.left_shift(2, pl.program_id(0) // per_group)

        @pl.when(i == 0)
        def _():
            ext[0:POOL_HALO, :] = jnp.zeros((POOL_HALO, cb), BF16)

        u = u_ref[...]
        ext[POOL_HALO:, :] = u
        win = jnp.dot(_pool_band(tr, False, w), ext[...], preferred_element_type=F32)
        pos = i * tr + lax.broadcasted_iota(jnp.int32, (tr, 1), 0)
        cnt = jnp.minimum(pos + 1, w).astype(F32)
        o_ref[...] = (win / cnt - u.astype(F32)).astype(BF16)
        ext[0:POOL_HALO, :] = u[tr - POOL_HALO:, :]

    return pl.pallas_call(
        body, name="pool_fwd", grid=(EMIX // cb, S // tr),
        in_specs=[pl.BlockSpec((tr, cb), lambda j, i: (i, j))],
        out_specs=pl.BlockSpec((tr, cb), lambda j, i: (i, j)),
        out_shape=jax.ShapeDtypeStruct((S, EMIX), BF16),
        scratch_shapes=[pltpu.VMEM((tr + POOL_HALO, cb), BF16)],
        compiler_params=_params("parallel", "arbitrary"),
    )(proj)


def _pool_bwd(dpooled, dproj, S, EMIX):
    PG = EMIX // N_POOL_GROUPS
    cb = _tile(PG, 256)
    tr = _tile(S, 256)
    per_group = PG // cb
    nrt = S // tr

    def body(d_ref, _, o_ref, ext):
        step = pl.program_id(1)
        i = nrt - 1 - step
        w = jnp.left_shift(2, pl.program_id(0) // per_group)

        @pl.when(step == 0)
        def _():
            ext[tr:, :] = jnp.zeros((POOL_HALO, cb), BF16)

        d = d_ref[...]
        pos = i * tr + lax.broadcasted_iota(jnp.int32, (tr, 1), 0)
        cnt = jnp.minimum(pos + 1, w).astype(F32)
        z = (d / cnt).astype(BF16)
        ext[0:tr, :] = z
        win = jnp.dot(_pool_band(tr, True, w), ext[...], preferred_element_type=F32)
        o_ref[...] = (win - d).astype(BF16)
        ext[tr:, :] = z[0:POOL_HALO, :]

    return pl.pallas_call(
        body, name="pool_bwd", grid=(EMIX // cb, nrt),
        in_specs=[pl.BlockSpec((tr, cb), lambda j, s: (nrt - 1 - s, j)), ANY],
        out_specs=pl.BlockSpec((tr, cb), lambda j, s: (nrt - 1 - s, j)),
        out_shape=jax.ShapeDtypeStruct(dproj.shape, dproj.dtype),
        scratch_shapes=[pltpu.VMEM((tr + POOL_HALO, cb), BF16)],
        input_output_aliases={1: 0},
        compiler_params=_params("parallel", "arbitrary"),
    )(dpooled, dproj)


def _ca_fwd(name, proj, qblk, kv, premix, S, ECA, EMIX):
    M = kv.shape[0]
    hd = ECA // CA_HEADS
    ts = _tile(S, 512)
    scale = hd ** -0.5

    def body(q_ref, kv_ref, _, o_ref):
        for h in range(CA_HEADS):
            q = q_ref[:, h * hd:(h + 1) * hd]
            k = kv_ref[:, h * hd:(h + 1) * hd]
            v = kv_ref[:, ECA + h * hd:ECA + (h + 1) * hd]
            s = lax.dot_general(q, k, (NT, ((), ())), preferred_element_type=F32) * scale
            s = s - jnp.max(s, axis=-1, keepdims=True)
            p = jnp.exp(s)
            p = p / jnp.sum(p, axis=-1, keepdims=True)
            o = jnp.dot(p.astype(BF16), v, preferred_element_type=F32)
            o_ref[:, h * hd:(h + 1) * hd] = o.astype(BF16)

    return pl.pallas_call(
        body, name=name, grid=(S // ts,),
        in_specs=[pl.BlockSpec((ts, ECA), lambda i: (i, qblk)), pl.BlockSpec((M, 2 * ECA), lambda i: (0, 0)), ANY],
        out_specs=pl.BlockSpec((ts, ECA), lambda i: (i, EMIX // ECA)),
        out_shape=jax.ShapeDtypeStruct(premix.shape, premix.dtype),
        input_output_aliases={2: 0}, compiler_params=_params("parallel"),
    )(proj, kv, premix)


def _ca_bwd(name, dpremix, proj, qblk, kv, dbuf, dblk, S, ECA, EMIX):
    M = kv.shape[0]
    hd = ECA // CA_HEADS
    ts = _tile(S, 512)
    scale = hd ** -0.5

    def body(do_ref, q_ref, kv_ref, _, dq_ref, dkv_ref):
        @pl.when(pl.program_id(0) == 0)
        def _():
            dkv_ref[...] = jnp.zeros_like(dkv_ref)

        for h in range(CA_HEADS):
            lo, hi = h * hd, (h + 1) * hd
            q = q_ref[:, lo:hi]
            k = kv_ref[:, lo:hi]
            v = kv_ref[:, ECA + lo:ECA + hi]
            do = do_ref[:, lo:hi]
            s = lax.dot_general(q, k, (NT, ((), ())), preferred_element_type=F32) * scale
            s = s - jnp.max(s, axis=-1, keepdims=True)
            p = jnp.exp(s)
            p = p / jnp.sum(p, axis=-1, keepdims=True)
            pb = p.astype(BF16)
            dkv_ref[:, ECA + lo:ECA + hi] += lax.dot_general(pb, do, (TN, ((), ())), preferred_element_type=F32)
            dp = lax.dot_general(do, v, (NT, ((), ())), preferred_element_type=F32)
            ds = (p * (dp - jnp.sum(p * dp, axis=-1, keepdims=True)) * scale).astype(BF16)
            dq_ref[:, lo:hi] = jnp.dot(ds, k, preferred_element_type=F32).astype(BF16)
            dkv_ref[:, lo:hi] += lax.dot_general(ds, q, (TN, ((), ())), preferred_element_type=F32)

    return pl.pallas_call(
        body, name=name, grid=(S // ts,),
        in_specs=[pl.BlockSpec((ts, ECA), lambda i: (i, EMIX // ECA)), pl.BlockSpec((ts, ECA), lambda i: (i, qblk)),
                  pl.BlockSpec((M, 2 * ECA), lambda i: (0, 0)), ANY],
        out_specs=[pl.BlockSpec((ts, ECA), lambda i: (i, dblk)), pl.BlockSpec((M, 2 * ECA), lambda i: (0, 0))],
        out_shape=[jax.ShapeDtypeStruct(dbuf.shape, dbuf.dtype), jax.ShapeDtypeStruct((M, 2 * ECA), F32)],
        input_output_aliases={3: 0}, compiler_params=_params("arbitrary"),
    )(dpremix, proj, kv, dbuf)


def _gate_fwd(name, premix, proj, gblk, colscale, S, EB, ECA):
    ts = _tile(S, 512)

    def body(p_ref, g_ref, c_ref, o_ref):
        g = g_ref[...].astype(F32)
        o_ref[...] = (p_ref[...].astype(F32) * c_ref[...] * (g * _sigmoid(g))).astype(BF16)

    return pl.pallas_call(
        body, name=name, grid=(S // ts, EB // ECA),
        in_specs=[pl.BlockSpec((ts, ECA), lambda i, j: (i, j)), pl.BlockSpec((ts, ECA), lambda i, j: (i, gblk + j)),
                  pl.BlockSpec((1, ECA), lambda i, j: (0, j))],
        out_specs=pl.BlockSpec((ts, ECA), lambda i, j: (i, j)),
        out_shape=jax.ShapeDtypeStruct((S, EB), BF16), compiler_params=_params("parallel", "parallel"),
    )(premix, proj, colscale)


def _gate_bwd(name, dbranch, premix, proj, gblk, colscale, dshape, dblk, S, EB, ECA):
    ts = _tile(S, 512)

    def body(db_ref, p_ref, g_ref, c_ref, dp_ref, dg_ref, dc_ref):
        g = g_ref[...].astype(F32)
        sg = _sigmoid(g)
        si = g * sg
        c = c_ref[...]
        db = db_ref[...].astype(F32)
        t = db * p_ref[...].astype(F32)
        dp_ref[...] = (db * si * c).astype(BF16)
        dg_ref[...] = (t * c * (sg * (1.0 + g * (1.0 - sg)))).astype(BF16)

        @pl.when(pl.program_id(1) == 0)
        def _():
            dc_ref[...] = jnp.zeros_like(dc_ref)

        dc_ref[...] += jnp.sum(t * si, axis=0, keepdims=True)

    blk = pl.BlockSpec((ts, ECA), lambda j, i: (i, j))
    vec = pl.BlockSpec((1, ECA), lambda j, i: (0, j))
    return pl.pallas_call(
        body, name=name, grid=(EB // ECA, S // ts),
        in_specs=[blk, blk, pl.BlockSpec((ts, ECA), lambda j, i: (i, gblk + j)), vec],
        out_specs=[blk, pl.BlockSpec((ts, ECA), lambda j, i: (i, dblk + j)), vec],
        out_shape=[jax.ShapeDtypeStruct((S, EB), BF16), jax.ShapeDtypeStruct(dshape, BF16),
                   jax.ShapeDtypeStruct((1, EB), F32)],
        compiler_params=_params("parallel", "arbitrary"),
    )(dbranch, premix, proj, colscale)


def _hgrn_common(qin, fin, lbh):
    C = HG_CHUNK
    row = lax.broadcasted_iota(jnp.int32, (C, C), 0)
    col = lax.broadcasted_iota(jnp.int32, (C, C), 1)
    causal = row >= col
    sg = _sigmoid(fin)
    f = lbh + (1.0 - lbh) * sg
    k = 1.0 - f
    g = jnp.log(f)
    b = jnp.dot(jnp.where(causal, 1.0, 0.0), g, preferred_element_type=F32, precision=lax.Precision.HIGHEST)
    b_last = jnp.sum(g, axis=0, keepdims=True)
    eb = jnp.exp(b)
    einv = jnp.exp(-b)
    eend = jnp.exp(b_last - b)
    sq = _sigmoid(qin)
    qt = qin * sq * (HG_HEAD_DIM ** -0.5)
    a = qt * eb
    bm = k * einv
    e = k * eend
    d = jnp.exp(b_last)
    p = lax.dot_general(a.astype(BF16), bm.astype(BF16), (NT, ((), ())), preferred_element_type=F32)
    p = jnp.where(causal, p, 0.0)
    return dict(causal=causal, sg=sg, f=f, k=k, eb=eb, einv=einv, eend=eend, sq=sq, a=a, bm=bm, e=e, d=d, p=p)


def _hgrn_lb(lb_ref):
    l0 = lb_ref[0:1, :]
    l1 = lb_ref[1:2, :]
    mx = jnp.maximum(l0, l1)
    e0 = jnp.exp(l0 - mx)
    e1 = jnp.exp(l1 - mx)
    return e1 / (e0 + e1)


def _hgrn_fwd(proj, fgate, hgrn_lb, S, EMIX, EB):
    HD, C = HG_HEAD_DIM, HG_CHUNK
    HH = EMIX // HD
    hb = 2 if HH % 2 == 0 else 1
    W = hb * HD
    tr = _tile(S, 512)
    nch = tr // C

    def body(q_ref, f_ref, i_ref, lb_ref, o_ref, rstd_ref, st_ref, state):
        @pl.when(pl.program_id(1) == 0)
        def _():
            state[...] = jnp.zeros_like(state)

        lb = _hgrn_lb(lb_ref)

        def chunk(ci, carry):
            r0 = pl.multiple_of(ci * C, C)
            for h in range(hb):
                cs = slice(h * HD, (h + 1) * HD)
                qin = q_ref[pl.ds(r0, C), cs].astype(F32)
                fin = f_ref[pl.ds(r0, C), cs]
                v = i_ref[pl.ds(r0, C), cs]
                t = _hgrn_common(qin, fin, lb[:, cs])
                st = state[h]
                st_ref[h, ci] = st
                o = jnp.dot(t["p"].astype(BF16), v, preferred_element_type=F32)
                o = o + lax.dot_general(t["a"].astype(BF16), st.astype(BF16), (NT, ((), ())),
                                        preferred_element_type=F32)
                state[h] = st * t["d"] + lax.dot_general(v, t["e"].astype(BF16), (TN, ((), ())),
                                                         preferred_element_type=F32)
                rstd = lax.rsqrt(jnp.mean(o * o, axis=-1, keepdims=True) + EPS)
                o_ref[pl.ds(r0, C), cs] = (o * rstd).astype(BF16)
                rstd_ref[pl.ds(r0, C), cs] = jnp.broadcast_to(rstd, (C, HD))
            return carry

        lax.fori_loop(0, nch, chunk, 0)

    blk = lambda off: pl.BlockSpec((tr, W), lambda g, i: (i, off + g))
    return pl.pallas_call(
        body, name="hgrn_fwd", grid=(HH // hb, S // tr),
        in_specs=[blk(0), blk(0), blk(2 * EMIX // W), pl.BlockSpec((2, W), lambda g, i: (0, g))],
        out_specs=[blk(0), blk(0), pl.BlockSpec((hb, nch, HD, HD), lambda g, i: (g, i, 0, 0))],
        out_shape=[jax.ShapeDtypeStruct((S, EB), BF16), jax.ShapeDtypeStruct((S, EMIX), F32),
                   jax.ShapeDtypeStruct((HH, S // C, HD, HD), F32)],
        scratch_shapes=[pltpu.VMEM((hb, HD, HD), F32)],
        compiler_params=_params("parallel", "arbitrary"),
    )(proj, fgate, proj, hgrn_lb)


def _hgrn_bwd(dpremix, premix, rstd, states, proj, fgate, hgrn_lb, S, EMIX):
    HD, C = HG_HEAD_DIM, HG_CHUNK
    HH = EMIX // HD
    hb = 2 if HH % 2 == 0 else 1
    W = hb * HD
    tr = _tile(S, 512)
    nch = tr // C
    nrt = S // tr

    def body(do_ref, on_ref, rstd_ref, st_ref, q_ref, f_ref, i_ref, lb_ref, d_ref, dlb_ref, dstate):
        @pl.when(pl.program_id(1) == 0)
        def _():
            dstate[...] = jnp.zeros_like(dstate)
            dlb_ref[...] = jnp.zeros_like(dlb_ref)

        lb = _hgrn_lb(lb_ref)

        def chunk(step, carry):
            ci = nch - 1 - step
            r0 = pl.multiple_of(ci * C, C)
            for h in range(hb):
                cs = slice(h * HD, (h + 1) * HD)
                qin = q_ref[pl.ds(r0, C), cs].astype(F32)
                fin = f_ref[pl.ds(r0, C), cs]
                v = i_ref[pl.ds(r0, C), cs]
                lbh = lb[:, cs]
                t = _hgrn_common(qin, fin, lbh)
                a, bm, e, d, p = t["a"], t["bm"], t["e"], t["d"], t["p"]
                ab, bmb, eb16 = a.astype(BF16), bm.astype(BF16), e.astype(BF16)
                st = st_ref[h, ci]
                dst = dstate[h]
                on = on_ref[pl.ds(r0, C), cs].astype(F32)
                dn = do_ref[pl.ds(r0, C), cs].astype(F32)
                do = rstd_ref[pl.ds(r0, C), cs] * (dn - on * jnp.mean(dn * on, axis=-1, keepdims=True))
                dob = do.astype(BF16)
                dstb = dst.astype(BF16)
                dp = lax.dot_general(dob, v, (NT, ((), ())), preferred_element_type=F32)
                dp = jnp.where(t["causal"], dp, 0.0).astype(BF16)
                dv = lax.dot_general(p.astype(BF16), dob, (TN, ((), ())), preferred_element_type=F32)
                dv = dv + lax.dot_general(eb16, dstb, (NT, ((), ())), preferred_element_type=F32)
                da = jnp.dot(dp, bmb, preferred_element_type=F32)
                da = da + jnp.dot(dob, st.astype(BF16), preferred_element_type=F32)
                dbm = lax.dot_general(dp, ab, (TN, ((), ())), preferred_element_type=F32)
                de = jnp.dot(v, dstb, preferred_element_type=F32)
                dd = jnp.sum(dst * st, axis=0, keepdims=True)
                dstate[h] = dst * d + lax.dot_general(dob, ab, (TN, ((), ())), preferred_element_type=F32)
                dqt = da * t["eb"]
                dk = dbm * t["einv"] + de * t["eend"]
                dee = de * e
                db = da * a - dbm * bm - dee
                extra = jnp.sum(dee, axis=0, keepdims=True) + dd * d
                row = lax.broadcasted_iota(jnp.int32, (C, C), 0)
                col = lax.broadcasted_iota(jnp.int32, (C, C), 1)
                dg = jnp.dot(jnp.where(col >= row, 1.0, 0.0), db, preferred_element_type=F32,
                             precision=lax.Precision.HIGHEST) + extra
                df = dg / t["f"] - dk
                sg, sq = t["sg"], t["sq"]
                d_ref[0, pl.ds(r0, C), cs] = (dqt * (HD ** -0.5) * (sq * (1.0 + qin * (1.0 - sq)))).astype(BF16)
                d_ref[1, pl.ds(r0, C), cs] = (df * (1.0 - lbh) * sg * (1.0 - sg)).astype(BF16)
                d_ref[2, pl.ds(r0, C), cs] = dv.astype(BF16)
                dlb_ref[:, cs] += jnp.sum(df * (1.0 - sg), axis=0, keepdims=True)
            return carry

        lax.fori_loop(0, nch, chunk, 0)

    rev = lambda off: pl.BlockSpec((tr, W), lambda g, s: (nrt - 1 - s, off + g))
    return pl.pallas_call(
        body, name="hgrn_bwd", grid=(HH // hb, nrt),
        in_specs=[rev(0), rev(0), rev(0), pl.BlockSpec((hb, nch, HD, HD), lambda g, s: (g, nrt - 1 - s, 0, 0)),
                  rev(0), rev(0), rev(2 * EMIX // W), pl.BlockSpec((2, W), lambda g, s: (0, g))],
        out_specs=[pl.BlockSpec((3, tr, W), lambda g, s: (0, nrt - 1 - s, g)), pl.BlockSpec((1, W), lambda g, s: (0, g))],
        out_shape=[jax.ShapeDtypeStruct((3, S, EMIX), BF16), jax.ShapeDtypeStruct((1, EMIX), F32)],
        scratch_shapes=[pltpu.VMEM((hb, HD, HD), F32)],
        compiler_params=_params("parallel", "arbitrary"),
    )(dpremix, premix, rstd, states, proj, fgate, proj, hgrn_lb)


def _ew_tiles(R, C):
    return _tile(R, 256), _tile(C, 1024)


def _add_halves(name, core, grad, got):
    _, _, R, C = grad.shape
    tr, tc = _ew_tiles(R, C)

    def body(c_ref, a_ref, b_ref, o_ref):
        o_ref[...] = (a_ref[...].astype(F32) + b_ref[...].astype(F32)).astype(BF16)

    blk = pl.BlockSpec((None, tr, tc), lambda s, i, j, c: (s, i, j))
    return pl.pallas_call(
        body, name=name, out_shape=jax.ShapeDtypeStruct(got.shape, BF16),
        grid_spec=pltpu.PrefetchScalarGridSpec(
            num_scalar_prefetch=1, grid=(N_CHIPS, R // tr, C // tc),
            in_specs=[pl.BlockSpec((None, None, tr, tc), lambda s, i, j, c: (s, c[0], i, j)), blk], out_specs=blk),
        compiler_params=_params("parallel", "parallel", "parallel"),
    )(core, grad, got)


def _sum_slots(name, core, parts):
    _, R, C = parts.shape
    tr, tc = _ew_tiles(R, C)

    def body(c_ref, p_ref, o_ref):
        acc = p_ref[0].astype(F32)
        for s in range(1, N_CHIPS):
            acc = acc + p_ref[s].astype(F32)
        o_ref[...] = acc

    return pl.pallas_call(
        body, name=name, out_shape=jax.ShapeDtypeStruct((2, R, C), F32),
        grid_spec=pltpu.PrefetchScalarGridSpec(
            num_scalar_prefetch=1, grid=(R // tr, C // tc),
            in_specs=[pl.BlockSpec((N_CHIPS, tr, tc), lambda i, j, c: (0, i, j))],
            out_specs=pl.BlockSpec((None, tr, tc), lambda i, j, c: (c[0], i, j))),
        compiler_params=_params("parallel", "parallel"),
    )(core, parts)


def _adamw(name, w, g, m, v):
    R, C = w.shape
    tr, tc = _ew_tiles(R, C)

    def body(w_ref, g_ref, m_ref, v_ref, d_ref, mo_ref, vo_ref):
        g = g_ref[...]
        mn = ADAM_B1 * m_ref[...] + (1.0 - ADAM_B1) * g
        vn = ADAM_B2 * v_ref[...] + (1.0 - ADAM_B2) * (g * g)
        m_hat = mn / (1.0 - ADAM_B1 ** ADAM_STEP)
        v_hat = vn / (1.0 - ADAM_B2 ** ADAM_STEP)
        d_ref[...] = -ADAM_LR * (m_hat / (jnp.sqrt(v_hat) + ADAM_EPS) + ADAM_WD * w_ref[...])
        mo_ref[...] = mn
        vo_ref[...] = vn

    blk = pl.BlockSpec((tr, tc), lambda i, j: (i, j))
    sds = jax.ShapeDtypeStruct((R, C), F32)
    return pl.pallas_call(
        body, name=name, grid=(R // tr, C // tc), in_specs=[blk] * 4, out_specs=[blk] * 3, out_shape=[sds] * 3,
        compiler_params=_params("parallel", "parallel"),
    )(w, g, m, v)


def _small_sum(gathered, lb_rows):
    _, RR, W = gathered.shape
    T = SMALL_ROWS
    LB_TILE = 4

    def body(g_ref, lb_ref, o_ref):
        acc = g_ref[0]
        for dev in range(1, N_DEV):
            acc = acc + g_ref[dev]
        o_ref[0:RR, :] = acc
        l0 = lb_ref[0:1, :]
        l1 = lb_ref[1:2, :]
        mx = jnp.maximum(l0, l1)
        e0 = jnp.exp(l0 - mx)
        e1 = jnp.exp(l1 - mx)
        lb = e1 / (e0 + e1)
        d1 = acc[LB_TILE * T:(LB_TILE + 1) * T, :] * (lb * (1.0 - lb))
        o_ref[RR:RR + T, :] = -d1
        o_ref[RR + T:RR + 2 * T, :] = d1

    vm = pl.BlockSpec(memory_space=pltpu.VMEM)
    return pl.pallas_call(
        body, name="small_sum", in_specs=[vm, vm], out_specs=vm,
        out_shape=jax.ShapeDtypeStruct((RR + 2 * T, W), F32),
    )(gathered, lb_rows)


def _place():
    return lax.axis_index("x"), lax.axis_index("y"), lax.axis_index("c")


def _other_chips(x, y):
    return [(1 - x, y), (x, 1 - y), (1 - x, 1 - y)]


def _allgather_weights(shards):
    nt = len(shards)

    def body(*refs):
        ins, outs = refs[:nt], refs[nt:2 * nt]
        send, recv, fsend, frecv, lsem = refs[2 * nt:]
        x, y, c = _place()
        me = 2 * x + y
        sib = (x, y, 1 - c)
        chips = _other_chips(x, y)

        def half(t, slot, hc):
            h = shards[t].shape[0] // 2
            return outs[t].at[slot, pl.ds(hc * h, h)]

        def copy(t, slot, hc, sems, j, to, src=None):
            return pltpu.make_async_remote_copy(
                src_ref=half(t, slot, hc) if src is None else src, dst_ref=half(t, slot, hc),
                send_sem=sems[0].at[t, j], recv_sem=sems[1].at[t, j], device_id=to, device_id_type=MESH)

        local, first, passed = [], [], []
        for t in range(nt):
            h = shards[t].shape[0] // 2
            cp = pltpu.make_async_copy(ins[t], outs[t].at[me], lsem.at[t])
            cp.start()
            local.append(cp)
            for j, (px, py) in enumerate(chips):
                cp = copy(t, me, c, (send, recv), j, (px, py, c), src=ins[t].at[pl.ds(c * h, h)])
                cp.start()
                first.append(cp)
        for t in range(nt):
            for j, (px, py) in enumerate(chips):
                slot = 2 * px + py
                copy(t, slot, c, (send, recv), j, (px, py, c)).wait_recv()
                cp = copy(t, slot, c, (fsend, frecv), j, sib)
                cp.start()
                passed.append(cp)
        for t in range(nt):
            for j, (px, py) in enumerate(chips):
                copy(t, 2 * px + py, 1 - c, (fsend, frecv), j, sib).wait_recv()
        for cp in first + passed:
            cp.wait_send()
        for cp in local:
            cp.wait()

    return pl.pallas_call(
        body, name="allgather_weights", in_specs=[ANY] * nt, out_specs=[ANY] * nt,
        out_shape=[jax.ShapeDtypeStruct((N_CHIPS,) + s.shape, s.dtype) for s in shards],
        scratch_shapes=[pltpu.SemaphoreType.DMA((nt, 3))] * 4 + [pltpu.SemaphoreType.DMA((nt,))],
    )(*shards)


def _chunk_rows(rows, row_bytes):
    cr = rows
    while cr * row_bytes > STREAM_CHUNK_BYTES and cr % 32 == 0:
        cr //= 2
    return cr


def _stream(pairs, buf, sems, t, peer):
    lsem, ssem, rsem = sems
    sends = []
    for k, (src, dst) in enumerate(pairs):
        slot = k % STREAM_SLOTS
        if k >= STREAM_SLOTS:
            sends[k - STREAM_SLOTS].wait_send()
        load = pltpu.make_async_copy(src, buf.at[slot], lsem.at[t, slot])
        load.start()
        load.wait()
        cp = pltpu.make_async_remote_copy(src_ref=buf.at[slot], dst_ref=dst, send_sem=ssem.at[t, slot],
                                          recv_sem=rsem.at[t], device_id=peer, device_id_type=MESH)
        cp.start()
        sends.append(cp)
    for cp in sends[-STREAM_SLOTS:]:
        cp.wait_send()


def _stream_scratch(shapes):
    nt = len(shapes)
    return ([pltpu.VMEM((STREAM_SLOTS,) + s, d) for s, d in shapes]
            + [pltpu.SemaphoreType.DMA((nt, STREAM_SLOTS)), pltpu.SemaphoreType.DMA((nt, STREAM_SLOTS)),
               pltpu.SemaphoreType.DMA((nt,))])


def _exchange_halves(grads):
    nt = len(grads)
    hs = [g.shape[1] // 2 for g in grads]
    crs = [_chunk_rows(h, g.shape[2] * g.dtype.itemsize) for h, g in zip(hs, grads)]

    def body(*refs):
        ins, gots, bufs, sems = refs[:nt], refs[nt:2 * nt], refs[2 * nt:3 * nt], refs[3 * nt:]
        x, y, c = _place()
        sib = (x, y, 1 - c)
        for t in range(nt):
            h, cr = hs[t], crs[t]
            pairs = [(ins[t].at[b, pl.ds((1 - c) * h + r0, cr)], gots[t].at[b, pl.ds(r0, cr)])
                     for b in range(N_CHIPS) for r0 in range(0, h, cr)]
            _stream(pairs, bufs[t], sems, t, sib)
        for t in range(nt):
            pltpu.make_async_remote_copy(src_ref=gots[t], dst_ref=gots[t], send_sem=sems[1].at[t, 0],
                                         recv_sem=sems[2].at[t], device_id=sib, device_id_type=MESH).wait_recv()

    return pl.pallas_call(
        body, name="exchange_halves", in_specs=[ANY] * nt, out_specs=[ANY] * nt,
        out_shape=[jax.ShapeDtypeStruct((N_CHIPS, h, g.shape[2]), g.dtype) for h, g in zip(hs, grads)],
        scratch_shapes=_stream_scratch([((cr, g.shape[2]), g.dtype) for cr, g in zip(crs, grads)]),
        compiler_params=pltpu.CompilerParams(vmem_limit_bytes=VMEM_LIMIT_BYTES),
    )(*grads)


def _scatter_partials(parts):
    nt = len(parts)

    def body(*refs):
        ins, outs = refs[:nt], refs[nt:2 * nt]
        send, recv, lsem = refs[2 * nt:]
        x, y, c = _place()
        me = 2 * x + y
        copies = []
        for t in range(nt):
            cp = pltpu.make_async_copy(ins[t].at[me], outs[t].at[me], lsem.at[t])
            cp.start()
            copies.append(cp)
            for j, (px, py) in enumerate(_other_chips(x, y)):
                cp = pltpu.make_async_remote_copy(
                    src_ref=ins[t].at[2 * px + py], dst_ref=outs[t].at[me], send_sem=send.at[t, j],
                    recv_sem=recv.at[t, j], device_id=(px, py, c), device_id_type=MESH)
                cp.start()
                copies.append(cp)
        for cp in copies:
            cp.wait()

    return pl.pallas_call(
        body, name="scatter_partials", in_specs=[ANY] * nt, out_specs=[ANY] * nt,
        out_shape=[jax.ShapeDtypeStruct(p.shape, p.dtype) for p in parts],
        scratch_shapes=[pltpu.SemaphoreType.DMA((nt, 3))] * 2 + [pltpu.SemaphoreType.DMA((nt,))],
    )(*parts)


def _share_halves(fulls):
    nt = len(fulls)
    crs = [_chunk_rows(f.shape[1], f.shape[2] * f.dtype.itemsize) for f in fulls]

    def body(*refs):
        outs, bufs, sems = refs[nt:2 * nt], refs[2 * nt:3 * nt], refs[3 * nt:]
        x, y, c = _place()
        sib = (x, y, 1 - c)
        for t in range(nt):
            rows = [outs[t].at[c, pl.ds(r0, crs[t])] for r0 in range(0, fulls[t].shape[1], crs[t])]
            _stream([(r, r) for r in rows], bufs[t], sems, t, sib)
        for t in range(nt):
            other = outs[t].at[1 - c]
            pltpu.make_async_remote_copy(src_ref=other, dst_ref=other, send_sem=sems[1].at[t, 0],
                                         recv_sem=sems[2].at[t], device_id=sib, device_id_type=MESH).wait_recv()

    return pl.pallas_call(
        body, name="share_halves", in_specs=[ANY] * nt, out_specs=[ANY] * nt,
        out_shape=[jax.ShapeDtypeStruct(f.shape, f.dtype) for f in fulls],
        scratch_shapes=_stream_scratch([((cr, f.shape[2]), f.dtype) for cr, f in zip(crs, fulls)]),
        input_output_aliases={t: t for t in range(nt)},
        compiler_params=pltpu.CompilerParams(vmem_limit_bytes=VMEM_LIMIT_BYTES),
    )(*fulls)


def _allgather_small(name, v):
    def body(v_ref, o_ref, send, recv, lsem):
        x, y, c = _place()
        me = 4 * x + 2 * y + c
        loc = pltpu.make_async_copy(v_ref, o_ref.at[me], lsem)
        loc.start()
        copies = []
        for k in range(1, N_DEV):
            px = 1 - x if k & 4 else x
            py = 1 - y if k & 2 else y
            pc = 1 - c if k & 1 else c
            cp = pltpu.make_async_remote_copy(
                src_ref=v_ref, dst_ref=o_ref.at[me], send_sem=send.at[k - 1], recv_sem=recv.at[k - 1],
                device_id=(px, py, pc), device_id_type=MESH)
            cp.start()
            copies.append(cp)
        for cp in copies:
            cp.wait()
        loc.wait()

    vm = pl.BlockSpec(memory_space=pltpu.VMEM)
    return pl.pallas_call(
        body, name=name, in_specs=[vm], out_specs=vm,
        out_shape=jax.ShapeDtypeStruct((N_DEV,) + v.shape, v.dtype),
        scratch_shapes=[pltpu.SemaphoreType.DMA((N_DEV - 1,))] * 2 + [pltpu.SemaphoreType.DMA],
    )(v)


def kernel(x, mem, norm_g, mem_norm_g, w_kv, w_out, pool_w_in, pool_w_grp, pool_scale, hgrn_w_in, hgrn_lb, hgrn_norm_g, final_g, loss_target, m_norm_g, m_mem_norm_g, m_w_kv, m_w_out, m_pool_w_in, m_pool_w_grp, m_pool_scale, m_hgrn_w_in, m_hgrn_lb, m_hgrn_norm_g, m_final_g, v_norm_g, v_mem_norm_g, v_w_kv, v_w_out, v_pool_w_in, v_pool_w_grp, v_pool_scale, v_hgrn_w_in, v_hgrn_lb, v_hgrn_norm_g, v_final_g):
    _, S, D = x.shape
    M = mem.shape[1]
    EB = 2 * D
    ECA = EB // 4
    EMIX = EB - ECA
    PG = EMIX // N_POOL_GROUPS
    NP0 = EMIX + ECA + EB
    NP1 = 3 * EMIX + ECA + EB
    SH0, SH1 = NP0 // N_CHIPS, NP1 // N_CHIPS
    DK, EK = D // N_CHIPS, EB // N_CHIPS
    TNP = 512 if all(v % 512 == 0 for v in (SH0, SH1, ECA, EMIX)) else 256
    TM = _tile(S, 1024)
    TMF = _tile(S, 2048)
    TD = _tile(D, 512)
    TDW = _tile(D, 1024)
    c0, c1 = SH0 // TNP, SH1 // TNP
    qt, et = EMIX // TNP, ECA // TNP
    chip = 2 * lax.axis_index("x") + lax.axis_index("y")

    xs, ms, tgt = x[0], mem[0], loss_target[0]

    flat = lambda w: w.reshape(-1, w.shape[-1])
    shard2d = [flat(w_kv), flat(w_out), flat(pool_w_in), flat(pool_w_grp), flat(hgrn_w_in)]
    g_kv, g_out, g_pin, g_grp, g_hin = _allgather_weights([s.astype(BF16) for s in shard2d])
    wkv = g_kv.reshape(N_CHIPS, 2, DK, 2 * ECA)
    wout = g_out.reshape(N_CHIPS, 2, EK, D)
    wpin = g_pin
    whin = g_hin
    wgrp = g_grp.reshape(N_CHIPS, N_POOL_GROUPS, PG // N_CHIPS, PG).transpose(1, 0, 2, 3).reshape(N_POOL_GROUPS, PG, PG)

    sds = jax.ShapeDtypeStruct
    tdk, tek, tew = _tile(DK, 512), _tile(EK, 512), _tile(EK, 1024)

    mem_n = _rms_fwd("rms_mem", ms, mem_norm_g.reshape(1, D))
    tkv = _tile(2 * ECA, 512)

    def kv_of(layer):
        return _matmul(
            f"kv{layer}", mem_n, wkv, grid=(1, 2 * ECA // tkv, D // tdk),
            a_spec=pl.BlockSpec((M, tdk), lambda i, j, k: (0, k)),
            b_spec=pl.BlockSpec((None, None, tdk, tkv), lambda i, j, k: (k // (DK // tdk), layer, k % (DK // tdk), j)),
            out_shape=sds((M, 2 * ECA), BF16), out_spec=pl.BlockSpec((M, tkv), lambda i, j, k: (0, j)),
            acc_shape=(M, tkv), dims=NN)

    kv = [kv_of(0), kv_of(1)]

    def out_proj(layer, branch, resid):
        return _matmul(
            f"out_proj{layer}", branch, wout, grid=(S // TM, D // TDW, EB // tew),
            a_spec=pl.BlockSpec((TM, tew), IK),
            b_spec=pl.BlockSpec((None, None, tew, TDW), lambda i, j, k: (k // (EK // tew), layer, k % (EK // tew), j)),
            out_shape=sds((S, D), F32), out_spec=pl.BlockSpec((TM, TDW), IJ),
            acc_shape=(TM, TDW), dims=NN, add=resid, add_spec=pl.BlockSpec((TM, TDW), IJ))

    ones_ca = jnp.ones((1, ECA), F32)

    h0 = _rms_fwd("rms0", xs, norm_g[0:1])
    proj0 = _matmul(
        "proj0", h0, wpin, grid=(S // TMF, NP0 // TNP, 1),
        a_spec=pl.BlockSpec((TMF, D), lambda i, j, k: (i, 0)),
        b_spec=pl.BlockSpec((None, D, TNP), lambda i, j, k: (j // c0, 0, j % c0)),
        out_shape=sds((S, NP0), BF16), out_spec=pl.BlockSpec((TMF, TNP), IJ),
        acc_shape=(TMF, TNP), dims=NN)
    pooled = _pool_fwd(proj0, S, EMIX)
    premix0 = _matmul(
        "pool_grp", pooled, wgrp, grid=(S // TM, N_POOL_GROUPS, 1),
        a_spec=pl.BlockSpec((TM, PG), lambda i, j, k: (i, j)),
        b_spec=pl.BlockSpec((None, PG, PG), lambda i, j, k: (j, 0, 0)),
        out_shape=sds((S, EB), BF16), out_spec=pl.BlockSpec((TM, PG), lambda i, j, k: (i, j)),
        acc_shape=(TM, PG), dims=NN)
    premix0 = _ca_fwd("ca_fwd0", proj0, EMIX // ECA, kv[0], premix0, S, ECA, EMIX)
    colscale0 = jnp.concatenate([pool_scale.reshape(1, EMIX), ones_ca], axis=1)
    gblk0 = (EMIX + ECA) // ECA
    branch0 = _gate_fwd("gate_fwd0", premix0, proj0, gblk0, colscale0, S, EB, ECA)
    x1 = out_proj(0, branch0, xs)

    h1 = _rms_fwd("rms1", x1, norm_g[1:2])

    def proj1_cols(name, ncols, col_of, out_cols, out_dtype, out_col_of):
        return _matmul(
            name, h1, whin, grid=(S // TMF, ncols, 1),
            a_spec=pl.BlockSpec((TMF, D), lambda i, j, k: (i, 0)),
            b_spec=pl.BlockSpec((None, D, TNP), lambda i, j, k: (col_of(j) // c1, 0, col_of(j) % c1)),
            out_shape=sds((S, out_cols), out_dtype), out_spec=pl.BlockSpec((TMF, TNP), lambda i, j, k: (i, out_col_of(j))),
            acc_shape=(TMF, TNP), dims=NN)

    skip_f = lambda j: jnp.where(j < qt, j, j + qt)
    proj1 = proj1_cols("proj1", NP1 // TNP - qt, skip_f, NP1, BF16, skip_f)
    fgate = proj1_cols("proj1_f", qt, lambda j: j + qt, EMIX, F32, lambda j: j)
    premix1, rstd1, states = _hgrn_fwd(proj1, fgate, hgrn_lb, S, EMIX, EB)
    premix1 = _ca_fwd("ca_fwd1", proj1, 3 * EMIX // ECA, kv[1], premix1, S, ECA, EMIX)
    norm_tiles = _allgather_small("allgather_norm_g", jnp.pad(hgrn_norm_g, ((0, SMALL_ROWS - 1), (0, 0))))
    hg_norm = norm_tiles[0::2, 0, :].reshape(1, EMIX)
    colscale1 = jnp.concatenate([hg_norm, ones_ca], axis=1)
    gblk1 = (3 * EMIX + ECA) // ECA
    branch1 = _gate_fwd("gate_fwd1", premix1, proj1, gblk1, colscale1, S, EB, ECA)
    x2 = out_proj(1, branch1, x1)

    dx2, dx2b, d_final_g, loss_part = _loss_head(x2, final_g.reshape(1, D), tgt)

    def out_proj_bwd(layer, dxb, branch, buf):
        dbranch = _matmul(
            f"dbranch{layer}", dxb, wout, grid=(S // TMF, EB // tek, 1),
            a_spec=pl.BlockSpec((TMF, D), lambda i, j, k: (i, 0)),
            b_spec=pl.BlockSpec((None, None, tek, D), lambda i, j, k: (j // (EK // tek), layer, j % (EK // tek), 0)),
            out_shape=sds((S, EB), BF16), out_spec=pl.BlockSpec((TMF, tek), IJ), acc_shape=(TMF, tek), dims=NT)
        dw = _matmul(
            f"dwout{layer}", branch, dxb, grid=(EB // tew, D // TD, 1),
            a_spec=pl.BlockSpec((S, tew), lambda i, j, k: (0, i)), b_spec=pl.BlockSpec((S, TD), lambda i, j, k: (0, j)),
            out_shape=sds((N_CHIPS, 2, EK, D), BF16),
            out_spec=pl.BlockSpec((None, None, tew, TD), lambda i, j, k: (i // (EK // tew), layer, i % (EK // tew), j)),
            acc_shape=(tew, TD), dims=TN, alias=buf)
        return dbranch, dw

    def kv_bwd(layer, dkv, buf, dmem_add):
        dkvb = dkv.astype(BF16)
        dmem = _matmul(
            f"dmem{layer}", dkvb, wkv, grid=(1, D // tdk, 2 * ECA // tkv),
            a_spec=pl.BlockSpec((M, tkv), lambda i, j, k: (0, k)),
            b_spec=pl.BlockSpec((None, None, tdk, tkv), lambda i, j, k: (j // (DK // tdk), layer, j % (DK // tdk), k)),
            out_shape=sds((M, D), F32), out_spec=pl.BlockSpec((M, tdk), lambda i, j, k: (0, j)), acc_shape=(M, tdk),
            dims=NT, add=dmem_add, add_spec=pl.BlockSpec((M, tdk), lambda i, j, k: (0, j)))
        dw = _matmul(
            f"dwkv{layer}", mem_n, dkvb, grid=(D // tdk, 2 * ECA // tkv, 1),
            a_spec=pl.BlockSpec((M, tdk), lambda i, j, k: (0, i)), b_spec=pl.BlockSpec((M, tkv), lambda i, j, k: (0, j)),
            out_shape=sds((N_CHIPS, 2, DK, 2 * ECA), BF16),
            out_spec=pl.BlockSpec((None, None, tdk, tkv), lambda i, j, k: (i // (DK // tdk), layer, i % (DK // tdk), j)),
            acc_shape=(tdk, tkv), dims=TN, alias=buf)
        return dmem, dw

    dbranch1, gw_out = out_proj_bwd(1, dx2b, branch1, None)
    dpremix1, drest1, dcol1 = _gate_bwd("gate_bwd1", dbranch1, premix1, proj1, gblk1, colscale1, (S, ECA + EB), 1,
                                        S, EB, ECA)
    drest1, dkv1 = _ca_bwd("ca_bwd1", dpremix1, proj1, 3 * EMIX // ECA, kv[1], drest1, 0, S, ECA, EMIX)
    dqfi, dlb = _hgrn_bwd(dpremix1, premix1, rstd1, states, proj1, fgate, hgrn_lb, S, EMIX)
    nq, nr = 3 * qt, (ECA + EB) // TNP
    dh1 = _matmul(
        "dh1_qfi", dqfi, whin, grid=(S // TM, D // TDW, nq),
        a_spec=pl.BlockSpec((None, TM, TNP), lambda i, j, k: (k // qt, i, k % qt)),
        b_spec=pl.BlockSpec((None, TDW, TNP), lambda i, j, k: (k // c1, j, k % c1)),
        out_shape=sds((S, D), F32), out_spec=pl.BlockSpec((TM, TDW), IJ), acc_shape=(TM, TDW), dims=NT)
    dh1 = _matmul(
        "dh1_rest", drest1, whin, grid=(S // TM, D // TDW, nr), a_spec=pl.BlockSpec((TM, TNP), IK),
        b_spec=pl.BlockSpec((None, TDW, TNP), lambda i, j, k: ((k + nq) // c1, j, (k + nq) % c1)),
        out_shape=sds((S, D), F32), out_spec=pl.BlockSpec((TM, TDW), IJ), acc_shape=(TM, TDW), dims=NT,
        add=dh1, add_spec=pl.BlockSpec((TM, TDW), IJ))
    gw_hin = _matmul(
        "dwhin_qfi", h1, dqfi, grid=(D // TDW, nq, 1), a_spec=pl.BlockSpec((S, TDW), lambda i, j, k: (0, i)),
        b_spec=pl.BlockSpec((None, S, TNP), lambda i, j, k: (j // qt, 0, j % qt)),
        out_shape=sds((N_CHIPS, D, SH1), BF16), out_spec=pl.BlockSpec((None, TDW, TNP), lambda i, j, k: (j // c1, i, j % c1)),
        acc_shape=(TDW, TNP), dims=TN)
    gw_hin = _matmul(
        "dwhin_rest", h1, drest1, grid=(D // TDW, nr, 1), a_spec=pl.BlockSpec((S, TDW), lambda i, j, k: (0, i)),
        b_spec=pl.BlockSpec((S, TNP), lambda i, j, k: (0, j)), out_shape=sds((N_CHIPS, D, SH1), BF16),
        out_spec=pl.BlockSpec((None, TDW, TNP), lambda i, j, k: ((j + nq) // c1, i, (j + nq) % c1)),
        acc_shape=(TDW, TNP), dims=TN, alias=gw_hin)
    dx1, dx1b, d_ng1 = _rms_bwd("rms_bwd1", dh1, x1, norm_g[1:2], dx2)
    dmem, gw_kv = kv_bwd(1, dkv1, None, None)

    dbranch0, gw_out = out_proj_bwd(0, dx1b, branch0, gw_out)
    dpremix0, dproj0, dcol0 = _gate_bwd("gate_bwd0", dbranch0, premix0, proj0, gblk0, colscale0, (S, NP0), gblk0,
                                        S, EB, ECA)
    dproj0, dkv0 = _ca_bwd("ca_bwd0", dpremix0, proj0, EMIX // ECA, kv[0], dproj0, EMIX // ECA, S, ECA, EMIX)
    dpooled = _matmul(
        "dpooled", dpremix0, wgrp, grid=(S // TM, N_POOL_GROUPS, 1), a_spec=pl.BlockSpec((TM, PG), IJ),
        b_spec=pl.BlockSpec((None, PG, PG), lambda i, j, k: (j, 0, 0)),
        out_shape=sds((S, EMIX), F32), out_spec=pl.BlockSpec((TM, PG), IJ), acc_shape=(TM, PG), dims=NT)
    dwgrp = _matmul(
        "dwgrp", pooled, dpremix0, grid=(N_POOL_GROUPS, 1, 1), a_spec=pl.BlockSpec((S, PG), lambda i, j, k: (0, i)),
        b_spec=pl.BlockSpec((S, PG), lambda i, j, k: (0, i)), out_shape=sds((N_POOL_GROUPS, PG, PG), F32),
        out_spec=pl.BlockSpec((None, PG, PG), lambda i, j, k: (i, 0, 0)), acc_shape=(PG, PG), dims=TN)
    dproj0 = _pool_bwd(dpooled, dproj0, S, EMIX)
    dh0 = _matmul(
        "dh0", dproj0, wpin, grid=(S // TM, D // TDW, N_CHIPS), a_spec=pl.BlockSpec((TM, SH0), IK),
        b_spec=pl.BlockSpec((None, TDW, SH0), lambda i, j, k: (k, j, 0)),
        out_shape=sds((S, D), F32), out_spec=pl.BlockSpec((TM, TDW), IJ), acc_shape=(TM, TDW), dims=NT)
    gw_pin = _matmul(
        "dwpin", h0, dproj0, grid=(D // TDW, NP0 // TNP, 1), a_spec=pl.BlockSpec((S, TDW), lambda i, j, k: (0, i)),
        b_spec=pl.BlockSpec((S, TNP), lambda i, j, k: (0, j)), out_shape=sds((N_CHIPS, D, SH0), BF16),
        out_spec=pl.BlockSpec((None, TDW, TNP), lambda i, j, k: (j // c0, i, j % c0)), acc_shape=(TDW, TNP), dims=TN)
    grad_x, _, d_ng0 = _rms_bwd("rms_bwd0", dh0, xs, norm_g[0:1], dx1)
    dmem, gw_kv = kv_bwd(0, dkv0, gw_kv, dmem)
    _, _, d_mng = _rms_bwd("rms_bwd_mem", dmem, ms, mem_norm_g.reshape(1, D), jnp.zeros_like(ms))

    gw_grp = dwgrp.reshape(N_POOL_GROUPS, N_CHIPS, PG // N_CHIPS, PG).transpose(1, 0, 2, 3).reshape(N_CHIPS, PG, PG)
    stacks = [gw_kv.reshape(N_CHIPS, 2 * DK, 2 * ECA), gw_out.reshape(N_CHIPS, 2 * EK, D), gw_pin,
              gw_grp.astype(BF16), gw_hin]
    core = lax.axis_index("c").astype(jnp.int32).reshape(1)
    got = _exchange_halves(stacks)
    parts = [_add_halves(f"add_halves{t}", core, g.reshape(N_CHIPS, 2, g.shape[1] // 2, g.shape[2]), r)
             for t, (g, r) in enumerate(zip(stacks, got))]
    landed = _scatter_partials(parts)
    fulls = _share_halves([_sum_slots(f"sum_slots{t}", core, p) for t, p in enumerate(landed)])
    big_g = [f.reshape(-1, f.shape[-1]) for f in fulls]
    big_names = ["w_kv", "w_out", "pool_w_in", "pool_w_grp", "hgrn_w_in"]
    big_m = [flat(a) for a in (m_w_kv, m_w_out, m_pool_w_in, m_pool_w_grp, m_hgrn_w_in)]
    big_v = [flat(a) for a in (v_w_kv, v_w_out, v_pool_w_in, v_pool_w_grp, v_hgrn_w_in)]
    big_shapes = [w_kv.shape, w_out.shape, pool_w_in.shape, pool_w_grp.shape, hgrn_w_in.shape]
    grads, deltas, new_m, new_v = {}, {}, {}, {}
    for t, n in enumerate(big_names):
        d, mn, vn = _adamw(f"adamw_{n}", shard2d[t], big_g[t], big_m[t], big_v[t])
        grads[n], deltas[n] = big_g[t].reshape(big_shapes[t]), d.reshape(big_shapes[t])
        new_m[n], new_v[n] = mn.reshape(big_shapes[t]), vn.reshape(big_shapes[t])

    Wd = EMIX
    tile_of = lambda v: jnp.pad(v, ((0, SMALL_ROWS - v.shape[0]), (0, Wd - v.shape[1])))
    partial = jnp.concatenate([tile_of(v) for v in (d_ng0, d_ng1, d_mng, dcol0[:, :EMIX], dlb, dcol1[:, :EMIX],
                                                    d_final_g, loss_part)], axis=0)
    summed = _small_sum(_allgather_small("allgather_grads", partial), tile_of(hgrn_lb))
    T = SMALL_ROWS
    tl = lambda i: summed[i * T:(i + 1) * T]
    nshard = EMIX // N_CHIPS
    g_hg_norm = lax.dynamic_slice_in_dim(summed[5 * T:5 * T + 1], chip * nshard, nshard, axis=1)
    small_names = ["norm_g0", "norm_g1", "mem_norm_g", "pool_scale", "hgrn_lb0", "hgrn_lb1", "hgrn_norm_g", "final_g"]
    small_w = [norm_g[0:1], norm_g[1:2], mem_norm_g.reshape(1, D), pool_scale, hgrn_lb[0:1], hgrn_lb[1:2], hgrn_norm_g,
               final_g.reshape(1, D)]
    small_m = [m_norm_g[0:1], m_norm_g[1:2], m_mem_norm_g.reshape(1, D), m_pool_scale, m_hgrn_lb[0:1], m_hgrn_lb[1:2],
               m_hgrn_norm_g, m_final_g.reshape(1, D)]
    small_v = [v_norm_g[0:1], v_norm_g[1:2], v_mem_norm_g.reshape(1, D), v_pool_scale, v_hgrn_lb[0:1], v_hgrn_lb[1:2],
               v_hgrn_norm_g, v_final_g.reshape(1, D)]
    pack = lambda vs: jnp.concatenate([tile_of(v) for v in vs], axis=0)
    g_pack = jnp.concatenate([tl(0), tl(1), tl(2), tl(3), tl(8), tl(9), tile_of(g_hg_norm), tl(6)], axis=0)
    d_pack, m_pack, v_pack = _adamw("adamw_small", pack(small_w), g_pack, pack(small_m), pack(small_v))
    widths = [v.shape[1] for v in small_w]
    rows = lambda p: {n: p[i * T, :widths[i]] for i, n in enumerate(small_names)}

    def assemble(r, out):
        out["norm_g"] = jnp.stack([r["norm_g0"], r["norm_g1"]])
        out["mem_norm_g"] = r["mem_norm_g"]
        out["pool_scale"] = r["pool_scale"].reshape(1, EMIX)
        out["hgrn_lb"] = jnp.stack([r["hgrn_lb0"], r["hgrn_lb1"]])
        out["hgrn_norm_g"] = r["hgrn_norm_g"].reshape(1, nshard)
        out["final_g"] = r["final_g"]

    assemble(rows(g_pack), grads)
    assemble(rows(d_pack), deltas)
    assemble(rows(m_pack), new_m)
    assemble(rows(v_pack), new_v)
    loss = summed[7 * T, 0]

    order = ["norm_g", "mem_norm_g", "w_kv", "w_out", "pool_w_in", "pool_w_grp", "pool_scale", "hgrn_w_in", "hgrn_lb",
             "hgrn_norm_g", "final_g"]
    return (loss, grad_x.reshape(1, S, D), *[grads[n] for n in order], *[deltas[n] for n in order],
            *[new_m[n] for n in order], *[new_v[n] for n in order])
```

```python
import functools

import jax
import jax.numpy as jnp
from jax import lax
from jax.experimental import pallas as pl
from jax.experimental.pallas import tpu as pltpu

F32 = jnp.float32
BF16 = jnp.bfloat16
MESH = pl.DeviceIdType.MESH
ANY = pl.BlockSpec(memory_space=pl.ANY)

EPS = 1e-6
HG_HEAD_DIM = 128
HG_CHUNK = 64
CA_HEADS = 4
N_POOL_GROUPS = 4
POOL_HALO = 128
ADAM_LR = 0.001
ADAM_B1 = 0.9
ADAM_B2 = 0.999
ADAM_EPS = 1e-08
ADAM_WD = 0.01
ADAM_STEP = 10
N_CHIPS = 4
N_DEV = 8
VMEM_LIMIT_BYTES = 56 * 1024 * 1024
SMALL_ROWS = 8
STREAM_CHUNK_BYTES = 2 * 1024 * 1024
STREAM_SLOTS = 2


def _params(*sem):
    return pltpu.CompilerParams(dimension_semantics=sem, vmem_limit_bytes=VMEM_LIMIT_BYTES)


def _tile(n, pref):
    t = pref
    while n % t:
        t //= 2
    return t


def _sigmoid(x):
    return 1.0 / (1.0 + jnp.exp(-x))


def _matmul(name, a, b, *, grid, a_spec, b_spec, out_shape, out_spec, acc_shape, dims,
            add=None, add_spec=None, alias=None):
    nk = grid[2]
    has_add = add is not None
    has_alias = alias is not None

    def body(*refs):
        a_ref, b_ref = refs[0], refs[1]
        pos = 2
        add_ref = None
        if has_add:
            add_ref = refs[pos]
            pos += 1
        if has_alias:
            pos += 1
        o_ref = refs[pos]
        prod = lax.dot_general(a_ref[...], b_ref[...], (dims, ((), ())), preferred_element_type=F32)

        def finish(r):
            if has_add:
                r = r + add_ref[...].astype(F32)
            o_ref[...] = r.astype(o_ref.dtype)

        if nk == 1:
            finish(prod)
            return
        acc_ref = refs[pos + 1]
        k = pl.program_id(2)

        @pl.when(k == 0)
        def _():
            acc_ref[...] = prod

        @pl.when(k > 0)
        def _():
            acc_ref[...] += prod

        @pl.when(k == nk - 1)
        def _():
            finish(acc_ref[...])

    operands = [a, b]
    in_specs = [a_spec, b_spec]
    if has_add:
        operands.append(add)
        in_specs.append(add_spec)
    aliases = {}
    if has_alias:
        aliases = {len(operands): 0}
        operands.append(alias)
        in_specs.append(ANY)
    return pl.pallas_call(
        body, name=name, grid=grid, in_specs=in_specs, out_specs=out_spec, out_shape=out_shape,
        scratch_shapes=[pltpu.VMEM(acc_shape, F32)] if nk > 1 else [], input_output_aliases=aliases,
        compiler_params=_params("parallel", "parallel", "arbitrary"),
    )(*operands)


IJ = lambda i, j, k: (i, j)
IK = lambda i, j, k: (i, k)
KJ = lambda i, j, k: (k, j)
KI = lambda i, j, k: (k, i)
NN = ((1,), (0,))
NT = ((1,), (1,))
TN = ((0,), (0,))


def _rms_fwd(name, x, g):
    R, D = x.shape
    tr = _tile(R, 256)

    def body(x_ref, g_ref, o_ref):
        xf = x_ref[...]
        r = lax.rsqrt(jnp.mean(xf * xf, axis=-1, keepdims=True) + EPS)
        o_ref[...] = (xf * r * g_ref[...]).astype(o_ref.dtype)

    return pl.pallas_call(
        body, name=name, grid=(R // tr,),
        in_specs=[pl.BlockSpec((tr, D), lambda i: (i, 0)), pl.BlockSpec((1, D), lambda i: (0, 0))],
        out_specs=pl.BlockSpec((tr, D), lambda i: (i, 0)),
        out_shape=jax.ShapeDtypeStruct((R, D), BF16), compiler_params=_params("parallel"),
    )(x, g)


def _rms_bwd(name, dh, x, g, dres):
    R, D = x.shape
    tr = _tile(R, 256)

    def body(dh_ref, x_ref, g_ref, dres_ref, dx_ref, dxb_ref, dg_ref):
        xf = x_ref[...]
        r = lax.rsqrt(jnp.mean(xf * xf, axis=-1, keepdims=True) + EPS)
        xn = xf * r
        d = dh_ref[...]
        dyg = d * g_ref[...]
        dx = r * (dyg - xn * jnp.mean(dyg * xn, axis=-1, keepdims=True)) + dres_ref[...]
        dx_ref[...] = dx
        dxb_ref[...] = dx.astype(BF16)

        @pl.when(pl.program_id(0) == 0)
        def _():
            dg_ref[...] = jnp.zeros_like(dg_ref)

        dg_ref[...] += jnp.sum(d * xn, axis=0, keepdims=True)

    row = pl.BlockSpec((tr, D), lambda i: (i, 0))
    vec = pl.BlockSpec((1, D), lambda i: (0, 0))
    return pl.pallas_call(
        body, name=name, grid=(R // tr,), in_specs=[row, row, vec, row], out_specs=[row, row, vec],
        out_shape=[jax.ShapeDtypeStruct((R, D), F32), jax.ShapeDtypeStruct((R, D), BF16),
                   jax.ShapeDtypeStruct((1, D), F32)],
        compiler_params=_params("arbitrary"),
    )(dh, x, g, dres)


def _loss_head(x2, g, target):
    R, D = x2.shape
    tr = _tile(R, 256)

    def body(x_ref, g_ref, t_ref, dx_ref, dxb_ref, dg_ref, loss_ref):
        xf = x_ref[...]
        gg = g_ref[...]
        r = lax.rsqrt(jnp.mean(xf * xf, axis=-1, keepdims=True) + EPS)
        xn = xf * r
        e = xn * gg - t_ref[...]
        part = 0.5 * jnp.sum(jnp.mean(e * e, axis=-1, keepdims=True), axis=0, keepdims=True)
        dy = e * (1.0 / D)
        dyg = dy * gg
        dx = r * (dyg - xn * jnp.mean(dyg * xn, axis=-1, keepdims=True))
        dx_ref[...] = dx
        dxb_ref[...] = dx.astype(BF16)

        @pl.when(pl.program_id(0) == 0)
        def _():
            dg_ref[...] = jnp.zeros_like(dg_ref)
            loss_ref[...] = jnp.zeros_like(loss_ref)

        dg_ref[...] += jnp.sum(dy * xn, axis=0, keepdims=True)
        loss_ref[...] += jnp.broadcast_to(part, loss_ref.shape)

    row = pl.BlockSpec((tr, D), lambda i: (i, 0))
    vec = pl.BlockSpec((1, D), lambda i: (0, 0))
    return pl.pallas_call(
        body, name="loss_head", grid=(R // tr,), in_specs=[row, vec, row],
        out_specs=[row, row, vec, pl.BlockSpec((1, 128), lambda i: (0, 0))],
        out_shape=[jax.ShapeDtypeStruct((R, D), F32), jax.ShapeDtypeStruct((R, D), BF16),
                   jax.ShapeDtypeStruct((1, D), F32), jax.ShapeDtypeStruct((1, 128), F32)],
        compiler_params=_params("arbitrary"),
    )(x2, g, target)


def _pool_band(tr, reverse, w):
    r = lax.broadcasted_iota(jnp.int32, (tr, tr + POOL_HALO), 0)
    c = lax.broadcasted_iota(jnp.int32, (tr, tr + POOL_HALO), 1)
    if reverse:
        inside = (c >= r) & (c < r + w)
    else:
        cc = c - POOL_HALO
        inside = (cc <= r) & (cc > r - w)
    return jnp.where(inside, 1.0, 0.0).astype(BF16)


def _pool_fwd(proj, S, EMIX):
    PG = EMIX // N_POOL_GROUPS
    cb = PG
    tr = _tile(S, 512)
    per_group = PG // cb

    def body(u_ref, o_ref, ext):
        i = pl.program_id(1)
        w = jnp.left_shift(2, pl.program_id(0) // per_group)

        @pl.when(i == 0)
        def _():
            ext[0:POOL_HALO, :] = jnp.zeros((POOL_HALO, cb), BF16)

        u = u_ref[...]
        ext[POOL_HALO:, :] = u
        win = jnp.dot(_pool_band(tr, False, w), ext[...], preferred_element_type=F32)
        pos = i * tr + lax.broadcasted_iota(jnp.int32, (tr, 1), 0)
        cnt = jnp.minimum(pos + 1, w).astype(F32)
        o_ref[...] = (win / cnt - u.astype(F32)).astype(BF16)
        ext[0:POOL_HALO, :] = u[tr - POOL_HALO:, :]

    return pl.pallas_call(
        body, name="pool_fwd", grid=(EMIX // cb, S // tr),
        in_specs=[pl.BlockSpec((tr, cb), lambda j, i: (i, j))],
        out_specs=pl.BlockSpec((tr, cb), lambda j, i: (i, j)),
        out_shape=jax.ShapeDtypeStruct((S, EMIX), BF16),
        scratch_shapes=[pltpu.VMEM((tr + POOL_HALO, cb), BF16)],
        compiler_params=_params("parallel", "arbitrary"),
    )(proj)


def _pool_bwd(dpooled, dproj, S, EMIX):
    PG = EMIX // N_POOL_GROUPS
    cb = PG
    tr = _tile(S, 512)
    per_group = PG // cb
    nrt = S // tr

    def body(d_ref, _, o_ref, ext):
        step = pl.program_id(1)
        i = nrt - 1 - step
        w = jnp.left_shift(2, pl.program_id(0) // per_group)

        @pl.when(step == 0)
        def _():
            ext[tr:, :] = jnp.zeros((POOL_HALO, cb), BF16)

        d = d_ref[...]
        pos = i * tr + lax.broadcasted_iota(jnp.int32, (tr, 1), 0)
        cnt = jnp.minimum(pos + 1, w).astype(F32)
        z = (d / cnt).astype(BF16)
        ext[0:tr, :] = z
        win = jnp.dot(_pool_band(tr, True, w), ext[...], preferred_element_type=F32)
        o_ref[...] = (win - d).astype(BF16)
        ext[tr:, :] = z[0:POOL_HALO, :]

    return pl.pallas_call(
        body, name="pool_bwd", grid=(EMIX // cb, nrt),
        in_specs=[pl.BlockSpec((tr, cb), lambda j, s: (nrt - 1 - s, j)), ANY],
        out_specs=pl.BlockSpec((tr, cb), lambda j, s: (nrt - 1 - s, j)),
        out_shape=jax.ShapeDtypeStruct(dproj.shape, dproj.dtype),
        scratch_shapes=[pltpu.VMEM((tr + POOL_HALO, cb), BF16)],
        input_output_aliases={1: 0},
        compiler_params=_params("parallel", "arbitrary"),
    )(dpooled, dproj)


def _ca_fwd(name, proj, qblk, kv, premix, S, ECA, EMIX):
    M = kv.shape[0]
    hd = ECA // CA_HEADS
    ts = _tile(S, 512)
    scale = hd ** -0.5

    def body(q_ref, kv_ref, _, o_ref):
        for h in range(CA_HEADS):
            q = q_ref[:, h * hd:(h + 1) * hd]
            k = kv_ref[:, h * hd:(h + 1) * hd]
            v = kv_ref[:, ECA + h * hd:ECA + (h + 1) * hd]
            s = lax.dot_general(q, k, (NT, ((), ())), preferred_element_type=F32) * scale
            s = s - jnp.max(s, axis=-1, keepdims=True)
            p = jnp.exp(s)
            p = p / jnp.sum(p, axis=-1, keepdims=True)
            o = jnp.dot(p.astype(BF16), v, preferred_element_type=F32)
            o_ref[:, h * hd:(h + 1) * hd] = o.astype(BF16)

    return pl.pallas_call(
        body, name=name, grid=(S // ts,),
        in_specs=[pl.BlockSpec((ts, ECA), lambda i: (i, qblk)), pl.BlockSpec((M, 2 * ECA), lambda i: (0, 0)), ANY],
        out_specs=pl.BlockSpec((ts, ECA), lambda i: (i, EMIX // ECA)),
        out_shape=jax.ShapeDtypeStruct(premix.shape, premix.dtype),
        input_output_aliases={2: 0}, compiler_params=_params("parallel"),
    )(proj, kv, premix)


def _ca_bwd(name, dpremix, proj, qblk, kv, dbuf, dblk, S, ECA, EMIX):
    M = kv.shape[0]
    hd = ECA // CA_HEADS
    ts = _tile(S, 512)
    scale = hd ** -0.5

    def body(do_ref, q_ref, kv_ref, _, dq_ref, dkv_ref):
        @pl.when(pl.program_id(0) == 0)
        def _():
            dkv_ref[...] = jnp.zeros_like(dkv_ref)

        for h in range(CA_HEADS):
            lo, hi = h * hd, (h + 1) * hd
            q = q_ref[:, lo:hi]
            k = kv_ref[:, lo:hi]
            v = kv_ref[:, ECA + lo:ECA + hi]
            do = do_ref[:, lo:hi]
            s = lax.dot_general(q, k, (NT, ((), ())), preferred_element_type=F32) * scale
            s = s - jnp.max(s, axis=-1, keepdims=True)
            p = jnp.exp(s)
            p = p / jnp.sum(p, axis=-1, keepdims=True)
            pb = p.astype(BF16)
            dkv_ref[:, ECA + lo:ECA + hi] += lax.dot_general(pb, do, (TN, ((), ())), preferred_element_type=F32)
            dp = lax.dot_general(do, v, (NT, ((), ())), preferred_element_type=F32)
            ds = (p * (dp - jnp.sum(p * dp, axis=-1, keepdims=True)) * scale).astype(BF16)
            dq_ref[:, lo:hi] = jnp.dot(ds, k, preferred_element_type=F32).astype(BF16)
            dkv_ref[:, lo:hi] += lax.dot_general(ds, q, (TN, ((), ())), preferred_element_type=F32)

    return pl.pallas_call(
        body, name=name, grid=(S // ts,),
        in_specs=[pl.BlockSpec((ts, ECA), lambda i: (i, EMIX // ECA)), pl.BlockSpec((ts, ECA), lambda i: (i, qblk)),
                  pl.BlockSpec((M, 2 * ECA), lambda i: (0, 0)), ANY],
        out_specs=[pl.BlockSpec((ts, ECA), lambda i: (i, dblk)), pl.BlockSpec((M, 2 * ECA), lambda i: (0, 0))],
        out_shape=[jax.ShapeDtypeStruct(dbuf.shape, dbuf.dtype), jax.ShapeDtypeStruct((M, 2 * ECA), F32)],
        input_output_aliases={3: 0}, compiler_params=_params("arbitrary"),
    )(dpremix, proj, kv, dbuf)


def _gate_fwd(name, premix, proj, gblk, colscale, S, EB, ECA):
    ts = _tile(S, 512)

    def body(p_ref, g_ref, c_ref, o_ref):
        g = g_ref[...].astype(F32)
        o_ref[...] = (p_ref[...].astype(F32) * c_ref[...] * (g * _sigmoid(g))).astype(BF16)

    return pl.pallas_call(
        body, name=name, grid=(S // ts, EB // ECA),
        in_specs=[pl.BlockSpec((ts, ECA), lambda i, j: (i, j)), pl.BlockSpec((ts, ECA), lambda i, j: (i, gblk + j)),
                  pl.BlockSpec((1, ECA), lambda i, j: (0, j))],
        out_specs=pl.BlockSpec((ts, ECA), lambda i, j: (i, j)),
        out_shape=jax.ShapeDtypeStruct((S, EB), BF16), compiler_params=_params("parallel", "parallel"),
    )(premix, proj, colscale)


def _gate_bwd(name, dbranch, premix, proj, gblk, colscale, dshape, dblk, S, EB, ECA):
    ts = _tile(S, 512)

    def body(db_ref, p_ref, g_ref, c_ref, dp_ref, dg_ref, dc_ref):
        g = g_ref[...].astype(F32)
        sg = _sigmoid(g)
        si = g * sg
        c = c_ref[...]
        db = db_ref[...].astype(F32)
        t = db * p_ref[...].astype(F32)
        dp_ref[...] = (db * si * c).astype(BF16)
        dg_ref[...] = (t * c * (sg * (1.0 + g * (1.0 - sg)))).astype(BF16)

        @pl.when(pl.program_id(1) == 0)
        def _():
            dc_ref[...] = jnp.zeros_like(dc_ref)

        dc_ref[...] += jnp.sum(t * si, axis=0, keepdims=True)

    blk = pl.BlockSpec((ts, ECA), lambda j, i: (i, j))
    vec = pl.BlockSpec((1, ECA), lambda j, i: (0, j))
    return pl.pallas_call(
        body, name=name, grid=(EB // ECA, S // ts),
        in_specs=[blk, blk, pl.BlockSpec((ts, ECA), lambda j, i: (i, gblk + j)), vec],
        out_specs=[blk, pl.BlockSpec((ts, ECA), lambda j, i: (i, dblk + j)), vec],
        out_shape=[jax.ShapeDtypeStruct((S, EB), BF16), jax.ShapeDtypeStruct(dshape, BF16),
                   jax.ShapeDtypeStruct((1, EB), F32)],
        compiler_params=_params("parallel", "arbitrary"),
    )(dbranch, premix, proj, colscale)


def _hgrn_lb(lb_ref):
    l0 = lb_ref[0:1, :]
    l1 = lb_ref[1:2, :]
    mx = jnp.maximum(l0, l1)
    e0 = jnp.exp(l0 - mx)
    e1 = jnp.exp(l1 - mx)
    return e1 / (e0 + e1)


def _bdot(a, b, ca, cb, precision=None):
    return lax.dot_general(a, b, (((ca,), (cb,)), ((0,), (0,))), preferred_element_type=F32, precision=precision)


def _hgrn_chunks(qin, fin, lbh, n):
    C = HG_CHUNK
    row = lax.broadcasted_iota(jnp.int32, (n, C, C), 1)
    col = lax.broadcasted_iota(jnp.int32, (n, C, C), 2)
    causal = row >= col
    sg = _sigmoid(fin)
    f = lbh + (1.0 - lbh) * sg
    k = 1.0 - f
    g = jnp.log(f)
    b = _bdot(jnp.where(causal, 1.0, 0.0), g, 2, 1, lax.Precision.HIGHEST)
    b_last = jnp.sum(g, axis=1, keepdims=True)
    eb = jnp.exp(b)
    einv = jnp.exp(-b)
    eend = jnp.exp(b_last - b)
    sq = _sigmoid(qin)
    a = qin * sq * (HG_HEAD_DIM ** -0.5) * eb
    bm = k * einv
    e = k * eend
    d = jnp.exp(b_last)
    p = jnp.where(causal, _bdot(a.astype(BF16), bm.astype(BF16), 2, 2), 0.0)
    return dict(causal=causal, sg=sg, f=f, eb=eb, einv=einv, eend=eend, sq=sq, a=a, bm=bm, e=e, d=d, p=p)


def _hgrn_fwd(proj, fgate, hgrn_lb, S, EMIX, EB):
    HD, C = HG_HEAD_DIM, HG_CHUNK
    HH = EMIX // HD
    hb = 2 if HH % 2 == 0 else 1
    W = hb * HD
    tr = _tile(S, 512)
    n = tr // C

    def body(q_ref, f_ref, i_ref, lb_ref, o_ref, rstd_ref, st_ref, state):
        @pl.when(pl.program_id(1) == 0)
        def _():
            state[...] = jnp.zeros_like(state)

        lb = _hgrn_lb(lb_ref)
        for h in range(hb):
            cs = slice(h * HD, (h + 1) * HD)
            qin = q_ref[:, cs].astype(F32).reshape(n, C, HD)
            fin = f_ref[:, cs].reshape(n, C, HD)
            v = i_ref[:, cs].reshape(n, C, HD)
            t = _hgrn_chunks(qin, fin, lb[:, cs], n)
            upd = _bdot(v, t["e"].astype(BF16), 1, 1)
            st = state[h]
            for c in range(n):
                st_ref[h, c] = st
                st = st * t["d"][c] + upd[c]
            state[h] = st
            o = _bdot(t["p"].astype(BF16), v, 2, 1) + _bdot(t["a"].astype(BF16), st_ref[h].astype(BF16), 2, 2)
            rstd = lax.rsqrt(jnp.mean(o * o, axis=-1, keepdims=True) + EPS)
            o_ref[:, cs] = (o * rstd).reshape(tr, HD).astype(BF16)
            rstd_ref[:, cs] = jnp.broadcast_to(rstd, (n, C, HD)).reshape(tr, HD)

    blk = lambda off: pl.BlockSpec((tr, W), lambda g, i: (i, off + g))
    return pl.pallas_call(
        body, name="hgrn_fwd", grid=(HH // hb, S // tr),
        in_specs=[blk(0), blk(0), blk(2 * EMIX // W), pl.BlockSpec((2, W), lambda g, i: (0, g))],
        out_specs=[blk(0), blk(0), pl.BlockSpec((hb, n, HD, HD), lambda g, i: (g, i, 0, 0))],
        out_shape=[jax.ShapeDtypeStruct((S, EB), BF16), jax.ShapeDtypeStruct((S, EMIX), F32),
                   jax.ShapeDtypeStruct((HH, S // C, HD, HD), F32)],
        scratch_shapes=[pltpu.VMEM((hb, HD, HD), F32)],
        compiler_params=_params("parallel", "arbitrary"),
    )(proj, fgate, proj, hgrn_lb)


def _hgrn_bwd(dpremix, premix, rstd, states, proj, fgate, hgrn_lb, S, EMIX):
    HD, C = HG_HEAD_DIM, HG_CHUNK
    HH = EMIX // HD
    hb = 2 if HH % 2 == 0 else 1
    W = hb * HD
    tr = _tile(S, 512)
    n = tr // C
    nrt = S // tr

    def body(do_ref, on_ref, rstd_ref, st_ref, q_ref, f_ref, i_ref, lb_ref, d_ref, dlb_ref, dstate, dsbuf):
        @pl.when(pl.program_id(1) == 0)
        def _():
            dstate[...] = jnp.zeros_like(dstate)
            dlb_ref[...] = jnp.zeros_like(dlb_ref)

        lb = _hgrn_lb(lb_ref)
        for h in range(hb):
            cs = slice(h * HD, (h + 1) * HD)
            qin = q_ref[:, cs].astype(F32).reshape(n, C, HD)
            fin = f_ref[:, cs].reshape(n, C, HD)
            v = i_ref[:, cs].reshape(n, C, HD)
            lbh = lb[:, cs]
            t = _hgrn_chunks(qin, fin, lbh, n)
            a, bm, e, d, p = t["a"], t["bm"], t["e"], t["d"], t["p"]
            ab, bmb, eb16 = a.astype(BF16), bm.astype(BF16), e.astype(BF16)
            on = on_ref[:, cs].astype(F32).reshape(n, C, HD)
            dn = do_ref[:, cs].astype(F32).reshape(n, C, HD)
            do = rstd_ref[:, cs].reshape(n, C, HD) * (dn - on * jnp.mean(dn * on, axis=-1, keepdims=True))
            dob = do.astype(BF16)
            grow = _bdot(dob, ab, 1, 1)
            ds = dstate[h]
            for c in reversed(range(n)):
                dsbuf[h, c] = ds
                ds = ds * d[c] + grow[c]
            dstate[h] = ds
            dst = dsbuf[h]
            st = st_ref[h]
            dstb = dst.astype(BF16)
            dp = jnp.where(t["causal"], _bdot(dob, v, 2, 2), 0.0).astype(BF16)
            dv = _bdot(p.astype(BF16), dob, 1, 1) + _bdot(eb16, dstb, 2, 2)
            da = _bdot(dp, bmb, 2, 1) + _bdot(dob, st.astype(BF16), 2, 1)
            dbm = _bdot(dp, ab, 1, 1)
            de = _bdot(v, dstb, 2, 1)
            dd = jnp.sum(dst * st, axis=1, keepdims=True)
            dk = dbm * t["einv"] + de * t["eend"]
            dee = de * e
            db = da * a - dbm * bm - dee
            extra = jnp.sum(dee, axis=1, keepdims=True) + dd * d
            upper = jnp.where(lax.broadcasted_iota(jnp.int32, (n, C, C), 2)
                              >= lax.broadcasted_iota(jnp.int32, (n, C, C), 1), 1.0, 0.0)
            dg = _bdot(upper, db, 2, 1, lax.Precision.HIGHEST) + extra
            df = dg / t["f"] - dk
            sg, sq = t["sg"], t["sq"]
            dq = da * t["eb"] * (HD ** -0.5) * (sq * (1.0 + qin * (1.0 - sq)))
            d_ref[0, :, cs] = dq.reshape(tr, HD).astype(BF16)
            d_ref[1, :, cs] = (df * (1.0 - lbh) * sg * (1.0 - sg)).reshape(tr, HD).astype(BF16)
            d_ref[2, :, cs] = dv.reshape(tr, HD).astype(BF16)
            dlb_ref[:, cs] += jnp.sum((df * (1.0 - sg)).reshape(tr, HD), axis=0, keepdims=True)

    rev = lambda off: pl.BlockSpec((tr, W), lambda g, s: (nrt - 1 - s, off + g))
    return pl.pallas_call(
        body, name="hgrn_bwd", grid=(HH // hb, nrt),
        in_specs=[rev(0), rev(0), rev(0), pl.BlockSpec((hb, n, HD, HD), lambda g, s: (g, nrt - 1 - s, 0, 0)),
                  rev(0), rev(0), rev(2 * EMIX // W), pl.BlockSpec((2, W), lambda g, s: (0, g))],
        out_specs=[pl.BlockSpec((3, tr, W), lambda g, s: (0, nrt - 1 - s, g)), pl.BlockSpec((1, W), lambda g, s: (0, g))],
        out_shape=[jax.ShapeDtypeStruct((3, S, EMIX), BF16), jax.ShapeDtypeStruct((1, EMIX), F32)],
        scratch_shapes=[pltpu.VMEM((hb, HD, HD), F32), pltpu.VMEM((hb, n, HD, HD), F32)],
        compiler_params=_params("parallel", "arbitrary"),
    )(dpremix, premix, rstd, states, proj, fgate, proj, hgrn_lb)


def _ew_tiles(R, C):
    return _tile(R, 256), _tile(C, 1024)


def _add_halves(name, core, grad, got):
    _, _, R, C = grad.shape
    tr, tc = _ew_tiles(R, C)

    def body(c_ref, a_ref, b_ref, o_ref):
        o_ref[...] = (a_ref[...].astype(F32) + b_ref[...].astype(F32)).astype(BF16)

    blk = pl.BlockSpec((None, tr, tc), lambda s, i, j, c: (s, i, j))
    return pl.pallas_call(
        body, name=name, out_shape=jax.ShapeDtypeStruct(got.shape, BF16),
        grid_spec=pltpu.PrefetchScalarGridSpec(
            num_scalar_prefetch=1, grid=(N_CHIPS, R // tr, C // tc),
            in_specs=[pl.BlockSpec((None, None, tr, tc), lambda s, i, j, c: (s, c[0], i, j)), blk], out_specs=blk),
        compiler_params=_params("parallel", "parallel", "parallel"),
    )(core, grad, got)


def _sum_slots(name, core, parts):
    _, R, C = parts.shape
    tr, tc = _ew_tiles(R, C)

    def body(c_ref, p_ref, o_ref):
        acc = p_ref[0].astype(F32)
        for s in range(1, N_CHIPS):
            acc = acc + p_ref[s].astype(F32)
        o_ref[...] = acc

    return pl.pallas_call(
        body, name=name, out_shape=jax.ShapeDtypeStruct((2, R, C), F32),
        grid_spec=pltpu.PrefetchScalarGridSpec(
            num_scalar_prefetch=1, grid=(R // tr, C // tc),
            in_specs=[pl.BlockSpec((N_CHIPS, tr, tc), lambda i, j, c: (0, i, j))],
            out_specs=pl.BlockSpec((None, tr, tc), lambda i, j, c: (c[0], i, j))),
        compiler_params=_params("parallel", "parallel"),
    )(core, parts)


def _adamw(name, w, g, m, v):
    R, C = w.shape
    tr, tc = _ew_tiles(R, C)

    def body(w_ref, g_ref, m_ref, v_ref, d_ref, mo_ref, vo_ref):
        g = g_ref[...]
        mn = ADAM_B1 * m_ref[...] + (1.0 - ADAM_B1) * g
        vn = ADAM_B2 * v_ref[...] + (1.0 - ADAM_B2) * (g * g)
        m_hat = mn / (1.0 - ADAM_B1 ** ADAM_STEP)
        v_hat = vn / (1.0 - ADAM_B2 ** ADAM_STEP)
        d_ref[...] = -ADAM_LR * (m_hat / (jnp.sqrt(v_hat) + ADAM_EPS) + ADAM_WD * w_ref[...])
        mo_ref[...] = mn
        vo_ref[...] = vn

    blk = pl.BlockSpec((tr, tc), lambda i, j: (i, j))
    sds = jax.ShapeDtypeStruct((R, C), F32)
    return pl.pallas_call(
        body, name=name, grid=(R // tr, C // tc), in_specs=[blk] * 4, out_specs=[blk] * 3, out_shape=[sds] * 3,
        compiler_params=_params("parallel", "parallel"),
    )(w, g, m, v)


def _pack_rows(name, vecs, W):
    nv = len(vecs)

    def body(*refs):
        o_ref = refs[nv]
        o_ref[...] = jnp.zeros_like(o_ref)
        for i in range(nv):
            o_ref[i:i + 1, 0:vecs[i].shape[1]] = refs[i][...]

    vm = pl.BlockSpec(memory_space=pltpu.VMEM)
    return pl.pallas_call(
        body, name=name, in_specs=[vm] * nv, out_specs=vm, out_shape=jax.ShapeDtypeStruct((SMALL_ROWS, W), F32),
    )(*vecs)


def _small_sum(gathered, hgrn_lb, lb_row):
    _, T, W = gathered.shape

    def body(g_ref, lb_ref, o_ref):
        acc = g_ref[0]
        for dev in range(1, N_DEV):
            acc = acc + g_ref[dev]
        o_ref[0:T, :] = acc
        lb = _hgrn_lb(lb_ref)
        d1 = o_ref[lb_row:lb_row + 1, :] * (lb * (1.0 - lb))
        o_ref[T:2 * T, :] = jnp.zeros((T, W), F32)
        o_ref[T:T + 1, :] = -d1
        o_ref[T + 1:T + 2, :] = d1

    vm = pl.BlockSpec(memory_space=pltpu.VMEM)
    return pl.pallas_call(
        body, name="small_sum", in_specs=[vm, vm], out_specs=vm, out_shape=jax.ShapeDtypeStruct((2 * T, W), F32),
    )(gathered, hgrn_lb)


def _place():
    return lax.axis_index("x"), lax.axis_index("y"), lax.axis_index("c")


def _other_chips(x, y):
    return [(1 - x, y), (x, 1 - y), (1 - x, 1 - y)]


def _allgather_weights(shards):
    nt = len(shards)

    def body(*refs):
        ins, outs = refs[:nt], refs[nt:2 * nt]
        send, recv, fsend, frecv, lsem = refs[2 * nt:]
        x, y, c = _place()
        me = 2 * x + y
        sib = (x, y, 1 - c)
        chips = _other_chips(x, y)

        def half(t, slot, hc):
            h = shards[t].shape[0] // 2
            return outs[t].at[slot, pl.ds(hc * h, h)]

        def copy(t, slot, hc, sems, j, to, src=None):
            return pltpu.make_async_remote_copy(
                src_ref=half(t, slot, hc) if src is None else src, dst_ref=half(t, slot, hc),
                send_sem=sems[0].at[t, j], recv_sem=sems[1].at[t, j], device_id=to, device_id_type=MESH)

        local, first, passed = [], [], []
        for t in range(nt):
            h = shards[t].shape[0] // 2
            cp = pltpu.make_async_copy(ins[t], outs[t].at[me], lsem.at[t])
            cp.start()
            local.append(cp)
            for j, (px, py) in enumerate(chips):
                cp = copy(t, me, c, (send, recv), j, (px, py, c), src=ins[t].at[pl.ds(c * h, h)])
                cp.start()
                first.append(cp)
        for t in range(nt):
            for j, (px, py) in enumerate(chips):
                slot = 2 * px + py
                copy(t, slot, c, (send, recv), j, (px, py, c)).wait_recv()
                cp = copy(t, slot, c, (fsend, frecv), j, sib)
                cp.start()
                passed.append(cp)
        for t in range(nt):
            for j, (px, py) in enumerate(chips):
                copy(t, 2 * px + py, 1 - c, (fsend, frecv), j, sib).wait_recv()
        for cp in first + passed:
            cp.wait_send()
        for cp in local:
            cp.wait()

    return pl.pallas_call(
        body, name="allgather_weights", in_specs=[ANY] * nt, out_specs=[ANY] * nt,
        out_shape=[jax.ShapeDtypeStruct((N_CHIPS,) + s.shape, s.dtype) for s in shards],
        scratch_shapes=[pltpu.SemaphoreType.DMA((nt, 3))] * 4 + [pltpu.SemaphoreType.DMA((nt,))],
    )(*shards)


def _chunk_rows(rows, row_bytes):
    cr = rows
    while cr * row_bytes > STREAM_CHUNK_BYTES and cr % 32 == 0:
        cr //= 2
    return cr


def _stream(pairs, buf, sems, t, peer):
    lsem, ssem, rsem = sems
    sends = []
    for k, (src, dst) in enumerate(pairs):
        slot = k % STREAM_SLOTS
        if k >= STREAM_SLOTS:
            sends[k - STREAM_SLOTS].wait_send()
        load = pltpu.make_async_copy(src, buf.at[slot], lsem.at[t, slot])
        load.start()
        load.wait()
        cp = pltpu.make_async_remote_copy(src_ref=buf.at[slot], dst_ref=dst, send_sem=ssem.at[t, slot],
                                          recv_sem=rsem.at[t], device_id=peer, device_id_type=MESH)
        cp.start()
        sends.append(cp)
    for cp in sends[-STREAM_SLOTS:]:
        cp.wait_send()


def _stream_scratch(shapes):
    nt = len(shapes)
    return ([pltpu.VMEM((STREAM_SLOTS,) + s, d) for s, d in shapes]
            + [pltpu.SemaphoreType.DMA((nt, STREAM_SLOTS)), pltpu.SemaphoreType.DMA((nt, STREAM_SLOTS)),
               pltpu.SemaphoreType.DMA((nt,))])


def _exchange_halves(grads):
    nt = len(grads)
    hs = [g.shape[1] // 2 for g in grads]
    crs = [_chunk_rows(h, g.shape[2] * g.dtype.itemsize) for h, g in zip(hs, grads)]

    def body(*refs):
        ins, gots, bufs, sems = refs[:nt], refs[nt:2 * nt], refs[2 * nt:3 * nt], refs[3 * nt:]
        x, y, c = _place()
        sib = (x, y, 1 - c)
        for t in range(nt):
            h, cr = hs[t], crs[t]
            pairs = [(ins[t].at[b, pl.ds((1 - c) * h + r0, cr)], gots[t].at[b, pl.ds(r0, cr)])
                     for b in range(N_CHIPS) for r0 in range(0, h, cr)]
            _stream(pairs, bufs[t], sems, t, sib)
        for t in range(nt):
            pltpu.make_async_remote_copy(src_ref=gots[t], dst_ref=gots[t], send_sem=sems[1].at[t, 0],
                                         recv_sem=sems[2].at[t], device_id=sib, device_id_type=MESH).wait_recv()

    return pl.pallas_call(
        body, name="exchange_halves", in_specs=[ANY] * nt, out_specs=[ANY] * nt,
        out_shape=[jax.ShapeDtypeStruct((N_CHIPS, h, g.shape[2]), g.dtype) for h, g in zip(hs, grads)],
        scratch_shapes=_stream_scratch([((cr, g.shape[2]), g.dtype) for cr, g in zip(crs, grads)]),
        compiler_params=pltpu.CompilerParams(vmem_limit_bytes=VMEM_LIMIT_BYTES),
    )(*grads)


def _scatter_partials(parts):
    nt = len(parts)

    def body(*refs):
        ins, outs = refs[:nt], refs[nt:2 * nt]
        send, recv, lsem = refs[2 * nt:]
        x, y, c = _place()
        me = 2 * x + y
        copies = []
        for t in range(nt):
            cp = pltpu.make_async_copy(ins[t].at[me], outs[t].at[me], lsem.at[t])
            cp.start()
            copies.append(cp)
            for j, (px, py) in enumerate(_other_chips(x, y)):
                cp = pltpu.make_async_remote_copy(
                    src_ref=ins[t].at[2 * px + py], dst_ref=outs[t].at[me], send_sem=send.at[t, j],
                    recv_sem=recv.at[t, j], device_id=(px, py, c), device_id_type=MESH)
                cp.start()
                copies.append(cp)
        for cp in copies:
            cp.wait()

    return pl.pallas_call(
        body, name="scatter_partials", in_specs=[ANY] * nt, out_specs=[ANY] * nt,
        out_shape=[jax.ShapeDtypeStruct(p.shape, p.dtype) for p in parts],
        scratch_shapes=[pltpu.SemaphoreType.DMA((nt, 3))] * 2 + [pltpu.SemaphoreType.DMA((nt,))],
    )(*parts)


def _share_halves(fulls):
    nt = len(fulls)
    crs = [_chunk_rows(f.shape[1], f.shape[2] * f.dtype.itemsize) for f in fulls]

    def body(*refs):
        outs, bufs, sems = refs[nt:2 * nt], refs[2 * nt:3 * nt], refs[3 * nt:]
        x, y, c = _place()
        sib = (x, y, 1 - c)
        for t in range(nt):
            rows = [outs[t].at[c, pl.ds(r0, crs[t])] for r0 in range(0, fulls[t].shape[1], crs[t])]
            _stream([(r, r) for r in rows], bufs[t], sems, t, sib)
        for t in range(nt):
            other = outs[t].at[1 - c]
            pltpu.make_async_remote_copy(src_ref=other, dst_ref=other, send_sem=sems[1].at[t, 0],
                                         recv_sem=sems[2].at[t], device_id=sib, device_id_type=MESH).wait_recv()

    return pl.pallas_call(
        body, name="share_halves", in_specs=[ANY] * nt, out_specs=[ANY] * nt,
        out_shape=[jax.ShapeDtypeStruct(f.shape, f.dtype) for f in fulls],
        scratch_shapes=_stream_scratch([((cr, f.shape[2]), f.dtype) for cr, f in zip(crs, fulls)]),
        input_output_aliases={t: t for t in range(nt)},
        compiler_params=pltpu.CompilerParams(vmem_limit_bytes=VMEM_LIMIT_BYTES),
    )(*fulls)


def _allgather_small(name, v):
    def body(v_ref, o_ref, send, recv, lsem):
        x, y, c = _place()
        me = 4 * x + 2 * y + c
        loc = pltpu.make_async_copy(v_ref, o_ref.at[me], lsem)
        loc.start()
        copies = []
        for k in range(1, N_DEV):
            px = 1 - x if k & 4 else x
            py = 1 - y if k & 2 else y
            pc = 1 - c if k & 1 else c
            cp = pltpu.make_async_remote_copy(
                src_ref=v_ref, dst_ref=o_ref.at[me], send_sem=send.at[k - 1], recv_sem=recv.at[k - 1],
                device_id=(px, py, pc), device_id_type=MESH)
            cp.start()
            copies.append(cp)
        for cp in copies:
            cp.wait()
        loc.wait()

    vm = pl.BlockSpec(memory_space=pltpu.VMEM)
    return pl.pallas_call(
        body, name=name, in_specs=[vm], out_specs=vm,
        out_shape=jax.ShapeDtypeStruct((N_DEV,) + v.shape, v.dtype),
        scratch_shapes=[pltpu.SemaphoreType.DMA((N_DEV - 1,))] * 2 + [pltpu.SemaphoreType.DMA],
    )(v)


def kernel(x, mem, norm_g, mem_norm_g, w_kv, w_out, pool_w_in, pool_w_grp, pool_scale, hgrn_w_in, hgrn_lb, hgrn_norm_g, final_g, loss_target, m_norm_g, m_mem_norm_g, m_w_kv, m_w_out, m_pool_w_in, m_pool_w_grp, m_pool_scale, m_hgrn_w_in, m_hgrn_lb, m_hgrn_norm_g, m_final_g, v_norm_g, v_mem_norm_g, v_w_kv, v_w_out, v_pool_w_in, v_pool_w_grp, v_pool_scale, v_hgrn_w_in, v_hgrn_lb, v_hgrn_norm_g, v_final_g):
    _, S, D = x.shape
    M = mem.shape[1]
    EB = 2 * D
    ECA = EB // 4
    EMIX = EB - ECA
    PG = EMIX // N_POOL_GROUPS
    NP0 = EMIX + ECA + EB
    NP1 = 3 * EMIX + ECA + EB
    SH0, SH1 = NP0 // N_CHIPS, NP1 // N_CHIPS
    DK, EK = D // N_CHIPS, EB // N_CHIPS
    TNP = 512 if all(v % 512 == 0 for v in (SH0, SH1, ECA, EMIX)) else 256
    TM = _tile(S, 1024)
    TMF = _tile(S, 2048)
    TD = _tile(D, 512)
    TDW = _tile(D, 1024)
    c0, c1 = SH0 // TNP, SH1 // TNP
    qt, et = EMIX // TNP, ECA // TNP
    chip = 2 * lax.axis_index("x") + lax.axis_index("y")

    xs, ms, tgt = x[0], mem[0], loss_target[0]

    flat = lambda w: w.reshape(-1, w.shape[-1])
    shard2d = [flat(w_kv), flat(w_out), flat(pool_w_in), flat(pool_w_grp), flat(hgrn_w_in)]
    g_kv, g_out, g_pin, g_grp, g_hin = _allgather_weights([s.astype(BF16) for s in shard2d])
    wkv = g_kv.reshape(N_CHIPS, 2, DK, 2 * ECA)
    wout = g_out.reshape(N_CHIPS, 2, EK, D)
    wpin = g_pin
    whin = g_hin
    wgrp = g_grp.reshape(N_CHIPS, N_POOL_GROUPS, PG // N_CHIPS, PG).transpose(1, 0, 2, 3).reshape(N_POOL_GROUPS, PG, PG)

    sds = jax.ShapeDtypeStruct
    tdk, tek, tew = _tile(DK, 512), _tile(EK, 512), _tile(EK, 1024)

    mem_n = _rms_fwd("rms_mem", ms, mem_norm_g.reshape(1, D))
    tkv = _tile(2 * ECA, 512)

    def kv_of(layer):
        return _matmul(
            f"kv{layer}", mem_n, wkv, grid=(1, 2 * ECA // tkv, D // tdk),
            a_spec=pl.BlockSpec((M, tdk), lambda i, j, k: (0, k)),
            b_spec=pl.BlockSpec((None, None, tdk, tkv), lambda i, j, k: (k // (DK // tdk), layer, k % (DK // tdk), j)),
            out_shape=sds((M, 2 * ECA), BF16), out_spec=pl.BlockSpec((M, tkv), lambda i, j, k: (0, j)),
            acc_shape=(M, tkv), dims=NN)

    kv = [kv_of(0), kv_of(1)]

    def out_proj(layer, branch, resid):
        return _matmul(
            f"out_proj{layer}", branch, wout, grid=(S // TM, D // TDW, EB // tew),
            a_spec=pl.BlockSpec((TM, tew), IK),
            b_spec=pl.BlockSpec((None, None, tew, TDW), lambda i, j, k: (k // (EK // tew), layer, k % (EK // tew), j)),
            out_shape=sds((S, D), F32), out_spec=pl.BlockSpec((TM, TDW), IJ),
            acc_shape=(TM, TDW), dims=NN, add=resid, add_spec=pl.BlockSpec((TM, TDW), IJ))

    ones_ca = jnp.ones((1, ECA), F32)

    h0 = _rms_fwd("rms0", xs, norm_g[0:1])
    proj0 = _matmul(
        "proj0", h0, wpin, grid=(S // TMF, NP0 // TNP, 1),
        a_spec=pl.BlockSpec((TMF, D), lambda i, j, k: (i, 0)),
        b_spec=pl.BlockSpec((None, D, TNP), lambda i, j, k: (j // c0, 0, j % c0)),
        out_shape=sds((S, NP0), BF16), out_spec=pl.BlockSpec((TMF, TNP), IJ),
        acc_shape=(TMF, TNP), dims=NN)
    pooled = _pool_fwd(proj0, S, EMIX)
    premix0 = _matmul(
        "pool_grp", pooled, wgrp, grid=(S // TM, N_POOL_GROUPS, 1),
        a_spec=pl.BlockSpec((TM, PG), lambda i, j, k: (i, j)),
        b_spec=pl.BlockSpec((None, PG, PG), lambda i, j, k: (j, 0, 0)),
        out_shape=sds((S, EB), BF16), out_spec=pl.BlockSpec((TM, PG), lambda i, j, k: (i, j)),
        acc_shape=(TM, PG), dims=NN)
    premix0 = _ca_fwd("ca_fwd0", proj0, EMIX // ECA, kv[0], premix0, S, ECA, EMIX)
    colscale0 = jnp.concatenate([pool_scale.reshape(1, EMIX), ones_ca], axis=1)
    gblk0 = (EMIX + ECA) // ECA
    branch0 = _gate_fwd("gate_fwd0", premix0, proj0, gblk0, colscale0, S, EB, ECA)
    x1 = out_proj(0, branch0, xs)

    h1 = _rms_fwd("rms1", x1, norm_g[1:2])

    def proj1_cols(name, ncols, col_of, out_cols, out_dtype, out_col_of):
        return _matmul(
            name, h1, whin, grid=(S // TMF, ncols, 1),
            a_spec=pl.BlockSpec((TMF, D), lambda i, j, k: (i, 0)),
            b_spec=pl.BlockSpec((None, D, TNP), lambda i, j, k: (col_of(j) // c1, 0, col_of(j) % c1)),
            out_shape=sds((S, out_cols), out_dtype), out_spec=pl.BlockSpec((TMF, TNP), lambda i, j, k: (i, out_col_of(j))),
            acc_shape=(TMF, TNP), dims=NN)

    skip_f = lambda j: jnp.where(j < qt, j, j + qt)
    proj1 = proj1_cols("proj1", NP1 // TNP - qt, skip_f, NP1, BF16, skip_f)
    fgate = proj1_cols("proj1_f", qt, lambda j: j + qt, EMIX, F32, lambda j: j)
    premix1, rstd1, states = _hgrn_fwd(proj1, fgate, hgrn_lb, S, EMIX, EB)
    premix1 = _ca_fwd("ca_fwd1", proj1, 3 * EMIX // ECA, kv[1], premix1, S, ECA, EMIX)
    norm_tiles = _allgather_small("allgather_norm_g", jnp.pad(hgrn_norm_g, ((0, SMALL_ROWS - 1), (0, 0))))
    hg_norm = norm_tiles[0::2, 0, :].reshape(1, EMIX)
    colscale1 = jnp.concatenate([hg_norm, ones_ca], axis=1)
    gblk1 = (3 * EMIX + ECA) // ECA
    branch1 = _gate_fwd("gate_fwd1", premix1, proj1, gblk1, colscale1, S, EB, ECA)
    x2 = out_proj(1, branch1, x1)

    dx2, dx2b, d_final_g, loss_part = _loss_head(x2, final_g.reshape(1, D), tgt)

    def out_proj_bwd(layer, dxb, branch, buf):
        dbranch = _matmul(
            f"dbranch{layer}", dxb, wout, grid=(S // TMF, EB // tek, 1),
            a_spec=pl.BlockSpec((TMF, D), lambda i, j, k: (i, 0)),
            b_spec=pl.BlockSpec((None, None, tek, D), lambda i, j, k: (j // (EK // tek), layer, j % (EK // tek), 0)),
            out_shape=sds((S, EB), BF16), out_spec=pl.BlockSpec((TMF, tek), IJ), acc_shape=(TMF, tek), dims=NT)
        dw = _matmul(
            f"dwout{layer}", branch, dxb, grid=(EB // tew, D // TD, 1),
            a_spec=pl.BlockSpec((S, tew), lambda i, j, k: (0, i)), b_spec=pl.BlockSpec((S, TD), lambda i, j, k: (0, j)),
            out_shape=sds((N_CHIPS, 2, EK, D), BF16),
            out_spec=pl.BlockSpec((None, None, tew, TD), lambda i, j, k: (i // (EK // tew), layer, i % (EK // tew), j)),
            acc_shape=(tew, TD), dims=TN, alias=buf)
        return dbranch, dw

    def kv_bwd(layer, dkv, buf, dmem_add):
        dkvb = dkv.astype(BF16)
        dmem = _matmul(
            f"dmem{layer}", dkvb, wkv, grid=(1, D // tdk, 2 * ECA // tkv),
            a_spec=pl.BlockSpec((M, tkv), lambda i, j, k: (0, k)),
            b_spec=pl.BlockSpec((None, None, tdk, tkv), lambda i, j, k: (j // (DK // tdk), layer, j % (DK // tdk), k)),
            out_shape=sds((M, D), F32), out_spec=pl.BlockSpec((M, tdk), lambda i, j, k: (0, j)), acc_shape=(M, tdk),
            dims=NT, add=dmem_add, add_spec=pl.BlockSpec((M, tdk), lambda i, j, k: (0, j)))
        dw = _matmul(
            f"dwkv{layer}", mem_n, dkvb, grid=(D // tdk, 2 * ECA // tkv, 1),
            a_spec=pl.BlockSpec((M, tdk), lambda i, j, k: (0, i)), b_spec=pl.BlockSpec((M, tkv), lambda i, j, k: (0, j)),
            out_shape=sds((N_CHIPS, 2, DK, 2 * ECA), BF16),
            out_spec=pl.BlockSpec((None, None, tdk, tkv), lambda i, j, k: (i // (DK // tdk), layer, i % (DK // tdk), j)),
            acc_shape=(tdk, tkv), dims=TN, alias=buf)
        return dmem, dw

    dbranch1, gw_out = out_proj_bwd(1, dx2b, branch1, None)
    dpremix1, drest1, dcol1 = _gate_bwd("gate_bwd1", dbranch1, premix1, proj1, gblk1, colscale1, (S, ECA + EB), 1,
                                        S, EB, ECA)
    drest1, dkv1 = _ca_bwd("ca_bwd1", dpremix1, proj1, 3 * EMIX // ECA, kv[1], drest1, 0, S, ECA, EMIX)
    dqfi, dlb = _hgrn_bwd(dpremix1, premix1, rstd1, states, proj1, fgate, hgrn_lb, S, EMIX)
    nq, nr = 3 * qt, (ECA + EB) // TNP
    dh1 = _matmul(
        "dh1_qfi", dqfi, whin, grid=(S // TM, D // TDW, nq),
        a_spec=pl.BlockSpec((None, TM, TNP), lambda i, j, k: (k // qt, i, k % qt)),
        b_spec=pl.BlockSpec((None, TDW, TNP), lambda i, j, k: (k // c1, j, k % c1)),
        out_shape=sds((S, D), F32), out_spec=pl.BlockSpec((TM, TDW), IJ), acc_shape=(TM, TDW), dims=NT)
    dh1 = _matmul(
        "dh1_rest", drest1, whin, grid=(S // TM, D // TDW, nr), a_spec=pl.BlockSpec((TM, TNP), IK),
        b_spec=pl.BlockSpec((None, TDW, TNP), lambda i, j, k: ((k + nq) // c1, j, (k + nq) % c1)),
        out_shape=sds((S, D), F32), out_spec=pl.BlockSpec((TM, TDW), IJ), acc_shape=(TM, TDW), dims=NT,
        add=dh1, add_spec=pl.BlockSpec((TM, TDW), IJ))
    gw_hin = _matmul(
        "dwhin_qfi", h1, dqfi, grid=(D // TDW, nq, 1), a_spec=pl.BlockSpec((S, TDW), lambda i, j, k: (0, i)),
        b_spec=pl.BlockSpec((None, S, TNP), lambda i, j, k: (j // qt, 0, j % qt)),
        out_shape=sds((N_CHIPS, D, SH1), BF16), out_spec=pl.BlockSpec((None, TDW, TNP), lambda i, j, k: (j // c1, i, j % c1)),
        acc_shape=(TDW, TNP), dims=TN)
    gw_hin = _matmul(
        "dwhin_rest", h1, drest1, grid=(D // TDW, nr, 1), a_spec=pl.BlockSpec((S, TDW), lambda i, j, k: (0, i)),
        b_spec=pl.BlockSpec((S, TNP), lambda i, j, k: (0, j)), out_shape=sds((N_CHIPS, D, SH1), BF16),
        out_spec=pl.BlockSpec((None, TDW, TNP), lambda i, j, k: ((j + nq) // c1, i, (j + nq) % c1)),
        acc_shape=(TDW, TNP), dims=TN, alias=gw_hin)
    dx1, dx1b, d_ng1 = _rms_bwd("rms_bwd1", dh1, x1, norm_g[1:2], dx2)
    dmem, gw_kv = kv_bwd(1, dkv1, None, None)

    dbranch0, gw_out = out_proj_bwd(0, dx1b, branch0, gw_out)
    dpremix0, dproj0, dcol0 = _gate_bwd("gate_bwd0", dbranch0, premix0, proj0, gblk0, colscale0, (S, NP0), gblk0,
                                        S, EB, ECA)
    dproj0, dkv0 = _ca_bwd("ca_bwd0", dpremix0, proj0, EMIX // ECA, kv[0], dproj0, EMIX // ECA, S, ECA, EMIX)
    dpooled = _matmul(
        "dpooled", dpremix0, wgrp, grid=(S // TM, N_POOL_GROUPS, 1), a_spec=pl.BlockSpec((TM, PG), IJ),
        b_spec=pl.BlockSpec((None, PG, PG), lambda i, j, k: (j, 0, 0)),
        out_shape=sds((S, EMIX), F32), out_spec=pl.BlockSpec((TM, PG), IJ), acc_shape=(TM, PG), dims=NT)
    dwgrp = _matmul(
        "dwgrp", pooled, dpremix0, grid=(N_POOL_GROUPS, 1, 1), a_spec=pl.BlockSpec((S, PG), lambda i, j, k: (0, i)),
        b_spec=pl.BlockSpec((S, PG), lambda i, j, k: (0, i)), out_shape=sds((N_POOL_GROUPS, PG, PG), F32),
        out_spec=pl.BlockSpec((None, PG, PG), lambda i, j, k: (i, 0, 0)), acc_shape=(PG, PG), dims=TN)
    dproj0 = _pool_bwd(dpooled, dproj0, S, EMIX)
    dh0 = _matmul(
        "dh0", dproj0, wpin, grid=(S // TM, D // TDW, N_CHIPS), a_spec=pl.BlockSpec((TM, SH0), IK),
        b_spec=pl.BlockSpec((None, TDW, SH0), lambda i, j, k: (k, j, 0)),
        out_shape=sds((S, D), F32), out_spec=pl.BlockSpec((TM, TDW), IJ), acc_shape=(TM, TDW), dims=NT)
    gw_pin = _matmul(
        "dwpin", h0, dproj0, grid=(D // TDW, NP0 // TNP, 1), a_spec=pl.BlockSpec((S, TDW), lambda i, j, k: (0, i)),
        b_spec=pl.BlockSpec((S, TNP), lambda i, j, k: (0, j)), out_shape=sds((N_CHIPS, D, SH0), BF16),
        out_spec=pl.BlockSpec((None, TDW, TNP), lambda i, j, k: (j // c0, i, j % c0)), acc_shape=(TDW, TNP), dims=TN)
    grad_x, _, d_ng0 = _rms_bwd("rms_bwd0", dh0, xs, norm_g[0:1], dx1)
    dmem, gw_kv = kv_bwd(0, dkv0, gw_kv, dmem)
    _, _, d_mng = _rms_bwd("rms_bwd_mem", dmem, ms, mem_norm_g.reshape(1, D), jnp.zeros_like(ms))

    gw_grp = dwgrp.reshape(N_POOL_GROUPS, N_CHIPS, PG // N_CHIPS, PG).transpose(1, 0, 2, 3).reshape(N_CHIPS, PG, PG)
    stacks = [gw_kv.reshape(N_CHIPS, 2 * DK, 2 * ECA), gw_out.reshape(N_CHIPS, 2 * EK, D), gw_pin,
              gw_grp.astype(BF16), gw_hin]
    core = lax.axis_index("c").astype(jnp.int32).reshape(1)
    got = _exchange_halves(stacks)
    parts = [_add_halves(f"add_halves{t}", core, g.reshape(N_CHIPS, 2, g.shape[1] // 2, g.shape[2]), r)
             for t, (g, r) in enumerate(zip(stacks, got))]
    landed = _scatter_partials(parts)
    fulls = _share_halves([_sum_slots(f"sum_slots{t}", core, p) for t, p in enumerate(landed)])
    big_g = [f.reshape(-1, f.shape[-1]) for f in fulls]
    big_names = ["w_kv", "w_out", "pool_w_in", "pool_w_grp", "hgrn_w_in"]
    big_m = [flat(a) for a in (m_w_kv, m_w_out, m_pool_w_in, m_pool_w_grp, m_hgrn_w_in)]
    big_v = [flat(a) for a in (v_w_kv, v_w_out, v_pool_w_in, v_pool_w_grp, v_hgrn_w_in)]
    big_shapes = [w_kv.shape, w_out.shape, pool_w_in.shape, pool_w_grp.shape, hgrn_w_in.shape]
    grads, deltas, new_m, new_v = {}, {}, {}, {}
    for t, n in enumerate(big_names):
        d, mn, vn = _adamw(f"adamw_{n}", shard2d[t], big_g[t], big_m[t], big_v[t])
        grads[n], deltas[n] = big_g[t].reshape(big_shapes[t]), d.reshape(big_shapes[t])
        new_m[n], new_v[n] = mn.reshape(big_shapes[t]), vn.reshape(big_shapes[t])

    Wd = EMIX
    partial = _pack_rows("pack_partials", [d_ng0, d_ng1, d_mng, dcol0[:, :EMIX], dlb, dcol1[:, :EMIX], d_final_g,
                                           loss_part], Wd)
    summed = _small_sum(_allgather_small("allgather_grads", partial), hgrn_lb, 4)
    row = lambda i, n=Wd: summed[i:i + 1, :n]
    nshard = EMIX // N_CHIPS
    g_hg_norm = lax.dynamic_slice_in_dim(row(5), chip * nshard, nshard, axis=1)
    small_names = ["norm_g0", "norm_g1", "mem_norm_g", "pool_scale", "hgrn_lb0", "hgrn_lb1", "hgrn_norm_g", "final_g"]
    small_w = [norm_g[0:1], norm_g[1:2], mem_norm_g.reshape(1, D), pool_scale, hgrn_lb[0:1], hgrn_lb[1:2], hgrn_norm_g,
               final_g.reshape(1, D)]
    small_m = [m_norm_g[0:1], m_norm_g[1:2], m_mem_norm_g.reshape(1, D), m_pool_scale, m_hgrn_lb[0:1], m_hgrn_lb[1:2],
               m_hgrn_norm_g, m_final_g.reshape(1, D)]
    small_v = [v_norm_g[0:1], v_norm_g[1:2], v_mem_norm_g.reshape(1, D), v_pool_scale, v_hgrn_lb[0:1], v_hgrn_lb[1:2],
               v_hgrn_norm_g, v_final_g.reshape(1, D)]
    g_pack = _pack_rows("pack_small_g", [row(0, D), row(1, D), row(2, D), row(3), row(8), row(9), g_hg_norm, row(6, D)], Wd)
    d_pack, m_pack, v_pack = _adamw("adamw_small", _pack_rows("pack_small_w", small_w, Wd), g_pack,
                                    _pack_rows("pack_small_m", small_m, Wd), _pack_rows("pack_small_v", small_v, Wd))
    widths = [v.shape[1] for v in small_w]
    rows = lambda p: {n: p[i, :widths[i]] for i, n in enumerate(small_names)}

    def assemble(r, out):
        out["norm_g"] = jnp.stack([r["norm_g0"], r["norm_g1"]])
        out["mem_norm_g"] = r["mem_norm_g"]
        out["pool_scale"] = r["pool_scale"].reshape(1, EMIX)
        out["hgrn_lb"] = jnp.stack([r["hgrn_lb0"], r["hgrn_lb1"]])
        out["hgrn_norm_g"] = r["hgrn_norm_g"].reshape(1, nshard)
        out["final_g"] = r["final_g"]

    assemble(rows(g_pack), grads)
    assemble(rows(d_pack), deltas)
    assemble(rows(m_pack), new_m)
    assemble(rows(v_pack), new_v)
    loss = summed[7, 0]

    order = ["norm_g", "mem_norm_g", "w_kv", "w_out", "pool_w_in", "pool_w_grp", "pool_scale", "hgrn_w_in", "hgrn_lb",
             "hgrn_norm_g", "final_g"]
    return (loss, grad_x.reshape(1, S, D), *[grads[n] for n in order], *[deltas[n] for n in order],
            *[new_m[n] for n in order], *[new_v[n] for n in order])
```

```python
import functools

import jax
import jax.numpy as jnp
from jax import lax
from jax.experimental import pallas as pl
from jax.experimental.pallas import tpu as pltpu

F32 = jnp.float32
BF16 = jnp.bfloat16
MESH = pl.DeviceIdType.MESH
ANY = pl.BlockSpec(memory_space=pl.ANY)

EPS = 1e-6
HG_HEAD_DIM = 128
HG_CHUNK = 64
CA_HEADS = 4
N_POOL_GROUPS = 4
POOL_HALO = 128
ADAM_LR = 0.001
ADAM_B1 = 0.9
ADAM_B2 = 0.999
ADAM_EPS = 1e-08
ADAM_WD = 0.01
ADAM_STEP = 10
N_CHIPS = 4
N_DEV = 8
VMEM_LIMIT_BYTES = 56 * 1024 * 1024
SMALL_ROWS = 8
STREAM_CHUNK_BYTES = 2 * 1024 * 1024
STREAM_SLOTS = 2


def _params(*sem):
    return pltpu.CompilerParams(dimension_semantics=sem, vmem_limit_bytes=VMEM_LIMIT_BYTES)


def _tile(n, pref):
    t = pref
    while n % t:
        t //= 2
    return t


def _sigmoid(x):
    return 1.0 / (1.0 + jnp.exp(-x))


def _matmul(name, a, b, *, grid, a_spec, b_spec, out_shape, out_spec, acc_shape, dims,
            add=None, add_spec=None, alias=None):
    nk = grid[2]
    has_add = add is not None
    has_alias = alias is not None

    def body(*refs):
        a_ref, b_ref = refs[0], refs[1]
        pos = 2
        add_ref = None
        if has_add:
            add_ref = refs[pos]
            pos += 1
        if has_alias:
            pos += 1
        o_ref = refs[pos]
        prod = lax.dot_general(a_ref[...], b_ref[...], (dims, ((), ())), preferred_element_type=F32)

        def finish(r):
            if has_add:
                r = r + add_ref[...].astype(F32)
            o_ref[...] = r.astype(o_ref.dtype)

        if nk == 1:
            finish(prod)
            return
        acc_ref = refs[pos + 1]
        k = pl.program_id(2)

        @pl.when(k == 0)
        def _():
            acc_ref[...] = prod

        @pl.when(k > 0)
        def _():
            acc_ref[...] += prod

        @pl.when(k == nk - 1)
        def _():
            finish(acc_ref[...])

    operands = [a, b]
    in_specs = [a_spec, b_spec]
    if has_add:
        operands.append(add)
        in_specs.append(add_spec)
    aliases = {}
    if has_alias:
        aliases = {len(operands): 0}
        operands.append(alias)
        in_specs.append(ANY)
    return pl.pallas_call(
        body, name=name, grid=grid, in_specs=in_specs, out_specs=out_spec, out_shape=out_shape,
        scratch_shapes=[pltpu.VMEM(acc_shape, F32)] if nk > 1 else [], input_output_aliases=aliases,
        compiler_params=_params("parallel", "parallel", "arbitrary"),
    )(*operands)


IJ = lambda i, j, k: (i, j)
IK = lambda i, j, k: (i, k)
KJ = lambda i, j, k: (k, j)
KI = lambda i, j, k: (k, i)
NN = ((1,), (0,))
NT = ((1,), (1,))
TN = ((0,), (0,))


def _rms_fwd(name, x, g):
    R, D = x.shape
    tr = _tile(R, 256)

    def body(x_ref, g_ref, o_ref):
        xf = x_ref[...]
        r = lax.rsqrt(jnp.mean(xf * xf, axis=-1, keepdims=True) + EPS)
        o_ref[...] = (xf * r * g_ref[...]).astype(o_ref.dtype)

    return pl.pallas_call(
        body, name=name, grid=(R // tr,),
        in_specs=[pl.BlockSpec((tr, D), lambda i: (i, 0)), pl.BlockSpec((1, D), lambda i: (0, 0))],
        out_specs=pl.BlockSpec((tr, D), lambda i: (i, 0)),
        out_shape=jax.ShapeDtypeStruct((R, D), BF16), compiler_params=_params("parallel"),
    )(x, g)


def _rms_bwd(name, dh, x, g, dres):
    R, D = x.shape
    tr = _tile(R, 256)

    def body(dh_ref, x_ref, g_ref, dres_ref, dx_ref, dxb_ref, dg_ref):
        xf = x_ref[...]
        r = lax.rsqrt(jnp.mean(xf * xf, axis=-1, keepdims=True) + EPS)
        xn = xf * r
        d = dh_ref[...]
        dyg = d * g_ref[...]
        dx = r * (dyg - xn * jnp.mean(dyg * xn, axis=-1, keepdims=True)) + dres_ref[...]
        dx_ref[...] = dx
        dxb_ref[...] = dx.astype(BF16)

        @pl.when(pl.program_id(0) == 0)
        def _():
            dg_ref[...] = jnp.zeros_like(dg_ref)

        dg_ref[...] += jnp.sum(d * xn, axis=0, keepdims=True)

    row = pl.BlockSpec((tr, D), lambda i: (i, 0))
    vec = pl.BlockSpec((1, D), lambda i: (0, 0))
    return pl.pallas_call(
        body, name=name, grid=(R // tr,), in_specs=[row, row, vec, row], out_specs=[row, row, vec],
        out_shape=[jax.ShapeDtypeStruct((R, D), F32), jax.ShapeDtypeStruct((R, D), BF16),
                   jax.ShapeDtypeStruct((1, D), F32)],
        compiler_params=_params("arbitrary"),
    )(dh, x, g, dres)


def _loss_head(x2, g, target):
    R, D = x2.shape
    tr = _tile(R, 256)

    def body(x_ref, g_ref, t_ref, dx_ref, dxb_ref, dg_ref, loss_ref):
        xf = x_ref[...]
        gg = g_ref[...]
        r = lax.rsqrt(jnp.mean(xf * xf, axis=-1, keepdims=True) + EPS)
        xn = xf * r
        e = xn * gg - t_ref[...]
        part = 0.5 * jnp.sum(jnp.mean(e * e, axis=-1, keepdims=True), axis=0, keepdims=True)
        dy = e * (1.0 / D)
        dyg = dy * gg
        dx = r * (dyg - xn * jnp.mean(dyg * xn, axis=-1, keepdims=True))
        dx_ref[...] = dx
        dxb_ref[...] = dx.astype(BF16)

        @pl.when(pl.program_id(0) == 0)
        def _():
            dg_ref[...] = jnp.zeros_like(dg_ref)
            loss_ref[...] = jnp.zeros_like(loss_ref)

        dg_ref[...] += jnp.sum(dy * xn, axis=0, keepdims=True)
        loss_ref[...] += jnp.broadcast_to(part, loss_ref.shape)

    row = pl.BlockSpec((tr, D), lambda i: (i, 0))
    vec = pl.BlockSpec((1, D), lambda i: (0, 0))
    return pl.pallas_call(
        body, name="loss_head", grid=(R // tr,), in_specs=[row, vec, row],
        out_specs=[row, row, vec, pl.BlockSpec((1, 128), lambda i: (0, 0))],
        out_shape=[jax.ShapeDtypeStruct((R, D), F32), jax.ShapeDtypeStruct((R, D), BF16),
                   jax.ShapeDtypeStruct((1, D), F32), jax.ShapeDtypeStruct((1, 128), F32)],
        compiler_params=_params("arbitrary"),
    )(x2, g, target)


def _pool_band(tr, reverse, w):
    r = lax.broadcasted_iota(jnp.int32, (tr, tr + POOL_HALO), 0)
    c = lax.broadcasted_iota(jnp.int32, (tr, tr + POOL_HALO), 1)
    if reverse:
        inside = (c >= r) & (c < r + w)
    else:
        cc = c - POOL_HALO
        inside = (cc <= r) & (cc > r - w)
    return jnp.where(inside, 1.0, 0.0).astype(BF16)


def _pool_fwd(proj, S, EMIX):
    PG = EMIX // N_POOL_GROUPS
    cb = PG
    tr = _tile(S, 512)
    per_group = PG // cb

    def body(u_ref, o_ref, ext):
        i = pl.program_id(1)
        w = jnp.left_shift(2, pl.program_id(0) // per_group)

        @pl.when(i == 0)
        def _():
            ext[0:POOL_HALO, :] = jnp.zeros((POOL_HALO, cb), BF16)

        u = u_ref[...]
        ext[POOL_HALO:, :] = u
        win = jnp.dot(_pool_band(tr, False, w), ext[...], preferred_element_type=F32)
        pos = i * tr + lax.broadcasted_iota(jnp.int32, (tr, 1), 0)
        cnt = jnp.minimum(pos + 1, w).astype(F32)
        o_ref[...] = (win / cnt - u.astype(F32)).astype(BF16)
        ext[0:POOL_HALO, :] = u[tr - POOL_HALO:, :]

    return pl.pallas_call(
        body, name="pool_fwd", grid=(EMIX // cb, S // tr),
        in_specs=[pl.BlockSpec((tr, cb), lambda j, i: (i, j))],
        out_specs=pl.BlockSpec((tr, cb), lambda j, i: (i, j)),
        out_shape=jax.ShapeDtypeStruct((S, EMIX), BF16),
        scratch_shapes=[pltpu.VMEM((tr + POOL_HALO, cb), BF16)],
        compiler_params=_params("parallel", "arbitrary"),
    )(proj)


def _pool_bwd(dpooled, dproj, S, EMIX):
    PG = EMIX // N_POOL_GROUPS
    cb = PG
    tr = _tile(S, 512)
    per_group = PG // cb
    nrt = S // tr

    def body(d_ref, _, o_ref, ext):
        step = pl.program_id(1)
        i = nrt - 1 - step
        w = jnp.left_shift(2, pl.program_id(0) // per_group)

        @pl.when(step == 0)
        def _():
            ext[tr:, :] = jnp.zeros((POOL_HALO, cb), BF16)

        d = d_ref[...]
        pos = i * tr + lax.broadcasted_iota(jnp.int32, (tr, 1), 0)
        cnt = jnp.minimum(pos + 1, w).astype(F32)
        z = (d / cnt).astype(BF16)
        ext[0:tr, :] = z
        win = jnp.dot(_pool_band(tr, True, w), ext[...], preferred_element_type=F32)
        o_ref[...] = (win - d).astype(BF16)
        ext[tr:, :] = z[0:POOL_HALO, :]

    return pl.pallas_call(
        body, name="pool_bwd", grid=(EMIX // cb, nrt),
        in_specs=[pl.BlockSpec((tr, cb), lambda j, s: (nrt - 1 - s, j)), ANY],
        out_specs=pl.BlockSpec((tr, cb), lambda j, s: (nrt - 1 - s, j)),
        out_shape=jax.ShapeDtypeStruct(dproj.shape, dproj.dtype),
        scratch_shapes=[pltpu.VMEM((tr + POOL_HALO, cb), BF16)],
        input_output_aliases={1: 0},
        compiler_params=_params("parallel", "arbitrary"),
    )(dpooled, dproj)


def _ca_fwd(name, proj, qblk, kv, premix, S, ECA, EMIX):
    M = kv.shape[0]
    hd = ECA // CA_HEADS
    ts = _tile(S, 512)
    scale = hd ** -0.5

    def body(q_ref, kv_ref, _, o_ref):
        for h in range(CA_HEADS):
            q = q_ref[:, h * hd:(h + 1) * hd]
            k = kv_ref[:, h * hd:(h + 1) * hd]
            v = kv_ref[:, ECA + h * hd:ECA + (h + 1) * hd]
            s = lax.dot_general(q, k, (NT, ((), ())), preferred_element_type=F32) * scale
            s = s - jnp.max(s, axis=-1, keepdims=True)
            p = jnp.exp(s)
            p = p / jnp.sum(p, axis=-1, keepdims=True)
            o = jnp.dot(p.astype(BF16), v, preferred_element_type=F32)
            o_ref[:, h * hd:(h + 1) * hd] = o.astype(BF16)

    return pl.pallas_call(
        body, name=name, grid=(S // ts,),
        in_specs=[pl.BlockSpec((ts, ECA), lambda i: (i, qblk)), pl.BlockSpec((M, 2 * ECA), lambda i: (0, 0)), ANY],
        out_specs=pl.BlockSpec((ts, ECA), lambda i: (i, EMIX // ECA)),
        out_shape=jax.ShapeDtypeStruct(premix.shape, premix.dtype),
        input_output_aliases={2: 0}, compiler_params=_params("parallel"),
    )(proj, kv, premix)


def _ca_bwd(name, dpremix, proj, qblk, kv, dbuf, dblk, S, ECA, EMIX):
    M = kv.shape[0]
    hd = ECA // CA_HEADS
    ts = _tile(S, 512)
    scale = hd ** -0.5

    def body(do_ref, q_ref, kv_ref, _, dq_ref, dkv_ref):
        @pl.when(pl.program_id(0) == 0)
        def _():
            dkv_ref[...] = jnp.zeros_like(dkv_ref)

        for h in range(CA_HEADS):
            lo, hi = h * hd, (h + 1) * hd
            q = q_ref[:, lo:hi]
            k = kv_ref[:, lo:hi]
            v = kv_ref[:, ECA + lo:ECA + hi]
            do = do_ref[:, lo:hi]
            s = lax.dot_general(q, k, (NT, ((), ())), preferred_element_type=F32) * scale
            s = s - jnp.max(s, axis=-1, keepdims=True)
            p = jnp.exp(s)
            p = p / jnp.sum(p, axis=-1, keepdims=True)
            pb = p.astype(BF16)
            dkv_ref[:, ECA + lo:ECA + hi] += lax.dot_general(pb, do, (TN, ((), ())), preferred_element_type=F32)
            dp = lax.dot_general(do, v, (NT, ((), ())), preferred_element_type=F32)
            ds = (p * (dp - jnp.sum(p * dp, axis=-1, keepdims=True)) * scale).astype(BF16)
            dq_ref[:, lo:hi] = jnp.dot(ds, k, preferred_element_type=F32).astype(BF16)
            dkv_ref[:, lo:hi] += lax.dot_general(ds, q, (TN, ((), ())), preferred_element_type=F32)

    return pl.pallas_call(
        body, name=name, grid=(S // ts,),
        in_specs=[pl.BlockSpec((ts, ECA), lambda i: (i, EMIX // ECA)), pl.BlockSpec((ts, ECA), lambda i: (i, qblk)),
                  pl.BlockSpec((M, 2 * ECA), lambda i: (0, 0)), ANY],
        out_specs=[pl.BlockSpec((ts, ECA), lambda i: (i, dblk)), pl.BlockSpec((M, 2 * ECA), lambda i: (0, 0))],
        out_shape=[jax.ShapeDtypeStruct(dbuf.shape, dbuf.dtype), jax.ShapeDtypeStruct((M, 2 * ECA), F32)],
        input_output_aliases={3: 0}, compiler_params=_params("arbitrary"),
    )(dpremix, proj, kv, dbuf)


def _gate_fwd(name, premix, proj, gblk, colscale, S, EB, ECA):
    ts = _tile(S, 512)

    def body(p_ref, g_ref, c_ref, o_ref):
        g = g_ref[...].astype(F32)
        o_ref[...] = (p_ref[...].astype(F32) * c_ref[...] * (g * _sigmoid(g))).astype(BF16)

    return pl.pallas_call(
        body, name=name, grid=(S // ts, EB // ECA),
        in_specs=[pl.BlockSpec((ts, ECA), lambda i, j: (i, j)), pl.BlockSpec((ts, ECA), lambda i, j: (i, gblk + j)),
                  pl.BlockSpec((1, ECA), lambda i, j: (0, j))],
        out_specs=pl.BlockSpec((ts, ECA), lambda i, j: (i, j)),
        out_shape=jax.ShapeDtypeStruct((S, EB), BF16), compiler_params=_params("parallel", "parallel"),
    )(premix, proj, colscale)


def _gate_bwd(name, dbranch, premix, proj, gblk, colscale, dshape, dblk, S, EB, ECA):
    ts = _tile(S, 512)

    def body(db_ref, p_ref, g_ref, c_ref, dp_ref, dg_ref, dc_ref):
        g = g_ref[...].astype(F32)
        sg = _sigmoid(g)
        si = g * sg
        c = c_ref[...]
        db = db_ref[...].astype(F32)
        t = db * p_ref[...].astype(F32)
        dp_ref[...] = (db * si * c).astype(BF16)
        dg_ref[...] = (t * c * (sg * (1.0 + g * (1.0 - sg)))).astype(BF16)

        @pl.when(pl.program_id(1) == 0)
        def _():
            dc_ref[...] = jnp.zeros_like(dc_ref)

        dc_ref[...] += jnp.sum(t * si, axis=0, keepdims=True)

    blk = pl.BlockSpec((ts, ECA), lambda j, i: (i, j))
    vec = pl.BlockSpec((1, ECA), lambda j, i: (0, j))
    return pl.pallas_call(
        body, name=name, grid=(EB // ECA, S // ts),
        in_specs=[blk, blk, pl.BlockSpec((ts, ECA), lambda j, i: (i, gblk + j)), vec],
        out_specs=[blk, pl.BlockSpec((ts, ECA), lambda j, i: (i, dblk + j)), vec],
        out_shape=[jax.ShapeDtypeStruct((S, EB), BF16), jax.ShapeDtypeStruct(dshape, BF16),
                   jax.ShapeDtypeStruct((1, EB), F32)],
        compiler_params=_params("parallel", "arbitrary"),
    )(dbranch, premix, proj, colscale)


def _hgrn_lb(lb_ref):
    l0 = lb_ref[0:1, :]
    l1 = lb_ref[1:2, :]
    mx = jnp.maximum(l0, l1)
    e0 = jnp.exp(l0 - mx)
    e1 = jnp.exp(l1 - mx)
    return e1 / (e0 + e1)


def _bdot(a, b, ca, cb, precision=None):
    return lax.dot_general(a, b, (((ca,), (cb,)), ((0,), (0,))), preferred_element_type=F32, precision=precision)


def _hgrn_chunks(qin, fin, lbh, n):
    C = HG_CHUNK
    row = lax.broadcasted_iota(jnp.int32, (n, C, C), 1)
    col = lax.broadcasted_iota(jnp.int32, (n, C, C), 2)
    causal = row >= col
    sg = _sigmoid(fin)
    f = lbh + (1.0 - lbh) * sg
    k = 1.0 - f
    g = jnp.log(f)
    b = _bdot(jnp.where(causal, 1.0, 0.0), g, 2, 1, lax.Precision.HIGHEST)
    b_last = jnp.sum(g, axis=1, keepdims=True)
    eb = jnp.exp(b)
    einv = jnp.exp(-b)
    eend = jnp.exp(b_last - b)
    sq = _sigmoid(qin)
    a = qin * sq * (HG_HEAD_DIM ** -0.5) * eb
    bm = k * einv
    e = k * eend
    d = jnp.exp(b_last)
    p = jnp.where(causal, _bdot(a.astype(BF16), bm.astype(BF16), 2, 2), 0.0)
    return dict(causal=causal, sg=sg, f=f, eb=eb, einv=einv, eend=eend, sq=sq, a=a, bm=bm, e=e, d=d, p=p)


def _hgrn_fwd(proj, fgate, hgrn_lb, S, EMIX, EB):
    HD, C = HG_HEAD_DIM, HG_CHUNK
    HH = EMIX // HD
    hb = 2 if HH % 2 == 0 else 1
    W = hb * HD
    tr = _tile(S, 512)
    n = tr // C

    def body(q_ref, f_ref, i_ref, lb_ref, o_ref, rstd_ref, st_ref, state):
        @pl.when(pl.program_id(1) == 0)
        def _():
            state[...] = jnp.zeros_like(state)

        lb = _hgrn_lb(lb_ref)
        for h in range(hb):
            cs = slice(h * HD, (h + 1) * HD)
            qin = q_ref[:, cs].astype(F32).reshape(n, C, HD)
            fin = f_ref[:, cs].reshape(n, C, HD)
            v = i_ref[:, cs].reshape(n, C, HD)
            t = _hgrn_chunks(qin, fin, lb[:, cs], n)
            upd = _bdot(v, t["e"].astype(BF16), 1, 1)
            st = state[h]
            for c in range(n):
                st_ref[h, c] = st
                st = st * t["d"][c] + upd[c]
            state[h] = st
            o = _bdot(t["p"].astype(BF16), v, 2, 1) + _bdot(t["a"].astype(BF16), st_ref[h].astype(BF16), 2, 2)
            rstd = lax.rsqrt(jnp.mean(o * o, axis=-1, keepdims=True) + EPS)
            o_ref[:, cs] = (o * rstd).reshape(tr, HD).astype(BF16)
            rstd_ref[:, cs] = jnp.broadcast_to(rstd, (n, C, HD)).reshape(tr, HD)

    blk = lambda off: pl.BlockSpec((tr, W), lambda g, i: (i, off + g))
    return pl.pallas_call(
        body, name="hgrn_fwd", grid=(HH // hb, S // tr),
        in_specs=[blk(0), blk(0), blk(2 * EMIX // W), pl.BlockSpec((2, W), lambda g, i: (0, g))],
        out_specs=[blk(0), blk(0), pl.BlockSpec((hb, n, HD, HD), lambda g, i: (g, i, 0, 0))],
        out_shape=[jax.ShapeDtypeStruct((S, EB), BF16), jax.ShapeDtypeStruct((S, EMIX), F32),
                   jax.ShapeDtypeStruct((HH, S // C, HD, HD), F32)],
        scratch_shapes=[pltpu.VMEM((hb, HD, HD), F32)],
        compiler_params=_params("parallel", "arbitrary"),
    )(proj, fgate, proj, hgrn_lb)


def _hgrn_bwd(dpremix, premix, rstd, states, proj, fgate, hgrn_lb, S, EMIX):
    HD, C = HG_HEAD_DIM, HG_CHUNK
    HH = EMIX // HD
    hb = 2 if HH % 2 == 0 else 1
    W = hb * HD
    tr = _tile(S, 512)
    n = tr // C
    nrt = S // tr

    def body(do_ref, on_ref, rstd_ref, st_ref, q_ref, f_ref, i_ref, lb_ref, d_ref, dlb_ref, dstate, dsbuf):
        @pl.when(pl.program_id(1) == 0)
        def _():
            dstate[...] = jnp.zeros_like(dstate)
            dlb_ref[...] = jnp.zeros_like(dlb_ref)

        lb = _hgrn_lb(lb_ref)
        for h in range(hb):
            cs = slice(h * HD, (h + 1) * HD)
            qin = q_ref[:, cs].astype(F32).reshape(n, C, HD)
            fin = f_ref[:, cs].reshape(n, C, HD)
            v = i_ref[:, cs].reshape(n, C, HD)
            lbh = lb[:, cs]
            t = _hgrn_chunks(qin, fin, lbh, n)
            a, bm, e, d, p = t["a"], t["bm"], t["e"], t["d"], t["p"]
            ab, bmb, eb16 = a.astype(BF16), bm.astype(BF16), e.astype(BF16)
            on = on_ref[:, cs].astype(F32).reshape(n, C, HD)
            dn = do_ref[:, cs].astype(F32).reshape(n, C, HD)
            do = rstd_ref[:, cs].reshape(n, C, HD) * (dn - on * jnp.mean(dn * on, axis=-1, keepdims=True))
            dob = do.astype(BF16)
            grow = _bdot(dob, ab, 1, 1)
            ds = dstate[h]
            for c in reversed(range(n)):
                dsbuf[h, c] = ds
                ds = ds * d[c] + grow[c]
            dstate[h] = ds
            dst = dsbuf[h]
            st = st_ref[h]
            dstb = dst.astype(BF16)
            dp = jnp.where(t["causal"], _bdot(dob, v, 2, 2), 0.0).astype(BF16)
            dv = _bdot(p.astype(BF16), dob, 1, 1) + _bdot(eb16, dstb, 2, 2)
            da = _bdot(dp, bmb, 2, 1) + _bdot(dob, st.astype(BF16), 2, 1)
            dbm = _bdot(dp, ab, 1, 1)
            de = _bdot(v, dstb, 2, 1)
            dd = jnp.sum(dst * st, axis=1, keepdims=True)
            dk = dbm * t["einv"] + de * t["eend"]
            dee = de * e
            db = da * a - dbm * bm - dee
            extra = jnp.sum(dee, axis=1, keepdims=True) + dd * d
            upper = jnp.where(lax.broadcasted_iota(jnp.int32, (n, C, C), 2)
                              >= lax.broadcasted_iota(jnp.int32, (n, C, C), 1), 1.0, 0.0)
            dg = _bdot(upper, db, 2, 1, lax.Precision.HIGHEST) + extra
            df = dg / t["f"] - dk
            sg, sq = t["sg"], t["sq"]
            dq = da * t["eb"] * (HD ** -0.5) * (sq * (1.0 + qin * (1.0 - sq)))
            d_ref[0, :, cs] = dq.reshape(tr, HD).astype(BF16)
            d_ref[1, :, cs] = (df * (1.0 - lbh) * sg * (1.0 - sg)).reshape(tr, HD).astype(BF16)
            d_ref[2, :, cs] = dv.reshape(tr, HD).astype(BF16)
            dlb_ref[:, cs] += jnp.sum((df * (1.0 - sg)).reshape(tr, HD), axis=0, keepdims=True)

    rev = lambda off: pl.BlockSpec((tr, W), lambda g, s: (nrt - 1 - s, off + g))
    return pl.pallas_call(
        body, name="hgrn_bwd", grid=(HH // hb, nrt),
        in_specs=[rev(0), rev(0), rev(0), pl.BlockSpec((hb, n, HD, HD), lambda g, s: (g, nrt - 1 - s, 0, 0)),
                  rev(0), rev(0), rev(2 * EMIX // W), pl.BlockSpec((2, W), lambda g, s: (0, g))],
        out_specs=[pl.BlockSpec((3, tr, W), lambda g, s: (0, nrt - 1 - s, g)), pl.BlockSpec((1, W), lambda g, s: (0, g))],
        out_shape=[jax.ShapeDtypeStruct((3, S, EMIX), BF16), jax.ShapeDtypeStruct((1, EMIX), F32)],
        scratch_shapes=[pltpu.VMEM((hb, HD, HD), F32), pltpu.VMEM((hb, n, HD, HD), F32)],
        compiler_params=_params("parallel", "arbitrary"),
    )(dpremix, premix, rstd, states, proj, fgate, proj, hgrn_lb)


def _ew_tiles(R, C):
    return _tile(R, 256), _tile(C, 1024)


def _add_halves(name, core_chip, grad, got):
    _, _, R, C = grad.shape
    tr, tc = _ew_tiles(R, C)

    def body(c_ref, a_ref, b_ref, o_ref, own_ref):
        r = (a_ref[...].astype(F32) + b_ref[...].astype(F32)).astype(BF16)
        o_ref[...] = r

        @pl.when(pl.program_id(2) == c_ref[1])
        def _():
            own_ref[...] = r

    blk = pl.BlockSpec((None, tr, tc), lambda i, j, s, c: (s, i, j))
    sds = jax.ShapeDtypeStruct(got.shape, BF16)
    return pl.pallas_call(
        body, name=name, out_shape=[sds, sds],
        grid_spec=pltpu.PrefetchScalarGridSpec(
            num_scalar_prefetch=1, grid=(R // tr, C // tc, N_CHIPS),
            in_specs=[pl.BlockSpec((None, None, tr, tc), lambda i, j, s, c: (s, c[0], i, j)), blk],
            out_specs=[blk, pl.BlockSpec((None, tr, tc), lambda i, j, s, c: (c[1], i, j))]),
        compiler_params=_params("parallel", "parallel", "arbitrary"),
    )(core_chip, grad, got)


def _sum_slots(name, core, parts):
    _, R, C = parts.shape
    tr, tc = _ew_tiles(R, C)

    def body(c_ref, p_ref, o_ref):
        acc = p_ref[0].astype(F32)
        for s in range(1, N_CHIPS):
            acc = acc + p_ref[s].astype(F32)
        o_ref[...] = acc

    return pl.pallas_call(
        body, name=name, out_shape=jax.ShapeDtypeStruct((2, R, C), F32),
        grid_spec=pltpu.PrefetchScalarGridSpec(
            num_scalar_prefetch=1, grid=(R // tr, C // tc),
            in_specs=[pl.BlockSpec((N_CHIPS, tr, tc), lambda i, j, c: (0, i, j))],
            out_specs=pl.BlockSpec((None, tr, tc), lambda i, j, c: (c[0], i, j))),
        compiler_params=_params("parallel", "parallel"),
    )(core, parts)


def _adamw(name, w, g, m, v):
    R, C = w.shape
    tr, tc = _ew_tiles(R, C)

    def body(w_ref, g_ref, m_ref, v_ref, d_ref, mo_ref, vo_ref):
        g = g_ref[...]
        mn = ADAM_B1 * m_ref[...] + (1.0 - ADAM_B1) * g
        vn = ADAM_B2 * v_ref[...] + (1.0 - ADAM_B2) * (g * g)
        m_hat = mn / (1.0 - ADAM_B1 ** ADAM_STEP)
        v_hat = vn / (1.0 - ADAM_B2 ** ADAM_STEP)
        d_ref[...] = -ADAM_LR * (m_hat / (jnp.sqrt(v_hat) + ADAM_EPS) + ADAM_WD * w_ref[...])
        mo_ref[...] = mn
        vo_ref[...] = vn

    blk = pl.BlockSpec((tr, tc), lambda i, j: (i, j))
    sds = jax.ShapeDtypeStruct((R, C), F32)
    return pl.pallas_call(
        body, name=name, grid=(R // tr, C // tc), in_specs=[blk] * 4, out_specs=[blk] * 3, out_shape=[sds] * 3,
        compiler_params=_params("parallel", "parallel"),
    )(w, g, m, v)


def _pack_rows(name, vecs, W):
    nv = len(vecs)

    def body(*refs):
        o_ref = refs[nv]
        o_ref[...] = jnp.zeros_like(o_ref)
        for i in range(nv):
            o_ref[i:i + 1, 0:vecs[i].shape[1]] = refs[i][...]

    vm = pl.BlockSpec(memory_space=pltpu.VMEM)
    return pl.pallas_call(
        body, name=name, in_specs=[vm] * nv, out_specs=vm, out_shape=jax.ShapeDtypeStruct((SMALL_ROWS, W), F32),
    )(*vecs)


def _small_sum(gathered, hgrn_lb, lb_row):
    _, T, W = gathered.shape

    def body(g_ref, lb_ref, o_ref):
        acc = g_ref[0]
        for dev in range(1, N_DEV):
            acc = acc + g_ref[dev]
        o_ref[0:T, :] = acc
        lb = _hgrn_lb(lb_ref)
        d1 = o_ref[lb_row:lb_row + 1, :] * (lb * (1.0 - lb))
        o_ref[T:2 * T, :] = jnp.zeros((T, W), F32)
        o_ref[T:T + 1, :] = -d1
        o_ref[T + 1:T + 2, :] = d1

    vm = pl.BlockSpec(memory_space=pltpu.VMEM)
    return pl.pallas_call(
        body, name="small_sum", in_specs=[vm, vm], out_specs=vm, out_shape=jax.ShapeDtypeStruct((2 * T, W), F32),
    )(gathered, hgrn_lb)


def _place():
    return lax.axis_index("x"), lax.axis_index("y"), lax.axis_index("c")


def _other_chips(x, y):
    return [(1 - x, y), (x, 1 - y), (1 - x, 1 - y)]


def _allgather_weights(shards):
    nt = len(shards)

    def body(*refs):
        ins, outs = refs[:nt], refs[nt:2 * nt]
        send, recv, fsend, frecv, lsem = refs[2 * nt:]
        x, y, c = _place()
        me = 2 * x + y
        sib = (x, y, 1 - c)
        chips = _other_chips(x, y)

        def half(t, slot, hc):
            h = shards[t].shape[0] // 2
            return outs[t].at[slot, pl.ds(hc * h, h)]

        def copy(t, slot, hc, sems, j, to, src=None):
            return pltpu.make_async_remote_copy(
                src_ref=half(t, slot, hc) if src is None else src, dst_ref=half(t, slot, hc),
                send_sem=sems[0].at[t, j], recv_sem=sems[1].at[t, j], device_id=to, device_id_type=MESH)

        local, first, passed = [], [], []
        for t in range(nt):
            h = shards[t].shape[0] // 2
            cp = pltpu.make_async_copy(ins[t], outs[t].at[me], lsem.at[t])
            cp.start()
            local.append(cp)
            for j, (px, py) in enumerate(chips):
                cp = copy(t, me, c, (send, recv), j, (px, py, c), src=ins[t].at[pl.ds(c * h, h)])
                cp.start()
                first.append(cp)
        for t in range(nt):
            for j, (px, py) in enumerate(chips):
                slot = 2 * px + py
                copy(t, slot, c, (send, recv), j, (px, py, c)).wait_recv()
                cp = copy(t, slot, c, (fsend, frecv), j, sib)
                cp.start()
                passed.append(cp)
        for t in range(nt):
            for j, (px, py) in enumerate(chips):
                copy(t, 2 * px + py, 1 - c, (fsend, frecv), j, sib).wait_recv()
        for cp in first + passed:
            cp.wait_send()
        for cp in local:
            cp.wait()

    return pl.pallas_call(
        body, name="allgather_weights", in_specs=[ANY] * nt, out_specs=[ANY] * nt,
        out_shape=[jax.ShapeDtypeStruct((N_CHIPS,) + s.shape, s.dtype) for s in shards],
        scratch_shapes=[pltpu.SemaphoreType.DMA((nt, 3))] * 4 + [pltpu.SemaphoreType.DMA((nt,))],
    )(*shards)


def _chunk_rows(rows, row_bytes):
    cr = rows
    while cr * row_bytes > STREAM_CHUNK_BYTES and cr % 32 == 0:
        cr //= 2
    return cr


def _stream(pairs, buf, sems, t, peer):
    lsem, ssem, rsem = sems
    sends = []
    for k, (src, dst) in enumerate(pairs):
        slot = k % STREAM_SLOTS
        if k >= STREAM_SLOTS:
            sends[k - STREAM_SLOTS]()
        load = pltpu.make_async_copy(src, buf.at[slot], lsem.at[t, slot])
        load.start()
        load.wait()
        if peer is None:
            cp = pltpu.make_async_copy(buf.at[slot], dst, ssem.at[t, slot])
            cp.start()
            sends.append(cp.wait)
        else:
            cp = pltpu.make_async_remote_copy(src_ref=buf.at[slot], dst_ref=dst, send_sem=ssem.at[t, slot],
                                              recv_sem=rsem.at[t], device_id=peer, device_id_type=MESH)
            cp.start()
            sends.append(cp.wait_send)
    for done in sends[-STREAM_SLOTS:]:
        done()


def _stream_scratch(shapes):
    nt = len(shapes)
    return ([pltpu.VMEM((STREAM_SLOTS,) + s, d) for s, d in shapes]
            + [pltpu.SemaphoreType.DMA((nt, STREAM_SLOTS)), pltpu.SemaphoreType.DMA((nt, STREAM_SLOTS)),
               pltpu.SemaphoreType.DMA((nt,))])


def _exchange_halves(name, grads):
    nt = len(grads)
    hs = [g.shape[1] // 2 for g in grads]
    crs = [_chunk_rows(h, g.shape[2] * g.dtype.itemsize) for h, g in zip(hs, grads)]

    def body(*refs):
        ins, gots, bufs, sems = refs[:nt], refs[nt:2 * nt], refs[2 * nt:3 * nt], refs[3 * nt:]
        x, y, c = _place()
        sib = (x, y, 1 - c)
        for t in range(nt):
            h, cr = hs[t], crs[t]
            pairs = [(ins[t].at[b, pl.ds((1 - c) * h + r0, cr)], gots[t].at[b, pl.ds(r0, cr)])
                     for b in range(N_CHIPS) for r0 in range(0, h, cr)]
            _stream(pairs, bufs[t], sems, t, sib)
        for t in range(nt):
            pltpu.make_async_remote_copy(src_ref=gots[t], dst_ref=gots[t], send_sem=sems[1].at[t, 0],
                                         recv_sem=sems[2].at[t], device_id=sib, device_id_type=MESH).wait_recv()

    return pl.pallas_call(
        body, name=name, in_specs=[ANY] * nt, out_specs=[ANY] * nt,
        out_shape=[jax.ShapeDtypeStruct((N_CHIPS, h, g.shape[2]), g.dtype) for h, g in zip(hs, grads)],
        scratch_shapes=_stream_scratch([((cr, g.shape[2]), g.dtype) for cr, g in zip(crs, grads)]),
        compiler_params=pltpu.CompilerParams(vmem_limit_bytes=VMEM_LIMIT_BYTES),
    )(*grads)


def _scatter_plan(srcs, dsts):
    x, y, c = _place()
    me = 2 * x + y
    return [(srcs[t].at[2 * px + py], dsts[t].at[me], (px, py, c))
            for t in range(len(srcs)) for px, py in _other_chips(x, y)]


def _gather_plan(srcs, dsts):
    x, y, c = _place()
    me = 2 * x + y
    plan = []
    for t in range(len(srcs)):
        h = srcs[t].shape[0] // 2
        plan += [(srcs[t].at[pl.ds(c * h, h)], dsts[t].at[me, pl.ds(c * h, h)], (px, py, c))
                 for px, py in _other_chips(x, y)]
    return plan


def _scatter_partials(parts, landed):
    nt = len(parts)

    def body(*refs):
        ins, outs = refs[:nt], refs[2 * nt:3 * nt]
        send, recv = refs[3 * nt:]
        copies = []
        for i, (src, dst, dev) in enumerate(_scatter_plan(ins, outs)):
            cp = pltpu.make_async_remote_copy(src_ref=src, dst_ref=dst, send_sem=send.at[i], recv_sem=recv.at[i],
                                              device_id=dev, device_id_type=MESH)
            cp.start()
            copies.append(cp)
        for cp in copies:
            cp.wait()

    return pl.pallas_call(
        body, name="scatter_partials", in_specs=[ANY] * (2 * nt), out_specs=[ANY] * nt,
        out_shape=[jax.ShapeDtypeStruct(p.shape, p.dtype) for p in landed],
        scratch_shapes=[pltpu.SemaphoreType.DMA((3 * nt,))] * 2, input_output_aliases={nt + t: t for t in range(nt)},
    )(*parts, *landed)


HBM_SPEC = pl.BlockSpec(memory_space=pltpu.HBM)
SEM_SPEC = pl.BlockSpec(memory_space=pltpu.SEMAPHORE)


def _split_start(name, srcs, dsts, plan, ncopies, after):
    bufs = [pltpu.with_memory_space_constraint(a, pltpu.HBM) for a in list(srcs) + list(dsts)]
    nb, ns = len(bufs), len(srcs)
    operands = bufs + ([after] if after is not None else [])

    def body(*refs):
        outs = refs[len(operands):]
        send, recv, token = outs[0], outs[1], outs[-1]
        for i, (src, dst, dev) in enumerate(plan(refs[:ns], refs[ns:nb])):
            pltpu.make_async_remote_copy(src_ref=src, dst_ref=dst, send_sem=send.at[i], recv_sem=recv.at[i],
                                         device_id=dev, device_id_type=MESH).start()
        token[...] = jnp.zeros_like(token)

    res = pl.pallas_call(
        body, name=name,
        out_shape=[pltpu.SemaphoreType.DMA((ncopies,)), pltpu.SemaphoreType.DMA((ncopies,))]
        + [pltpu.HBM(a.shape, a.dtype) for a in bufs] + [jax.ShapeDtypeStruct((8, 128), F32)],
        in_specs=[HBM_SPEC] * nb + [ANY] * (len(operands) - nb),
        out_specs=[SEM_SPEC, SEM_SPEC] + [HBM_SPEC] * nb + [pl.BlockSpec(memory_space=pltpu.VMEM)],
        input_output_aliases={i: 2 + i for i in range(nb)},
        compiler_params=pltpu.CompilerParams(has_side_effects=pltpu.SideEffectType.DATAFLOW_SIDE_EFFECTING),
    )(*operands)
    return res[:-1], res[-1]


def _split_wait(name, started, plan, ns, after):
    send, recv, bufs = started[0], started[1], list(started[2:])
    nb = len(bufs)

    def body(*refs):
        send_ref, recv_ref = refs[nb], refs[nb + 1]
        for i, (src, dst, dev) in enumerate(plan(refs[:ns], refs[ns:nb])):
            cp = pltpu.make_async_remote_copy(src_ref=src, dst_ref=dst, send_sem=send_ref.at[i], recv_sem=recv_ref.at[i],
                                              device_id=dev, device_id_type=MESH)
            cp.wait_send()
            cp.wait_recv()

    res = pl.pallas_call(
        body, name=name, out_shape=[pltpu.HBM(a.shape, a.dtype) for a in bufs],
        in_specs=[HBM_SPEC] * nb + [SEM_SPEC, SEM_SPEC, ANY], out_specs=[HBM_SPEC] * nb,
        input_output_aliases={i: i for i in range(nb)},
        compiler_params=pltpu.CompilerParams(has_side_effects=pltpu.SideEffectType.DATAFLOW_SIDE_EFFECTING),
    )(*bufs, send, recv, after)
    return res[:ns], res[ns:]


def _gather_finish(shards, stacks):
    nt = len(shards)
    hs = [s.shape[0] // 2 for s in shards]
    crs = [_chunk_rows(h, s.shape[1] * s.dtype.itemsize) for h, s in zip(hs, shards)]

    def body(*refs):
        ins, outs, bufs, sems = refs[:nt], refs[2 * nt:3 * nt], refs[3 * nt:4 * nt], refs[4 * nt:]
        x, y, c = _place()
        me = 2 * x + y
        sib = (x, y, 1 - c)
        for t in range(nt):
            h, cr = hs[t], crs[t]
            chunks = range(0, h, cr)
            passed = [outs[t].at[2 * px + py, pl.ds(c * h + r0, cr)] for px, py in _other_chips(x, y) for r0 in chunks]
            _stream([(r, r) for r in passed], bufs[t], sems, t, sib)
            own = [(ins[t].at[pl.ds(r0, cr)], outs[t].at[me, pl.ds(r0, cr)]) for r0 in range(0, 2 * h, cr)]
            _stream(own, bufs[t], sems, t, None)
        for t in range(nt):
            three = outs[t].at[pl.ds(0, 3), pl.ds(0, hs[t])]
            pltpu.make_async_remote_copy(src_ref=three, dst_ref=three, send_sem=sems[1].at[t, 0],
                                         recv_sem=sems[2].at[t], device_id=sib, device_id_type=MESH).wait_recv()

    return pl.pallas_call(
        body, name="gather1_finish", in_specs=[ANY] * (2 * nt), out_specs=[ANY] * nt,
        out_shape=[jax.ShapeDtypeStruct(s.shape, s.dtype) for s in stacks],
        scratch_shapes=_stream_scratch([((cr, s.shape[1]), s.dtype) for cr, s in zip(crs, shards)]),
        input_output_aliases={nt + t: t for t in range(nt)},
        compiler_params=pltpu.CompilerParams(vmem_limit_bytes=VMEM_LIMIT_BYTES),
    )(*shards, *stacks)


def _share_halves(fulls):
    nt = len(fulls)
    crs = [_chunk_rows(f.shape[1], f.shape[2] * f.dtype.itemsize) for f in fulls]

    def body(*refs):
        outs, bufs, sems = refs[nt:2 * nt], refs[2 * nt:3 * nt], refs[3 * nt:]
        x, y, c = _place()
        sib = (x, y, 1 - c)
        for t in range(nt):
            rows = [outs[t].at[c, pl.ds(r0, crs[t])] for r0 in range(0, fulls[t].shape[1], crs[t])]
            _stream([(r, r) for r in rows], bufs[t], sems, t, sib)
        for t in range(nt):
            other = outs[t].at[1 - c]
            pltpu.make_async_remote_copy(src_ref=other, dst_ref=other, send_sem=sems[1].at[t, 0],
                                         recv_sem=sems[2].at[t], device_id=sib, device_id_type=MESH).wait_recv()

    return pl.pallas_call(
        body, name="share_halves", in_specs=[ANY] * nt, out_specs=[ANY] * nt,
        out_shape=[jax.ShapeDtypeStruct(f.shape, f.dtype) for f in fulls],
        scratch_shapes=_stream_scratch([((cr, f.shape[2]), f.dtype) for cr, f in zip(crs, fulls)]),
        input_output_aliases={t: t for t in range(nt)},
        compiler_params=pltpu.CompilerParams(vmem_limit_bytes=VMEM_LIMIT_BYTES),
    )(*fulls)


def _allgather_small(name, v):
    def body(v_ref, o_ref, send, recv, lsem):
        x, y, c = _place()
        me = 4 * x + 2 * y + c
        loc = pltpu.make_async_copy(v_ref, o_ref.at[me], lsem)
        loc.start()
        copies = []
        for k in range(1, N_DEV):
            px = 1 - x if k & 4 else x
            py = 1 - y if k & 2 else y
            pc = 1 - c if k & 1 else c
            cp = pltpu.make_async_remote_copy(
                src_ref=v_ref, dst_ref=o_ref.at[me], send_sem=send.at[k - 1], recv_sem=recv.at[k - 1],
                device_id=(px, py, pc), device_id_type=MESH)
            cp.start()
            copies.append(cp)
        for cp in copies:
            cp.wait()
        loc.wait()

    vm = pl.BlockSpec(memory_space=pltpu.VMEM)
    return pl.pallas_call(
        body, name=name, in_specs=[vm], out_specs=vm,
        out_shape=jax.ShapeDtypeStruct((N_DEV,) + v.shape, v.dtype),
        scratch_shapes=[pltpu.SemaphoreType.DMA((N_DEV - 1,))] * 2 + [pltpu.SemaphoreType.DMA],
    )(v)


def kernel(x, mem, norm_g, mem_norm_g, w_kv, w_out, pool_w_in, pool_w_grp, pool_scale, hgrn_w_in, hgrn_lb, hgrn_norm_g, final_g, loss_target, m_norm_g, m_mem_norm_g, m_w_kv, m_w_out, m_pool_w_in, m_pool_w_grp, m_pool_scale, m_hgrn_w_in, m_hgrn_lb, m_hgrn_norm_g, m_final_g, v_norm_g, v_mem_norm_g, v_w_kv, v_w_out, v_pool_w_in, v_pool_w_grp, v_pool_scale, v_hgrn_w_in, v_hgrn_lb, v_hgrn_norm_g, v_final_g):
    _, S, D = x.shape
    M = mem.shape[1]
    EB = 2 * D
    ECA = EB // 4
    EMIX = EB - ECA
    PG = EMIX // N_POOL_GROUPS
    NP0 = EMIX + ECA + EB
    NP1 = 3 * EMIX + ECA + EB
    SH0, SH1 = NP0 // N_CHIPS, NP1 // N_CHIPS
    DK, EK = D // N_CHIPS, EB // N_CHIPS
    TNP = 512 if all(v % 512 == 0 for v in (SH0, SH1, ECA, EMIX)) else 256
    TM = _tile(S, 1024)
    TMF = _tile(S, 2048)
    TD = _tile(D, 512)
    TDW = _tile(D, 1024)
    c0, c1 = SH0 // TNP, SH1 // TNP
    qt, et = EMIX // TNP, ECA // TNP
    chip = 2 * lax.axis_index("x") + lax.axis_index("y")

    xs, ms, tgt = x[0], mem[0], loss_target[0]

    flat = lambda w: w.reshape(-1, w.shape[-1])
    shard2d = [flat(w_kv), flat(w_out), flat(pool_w_in), flat(pool_w_grp), flat(hgrn_w_in)]
    bf = lambda w: w.astype(BF16)
    wkv0, wout0, wpin, g_grp = _allgather_weights([bf(w_kv[0]), bf(w_out[0]), bf(flat(pool_w_in)), bf(flat(pool_w_grp))])
    late = [bf(w_kv[1]), bf(w_out[1]), bf(flat(hgrn_w_in))]
    gather1, token = _split_start("gather1_start", late, [lax.empty((N_CHIPS,) + s.shape, BF16) for s in late],
                                  _gather_plan, 3 * len(late), wpin)
    wgrp = g_grp.reshape(N_CHIPS, N_POOL_GROUPS, PG // N_CHIPS, PG).transpose(1, 0, 2, 3).reshape(N_POOL_GROUPS, PG, PG)

    sds = jax.ShapeDtypeStruct
    tdk, tek, tew = _tile(DK, 512), _tile(EK, 512), _tile(EK, 1024)

    mem_n = _rms_fwd("rms_mem", ms, mem_norm_g.reshape(1, D))
    tkv = _tile(2 * ECA, 512)

    def kv_of(layer, wkv):
        return _matmul(
            f"kv{layer}", mem_n, wkv, grid=(1, 2 * ECA // tkv, D // tdk),
            a_spec=pl.BlockSpec((M, tdk), lambda i, j, k: (0, k)),
            b_spec=pl.BlockSpec((None, tdk, tkv), lambda i, j, k: (k // (DK // tdk), k % (DK // tdk), j)),
            out_shape=sds((M, 2 * ECA), BF16), out_spec=pl.BlockSpec((M, tkv), lambda i, j, k: (0, j)),
            acc_shape=(M, tkv), dims=NN)

    def out_proj(layer, branch, wout, resid):
        return _matmul(
            f"out_proj{layer}", branch, wout, grid=(S // TM, D // TDW, EB // tew),
            a_spec=pl.BlockSpec((TM, tew), IK),
            b_spec=pl.BlockSpec((None, tew, TDW), lambda i, j, k: (k // (EK // tew), k % (EK // tew), j)),
            out_shape=sds((S, D), F32), out_spec=pl.BlockSpec((TM, TDW), IJ),
            acc_shape=(TM, TDW), dims=NN, add=resid, add_spec=pl.BlockSpec((TM, TDW), IJ))

    ones_ca = jnp.ones((1, ECA), F32)

    h0 = _rms_fwd("rms0", xs, norm_g[0:1] + token[0:1, 0:1])
    kv = [kv_of(0, wkv0), None]
    proj0 = _matmul(
        "proj0", h0, wpin, grid=(S // TMF, NP0 // TNP, 1),
        a_spec=pl.BlockSpec((TMF, D), lambda i, j, k: (i, 0)),
        b_spec=pl.BlockSpec((None, D, TNP), lambda i, j, k: (j // c0, 0, j % c0)),
        out_shape=sds((S, NP0), BF16), out_spec=pl.BlockSpec((TMF, TNP), IJ),
        acc_shape=(TMF, TNP), dims=NN)
    pooled = _pool_fwd(proj0, S, EMIX)
    premix0 = _matmul(
        "pool_grp", pooled, wgrp, grid=(S // TM, N_POOL_GROUPS, 1),
        a_spec=pl.BlockSpec((TM, PG), lambda i, j, k: (i, j)),
        b_spec=pl.BlockSpec((None, PG, PG), lambda i, j, k: (j, 0, 0)),
        out_shape=sds((S, EB), BF16), out_spec=pl.BlockSpec((TM, PG), lambda i, j, k: (i, j)),
        acc_shape=(TM, PG), dims=NN)
    premix0 = _ca_fwd("ca_fwd0", proj0, EMIX // ECA, kv[0], premix0, S, ECA, EMIX)
    colscale0 = jnp.concatenate([pool_scale.reshape(1, EMIX), ones_ca], axis=1)
    gblk0 = (EMIX + ECA) // ECA
    branch0 = _gate_fwd("gate_fwd0", premix0, proj0, gblk0, colscale0, S, EB, ECA)
    x1 = out_proj(0, branch0, wout0, xs)

    wkv1, wout1, whin = _gather_finish(*_split_wait("gather1_wait", gather1, _gather_plan, len(late), x1))
    kv[1] = kv_of(1, wkv1)
    h1 = _rms_fwd("rms1", x1, norm_g[1:2])

    def proj1_cols(name, ncols, col_of, out_cols, out_dtype, out_col_of):
        return _matmul(
            name, h1, whin, grid=(S // TMF, ncols, 1),
            a_spec=pl.BlockSpec((TMF, D), lambda i, j, k: (i, 0)),
            b_spec=pl.BlockSpec((None, D, TNP), lambda i, j, k: (col_of(j) // c1, 0, col_of(j) % c1)),
            out_shape=sds((S, out_cols), out_dtype), out_spec=pl.BlockSpec((TMF, TNP), lambda i, j, k: (i, out_col_of(j))),
            acc_shape=(TMF, TNP), dims=NN)

    skip_f = lambda j: jnp.where(j < qt, j, j + qt)
    proj1 = proj1_cols("proj1", NP1 // TNP - qt, skip_f, NP1, BF16, skip_f)
    fgate = proj1_cols("proj1_f", qt, lambda j: j + qt, EMIX, F32, lambda j: j)
    premix1, rstd1, states = _hgrn_fwd(proj1, fgate, hgrn_lb, S, EMIX, EB)
    premix1 = _ca_fwd("ca_fwd1", proj1, 3 * EMIX // ECA, kv[1], premix1, S, ECA, EMIX)
    norm_tiles = _allgather_small("allgather_norm_g", jnp.pad(hgrn_norm_g, ((0, SMALL_ROWS - 1), (0, 0))))
    hg_norm = norm_tiles[0::2, 0, :].reshape(1, EMIX)
    colscale1 = jnp.concatenate([hg_norm, ones_ca], axis=1)
    gblk1 = (3 * EMIX + ECA) // ECA
    branch1 = _gate_fwd("gate_fwd1", premix1, proj1, gblk1, colscale1, S, EB, ECA)
    x2 = out_proj(1, branch1, wout1, x1)

    dx2, dx2b, d_final_g, loss_part = _loss_head(x2, final_g.reshape(1, D), tgt)

    def out_proj_bwd(layer, dxb, branch, wout):
        dbranch = _matmul(
            f"dbranch{layer}", dxb, wout, grid=(S // TMF, EB // tek, 1),
            a_spec=pl.BlockSpec((TMF, D), lambda i, j, k: (i, 0)),
            b_spec=pl.BlockSpec((None, tek, D), lambda i, j, k: (j // (EK // tek), j % (EK // tek), 0)),
            out_shape=sds((S, EB), BF16), out_spec=pl.BlockSpec((TMF, tek), IJ), acc_shape=(TMF, tek), dims=NT)
        dw = _matmul(
            f"dwout{layer}", branch, dxb, grid=(EB // tew, D // TD, 1),
            a_spec=pl.BlockSpec((S, tew), lambda i, j, k: (0, i)), b_spec=pl.BlockSpec((S, TD), lambda i, j, k: (0, j)),
            out_shape=sds((N_CHIPS, EK, D), BF16),
            out_spec=pl.BlockSpec((None, tew, TD), lambda i, j, k: (i // (EK // tew), i % (EK // tew), j)),
            acc_shape=(tew, TD), dims=TN)
        return dbranch, dw

    def kv_bwd(layer, dkv, wkv, dmem_add):
        dkvb = dkv.astype(BF16)
        dmem = _matmul(
            f"dmem{layer}", dkvb, wkv, grid=(1, D // tdk, 2 * ECA // tkv),
            a_spec=pl.BlockSpec((M, tkv), lambda i, j, k: (0, k)),
            b_spec=pl.BlockSpec((None, tdk, tkv), lambda i, j, k: (j // (DK // tdk), j % (DK // tdk), k)),
            out_shape=sds((M, D), F32), out_spec=pl.BlockSpec((M, tdk), lambda i, j, k: (0, j)), acc_shape=(M, tdk),
            dims=NT, add=dmem_add, add_spec=pl.BlockSpec((M, tdk), lambda i, j, k: (0, j)))
        dw = _matmul(
            f"dwkv{layer}", mem_n, dkvb, grid=(D // tdk, 2 * ECA // tkv, 1),
            a_spec=pl.BlockSpec((M, tdk), lambda i, j, k: (0, i)), b_spec=pl.BlockSpec((M, tkv), lambda i, j, k: (0, j)),
            out_shape=sds((N_CHIPS, DK, 2 * ECA), BF16),
            out_spec=pl.BlockSpec((None, tdk, tkv), lambda i, j, k: (i // (DK // tdk), i % (DK // tdk), j)),
            acc_shape=(tdk, tkv), dims=TN)
        return dmem, dw

    dbranch1, gw_out1 = out_proj_bwd(1, dx2b, branch1, wout1)
    dpremix1, drest1, dcol1 = _gate_bwd("gate_bwd1", dbranch1, premix1, proj1, gblk1, colscale1, (S, ECA + EB), 1,
                                        S, EB, ECA)
    drest1, dkv1 = _ca_bwd("ca_bwd1", dpremix1, proj1, 3 * EMIX // ECA, kv[1], drest1, 0, S, ECA, EMIX)
    dqfi, dlb = _hgrn_bwd(dpremix1, premix1, rstd1, states, proj1, fgate, hgrn_lb, S, EMIX)
    nq, nr = 3 * qt, (ECA + EB) // TNP
    dh1 = _matmul(
        "dh1_qfi", dqfi, whin, grid=(S // TM, D // TDW, nq),
        a_spec=pl.BlockSpec((None, TM, TNP), lambda i, j, k: (k // qt, i, k % qt)),
        b_spec=pl.BlockSpec((None, TDW, TNP), lambda i, j, k: (k // c1, j, k % c1)),
        out_shape=sds((S, D), F32), out_spec=pl.BlockSpec((TM, TDW), IJ), acc_shape=(TM, TDW), dims=NT)
    dh1 = _matmul(
        "dh1_rest", drest1, whin, grid=(S // TM, D // TDW, nr), a_spec=pl.BlockSpec((TM, TNP), IK),
        b_spec=pl.BlockSpec((None, TDW, TNP), lambda i, j, k: ((k + nq) // c1, j, (k + nq) % c1)),
        out_shape=sds((S, D), F32), out_spec=pl.BlockSpec((TM, TDW), IJ), acc_shape=(TM, TDW), dims=NT,
        add=dh1, add_spec=pl.BlockSpec((TM, TDW), IJ))
    gw_hin = _matmul(
        "dwhin_qfi", h1, dqfi, grid=(D // TDW, nq, 1), a_spec=pl.BlockSpec((S, TDW), lambda i, j, k: (0, i)),
        b_spec=pl.BlockSpec((None, S, TNP), lambda i, j, k: (j // qt, 0, j % qt)),
        out_shape=sds((N_CHIPS, D, SH1), BF16), out_spec=pl.BlockSpec((None, TDW, TNP), lambda i, j, k: (j // c1, i, j % c1)),
        acc_shape=(TDW, TNP), dims=TN)
    gw_hin = _matmul(
        "dwhin_rest", h1, drest1, grid=(D // TDW, nr, 1), a_spec=pl.BlockSpec((S, TDW), lambda i, j, k: (0, i)),
        b_spec=pl.BlockSpec((S, TNP), lambda i, j, k: (0, j)), out_shape=sds((N_CHIPS, D, SH1), BF16),
        out_spec=pl.BlockSpec((None, TDW, TNP), lambda i, j, k: ((j + nq) // c1, i, (j + nq) % c1)),
        acc_shape=(TDW, TNP), dims=TN, alias=gw_hin)
    dmem, gw_kv1 = kv_bwd(1, dkv1, wkv1, None)

    core_chip = jnp.stack([lax.axis_index("c"), chip]).astype(jnp.int32)

    def reduce_in_chip(tag, stacks):
        got = _exchange_halves(f"exchange_halves{tag}", stacks)
        pairs = [_add_halves(f"add_halves{tag}_{t}", core_chip, g.reshape(N_CHIPS, 2, g.shape[1] // 2, g.shape[2]), r)
                 for t, (g, r) in enumerate(zip(stacks, got))]
        return [p for p, _ in pairs], [own for _, own in pairs]

    parts1, landed1 = reduce_in_chip(1, [gw_kv1, gw_out1, gw_hin])
    scatter1, token1 = _split_start("scatter1_start", parts1, landed1, _scatter_plan, 3 * len(parts1), None)
    dx1, dx1b, d_ng1 = _rms_bwd("rms_bwd1", dh1, x1, norm_g[1:2] + token1[0:1, 0:1], dx2)

    dbranch0, gw_out0 = out_proj_bwd(0, dx1b, branch0, wout0)
    dpremix0, dproj0, dcol0 = _gate_bwd("gate_bwd0", dbranch0, premix0, proj0, gblk0, colscale0, (S, NP0), gblk0,
                                        S, EB, ECA)
    dproj0, dkv0 = _ca_bwd("ca_bwd0", dpremix0, proj0, EMIX // ECA, kv[0], dproj0, EMIX // ECA, S, ECA, EMIX)
    dpooled = _matmul(
        "dpooled", dpremix0, wgrp, grid=(S // TM, N_POOL_GROUPS, 1), a_spec=pl.BlockSpec((TM, PG), IJ),
        b_spec=pl.BlockSpec((None, PG, PG), lambda i, j, k: (j, 0, 0)),
        out_shape=sds((S, EMIX), F32), out_spec=pl.BlockSpec((TM, PG), IJ), acc_shape=(TM, PG), dims=NT)
    dwgrp = _matmul(
        "dwgrp", pooled, dpremix0, grid=(N_POOL_GROUPS, 1, 1), a_spec=pl.BlockSpec((S, PG), lambda i, j, k: (0, i)),
        b_spec=pl.BlockSpec((S, PG), lambda i, j, k: (0, i)), out_shape=sds((N_POOL_GROUPS, PG, PG), F32),
        out_spec=pl.BlockSpec((None, PG, PG), lambda i, j, k: (i, 0, 0)), acc_shape=(PG, PG), dims=TN)
    dproj0 = _pool_bwd(dpooled, dproj0, S, EMIX)
    dh0 = _matmul(
        "dh0", dproj0, wpin, grid=(S // TM, D // TDW, N_CHIPS), a_spec=pl.BlockSpec((TM, SH0), IK),
        b_spec=pl.BlockSpec((None, TDW, SH0), lambda i, j, k: (k, j, 0)),
        out_shape=sds((S, D), F32), out_spec=pl.BlockSpec((TM, TDW), IJ), acc_shape=(TM, TDW), dims=NT)
    gw_pin = _matmul(
        "dwpin", h0, dproj0, grid=(D // TDW, NP0 // TNP, 1), a_spec=pl.BlockSpec((S, TDW), lambda i, j, k: (0, i)),
        b_spec=pl.BlockSpec((S, TNP), lambda i, j, k: (0, j)), out_shape=sds((N_CHIPS, D, SH0), BF16),
        out_spec=pl.BlockSpec((None, TDW, TNP), lambda i, j, k: (j // c0, i, j % c0)), acc_shape=(TDW, TNP), dims=TN)
    grad_x, _, d_ng0 = _rms_bwd("rms_bwd0", dh0, xs, norm_g[0:1], dx1)
    dmem, gw_kv0 = kv_bwd(0, dkv0, wkv0, dmem)
    _, _, d_mng = _rms_bwd("rms_bwd_mem", dmem, ms, mem_norm_g.reshape(1, D), jnp.zeros_like(ms))

    _, landed1 = _split_wait("scatter1_wait", scatter1, _scatter_plan, len(parts1), grad_x)
    gw_grp = dwgrp.reshape(N_POOL_GROUPS, N_CHIPS, PG // N_CHIPS, PG).transpose(1, 0, 2, 3).reshape(N_CHIPS, PG, PG)
    parts0, landed0 = reduce_in_chip(0, [gw_kv0, gw_out0, gw_pin, gw_grp.astype(BF16)])
    landed0 = _scatter_partials(parts0, landed0)
    landed = [landed0[0], landed1[0], landed0[1], landed1[1], landed0[2], landed0[3], landed1[2]]
    fulls = _share_halves([_sum_slots(f"sum_slots{t}", core_chip, p) for t, p in enumerate(landed)])
    f2 = [f.reshape(-1, f.shape[-1]) for f in fulls]
    big_g = [jnp.concatenate(f2[0:2], axis=0), jnp.concatenate(f2[2:4], axis=0), f2[4], f2[5], f2[6]]
    big_names = ["w_kv", "w_out", "pool_w_in", "pool_w_grp", "hgrn_w_in"]
    big_m = [flat(a) for a in (m_w_kv, m_w_out, m_pool_w_in, m_pool_w_grp, m_hgrn_w_in)]
    big_v = [flat(a) for a in (v_w_kv, v_w_out, v_pool_w_in, v_pool_w_grp, v_hgrn_w_in)]
    big_shapes = [w_kv.shape, w_out.shape, pool_w_in.shape, pool_w_grp.shape, hgrn_w_in.shape]
    grads, deltas, new_m, new_v = {}, {}, {}, {}
    for t, n in enumerate(big_names):
        d, mn, vn = _adamw(f"adamw_{n}", shard2d[t], big_g[t], big_m[t], big_v[t])
        grads[n], deltas[n] = big_g[t].reshape(big_shapes[t]), d.reshape(big_shapes[t])
        new_m[n], new_v[n] = mn.reshape(big_shapes[t]), vn.reshape(big_shapes[t])

    Wd = EMIX
    partial = _pack_rows("pack_partials", [d_ng0, d_ng1, d_mng, dcol0[:, :EMIX], dlb, dcol1[:, :EMIX], d_final_g,
                                           loss_part], Wd)
    summed = _small_sum(_allgather_small("allgather_grads", partial), hgrn_lb, 4)
    row = lambda i, n=Wd: summed[i:i + 1, :n]
    nshard = EMIX // N_CHIPS
    g_hg_norm = lax.dynamic_slice_in_dim(row(5), chip * nshard, nshard, axis=1)
    small_names = ["norm_g0", "norm_g1", "mem_norm_g", "pool_scale", "hgrn_lb0", "hgrn_lb1", "hgrn_norm_g", "final_g"]
    small_w = [norm_g[0:1], norm_g[1:2], mem_norm_g.reshape(1, D), pool_scale, hgrn_lb[0:1], hgrn_lb[1:2], hgrn_norm_g,
               final_g.reshape(1, D)]
    small_m = [m_norm_g[0:1], m_norm_g[1:2], m_mem_norm_g.reshape(1, D), m_pool_scale, m_hgrn_lb[0:1], m_hgrn_lb[1:2],
               m_hgrn_norm_g, m_final_g.reshape(1, D)]
    small_v = [v_norm_g[0:1], v_norm_g[1:2], v_mem_norm_g.reshape(1, D), v_pool_scale, v_hgrn_lb[0:1], v_hgrn_lb[1:2],
               v_hgrn_norm_g, v_final_g.reshape(1, D)]
    g_pack = _pack_rows("pack_small_g", [row(0, D), row(1, D), row(2, D), row(3), row(8), row(9), g_hg_norm, row(6, D)], Wd)
    d_pack, m_pack, v_pack = _adamw("adamw_small", _pack_rows("pack_small_w", small_w, Wd), g_pack,
                                    _pack_rows("pack_small_m", small_m, Wd), _pack_rows("pack_small_v", small_v, Wd))
    widths = [v.shape[1] for v in small_w]
    rows = lambda p: {n: p[i, :widths[i]] for i, n in enumerate(small_names)}

    def assemble(r, out):
        out["norm_g"] = jnp.stack([r["norm_g0"], r["norm_g1"]])
        out["mem_norm_g"] = r["mem_norm_g"]
        out["pool_scale"] = r["pool_scale"].reshape(1, EMIX)
        out["hgrn_lb"] = jnp.stack([r["hgrn_lb0"], r["hgrn_lb1"]])
        out["hgrn_norm_g"] = r["hgrn_norm_g"].reshape(1, nshard)
        out["final_g"] = r["final_g"]

    assemble(rows(g_pack), grads)
    assemble(rows(d_pack), deltas)
    assemble(rows(m_pack), new_m)
    assemble(rows(v_pack), new_v)
    loss = summed[7, 0]

    order = ["norm_g", "mem_norm_g", "w_kv", "w_out", "pool_w_in", "pool_w_grp", "pool_scale", "hgrn_w_in", "hgrn_lb",
             "hgrn_norm_g", "final_g"]
    return (loss, grad_x.reshape(1, S, D), *[grads[n] for n in order], *[deltas[n] for n in order],
            *[new_m[n] for n in order], *[new_v[n] for n in order])
```

```python
import functools

import jax
import jax.numpy as jnp
from jax import lax
from jax.experimental import pallas as pl
from jax.experimental.pallas import tpu as pltpu

F32 = jnp.float32
BF16 = jnp.bfloat16
MESH = pl.DeviceIdType.MESH
ANY = pl.BlockSpec(memory_space=pl.ANY)

EPS = 1e-6
HG_HEAD_DIM = 128
HG_CHUNK = 64
CA_HEADS = 4
N_POOL_GROUPS = 4
POOL_HALO = 128
ADAM_LR = 0.001
ADAM_B1 = 0.9
ADAM_B2 = 0.999
ADAM_EPS = 1e-08
ADAM_WD = 0.01
ADAM_STEP = 10
N_CHIPS = 4
N_DEV = 8
VMEM_LIMIT_BYTES = 56 * 1024 * 1024
SMALL_ROWS = 8
STREAM_CHUNK_BYTES = 2 * 1024 * 1024
STREAM_SLOTS = 2


def _params(*sem):
    return pltpu.CompilerParams(dimension_semantics=sem, vmem_limit_bytes=VMEM_LIMIT_BYTES)


def _tile(n, pref):
    t = pref
    while n % t:
        t //= 2
    return t


def _sigmoid(x):
    return 1.0 / (1.0 + jnp.exp(-x))


def _matmul(name, a, b, *, grid, a_spec, b_spec, out_shape, out_spec, acc_shape, dims,
            add=None, add_spec=None, alias=None, after=None):
    nk = grid[2]
    has_add = add is not None
    has_alias = alias is not None
    has_after = after is not None

    def body(*refs):
        a_ref, b_ref = refs[0], refs[1]
        pos = 2
        add_ref = None
        if has_add:
            add_ref = refs[pos]
            pos += 1
        pos += has_alias + has_after
        o_ref = refs[pos]
        prod = lax.dot_general(a_ref[...], b_ref[...], (dims, ((), ())), preferred_element_type=F32)

        def finish(r):
            if has_add:
                r = r + add_ref[...].astype(F32)
            o_ref[...] = r.astype(o_ref.dtype)

        if nk == 1:
            finish(prod)
            return
        acc_ref = refs[pos + 1]
        k = pl.program_id(2)

        @pl.when(k == 0)
        def _():
            acc_ref[...] = prod

        @pl.when(k > 0)
        def _():
            acc_ref[...] += prod

        @pl.when(k == nk - 1)
        def _():
            finish(acc_ref[...])

    operands = [a, b]
    in_specs = [a_spec, b_spec]
    if has_add:
        operands.append(add)
        in_specs.append(add_spec)
    aliases = {}
    if has_alias:
        aliases = {len(operands): 0}
        operands.append(alias)
        in_specs.append(ANY)
    if has_after:
        operands.append(after)
        in_specs.append(ANY)
    return pl.pallas_call(
        body, name=name, grid=grid, in_specs=in_specs, out_specs=out_spec, out_shape=out_shape,
        scratch_shapes=[pltpu.VMEM(acc_shape, F32)] if nk > 1 else [], input_output_aliases=aliases,
        compiler_params=_params("parallel", "parallel", "arbitrary"),
    )(*operands)


IJ = lambda i, j, k: (i, j)
IK = lambda i, j, k: (i, k)
KJ = lambda i, j, k: (k, j)
KI = lambda i, j, k: (k, i)
NN = ((1,), (0,))
NT = ((1,), (1,))
TN = ((0,), (0,))


def _rms_fwd(name, x, g):
    R, D = x.shape
    tr = _tile(R, 256)

    def body(x_ref, g_ref, o_ref):
        xf = x_ref[...]
        r = lax.rsqrt(jnp.mean(xf * xf, axis=-1, keepdims=True) + EPS)
        o_ref[...] = (xf * r * g_ref[...]).astype(o_ref.dtype)

    return pl.pallas_call(
        body, name=name, grid=(R // tr,),
        in_specs=[pl.BlockSpec((tr, D), lambda i: (i, 0)), pl.BlockSpec((1, D), lambda i: (0, 0))],
        out_specs=pl.BlockSpec((tr, D), lambda i: (i, 0)),
        out_shape=jax.ShapeDtypeStruct((R, D), BF16), compiler_params=_params("parallel"),
    )(x, g)


def _rms_bwd(name, dh, x, g, dres):
    R, D = x.shape
    tr = _tile(R, 256)

    def body(dh_ref, x_ref, g_ref, dres_ref, dx_ref, dxb_ref, dg_ref):
        xf = x_ref[...]
        r = lax.rsqrt(jnp.mean(xf * xf, axis=-1, keepdims=True) + EPS)
        xn = xf * r
        d = dh_ref[...]
        dyg = d * g_ref[...]
        dx = r * (dyg - xn * jnp.mean(dyg * xn, axis=-1, keepdims=True)) + dres_ref[...]
        dx_ref[...] = dx
        dxb_ref[...] = dx.astype(BF16)

        @pl.when(pl.program_id(0) == 0)
        def _():
            dg_ref[...] = jnp.zeros_like(dg_ref)

        dg_ref[...] += jnp.sum(d * xn, axis=0, keepdims=True)

    row = pl.BlockSpec((tr, D), lambda i: (i, 0))
    vec = pl.BlockSpec((1, D), lambda i: (0, 0))
    return pl.pallas_call(
        body, name=name, grid=(R // tr,), in_specs=[row, row, vec, row], out_specs=[row, row, vec],
        out_shape=[jax.ShapeDtypeStruct((R, D), F32), jax.ShapeDtypeStruct((R, D), BF16),
                   jax.ShapeDtypeStruct((1, D), F32)],
        compiler_params=_params("arbitrary"),
    )(dh, x, g, dres)


def _loss_head(x2, g, target):
    R, D = x2.shape
    tr = _tile(R, 256)

    def body(x_ref, g_ref, t_ref, dx_ref, dxb_ref, dg_ref, loss_ref):
        xf = x_ref[...]
        gg = g_ref[...]
        r = lax.rsqrt(jnp.mean(xf * xf, axis=-1, keepdims=True) + EPS)
        xn = xf * r
        e = xn * gg - t_ref[...]
        part = 0.5 * jnp.sum(jnp.mean(e * e, axis=-1, keepdims=True), axis=0, keepdims=True)
        dy = e * (1.0 / D)
        dyg = dy * gg
        dx = r * (dyg - xn * jnp.mean(dyg * xn, axis=-1, keepdims=True))
        dx_ref[...] = dx
        dxb_ref[...] = dx.astype(BF16)

        @pl.when(pl.program_id(0) == 0)
        def _():
            dg_ref[...] = jnp.zeros_like(dg_ref)
            loss_ref[...] = jnp.zeros_like(loss_ref)

        dg_ref[...] += jnp.sum(dy * xn, axis=0, keepdims=True)
        loss_ref[...] += jnp.broadcast_to(part, loss_ref.shape)

    row = pl.BlockSpec((tr, D), lambda i: (i, 0))
    vec = pl.BlockSpec((1, D), lambda i: (0, 0))
    return pl.pallas_call(
        body, name="loss_head", grid=(R // tr,), in_specs=[row, vec, row],
        out_specs=[row, row, vec, pl.BlockSpec((1, 128), lambda i: (0, 0))],
        out_shape=[jax.ShapeDtypeStruct((R, D), F32), jax.ShapeDtypeStruct((R, D), BF16),
                   jax.ShapeDtypeStruct((1, D), F32), jax.ShapeDtypeStruct((1, 128), F32)],
        compiler_params=_params("arbitrary"),
    )(x2, g, target)


def _pool_band(tr, reverse, w):
    r = lax.broadcasted_iota(jnp.int32, (tr, tr + POOL_HALO), 0)
    c = lax.broadcasted_iota(jnp.int32, (tr, tr + POOL_HALO), 1)
    if reverse:
        inside = (c >= r) & (c < r + w)
    else:
        cc = c - POOL_HALO
        inside = (cc <= r) & (cc > r - w)
    return jnp.where(inside, 1.0, 0.0).astype(BF16)


def _pool_fwd(proj, S, EMIX):
    PG = EMIX // N_POOL_GROUPS
    cb = PG
    tr = _tile(S, 512)
    per_group = PG // cb

    def body(u_ref, o_ref, ext):
        i = pl.program_id(1)
        w = jnp.left_shift(2, pl.program_id(0) // per_group)

        @pl.when(i == 0)
        def _():
            ext[0:POOL_HALO, :] = jnp.zeros((POOL_HALO, cb), BF16)

        u = u_ref[...]
        ext[POOL_HALO:, :] = u
        win = jnp.dot(_pool_band(tr, False, w), ext[...], preferred_element_type=F32)
        pos = i * tr + lax.broadcasted_iota(jnp.int32, (tr, 1), 0)
        cnt = jnp.minimum(pos + 1, w).astype(F32)
        o_ref[...] = (win / cnt - u.astype(F32)).astype(BF16)
        ext[0:POOL_HALO, :] = u[tr - POOL_HALO:, :]

    return pl.pallas_call(
        body, name="pool_fwd", grid=(EMIX // cb, S // tr),
        in_specs=[pl.BlockSpec((tr, cb), lambda j, i: (i, j))],
        out_specs=pl.BlockSpec((tr, cb), lambda j, i: (i, j)),
        out_shape=jax.ShapeDtypeStruct((S, EMIX), BF16),
        scratch_shapes=[pltpu.VMEM((tr + POOL_HALO, cb), BF16)],
        compiler_params=_params("parallel", "arbitrary"),
    )(proj)


def _pool_bwd(dpooled, dproj, S, EMIX):
    PG = EMIX // N_POOL_GROUPS
    cb = PG
    tr = _tile(S, 512)
    per_group = PG // cb
    nrt = S // tr

    def body(d_ref, _, o_ref, ext):
        step = pl.program_id(1)
        i = nrt - 1 - step
        w = jnp.left_shift(2, pl.program_id(0) // per_group)

        @pl.when(step == 0)
        def _():
            ext[tr:, :] = jnp.zeros((POOL_HALO, cb), BF16)

        d = d_ref[...]
        pos = i * tr + lax.broadcasted_iota(jnp.int32, (tr, 1), 0)
        cnt = jnp.minimum(pos + 1, w).astype(F32)
        z = (d / cnt).astype(BF16)
        ext[0:tr, :] = z
        win = jnp.dot(_pool_band(tr, True, w), ext[...], preferred_element_type=F32)
        o_ref[...] = (win - d).astype(BF16)
        ext[tr:, :] = z[0:POOL_HALO, :]

    return pl.pallas_call(
        body, name="pool_bwd", grid=(EMIX // cb, nrt),
        in_specs=[pl.BlockSpec((tr, cb), lambda j, s: (nrt - 1 - s, j)), ANY],
        out_specs=pl.BlockSpec((tr, cb), lambda j, s: (nrt - 1 - s, j)),
        out_shape=jax.ShapeDtypeStruct(dproj.shape, dproj.dtype),
        scratch_shapes=[pltpu.VMEM((tr + POOL_HALO, cb), BF16)],
        input_output_aliases={1: 0},
        compiler_params=_params("parallel", "arbitrary"),
    )(dpooled, dproj)


def _ca_fwd(name, proj, qblk, kv, premix, S, ECA, EMIX):
    M = kv.shape[0]
    hd = ECA // CA_HEADS
    ts = _tile(S, 512)
    scale = hd ** -0.5

    def body(q_ref, kv_ref, _, o_ref):
        for h in range(CA_HEADS):
            q = q_ref[:, h * hd:(h + 1) * hd]
            k = kv_ref[:, h * hd:(h + 1) * hd]
            v = kv_ref[:, ECA + h * hd:ECA + (h + 1) * hd]
            s = lax.dot_general(q, k, (NT, ((), ())), preferred_element_type=F32) * scale
            s = s - jnp.max(s, axis=-1, keepdims=True)
            p = jnp.exp(s)
            p = p / jnp.sum(p, axis=-1, keepdims=True)
            o = jnp.dot(p.astype(BF16), v, preferred_element_type=F32)
            o_ref[:, h * hd:(h + 1) * hd] = o.astype(BF16)

    return pl.pallas_call(
        body, name=name, grid=(S // ts,),
        in_specs=[pl.BlockSpec((ts, ECA), lambda i: (i, qblk)), pl.BlockSpec((M, 2 * ECA), lambda i: (0, 0)), ANY],
        out_specs=pl.BlockSpec((ts, ECA), lambda i: (i, EMIX // ECA)),
        out_shape=jax.ShapeDtypeStruct(premix.shape, premix.dtype),
        input_output_aliases={2: 0}, compiler_params=_params("parallel"),
    )(proj, kv, premix)


def _ca_bwd(name, dpremix, proj, qblk, kv, dbuf, dblk, S, ECA, EMIX):
    M = kv.shape[0]
    hd = ECA // CA_HEADS
    ts = _tile(S, 512)
    scale = hd ** -0.5

    def body(do_ref, q_ref, kv_ref, _, dq_ref, dkv_ref):
        @pl.when(pl.program_id(0) == 0)
        def _():
            dkv_ref[...] = jnp.zeros_like(dkv_ref)

        for h in range(CA_HEADS):
            lo, hi = h * hd, (h + 1) * hd
            q = q_ref[:, lo:hi]
            k = kv_ref[:, lo:hi]
            v = kv_ref[:, ECA + lo:ECA + hi]
            do = do_ref[:, lo:hi]
            s = lax.dot_general(q, k, (NT, ((), ())), preferred_element_type=F32) * scale
            s = s - jnp.max(s, axis=-1, keepdims=True)
            p = jnp.exp(s)
            p = p / jnp.sum(p, axis=-1, keepdims=True)
            pb = p.astype(BF16)
            dkv_ref[:, ECA + lo:ECA + hi] += lax.dot_general(pb, do, (TN, ((), ())), preferred_element_type=F32)
            dp = lax.dot_general(do, v, (NT, ((), ())), preferred_element_type=F32)
            ds = (p * (dp - jnp.sum(p * dp, axis=-1, keepdims=True)) * scale).astype(BF16)
            dq_ref[:, lo:hi] = jnp.dot(ds, k, preferred_element_type=F32).astype(BF16)
            dkv_ref[:, lo:hi] += lax.dot_general(ds, q, (TN, ((), ())), preferred_element_type=F32)

    return pl.pallas_call(
        body, name=name, grid=(S // ts,),
        in_specs=[pl.BlockSpec((ts, ECA), lambda i: (i, EMIX // ECA)), pl.BlockSpec((ts, ECA), lambda i: (i, qblk)),
                  pl.BlockSpec((M, 2 * ECA), lambda i: (0, 0)), ANY],
        out_specs=[pl.BlockSpec((ts, ECA), lambda i: (i, dblk)), pl.BlockSpec((M, 2 * ECA), lambda i: (0, 0))],
        out_shape=[jax.ShapeDtypeStruct(dbuf.shape, dbuf.dtype), jax.ShapeDtypeStruct((M, 2 * ECA), F32)],
        input_output_aliases={3: 0}, compiler_params=_params("arbitrary"),
    )(dpremix, proj, kv, dbuf)


def _gate_fwd(name, premix, proj, gblk, colscale, S, EB, ECA):
    ts = _tile(S, 512)

    def body(p_ref, g_ref, c_ref, o_ref):
        g = g_ref[...].astype(F32)
        o_ref[...] = (p_ref[...].astype(F32) * c_ref[...] * (g * _sigmoid(g))).astype(BF16)

    return pl.pallas_call(
        body, name=name, grid=(S // ts, EB // ECA),
        in_specs=[pl.BlockSpec((ts, ECA), lambda i, j: (i, j)), pl.BlockSpec((ts, ECA), lambda i, j: (i, gblk + j)),
                  pl.BlockSpec((1, ECA), lambda i, j: (0, j))],
        out_specs=pl.BlockSpec((ts, ECA), lambda i, j: (i, j)),
        out_shape=jax.ShapeDtypeStruct((S, EB), BF16), compiler_params=_params("parallel", "parallel"),
    )(premix, proj, colscale)


def _gate_bwd(name, dbranch, premix, proj, gblk, colscale, dshape, dblk, S, EB, ECA):
    ts = _tile(S, 512)

    def body(db_ref, p_ref, g_ref, c_ref, dp_ref, dg_ref, dc_ref):
        g = g_ref[...].astype(F32)
        sg = _sigmoid(g)
        si = g * sg
        c = c_ref[...]
        db = db_ref[...].astype(F32)
        t = db * p_ref[...].astype(F32)
        dp_ref[...] = (db * si * c).astype(BF16)
        dg_ref[...] = (t * c * (sg * (1.0 + g * (1.0 - sg)))).astype(BF16)

        @pl.when(pl.program_id(1) == 0)
        def _():
            dc_ref[...] = jnp.zeros_like(dc_ref)

        dc_ref[...] += jnp.sum(t * si, axis=0, keepdims=True)

    blk = pl.BlockSpec((ts, ECA), lambda j, i: (i, j))
    vec = pl.BlockSpec((1, ECA), lambda j, i: (0, j))
    return pl.pallas_call(
        body, name=name, grid=(EB // ECA, S // ts),
        in_specs=[blk, blk, pl.BlockSpec((ts, ECA), lambda j, i: (i, gblk + j)), vec],
        out_specs=[blk, pl.BlockSpec((ts, ECA), lambda j, i: (i, dblk + j)), vec],
        out_shape=[jax.ShapeDtypeStruct((S, EB), BF16), jax.ShapeDtypeStruct(dshape, BF16),
                   jax.ShapeDtypeStruct((1, EB), F32)],
        compiler_params=_params("parallel", "arbitrary"),
    )(dbranch, premix, proj, colscale)


def _hgrn_lb(lb_ref):
    l0 = lb_ref[0:1, :]
    l1 = lb_ref[1:2, :]
    mx = jnp.maximum(l0, l1)
    e0 = jnp.exp(l0 - mx)
    e1 = jnp.exp(l1 - mx)
    return e1 / (e0 + e1)


def _bdot(a, b, ca, cb, precision=None):
    return lax.dot_general(a, b, (((ca,), (cb,)), ((0,), (0,))), preferred_element_type=F32, precision=precision)


def _hgrn_chunks(qin, fin, lbh, n):
    C = HG_CHUNK
    row = lax.broadcasted_iota(jnp.int32, (n, C, C), 1)
    col = lax.broadcasted_iota(jnp.int32, (n, C, C), 2)
    causal = row >= col
    sg = _sigmoid(fin)
    f = lbh + (1.0 - lbh) * sg
    k = 1.0 - f
    g = jnp.log(f)
    b = _bdot(jnp.where(causal, 1.0, 0.0), g, 2, 1, lax.Precision.HIGHEST)
    b_last = jnp.sum(g, axis=1, keepdims=True)
    eb = jnp.exp(b)
    einv = jnp.exp(-b)
    eend = jnp.exp(b_last - b)
    sq = _sigmoid(qin)
    a = qin * sq * (HG_HEAD_DIM ** -0.5) * eb
    bm = k * einv
    e = k * eend
    d = jnp.exp(b_last)
    p = jnp.where(causal, _bdot(a.astype(BF16), bm.astype(BF16), 2, 2), 0.0)
    return dict(causal=causal, sg=sg, f=f, eb=eb, einv=einv, eend=eend, sq=sq, a=a, bm=bm, e=e, d=d, p=p)


def _hgrn_fwd(proj, fgate, hgrn_lb, S, EMIX, EB):
    HD, C = HG_HEAD_DIM, HG_CHUNK
    HH = EMIX // HD
    hb = 2 if HH % 2 == 0 else 1
    W = hb * HD
    tr = _tile(S, 512)
    n = tr // C

    def body(q_ref, f_ref, i_ref, lb_ref, o_ref, rstd_ref, st_ref, state):
        @pl.when(pl.program_id(1) == 0)
        def _():
            state[...] = jnp.zeros_like(state)

        lb = _hgrn_lb(lb_ref)
        for h in range(hb):
            cs = slice(h * HD, (h + 1) * HD)
            qin = q_ref[:, cs].astype(F32).reshape(n, C, HD)
            fin = f_ref[:, cs].reshape(n, C, HD)
            v = i_ref[:, cs].reshape(n, C, HD)
            t = _hgrn_chunks(qin, fin, lb[:, cs], n)
            upd = _bdot(v, t["e"].astype(BF16), 1, 1)
            st = state[h]
            for c in range(n):
                st_ref[h, c] = st
                st = st * t["d"][c] + upd[c]
            state[h] = st
            o = _bdot(t["p"].astype(BF16), v, 2, 1) + _bdot(t["a"].astype(BF16), st_ref[h].astype(BF16), 2, 2)
            rstd = lax.rsqrt(jnp.mean(o * o, axis=-1, keepdims=True) + EPS)
            o_ref[:, cs] = (o * rstd).reshape(tr, HD).astype(BF16)
            rstd_ref[:, cs] = jnp.broadcast_to(rstd, (n, C, HD)).reshape(tr, HD)

    blk = lambda off: pl.BlockSpec((tr, W), lambda g, i: (i, off + g))
    return pl.pallas_call(
        body, name="hgrn_fwd", grid=(HH // hb, S // tr),
        in_specs=[blk(0), blk(0), blk(2 * EMIX // W), pl.BlockSpec((2, W), lambda g, i: (0, g))],
        out_specs=[blk(0), blk(0), pl.BlockSpec((hb, n, HD, HD), lambda g, i: (g, i, 0, 0))],
        out_shape=[jax.ShapeDtypeStruct((S, EB), BF16), jax.ShapeDtypeStruct((S, EMIX), F32),
                   jax.ShapeDtypeStruct((HH, S // C, HD, HD), F32)],
        scratch_shapes=[pltpu.VMEM((hb, HD, HD), F32)],
        compiler_params=_params("parallel", "arbitrary"),
    )(proj, fgate, proj, hgrn_lb)


def _hgrn_bwd(dpremix, premix, rstd, states, proj, fgate, hgrn_lb, S, EMIX):
    HD, C = HG_HEAD_DIM, HG_CHUNK
    HH = EMIX // HD
    hb = 2 if HH % 2 == 0 else 1
    W = hb * HD
    tr = _tile(S, 512)
    n = tr // C
    nrt = S // tr

    def body(do_ref, on_ref, rstd_ref, st_ref, q_ref, f_ref, i_ref, lb_ref, d_ref, dlb_ref, dstate, dsbuf):
        @pl.when(pl.program_id(1) == 0)
        def _():
            dstate[...] = jnp.zeros_like(dstate)
            dlb_ref[...] = jnp.zeros_like(dlb_ref)

        lb = _hgrn_lb(lb_ref)
        for h in range(hb):
            cs = slice(h * HD, (h + 1) * HD)
            qin = q_ref[:, cs].astype(F32).reshape(n, C, HD)
            fin = f_ref[:, cs].reshape(n, C, HD)
            v = i_ref[:, cs].reshape(n, C, HD)
            lbh = lb[:, cs]
            t = _hgrn_chunks(qin, fin, lbh, n)
            a, bm, e, d, p = t["a"], t["bm"], t["e"], t["d"], t["p"]
            ab, bmb, eb16 = a.astype(BF16), bm.astype(BF16), e.astype(BF16)
            on = on_ref[:, cs].astype(F32).reshape(n, C, HD)
            dn = do_ref[:, cs].astype(F32).reshape(n, C, HD)
            do = rstd_ref[:, cs].reshape(n, C, HD) * (dn - on * jnp.mean(dn * on, axis=-1, keepdims=True))
            dob = do.astype(BF16)
            grow = _bdot(dob, ab, 1, 1)
            ds = dstate[h]
            for c in reversed(range(n)):
                dsbuf[h, c] = ds
                ds = ds * d[c] + grow[c]
            dstate[h] = ds
            dst = dsbuf[h]
            st = st_ref[h]
            dstb = dst.astype(BF16)
            dp = jnp.where(t["causal"], _bdot(dob, v, 2, 2), 0.0).astype(BF16)
            dv = _bdot(p.astype(BF16), dob, 1, 1) + _bdot(eb16, dstb, 2, 2)
            da = _bdot(dp, bmb, 2, 1) + _bdot(dob, st.astype(BF16), 2, 1)
            dbm = _bdot(dp, ab, 1, 1)
            de = _bdot(v, dstb, 2, 1)
            dd = jnp.sum(dst * st, axis=1, keepdims=True)
            dk = dbm * t["einv"] + de * t["eend"]
            dee = de * e
            db = da * a - dbm * bm - dee
            extra = jnp.sum(dee, axis=1, keepdims=True) + dd * d
            upper = jnp.where(lax.broadcasted_iota(jnp.int32, (n, C, C), 2)
                              >= lax.broadcasted_iota(jnp.int32, (n, C, C), 1), 1.0, 0.0)
            dg = _bdot(upper, db, 2, 1, lax.Precision.HIGHEST) + extra
            df = dg / t["f"] - dk
            sg, sq = t["sg"], t["sq"]
            dq = da * t["eb"] * (HD ** -0.5) * (sq * (1.0 + qin * (1.0 - sq)))
            d_ref[0, :, cs] = dq.reshape(tr, HD).astype(BF16)
            d_ref[1, :, cs] = (df * (1.0 - lbh) * sg * (1.0 - sg)).reshape(tr, HD).astype(BF16)
            d_ref[2, :, cs] = dv.reshape(tr, HD).astype(BF16)
            dlb_ref[:, cs] += jnp.sum((df * (1.0 - sg)).reshape(tr, HD), axis=0, keepdims=True)

    rev = lambda off: pl.BlockSpec((tr, W), lambda g, s: (nrt - 1 - s, off + g))
    return pl.pallas_call(
        body, name="hgrn_bwd", grid=(HH // hb, nrt),
        in_specs=[rev(0), rev(0), rev(0), pl.BlockSpec((hb, n, HD, HD), lambda g, s: (g, nrt - 1 - s, 0, 0)),
                  rev(0), rev(0), rev(2 * EMIX // W), pl.BlockSpec((2, W), lambda g, s: (0, g))],
        out_specs=[pl.BlockSpec((3, tr, W), lambda g, s: (0, nrt - 1 - s, g)), pl.BlockSpec((1, W), lambda g, s: (0, g))],
        out_shape=[jax.ShapeDtypeStruct((3, S, EMIX), BF16), jax.ShapeDtypeStruct((1, EMIX), F32)],
        scratch_shapes=[pltpu.VMEM((hb, HD, HD), F32), pltpu.VMEM((hb, n, HD, HD), F32)],
        compiler_params=_params("parallel", "arbitrary"),
    )(dpremix, premix, rstd, states, proj, fgate, proj, hgrn_lb)


EW_BLOCK_ELEMS = 512 * 1024


def _ew_tiles(R, C):
    tc = C if C <= 4096 else _tile(C, 2048)
    tr = _tile(R, 512)
    while tr * tc > EW_BLOCK_ELEMS and tr % 16 == 0:
        tr //= 2
    return tr, tc


def _add_halves(name, core_chip, grad, got):
    _, _, R, C = grad.shape
    tr, tc = _ew_tiles(R, C)

    def body(c_ref, a_ref, b_ref, o_ref, own_ref):
        r = (a_ref[...].astype(F32) + b_ref[...].astype(F32)).astype(BF16)
        o_ref[...] = r

        @pl.when(pl.program_id(2) == c_ref[1])
        def _():
            own_ref[...] = r

    blk = pl.BlockSpec((None, tr, tc), lambda i, j, s, c: (s, i, j))
    sds = jax.ShapeDtypeStruct(got.shape, BF16)
    return pl.pallas_call(
        body, name=name, out_shape=[sds, sds],
        grid_spec=pltpu.PrefetchScalarGridSpec(
            num_scalar_prefetch=1, grid=(R // tr, C // tc, N_CHIPS),
            in_specs=[pl.BlockSpec((None, None, tr, tc), lambda i, j, s, c: (s, c[0], i, j)), blk],
            out_specs=[blk, pl.BlockSpec((None, tr, tc), lambda i, j, s, c: (c[1], i, j))]),
        compiler_params=_params("parallel", "parallel", "arbitrary"),
    )(core_chip, grad, got)


def _sum_slots(name, core, parts):
    _, R, C = parts.shape
    tr, tc = _ew_tiles(R, C)

    def body(c_ref, p_ref, o_ref):
        acc = p_ref[0].astype(F32)
        for s in range(1, N_CHIPS):
            acc = acc + p_ref[s].astype(F32)
        o_ref[...] = acc

    return pl.pallas_call(
        body, name=name, out_shape=jax.ShapeDtypeStruct((2, R, C), F32),
        grid_spec=pltpu.PrefetchScalarGridSpec(
            num_scalar_prefetch=1, grid=(R // tr, C // tc),
            in_specs=[pl.BlockSpec((N_CHIPS, tr, tc), lambda i, j, c: (0, i, j))],
            out_specs=pl.BlockSpec((None, tr, tc), lambda i, j, c: (c[0], i, j))),
        compiler_params=_params("parallel", "parallel"),
    )(core, parts)


def _adamw(name, w, g, m, v):
    R, C = w.shape
    tr, tc = _ew_tiles(R, C)

    def body(w_ref, g_ref, m_ref, v_ref, d_ref, mo_ref, vo_ref):
        g = g_ref[...]
        mn = ADAM_B1 * m_ref[...] + (1.0 - ADAM_B1) * g
        vn = ADAM_B2 * v_ref[...] + (1.0 - ADAM_B2) * (g * g)
        m_hat = mn / (1.0 - ADAM_B1 ** ADAM_STEP)
        v_hat = vn / (1.0 - ADAM_B2 ** ADAM_STEP)
        d_ref[...] = -ADAM_LR * (m_hat / (jnp.sqrt(v_hat) + ADAM_EPS) + ADAM_WD * w_ref[...])
        mo_ref[...] = mn
        vo_ref[...] = vn

    blk = pl.BlockSpec((tr, tc), lambda i, j: (i, j))
    sds = jax.ShapeDtypeStruct((R, C), F32)
    return pl.pallas_call(
        body, name=name, grid=(R // tr, C // tc), in_specs=[blk] * 4, out_specs=[blk] * 3, out_shape=[sds] * 3,
        compiler_params=_params("parallel", "parallel"),
    )(w, g, m, v)


def _pack_rows(name, vecs, W):
    nv = len(vecs)

    def body(*refs):
        o_ref = refs[nv]
        o_ref[...] = jnp.zeros_like(o_ref)
        for i in range(nv):
            o_ref[i:i + 1, 0:vecs[i].shape[1]] = refs[i][...]

    vm = pl.BlockSpec(memory_space=pltpu.VMEM)
    return pl.pallas_call(
        body, name=name, in_specs=[vm] * nv, out_specs=vm, out_shape=jax.ShapeDtypeStruct((SMALL_ROWS, W), F32),
    )(*vecs)


def _small_sum(gathered, hgrn_lb, lb_row):
    _, T, W = gathered.shape

    def body(g_ref, lb_ref, o_ref):
        acc = g_ref[0]
        for dev in range(1, N_DEV):
            acc = acc + g_ref[dev]
        o_ref[0:T, :] = acc
        lb = _hgrn_lb(lb_ref)
        d1 = o_ref[lb_row:lb_row + 1, :] * (lb * (1.0 - lb))
        o_ref[T:2 * T, :] = jnp.zeros((T, W), F32)
        o_ref[T:T + 1, :] = -d1
        o_ref[T + 1:T + 2, :] = d1

    vm = pl.BlockSpec(memory_space=pltpu.VMEM)
    return pl.pallas_call(
        body, name="small_sum", in_specs=[vm, vm], out_specs=vm, out_shape=jax.ShapeDtypeStruct((2 * T, W), F32),
    )(gathered, hgrn_lb)


def _place():
    return lax.axis_index("x"), lax.axis_index("y"), lax.axis_index("c")


def _other_chips(x, y):
    return [(1 - x, y), (x, 1 - y), (1 - x, 1 - y)]


def _allgather_weights(shards):
    nt = len(shards)

    def body(*refs):
        ins, outs = refs[:nt], refs[nt:2 * nt]
        send, recv, fsend, frecv, lsem = refs[2 * nt:]
        x, y, c = _place()
        me = 2 * x + y
        sib = (x, y, 1 - c)
        chips = _other_chips(x, y)

        def half(t, slot, hc):
            h = shards[t].shape[0] // 2
            return outs[t].at[slot, pl.ds(hc * h, h)]

        def copy(t, slot, hc, sems, j, to, src=None):
            return pltpu.make_async_remote_copy(
                src_ref=half(t, slot, hc) if src is None else src, dst_ref=half(t, slot, hc),
                send_sem=sems[0].at[t, j], recv_sem=sems[1].at[t, j], device_id=to, device_id_type=MESH)

        local, first, passed = [], [], []
        for t in range(nt):
            h = shards[t].shape[0] // 2
            cp = pltpu.make_async_copy(ins[t], outs[t].at[me], lsem.at[t])
            cp.start()
            local.append(cp)
            for j, (px, py) in enumerate(chips):
                cp = copy(t, me, c, (send, recv), j, (px, py, c), src=ins[t].at[pl.ds(c * h, h)])
                cp.start()
                first.append(cp)
        for t in range(nt):
            for j, (px, py) in enumerate(chips):
                slot = 2 * px + py
                copy(t, slot, c, (send, recv), j, (px, py, c)).wait_recv()
                cp = copy(t, slot, c, (fsend, frecv), j, sib)
                cp.start()
                passed.append(cp)
        for t in range(nt):
            for j, (px, py) in enumerate(chips):
                copy(t, 2 * px + py, 1 - c, (fsend, frecv), j, sib).wait_recv()
        for cp in first + passed:
            cp.wait_send()
        for cp in local:
            cp.wait()

    return pl.pallas_call(
        body, name="allgather_weights", in_specs=[ANY] * nt, out_specs=[ANY] * nt,
        out_shape=[jax.ShapeDtypeStruct((N_CHIPS,) + s.shape, s.dtype) for s in shards],
        scratch_shapes=[pltpu.SemaphoreType.DMA((nt, 3))] * 4 + [pltpu.SemaphoreType.DMA((nt,))],
    )(*shards)


def _chunk_rows(rows, row_bytes):
    cr = rows
    while cr * row_bytes > STREAM_CHUNK_BYTES and cr % 32 == 0:
        cr //= 2
    return cr


def _stream(pairs, buf, sems, t, peer):
    lsem, ssem, rsem = sems
    sends = []
    for k, (src, dst) in enumerate(pairs):
        slot = k % STREAM_SLOTS
        if k >= STREAM_SLOTS:
            sends[k - STREAM_SLOTS]()
        load = pltpu.make_async_copy(src, buf.at[slot], lsem.at[t, slot])
        load.start()
        load.wait()
        if peer is None:
            cp = pltpu.make_async_copy(buf.at[slot], dst, ssem.at[t, slot])
            cp.start()
            sends.append(cp.wait)
        else:
            cp = pltpu.make_async_remote_copy(src_ref=buf.at[slot], dst_ref=dst, send_sem=ssem.at[t, slot],
                                              recv_sem=rsem.at[t], device_id=peer, device_id_type=MESH)
            cp.start()
            sends.append(cp.wait_send)
    for done in sends[-STREAM_SLOTS:]:
        done()


def _stream_scratch(shapes):
    nt = len(shapes)
    return ([pltpu.VMEM((STREAM_SLOTS,) + s, d) for s, d in shapes]
            + [pltpu.SemaphoreType.DMA((nt, STREAM_SLOTS)), pltpu.SemaphoreType.DMA((nt, STREAM_SLOTS)),
               pltpu.SemaphoreType.DMA((nt,))])


def _exchange_halves(name, grads):
    nt = len(grads)
    hs = [g.shape[1] // 2 for g in grads]
    crs = [_chunk_rows(h, g.shape[2] * g.dtype.itemsize) for h, g in zip(hs, grads)]

    def body(*refs):
        ins, gots, bufs, sems = refs[:nt], refs[nt:2 * nt], refs[2 * nt:3 * nt], refs[3 * nt:]
        x, y, c = _place()
        sib = (x, y, 1 - c)
        for t in range(nt):
            h, cr = hs[t], crs[t]
            pairs = [(ins[t].at[b, pl.ds((1 - c) * h + r0, cr)], gots[t].at[b, pl.ds(r0, cr)])
                     for b in range(N_CHIPS) for r0 in range(0, h, cr)]
            _stream(pairs, bufs[t], sems, t, sib)
        for t in range(nt):
            pltpu.make_async_remote_copy(src_ref=gots[t], dst_ref=gots[t], send_sem=sems[1].at[t, 0],
                                         recv_sem=sems[2].at[t], device_id=sib, device_id_type=MESH).wait_recv()

    return pl.pallas_call(
        body, name=name, in_specs=[ANY] * nt, out_specs=[ANY] * nt,
        out_shape=[jax.ShapeDtypeStruct((N_CHIPS, h, g.shape[2]), g.dtype) for h, g in zip(hs, grads)],
        scratch_shapes=_stream_scratch([((cr, g.shape[2]), g.dtype) for cr, g in zip(crs, grads)]),
        compiler_params=pltpu.CompilerParams(vmem_limit_bytes=VMEM_LIMIT_BYTES),
    )(*grads)


def _scatter_plan(srcs, dsts):
    x, y, c = _place()
    me = 2 * x + y
    return [(srcs[t].at[2 * px + py], dsts[t].at[me], (px, py, c))
            for t in range(len(srcs)) for px, py in _other_chips(x, y)]


def _slot(dst, chip, r0, rows, cols):
    if len(dst.shape) == 3:
        return dst.at[chip, pl.ds(r0, rows)]
    return dst.at[pl.ds(r0, rows), pl.ds(pl.multiple_of(chip * cols, 128), cols)]


def _gather_plan(srcs, dsts):
    x, y, c = _place()
    me = 2 * x + y
    plan = []
    for t in range(len(srcs)):
        h, cols = srcs[t].shape[0] // 2, srcs[t].shape[1]
        plan += [(srcs[t].at[pl.ds(c * h, h)], _slot(dsts[t], me, c * h, h, cols), (px, py, c))
                 for px, py in _other_chips(x, y)]
    return plan


HBM_SPEC = pl.BlockSpec(memory_space=pltpu.HBM)
SEM_SPEC = pl.BlockSpec(memory_space=pltpu.SEMAPHORE)


def _split_start(name, srcs, dsts, plan, ncopies, after):
    bufs = [pltpu.with_memory_space_constraint(a, pltpu.HBM) for a in list(srcs) + list(dsts)]
    nb, ns = len(bufs), len(srcs)
    operands = bufs + ([after] if after is not None else [])

    def body(*refs):
        outs = refs[len(operands):]
        send, recv, token = outs[0], outs[1], outs[-1]
        for i, (src, dst, dev) in enumerate(plan(refs[:ns], refs[ns:nb])):
            pltpu.make_async_remote_copy(src_ref=src, dst_ref=dst, send_sem=send.at[i], recv_sem=recv.at[i],
                                         device_id=dev, device_id_type=MESH).start()
        token[...] = jnp.zeros_like(token)

    res = pl.pallas_call(
        body, name=name,
        out_shape=[pltpu.SemaphoreType.DMA((ncopies,)), pltpu.SemaphoreType.DMA((ncopies,))]
        + [pltpu.HBM(a.shape, a.dtype) for a in bufs] + [jax.ShapeDtypeStruct((8, 128), F32)],
        in_specs=[HBM_SPEC] * nb + [ANY] * (len(operands) - nb),
        out_specs=[SEM_SPEC, SEM_SPEC] + [HBM_SPEC] * nb + [pl.BlockSpec(memory_space=pltpu.VMEM)],
        input_output_aliases={i: 2 + i for i in range(nb)},
        compiler_params=pltpu.CompilerParams(has_side_effects=pltpu.SideEffectType.DATAFLOW_SIDE_EFFECTING),
    )(*operands)
    return res[:-1], res[-1]


def _split_wait(name, started, plan, ns, after):
    send, recv, bufs = started[0], started[1], list(started[2:])
    nb = len(bufs)

    def body(*refs):
        send_ref, recv_ref = refs[nb], refs[nb + 1]
        for i, (src, dst, dev) in enumerate(plan(refs[:ns], refs[ns:nb])):
            cp = pltpu.make_async_remote_copy(src_ref=src, dst_ref=dst, send_sem=send_ref.at[i], recv_sem=recv_ref.at[i],
                                              device_id=dev, device_id_type=MESH)
            cp.wait_send()
            cp.wait_recv()

    res = pl.pallas_call(
        body, name=name, out_shape=[pltpu.HBM(a.shape, a.dtype) for a in bufs],
        in_specs=[HBM_SPEC] * nb + [SEM_SPEC, SEM_SPEC, ANY], out_specs=[HBM_SPEC] * nb,
        input_output_aliases={i: i for i in range(nb)},
        compiler_params=pltpu.CompilerParams(has_side_effects=pltpu.SideEffectType.DATAFLOW_SIDE_EFFECTING),
    )(*bufs, send, recv, after)
    return res[:ns], res[ns:]


def _gather_finish(shards, stacks):
    nt = len(shards)
    hs = [s.shape[0] // 2 for s in shards]
    crs = [_chunk_rows(h, s.shape[1] * s.dtype.itemsize) for h, s in zip(hs, shards)]

    def body(*refs):
        ins, outs, bufs, sems = refs[:nt], refs[2 * nt:3 * nt], refs[3 * nt:4 * nt], refs[4 * nt:]
        x, y, c = _place()
        me = 2 * x + y
        sib = (x, y, 1 - c)
        for t in range(nt):
            h, cr, cols = hs[t], crs[t], shards[t].shape[1]
            passed = [_slot(outs[t], 2 * px + py, c * h + r0, cr, cols)
                      for px, py in _other_chips(x, y) for r0 in range(0, h, cr)]
            _stream([(r, r) for r in passed], bufs[t], sems, t, sib)
            own = [(ins[t].at[pl.ds(r0, cr)], _slot(outs[t], me, r0, cr, cols)) for r0 in range(0, 2 * h, cr)]
            _stream(own, bufs[t], sems, t, None)
        for t in range(nt):
            if len(stacks[t].shape) == 3:
                three = outs[t].at[pl.ds(0, 3), pl.ds(0, hs[t])]
            else:
                three = outs[t].at[pl.ds(0, hs[t]), pl.ds(0, 3 * shards[t].shape[1])]
            pltpu.make_async_remote_copy(src_ref=three, dst_ref=three, send_sem=sems[1].at[t, 0],
                                         recv_sem=sems[2].at[t], device_id=sib, device_id_type=MESH).wait_recv()

    return pl.pallas_call(
        body, name="gather1_finish", in_specs=[ANY] * (2 * nt), out_specs=[ANY] * nt,
        out_shape=[jax.ShapeDtypeStruct(s.shape, s.dtype) for s in stacks],
        scratch_shapes=_stream_scratch([((cr, s.shape[1]), s.dtype) for cr, s in zip(crs, shards)]),
        input_output_aliases={nt + t: t for t in range(nt)},
        compiler_params=pltpu.CompilerParams(vmem_limit_bytes=VMEM_LIMIT_BYTES),
    )(*shards, *stacks)


def _share_halves(fulls):
    nt = len(fulls)
    crs = [_chunk_rows(f.shape[1], f.shape[2] * f.dtype.itemsize) for f in fulls]

    def body(*refs):
        outs, bufs, sems = refs[nt:2 * nt], refs[2 * nt:3 * nt], refs[3 * nt:]
        x, y, c = _place()
        sib = (x, y, 1 - c)
        for t in range(nt):
            rows = [outs[t].at[c, pl.ds(r0, crs[t])] for r0 in range(0, fulls[t].shape[1], crs[t])]
            _stream([(r, r) for r in rows], bufs[t], sems, t, sib)
        for t in range(nt):
            other = outs[t].at[1 - c]
            pltpu.make_async_remote_copy(src_ref=other, dst_ref=other, send_sem=sems[1].at[t, 0],
                                         recv_sem=sems[2].at[t], device_id=sib, device_id_type=MESH).wait_recv()

    return pl.pallas_call(
        body, name="share_halves", in_specs=[ANY] * nt, out_specs=[ANY] * nt,
        out_shape=[jax.ShapeDtypeStruct(f.shape, f.dtype) for f in fulls],
        scratch_shapes=_stream_scratch([((cr, f.shape[2]), f.dtype) for cr, f in zip(crs, fulls)]),
        input_output_aliases={t: t for t in range(nt)},
        compiler_params=pltpu.CompilerParams(vmem_limit_bytes=VMEM_LIMIT_BYTES),
    )(*fulls)


def _allgather_small(name, v):
    def body(v_ref, o_ref, send, recv, lsem):
        x, y, c = _place()
        me = 4 * x + 2 * y + c
        loc = pltpu.make_async_copy(v_ref, o_ref.at[me], lsem)
        loc.start()
        copies = []
        for k in range(1, N_DEV):
            px = 1 - x if k & 4 else x
            py = 1 - y if k & 2 else y
            pc = 1 - c if k & 1 else c
            cp = pltpu.make_async_remote_copy(
                src_ref=v_ref, dst_ref=o_ref.at[me], send_sem=send.at[k - 1], recv_sem=recv.at[k - 1],
                device_id=(px, py, pc), device_id_type=MESH)
            cp.start()
            copies.append(cp)
        for cp in copies:
            cp.wait()
        loc.wait()

    vm = pl.BlockSpec(memory_space=pltpu.VMEM)
    return pl.pallas_call(
        body, name=name, in_specs=[vm], out_specs=vm,
        out_shape=jax.ShapeDtypeStruct((N_DEV,) + v.shape, v.dtype),
        scratch_shapes=[pltpu.SemaphoreType.DMA((N_DEV - 1,))] * 2 + [pltpu.SemaphoreType.DMA],
    )(v)


def kernel(x, mem, norm_g, mem_norm_g, w_kv, w_out, pool_w_in, pool_w_grp, pool_scale, hgrn_w_in, hgrn_lb, hgrn_norm_g, final_g, loss_target, m_norm_g, m_mem_norm_g, m_w_kv, m_w_out, m_pool_w_in, m_pool_w_grp, m_pool_scale, m_hgrn_w_in, m_hgrn_lb, m_hgrn_norm_g, m_final_g, v_norm_g, v_mem_norm_g, v_w_kv, v_w_out, v_pool_w_in, v_pool_w_grp, v_pool_scale, v_hgrn_w_in, v_hgrn_lb, v_hgrn_norm_g, v_final_g):
    _, S, D = x.shape
    M = mem.shape[1]
    EB = 2 * D
    ECA = EB // 4
    EMIX = EB - ECA
    PG = EMIX // N_POOL_GROUPS
    NP0 = EMIX + ECA + EB
    NP1 = 3 * EMIX + ECA + EB
    SH0, SH1 = NP0 // N_CHIPS, NP1 // N_CHIPS
    DK, EK = D // N_CHIPS, EB // N_CHIPS
    TNP = 512 if all(v % 512 == 0 for v in (SH0, SH1, ECA, EMIX)) else 256
    TM = _tile(S, 1024)
    TMF = _tile(S, 2048)
    TD = _tile(D, 512)
    TDW = _tile(D, 1024)
    c0, c1 = SH0 // TNP, SH1 // TNP
    qt, et = EMIX // TNP, ECA // TNP
    chip = 2 * lax.axis_index("x") + lax.axis_index("y")

    xs, ms, tgt = x[0], mem[0], loss_target[0]

    flat = lambda w: w.reshape(-1, w.shape[-1])
    shard2d = [flat(w_kv), flat(w_out), flat(pool_w_in), flat(pool_w_grp), flat(hgrn_w_in)]
    bf = lambda w: w.astype(BF16)
    wkv0, wout0, wpin, g_grp = _allgather_weights([bf(w_kv[0]), bf(w_out[0]), bf(flat(pool_w_in)), bf(flat(pool_w_grp))])
    late = [bf(w_kv[1]), bf(w_out[1]), bf(flat(hgrn_w_in))]
    landing = [lax.empty((N_CHIPS,) + late[0].shape, BF16), lax.empty((N_CHIPS,) + late[1].shape, BF16),
               lax.empty((D, NP1), BF16)]
    gather1, token = _split_start("gather1_start", late, landing, _gather_plan, 3 * len(late), wpin)
    wgrp = g_grp.reshape(N_CHIPS, N_POOL_GROUPS, PG // N_CHIPS, PG).transpose(1, 0, 2, 3).reshape(N_POOL_GROUPS, PG, PG)

    sds = jax.ShapeDtypeStruct
    tdk, tek, tew = _tile(DK, 512), _tile(EK, 512), _tile(EK, 1024)

    mem_n = _rms_fwd("rms_mem", ms, mem_norm_g.reshape(1, D))
    tkv = _tile(2 * ECA, 512)

    def kv_of(layer, wkv):
        return _matmul(
            f"kv{layer}", mem_n, wkv, grid=(1, 2 * ECA // tkv, D // tdk),
            a_spec=pl.BlockSpec((M, tdk), lambda i, j, k: (0, k)),
            b_spec=pl.BlockSpec((None, tdk, tkv), lambda i, j, k: (k // (DK // tdk), k % (DK // tdk), j)),
            out_shape=sds((M, 2 * ECA), BF16), out_spec=pl.BlockSpec((M, tkv), lambda i, j, k: (0, j)),
            acc_shape=(M, tkv), dims=NN)

    def out_proj(layer, branch, wout, resid):
        return _matmul(
            f"out_proj{layer}", branch, wout, grid=(S // TM, D // TDW, EB // tew),
            a_spec=pl.BlockSpec((TM, tew), IK),
            b_spec=pl.BlockSpec((None, tew, TDW), lambda i, j, k: (k // (EK // tew), k % (EK // tew), j)),
            out_shape=sds((S, D), F32), out_spec=pl.BlockSpec((TM, TDW), IJ),
            acc_shape=(TM, TDW), dims=NN, add=resid, add_spec=pl.BlockSpec((TM, TDW), IJ))

    ones_ca = jnp.ones((1, ECA), F32)

    h0 = _rms_fwd("rms0", xs, norm_g[0:1] + token[0:1, 0:1])
    kv = [kv_of(0, wkv0), None]
    proj0 = _matmul(
        "proj0", h0, wpin, grid=(S // TMF, NP0 // TNP, 1),
        a_spec=pl.BlockSpec((TMF, D), lambda i, j, k: (i, 0)),
        b_spec=pl.BlockSpec((None, D, TNP), lambda i, j, k: (j // c0, 0, j % c0)),
        out_shape=sds((S, NP0), BF16), out_spec=pl.BlockSpec((TMF, TNP), IJ),
        acc_shape=(TMF, TNP), dims=NN)
    pooled = _pool_fwd(proj0, S, EMIX)
    premix0 = _matmul(
        "pool_grp", pooled, wgrp, grid=(S // TM, N_POOL_GROUPS, 1),
        a_spec=pl.BlockSpec((TM, PG), lambda i, j, k: (i, j)),
        b_spec=pl.BlockSpec((None, PG, PG), lambda i, j, k: (j, 0, 0)),
        out_shape=sds((S, EB), BF16), out_spec=pl.BlockSpec((TM, PG), lambda i, j, k: (i, j)),
        acc_shape=(TM, PG), dims=NN)
    premix0 = _ca_fwd("ca_fwd0", proj0, EMIX // ECA, kv[0], premix0, S, ECA, EMIX)
    colscale0 = jnp.concatenate([pool_scale.reshape(1, EMIX), ones_ca], axis=1)
    gblk0 = (EMIX + ECA) // ECA
    branch0 = _gate_fwd("gate_fwd0", premix0, proj0, gblk0, colscale0, S, EB, ECA)
    x1 = out_proj(0, branch0, wout0, xs)

    wkv1, wout1, whin = _gather_finish(*_split_wait("gather1_wait", gather1, _gather_plan, len(late), x1))
    kv[1] = kv_of(1, wkv1)
    h1 = _rms_fwd("rms1", x1, norm_g[1:2])

    def proj1_cols(name, ncols, col_of, out_cols, out_dtype, out_col_of):
        return _matmul(
            name, h1, whin, grid=(S // TMF, ncols, 1),
            a_spec=pl.BlockSpec((TMF, D), lambda i, j, k: (i, 0)),
            b_spec=pl.BlockSpec((D, TNP), lambda i, j, k: (0, col_of(j))),
            out_shape=sds((S, out_cols), out_dtype), out_spec=pl.BlockSpec((TMF, TNP), lambda i, j, k: (i, out_col_of(j))),
            acc_shape=(TMF, TNP), dims=NN)

    skip_f = lambda j: jnp.where(j < qt, j, j + qt)
    proj1 = proj1_cols("proj1", NP1 // TNP - qt, skip_f, NP1, BF16, skip_f)
    fgate = proj1_cols("proj1_f", qt, lambda j: j + qt, EMIX, F32, lambda j: j)
    premix1, rstd1, states = _hgrn_fwd(proj1, fgate, hgrn_lb, S, EMIX, EB)
    premix1 = _ca_fwd("ca_fwd1", proj1, 3 * EMIX // ECA, kv[1], premix1, S, ECA, EMIX)
    norm_tiles = _allgather_small("allgather_norm_g", jnp.pad(hgrn_norm_g, ((0, SMALL_ROWS - 1), (0, 0))))
    hg_norm = norm_tiles[0::2, 0, :].reshape(1, EMIX)
    colscale1 = jnp.concatenate([hg_norm, ones_ca], axis=1)
    gblk1 = (3 * EMIX + ECA) // ECA
    branch1 = _gate_fwd("gate_fwd1", premix1, proj1, gblk1, colscale1, S, EB, ECA)
    x2 = out_proj(1, branch1, wout1, x1)

    dx2, dx2b, d_final_g, loss_part = _loss_head(x2, final_g.reshape(1, D), tgt)

    def out_proj_bwd(layer, dxb, branch, wout):
        dbranch = _matmul(
            f"dbranch{layer}", dxb, wout, grid=(S // TMF, EB // tek, 1),
            a_spec=pl.BlockSpec((TMF, D), lambda i, j, k: (i, 0)),
            b_spec=pl.BlockSpec((None, tek, D), lambda i, j, k: (j // (EK // tek), j % (EK // tek), 0)),
            out_shape=sds((S, EB), BF16), out_spec=pl.BlockSpec((TMF, tek), IJ), acc_shape=(TMF, tek), dims=NT)
        dw = _matmul(
            f"dwout{layer}", branch, dxb, grid=(EB // tew, D // TD, 1),
            a_spec=pl.BlockSpec((S, tew), lambda i, j, k: (0, i)), b_spec=pl.BlockSpec((S, TD), lambda i, j, k: (0, j)),
            out_shape=sds((N_CHIPS, EK, D), BF16),
            out_spec=pl.BlockSpec((None, tew, TD), lambda i, j, k: (i // (EK // tew), i % (EK // tew), j)),
            acc_shape=(tew, TD), dims=TN)
        return dbranch, dw

    def kv_bwd(layer, dkv, wkv, dmem_add):
        dkvb = dkv.astype(BF16)
        dmem = _matmul(
            f"dmem{layer}", dkvb, wkv, grid=(1, D // tdk, 2 * ECA // tkv),
            a_spec=pl.BlockSpec((M, tkv), lambda i, j, k: (0, k)),
            b_spec=pl.BlockSpec((None, tdk, tkv), lambda i, j, k: (j // (DK // tdk), j % (DK // tdk), k)),
            out_shape=sds((M, D), F32), out_spec=pl.BlockSpec((M, tdk), lambda i, j, k: (0, j)), acc_shape=(M, tdk),
            dims=NT, add=dmem_add, add_spec=pl.BlockSpec((M, tdk), lambda i, j, k: (0, j)))
        dw = _matmul(
            f"dwkv{layer}", mem_n, dkvb, grid=(D // tdk, 2 * ECA // tkv, 1),
            a_spec=pl.BlockSpec((M, tdk), lambda i, j, k: (0, i)), b_spec=pl.BlockSpec((M, tkv), lambda i, j, k: (0, j)),
            out_shape=sds((N_CHIPS, DK, 2 * ECA), BF16),
            out_spec=pl.BlockSpec((None, tdk, tkv), lambda i, j, k: (i // (DK // tdk), i % (DK // tdk), j)),
            acc_shape=(tdk, tkv), dims=TN)
        return dmem, dw

    dbranch1, gw_out1 = out_proj_bwd(1, dx2b, branch1, wout1)
    dpremix1, drest1, dcol1 = _gate_bwd("gate_bwd1", dbranch1, premix1, proj1, gblk1, colscale1, (S, ECA + EB), 1,
                                        S, EB, ECA)
    drest1, dkv1 = _ca_bwd("ca_bwd1", dpremix1, proj1, 3 * EMIX // ECA, kv[1], drest1, 0, S, ECA, EMIX)
    dqfi, dlb = _hgrn_bwd(dpremix1, premix1, rstd1, states, proj1, fgate, hgrn_lb, S, EMIX)
    nq, nr = 3 * qt, (ECA + EB) // TNP
    tkh = _tile(EMIX, 1024) if (ECA + EB) % _tile(EMIX, 1024) == 0 else TNP
    kq = EMIX // tkh
    dh1 = _matmul(
        "dh1_qfi", dqfi, whin, grid=(S // TM, D // TDW, 3 * kq),
        a_spec=pl.BlockSpec((None, TM, tkh), lambda i, j, k: (k // kq, i, k % kq)),
        b_spec=pl.BlockSpec((TDW, tkh), lambda i, j, k: (j, k)),
        out_shape=sds((S, D), F32), out_spec=pl.BlockSpec((TM, TDW), IJ), acc_shape=(TM, TDW), dims=NT)
    dh1 = _matmul(
        "dh1_rest", drest1, whin, grid=(S // TM, D // TDW, (ECA + EB) // tkh), a_spec=pl.BlockSpec((TM, tkh), IK),
        b_spec=pl.BlockSpec((TDW, tkh), lambda i, j, k: (j, k + 3 * kq)),
        out_shape=sds((S, D), F32), out_spec=pl.BlockSpec((TM, TDW), IJ), acc_shape=(TM, TDW), dims=NT,
        add=dh1, add_spec=pl.BlockSpec((TM, TDW), IJ))
    gw_hin = _matmul(
        "dwhin_qfi", h1, dqfi, grid=(D // TDW, nq, 1), a_spec=pl.BlockSpec((S, TDW), lambda i, j, k: (0, i)),
        b_spec=pl.BlockSpec((None, S, TNP), lambda i, j, k: (j // qt, 0, j % qt)),
        out_shape=sds((N_CHIPS, D, SH1), BF16), out_spec=pl.BlockSpec((None, TDW, TNP), lambda i, j, k: (j // c1, i, j % c1)),
        acc_shape=(TDW, TNP), dims=TN)
    gw_hin = _matmul(
        "dwhin_rest", h1, drest1, grid=(D // TDW, nr, 1), a_spec=pl.BlockSpec((S, TDW), lambda i, j, k: (0, i)),
        b_spec=pl.BlockSpec((S, TNP), lambda i, j, k: (0, j)), out_shape=sds((N_CHIPS, D, SH1), BF16),
        out_spec=pl.BlockSpec((None, TDW, TNP), lambda i, j, k: ((j + nq) // c1, i, (j + nq) % c1)),
        acc_shape=(TDW, TNP), dims=TN, alias=gw_hin)
    dmem, gw_kv1 = kv_bwd(1, dkv1, wkv1, None)

    core_chip = jnp.stack([lax.axis_index("c"), chip]).astype(jnp.int32)

    def reduce_in_chip(tag, stacks):
        got = _exchange_halves(f"exchange_halves{tag}", stacks)
        pairs = [_add_halves(f"add_halves{tag}_{t}", core_chip, g.reshape(N_CHIPS, 2, g.shape[1] // 2, g.shape[2]), r)
                 for t, (g, r) in enumerate(zip(stacks, got))]
        return [p for p, _ in pairs], [own for _, own in pairs]

    parts1, landed1 = reduce_in_chip(1, [gw_kv1, gw_out1, gw_hin])
    scatter1, token1 = _split_start("scatter1_start", parts1, landed1, _scatter_plan, 3 * len(parts1), None)
    dx1, dx1b, d_ng1 = _rms_bwd("rms_bwd1", dh1, x1, norm_g[1:2] + token1[0:1, 0:1], dx2)

    dbranch0, gw_out0 = out_proj_bwd(0, dx1b, branch0, wout0)
    dpremix0, dproj0, dcol0 = _gate_bwd("gate_bwd0", dbranch0, premix0, proj0, gblk0, colscale0, (S, NP0), gblk0,
                                        S, EB, ECA)
    dproj0, dkv0 = _ca_bwd("ca_bwd0", dpremix0, proj0, EMIX // ECA, kv[0], dproj0, EMIX // ECA, S, ECA, EMIX)
    dmem, gw_kv0 = kv_bwd(0, dkv0, wkv0, dmem)
    parts_a, landed_a = reduce_in_chip("0a", [gw_kv0, gw_out0])
    scatter_a, token_a = _split_start("scatter0a_start", parts_a, landed_a, _scatter_plan, 3 * len(parts_a), None)
    dpooled = _matmul(
        "dpooled", dpremix0, wgrp, grid=(S // TM, N_POOL_GROUPS, 1), a_spec=pl.BlockSpec((TM, PG), IJ),
        b_spec=pl.BlockSpec((None, PG, PG), lambda i, j, k: (j, 0, 0)),
        out_shape=sds((S, EMIX), F32), out_spec=pl.BlockSpec((TM, PG), IJ), acc_shape=(TM, PG), dims=NT, after=token_a)
    dwgrp = _matmul(
        "dwgrp", pooled, dpremix0, grid=(N_POOL_GROUPS, 1, 1), a_spec=pl.BlockSpec((S, PG), lambda i, j, k: (0, i)),
        b_spec=pl.BlockSpec((S, PG), lambda i, j, k: (0, i)), out_shape=sds((N_POOL_GROUPS, PG, PG), F32),
        out_spec=pl.BlockSpec((None, PG, PG), lambda i, j, k: (i, 0, 0)), acc_shape=(PG, PG), dims=TN)
    dproj0 = _pool_bwd(dpooled, dproj0, S, EMIX)
    gw_pin = _matmul(
        "dwpin", h0, dproj0, grid=(D // TDW, NP0 // TNP, 1), a_spec=pl.BlockSpec((S, TDW), lambda i, j, k: (0, i)),
        b_spec=pl.BlockSpec((S, TNP), lambda i, j, k: (0, j)), out_shape=sds((N_CHIPS, D, SH0), BF16),
        out_spec=pl.BlockSpec((None, TDW, TNP), lambda i, j, k: (j // c0, i, j % c0)), acc_shape=(TDW, TNP), dims=TN)
    gw_grp = dwgrp.reshape(N_POOL_GROUPS, N_CHIPS, PG // N_CHIPS, PG).transpose(1, 0, 2, 3).reshape(N_CHIPS, PG, PG)
    parts_b, landed_b = reduce_in_chip("0b", [gw_pin, gw_grp.astype(BF16)])
    scatter_b, token_b = _split_start("scatter0b_start", parts_b, landed_b, _scatter_plan, 3 * len(parts_b), None)
    dh0 = _matmul(
        "dh0", dproj0, wpin, grid=(S // TM, D // TDW, N_CHIPS), a_spec=pl.BlockSpec((TM, SH0), IK),
        b_spec=pl.BlockSpec((None, TDW, SH0), lambda i, j, k: (k, j, 0)),
        out_shape=sds((S, D), F32), out_spec=pl.BlockSpec((TM, TDW), IJ), acc_shape=(TM, TDW), dims=NT, after=token_b)
    grad_x, _, d_ng0 = _rms_bwd("rms_bwd0", dh0, xs, norm_g[0:1], dx1)
    _, _, d_mng = _rms_bwd("rms_bwd_mem", dmem, ms, mem_norm_g.reshape(1, D), jnp.zeros_like(ms))

    _, landed1 = _split_wait("scatter1_wait", scatter1, _scatter_plan, len(parts1), grad_x)
    _, landed_a = _split_wait("scatter0a_wait", scatter_a, _scatter_plan, len(parts_a), grad_x)
    _, landed_b = _split_wait("scatter0b_wait", scatter_b, _scatter_plan, len(parts_b), grad_x)
    landed = [landed_a[0], landed1[0], landed_a[1], landed1[1], landed_b[0], landed_b[1], landed1[2]]
    fulls = _share_halves([_sum_slots(f"sum_slots{t}", core_chip, p) for t, p in enumerate(landed)])
    f2 = [f.reshape(-1, f.shape[-1]) for f in fulls]
    big_g = [jnp.concatenate(f2[0:2], axis=0), jnp.concatenate(f2[2:4], axis=0), f2[4], f2[5], f2[6]]
    big_names = ["w_kv", "w_out", "pool_w_in", "pool_w_grp", "hgrn_w_in"]
    big_m = [flat(a) for a in (m_w_kv, m_w_out, m_pool_w_in, m_pool_w_grp, m_hgrn_w_in)]
    big_v = [flat(a) for a in (v_w_kv, v_w_out, v_pool_w_in, v_pool_w_grp, v_hgrn_w_in)]
    big_shapes = [w_kv.shape, w_out.shape, pool_w_in.shape, pool_w_grp.shape, hgrn_w_in.shape]
    grads, deltas, new_m, new_v = {}, {}, {}, {}
    for t, n in enumerate(big_names):
        d, mn, vn = _adamw(f"adamw_{n}", shard2d[t], big_g[t], big_m[t], big_v[t])
        grads[n], deltas[n] = big_g[t].reshape(big_shapes[t]), d.reshape(big_shapes[t])
        new_m[n], new_v[n] = mn.reshape(big_shapes[t]), vn.reshape(big_shapes[t])

    Wd = EMIX
    partial = _pack_rows("pack_partials", [d_ng0, d_ng1, d_mng, dcol0[:, :EMIX], dlb, dcol1[:, :EMIX], d_final_g,
                                           loss_part], Wd)
    summed = _small_sum(_allgather_small("allgather_grads", partial), hgrn_lb, 4)
    row = lambda i, n=Wd: summed[i:i + 1, :n]
    nshard = EMIX // N_CHIPS
    g_hg_norm = lax.dynamic_slice_in_dim(row(5), chip * nshard, nshard, axis=1)
    small_names = ["norm_g0", "norm_g1", "mem_norm_g", "pool_scale", "hgrn_lb0", "hgrn_lb1", "hgrn_norm_g", "final_g"]
    small_w = [norm_g[0:1], norm_g[1:2], mem_norm_g.reshape(1, D), pool_scale, hgrn_lb[0:1], hgrn_lb[1:2], hgrn_norm_g,
               final_g.reshape(1, D)]
    small_m = [m_norm_g[0:1], m_norm_g[1:2], m_mem_norm_g.reshape(1, D), m_pool_scale, m_hgrn_lb[0:1], m_hgrn_lb[1:2],
               m_hgrn_norm_g, m_final_g.reshape(1, D)]
    small_v = [v_norm_g[0:1], v_norm_g[1:2], v_mem_norm_g.reshape(1, D), v_pool_scale, v_hgrn_lb[0:1], v_hgrn_lb[1:2],
               v_hgrn_norm_g, v_final_g.reshape(1, D)]
    g_pack = _pack_rows("pack_small_g", [row(0, D), row(1, D), row(2, D), row(3), row(8), row(9), g_hg_norm, row(6, D)], Wd)
    d_pack, m_pack, v_pack = _adamw("adamw_small", _pack_rows("pack_small_w", small_w, Wd), g_pack,
                                    _pack_rows("pack_small_m", small_m, Wd), _pack_rows("pack_small_v", small_v, Wd))
    widths = [v.shape[1] for v in small_w]
    rows = lambda p: {n: p[i, :widths[i]] for i, n in enumerate(small_names)}

    def assemble(r, out):
        out["norm_g"] = jnp.stack([r["norm_g0"], r["norm_g1"]])
        out["mem_norm_g"] = r["mem_norm_g"]
        out["pool_scale"] = r["pool_scale"].reshape(1, EMIX)
        out["hgrn_lb"] = jnp.stack([r["hgrn_lb0"], r["hgrn_lb1"]])
        out["hgrn_norm_g"] = r["hgrn_norm_g"].reshape(1, nshard)
        out["final_g"] = r["final_g"]

    assemble(rows(g_pack), grads)
    assemble(rows(d_pack), deltas)
    assemble(rows(m_pack), new_m)
    assemble(rows(v_pack), new_v)
    loss = summed[7, 0]

    order = ["norm_g", "mem_norm_g", "w_kv", "w_out", "pool_w_in", "pool_w_grp", "pool_scale", "hgrn_w_in", "hgrn_lb",
             "hgrn_norm_g", "final_g"]
    return (loss, grad_x.reshape(1, S, D), *[grads[n] for n in order], *[deltas[n] for n in order],
            *[new_m[n] for n in order], *[new_v[n] for n in order])
```

```python
import functools

import jax
import jax.numpy as jnp
from jax import lax
from jax.experimental import pallas as pl
from jax.experimental.pallas import tpu as pltpu

F32 = jnp.float32
BF16 = jnp.bfloat16
MESH = pl.DeviceIdType.MESH
ANY = pl.BlockSpec(memory_space=pl.ANY)

EPS = 1e-6
HG_HEAD_DIM = 128
HG_CHUNK = 64
CA_HEADS = 4
N_POOL_GROUPS = 4
POOL_HALO = 128
ADAM_LR = 0.001
ADAM_B1 = 0.9
ADAM_B2 = 0.999
ADAM_EPS = 1e-08
ADAM_WD = 0.01
ADAM_STEP = 10
N_CHIPS = 4
N_DEV = 8
VMEM_LIMIT_BYTES = 56 * 1024 * 1024
SMALL_ROWS = 8
STREAM_CHUNK_BYTES = 2 * 1024 * 1024
STREAM_SLOTS = 3


def _params(*sem):
    return pltpu.CompilerParams(dimension_semantics=sem, vmem_limit_bytes=VMEM_LIMIT_BYTES)


def _tile(n, pref):
    t = pref
    while n % t:
        t //= 2
    return t


def _sigmoid(x):
    return 1.0 / (1.0 + jnp.exp(-x))


def _matmul(name, a, b, *, grid, a_spec, b_spec, out_shape, out_spec, acc_shape, dims,
            add=None, add_spec=None, alias=None, after=None):
    nk = grid[2]
    has_add = add is not None
    has_alias = alias is not None
    has_after = after is not None

    def body(*refs):
        a_ref, b_ref = refs[0], refs[1]
        pos = 2
        add_ref = None
        if has_add:
            add_ref = refs[pos]
            pos += 1
        pos += has_alias + has_after
        o_ref = refs[pos]
        prod = lax.dot_general(a_ref[...], b_ref[...], (dims, ((), ())), preferred_element_type=F32)

        def finish(r):
            if has_add:
                r = r + add_ref[...].astype(F32)
            o_ref[...] = r.astype(o_ref.dtype)

        if nk == 1:
            finish(prod)
            return
        acc_ref = refs[pos + 1]
        k = pl.program_id(2)

        @pl.when(k == 0)
        def _():
            acc_ref[...] = prod

        @pl.when(k > 0)
        def _():
            acc_ref[...] += prod

        @pl.when(k == nk - 1)
        def _():
            finish(acc_ref[...])

    operands = [a, b]
    in_specs = [a_spec, b_spec]
    if has_add:
        operands.append(add)
        in_specs.append(add_spec)
    aliases = {}
    if has_alias:
        aliases = {len(operands): 0}
        operands.append(alias)
        in_specs.append(ANY)
    if has_after:
        operands.append(after)
        in_specs.append(ANY)
    return pl.pallas_call(
        body, name=name, grid=grid, in_specs=in_specs, out_specs=out_spec, out_shape=out_shape,
        scratch_shapes=[pltpu.VMEM(acc_shape, F32)] if nk > 1 else [], input_output_aliases=aliases,
        compiler_params=_params("parallel", "parallel", "arbitrary"),
    )(*operands)


IJ = lambda i, j, k: (i, j)
IK = lambda i, j, k: (i, k)
KJ = lambda i, j, k: (k, j)
KI = lambda i, j, k: (k, i)
NN = ((1,), (0,))
NT = ((1,), (1,))
TN = ((0,), (0,))


def _rms_fwd(name, x, g):
    R, D = x.shape
    tr = _tile(R, 256)

    def body(x_ref, g_ref, o_ref):
        xf = x_ref[...]
        r = lax.rsqrt(jnp.mean(xf * xf, axis=-1, keepdims=True) + EPS)
        o_ref[...] = (xf * r * g_ref[...]).astype(o_ref.dtype)

    return pl.pallas_call(
        body, name=name, grid=(R // tr,),
        in_specs=[pl.BlockSpec((tr, D), lambda i: (i, 0)), pl.BlockSpec((1, D), lambda i: (0, 0))],
        out_specs=pl.BlockSpec((tr, D), lambda i: (i, 0)),
        out_shape=jax.ShapeDtypeStruct((R, D), BF16), compiler_params=_params("parallel"),
    )(x, g)


def _rms_bwd(name, dh, x, g, dres):
    R, D = x.shape
    tr = _tile(R, 256)

    def body(dh_ref, x_ref, g_ref, dres_ref, dx_ref, dxb_ref, dg_ref):
        xf = x_ref[...]
        r = lax.rsqrt(jnp.mean(xf * xf, axis=-1, keepdims=True) + EPS)
        xn = xf * r
        d = dh_ref[...]
        dyg = d * g_ref[...]
        dx = r * (dyg - xn * jnp.mean(dyg * xn, axis=-1, keepdims=True)) + dres_ref[...]
        dx_ref[...] = dx
        dxb_ref[...] = dx.astype(BF16)

        @pl.when(pl.program_id(0) == 0)
        def _():
            dg_ref[...] = jnp.zeros_like(dg_ref)

        dg_ref[...] += jnp.sum(d * xn, axis=0, keepdims=True)

    row = pl.BlockSpec((tr, D), lambda i: (i, 0))
    vec = pl.BlockSpec((1, D), lambda i: (0, 0))
    return pl.pallas_call(
        body, name=name, grid=(R // tr,), in_specs=[row, row, vec, row], out_specs=[row, row, vec],
        out_shape=[jax.ShapeDtypeStruct((R, D), F32), jax.ShapeDtypeStruct((R, D), BF16),
                   jax.ShapeDtypeStruct((1, D), F32)],
        compiler_params=_params("arbitrary"),
    )(dh, x, g, dres)


def _loss_head(x2, g, target):
    R, D = x2.shape
    tr = _tile(R, 256)

    def body(x_ref, g_ref, t_ref, dx_ref, dxb_ref, dg_ref, loss_ref):
        xf = x_ref[...]
        gg = g_ref[...]
        r = lax.rsqrt(jnp.mean(xf * xf, axis=-1, keepdims=True) + EPS)
        xn = xf * r
        e = xn * gg - t_ref[...]
        part = 0.5 * jnp.sum(jnp.mean(e * e, axis=-1, keepdims=True), axis=0, keepdims=True)
        dy = e * (1.0 / D)
        dyg = dy * gg
        dx = r * (dyg - xn * jnp.mean(dyg * xn, axis=-1, keepdims=True))
        dx_ref[...] = dx
        dxb_ref[...] = dx.astype(BF16)

        @pl.when(pl.program_id(0) == 0)
        def _():
            dg_ref[...] = jnp.zeros_like(dg_ref)
            loss_ref[...] = jnp.zeros_like(loss_ref)

        dg_ref[...] += jnp.sum(dy * xn, axis=0, keepdims=True)
        loss_ref[...] += jnp.broadcast_to(part, loss_ref.shape)

    row = pl.BlockSpec((tr, D), lambda i: (i, 0))
    vec = pl.BlockSpec((1, D), lambda i: (0, 0))
    return pl.pallas_call(
        body, name="loss_head", grid=(R // tr,), in_specs=[row, vec, row],
        out_specs=[row, row, vec, pl.BlockSpec((1, 128), lambda i: (0, 0))],
        out_shape=[jax.ShapeDtypeStruct((R, D), F32), jax.ShapeDtypeStruct((R, D), BF16),
                   jax.ShapeDtypeStruct((1, D), F32), jax.ShapeDtypeStruct((1, 128), F32)],
        compiler_params=_params("arbitrary"),
    )(x2, g, target)


def _pool_band(tr, reverse, w):
    r = lax.broadcasted_iota(jnp.int32, (tr, tr + POOL_HALO), 0)
    c = lax.broadcasted_iota(jnp.int32, (tr, tr + POOL_HALO), 1)
    if reverse:
        inside = (c >= r) & (c < r + w)
    else:
        cc = c - POOL_HALO
        inside = (cc <= r) & (cc > r - w)
    return jnp.where(inside, 1.0, 0.0).astype(BF16)


def _pool_fwd(proj, S, EMIX):
    PG = EMIX // N_POOL_GROUPS
    cb = PG
    tr = _tile(S, 512)
    per_group = PG // cb

    def body(u_ref, o_ref, ext):
        i = pl.program_id(1)
        w = jnp.left_shift(2, pl.program_id(0) // per_group)

        @pl.when(i == 0)
        def _():
            ext[0:POOL_HALO, :] = jnp.zeros((POOL_HALO, cb), BF16)

        u = u_ref[...]
        ext[POOL_HALO:, :] = u
        win = jnp.dot(_pool_band(tr, False, w), ext[...], preferred_element_type=F32)
        pos = i * tr + lax.broadcasted_iota(jnp.int32, (tr, 1), 0)
        cnt = jnp.minimum(pos + 1, w).astype(F32)
        o_ref[...] = (win / cnt - u.astype(F32)).astype(BF16)
        ext[0:POOL_HALO, :] = u[tr - POOL_HALO:, :]

    return pl.pallas_call(
        body, name="pool_fwd", grid=(EMIX // cb, S // tr),
        in_specs=[pl.BlockSpec((tr, cb), lambda j, i: (i, j))],
        out_specs=pl.BlockSpec((tr, cb), lambda j, i: (i, j)),
        out_shape=jax.ShapeDtypeStruct((S, EMIX), BF16),
        scratch_shapes=[pltpu.VMEM((tr + POOL_HALO, cb), BF16)],
        compiler_params=_params("parallel", "arbitrary"),
    )(proj)


def _pool_bwd(dpooled, dproj, S, EMIX):
    PG = EMIX // N_POOL_GROUPS
    cb = PG
    tr = _tile(S, 512)
    per_group = PG // cb
    nrt = S // tr

    def body(d_ref, _, o_ref, ext):
        step = pl.program_id(1)
        i = nrt - 1 - step
        w = jnp.left_shift(2, pl.program_id(0) // per_group)

        @pl.when(step == 0)
        def _():
            ext[tr:, :] = jnp.zeros((POOL_HALO, cb), BF16)

        d = d_ref[...]
        pos = i * tr + lax.broadcasted_iota(jnp.int32, (tr, 1), 0)
        cnt = jnp.minimum(pos + 1, w).astype(F32)
        z = (d / cnt).astype(BF16)
        ext[0:tr, :] = z
        win = jnp.dot(_pool_band(tr, True, w), ext[...], preferred_element_type=F32)
        o_ref[...] = (win - d).astype(BF16)
        ext[tr:, :] = z[0:POOL_HALO, :]

    return pl.pallas_call(
        body, name="pool_bwd", grid=(EMIX // cb, nrt),
        in_specs=[pl.BlockSpec((tr, cb), lambda j, s: (nrt - 1 - s, j)), ANY],
        out_specs=pl.BlockSpec((tr, cb), lambda j, s: (nrt - 1 - s, j)),
        out_shape=jax.ShapeDtypeStruct(dproj.shape, dproj.dtype),
        scratch_shapes=[pltpu.VMEM((tr + POOL_HALO, cb), BF16)],
        input_output_aliases={1: 0},
        compiler_params=_params("parallel", "arbitrary"),
    )(dpooled, dproj)


def _ca_fwd(name, proj, qblk, kv, premix, S, ECA, EMIX):
    M = kv.shape[0]
    hd = ECA // CA_HEADS
    ts = _tile(S, 512)
    scale = hd ** -0.5

    def body(q_ref, kv_ref, _, o_ref):
        for h in range(CA_HEADS):
            q = q_ref[:, h * hd:(h + 1) * hd]
            k = kv_ref[:, h * hd:(h + 1) * hd]
            v = kv_ref[:, ECA + h * hd:ECA + (h + 1) * hd]
            s = lax.dot_general(q, k, (NT, ((), ())), preferred_element_type=F32) * scale
            s = s - jnp.max(s, axis=-1, keepdims=True)
            p = jnp.exp(s)
            p = p / jnp.sum(p, axis=-1, keepdims=True)
            o = jnp.dot(p.astype(BF16), v, preferred_element_type=F32)
            o_ref[:, h * hd:(h + 1) * hd] = o.astype(BF16)

    return pl.pallas_call(
        body, name=name, grid=(S // ts,),
        in_specs=[pl.BlockSpec((ts, ECA), lambda i: (i, qblk)), pl.BlockSpec((M, 2 * ECA), lambda i: (0, 0)), ANY],
        out_specs=pl.BlockSpec((ts, ECA), lambda i: (i, EMIX // ECA)),
        out_shape=jax.ShapeDtypeStruct(premix.shape, premix.dtype),
        input_output_aliases={2: 0}, compiler_params=_params("parallel"),
    )(proj, kv, premix)


def _ca_bwd(name, dpremix, proj, qblk, kv, dbuf, dblk, S, ECA, EMIX):
    M = kv.shape[0]
    hd = ECA // CA_HEADS
    ts = _tile(S, 512)
    scale = hd ** -0.5

    def body(do_ref, q_ref, kv_ref, _, dq_ref, dkv_ref):
        @pl.when(pl.program_id(0) == 0)
        def _():
            dkv_ref[...] = jnp.zeros_like(dkv_ref)

        for h in range(CA_HEADS):
            lo, hi = h * hd, (h + 1) * hd
            q = q_ref[:, lo:hi]
            k = kv_ref[:, lo:hi]
            v = kv_ref[:, ECA + lo:ECA + hi]
            do = do_ref[:, lo:hi]
            s = lax.dot_general(q, k, (NT, ((), ())), preferred_element_type=F32) * scale
            s = s - jnp.max(s, axis=-1, keepdims=True)
            p = jnp.exp(s)
            p = p / jnp.sum(p, axis=-1, keepdims=True)
            pb = p.astype(BF16)
            dkv_ref[:, ECA + lo:ECA + hi] += lax.dot_general(pb, do, (TN, ((), ())), preferred_element_type=F32)
            dp = lax.dot_general(do, v, (NT, ((), ())), preferred_element_type=F32)
            ds = (p * (dp - jnp.sum(p * dp, axis=-1, keepdims=True)) * scale).astype(BF16)
            dq_ref[:, lo:hi] = jnp.dot(ds, k, preferred_element_type=F32).astype(BF16)
            dkv_ref[:, lo:hi] += lax.dot_general(ds, q, (TN, ((), ())), preferred_element_type=F32)

    return pl.pallas_call(
        body, name=name, grid=(S // ts,),
        in_specs=[pl.BlockSpec((ts, ECA), lambda i: (i, EMIX // ECA)), pl.BlockSpec((ts, ECA), lambda i: (i, qblk)),
                  pl.BlockSpec((M, 2 * ECA), lambda i: (0, 0)), ANY],
        out_specs=[pl.BlockSpec((ts, ECA), lambda i: (i, dblk)), pl.BlockSpec((M, 2 * ECA), lambda i: (0, 0))],
        out_shape=[jax.ShapeDtypeStruct(dbuf.shape, dbuf.dtype), jax.ShapeDtypeStruct((M, 2 * ECA), F32)],
        input_output_aliases={3: 0}, compiler_params=_params("arbitrary"),
    )(dpremix, proj, kv, dbuf)


def _gate_fwd(name, premix, proj, gblk, colscale, S, EB, ECA):
    ts = _tile(S, 512)

    def body(p_ref, g_ref, c_ref, o_ref):
        g = g_ref[...].astype(F32)
        o_ref[...] = (p_ref[...].astype(F32) * c_ref[...] * (g * _sigmoid(g))).astype(BF16)

    return pl.pallas_call(
        body, name=name, grid=(S // ts, EB // ECA),
        in_specs=[pl.BlockSpec((ts, ECA), lambda i, j: (i, j)), pl.BlockSpec((ts, ECA), lambda i, j: (i, gblk + j)),
                  pl.BlockSpec((1, ECA), lambda i, j: (0, j))],
        out_specs=pl.BlockSpec((ts, ECA), lambda i, j: (i, j)),
        out_shape=jax.ShapeDtypeStruct((S, EB), BF16), compiler_params=_params("parallel", "parallel"),
    )(premix, proj, colscale)


def _gate_bwd(name, dbranch, premix, proj, gblk, colscale, dshape, dblk, S, EB, ECA):
    ts = _tile(S, 512)

    def body(db_ref, p_ref, g_ref, c_ref, dp_ref, dg_ref, dc_ref):
        g = g_ref[...].astype(F32)
        sg = _sigmoid(g)
        si = g * sg
        c = c_ref[...]
        db = db_ref[...].astype(F32)
        t = db * p_ref[...].astype(F32)
        dp_ref[...] = (db * si * c).astype(BF16)
        dg_ref[...] = (t * c * (sg * (1.0 + g * (1.0 - sg)))).astype(BF16)

        @pl.when(pl.program_id(1) == 0)
        def _():
            dc_ref[...] = jnp.zeros_like(dc_ref)

        dc_ref[...] += jnp.sum(t * si, axis=0, keepdims=True)

    blk = pl.BlockSpec((ts, ECA), lambda j, i: (i, j))
    vec = pl.BlockSpec((1, ECA), lambda j, i: (0, j))
    return pl.pallas_call(
        body, name=name, grid=(EB // ECA, S // ts),
        in_specs=[blk, blk, pl.BlockSpec((ts, ECA), lambda j, i: (i, gblk + j)), vec],
        out_specs=[blk, pl.BlockSpec((ts, ECA), lambda j, i: (i, dblk + j)), vec],
        out_shape=[jax.ShapeDtypeStruct((S, EB), BF16), jax.ShapeDtypeStruct(dshape, BF16),
                   jax.ShapeDtypeStruct((1, EB), F32)],
        compiler_params=_params("parallel", "arbitrary"),
    )(dbranch, premix, proj, colscale)


def _hgrn_lb(lb_ref):
    l0 = lb_ref[0:1, :]
    l1 = lb_ref[1:2, :]
    mx = jnp.maximum(l0, l1)
    e0 = jnp.exp(l0 - mx)
    e1 = jnp.exp(l1 - mx)
    return e1 / (e0 + e1)


def _bdot(a, b, ca, cb):
    return lax.dot_general(a, b, (((ca,), (cb,)), ((0,), (0,))), preferred_element_type=F32)


def _tri_sum(tri, x):
    hi = x.astype(BF16)
    lo = (x - hi.astype(F32)).astype(BF16)
    tri = tri.astype(BF16)
    return _bdot(tri, hi, 2, 1) + _bdot(tri, lo, 2, 1)


def _hgrn_chunks(qin, fin, lbh, n):
    C = HG_CHUNK
    row = lax.broadcasted_iota(jnp.int32, (n, C, C), 1)
    col = lax.broadcasted_iota(jnp.int32, (n, C, C), 2)
    causal = row >= col
    sg = _sigmoid(fin)
    f = lbh + (1.0 - lbh) * sg
    k = 1.0 - f
    g = jnp.log(f)
    b = _tri_sum(jnp.where(causal, 1.0, 0.0), g)
    b_last = jnp.sum(g, axis=1, keepdims=True)
    eb = jnp.exp(b)
    einv = jnp.exp(-b)
    eend = jnp.exp(b_last - b)
    sq = _sigmoid(qin)
    a = qin * sq * (HG_HEAD_DIM ** -0.5) * eb
    bm = k * einv
    e = k * eend
    d = jnp.exp(b_last)
    p = jnp.where(causal, _bdot(a.astype(BF16), bm.astype(BF16), 2, 2), 0.0)
    return dict(causal=causal, sg=sg, f=f, eb=eb, einv=einv, eend=eend, sq=sq, a=a, bm=bm, e=e, d=d, p=p)


def _hgrn_fwd(proj, fgate, hgrn_lb, S, EMIX, EB):
    HD, C = HG_HEAD_DIM, HG_CHUNK
    HH = EMIX // HD
    hb = 2 if HH % 2 == 0 else 1
    W = hb * HD
    tr = _tile(S, 512)
    n = tr // C

    def body(q_ref, f_ref, i_ref, lb_ref, o_ref, rstd_ref, st_ref, state):
        @pl.when(pl.program_id(1) == 0)
        def _():
            state[...] = jnp.zeros_like(state)

        lb = _hgrn_lb(lb_ref)
        for h in range(hb):
            cs = slice(h * HD, (h + 1) * HD)
            qin = q_ref[:, cs].astype(F32).reshape(n, C, HD)
            fin = f_ref[:, cs].reshape(n, C, HD)
            v = i_ref[:, cs].reshape(n, C, HD)
            t = _hgrn_chunks(qin, fin, lb[:, cs], n)
            upd = _bdot(v, t["e"].astype(BF16), 1, 1)
            st = state[h]
            for c in range(n):
                st_ref[h, c] = st
                st = st * t["d"][c] + upd[c]
            state[h] = st
            o = _bdot(t["p"].astype(BF16), v, 2, 1) + _bdot(t["a"].astype(BF16), st_ref[h].astype(BF16), 2, 2)
            rstd = lax.rsqrt(jnp.mean(o * o, axis=-1, keepdims=True) + EPS)
            o_ref[:, cs] = (o * rstd).reshape(tr, HD).astype(BF16)
            rstd_ref[:, cs] = jnp.broadcast_to(rstd, (n, C, HD)).reshape(tr, HD)

    blk = lambda off: pl.BlockSpec((tr, W), lambda g, i: (i, off + g))
    return pl.pallas_call(
        body, name="hgrn_fwd", grid=(HH // hb, S // tr),
        in_specs=[blk(0), blk(0), blk(2 * EMIX // W), pl.BlockSpec((2, W), lambda g, i: (0, g))],
        out_specs=[blk(0), blk(0), pl.BlockSpec((hb, n, HD, HD), lambda g, i: (g, i, 0, 0))],
        out_shape=[jax.ShapeDtypeStruct((S, EB), BF16), jax.ShapeDtypeStruct((S, EMIX), F32),
                   jax.ShapeDtypeStruct((HH, S // C, HD, HD), F32)],
        scratch_shapes=[pltpu.VMEM((hb, HD, HD), F32)],
        compiler_params=_params("parallel", "arbitrary"),
    )(proj, fgate, proj, hgrn_lb)


def _hgrn_bwd(dpremix, premix, rstd, states, proj, fgate, hgrn_lb, S, EMIX):
    HD, C = HG_HEAD_DIM, HG_CHUNK
    HH = EMIX // HD
    hb = 2 if HH % 2 == 0 else 1
    W = hb * HD
    tr = _tile(S, 512)
    n = tr // C
    nrt = S // tr

    def body(do_ref, on_ref, rstd_ref, st_ref, q_ref, f_ref, i_ref, lb_ref, d_ref, dlb_ref, dstate, dsbuf):
        @pl.when(pl.program_id(1) == 0)
        def _():
            dstate[...] = jnp.zeros_like(dstate)
            dlb_ref[...] = jnp.zeros_like(dlb_ref)

        lb = _hgrn_lb(lb_ref)
        for h in range(hb):
            cs = slice(h * HD, (h + 1) * HD)
            qin = q_ref[:, cs].astype(F32).reshape(n, C, HD)
            fin = f_ref[:, cs].reshape(n, C, HD)
            v = i_ref[:, cs].reshape(n, C, HD)
            lbh = lb[:, cs]
            t = _hgrn_chunks(qin, fin, lbh, n)
            a, bm, e, d, p = t["a"], t["bm"], t["e"], t["d"], t["p"]
            ab, bmb, eb16 = a.astype(BF16), bm.astype(BF16), e.astype(BF16)
            on = on_ref[:, cs].astype(F32).reshape(n, C, HD)
            dn = do_ref[:, cs].astype(F32).reshape(n, C, HD)
            do = rstd_ref[:, cs].reshape(n, C, HD) * (dn - on * jnp.mean(dn * on, axis=-1, keepdims=True))
            dob = do.astype(BF16)
            grow = _bdot(dob, ab, 1, 1)
            ds = dstate[h]
            for c in reversed(range(n)):
                dsbuf[h, c] = ds
                ds = ds * d[c] + grow[c]
            dstate[h] = ds
            dst = dsbuf[h]
            st = st_ref[h]
            dstb = dst.astype(BF16)
            dp = jnp.where(t["causal"], _bdot(dob, v, 2, 2), 0.0).astype(BF16)
            dv = _bdot(p.astype(BF16), dob, 1, 1) + _bdot(eb16, dstb, 2, 2)
            da = _bdot(dp, bmb, 2, 1) + _bdot(dob, st.astype(BF16), 2, 1)
            dbm = _bdot(dp, ab, 1, 1)
            de = _bdot(v, dstb, 2, 1)
            dd = jnp.sum(dst * st, axis=1, keepdims=True)
            dk = dbm * t["einv"] + de * t["eend"]
            dee = de * e
            db = da * a - dbm * bm - dee
            extra = jnp.sum(dee, axis=1, keepdims=True) + dd * d
            upper = jnp.where(lax.broadcasted_iota(jnp.int32, (n, C, C), 2)
                              >= lax.broadcasted_iota(jnp.int32, (n, C, C), 1), 1.0, 0.0)
            dg = _tri_sum(upper, db) + extra
            df = dg / t["f"] - dk
            sg, sq = t["sg"], t["sq"]
            dq = da * t["eb"] * (HD ** -0.5) * (sq * (1.0 + qin * (1.0 - sq)))
            d_ref[0, :, cs] = dq.reshape(tr, HD).astype(BF16)
            d_ref[1, :, cs] = (df * (1.0 - lbh) * sg * (1.0 - sg)).reshape(tr, HD).astype(BF16)
            d_ref[2, :, cs] = dv.reshape(tr, HD).astype(BF16)
            dlb_ref[:, cs] += jnp.sum((df * (1.0 - sg)).reshape(tr, HD), axis=0, keepdims=True)

    rev = lambda off: pl.BlockSpec((tr, W), lambda g, s: (nrt - 1 - s, off + g))
    return pl.pallas_call(
        body, name="hgrn_bwd", grid=(HH // hb, nrt),
        in_specs=[rev(0), rev(0), rev(0), pl.BlockSpec((hb, n, HD, HD), lambda g, s: (g, nrt - 1 - s, 0, 0)),
                  rev(0), rev(0), rev(2 * EMIX // W), pl.BlockSpec((2, W), lambda g, s: (0, g))],
        out_specs=[pl.BlockSpec((3, tr, W), lambda g, s: (0, nrt - 1 - s, g)), pl.BlockSpec((1, W), lambda g, s: (0, g))],
        out_shape=[jax.ShapeDtypeStruct((3, S, EMIX), BF16), jax.ShapeDtypeStruct((1, EMIX), F32)],
        scratch_shapes=[pltpu.VMEM((hb, HD, HD), F32), pltpu.VMEM((hb, n, HD, HD), F32)],
        compiler_params=_params("parallel", "arbitrary"),
    )(dpremix, premix, rstd, states, proj, fgate, proj, hgrn_lb)


EW_BLOCK_ELEMS = 512 * 1024


def _ew_tiles(R, C):
    tc = C if C <= 4096 else _tile(C, 2048)
    tr = _tile(R, 512)
    while tr * tc > EW_BLOCK_ELEMS and tr % 16 == 0:
        tr //= 2
    return tr, tc


def _add_halves(name, core_chip, grad, got):
    _, _, R, C = grad.shape
    tr, tc = _ew_tiles(R, C)

    def body(c_ref, a_ref, b_ref, o_ref, own_ref):
        r = (a_ref[...].astype(F32) + b_ref[...].astype(F32)).astype(BF16)
        o_ref[...] = r

        @pl.when(pl.program_id(2) == c_ref[1])
        def _():
            own_ref[...] = r

    blk = pl.BlockSpec((None, tr, tc), lambda i, j, s, c: (s, i, j))
    sds = jax.ShapeDtypeStruct(got.shape, BF16)
    return pl.pallas_call(
        body, name=name, out_shape=[sds, sds],
        grid_spec=pltpu.PrefetchScalarGridSpec(
            num_scalar_prefetch=1, grid=(R // tr, C // tc, N_CHIPS),
            in_specs=[pl.BlockSpec((None, None, tr, tc), lambda i, j, s, c: (s, c[0], i, j)), blk],
            out_specs=[blk, pl.BlockSpec((None, tr, tc), lambda i, j, s, c: (c[1], i, j))]),
        compiler_params=_params("parallel", "parallel", "arbitrary"),
    )(core_chip, grad, got)


def _sum_slots(name, core, parts):
    _, R, C = parts.shape
    tr, tc = _ew_tiles(R, C)

    def body(c_ref, p_ref, o_ref):
        acc = p_ref[0].astype(F32)
        for s in range(1, N_CHIPS):
            acc = acc + p_ref[s].astype(F32)
        o_ref[...] = acc

    return pl.pallas_call(
        body, name=name, out_shape=jax.ShapeDtypeStruct((2, R, C), F32),
        grid_spec=pltpu.PrefetchScalarGridSpec(
            num_scalar_prefetch=1, grid=(R // tr, C // tc),
            in_specs=[pl.BlockSpec((N_CHIPS, tr, tc), lambda i, j, c: (0, i, j))],
            out_specs=pl.BlockSpec((None, tr, tc), lambda i, j, c: (c[0], i, j))),
        compiler_params=_params("parallel", "parallel"),
    )(core, parts)


def _adamw(name, w, g, m, v):
    R, C = w.shape
    tr, tc = _ew_tiles(R, C)

    def body(w_ref, g_ref, m_ref, v_ref, d_ref, mo_ref, vo_ref):
        g = g_ref[...]
        mn = ADAM_B1 * m_ref[...] + (1.0 - ADAM_B1) * g
        vn = ADAM_B2 * v_ref[...] + (1.0 - ADAM_B2) * (g * g)
        m_hat = mn / (1.0 - ADAM_B1 ** ADAM_STEP)
        v_hat = vn / (1.0 - ADAM_B2 ** ADAM_STEP)
        d_ref[...] = -ADAM_LR * (m_hat / (jnp.sqrt(v_hat) + ADAM_EPS) + ADAM_WD * w_ref[...])
        mo_ref[...] = mn
        vo_ref[...] = vn

    blk = pl.BlockSpec((tr, tc), lambda i, j: (i, j))
    sds = jax.ShapeDtypeStruct((R, C), F32)
    return pl.pallas_call(
        body, name=name, grid=(R // tr, C // tc), in_specs=[blk] * 4, out_specs=[blk] * 3, out_shape=[sds] * 3,
        compiler_params=_params("parallel", "parallel"),
    )(w, g, m, v)


def _pack_rows(name, vecs, W):
    nv = len(vecs)

    def body(*refs):
        o_ref = refs[nv]
        o_ref[...] = jnp.zeros_like(o_ref)
        for i in range(nv):
            o_ref[i:i + 1, 0:vecs[i].shape[1]] = refs[i][...]

    vm = pl.BlockSpec(memory_space=pltpu.VMEM)
    return pl.pallas_call(
        body, name=name, in_specs=[vm] * nv, out_specs=vm, out_shape=jax.ShapeDtypeStruct((SMALL_ROWS, W), F32),
    )(*vecs)


def _small_sum(gathered, hgrn_lb, lb_row):
    _, T, W = gathered.shape

    def body(g_ref, lb_ref, o_ref):
        acc = g_ref[0]
        for dev in range(1, N_DEV):
            acc = acc + g_ref[dev]
        o_ref[0:T, :] = acc
        lb = _hgrn_lb(lb_ref)
        d1 = o_ref[lb_row:lb_row + 1, :] * (lb * (1.0 - lb))
        o_ref[T:2 * T, :] = jnp.zeros((T, W), F32)
        o_ref[T:T + 1, :] = -d1
        o_ref[T + 1:T + 2, :] = d1

    vm = pl.BlockSpec(memory_space=pltpu.VMEM)
    return pl.pallas_call(
        body, name="small_sum", in_specs=[vm, vm], out_specs=vm, out_shape=jax.ShapeDtypeStruct((2 * T, W), F32),
    )(gathered, hgrn_lb)


def _place():
    return lax.axis_index("x"), lax.axis_index("y"), lax.axis_index("c")


def _other_chips(x, y):
    return [(1 - x, y), (x, 1 - y), (1 - x, 1 - y)]


def _copies_now(name, srcs, dst_shapes, plan, ncopies):
    def body(*refs):
        ns = len(srcs)
        send, recv = refs[ns + len(dst_shapes):]
        copies = []
        for i, (src, dst, dev) in enumerate(plan(refs[:ns], refs[ns:ns + len(dst_shapes)])):
            cp = pltpu.make_async_remote_copy(src_ref=src, dst_ref=dst, send_sem=send.at[i], recv_sem=recv.at[i],
                                              device_id=dev, device_id_type=MESH)
            cp.start()
            copies.append(cp)
        for cp in copies:
            cp.wait()

    return pl.pallas_call(
        body, name=name, in_specs=[ANY] * len(srcs), out_specs=[ANY] * len(dst_shapes), out_shape=dst_shapes,
        scratch_shapes=[pltpu.SemaphoreType.DMA((ncopies,))] * 2,
    )(*srcs)


def _chunk_rows(rows, row_bytes):
    cr = rows
    while cr * row_bytes > STREAM_CHUNK_BYTES and cr % 32 == 0:
        cr //= 2
    return cr


def _stream(pairs, buf, sems, t, peer):
    lsem, ssem, rsem = sems
    n = len(pairs)
    loads, sent = [None] * n, [None] * n

    def load(k):
        slot = k % STREAM_SLOTS
        if k >= STREAM_SLOTS:
            sent[k - STREAM_SLOTS]()
        loads[k] = pltpu.make_async_copy(pairs[k][0], buf.at[slot], lsem.at[t, slot])
        loads[k].start()

    load(0)
    for k in range(n):
        slot = k % STREAM_SLOTS
        if k + 1 < n:
            load(k + 1)
        loads[k].wait()
        if peer is None:
            cp = pltpu.make_async_copy(buf.at[slot], pairs[k][1], ssem.at[t, slot])
            cp.start()
            sent[k] = cp.wait
        else:
            cp = pltpu.make_async_remote_copy(src_ref=buf.at[slot], dst_ref=pairs[k][1], send_sem=ssem.at[t, slot],
                                              recv_sem=rsem.at[t], device_id=peer, device_id_type=MESH)
            cp.start()
            sent[k] = cp.wait_send
    for k in range(max(0, n - STREAM_SLOTS), n):
        sent[k]()


def _stream_scratch(shapes):
    nt = len(shapes)
    return ([pltpu.VMEM((STREAM_SLOTS,) + s, d) for s, d in shapes]
            + [pltpu.SemaphoreType.DMA((nt, STREAM_SLOTS)), pltpu.SemaphoreType.DMA((nt, STREAM_SLOTS)),
               pltpu.SemaphoreType.DMA((nt,))])


def _exchange_halves(name, grads):
    nt = len(grads)
    hs = [g.shape[1] // 2 for g in grads]
    crs = [_chunk_rows(h, g.shape[2] * g.dtype.itemsize) for h, g in zip(hs, grads)]

    def body(*refs):
        ins, gots, bufs, sems = refs[:nt], refs[nt:2 * nt], refs[2 * nt:3 * nt], refs[3 * nt:]
        x, y, c = _place()
        sib = (x, y, 1 - c)
        for t in range(nt):
            h, cr = hs[t], crs[t]
            pairs = [(ins[t].at[b, pl.ds((1 - c) * h + r0, cr)], gots[t].at[b, pl.ds(r0, cr)])
                     for b in range(N_CHIPS) for r0 in range(0, h, cr)]
            _stream(pairs, bufs[t], sems, t, sib)
        for t in range(nt):
            pltpu.make_async_remote_copy(src_ref=gots[t], dst_ref=gots[t], send_sem=sems[1].at[t, 0],
                                         recv_sem=sems[2].at[t], device_id=sib, device_id_type=MESH).wait_recv()

    return pl.pallas_call(
        body, name=name, in_specs=[ANY] * nt, out_specs=[ANY] * nt,
        out_shape=[jax.ShapeDtypeStruct((N_CHIPS, h, g.shape[2]), g.dtype) for h, g in zip(hs, grads)],
        scratch_shapes=_stream_scratch([((cr, g.shape[2]), g.dtype) for cr, g in zip(crs, grads)]),
        compiler_params=pltpu.CompilerParams(vmem_limit_bytes=VMEM_LIMIT_BYTES),
    )(*grads)


def _scatter_plan(srcs, dsts):
    x, y, c = _place()
    me = 2 * x + y
    return [(srcs[t].at[2 * px + py], dsts[t].at[me], (px, py, c))
            for t in range(len(srcs)) for px, py in _other_chips(x, y)]


def _slot(dst, chip, r0, rows, cols):
    if len(dst.shape) == 3:
        return dst.at[chip, pl.ds(r0, rows)]
    return dst.at[pl.ds(r0, rows), pl.ds(pl.multiple_of(chip * cols, 128), cols)]


def _gather_plan(srcs, dsts):
    x, y, c = _place()
    me = 2 * x + y
    plan = []
    for t in range(len(srcs)):
        h, cols = srcs[t].shape[0] // 2, srcs[t].shape[1]
        plan += [(srcs[t].at[pl.ds(c * h, h)], _slot(dsts[t], me, c * h, h, cols), (px, py, c))
                 for px, py in _other_chips(x, y)]
    return plan


HBM_SPEC = pl.BlockSpec(memory_space=pltpu.HBM)
SEM_SPEC = pl.BlockSpec(memory_space=pltpu.SEMAPHORE)


def _split_start(name, srcs, dsts, plan, ncopies, after):
    bufs = [pltpu.with_memory_space_constraint(a, pltpu.HBM) for a in list(srcs) + list(dsts)]
    nb, ns = len(bufs), len(srcs)
    operands = bufs + ([after] if after is not None else [])

    def body(*refs):
        outs = refs[len(operands):]
        send, recv, token = outs[0], outs[1], outs[-1]
        for i, (src, dst, dev) in enumerate(plan(refs[:ns], refs[ns:nb])):
            pltpu.make_async_remote_copy(src_ref=src, dst_ref=dst, send_sem=send.at[i], recv_sem=recv.at[i],
                                         device_id=dev, device_id_type=MESH).start()
        token[...] = jnp.zeros_like(token)

    res = pl.pallas_call(
        body, name=name,
        out_shape=[pltpu.SemaphoreType.DMA((ncopies,)), pltpu.SemaphoreType.DMA((ncopies,))]
        + [pltpu.HBM(a.shape, a.dtype) for a in bufs] + [jax.ShapeDtypeStruct((8, 128), F32)],
        in_specs=[HBM_SPEC] * nb + [ANY] * (len(operands) - nb),
        out_specs=[SEM_SPEC, SEM_SPEC] + [HBM_SPEC] * nb + [pl.BlockSpec(memory_space=pltpu.VMEM)],
        input_output_aliases={i: 2 + i for i in range(nb)},
        compiler_params=pltpu.CompilerParams(has_side_effects=pltpu.SideEffectType.DATAFLOW_SIDE_EFFECTING),
    )(*operands)
    return res[:-1], res[-1]


def _split_wait(name, started, plan, ns, after):
    send, recv, bufs = started[0], started[1], list(started[2:])
    nb = len(bufs)

    def body(*refs):
        send_ref, recv_ref = refs[nb], refs[nb + 1]
        for i, (src, dst, dev) in enumerate(plan(refs[:ns], refs[ns:nb])):
            cp = pltpu.make_async_remote_copy(src_ref=src, dst_ref=dst, send_sem=send_ref.at[i], recv_sem=recv_ref.at[i],
                                              device_id=dev, device_id_type=MESH)
            cp.wait_send()
            cp.wait_recv()

    res = pl.pallas_call(
        body, name=name, out_shape=[pltpu.HBM(a.shape, a.dtype) for a in bufs],
        in_specs=[HBM_SPEC] * nb + [SEM_SPEC, SEM_SPEC, ANY], out_specs=[HBM_SPEC] * nb,
        input_output_aliases={i: i for i in range(nb)},
        compiler_params=pltpu.CompilerParams(has_side_effects=pltpu.SideEffectType.DATAFLOW_SIDE_EFFECTING),
    )(*bufs, send, recv, after)
    return res[:ns], res[ns:]


def _gather_finish(name, shards, stacks):
    nt = len(shards)
    hs = [s.shape[0] // 2 for s in shards]
    crs = [_chunk_rows(h, s.shape[1] * s.dtype.itemsize) for h, s in zip(hs, shards)]

    def body(*refs):
        ins, outs, bufs, sems = refs[:nt], refs[2 * nt:3 * nt], refs[3 * nt:4 * nt], refs[4 * nt:]
        x, y, c = _place()
        me = 2 * x + y
        sib = (x, y, 1 - c)
        for t in range(nt):
            h, cr, cols = hs[t], crs[t], shards[t].shape[1]
            passed = [_slot(outs[t], 2 * px + py, c * h + r0, cr, cols)
                      for px, py in _other_chips(x, y) for r0 in range(0, h, cr)]
            _stream([(r, r) for r in passed], bufs[t], sems, t, sib)
            own = [(ins[t].at[pl.ds(r0, cr)], _slot(outs[t], me, r0, cr, cols)) for r0 in range(0, 2 * h, cr)]
            _stream(own, bufs[t], sems, t, None)
        for t in range(nt):
            if len(stacks[t].shape) == 3:
                three = outs[t].at[pl.ds(0, 3), pl.ds(0, hs[t])]
            else:
                three = outs[t].at[pl.ds(0, hs[t]), pl.ds(0, 3 * shards[t].shape[1])]
            pltpu.make_async_remote_copy(src_ref=three, dst_ref=three, send_sem=sems[1].at[t, 0],
                                         recv_sem=sems[2].at[t], device_id=sib, device_id_type=MESH).wait_recv()

    return pl.pallas_call(
        body, name=name, in_specs=[ANY] * (2 * nt), out_specs=[ANY] * nt,
        out_shape=[jax.ShapeDtypeStruct(s.shape, s.dtype) for s in stacks],
        scratch_shapes=_stream_scratch([((cr, s.shape[1]), s.dtype) for cr, s in zip(crs, shards)]),
        input_output_aliases={nt + t: t for t in range(nt)},
        compiler_params=pltpu.CompilerParams(vmem_limit_bytes=VMEM_LIMIT_BYTES),
    )(*shards, *stacks)


def _share_halves(fulls):
    nt = len(fulls)
    crs = [_chunk_rows(f.shape[1], f.shape[2] * f.dtype.itemsize) for f in fulls]

    def body(*refs):
        outs, bufs, sems = refs[nt:2 * nt], refs[2 * nt:3 * nt], refs[3 * nt:]
        x, y, c = _place()
        sib = (x, y, 1 - c)
        for t in range(nt):
            rows = [outs[t].at[c, pl.ds(r0, crs[t])] for r0 in range(0, fulls[t].shape[1], crs[t])]
            _stream([(r, r) for r in rows], bufs[t], sems, t, sib)
        for t in range(nt):
            other = outs[t].at[1 - c]
            pltpu.make_async_remote_copy(src_ref=other, dst_ref=other, send_sem=sems[1].at[t, 0],
                                         recv_sem=sems[2].at[t], device_id=sib, device_id_type=MESH).wait_recv()

    return pl.pallas_call(
        body, name="share_halves", in_specs=[ANY] * nt, out_specs=[ANY] * nt,
        out_shape=[jax.ShapeDtypeStruct(f.shape, f.dtype) for f in fulls],
        scratch_shapes=_stream_scratch([((cr, f.shape[2]), f.dtype) for cr, f in zip(crs, fulls)]),
        input_output_aliases={t: t for t in range(nt)},
        compiler_params=pltpu.CompilerParams(vmem_limit_bytes=VMEM_LIMIT_BYTES),
    )(*fulls)


def _allgather_small(name, v):
    def body(v_ref, o_ref, send, recv, lsem):
        x, y, c = _place()
        me = 4 * x + 2 * y + c
        loc = pltpu.make_async_copy(v_ref, o_ref.at[me], lsem)
        loc.start()
        copies = []
        for k in range(1, N_DEV):
            px = 1 - x if k & 4 else x
            py = 1 - y if k & 2 else y
            pc = 1 - c if k & 1 else c
            cp = pltpu.make_async_remote_copy(
                src_ref=v_ref, dst_ref=o_ref.at[me], send_sem=send.at[k - 1], recv_sem=recv.at[k - 1],
                device_id=(px, py, pc), device_id_type=MESH)
            cp.start()
            copies.append(cp)
        for cp in copies:
            cp.wait()
        loc.wait()

    vm = pl.BlockSpec(memory_space=pltpu.VMEM)
    return pl.pallas_call(
        body, name=name, in_specs=[vm], out_specs=vm,
        out_shape=jax.ShapeDtypeStruct((N_DEV,) + v.shape, v.dtype),
        scratch_shapes=[pltpu.SemaphoreType.DMA((N_DEV - 1,))] * 2 + [pltpu.SemaphoreType.DMA],
    )(v)


def kernel(x, mem, norm_g, mem_norm_g, w_kv, w_out, pool_w_in, pool_w_grp, pool_scale, hgrn_w_in, hgrn_lb, hgrn_norm_g, final_g, loss_target, m_norm_g, m_mem_norm_g, m_w_kv, m_w_out, m_pool_w_in, m_pool_w_grp, m_pool_scale, m_hgrn_w_in, m_hgrn_lb, m_hgrn_norm_g, m_final_g, v_norm_g, v_mem_norm_g, v_w_kv, v_w_out, v_pool_w_in, v_pool_w_grp, v_pool_scale, v_hgrn_w_in, v_hgrn_lb, v_hgrn_norm_g, v_final_g):
    _, S, D = x.shape
    M = mem.shape[1]
    EB = 2 * D
    ECA = EB // 4
    EMIX = EB - ECA
    PG = EMIX // N_POOL_GROUPS
    NP0 = EMIX + ECA + EB
    NP1 = 3 * EMIX + ECA + EB
    SH0, SH1 = NP0 // N_CHIPS, NP1 // N_CHIPS
    DK, EK = D // N_CHIPS, EB // N_CHIPS
    TNP = 512 if all(v % 512 == 0 for v in (SH0, SH1, ECA, EMIX)) else 256
    TM = _tile(S, 1024)
    TMF = _tile(S, 2048)
    TD = _tile(D, 512)
    TDW = _tile(D, 1024)
    c0, c1 = SH0 // TNP, SH1 // TNP
    qt, et = EMIX // TNP, ECA // TNP
    chip = 2 * lax.axis_index("x") + lax.axis_index("y")

    xs, ms, tgt = x[0], mem[0], loss_target[0]

    flat = lambda w: w.reshape(-1, w.shape[-1])
    shard2d = [flat(w_kv), flat(w_out), flat(pool_w_in), flat(pool_w_grp), flat(hgrn_w_in)]
    bf = lambda w: w.astype(BF16)
    early = [bf(w_kv[0]), bf(w_out[0]), bf(flat(pool_w_in)), bf(flat(pool_w_grp))]
    stacked = [jax.ShapeDtypeStruct((N_CHIPS,) + s.shape, BF16) for s in early]
    wkv0, wout0, wpin, g_grp = _gather_finish(
        "gather0_finish", early, _copies_now("gather0", early, stacked, _gather_plan, 3 * len(early)))
    late = [bf(w_kv[1]), bf(w_out[1]), bf(flat(hgrn_w_in))]
    landing = [lax.empty((N_CHIPS,) + late[0].shape, BF16), lax.empty((N_CHIPS,) + late[1].shape, BF16),
               lax.empty((D, NP1), BF16)]
    gather1, token = _split_start("gather1_start", late, landing, _gather_plan, 3 * len(late), wpin)
    wgrp = g_grp.reshape(N_CHIPS, N_POOL_GROUPS, PG // N_CHIPS, PG).transpose(1, 0, 2, 3).reshape(N_POOL_GROUPS, PG, PG)

    sds = jax.ShapeDtypeStruct
    tdk, tek, tew = _tile(DK, 512), _tile(EK, 512), _tile(EK, 1024)

    mem_n = _rms_fwd("rms_mem", ms, mem_norm_g.reshape(1, D))
    tkv = _tile(2 * ECA, 512)

    def kv_of(layer, wkv):
        return _matmul(
            f"kv{layer}", mem_n, wkv, grid=(1, 2 * ECA // tkv, D // tdk),
            a_spec=pl.BlockSpec((M, tdk), lambda i, j, k: (0, k)),
            b_spec=pl.BlockSpec((None, tdk, tkv), lambda i, j, k: (k // (DK // tdk), k % (DK // tdk), j)),
            out_shape=sds((M, 2 * ECA), BF16), out_spec=pl.BlockSpec((M, tkv), lambda i, j, k: (0, j)),
            acc_shape=(M, tkv), dims=NN)

    def out_proj(layer, branch, wout, resid):
        return _matmul(
            f"out_proj{layer}", branch, wout, grid=(S // TM, D // TDW, EB // tew),
            a_spec=pl.BlockSpec((TM, tew), IK),
            b_spec=pl.BlockSpec((None, tew, TDW), lambda i, j, k: (k // (EK // tew), k % (EK // tew), j)),
            out_shape=sds((S, D), F32), out_spec=pl.BlockSpec((TM, TDW), IJ),
            acc_shape=(TM, TDW), dims=NN, add=resid, add_spec=pl.BlockSpec((TM, TDW), IJ))

    ones_ca = jnp.ones((1, ECA), F32)

    h0 = _rms_fwd("rms0", xs, norm_g[0:1] + token[0:1, 0:1])
    kv = [kv_of(0, wkv0), None]
    proj0 = _matmul(
        "proj0", h0, wpin, grid=(S // TMF, NP0 // TNP, 1),
        a_spec=pl.BlockSpec((TMF, D), lambda i, j, k: (i, 0)),
        b_spec=pl.BlockSpec((None, D, TNP), lambda i, j, k: (j // c0, 0, j % c0)),
        out_shape=sds((S, NP0), BF16), out_spec=pl.BlockSpec((TMF, TNP), IJ),
        acc_shape=(TMF, TNP), dims=NN)
    pooled = _pool_fwd(proj0, S, EMIX)
    premix0 = _matmul(
        "pool_grp", pooled, wgrp, grid=(S // TM, N_POOL_GROUPS, 1),
        a_spec=pl.BlockSpec((TM, PG), lambda i, j, k: (i, j)),
        b_spec=pl.BlockSpec((None, PG, PG), lambda i, j, k: (j, 0, 0)),
        out_shape=sds((S, EB), BF16), out_spec=pl.BlockSpec((TM, PG), lambda i, j, k: (i, j)),
        acc_shape=(TM, PG), dims=NN)
    premix0 = _ca_fwd("ca_fwd0", proj0, EMIX // ECA, kv[0], premix0, S, ECA, EMIX)
    colscale0 = jnp.concatenate([pool_scale.reshape(1, EMIX), ones_ca], axis=1)
    gblk0 = (EMIX + ECA) // ECA
    branch0 = _gate_fwd("gate_fwd0", premix0, proj0, gblk0, colscale0, S, EB, ECA)
    x1 = out_proj(0, branch0, wout0, xs)

    wkv1, wout1, whin = _gather_finish("gather1_finish", *_split_wait("gather1_wait", gather1, _gather_plan, len(late), x1))
    kv[1] = kv_of(1, wkv1)
    h1 = _rms_fwd("rms1", x1, norm_g[1:2])

    def proj1_cols(name, ncols, col_of, out_cols, out_dtype, out_col_of):
        return _matmul(
            name, h1, whin, grid=(S // TMF, ncols, 1),
            a_spec=pl.BlockSpec((TMF, D), lambda i, j, k: (i, 0)),
            b_spec=pl.BlockSpec((D, TNP), lambda i, j, k: (0, col_of(j))),
            out_shape=sds((S, out_cols), out_dtype), out_spec=pl.BlockSpec((TMF, TNP), lambda i, j, k: (i, out_col_of(j))),
            acc_shape=(TMF, TNP), dims=NN)

    skip_f = lambda j: jnp.where(j < qt, j, j + qt)
    proj1 = proj1_cols("proj1", NP1 // TNP - qt, skip_f, NP1, BF16, skip_f)
    fgate = proj1_cols("proj1_f", qt, lambda j: j + qt, EMIX, F32, lambda j: j)
    premix1, rstd1, states = _hgrn_fwd(proj1, fgate, hgrn_lb, S, EMIX, EB)
    premix1 = _ca_fwd("ca_fwd1", proj1, 3 * EMIX // ECA, kv[1], premix1, S, ECA, EMIX)
    norm_tiles = _allgather_small("allgather_norm_g", jnp.pad(hgrn_norm_g, ((0, SMALL_ROWS - 1), (0, 0))))
    hg_norm = norm_tiles[0::2, 0, :].reshape(1, EMIX)
    colscale1 = jnp.concatenate([hg_norm, ones_ca], axis=1)
    gblk1 = (3 * EMIX + ECA) // ECA
    branch1 = _gate_fwd("gate_fwd1", premix1, proj1, gblk1, colscale1, S, EB, ECA)
    x2 = out_proj(1, branch1, wout1, x1)

    dx2, dx2b, d_final_g, loss_part = _loss_head(x2, final_g.reshape(1, D), tgt)

    def out_proj_bwd(layer, dxb, branch, wout):
        dbranch = _matmul(
            f"dbranch{layer}", dxb, wout, grid=(S // TMF, EB // tek, 1),
            a_spec=pl.BlockSpec((TMF, D), lambda i, j, k: (i, 0)),
            b_spec=pl.BlockSpec((None, tek, D), lambda i, j, k: (j // (EK // tek), j % (EK // tek), 0)),
            out_shape=sds((S, EB), BF16), out_spec=pl.BlockSpec((TMF, tek), IJ), acc_shape=(TMF, tek), dims=NT)
        dw = _matmul(
            f"dwout{layer}", branch, dxb, grid=(EB // tew, D // TD, 1),
            a_spec=pl.BlockSpec((S, tew), lambda i, j, k: (0, i)), b_spec=pl.BlockSpec((S, TD), lambda i, j, k: (0, j)),
            out_shape=sds((N_CHIPS, EK, D), BF16),
            out_spec=pl.BlockSpec((None, tew, TD), lambda i, j, k: (i // (EK // tew), i % (EK // tew), j)),
            acc_shape=(tew, TD), dims=TN)
        return dbranch, dw

    def kv_bwd(layer, dkv, wkv, dmem_add):
        dkvb = dkv.astype(BF16)
        dmem = _matmul(
            f"dmem{layer}", dkvb, wkv, grid=(1, D // tdk, 2 * ECA // tkv),
            a_spec=pl.BlockSpec((M, tkv), lambda i, j, k: (0, k)),
            b_spec=pl.BlockSpec((None, tdk, tkv), lambda i, j, k: (j // (DK // tdk), j % (DK // tdk), k)),
            out_shape=sds((M, D), F32), out_spec=pl.BlockSpec((M, tdk), lambda i, j, k: (0, j)), acc_shape=(M, tdk),
            dims=NT, add=dmem_add, add_spec=pl.BlockSpec((M, tdk), lambda i, j, k: (0, j)))
        dw = _matmul(
            f"dwkv{layer}", mem_n, dkvb, grid=(D // tdk, 2 * ECA // tkv, 1),
            a_spec=pl.BlockSpec((M, tdk), lambda i, j, k: (0, i)), b_spec=pl.BlockSpec((M, tkv), lambda i, j, k: (0, j)),
            out_shape=sds((N_CHIPS, DK, 2 * ECA), BF16),
            out_spec=pl.BlockSpec((None, tdk, tkv), lambda i, j, k: (i // (DK // tdk), i % (DK // tdk), j)),
            acc_shape=(tdk, tkv), dims=TN)
        return dmem, dw

    dbranch1, gw_out1 = out_proj_bwd(1, dx2b, branch1, wout1)
    dpremix1, drest1, dcol1 = _gate_bwd("gate_bwd1", dbranch1, premix1, proj1, gblk1, colscale1, (S, ECA + EB), 1,
                                        S, EB, ECA)
    drest1, dkv1 = _ca_bwd("ca_bwd1", dpremix1, proj1, 3 * EMIX // ECA, kv[1], drest1, 0, S, ECA, EMIX)
    dqfi, dlb = _hgrn_bwd(dpremix1, premix1, rstd1, states, proj1, fgate, hgrn_lb, S, EMIX)
    nq, nr = 3 * qt, (ECA + EB) // TNP
    tkh = _tile(EMIX, 1024) if (ECA + EB) % _tile(EMIX, 1024) == 0 else TNP
    kq = EMIX // tkh
    dh1 = _matmul(
        "dh1_qfi", dqfi, whin, grid=(S // TM, D // TDW, 3 * kq),
        a_spec=pl.BlockSpec((None, TM, tkh), lambda i, j, k: (k // kq, i, k % kq)),
        b_spec=pl.BlockSpec((TDW, tkh), lambda i, j, k: (j, k)),
        out_shape=sds((S, D), F32), out_spec=pl.BlockSpec((TM, TDW), IJ), acc_shape=(TM, TDW), dims=NT)
    dh1 = _matmul(
        "dh1_rest", drest1, whin, grid=(S // TM, D // TDW, (ECA + EB) // tkh), a_spec=pl.BlockSpec((TM, tkh), IK),
        b_spec=pl.BlockSpec((TDW, tkh), lambda i, j, k: (j, k + 3 * kq)),
        out_shape=sds((S, D), F32), out_spec=pl.BlockSpec((TM, TDW), IJ), acc_shape=(TM, TDW), dims=NT,
        add=dh1, add_spec=pl.BlockSpec((TM, TDW), IJ))
    gw_hin = _matmul(
        "dwhin_qfi", h1, dqfi, grid=(D // TDW, nq, 1), a_spec=pl.BlockSpec((S, TDW), lambda i, j, k: (0, i)),
        b_spec=pl.BlockSpec((None, S, TNP), lambda i, j, k: (j // qt, 0, j % qt)),
        out_shape=sds((N_CHIPS, D, SH1), BF16), out_spec=pl.BlockSpec((None, TDW, TNP), lambda i, j, k: (j // c1, i, j % c1)),
        acc_shape=(TDW, TNP), dims=TN)
    gw_hin = _matmul(
        "dwhin_rest", h1, drest1, grid=(D // TDW, nr, 1), a_spec=pl.BlockSpec((S, TDW), lambda i, j, k: (0, i)),
        b_spec=pl.BlockSpec((S, TNP), lambda i, j, k: (0, j)), out_shape=sds((N_CHIPS, D, SH1), BF16),
        out_spec=pl.BlockSpec((None, TDW, TNP), lambda i, j, k: ((j + nq) // c1, i, (j + nq) % c1)),
        acc_shape=(TDW, TNP), dims=TN, alias=gw_hin)
    dmem, gw_kv1 = kv_bwd(1, dkv1, wkv1, None)

    core_chip = jnp.stack([lax.axis_index("c"), chip]).astype(jnp.int32)

    def reduce_in_chip(tag, stacks):
        got = _exchange_halves(f"exchange_halves{tag}", stacks)
        pairs = [_add_halves(f"add_halves{tag}_{t}", core_chip, g.reshape(N_CHIPS, 2, g.shape[1] // 2, g.shape[2]), r)
                 for t, (g, r) in enumerate(zip(stacks, got))]
        return [p for p, _ in pairs], [own for _, own in pairs]

    parts1, landed1 = reduce_in_chip(1, [gw_kv1, gw_out1, gw_hin])
    scatter1, token1 = _split_start("scatter1_start", parts1, landed1, _scatter_plan, 3 * len(parts1), None)
    dx1, dx1b, d_ng1 = _rms_bwd("rms_bwd1", dh1, x1, norm_g[1:2] + token1[0:1, 0:1], dx2)

    dbranch0, gw_out0 = out_proj_bwd(0, dx1b, branch0, wout0)
    dpremix0, dproj0, dcol0 = _gate_bwd("gate_bwd0", dbranch0, premix0, proj0, gblk0, colscale0, (S, NP0), gblk0,
                                        S, EB, ECA)
    dproj0, dkv0 = _ca_bwd("ca_bwd0", dpremix0, proj0, EMIX // ECA, kv[0], dproj0, EMIX // ECA, S, ECA, EMIX)
    dmem, gw_kv0 = kv_bwd(0, dkv0, wkv0, dmem)
    parts_a, landed_a = reduce_in_chip("0a", [gw_kv0, gw_out0])
    scatter_a, token_a = _split_start("scatter0a_start", parts_a, landed_a, _scatter_plan, 3 * len(parts_a), None)
    dpooled = _matmul(
        "dpooled", dpremix0, wgrp, grid=(S // TM, N_POOL_GROUPS, 1), a_spec=pl.BlockSpec((TM, PG), IJ),
        b_spec=pl.BlockSpec((None, PG, PG), lambda i, j, k: (j, 0, 0)),
        out_shape=sds((S, EMIX), F32), out_spec=pl.BlockSpec((TM, PG), IJ), acc_shape=(TM, PG), dims=NT, after=token_a)
    dwgrp = _matmul(
        "dwgrp", pooled, dpremix0, grid=(N_POOL_GROUPS, 1, 1), a_spec=pl.BlockSpec((S, PG), lambda i, j, k: (0, i)),
        b_spec=pl.BlockSpec((S, PG), lambda i, j, k: (0, i)), out_shape=sds((N_POOL_GROUPS, PG, PG), F32),
        out_spec=pl.BlockSpec((None, PG, PG), lambda i, j, k: (i, 0, 0)), acc_shape=(PG, PG), dims=TN)
    dproj0 = _pool_bwd(dpooled, dproj0, S, EMIX)
    gw_pin = _matmul(
        "dwpin", h0, dproj0, grid=(D // TDW, NP0 // TNP, 1), a_spec=pl.BlockSpec((S, TDW), lambda i, j, k: (0, i)),
        b_spec=pl.BlockSpec((S, TNP), lambda i, j, k: (0, j)), out_shape=sds((N_CHIPS, D, SH0), BF16),
        out_spec=pl.BlockSpec((None, TDW, TNP), lambda i, j, k: (j // c0, i, j % c0)), acc_shape=(TDW, TNP), dims=TN)
    gw_grp = dwgrp.reshape(N_POOL_GROUPS, N_CHIPS, PG // N_CHIPS, PG).transpose(1, 0, 2, 3).reshape(N_CHIPS, PG, PG)
    parts_b, landed_b = reduce_in_chip("0b", [gw_pin, gw_grp.astype(BF16)])
    scatter_b, token_b = _split_start("scatter0b_start", parts_b, landed_b, _scatter_plan, 3 * len(parts_b), None)
    dh0 = _matmul(
        "dh0", dproj0, wpin, grid=(S // TM, D // TDW, N_CHIPS), a_spec=pl.BlockSpec((TM, SH0), IK),
        b_spec=pl.BlockSpec((None, TDW, SH0), lambda i, j, k: (k, j, 0)),
        out_shape=sds((S, D), F32), out_spec=pl.BlockSpec((TM, TDW), IJ), acc_shape=(TM, TDW), dims=NT, after=token_b)
    grad_x, _, d_ng0 = _rms_bwd("rms_bwd0", dh0, xs, norm_g[0:1], dx1)
    _, _, d_mng = _rms_bwd("rms_bwd_mem", dmem, ms, mem_norm_g.reshape(1, D), jnp.zeros_like(ms))

    _, landed1 = _split_wait("scatter1_wait", scatter1, _scatter_plan, len(parts1), grad_x)
    _, landed_a = _split_wait("scatter0a_wait", scatter_a, _scatter_plan, len(parts_a), grad_x)
    _, landed_b = _split_wait("scatter0b_wait", scatter_b, _scatter_plan, len(parts_b), grad_x)
    landed = [landed_a[0], landed1[0], landed_a[1], landed1[1], landed_b[0], landed_b[1], landed1[2]]
    fulls = _share_halves([_sum_slots(f"sum_slots{t}", core_chip, p) for t, p in enumerate(landed)])
    f2 = [f.reshape(-1, f.shape[-1]) for f in fulls]
    big_g = [jnp.concatenate(f2[0:2], axis=0), jnp.concatenate(f2[2:4], axis=0), f2[4], f2[5], f2[6]]
    big_names = ["w_kv", "w_out", "pool_w_in", "pool_w_grp", "hgrn_w_in"]
    big_m = [flat(a) for a in (m_w_kv, m_w_out, m_pool_w_in, m_pool_w_grp, m_hgrn_w_in)]
    big_v = [flat(a) for a in (v_w_kv, v_w_out, v_pool_w_in, v_pool_w_grp, v_hgrn_w_in)]
    big_shapes = [w_kv.shape, w_out.shape, pool_w_in.shape, pool_w_grp.shape, hgrn_w_in.shape]
    grads, deltas, new_m, new_v = {}, {}, {}, {}
    for t, n in enumerate(big_names):
        d, mn, vn = _adamw(f"adamw_{n}", shard2d[t], big_g[t], big_m[t], big_v[t])
        grads[n], deltas[n] = big_g[t].reshape(big_shapes[t]), d.reshape(big_shapes[t])
        new_m[n], new_v[n] = mn.reshape(big_shapes[t]), vn.reshape(big_shapes[t])

    Wd = EMIX
    partial = _pack_rows("pack_partials", [d_ng0, d_ng1, d_mng, dcol0[:, :EMIX], dlb, dcol1[:, :EMIX], d_final_g,
                                           loss_part], Wd)
    summed = _small_sum(_allgather_small("allgather_grads", partial), hgrn_lb, 4)
    row = lambda i, n=Wd: summed[i:i + 1, :n]
    nshard = EMIX // N_CHIPS
    g_hg_norm = lax.dynamic_slice_in_dim(row(5), chip * nshard, nshard, axis=1)
    small_names = ["norm_g0", "norm_g1", "mem_norm_g", "pool_scale", "hgrn_lb0", "hgrn_lb1", "hgrn_norm_g", "final_g"]
    small_w = [norm_g[0:1], norm_g[1:2], mem_norm_g.reshape(1, D), pool_scale, hgrn_lb[0:1], hgrn_lb[1:2], hgrn_norm_g,
               final_g.reshape(1, D)]
    small_m = [m_norm_g[0:1], m_norm_g[1:2], m_mem_norm_g.reshape(1, D), m_pool_scale, m_hgrn_lb[0:1], m_hgrn_lb[1:2],
               m_hgrn_norm_g, m_final_g.reshape(1, D)]
    small_v = [v_norm_g[0:1], v_norm_g[1:2], v_mem_norm_g.reshape(1, D), v_pool_scale, v_hgrn_lb[0:1], v_hgrn_lb[1:2],
               v_hgrn_norm_g, v_final_g.reshape(1, D)]
    g_pack = _pack_rows("pack_small_g", [row(0, D), row(1, D), row(2, D), row(3), row(8), row(9), g_hg_norm, row(6, D)], Wd)
    d_pack, m_pack, v_pack = _adamw("adamw_small", _pack_rows("pack_small_w", small_w, Wd), g_pack,
                                    _pack_rows("pack_small_m", small_m, Wd), _pack_rows("pack_small_v", small_v, Wd))
    widths = [v.shape[1] for v in small_w]
    rows = lambda p: {n: p[i, :widths[i]] for i, n in enumerate(small_names)}

    def assemble(r, out):
        out["norm_g"] = jnp.stack([r["norm_g0"], r["norm_g1"]])
        out["mem_norm_g"] = r["mem_norm_g"]
        out["pool_scale"] = r["pool_scale"].reshape(1, EMIX)
        out["hgrn_lb"] = jnp.stack([r["hgrn_lb0"], r["hgrn_lb1"]])
        out["hgrn_norm_g"] = r["hgrn_norm_g"].reshape(1, nshard)
        out["final_g"] = r["final_g"]

    assemble(rows(g_pack), grads)
    assemble(rows(d_pack), deltas)
    assemble(rows(m_pack), new_m)
    assemble(rows(v_pack), new_v)
    loss = summed[7, 0]

    order = ["norm_g", "mem_norm_g", "w_kv", "w_out", "pool_w_in", "pool_w_grp", "pool_scale", "hgrn_w_in", "hgrn_lb",
             "hgrn_norm_g", "final_g"]
    return (loss, grad_x.reshape(1, S, D), *[grads[n] for n in order], *[deltas[n] for n in order],
            *[new_m[n] for n in order], *[new_v[n] for n in order])
```

```python
import functools

import jax
import jax.numpy as jnp
from jax import lax
from jax.experimental import pallas as pl
from jax.experimental.pallas import tpu as pltpu

F32 = jnp.float32
BF16 = jnp.bfloat16
MESH = pl.DeviceIdType.MESH
ANY = pl.BlockSpec(memory_space=pl.ANY)

EPS = 1e-6
HG_HEAD_DIM = 128
HG_CHUNK = 64
CA_HEADS = 4
N_POOL_GROUPS = 4
POOL_HALO = 128
ADAM_LR = 0.001
ADAM_B1 = 0.9
ADAM_B2 = 0.999
ADAM_EPS = 1e-08
ADAM_WD = 0.01
ADAM_STEP = 10
N_CHIPS = 4
N_DEV = 8
VMEM_LIMIT_BYTES = 56 * 1024 * 1024
SMALL_ROWS = 8
STREAM_CHUNK_BYTES = 2 * 1024 * 1024
STREAM_SLOTS = 3


def _params(*sem):
    return pltpu.CompilerParams(dimension_semantics=sem, vmem_limit_bytes=VMEM_LIMIT_BYTES)


def _tile(n, pref):
    t = pref
    while n % t:
        t //= 2
    return t


def _sigmoid(x):
    return 1.0 / (1.0 + jnp.exp(-x))


def _matmul(name, a, b, *, grid, a_spec, b_spec, out_shape, out_spec, acc_shape, dims,
            add=None, add_spec=None, alias=None, after=None):
    nk = grid[2]
    has_add = add is not None
    has_alias = alias is not None
    has_after = after is not None

    def body(*refs):
        a_ref, b_ref = refs[0], refs[1]
        pos = 2
        add_ref = None
        if has_add:
            add_ref = refs[pos]
            pos += 1
        pos += has_alias + has_after
        o_ref = refs[pos]
        prod = lax.dot_general(a_ref[...], b_ref[...], (dims, ((), ())), preferred_element_type=F32)

        def finish(r):
            if has_add:
                r = r + add_ref[...].astype(F32)
            o_ref[...] = r.astype(o_ref.dtype)

        if nk == 1:
            finish(prod)
            return
        acc_ref = refs[pos + 1]
        k = pl.program_id(2)

        @pl.when(k == 0)
        def _():
            acc_ref[...] = prod

        @pl.when(k > 0)
        def _():
            acc_ref[...] += prod

        @pl.when(k == nk - 1)
        def _():
            finish(acc_ref[...])

    operands = [a, b]
    in_specs = [a_spec, b_spec]
    if has_add:
        operands.append(add)
        in_specs.append(add_spec)
    aliases = {}
    if has_alias:
        aliases = {len(operands): 0}
        operands.append(alias)
        in_specs.append(ANY)
    if has_after:
        operands.append(after)
        in_specs.append(ANY)
    return pl.pallas_call(
        body, name=name, grid=grid, in_specs=in_specs, out_specs=out_spec, out_shape=out_shape,
        scratch_shapes=[pltpu.VMEM(acc_shape, F32)] if nk > 1 else [], input_output_aliases=aliases,
        compiler_params=_params("parallel", "parallel", "arbitrary"),
    )(*operands)


IJ = lambda i, j, k: (i, j)
IK = lambda i, j, k: (i, k)
KJ = lambda i, j, k: (k, j)
KI = lambda i, j, k: (k, i)
NN = ((1,), (0,))
NT = ((1,), (1,))
TN = ((0,), (0,))


def _rms_fwd(name, x, g):
    R, D = x.shape
    tr = _tile(R, 256)

    def body(x_ref, g_ref, o_ref):
        xf = x_ref[...]
        r = lax.rsqrt(jnp.mean(xf * xf, axis=-1, keepdims=True) + EPS)
        o_ref[...] = (xf * r * g_ref[...]).astype(o_ref.dtype)

    return pl.pallas_call(
        body, name=name, grid=(R // tr,),
        in_specs=[pl.BlockSpec((tr, D), lambda i: (i, 0)), pl.BlockSpec((1, D), lambda i: (0, 0))],
        out_specs=pl.BlockSpec((tr, D), lambda i: (i, 0)),
        out_shape=jax.ShapeDtypeStruct((R, D), BF16), compiler_params=_params("parallel"),
    )(x, g)


def _rms_bwd(name, dh, x, g, dres):
    R, D = x.shape
    tr = _tile(R, 256)

    def body(dh_ref, x_ref, g_ref, dres_ref, dx_ref, dxb_ref, dg_ref):
        xf = x_ref[...]
        r = lax.rsqrt(jnp.mean(xf * xf, axis=-1, keepdims=True) + EPS)
        xn = xf * r
        d = dh_ref[...]
        dyg = d * g_ref[...]
        dx = r * (dyg - xn * jnp.mean(dyg * xn, axis=-1, keepdims=True)) + dres_ref[...]
        dx_ref[...] = dx
        dxb_ref[...] = dx.astype(BF16)

        @pl.when(pl.program_id(0) == 0)
        def _():
            dg_ref[...] = jnp.zeros_like(dg_ref)

        dg_ref[...] += jnp.sum(d * xn, axis=0, keepdims=True)

    row = pl.BlockSpec((tr, D), lambda i: (i, 0))
    vec = pl.BlockSpec((1, D), lambda i: (0, 0))
    return pl.pallas_call(
        body, name=name, grid=(R // tr,), in_specs=[row, row, vec, row], out_specs=[row, row, vec],
        out_shape=[jax.ShapeDtypeStruct((R, D), F32), jax.ShapeDtypeStruct((R, D), BF16),
                   jax.ShapeDtypeStruct((1, D), F32)],
        compiler_params=_params("arbitrary"),
    )(dh, x, g, dres)


def _loss_head(x2, g, target):
    R, D = x2.shape
    tr = _tile(R, 256)

    def body(x_ref, g_ref, t_ref, dx_ref, dxb_ref, dg_ref, loss_ref):
        xf = x_ref[...]
        gg = g_ref[...]
        r = lax.rsqrt(jnp.mean(xf * xf, axis=-1, keepdims=True) + EPS)
        xn = xf * r
        e = xn * gg - t_ref[...]
        part = 0.5 * jnp.sum(jnp.mean(e * e, axis=-1, keepdims=True), axis=0, keepdims=True)
        dy = e * (1.0 / D)
        dyg = dy * gg
        dx = r * (dyg - xn * jnp.mean(dyg * xn, axis=-1, keepdims=True))
        dx_ref[...] = dx
        dxb_ref[...] = dx.astype(BF16)

        @pl.when(pl.program_id(0) == 0)
        def _():
            dg_ref[...] = jnp.zeros_like(dg_ref)
            loss_ref[...] = jnp.zeros_like(loss_ref)

        dg_ref[...] += jnp.sum(dy * xn, axis=0, keepdims=True)
        loss_ref[...] += jnp.broadcast_to(part, loss_ref.shape)

    row = pl.BlockSpec((tr, D), lambda i: (i, 0))
    vec = pl.BlockSpec((1, D), lambda i: (0, 0))
    return pl.pallas_call(
        body, name="loss_head", grid=(R // tr,), in_specs=[row, vec, row],
        out_specs=[row, row, vec, pl.BlockSpec((1, 128), lambda i: (0, 0))],
        out_shape=[jax.ShapeDtypeStruct((R, D), F32), jax.ShapeDtypeStruct((R, D), BF16),
                   jax.ShapeDtypeStruct((1, D), F32), jax.ShapeDtypeStruct((1, 128), F32)],
        compiler_params=_params("arbitrary"),
    )(x2, g, target)


def _pool_band(tr, reverse, w):
    r = lax.broadcasted_iota(jnp.int32, (tr, tr + POOL_HALO), 0)
    c = lax.broadcasted_iota(jnp.int32, (tr, tr + POOL_HALO), 1)
    if reverse:
        inside = (c >= r) & (c < r + w)
    else:
        cc = c - POOL_HALO
        inside = (cc <= r) & (cc > r - w)
    return jnp.where(inside, 1.0, 0.0).astype(BF16)


def _pool_fwd(proj, S, EMIX):
    PG = EMIX // N_POOL_GROUPS
    cb = PG
    tr = _tile(S, 512)
    per_group = PG // cb

    def body(u_ref, o_ref, ext):
        i = pl.program_id(1)
        w = jnp.left_shift(2, pl.program_id(0) // per_group)

        @pl.when(i == 0)
        def _():
            ext[0:POOL_HALO, :] = jnp.zeros((POOL_HALO, cb), BF16)

        u = u_ref[...]
        ext[POOL_HALO:, :] = u
        win = jnp.dot(_pool_band(tr, False, w), ext[...], preferred_element_type=F32)
        pos = i * tr + lax.broadcasted_iota(jnp.int32, (tr, 1), 0)
        cnt = jnp.minimum(pos + 1, w).astype(F32)
        o_ref[...] = (win / cnt - u.astype(F32)).astype(BF16)
        ext[0:POOL_HALO, :] = u[tr - POOL_HALO:, :]

    return pl.pallas_call(
        body, name="pool_fwd", grid=(EMIX // cb, S // tr),
        in_specs=[pl.BlockSpec((tr, cb), lambda j, i: (i, j))],
        out_specs=pl.BlockSpec((tr, cb), lambda j, i: (i, j)),
        out_shape=jax.ShapeDtypeStruct((S, EMIX), BF16),
        scratch_shapes=[pltpu.VMEM((tr + POOL_HALO, cb), BF16)],
        compiler_params=_params("parallel", "arbitrary"),
    )(proj)


def _pool_bwd(dpooled, dproj, S, EMIX):
    PG = EMIX // N_POOL_GROUPS
    cb = PG
    tr = _tile(S, 512)
    per_group = PG // cb
    nrt = S // tr

    def body(d_ref, _, o_ref, ext):
        step = pl.program_id(1)
        i = nrt - 1 - step
        w = jnp.left_shift(2, pl.program_id(0) // per_group)

        @pl.when(step == 0)
        def _():
            ext[tr:, :] = jnp.zeros((POOL_HALO, cb), BF16)

        d = d_ref[...]
        pos = i * tr + lax.broadcasted_iota(jnp.int32, (tr, 1), 0)
        cnt = jnp.minimum(pos + 1, w).astype(F32)
        z = (d / cnt).astype(BF16)
        ext[0:tr, :] = z
        win = jnp.dot(_pool_band(tr, True, w), ext[...], preferred_element_type=F32)
        o_ref[...] = (win - d).astype(BF16)
        ext[tr:, :] = z[0:POOL_HALO, :]

    return pl.pallas_call(
        body, name="pool_bwd", grid=(EMIX // cb, nrt),
        in_specs=[pl.BlockSpec((tr, cb), lambda j, s: (nrt - 1 - s, j)), ANY],
        out_specs=pl.BlockSpec((tr, cb), lambda j, s: (nrt - 1 - s, j)),
        out_shape=jax.ShapeDtypeStruct(dproj.shape, dproj.dtype),
        scratch_shapes=[pltpu.VMEM((tr + POOL_HALO, cb), BF16)],
        input_output_aliases={1: 0},
        compiler_params=_params("parallel", "arbitrary"),
    )(dpooled, dproj)


def _ca_fwd(name, proj, qblk, kv, premix, S, ECA, EMIX):
    M = kv.shape[0]
    hd = ECA // CA_HEADS
    ts = _tile(S, 512)
    scale = hd ** -0.5

    def body(q_ref, kv_ref, _, o_ref):
        for h in range(CA_HEADS):
            q = q_ref[:, h * hd:(h + 1) * hd]
            k = kv_ref[:, h * hd:(h + 1) * hd]
            v = kv_ref[:, ECA + h * hd:ECA + (h + 1) * hd]
            s = lax.dot_general(q, k, (NT, ((), ())), preferred_element_type=F32) * scale
            s = s - jnp.max(s, axis=-1, keepdims=True)
            p = jnp.exp(s)
            p = p / jnp.sum(p, axis=-1, keepdims=True)
            o = jnp.dot(p.astype(BF16), v, preferred_element_type=F32)
            o_ref[:, h * hd:(h + 1) * hd] = o.astype(BF16)

    return pl.pallas_call(
        body, name=name, grid=(S // ts,),
        in_specs=[pl.BlockSpec((ts, ECA), lambda i: (i, qblk)), pl.BlockSpec((M, 2 * ECA), lambda i: (0, 0)), ANY],
        out_specs=pl.BlockSpec((ts, ECA), lambda i: (i, EMIX // ECA)),
        out_shape=jax.ShapeDtypeStruct(premix.shape, premix.dtype),
        input_output_aliases={2: 0}, compiler_params=_params("parallel"),
    )(proj, kv, premix)


def _ca_bwd(name, dpremix, proj, qblk, kv, dbuf, dblk, S, ECA, EMIX):
    M = kv.shape[0]
    hd = ECA // CA_HEADS
    ts = _tile(S, 512)
    scale = hd ** -0.5

    def body(do_ref, q_ref, kv_ref, _, dq_ref, dkv_ref):
        @pl.when(pl.program_id(0) == 0)
        def _():
            dkv_ref[...] = jnp.zeros_like(dkv_ref)

        for h in range(CA_HEADS):
            lo, hi = h * hd, (h + 1) * hd
            q = q_ref[:, lo:hi]
            k = kv_ref[:, lo:hi]
            v = kv_ref[:, ECA + lo:ECA + hi]
            do = do_ref[:, lo:hi]
            s = lax.dot_general(q, k, (NT, ((), ())), preferred_element_type=F32) * scale
            s = s - jnp.max(s, axis=-1, keepdims=True)
            p = jnp.exp(s)
            p = p / jnp.sum(p, axis=-1, keepdims=True)
            pb = p.astype(BF16)
            dkv_ref[:, ECA + lo:ECA + hi] += lax.dot_general(pb, do, (TN, ((), ())), preferred_element_type=F32)
            dp = lax.dot_general(do, v, (NT, ((), ())), preferred_element_type=F32)
            ds = (p * (dp - jnp.sum(p * dp, axis=-1, keepdims=True)) * scale).astype(BF16)
            dq_ref[:, lo:hi] = jnp.dot(ds, k, preferred_element_type=F32).astype(BF16)
            dkv_ref[:, lo:hi] += lax.dot_general(ds, q, (TN, ((), ())), preferred_element_type=F32)

    return pl.pallas_call(
        body, name=name, grid=(S // ts,),
        in_specs=[pl.BlockSpec((ts, ECA), lambda i: (i, EMIX // ECA)), pl.BlockSpec((ts, ECA), lambda i: (i, qblk)),
                  pl.BlockSpec((M, 2 * ECA), lambda i: (0, 0)), ANY],
        out_specs=[pl.BlockSpec((ts, ECA), lambda i: (i, dblk)), pl.BlockSpec((M, 2 * ECA), lambda i: (0, 0))],
        out_shape=[jax.ShapeDtypeStruct(dbuf.shape, dbuf.dtype), jax.ShapeDtypeStruct((M, 2 * ECA), F32)],
        input_output_aliases={3: 0}, compiler_params=_params("arbitrary"),
    )(dpremix, proj, kv, dbuf)


def _gate_fwd(name, premix, proj, gblk, colscale, S, EB, ECA):
    ts = _tile(S, 512)

    def body(p_ref, g_ref, c_ref, o_ref):
        g = g_ref[...].astype(F32)
        o_ref[...] = (p_ref[...].astype(F32) * c_ref[...] * (g * _sigmoid(g))).astype(BF16)

    return pl.pallas_call(
        body, name=name, grid=(S // ts, EB // ECA),
        in_specs=[pl.BlockSpec((ts, ECA), lambda i, j: (i, j)), pl.BlockSpec((ts, ECA), lambda i, j: (i, gblk + j)),
                  pl.BlockSpec((1, ECA), lambda i, j: (0, j))],
        out_specs=pl.BlockSpec((ts, ECA), lambda i, j: (i, j)),
        out_shape=jax.ShapeDtypeStruct((S, EB), BF16), compiler_params=_params("parallel", "parallel"),
    )(premix, proj, colscale)


def _gate_bwd(name, dbranch, premix, proj, gblk, colscale, dshape, dblk, S, EB, ECA):
    ts = _tile(S, 512)

    def body(db_ref, p_ref, g_ref, c_ref, dp_ref, dg_ref, dc_ref):
        g = g_ref[...].astype(F32)
        sg = _sigmoid(g)
        si = g * sg
        c = c_ref[...]
        db = db_ref[...].astype(F32)
        t = db * p_ref[...].astype(F32)
        dp_ref[...] = (db * si * c).astype(BF16)
        dg_ref[...] = (t * c * (sg * (1.0 + g * (1.0 - sg)))).astype(BF16)

        @pl.when(pl.program_id(1) == 0)
        def _():
            dc_ref[...] = jnp.zeros_like(dc_ref)

        dc_ref[...] += jnp.sum(t * si, axis=0, keepdims=True)

    blk = pl.BlockSpec((ts, ECA), lambda j, i: (i, j))
    vec = pl.BlockSpec((1, ECA), lambda j, i: (0, j))
    return pl.pallas_call(
        body, name=name, grid=(EB // ECA, S // ts),
        in_specs=[blk, blk, pl.BlockSpec((ts, ECA), lambda j, i: (i, gblk + j)), vec],
        out_specs=[blk, pl.BlockSpec((ts, ECA), lambda j, i: (i, dblk + j)), vec],
        out_shape=[jax.ShapeDtypeStruct((S, EB), BF16), jax.ShapeDtypeStruct(dshape, BF16),
                   jax.ShapeDtypeStruct((1, EB), F32)],
        compiler_params=_params("parallel", "arbitrary"),
    )(dbranch, premix, proj, colscale)


def _hgrn_lb(lb_ref):
    l0 = lb_ref[0:1, :]
    l1 = lb_ref[1:2, :]
    mx = jnp.maximum(l0, l1)
    e0 = jnp.exp(l0 - mx)
    e1 = jnp.exp(l1 - mx)
    return e1 / (e0 + e1)


def _bdot(a, b, ca, cb):
    return lax.dot_general(a, b, (((ca,), (cb,)), ((0,), (0,))), preferred_element_type=F32)


def _tri_sum(tri, x):
    hi = x.astype(BF16)
    lo = (x - hi.astype(F32)).astype(BF16)
    tri = tri.astype(BF16)
    return _bdot(tri, hi, 2, 1) + _bdot(tri, lo, 2, 1)


def _hgrn_chunks(qin, fin, lbh, n):
    C = HG_CHUNK
    row = lax.broadcasted_iota(jnp.int32, (n, C, C), 1)
    col = lax.broadcasted_iota(jnp.int32, (n, C, C), 2)
    causal = row >= col
    sg = _sigmoid(fin)
    f = lbh + (1.0 - lbh) * sg
    k = 1.0 - f
    g = jnp.log(f)
    b = _tri_sum(jnp.where(causal, 1.0, 0.0), g)
    b_last = jnp.sum(g, axis=1, keepdims=True)
    eb = jnp.exp(b)
    einv = jnp.exp(-b)
    eend = jnp.exp(b_last - b)
    sq = _sigmoid(qin)
    a = qin * sq * (HG_HEAD_DIM ** -0.5) * eb
    bm = k * einv
    e = k * eend
    d = jnp.exp(b_last)
    p = jnp.where(causal, _bdot(a.astype(BF16), bm.astype(BF16), 2, 2), 0.0)
    return dict(causal=causal, sg=sg, f=f, eb=eb, einv=einv, eend=eend, sq=sq, a=a, bm=bm, e=e, d=d, p=p)


def _hgrn_fwd(proj, fgate, hgrn_lb, S, EMIX, EB):
    HD, C = HG_HEAD_DIM, HG_CHUNK
    HH = EMIX // HD
    hb = 2 if HH % 2 == 0 else 1
    W = hb * HD
    tr = _tile(S, 512)
    n = tr // C

    def body(q_ref, f_ref, i_ref, lb_ref, o_ref, rstd_ref, st_ref, state):
        @pl.when(pl.program_id(1) == 0)
        def _():
            state[...] = jnp.zeros_like(state)

        lb = _hgrn_lb(lb_ref)
        for h in range(hb):
            cs = slice(h * HD, (h + 1) * HD)
            qin = q_ref[:, cs].astype(F32).reshape(n, C, HD)
            fin = f_ref[:, cs].reshape(n, C, HD)
            v = i_ref[:, cs].reshape(n, C, HD)
            t = _hgrn_chunks(qin, fin, lb[:, cs], n)
            upd = _bdot(v, t["e"].astype(BF16), 1, 1)
            st = state[h]
            for c in range(n):
                st_ref[h, c] = st
                st = st * t["d"][c] + upd[c]
            state[h] = st
            o = _bdot(t["p"].astype(BF16), v, 2, 1) + _bdot(t["a"].astype(BF16), st_ref[h].astype(BF16), 2, 2)
            rstd = lax.rsqrt(jnp.mean(o * o, axis=-1, keepdims=True) + EPS)
            o_ref[:, cs] = (o * rstd).reshape(tr, HD).astype(BF16)
            rstd_ref[:, cs] = jnp.broadcast_to(rstd, (n, C, HD)).reshape(tr, HD)

    blk = lambda off: pl.BlockSpec((tr, W), lambda g, i: (i, off + g))
    return pl.pallas_call(
        body, name="hgrn_fwd", grid=(HH // hb, S // tr),
        in_specs=[blk(0), blk(0), blk(2 * EMIX // W), pl.BlockSpec((2, W), lambda g, i: (0, g))],
        out_specs=[blk(0), blk(0), pl.BlockSpec((hb, n, HD, HD), lambda g, i: (g, i, 0, 0))],
        out_shape=[jax.ShapeDtypeStruct((S, EB), BF16), jax.ShapeDtypeStruct((S, EMIX), F32),
                   jax.ShapeDtypeStruct((HH, S // C, HD, HD), F32)],
        scratch_shapes=[pltpu.VMEM((hb, HD, HD), F32)],
        compiler_params=_params("parallel", "arbitrary"),
    )(proj, fgate, proj, hgrn_lb)


def _hgrn_bwd(dpremix, premix, rstd, states, proj, fgate, hgrn_lb, S, EMIX):
    HD, C = HG_HEAD_DIM, HG_CHUNK
    HH = EMIX // HD
    hb = 2 if HH % 2 == 0 else 1
    W = hb * HD
    tr = _tile(S, 512)
    n = tr // C
    nrt = S // tr

    def body(do_ref, on_ref, rstd_ref, st_ref, q_ref, f_ref, i_ref, lb_ref, d_ref, dlb_ref, dstate, dsbuf):
        @pl.when(pl.program_id(1) == 0)
        def _():
            dstate[...] = jnp.zeros_like(dstate)
            dlb_ref[...] = jnp.zeros_like(dlb_ref)

        lb = _hgrn_lb(lb_ref)
        for h in range(hb):
            cs = slice(h * HD, (h + 1) * HD)
            qin = q_ref[:, cs].astype(F32).reshape(n, C, HD)
            fin = f_ref[:, cs].reshape(n, C, HD)
            v = i_ref[:, cs].reshape(n, C, HD)
            lbh = lb[:, cs]
            t = _hgrn_chunks(qin, fin, lbh, n)
            a, bm, e, d, p = t["a"], t["bm"], t["e"], t["d"], t["p"]
            ab, bmb, eb16 = a.astype(BF16), bm.astype(BF16), e.astype(BF16)
            on = on_ref[:, cs].astype(F32).reshape(n, C, HD)
            dn = do_ref[:, cs].astype(F32).reshape(n, C, HD)
            do = rstd_ref[:, cs].reshape(n, C, HD) * (dn - on * jnp.mean(dn * on, axis=-1, keepdims=True))
            dob = do.astype(BF16)
            grow = _bdot(dob, ab, 1, 1)
            ds = dstate[h]
            for c in reversed(range(n)):
                dsbuf[h, c] = ds
                ds = ds * d[c] + grow[c]
            dstate[h] = ds
            dst = dsbuf[h]
            st = st_ref[h]
            dstb = dst.astype(BF16)
            dp = jnp.where(t["causal"], _bdot(dob, v, 2, 2), 0.0).astype(BF16)
            dv = _bdot(p.astype(BF16), dob, 1, 1) + _bdot(eb16, dstb, 2, 2)
            da = _bdot(dp, bmb, 2, 1) + _bdot(dob, st.astype(BF16), 2, 1)
            dbm = _bdot(dp, ab, 1, 1)
            de = _bdot(v, dstb, 2, 1)
            dd = jnp.sum(dst * st, axis=1, keepdims=True)
            dk = dbm * t["einv"] + de * t["eend"]
            dee = de * e
            db = da * a - dbm * bm - dee
            extra = jnp.sum(dee, axis=1, keepdims=True) + dd * d
            upper = jnp.where(lax.broadcasted_iota(jnp.int32, (n, C, C), 2)
                              >= lax.broadcasted_iota(jnp.int32, (n, C, C), 1), 1.0, 0.0)
            dg = _tri_sum(upper, db) + extra
            df = dg / t["f"] - dk
            sg, sq = t["sg"], t["sq"]
            dq = da * t["eb"] * (HD ** -0.5) * (sq * (1.0 + qin * (1.0 - sq)))
            d_ref[0, :, cs] = dq.reshape(tr, HD).astype(BF16)
            d_ref[1, :, cs] = (df * (1.0 - lbh) * sg * (1.0 - sg)).reshape(tr, HD).astype(BF16)
            d_ref[2, :, cs] = dv.reshape(tr, HD).astype(BF16)
            dlb_ref[:, cs] += jnp.sum((df * (1.0 - sg)).reshape(tr, HD), axis=0, keepdims=True)

    rev = lambda off: pl.BlockSpec((tr, W), lambda g, s: (nrt - 1 - s, off + g))
    return pl.pallas_call(
        body, name="hgrn_bwd", grid=(HH // hb, nrt),
        in_specs=[rev(0), rev(0), rev(0), pl.BlockSpec((hb, n, HD, HD), lambda g, s: (g, nrt - 1 - s, 0, 0)),
                  rev(0), rev(0), rev(2 * EMIX // W), pl.BlockSpec((2, W), lambda g, s: (0, g))],
        out_specs=[pl.BlockSpec((3, tr, W), lambda g, s: (0, nrt - 1 - s, g)), pl.BlockSpec((1, W), lambda g, s: (0, g))],
        out_shape=[jax.ShapeDtypeStruct((3, S, EMIX), BF16), jax.ShapeDtypeStruct((1, EMIX), F32)],
        scratch_shapes=[pltpu.VMEM((hb, HD, HD), F32), pltpu.VMEM((hb, n, HD, HD), F32)],
        compiler_params=_params("parallel", "arbitrary"),
    )(dpremix, premix, rstd, states, proj, fgate, proj, hgrn_lb)


EW_BLOCK_ELEMS = 512 * 1024


def _ew_tiles(R, C):
    tc = C if C <= 4096 else _tile(C, 2048)
    tr = _tile(R, 512)
    while tr * tc > EW_BLOCK_ELEMS and tr % 16 == 0:
        tr //= 2
    return tr, tc


def _add_halves(name, core_chip, grad, got):
    _, _, R, C = grad.shape
    tr, tc = _ew_tiles(R, C)

    def body(c_ref, a_ref, b_ref, o_ref, own_ref):
        r = (a_ref[...].astype(F32) + b_ref[...].astype(F32)).astype(BF16)
        o_ref[...] = r

        @pl.when(pl.program_id(2) == c_ref[1])
        def _():
            own_ref[...] = r

    blk = pl.BlockSpec((None, tr, tc), lambda i, j, s, c: (s, i, j))
    sds = jax.ShapeDtypeStruct(got.shape, BF16)
    return pl.pallas_call(
        body, name=name, out_shape=[sds, sds],
        grid_spec=pltpu.PrefetchScalarGridSpec(
            num_scalar_prefetch=1, grid=(R // tr, C // tc, N_CHIPS),
            in_specs=[pl.BlockSpec((None, None, tr, tc), lambda i, j, s, c: (s, c[0], i, j)), blk],
            out_specs=[blk, pl.BlockSpec((None, tr, tc), lambda i, j, s, c: (c[1], i, j))]),
        compiler_params=_params("parallel", "parallel", "arbitrary"),
    )(core_chip, grad, got)


def _sum_slots(name, core, parts):
    _, R, C = parts.shape
    tr, tc = _ew_tiles(R, C)

    def body(c_ref, p_ref, o_ref):
        acc = p_ref[0].astype(F32)
        for s in range(1, N_CHIPS):
            acc = acc + p_ref[s].astype(F32)
        o_ref[...] = acc

    return pl.pallas_call(
        body, name=name, out_shape=jax.ShapeDtypeStruct((2, R, C), F32),
        grid_spec=pltpu.PrefetchScalarGridSpec(
            num_scalar_prefetch=1, grid=(R // tr, C // tc),
            in_specs=[pl.BlockSpec((N_CHIPS, tr, tc), lambda i, j, c: (0, i, j))],
            out_specs=pl.BlockSpec((None, tr, tc), lambda i, j, c: (c[0], i, j))),
        compiler_params=_params("parallel", "parallel"),
    )(core, parts)


def _adamw(name, w, g, m, v):
    R, C = w.shape
    tr, tc = _ew_tiles(R, C)

    def body(w_ref, g_ref, m_ref, v_ref, d_ref, mo_ref, vo_ref):
        g = g_ref[...]
        mn = ADAM_B1 * m_ref[...] + (1.0 - ADAM_B1) * g
        vn = ADAM_B2 * v_ref[...] + (1.0 - ADAM_B2) * (g * g)
        m_hat = mn / (1.0 - ADAM_B1 ** ADAM_STEP)
        v_hat = vn / (1.0 - ADAM_B2 ** ADAM_STEP)
        d_ref[...] = -ADAM_LR * (m_hat / (jnp.sqrt(v_hat) + ADAM_EPS) + ADAM_WD * w_ref[...])
        mo_ref[...] = mn
        vo_ref[...] = vn

    blk = pl.BlockSpec((tr, tc), lambda i, j: (i, j))
    sds = jax.ShapeDtypeStruct((R, C), F32)
    return pl.pallas_call(
        body, name=name, grid=(R // tr, C // tc), in_specs=[blk] * 4, out_specs=[blk] * 3, out_shape=[sds] * 3,
        compiler_params=_params("parallel", "parallel"),
    )(w, g, m, v)


def _pack_rows(name, vecs, W):
    nv = len(vecs)

    def body(*refs):
        o_ref = refs[nv]
        o_ref[...] = jnp.zeros_like(o_ref)
        for i in range(nv):
            o_ref[i:i + 1, 0:vecs[i].shape[1]] = refs[i][...]

    vm = pl.BlockSpec(memory_space=pltpu.VMEM)
    return pl.pallas_call(
        body, name=name, in_specs=[vm] * nv, out_specs=vm, out_shape=jax.ShapeDtypeStruct((SMALL_ROWS, W), F32),
    )(*vecs)


def _small_sum(gathered, hgrn_lb, lb_row):
    _, T, W = gathered.shape

    def body(g_ref, lb_ref, o_ref):
        acc = g_ref[0]
        for dev in range(1, N_DEV):
            acc = acc + g_ref[dev]
        o_ref[0:T, :] = acc
        lb = _hgrn_lb(lb_ref)
        d1 = o_ref[lb_row:lb_row + 1, :] * (lb * (1.0 - lb))
        o_ref[T:2 * T, :] = jnp.zeros((T, W), F32)
        o_ref[T:T + 1, :] = -d1
        o_ref[T + 1:T + 2, :] = d1

    vm = pl.BlockSpec(memory_space=pltpu.VMEM)
    return pl.pallas_call(
        body, name="small_sum", in_specs=[vm, vm], out_specs=vm, out_shape=jax.ShapeDtypeStruct((2 * T, W), F32),
    )(gathered, hgrn_lb)


def _place():
    return lax.axis_index("x"), lax.axis_index("y"), lax.axis_index("c")


def _other_chips(x, y):
    return [(1 - x, y), (x, 1 - y), (1 - x, 1 - y)]


def _copies_now(name, srcs, dst_shapes, plan, ncopies):
    def body(*refs):
        ns = len(srcs)
        send, recv = refs[ns + len(dst_shapes):]
        copies = []
        for i, (src, dst, dev) in enumerate(plan(refs[:ns], refs[ns:ns + len(dst_shapes)])):
            cp = pltpu.make_async_remote_copy(src_ref=src, dst_ref=dst, send_sem=send.at[i], recv_sem=recv.at[i],
                                              device_id=dev, device_id_type=MESH)
            cp.start()
            copies.append(cp)
        for cp in copies:
            cp.wait()

    return pl.pallas_call(
        body, name=name, in_specs=[ANY] * len(srcs), out_specs=[ANY] * len(dst_shapes), out_shape=dst_shapes,
        scratch_shapes=[pltpu.SemaphoreType.DMA((ncopies,))] * 2,
    )(*srcs)


def _chunk_rows(rows, row_bytes):
    cr = rows
    while cr * row_bytes > STREAM_CHUNK_BYTES and cr % 32 == 0:
        cr //= 2
    return cr


def _stream(pairs, buf, sems, t, peer):
    lsem, ssem, rsem = sems
    n = len(pairs)
    loads, sent = [None] * n, [None] * n

    def load(k):
        slot = k % STREAM_SLOTS
        if k >= STREAM_SLOTS:
            sent[k - STREAM_SLOTS]()
        loads[k] = pltpu.make_async_copy(pairs[k][0], buf.at[slot], lsem.at[t, slot])
        loads[k].start()

    load(0)
    for k in range(n):
        slot = k % STREAM_SLOTS
        if k + 1 < n:
            load(k + 1)
        loads[k].wait()
        if peer is None:
            cp = pltpu.make_async_copy(buf.at[slot], pairs[k][1], ssem.at[t, slot])
            cp.start()
            sent[k] = cp.wait
        else:
            cp = pltpu.make_async_remote_copy(src_ref=buf.at[slot], dst_ref=pairs[k][1], send_sem=ssem.at[t, slot],
                                              recv_sem=rsem.at[t], device_id=peer, device_id_type=MESH)
            cp.start()
            sent[k] = cp.wait_send
    for k in range(max(0, n - STREAM_SLOTS), n):
        sent[k]()


def _stream_scratch(shapes):
    nt = len(shapes)
    return ([pltpu.VMEM((STREAM_SLOTS,) + s, d) for s, d in shapes]
            + [pltpu.SemaphoreType.DMA((nt, STREAM_SLOTS)), pltpu.SemaphoreType.DMA((nt, STREAM_SLOTS)),
               pltpu.SemaphoreType.DMA((nt,))])


def _exchange_halves(name, grads):
    nt = len(grads)
    hs = [g.shape[1] // 2 for g in grads]
    crs = [_chunk_rows(h, g.shape[2] * g.dtype.itemsize) for h, g in zip(hs, grads)]

    def body(*refs):
        ins, gots, bufs, sems = refs[:nt], refs[nt:2 * nt], refs[2 * nt:3 * nt], refs[3 * nt:]
        x, y, c = _place()
        sib = (x, y, 1 - c)
        for t in range(nt):
            h, cr = hs[t], crs[t]
            pairs = [(ins[t].at[b, pl.ds((1 - c) * h + r0, cr)], gots[t].at[b, pl.ds(r0, cr)])
                     for b in range(N_CHIPS) for r0 in range(0, h, cr)]
            _stream(pairs, bufs[t], sems, t, sib)
        for t in range(nt):
            pltpu.make_async_remote_copy(src_ref=gots[t], dst_ref=gots[t], send_sem=sems[1].at[t, 0],
                                         recv_sem=sems[2].at[t], device_id=sib, device_id_type=MESH).wait_recv()

    return pl.pallas_call(
        body, name=name, in_specs=[ANY] * nt, out_specs=[ANY] * nt,
        out_shape=[jax.ShapeDtypeStruct((N_CHIPS, h, g.shape[2]), g.dtype) for h, g in zip(hs, grads)],
        scratch_shapes=_stream_scratch([((cr, g.shape[2]), g.dtype) for cr, g in zip(crs, grads)]),
        compiler_params=pltpu.CompilerParams(vmem_limit_bytes=VMEM_LIMIT_BYTES),
    )(*grads)


def _scatter_plan(srcs, dsts):
    x, y, c = _place()
    me = 2 * x + y
    return [(srcs[t].at[2 * px + py], dsts[t].at[me], (px, py, c))
            for t in range(len(srcs)) for px, py in _other_chips(x, y)]


def _slot(dst, chip, r0, rows, cols):
    if len(dst.shape) == 3:
        return dst.at[chip, pl.ds(r0, rows)]
    return dst.at[pl.ds(r0, rows), pl.ds(pl.multiple_of(chip * cols, 128), cols)]


def _gather_plan(srcs, dsts):
    x, y, c = _place()
    me = 2 * x + y
    plan = []
    for t in range(len(srcs)):
        h, cols = srcs[t].shape[0] // 2, srcs[t].shape[1]
        plan += [(srcs[t].at[pl.ds(c * h, h)], _slot(dsts[t], me, c * h, h, cols), (px, py, c))
                 for px, py in _other_chips(x, y)]
    return plan


HBM_SPEC = pl.BlockSpec(memory_space=pltpu.HBM)
SEM_SPEC = pl.BlockSpec(memory_space=pltpu.SEMAPHORE)


def _split_start(name, srcs, dsts, plan, ncopies, after):
    bufs = [pltpu.with_memory_space_constraint(a, pltpu.HBM) for a in list(srcs) + list(dsts)]
    nb, ns = len(bufs), len(srcs)
    operands = bufs + ([after] if after is not None else [])

    def body(*refs):
        outs = refs[len(operands):]
        send, recv, token = outs[0], outs[1], outs[-1]
        for i, (src, dst, dev) in enumerate(plan(refs[:ns], refs[ns:nb])):
            pltpu.make_async_remote_copy(src_ref=src, dst_ref=dst, send_sem=send.at[i], recv_sem=recv.at[i],
                                         device_id=dev, device_id_type=MESH).start()
        token[...] = jnp.zeros_like(token)

    res = pl.pallas_call(
        body, name=name,
        out_shape=[pltpu.SemaphoreType.DMA((ncopies,)), pltpu.SemaphoreType.DMA((ncopies,))]
        + [pltpu.HBM(a.shape, a.dtype) for a in bufs] + [jax.ShapeDtypeStruct((8, 128), F32)],
        in_specs=[HBM_SPEC] * nb + [ANY] * (len(operands) - nb),
        out_specs=[SEM_SPEC, SEM_SPEC] + [HBM_SPEC] * nb + [pl.BlockSpec(memory_space=pltpu.VMEM)],
        input_output_aliases={i: 2 + i for i in range(nb)},
        compiler_params=pltpu.CompilerParams(has_side_effects=pltpu.SideEffectType.DATAFLOW_SIDE_EFFECTING),
    )(*operands)
    return res[:-1], res[-1]


def _split_wait(name, started, plan, ns, after):
    send, recv, bufs = started[0], started[1], list(started[2:])
    nb = len(bufs)

    def body(*refs):
        send_ref, recv_ref = refs[nb], refs[nb + 1]
        for i, (src, dst, dev) in enumerate(plan(refs[:ns], refs[ns:nb])):
            cp = pltpu.make_async_remote_copy(src_ref=src, dst_ref=dst, send_sem=send_ref.at[i], recv_sem=recv_ref.at[i],
                                              device_id=dev, device_id_type=MESH)
            cp.wait_send()
            cp.wait_recv()

    res = pl.pallas_call(
        body, name=name, out_shape=[pltpu.HBM(a.shape, a.dtype) for a in bufs],
        in_specs=[HBM_SPEC] * nb + [SEM_SPEC, SEM_SPEC, ANY], out_specs=[HBM_SPEC] * nb,
        input_output_aliases={i: i for i in range(nb)},
        compiler_params=pltpu.CompilerParams(has_side_effects=pltpu.SideEffectType.DATAFLOW_SIDE_EFFECTING),
    )(*bufs, send, recv, after)
    return res[:ns], res[ns:]


def _gather_finish(name, shards, stacks):
    nt = len(shards)
    hs = [s.shape[0] // 2 for s in shards]
    crs = [_chunk_rows(h, s.shape[1] * s.dtype.itemsize) for h, s in zip(hs, shards)]

    def body(*refs):
        ins, outs, bufs, sems = refs[:nt], refs[2 * nt:3 * nt], refs[3 * nt:4 * nt], refs[4 * nt:]
        x, y, c = _place()
        me = 2 * x + y
        sib = (x, y, 1 - c)
        for t in range(nt):
            h, cr, cols = hs[t], crs[t], shards[t].shape[1]
            passed = [_slot(outs[t], 2 * px + py, c * h + r0, cr, cols)
                      for px, py in _other_chips(x, y) for r0 in range(0, h, cr)]
            _stream([(r, r) for r in passed], bufs[t], sems, t, sib)
            own = [(ins[t].at[pl.ds(r0, cr)], _slot(outs[t], me, r0, cr, cols)) for r0 in range(0, 2 * h, cr)]
            _stream(own, bufs[t], sems, t, None)
        for t in range(nt):
            if len(stacks[t].shape) == 3:
                three = outs[t].at[pl.ds(0, 3), pl.ds(0, hs[t])]
            else:
                three = outs[t].at[pl.ds(0, hs[t]), pl.ds(0, 3 * shards[t].shape[1])]
            pltpu.make_async_remote_copy(src_ref=three, dst_ref=three, send_sem=sems[1].at[t, 0],
                                         recv_sem=sems[2].at[t], device_id=sib, device_id_type=MESH).wait_recv()

    return pl.pallas_call(
        body, name=name, in_specs=[ANY] * (2 * nt), out_specs=[ANY] * nt,
        out_shape=[jax.ShapeDtypeStruct(s.shape, s.dtype) for s in stacks],
        scratch_shapes=_stream_scratch([((cr, s.shape[1]), s.dtype) for cr, s in zip(crs, shards)]),
        input_output_aliases={nt + t: t for t in range(nt)},
        compiler_params=pltpu.CompilerParams(vmem_limit_bytes=VMEM_LIMIT_BYTES),
    )(*shards, *stacks)


def _share_halves(fulls):
    nt = len(fulls)
    crs = [_chunk_rows(f.shape[1], f.shape[2] * f.dtype.itemsize) for f in fulls]

    def body(*refs):
        outs, bufs, sems = refs[nt:2 * nt], refs[2 * nt:3 * nt], refs[3 * nt:]
        x, y, c = _place()
        sib = (x, y, 1 - c)
        for t in range(nt):
            rows = [outs[t].at[c, pl.ds(r0, crs[t])] for r0 in range(0, fulls[t].shape[1], crs[t])]
            _stream([(r, r) for r in rows], bufs[t], sems, t, sib)
        for t in range(nt):
            other = outs[t].at[1 - c]
            pltpu.make_async_remote_copy(src_ref=other, dst_ref=other, send_sem=sems[1].at[t, 0],
                                         recv_sem=sems[2].at[t], device_id=sib, device_id_type=MESH).wait_recv()

    return pl.pallas_call(
        body, name="share_halves", in_specs=[ANY] * nt, out_specs=[ANY] * nt,
        out_shape=[jax.ShapeDtypeStruct(f.shape, f.dtype) for f in fulls],
        scratch_shapes=_stream_scratch([((cr, f.shape[2]), f.dtype) for cr, f in zip(crs, fulls)]),
        input_output_aliases={t: t for t in range(nt)},
        compiler_params=pltpu.CompilerParams(vmem_limit_bytes=VMEM_LIMIT_BYTES),
    )(*fulls)


def _allgather_small(name, v):
    def body(v_ref, o_ref, send, recv, lsem):
        x, y, c = _place()
        me = 4 * x + 2 * y + c
        loc = pltpu.make_async_copy(v_ref, o_ref.at[me], lsem)
        loc.start()
        copies = []
        for k in range(1, N_DEV):
            px = 1 - x if k & 4 else x
            py = 1 - y if k & 2 else y
            pc = 1 - c if k & 1 else c
            cp = pltpu.make_async_remote_copy(
                src_ref=v_ref, dst_ref=o_ref.at[me], send_sem=send.at[k - 1], recv_sem=recv.at[k - 1],
                device_id=(px, py, pc), device_id_type=MESH)
            cp.start()
            copies.append(cp)
        for cp in copies:
            cp.wait()
        loc.wait()

    vm = pl.BlockSpec(memory_space=pltpu.VMEM)
    return pl.pallas_call(
        body, name=name, in_specs=[vm], out_specs=vm,
        out_shape=jax.ShapeDtypeStruct((N_DEV,) + v.shape, v.dtype),
        scratch_shapes=[pltpu.SemaphoreType.DMA((N_DEV - 1,))] * 2 + [pltpu.SemaphoreType.DMA],
    )(v)


def kernel(x, mem, norm_g, mem_norm_g, w_kv, w_out, pool_w_in, pool_w_grp, pool_scale, hgrn_w_in, hgrn_lb, hgrn_norm_g, final_g, loss_target, m_norm_g, m_mem_norm_g, m_w_kv, m_w_out, m_pool_w_in, m_pool_w_grp, m_pool_scale, m_hgrn_w_in, m_hgrn_lb, m_hgrn_norm_g, m_final_g, v_norm_g, v_mem_norm_g, v_w_kv, v_w_out, v_pool_w_in, v_pool_w_grp, v_pool_scale, v_hgrn_w_in, v_hgrn_lb, v_hgrn_norm_g, v_final_g):
    _, S, D = x.shape
    M = mem.shape[1]
    EB = 2 * D
    ECA = EB // 4
    EMIX = EB - ECA
    PG = EMIX // N_POOL_GROUPS
    NP0 = EMIX + ECA + EB
    NP1 = 3 * EMIX + ECA + EB
    SH0, SH1 = NP0 // N_CHIPS, NP1 // N_CHIPS
    DK, EK = D // N_CHIPS, EB // N_CHIPS
    TNP = 512 if all(v % 512 == 0 for v in (SH0, SH1, ECA, EMIX)) else 256
    TM = _tile(S, 1024)
    TMF = _tile(S, 2048)
    TD = _tile(D, 512)
    TDW = _tile(D, 1024)
    c0, c1 = SH0 // TNP, SH1 // TNP
    qt, et = EMIX // TNP, ECA // TNP
    chip = 2 * lax.axis_index("x") + lax.axis_index("y")

    xs, ms, tgt = x[0], mem[0], loss_target[0]

    flat = lambda w: w.reshape(-1, w.shape[-1])
    shard2d = [flat(w_kv), flat(w_out), flat(pool_w_in), flat(pool_w_grp), flat(hgrn_w_in)]
    bf = lambda w: w.astype(BF16)
    sds = jax.ShapeDtypeStruct
    first = [bf(flat(pool_w_in))]
    wpin, = _gather_finish("gather_pin_finish", first, _copies_now(
        "gather_pin", first, [sds((N_CHIPS,) + first[0].shape, BF16)], _gather_plan, 3))
    stack_of = lambda s: lax.empty((N_CHIPS,) + s.shape, BF16)
    group_b = [bf(w_kv[0]), bf(w_out[0]), bf(flat(pool_w_grp))]
    gather_b, token = _split_start("gather_b_start", group_b, [stack_of(s) for s in group_b], _gather_plan, 9, wpin)
    group_c = [bf(flat(hgrn_w_in))]
    gather_c, token = _split_start("gather_c_start", group_c, [lax.empty((D, NP1), BF16)], _gather_plan, 3, token)
    group_d = [bf(w_kv[1]), bf(w_out[1])]
    gather_d, token = _split_start("gather_d_start", group_d, [stack_of(s) for s in group_d], _gather_plan, 6, token)

    tdk, tek, tew = _tile(DK, 512), _tile(EK, 512), _tile(EK, 1024)

    mem_n = _rms_fwd("rms_mem", ms, mem_norm_g.reshape(1, D))
    tkv = _tile(2 * ECA, 512)

    def kv_of(layer, wkv):
        return _matmul(
            f"kv{layer}", mem_n, wkv, grid=(1, 2 * ECA // tkv, D // tdk),
            a_spec=pl.BlockSpec((M, tdk), lambda i, j, k: (0, k)),
            b_spec=pl.BlockSpec((None, tdk, tkv), lambda i, j, k: (k // (DK // tdk), k % (DK // tdk), j)),
            out_shape=sds((M, 2 * ECA), BF16), out_spec=pl.BlockSpec((M, tkv), lambda i, j, k: (0, j)),
            acc_shape=(M, tkv), dims=NN)

    tko = _tile(EB, 2048)

    def out_proj(layer, branch, wout, resid):
        return _matmul(
            f"out_proj{layer}", branch, wout.reshape(EB, D), grid=(S // TM, D // TDW, EB // tko),
            a_spec=pl.BlockSpec((TM, tko), IK), b_spec=pl.BlockSpec((tko, TDW), KJ),
            out_shape=sds((S, D), F32), out_spec=pl.BlockSpec((TM, TDW), IJ),
            acc_shape=(TM, TDW), dims=NN, add=resid, add_spec=pl.BlockSpec((TM, TDW), IJ))

    ones_ca = jnp.ones((1, ECA), F32)

    h0 = _rms_fwd("rms0", xs, norm_g[0:1] + token[0:1, 0:1])
    proj0 = _matmul(
        "proj0", h0, wpin, grid=(S // TMF, NP0 // TNP, 1),
        a_spec=pl.BlockSpec((TMF, D), lambda i, j, k: (i, 0)),
        b_spec=pl.BlockSpec((None, D, TNP), lambda i, j, k: (j // c0, 0, j % c0)),
        out_shape=sds((S, NP0), BF16), out_spec=pl.BlockSpec((TMF, TNP), IJ),
        acc_shape=(TMF, TNP), dims=NN)
    pooled = _pool_fwd(proj0, S, EMIX)
    wkv0, wout0, g_grp = _gather_finish("gather_b_finish", *_split_wait("gather_b_wait", gather_b, _gather_plan, 3, pooled))
    wgrp = g_grp.reshape(N_CHIPS, N_POOL_GROUPS, PG // N_CHIPS, PG).transpose(1, 0, 2, 3).reshape(N_POOL_GROUPS, PG, PG)
    kv = [kv_of(0, wkv0), None]
    premix0 = _matmul(
        "pool_grp", pooled, wgrp, grid=(S // TM, N_POOL_GROUPS, 1),
        a_spec=pl.BlockSpec((TM, PG), lambda i, j, k: (i, j)),
        b_spec=pl.BlockSpec((None, PG, PG), lambda i, j, k: (j, 0, 0)),
        out_shape=sds((S, EB), BF16), out_spec=pl.BlockSpec((TM, PG), lambda i, j, k: (i, j)),
        acc_shape=(TM, PG), dims=NN)
    premix0 = _ca_fwd("ca_fwd0", proj0, EMIX // ECA, kv[0], premix0, S, ECA, EMIX)
    colscale0 = jnp.concatenate([pool_scale.reshape(1, EMIX), ones_ca], axis=1)
    gblk0 = (EMIX + ECA) // ECA
    branch0 = _gate_fwd("gate_fwd0", premix0, proj0, gblk0, colscale0, S, EB, ECA)
    x1 = out_proj(0, branch0, wout0, xs)

    whin, = _gather_finish("gather_c_finish", *_split_wait("gather_c_wait", gather_c, _gather_plan, 1, x1))
    h1 = _rms_fwd("rms1", x1, norm_g[1:2])

    def proj1_cols(name, ncols, col_of, out_cols, out_dtype, out_col_of):
        return _matmul(
            name, h1, whin, grid=(S // TMF, ncols, 1),
            a_spec=pl.BlockSpec((TMF, D), lambda i, j, k: (i, 0)),
            b_spec=pl.BlockSpec((D, TNP), lambda i, j, k: (0, col_of(j))),
            out_shape=sds((S, out_cols), out_dtype), out_spec=pl.BlockSpec((TMF, TNP), lambda i, j, k: (i, out_col_of(j))),
            acc_shape=(TMF, TNP), dims=NN)

    skip_f = lambda j: jnp.where(j < qt, j, j + qt)
    proj1 = proj1_cols("proj1", NP1 // TNP - qt, skip_f, NP1, BF16, skip_f)
    fgate = proj1_cols("proj1_f", qt, lambda j: j + qt, EMIX, F32, lambda j: j)
    premix1, rstd1, states = _hgrn_fwd(proj1, fgate, hgrn_lb, S, EMIX, EB)
    wkv1, wout1 = _gather_finish("gather_d_finish", *_split_wait("gather_d_wait", gather_d, _gather_plan, 2, rstd1))
    kv[1] = kv_of(1, wkv1)
    premix1 = _ca_fwd("ca_fwd1", proj1, 3 * EMIX // ECA, kv[1], premix1, S, ECA, EMIX)
    norm_tiles = _allgather_small("allgather_norm_g", jnp.pad(hgrn_norm_g, ((0, SMALL_ROWS - 1), (0, 0))))
    hg_norm = norm_tiles[0::2, 0, :].reshape(1, EMIX)
    colscale1 = jnp.concatenate([hg_norm, ones_ca], axis=1)
    gblk1 = (3 * EMIX + ECA) // ECA
    branch1 = _gate_fwd("gate_fwd1", premix1, proj1, gblk1, colscale1, S, EB, ECA)
    x2 = out_proj(1, branch1, wout1, x1)

    dx2, dx2b, d_final_g, loss_part = _loss_head(x2, final_g.reshape(1, D), tgt)

    def out_proj_bwd(layer, dxb, branch, wout):
        dbranch = _matmul(
            f"dbranch{layer}", dxb, wout, grid=(S // TMF, EB // tek, 1),
            a_spec=pl.BlockSpec((TMF, D), lambda i, j, k: (i, 0)),
            b_spec=pl.BlockSpec((None, tek, D), lambda i, j, k: (j // (EK // tek), j % (EK // tek), 0)),
            out_shape=sds((S, EB), BF16), out_spec=pl.BlockSpec((TMF, tek), IJ), acc_shape=(TMF, tek), dims=NT)
        dw = _matmul(
            f"dwout{layer}", branch, dxb, grid=(EB // tew, D // TD, 1),
            a_spec=pl.BlockSpec((S, tew), lambda i, j, k: (0, i)), b_spec=pl.BlockSpec((S, TD), lambda i, j, k: (0, j)),
            out_shape=sds((N_CHIPS, EK, D), BF16),
            out_spec=pl.BlockSpec((None, tew, TD), lambda i, j, k: (i // (EK // tew), i % (EK // tew), j)),
            acc_shape=(tew, TD), dims=TN)
        return dbranch, dw

    def kv_bwd(layer, dkv, wkv, dmem_add):
        dkvb = dkv.astype(BF16)
        dmem = _matmul(
            f"dmem{layer}", dkvb, wkv, grid=(1, D // tdk, 2 * ECA // tkv),
            a_spec=pl.BlockSpec((M, tkv), lambda i, j, k: (0, k)),
            b_spec=pl.BlockSpec((None, tdk, tkv), lambda i, j, k: (j // (DK // tdk), j % (DK // tdk), k)),
            out_shape=sds((M, D), F32), out_spec=pl.BlockSpec((M, tdk), lambda i, j, k: (0, j)), acc_shape=(M, tdk),
            dims=NT, add=dmem_add, add_spec=pl.BlockSpec((M, tdk), lambda i, j, k: (0, j)))
        dw = _matmul(
            f"dwkv{layer}", mem_n, dkvb, grid=(D // tdk, 2 * ECA // tkv, 1),
            a_spec=pl.BlockSpec((M, tdk), lambda i, j, k: (0, i)), b_spec=pl.BlockSpec((M, tkv), lambda i, j, k: (0, j)),
            out_shape=sds((N_CHIPS, DK, 2 * ECA), BF16),
            out_spec=pl.BlockSpec((None, tdk, tkv), lambda i, j, k: (i // (DK // tdk), i % (DK // tdk), j)),
            acc_shape=(tdk, tkv), dims=TN)
        return dmem, dw

    dbranch1, gw_out1 = out_proj_bwd(1, dx2b, branch1, wout1)
    dpremix1, drest1, dcol1 = _gate_bwd("gate_bwd1", dbranch1, premix1, proj1, gblk1, colscale1, (S, ECA + EB), 1,
                                        S, EB, ECA)
    drest1, dkv1 = _ca_bwd("ca_bwd1", dpremix1, proj1, 3 * EMIX // ECA, kv[1], drest1, 0, S, ECA, EMIX)
    dqfi, dlb = _hgrn_bwd(dpremix1, premix1, rstd1, states, proj1, fgate, hgrn_lb, S, EMIX)
    nq, nr = 3 * qt, (ECA + EB) // TNP
    tkh = _tile(EMIX, 1024) if (ECA + EB) % _tile(EMIX, 1024) == 0 else TNP
    kq = EMIX // tkh
    dh1 = _matmul(
        "dh1_qfi", dqfi, whin, grid=(S // TM, D // TDW, 3),
        a_spec=pl.BlockSpec((None, TM, EMIX), lambda i, j, k: (k, i, 0)),
        b_spec=pl.BlockSpec((TDW, EMIX), lambda i, j, k: (j, k)),
        out_shape=sds((S, D), F32), out_spec=pl.BlockSpec((TM, TDW), IJ), acc_shape=(TM, TDW), dims=NT)
    dh1 = _matmul(
        "dh1_rest", drest1, whin, grid=(S // TM, D // TDW, (ECA + EB) // tkh), a_spec=pl.BlockSpec((TM, tkh), IK),
        b_spec=pl.BlockSpec((TDW, tkh), lambda i, j, k: (j, k + 3 * kq)),
        out_shape=sds((S, D), F32), out_spec=pl.BlockSpec((TM, TDW), IJ), acc_shape=(TM, TDW), dims=NT,
        add=dh1, add_spec=pl.BlockSpec((TM, TDW), IJ))
    gw_hin = _matmul(
        "dwhin_qfi", h1, dqfi, grid=(D // TDW, nq, 1), a_spec=pl.BlockSpec((S, TDW), lambda i, j, k: (0, i)),
        b_spec=pl.BlockSpec((None, S, TNP), lambda i, j, k: (j // qt, 0, j % qt)),
        out_shape=sds((N_CHIPS, D, SH1), BF16), out_spec=pl.BlockSpec((None, TDW, TNP), lambda i, j, k: (j // c1, i, j % c1)),
        acc_shape=(TDW, TNP), dims=TN)
    gw_hin = _matmul(
        "dwhin_rest", h1, drest1, grid=(D // TDW, nr, 1), a_spec=pl.BlockSpec((S, TDW), lambda i, j, k: (0, i)),
        b_spec=pl.BlockSpec((S, TNP), lambda i, j, k: (0, j)), out_shape=sds((N_CHIPS, D, SH1), BF16),
        out_spec=pl.BlockSpec((None, TDW, TNP), lambda i, j, k: ((j + nq) // c1, i, (j + nq) % c1)),
        acc_shape=(TDW, TNP), dims=TN, alias=gw_hin)
    dmem, gw_kv1 = kv_bwd(1, dkv1, wkv1, None)

    core_chip = jnp.stack([lax.axis_index("c"), chip]).astype(jnp.int32)

    def reduce_in_chip(tag, stacks):
        got = _exchange_halves(f"exchange_halves{tag}", stacks)
        pairs = [_add_halves(f"add_halves{tag}_{t}", core_chip, g.reshape(N_CHIPS, 2, g.shape[1] // 2, g.shape[2]), r)
                 for t, (g, r) in enumerate(zip(stacks, got))]
        return [p for p, _ in pairs], [own for _, own in pairs]

    parts1, landed1 = reduce_in_chip(1, [gw_kv1, gw_out1, gw_hin])
    scatter1, token1 = _split_start("scatter1_start", parts1, landed1, _scatter_plan, 3 * len(parts1), None)
    dx1, dx1b, d_ng1 = _rms_bwd("rms_bwd1", dh1, x1, norm_g[1:2] + token1[0:1, 0:1], dx2)

    dbranch0, gw_out0 = out_proj_bwd(0, dx1b, branch0, wout0)
    dpremix0, dproj0, dcol0 = _gate_bwd("gate_bwd0", dbranch0, premix0, proj0, gblk0, colscale0, (S, NP0), gblk0,
                                        S, EB, ECA)
    dproj0, dkv0 = _ca_bwd("ca_bwd0", dpremix0, proj0, EMIX // ECA, kv[0], dproj0, EMIX // ECA, S, ECA, EMIX)
    dmem, gw_kv0 = kv_bwd(0, dkv0, wkv0, dmem)
    parts_a, landed_a = reduce_in_chip("0a", [gw_kv0, gw_out0])
    scatter_a, token_a = _split_start("scatter0a_start", parts_a, landed_a, _scatter_plan, 3 * len(parts_a), None)
    dpooled = _matmul(
        "dpooled", dpremix0, wgrp, grid=(S // TM, N_POOL_GROUPS, 1), a_spec=pl.BlockSpec((TM, PG), IJ),
        b_spec=pl.BlockSpec((None, PG, PG), lambda i, j, k: (j, 0, 0)),
        out_shape=sds((S, EMIX), F32), out_spec=pl.BlockSpec((TM, PG), IJ), acc_shape=(TM, PG), dims=NT, after=token_a)
    dwgrp = _matmul(
        "dwgrp", pooled, dpremix0, grid=(N_POOL_GROUPS, 1, 1), a_spec=pl.BlockSpec((S, PG), lambda i, j, k: (0, i)),
        b_spec=pl.BlockSpec((S, PG), lambda i, j, k: (0, i)), out_shape=sds((N_POOL_GROUPS, PG, PG), F32),
        out_spec=pl.BlockSpec((None, PG, PG), lambda i, j, k: (i, 0, 0)), acc_shape=(PG, PG), dims=TN)
    dproj0 = _pool_bwd(dpooled, dproj0, S, EMIX)
    gw_pin = _matmul(
        "dwpin", h0, dproj0, grid=(D // TDW, NP0 // TNP, 1), a_spec=pl.BlockSpec((S, TDW), lambda i, j, k: (0, i)),
        b_spec=pl.BlockSpec((S, TNP), lambda i, j, k: (0, j)), out_shape=sds((N_CHIPS, D, SH0), BF16),
        out_spec=pl.BlockSpec((None, TDW, TNP), lambda i, j, k: (j // c0, i, j % c0)), acc_shape=(TDW, TNP), dims=TN)
    gw_grp = dwgrp.reshape(N_POOL_GROUPS, N_CHIPS, PG // N_CHIPS, PG).transpose(1, 0, 2, 3).reshape(N_CHIPS, PG, PG)
    parts_b, landed_b = reduce_in_chip("0b", [gw_pin, gw_grp.astype(BF16)])
    scatter_b, token_b = _split_start("scatter0b_start", parts_b, landed_b, _scatter_plan, 3 * len(parts_b), None)
    dh0 = _matmul(
        "dh0", dproj0, wpin, grid=(S // TM, D // TDW, N_CHIPS), a_spec=pl.BlockSpec((TM, SH0), IK),
        b_spec=pl.BlockSpec((None, TDW, SH0), lambda i, j, k: (k, j, 0)),
        out_shape=sds((S, D), F32), out_spec=pl.BlockSpec((TM, TDW), IJ), acc_shape=(TM, TDW), dims=NT, after=token_b)
    grad_x, _, d_ng0 = _rms_bwd("rms_bwd0", dh0, xs, norm_g[0:1], dx1)
    _, _, d_mng = _rms_bwd("rms_bwd_mem", dmem, ms, mem_norm_g.reshape(1, D), jnp.zeros_like(ms))

    _, landed1 = _split_wait("scatter1_wait", scatter1, _scatter_plan, len(parts1), grad_x)
    _, landed_a = _split_wait("scatter0a_wait", scatter_a, _scatter_plan, len(parts_a), grad_x)
    _, landed_b = _split_wait("scatter0b_wait", scatter_b, _scatter_plan, len(parts_b), grad_x)
    landed = [landed_a[0], landed1[0], landed_a[1], landed1[1], landed_b[0], landed_b[1], landed1[2]]
    fulls = _share_halves([_sum_slots(f"sum_slots{t}", core_chip, p) for t, p in enumerate(landed)])
    f2 = [f.reshape(-1, f.shape[-1]) for f in fulls]
    big_g = [jnp.concatenate(f2[0:2], axis=0), jnp.concatenate(f2[2:4], axis=0), f2[4], f2[5], f2[6]]
    big_names = ["w_kv", "w_out", "pool_w_in", "pool_w_grp", "hgrn_w_in"]
    big_m = [flat(a) for a in (m_w_kv, m_w_out, m_pool_w_in, m_pool_w_grp, m_hgrn_w_in)]
    big_v = [flat(a) for a in (v_w_kv, v_w_out, v_pool_w_in, v_pool_w_grp, v_hgrn_w_in)]
    big_shapes = [w_kv.shape, w_out.shape, pool_w_in.shape, pool_w_grp.shape, hgrn_w_in.shape]
    grads, deltas, new_m, new_v = {}, {}, {}, {}
    for t, n in enumerate(big_names):
        d, mn, vn = _adamw(f"adamw_{n}", shard2d[t], big_g[t], big_m[t], big_v[t])
        grads[n], deltas[n] = big_g[t].reshape(big_shapes[t]), d.reshape(big_shapes[t])
        new_m[n], new_v[n] = mn.reshape(big_shapes[t]), vn.reshape(big_shapes[t])

    Wd = EMIX
    partial = _pack_rows("pack_partials", [d_ng0, d_ng1, d_mng, dcol0[:, :EMIX], dlb, dcol1[:, :EMIX], d_final_g,
                                           loss_part], Wd)
    summed = _small_sum(_allgather_small("allgather_grads", partial), hgrn_lb, 4)
    row = lambda i, n=Wd: summed[i:i + 1, :n]
    nshard = EMIX // N_CHIPS
    g_hg_norm = lax.dynamic_slice_in_dim(row(5), chip * nshard, nshard, axis=1)
    small_names = ["norm_g0", "norm_g1", "mem_norm_g", "pool_scale", "hgrn_lb0", "hgrn_lb1", "hgrn_norm_g", "final_g"]
    small_w = [norm_g[0:1], norm_g[1:2], mem_norm_g.reshape(1, D), pool_scale, hgrn_lb[0:1], hgrn_lb[1:2], hgrn_norm_g,
               final_g.reshape(1, D)]
    small_m = [m_norm_g[0:1], m_norm_g[1:2], m_mem_norm_g.reshape(1, D), m_pool_scale, m_hgrn_lb[0:1], m_hgrn_lb[1:2],
               m_hgrn_norm_g, m_final_g.reshape(1, D)]
    small_v = [v_norm_g[0:1], v_norm_g[1:2], v_mem_norm_g.reshape(1, D), v_pool_scale, v_hgrn_lb[0:1], v_hgrn_lb[1:2],
               v_hgrn_norm_g, v_final_g.reshape(1, D)]
    g_pack = _pack_rows("pack_small_g", [row(0, D), row(1, D), row(2, D), row(3), row(8), row(9), g_hg_norm, row(6, D)], Wd)
    d_pack, m_pack, v_pack = _adamw("adamw_small", _pack_rows("pack_small_w", small_w, Wd), g_pack,
                                    _pack_rows("pack_small_m", small_m, Wd), _pack_rows("pack_small_v", small_v, Wd))
    widths = [v.shape[1] for v in small_w]
    rows = lambda p: {n: p[i, :widths[i]] for i, n in enumerate(small_names)}

    def assemble(r, out):
        out["norm_g"] = jnp.stack([r["norm_g0"], r["norm_g1"]])
        out["mem_norm_g"] = r["mem_norm_g"]
        out["pool_scale"] = r["pool_scale"].reshape(1, EMIX)
        out["hgrn_lb"] = jnp.stack([r["hgrn_lb0"], r["hgrn_lb1"]])
        out["hgrn_norm_g"] = r["hgrn_norm_g"].reshape(1, nshard)
        out["final_g"] = r["final_g"]

    assemble(rows(g_pack), grads)
    assemble(rows(d_pack), deltas)
    assemble(rows(m_pack), new_m)
    assemble(rows(v_pack), new_v)
    loss = summed[7, 0]

    order = ["norm_g", "mem_norm_g", "w_kv", "w_out", "pool_w_in", "pool_w_grp", "pool_scale", "hgrn_w_in", "hgrn_lb",
             "hgrn_norm_g", "final_g"]
    return (loss, grad_x.reshape(1, S, D), *[grads[n] for n in order], *[deltas[n] for n in order],
            *[new_m[n] for n in order], *[new_v[n] for n in order])
```

```python
import functools

import jax
import jax.numpy as jnp
from jax import lax
from jax.experimental import pallas as pl
from jax.experimental.pallas import tpu as pltpu

F32 = jnp.float32
BF16 = jnp.bfloat16
MESH = pl.DeviceIdType.MESH
ANY = pl.BlockSpec(memory_space=pl.ANY)

EPS = 1e-6
HG_HEAD_DIM = 128
HG_CHUNK = 64
CA_HEADS = 4
N_POOL_GROUPS = 4
POOL_HALO = 128
ADAM_LR = 0.001
ADAM_B1 = 0.9
ADAM_B2 = 0.999
ADAM_EPS = 1e-08
ADAM_WD = 0.01
ADAM_STEP = 10
N_CHIPS = 4
N_DEV = 8
VMEM_LIMIT_BYTES = 56 * 1024 * 1024
SMALL_ROWS = 8
STREAM_CHUNK_BYTES = 2 * 1024 * 1024
STREAM_SLOTS = 3


def _params(*sem):
    return pltpu.CompilerParams(dimension_semantics=sem, vmem_limit_bytes=VMEM_LIMIT_BYTES)


def _tile(n, pref):
    t = pref
    while n % t:
        t //= 2
    return t


def _sigmoid(x):
    return 1.0 / (1.0 + jnp.exp(-x))


def _matmul(name, a, b, *, grid, a_spec, b_spec, out_shape, out_spec, acc_shape, dims,
            add=None, add_spec=None, alias=None, after=None, extras=(), epilogue=None):
    nk = grid[2]
    has_add = add is not None
    has_alias = alias is not None
    has_after = after is not None
    n_out = len(out_shape) if epilogue is not None else 1

    def body(*refs):
        a_ref, b_ref = refs[0], refs[1]
        pos = 2
        add_ref = None
        if has_add:
            add_ref = refs[pos]
            pos += 1
        extra_refs = refs[pos:pos + len(extras)]
        pos += len(extras) + has_alias + has_after
        o_refs = refs[pos:pos + n_out]
        prod = lax.dot_general(a_ref[...], b_ref[...], (dims, ((), ())), preferred_element_type=F32)

        def finish(r):
            if epilogue is not None:
                epilogue(r, extra_refs, o_refs)
                return
            if has_add:
                r = r + add_ref[...].astype(F32)
            o_refs[0][...] = r.astype(o_refs[0].dtype)

        if nk == 1:
            finish(prod)
            return
        acc_ref = refs[pos + n_out]
        k = pl.program_id(2)

        @pl.when(k == 0)
        def _():
            acc_ref[...] = prod

        @pl.when(k > 0)
        def _():
            acc_ref[...] += prod

        @pl.when(k == nk - 1)
        def _():
            finish(acc_ref[...])

    operands = [a, b]
    in_specs = [a_spec, b_spec]
    if has_add:
        operands.append(add)
        in_specs.append(add_spec)
    for arr, spec in extras:
        operands.append(arr)
        in_specs.append(spec)
    aliases = {}
    if has_alias:
        aliases = {len(operands): 0}
        operands.append(alias)
        in_specs.append(ANY)
    if has_after:
        operands.append(after)
        in_specs.append(ANY)
    return pl.pallas_call(
        body, name=name, grid=grid, in_specs=in_specs, out_specs=out_spec, out_shape=out_shape,
        scratch_shapes=[pltpu.VMEM(acc_shape, F32)] if nk > 1 else [], input_output_aliases=aliases,
        compiler_params=_params("parallel", "parallel", "arbitrary"),
    )(*operands)


IJ = lambda i, j, k: (i, j)
IK = lambda i, j, k: (i, k)
KJ = lambda i, j, k: (k, j)
KI = lambda i, j, k: (k, i)
NN = ((1,), (0,))
NT = ((1,), (1,))
TN = ((0,), (0,))


def _rms_fwd(name, x, g):
    R, D = x.shape
    tr = _tile(R, 256)

    def body(x_ref, g_ref, o_ref):
        xf = x_ref[...]
        r = lax.rsqrt(jnp.mean(xf * xf, axis=-1, keepdims=True) + EPS)
        o_ref[...] = (xf * r * g_ref[...]).astype(o_ref.dtype)

    return pl.pallas_call(
        body, name=name, grid=(R // tr,),
        in_specs=[pl.BlockSpec((tr, D), lambda i: (i, 0)), pl.BlockSpec((1, D), lambda i: (0, 0))],
        out_specs=pl.BlockSpec((tr, D), lambda i: (i, 0)),
        out_shape=jax.ShapeDtypeStruct((R, D), BF16), compiler_params=_params("parallel"),
    )(x, g)


def _rms_bwd(name, dh, x, g, dres):
    R, D = x.shape
    tr = _tile(R, 256)

    def body(dh_ref, x_ref, g_ref, dres_ref, dx_ref, dxb_ref, dg_ref):
        xf = x_ref[...]
        r = lax.rsqrt(jnp.mean(xf * xf, axis=-1, keepdims=True) + EPS)
        xn = xf * r
        d = dh_ref[...]
        dyg = d * g_ref[...]
        dx = r * (dyg - xn * jnp.mean(dyg * xn, axis=-1, keepdims=True)) + dres_ref[...]
        dx_ref[...] = dx
        dxb_ref[...] = dx.astype(BF16)

        @pl.when(pl.program_id(0) == 0)
        def _():
            dg_ref[...] = jnp.zeros_like(dg_ref)

        dg_ref[...] += jnp.sum(d * xn, axis=0, keepdims=True)

    row = pl.BlockSpec((tr, D), lambda i: (i, 0))
    vec = pl.BlockSpec((1, D), lambda i: (0, 0))
    return pl.pallas_call(
        body, name=name, grid=(R // tr,), in_specs=[row, row, vec, row], out_specs=[row, row, vec],
        out_shape=[jax.ShapeDtypeStruct((R, D), F32), jax.ShapeDtypeStruct((R, D), BF16),
                   jax.ShapeDtypeStruct((1, D), F32)],
        compiler_params=_params("arbitrary"),
    )(dh, x, g, dres)


def _loss_head(x2, g, target):
    R, D = x2.shape
    tr = _tile(R, 256)

    def body(x_ref, g_ref, t_ref, dx_ref, dxb_ref, dg_ref, loss_ref):
        xf = x_ref[...]
        gg = g_ref[...]
        r = lax.rsqrt(jnp.mean(xf * xf, axis=-1, keepdims=True) + EPS)
        xn = xf * r
        e = xn * gg - t_ref[...]
        part = 0.5 * jnp.sum(jnp.mean(e * e, axis=-1, keepdims=True), axis=0, keepdims=True)
        dy = e * (1.0 / D)
        dyg = dy * gg
        dx = r * (dyg - xn * jnp.mean(dyg * xn, axis=-1, keepdims=True))
        dx_ref[...] = dx
        dxb_ref[...] = dx.astype(BF16)

        @pl.when(pl.program_id(0) == 0)
        def _():
            dg_ref[...] = jnp.zeros_like(dg_ref)
            loss_ref[...] = jnp.zeros_like(loss_ref)

        dg_ref[...] += jnp.sum(dy * xn, axis=0, keepdims=True)
        loss_ref[...] += jnp.broadcast_to(part, loss_ref.shape)

    row = pl.BlockSpec((tr, D), lambda i: (i, 0))
    vec = pl.BlockSpec((1, D), lambda i: (0, 0))
    return pl.pallas_call(
        body, name="loss_head", grid=(R // tr,), in_specs=[row, vec, row],
        out_specs=[row, row, vec, pl.BlockSpec((1, 128), lambda i: (0, 0))],
        out_shape=[jax.ShapeDtypeStruct((R, D), F32), jax.ShapeDtypeStruct((R, D), BF16),
                   jax.ShapeDtypeStruct((1, D), F32), jax.ShapeDtypeStruct((1, 128), F32)],
        compiler_params=_params("arbitrary"),
    )(x2, g, target)


def _pool_band(tr, reverse, w):
    r = lax.broadcasted_iota(jnp.int32, (tr, tr + POOL_HALO), 0)
    c = lax.broadcasted_iota(jnp.int32, (tr, tr + POOL_HALO), 1)
    if reverse:
        inside = (c >= r) & (c < r + w)
    else:
        cc = c - POOL_HALO
        inside = (cc <= r) & (cc > r - w)
    return jnp.where(inside, 1.0, 0.0).astype(BF16)


def _pool_fwd(proj, S, EMIX):
    PG = EMIX // N_POOL_GROUPS
    cb = PG
    tr = _tile(S, 512)
    per_group = PG // cb

    def body(u_ref, o_ref, ext):
        i = pl.program_id(1)
        w = jnp.left_shift(2, pl.program_id(0) // per_group)

        @pl.when(i == 0)
        def _():
            ext[0:POOL_HALO, :] = jnp.zeros((POOL_HALO, cb), BF16)

        u = u_ref[...]
        ext[POOL_HALO:, :] = u
        win = jnp.dot(_pool_band(tr, False, w), ext[...], preferred_element_type=F32)
        pos = i * tr + lax.broadcasted_iota(jnp.int32, (tr, 1), 0)
        cnt = jnp.minimum(pos + 1, w).astype(F32)
        o_ref[...] = (win / cnt - u.astype(F32)).astype(BF16)
        ext[0:POOL_HALO, :] = u[tr - POOL_HALO:, :]

    return pl.pallas_call(
        body, name="pool_fwd", grid=(EMIX // cb, S // tr),
        in_specs=[pl.BlockSpec((tr, cb), lambda j, i: (i, j))],
        out_specs=pl.BlockSpec((tr, cb), lambda j, i: (i, j)),
        out_shape=jax.ShapeDtypeStruct((S, EMIX), BF16),
        scratch_shapes=[pltpu.VMEM((tr + POOL_HALO, cb), BF16)],
        compiler_params=_params("parallel", "arbitrary"),
    )(proj)


def _pool_bwd(dpooled, dproj, S, EMIX):
    PG = EMIX // N_POOL_GROUPS
    cb = PG
    tr = _tile(S, 512)
    per_group = PG // cb
    nrt = S // tr

    def body(d_ref, _, o_ref, ext):
        step = pl.program_id(1)
        i = nrt - 1 - step
        w = jnp.left_shift(2, pl.program_id(0) // per_group)

        @pl.when(step == 0)
        def _():
            ext[tr:, :] = jnp.zeros((POOL_HALO, cb), BF16)

        d = d_ref[...]
        pos = i * tr + lax.broadcasted_iota(jnp.int32, (tr, 1), 0)
        cnt = jnp.minimum(pos + 1, w).astype(F32)
        z = (d / cnt).astype(BF16)
        ext[0:tr, :] = z
        win = jnp.dot(_pool_band(tr, True, w), ext[...], preferred_element_type=F32)
        o_ref[...] = (win - d).astype(BF16)
        ext[tr:, :] = z[0:POOL_HALO, :]

    return pl.pallas_call(
        body, name="pool_bwd", grid=(EMIX // cb, nrt),
        in_specs=[pl.BlockSpec((tr, cb), lambda j, s: (nrt - 1 - s, j)), ANY],
        out_specs=pl.BlockSpec((tr, cb), lambda j, s: (nrt - 1 - s, j)),
        out_shape=jax.ShapeDtypeStruct(dproj.shape, dproj.dtype),
        scratch_shapes=[pltpu.VMEM((tr + POOL_HALO, cb), BF16)],
        input_output_aliases={1: 0},
        compiler_params=_params("parallel", "arbitrary"),
    )(dpooled, dproj)


def _ca_fwd(name, proj, qblk, kv, premix, S, ECA, EMIX):
    M = kv.shape[0]
    hd = ECA // CA_HEADS
    ts = _tile(S, 512)
    scale = hd ** -0.5

    def body(q_ref, kv_ref, _, o_ref):
        for h in range(CA_HEADS):
            q = q_ref[:, h * hd:(h + 1) * hd]
            k = kv_ref[:, h * hd:(h + 1) * hd]
            v = kv_ref[:, ECA + h * hd:ECA + (h + 1) * hd]
            s = lax.dot_general(q, k, (NT, ((), ())), preferred_element_type=F32) * scale
            s = s - jnp.max(s, axis=-1, keepdims=True)
            p = jnp.exp(s)
            p = p / jnp.sum(p, axis=-1, keepdims=True)
            o = jnp.dot(p.astype(BF16), v, preferred_element_type=F32)
            o_ref[:, h * hd:(h + 1) * hd] = o.astype(BF16)

    return pl.pallas_call(
        body, name=name, grid=(S // ts,),
        in_specs=[pl.BlockSpec((ts, ECA), lambda i: (i, qblk)), pl.BlockSpec((M, 2 * ECA), lambda i: (0, 0)), ANY],
        out_specs=pl.BlockSpec((ts, ECA), lambda i: (i, EMIX // ECA)),
        out_shape=jax.ShapeDtypeStruct(premix.shape, premix.dtype),
        input_output_aliases={2: 0}, compiler_params=_params("parallel"),
    )(proj, kv, premix)


def _ca_bwd(name, dpremix, proj, qblk, kv, dbuf, dblk, S, ECA, EMIX):
    M = kv.shape[0]
    hd = ECA // CA_HEADS
    ts = _tile(S, 512)
    scale = hd ** -0.5

    def body(do_ref, q_ref, kv_ref, _, dq_ref, dkv_ref):
        @pl.when(pl.program_id(0) == 0)
        def _():
            dkv_ref[...] = jnp.zeros_like(dkv_ref)

        for h in range(CA_HEADS):
            lo, hi = h * hd, (h + 1) * hd
            q = q_ref[:, lo:hi]
            k = kv_ref[:, lo:hi]
            v = kv_ref[:, ECA + lo:ECA + hi]
            do = do_ref[:, lo:hi]
            s = lax.dot_general(q, k, (NT, ((), ())), preferred_element_type=F32) * scale
            s = s - jnp.max(s, axis=-1, keepdims=True)
            p = jnp.exp(s)
            p = p / jnp.sum(p, axis=-1, keepdims=True)
            pb = p.astype(BF16)
            dkv_ref[:, ECA + lo:ECA + hi] += lax.dot_general(pb, do, (TN, ((), ())), preferred_element_type=F32)
            dp = lax.dot_general(do, v, (NT, ((), ())), preferred_element_type=F32)
            ds = (p * (dp - jnp.sum(p * dp, axis=-1, keepdims=True)) * scale).astype(BF16)
            dq_ref[:, lo:hi] = jnp.dot(ds, k, preferred_element_type=F32).astype(BF16)
            dkv_ref[:, lo:hi] += lax.dot_general(ds, q, (TN, ((), ())), preferred_element_type=F32)

    return pl.pallas_call(
        body, name=name, grid=(S // ts,),
        in_specs=[pl.BlockSpec((ts, ECA), lambda i: (i, EMIX // ECA)), pl.BlockSpec((ts, ECA), lambda i: (i, qblk)),
                  pl.BlockSpec((M, 2 * ECA), lambda i: (0, 0)), ANY],
        out_specs=[pl.BlockSpec((ts, ECA), lambda i: (i, dblk)), pl.BlockSpec((M, 2 * ECA), lambda i: (0, 0))],
        out_shape=[jax.ShapeDtypeStruct(dbuf.shape, dbuf.dtype), jax.ShapeDtypeStruct((M, 2 * ECA), F32)],
        input_output_aliases={3: 0}, compiler_params=_params("arbitrary"),
    )(dpremix, proj, kv, dbuf)


def _gate_fwd(name, premix, proj, gblk, colscale, S, EB, ECA):
    ts = _tile(S, 512)

    def body(p_ref, g_ref, c_ref, o_ref):
        g = g_ref[...].astype(F32)
        o_ref[...] = (p_ref[...].astype(F32) * c_ref[...] * (g * _sigmoid(g))).astype(BF16)

    return pl.pallas_call(
        body, name=name, grid=(S // ts, EB // ECA),
        in_specs=[pl.BlockSpec((ts, ECA), lambda i, j: (i, j)), pl.BlockSpec((ts, ECA), lambda i, j: (i, gblk + j)),
                  pl.BlockSpec((1, ECA), lambda i, j: (0, j))],
        out_specs=pl.BlockSpec((ts, ECA), lambda i, j: (i, j)),
        out_shape=jax.ShapeDtypeStruct((S, EB), BF16), compiler_params=_params("parallel", "parallel"),
    )(premix, proj, colscale)


def _gate_bwd_epilogue(db, extra_refs, out_refs):
    p_ref, g_ref, c_ref = extra_refs
    dp_ref, dg_ref, dc_ref = out_refs
    g = g_ref[...].astype(F32)
    sg = _sigmoid(g)
    si = g * sg
    c = c_ref[...]
    t = db * p_ref[...].astype(F32)
    dp_ref[...] = (db * si * c).astype(BF16)
    dg_ref[...] = (t * c * (sg * (1.0 + g * (1.0 - sg)))).astype(BF16)
    dc_ref[...] = jnp.sum(t * si, axis=0, keepdims=True)


def _hgrn_lb(lb_ref):
    l0 = lb_ref[0:1, :]
    l1 = lb_ref[1:2, :]
    mx = jnp.maximum(l0, l1)
    e0 = jnp.exp(l0 - mx)
    e1 = jnp.exp(l1 - mx)
    return e1 / (e0 + e1)


def _bdot(a, b, ca, cb):
    return lax.dot_general(a, b, (((ca,), (cb,)), ((0,), (0,))), preferred_element_type=F32)


def _tri_sum(tri, x):
    hi = x.astype(BF16)
    lo = (x - hi.astype(F32)).astype(BF16)
    tri = tri.astype(BF16)
    return _bdot(tri, hi, 2, 1) + _bdot(tri, lo, 2, 1)


def _hgrn_chunks(qin, fin, lbh, n):
    C = HG_CHUNK
    row = lax.broadcasted_iota(jnp.int32, (n, C, C), 1)
    col = lax.broadcasted_iota(jnp.int32, (n, C, C), 2)
    causal = row >= col
    sg = _sigmoid(fin)
    f = lbh + (1.0 - lbh) * sg
    k = 1.0 - f
    g = jnp.log(f)
    b = _tri_sum(jnp.where(causal, 1.0, 0.0), g)
    b_last = jnp.sum(g, axis=1, keepdims=True)
    eb = jnp.exp(b)
    einv = jnp.exp(-b)
    eend = jnp.exp(b_last - b)
    sq = _sigmoid(qin)
    a = qin * sq * (HG_HEAD_DIM ** -0.5) * eb
    bm = k * einv
    e = k * eend
    d = jnp.exp(b_last)
    p = jnp.where(causal, _bdot(a.astype(BF16), bm.astype(BF16), 2, 2), 0.0)
    return dict(causal=causal, sg=sg, f=f, eb=eb, einv=einv, eend=eend, sq=sq, a=a, bm=bm, e=e, d=d, p=p)


def _hgrn_fwd(proj, fgate, hgrn_lb, S, EMIX, EB):
    HD, C = HG_HEAD_DIM, HG_CHUNK
    HH = EMIX // HD
    hb = 2 if HH % 2 == 0 else 1
    W = hb * HD
    tr = _tile(S, 512)
    n = tr // C

    def body(q_ref, f_ref, i_ref, lb_ref, o_ref, rstd_ref, st_ref, state):
        @pl.when(pl.program_id(1) == 0)
        def _():
            state[...] = jnp.zeros_like(state)

        lb = _hgrn_lb(lb_ref)
        for h in range(hb):
            cs = slice(h * HD, (h + 1) * HD)
            qin = q_ref[:, cs].astype(F32).reshape(n, C, HD)
            fin = f_ref[:, cs].reshape(n, C, HD)
            v = i_ref[:, cs].reshape(n, C, HD)
            t = _hgrn_chunks(qin, fin, lb[:, cs], n)
            upd = _bdot(v, t["e"].astype(BF16), 1, 1)
            st = state[h]
            for c in range(n):
                st_ref[h, c] = st
                st = st * t["d"][c] + upd[c]
            state[h] = st
            o = _bdot(t["p"].astype(BF16), v, 2, 1) + _bdot(t["a"].astype(BF16), st_ref[h].astype(BF16), 2, 2)
            rstd = lax.rsqrt(jnp.mean(o * o, axis=-1, keepdims=True) + EPS)
            o_ref[:, cs] = (o * rstd).reshape(tr, HD).astype(BF16)
            rstd_ref[:, cs] = jnp.broadcast_to(rstd, (n, C, HD)).reshape(tr, HD)

    blk = lambda off: pl.BlockSpec((tr, W), lambda g, i: (i, off + g))
    return pl.pallas_call(
        body, name="hgrn_fwd", grid=(HH // hb, S // tr),
        in_specs=[blk(0), blk(0), blk(2 * EMIX // W), pl.BlockSpec((2, W), lambda g, i: (0, g))],
        out_specs=[blk(0), blk(0), pl.BlockSpec((hb, n, HD, HD), lambda g, i: (g, i, 0, 0))],
        out_shape=[jax.ShapeDtypeStruct((S, EB), BF16), jax.ShapeDtypeStruct((S, EMIX), F32),
                   jax.ShapeDtypeStruct((HH, S // C, HD, HD), F32)],
        scratch_shapes=[pltpu.VMEM((hb, HD, HD), F32)],
        compiler_params=_params("parallel", "arbitrary"),
    )(proj, fgate, proj, hgrn_lb)


def _hgrn_bwd(dpremix, premix, rstd, states, proj, fgate, hgrn_lb, S, EMIX):
    HD, C = HG_HEAD_DIM, HG_CHUNK
    HH = EMIX // HD
    hb = 2 if HH % 2 == 0 else 1
    W = hb * HD
    tr = _tile(S, 512)
    n = tr // C
    nrt = S // tr

    def body(do_ref, on_ref, rstd_ref, st_ref, q_ref, f_ref, i_ref, lb_ref, d_ref, dlb_ref, dstate, dsbuf):
        @pl.when(pl.program_id(1) == 0)
        def _():
            dstate[...] = jnp.zeros_like(dstate)
            dlb_ref[...] = jnp.zeros_like(dlb_ref)

        lb = _hgrn_lb(lb_ref)
        for h in range(hb):
            cs = slice(h * HD, (h + 1) * HD)
            qin = q_ref[:, cs].astype(F32).reshape(n, C, HD)
            fin = f_ref[:, cs].reshape(n, C, HD)
            v = i_ref[:, cs].reshape(n, C, HD)
            lbh = lb[:, cs]
            t = _hgrn_chunks(qin, fin, lbh, n)
            a, bm, e, d, p = t["a"], t["bm"], t["e"], t["d"], t["p"]
            ab, bmb, eb16 = a.astype(BF16), bm.astype(BF16), e.astype(BF16)
            on = on_ref[:, cs].astype(F32).reshape(n, C, HD)
            dn = do_ref[:, cs].astype(F32).reshape(n, C, HD)
            do = rstd_ref[:, cs].reshape(n, C, HD) * (dn - on * jnp.mean(dn * on, axis=-1, keepdims=True))
            dob = do.astype(BF16)
            grow = _bdot(dob, ab, 1, 1)
            ds = dstate[h]
            for c in reversed(range(n)):
                dsbuf[h, c] = ds
                ds = ds * d[c] + grow[c]
            dstate[h] = ds
            dst = dsbuf[h]
            st = st_ref[h]
            dstb = dst.astype(BF16)
            dp = jnp.where(t["causal"], _bdot(dob, v, 2, 2), 0.0).astype(BF16)
            dv = _bdot(p.astype(BF16), dob, 1, 1) + _bdot(eb16, dstb, 2, 2)
            da = _bdot(dp, bmb, 2, 1) + _bdot(dob, st.astype(BF16), 2, 1)
            dbm = _bdot(dp, ab, 1, 1)
            de = _bdot(v, dstb, 2, 1)
            dd = jnp.sum(dst * st, axis=1, keepdims=True)
            dk = dbm * t["einv"] + de * t["eend"]
            dee = de * e
            db = da * a - dbm * bm - dee
            extra = jnp.sum(dee, axis=1, keepdims=True) + dd * d
            upper = jnp.where(lax.broadcasted_iota(jnp.int32, (n, C, C), 2)
                              >= lax.broadcasted_iota(jnp.int32, (n, C, C), 1), 1.0, 0.0)
            dg = _tri_sum(upper, db) + extra
            df = dg / t["f"] - dk
            sg, sq = t["sg"], t["sq"]
            dq = da * t["eb"] * (HD ** -0.5) * (sq * (1.0 + qin * (1.0 - sq)))
            d_ref[0, :, cs] = dq.reshape(tr, HD).astype(BF16)
            d_ref[1, :, cs] = (df * (1.0 - lbh) * sg * (1.0 - sg)).reshape(tr, HD).astype(BF16)
            d_ref[2, :, cs] = dv.reshape(tr, HD).astype(BF16)
            dlb_ref[:, cs] += jnp.sum((df * (1.0 - sg)).reshape(tr, HD), axis=0, keepdims=True)

    rev = lambda off: pl.BlockSpec((tr, W), lambda g, s: (nrt - 1 - s, off + g))
    return pl.pallas_call(
        body, name="hgrn_bwd", grid=(HH // hb, nrt),
        in_specs=[rev(0), rev(0), rev(0), pl.BlockSpec((hb, n, HD, HD), lambda g, s: (g, nrt - 1 - s, 0, 0)),
                  rev(0), rev(0), rev(2 * EMIX // W), pl.BlockSpec((2, W), lambda g, s: (0, g))],
        out_specs=[pl.BlockSpec((3, tr, W), lambda g, s: (0, nrt - 1 - s, g)), pl.BlockSpec((1, W), lambda g, s: (0, g))],
        out_shape=[jax.ShapeDtypeStruct((3, S, EMIX), BF16), jax.ShapeDtypeStruct((1, EMIX), F32)],
        scratch_shapes=[pltpu.VMEM((hb, HD, HD), F32), pltpu.VMEM((hb, n, HD, HD), F32)],
        compiler_params=_params("parallel", "arbitrary"),
    )(dpremix, premix, rstd, states, proj, fgate, proj, hgrn_lb)


EW_BLOCK_ELEMS = 512 * 1024


def _ew_tiles(R, C):
    tc = C if C <= 4096 else _tile(C, 2048)
    tr = _tile(R, 512)
    while tr * tc > EW_BLOCK_ELEMS and tr % 16 == 0:
        tr //= 2
    return tr, tc


def _add_halves(name, core_chip, grad, got):
    _, _, R, C = grad.shape
    tr, tc = _ew_tiles(R, C)

    def body(c_ref, a_ref, b_ref, o_ref, own_ref):
        r = (a_ref[...].astype(F32) + b_ref[...].astype(F32)).astype(BF16)
        o_ref[...] = r

        @pl.when(pl.program_id(2) == c_ref[1])
        def _():
            own_ref[...] = r

    blk = pl.BlockSpec((None, tr, tc), lambda i, j, s, c: (s, i, j))
    sds = jax.ShapeDtypeStruct(got.shape, BF16)
    return pl.pallas_call(
        body, name=name, out_shape=[sds, sds],
        grid_spec=pltpu.PrefetchScalarGridSpec(
            num_scalar_prefetch=1, grid=(R // tr, C // tc, N_CHIPS),
            in_specs=[pl.BlockSpec((None, None, tr, tc), lambda i, j, s, c: (s, c[0], i, j)), blk],
            out_specs=[blk, pl.BlockSpec((None, tr, tc), lambda i, j, s, c: (c[1], i, j))]),
        compiler_params=_params("parallel", "parallel", "arbitrary"),
    )(core_chip, grad, got)


def _sum_slots(name, core, parts):
    _, R, C = parts.shape
    tr, tc = _ew_tiles(R, C)

    def body(c_ref, p_ref, o_ref):
        acc = p_ref[0].astype(F32)
        for s in range(1, N_CHIPS):
            acc = acc + p_ref[s].astype(F32)
        o_ref[...] = acc

    return pl.pallas_call(
        body, name=name, out_shape=jax.ShapeDtypeStruct((2, R, C), F32),
        grid_spec=pltpu.PrefetchScalarGridSpec(
            num_scalar_prefetch=1, grid=(R // tr, C // tc),
            in_specs=[pl.BlockSpec((N_CHIPS, tr, tc), lambda i, j, c: (0, i, j))],
            out_specs=pl.BlockSpec((None, tr, tc), lambda i, j, c: (c[0], i, j))),
        compiler_params=_params("parallel", "parallel"),
    )(core, parts)


def _adam_step(w, g, m, v):
    mn = ADAM_B1 * m + (1.0 - ADAM_B1) * g
    vn = ADAM_B2 * v + (1.0 - ADAM_B2) * (g * g)
    m_hat = mn / (1.0 - ADAM_B1 ** ADAM_STEP)
    v_hat = vn / (1.0 - ADAM_B2 ** ADAM_STEP)
    return -ADAM_LR * (m_hat / (jnp.sqrt(v_hat) + ADAM_EPS) + ADAM_WD * w), mn, vn


def _adamw(name, w, g, m, v):
    R, C = w.shape
    tr, tc = _ew_tiles(R, C)

    def body(w_ref, g_ref, m_ref, v_ref, d_ref, mo_ref, vo_ref):
        d_ref[...], mo_ref[...], vo_ref[...] = _adam_step(w_ref[...], g_ref[...], m_ref[...], v_ref[...])

    blk = pl.BlockSpec((tr, tc), lambda i, j: (i, j))
    sds = jax.ShapeDtypeStruct((R, C), F32)
    return pl.pallas_call(
        body, name=name, grid=(R // tr, C // tc), in_specs=[blk] * 4, out_specs=[blk] * 3, out_shape=[sds] * 3,
        compiler_params=_params("parallel", "parallel"),
    )(w, g, m, v)


def _adamw_layers(name, w, gs, m, v):
    L, R, C = w.shape
    tr, tc = _ew_tiles(R, C)

    def body(*refs):
        w_ref, m_ref, v_ref = refs[:3]
        g_refs = refs[3:3 + L]
        go_ref, d_ref, mo_ref, vo_ref = refs[3 + L:]
        layer = pl.program_id(0)
        g = g_refs[0][...]
        for n in range(1, L):
            g = jnp.where(layer == n, g_refs[n][...], g)
        go_ref[...] = g
        d_ref[...], mo_ref[...], vo_ref[...] = _adam_step(w_ref[...], g, m_ref[...], v_ref[...])

    blk = pl.BlockSpec((None, tr, tc), lambda l, i, j: (l, i, j))
    of_layer = lambda n: pl.BlockSpec((tr, tc), lambda l, i, j: (jnp.where(l == n, i, 0), jnp.where(l == n, j, 0)))
    sds = jax.ShapeDtypeStruct((L, R, C), F32)
    return pl.pallas_call(
        body, name=name, grid=(L, R // tr, C // tc), in_specs=[blk] * 3 + [of_layer(n) for n in range(L)],
        out_specs=[blk] * 4, out_shape=[sds] * 4, compiler_params=_params("parallel", "parallel", "parallel"),
    )(w, m, v, *gs)


def _pack_rows(name, vecs, W):
    nv = len(vecs)

    def body(*refs):
        o_ref = refs[nv]
        o_ref[...] = jnp.zeros_like(o_ref)
        for i in range(nv):
            o_ref[i:i + 1, 0:vecs[i].shape[1]] = jnp.sum(refs[i][...], axis=0, keepdims=True)

    vm = pl.BlockSpec(memory_space=pltpu.VMEM)
    return pl.pallas_call(
        body, name=name, in_specs=[vm] * nv, out_specs=vm, out_shape=jax.ShapeDtypeStruct((SMALL_ROWS, W), F32),
    )(*vecs)


def _small_sum(gathered, hgrn_lb, lb_row):
    _, T, W = gathered.shape

    def body(g_ref, lb_ref, o_ref):
        acc = g_ref[0]
        for dev in range(1, N_DEV):
            acc = acc + g_ref[dev]
        o_ref[0:T, :] = acc
        lb = _hgrn_lb(lb_ref)
        d1 = o_ref[lb_row:lb_row + 1, :] * (lb * (1.0 - lb))
        o_ref[T:2 * T, :] = jnp.zeros((T, W), F32)
        o_ref[T:T + 1, :] = -d1
        o_ref[T + 1:T + 2, :] = d1

    vm = pl.BlockSpec(memory_space=pltpu.VMEM)
    return pl.pallas_call(
        body, name="small_sum", in_specs=[vm, vm], out_specs=vm, out_shape=jax.ShapeDtypeStruct((2 * T, W), F32),
    )(gathered, hgrn_lb)


def _place():
    return lax.axis_index("x"), lax.axis_index("y"), lax.axis_index("c")


def _other_chips(x, y):
    return [(1 - x, y), (x, 1 - y), (1 - x, 1 - y)]


def _copies_now(name, srcs, dst_shapes, plan, ncopies):
    def body(*refs):
        ns = len(srcs)
        send, recv = refs[ns + len(dst_shapes):]
        copies = []
        for i, (src, dst, dev) in enumerate(plan(refs[:ns], refs[ns:ns + len(dst_shapes)])):
            cp = pltpu.make_async_remote_copy(src_ref=src, dst_ref=dst, send_sem=send.at[i], recv_sem=recv.at[i],
                                              device_id=dev, device_id_type=MESH)
            cp.start()
            copies.append(cp)
        for cp in copies:
            cp.wait()

    return pl.pallas_call(
        body, name=name, in_specs=[ANY] * len(srcs), out_specs=[ANY] * len(dst_shapes), out_shape=dst_shapes,
        scratch_shapes=[pltpu.SemaphoreType.DMA((ncopies,))] * 2,
    )(*srcs)


def _chunk_rows(rows, row_bytes):
    cr = rows
    while cr * row_bytes > STREAM_CHUNK_BYTES and cr % 32 == 0:
        cr //= 2
    return cr


def _stream(pairs, buf, sems, t, peer):
    lsem, ssem, rsem = sems
    n = len(pairs)
    loads, sent = [None] * n, [None] * n

    def load(k):
        slot = k % STREAM_SLOTS
        if k >= STREAM_SLOTS:
            sent[k - STREAM_SLOTS]()
        loads[k] = pltpu.make_async_copy(pairs[k][0], buf.at[slot], lsem.at[t, slot])
        loads[k].start()

    load(0)
    for k in range(n):
        slot = k % STREAM_SLOTS
        if k + 1 < n:
            load(k + 1)
        loads[k].wait()
        if peer is None:
            cp = pltpu.make_async_copy(buf.at[slot], pairs[k][1], ssem.at[t, slot])
            cp.start()
            sent[k] = cp.wait
        else:
            cp = pltpu.make_async_remote_copy(src_ref=buf.at[slot], dst_ref=pairs[k][1], send_sem=ssem.at[t, slot],
                                              recv_sem=rsem.at[t], device_id=peer, device_id_type=MESH)
            cp.start()
            sent[k] = cp.wait_send
    for k in range(max(0, n - STREAM_SLOTS), n):
        sent[k]()


def _stream_scratch(shapes):
    nt = len(shapes)
    return ([pltpu.VMEM((STREAM_SLOTS,) + s, d) for s, d in shapes]
            + [pltpu.SemaphoreType.DMA((nt, STREAM_SLOTS)), pltpu.SemaphoreType.DMA((nt, STREAM_SLOTS)),
               pltpu.SemaphoreType.DMA((nt,))])


def _exchange_halves(name, grads):
    nt = len(grads)
    hs = [g.shape[1] // 2 for g in grads]
    crs = [_chunk_rows(h, g.shape[2] * g.dtype.itemsize) for h, g in zip(hs, grads)]

    def body(*refs):
        ins, gots, bufs, sems = refs[:nt], refs[nt:2 * nt], refs[2 * nt:3 * nt], refs[3 * nt:]
        x, y, c = _place()
        sib = (x, y, 1 - c)
        for t in range(nt):
            h, cr = hs[t], crs[t]
            pairs = [(ins[t].at[b, pl.ds((1 - c) * h + r0, cr)], gots[t].at[b, pl.ds(r0, cr)])
                     for b in range(N_CHIPS) for r0 in range(0, h, cr)]
            _stream(pairs, bufs[t], sems, t, sib)
        for t in range(nt):
            pltpu.make_async_remote_copy(src_ref=gots[t], dst_ref=gots[t], send_sem=sems[1].at[t, 0],
                                         recv_sem=sems[2].at[t], device_id=sib, device_id_type=MESH).wait_recv()

    return pl.pallas_call(
        body, name=name, in_specs=[ANY] * nt, out_specs=[ANY] * nt,
        out_shape=[jax.ShapeDtypeStruct((N_CHIPS, h, g.shape[2]), g.dtype) for h, g in zip(hs, grads)],
        scratch_shapes=_stream_scratch([((cr, g.shape[2]), g.dtype) for cr, g in zip(crs, grads)]),
        compiler_params=pltpu.CompilerParams(vmem_limit_bytes=VMEM_LIMIT_BYTES),
    )(*grads)


def _scatter_plan(srcs, dsts):
    x, y, c = _place()
    me = 2 * x + y
    return [(srcs[t].at[2 * px + py], dsts[t].at[me], (px, py, c))
            for t in range(len(srcs)) for px, py in _other_chips(x, y)]


def _slot(dst, chip, r0, rows, cols):
    if len(dst.shape) == 3:
        return dst.at[chip, pl.ds(r0, rows)]
    return dst.at[pl.ds(r0, rows), pl.ds(pl.multiple_of(chip * cols, 128), cols)]


def _gather_plan(srcs, dsts):
    x, y, c = _place()
    me = 2 * x + y
    plan = []
    for t in range(len(srcs)):
        h, cols = srcs[t].shape[0] // 2, srcs[t].shape[1]
        plan += [(srcs[t].at[pl.ds(c * h, h)], _slot(dsts[t], me, c * h, h, cols), (px, py, c))
                 for px, py in _other_chips(x, y)]
    return plan


HBM_SPEC = pl.BlockSpec(memory_space=pltpu.HBM)
SEM_SPEC = pl.BlockSpec(memory_space=pltpu.SEMAPHORE)


def _split_start(name, srcs, dsts, plan, ncopies, after):
    bufs = [pltpu.with_memory_space_constraint(a, pltpu.HBM) for a in list(srcs) + list(dsts)]
    nb, ns = len(bufs), len(srcs)
    operands = bufs + ([after] if after is not None else [])

    def body(*refs):
        outs = refs[len(operands):]
        send, recv, token = outs[0], outs[1], outs[-1]
        for i, (src, dst, dev) in enumerate(plan(refs[:ns], refs[ns:nb])):
            pltpu.make_async_remote_copy(src_ref=src, dst_ref=dst, send_sem=send.at[i], recv_sem=recv.at[i],
                                         device_id=dev, device_id_type=MESH).start()
        token[...] = jnp.zeros_like(token)

    res = pl.pallas_call(
        body, name=name,
        out_shape=[pltpu.SemaphoreType.DMA((ncopies,)), pltpu.SemaphoreType.DMA((ncopies,))]
        + [pltpu.HBM(a.shape, a.dtype) for a in bufs] + [jax.ShapeDtypeStruct((8, 128), F32)],
        in_specs=[HBM_SPEC] * nb + [ANY] * (len(operands) - nb),
        out_specs=[SEM_SPEC, SEM_SPEC] + [HBM_SPEC] * nb + [pl.BlockSpec(memory_space=pltpu.VMEM)],
        input_output_aliases={i: 2 + i for i in range(nb)},
        compiler_params=pltpu.CompilerParams(has_side_effects=pltpu.SideEffectType.DATAFLOW_SIDE_EFFECTING),
    )(*operands)
    return res[:-1], res[-1]


def _split_wait(name, started, plan, ns, after):
    send, recv, bufs = started[0], started[1], list(started[2:])
    nb = len(bufs)

    def body(*refs):
        send_ref, recv_ref = refs[nb], refs[nb + 1]
        for i, (src, dst, dev) in enumerate(plan(refs[:ns], refs[ns:nb])):
            cp = pltpu.make_async_remote_copy(src_ref=src, dst_ref=dst, send_sem=send_ref.at[i], recv_sem=recv_ref.at[i],
                                              device_id=dev, device_id_type=MESH)
            cp.wait_send()
            cp.wait_recv()

    res = pl.pallas_call(
        body, name=name, out_shape=[pltpu.HBM(a.shape, a.dtype) for a in bufs],
        in_specs=[HBM_SPEC] * nb + [SEM_SPEC, SEM_SPEC, ANY], out_specs=[HBM_SPEC] * nb,
        input_output_aliases={i: i for i in range(nb)},
        compiler_params=pltpu.CompilerParams(has_side_effects=pltpu.SideEffectType.DATAFLOW_SIDE_EFFECTING),
    )(*bufs, send, recv, after)
    return res[:ns], res[ns:]


def _gather_finish(name, shards, stacks):
    nt = len(shards)
    hs = [s.shape[0] // 2 for s in shards]
    crs = [_chunk_rows(h, s.shape[1] * s.dtype.itemsize) for h, s in zip(hs, shards)]

    def body(*refs):
        ins, outs, bufs, sems = refs[:nt], refs[2 * nt:3 * nt], refs[3 * nt:4 * nt], refs[4 * nt:]
        x, y, c = _place()
        me = 2 * x + y
        sib = (x, y, 1 - c)
        for t in range(nt):
            h, cr, cols = hs[t], crs[t], shards[t].shape[1]
            passed = [_slot(outs[t], 2 * px + py, c * h + r0, cr, cols)
                      for px, py in _other_chips(x, y) for r0 in range(0, h, cr)]
            _stream([(r, r) for r in passed], bufs[t], sems, t, sib)
            own = [(ins[t].at[pl.ds(r0, cr)], _slot(outs[t], me, r0, cr, cols)) for r0 in range(0, 2 * h, cr)]
            _stream(own, bufs[t], sems, t, None)
        for t in range(nt):
            if len(stacks[t].shape) == 3:
                three = outs[t].at[pl.ds(0, 3), pl.ds(0, hs[t])]
            else:
                three = outs[t].at[pl.ds(0, hs[t]), pl.ds(0, 3 * shards[t].shape[1])]
            pltpu.make_async_remote_copy(src_ref=three, dst_ref=three, send_sem=sems[1].at[t, 0],
                                         recv_sem=sems[2].at[t], device_id=sib, device_id_type=MESH).wait_recv()

    return pl.pallas_call(
        body, name=name, in_specs=[ANY] * (2 * nt), out_specs=[ANY] * nt,
        out_shape=[jax.ShapeDtypeStruct(s.shape, s.dtype) for s in stacks],
        scratch_shapes=_stream_scratch([((cr, s.shape[1]), s.dtype) for cr, s in zip(crs, shards)]),
        input_output_aliases={nt + t: t for t in range(nt)},
        compiler_params=pltpu.CompilerParams(vmem_limit_bytes=VMEM_LIMIT_BYTES),
    )(*shards, *stacks)


def _share_halves(fulls):
    nt = len(fulls)
    crs = [_chunk_rows(f.shape[1], f.shape[2] * f.dtype.itemsize) for f in fulls]

    def body(*refs):
        outs, bufs, sems = refs[nt:2 * nt], refs[2 * nt:3 * nt], refs[3 * nt:]
        x, y, c = _place()
        sib = (x, y, 1 - c)
        for t in range(nt):
            rows = [outs[t].at[c, pl.ds(r0, crs[t])] for r0 in range(0, fulls[t].shape[1], crs[t])]
            _stream([(r, r) for r in rows], bufs[t], sems, t, sib)
        for t in range(nt):
            other = outs[t].at[1 - c]
            pltpu.make_async_remote_copy(src_ref=other, dst_ref=other, send_sem=sems[1].at[t, 0],
                                         recv_sem=sems[2].at[t], device_id=sib, device_id_type=MESH).wait_recv()

    return pl.pallas_call(
        body, name="share_halves", in_specs=[ANY] * nt, out_specs=[ANY] * nt,
        out_shape=[jax.ShapeDtypeStruct(f.shape, f.dtype) for f in fulls],
        scratch_shapes=_stream_scratch([((cr, f.shape[2]), f.dtype) for cr, f in zip(crs, fulls)]),
        input_output_aliases={t: t for t in range(nt)},
        compiler_params=pltpu.CompilerParams(vmem_limit_bytes=VMEM_LIMIT_BYTES),
    )(*fulls)


def _allgather_small(name, v):
    def body(v_ref, o_ref, send, recv, lsem):
        x, y, c = _place()
        me = 4 * x + 2 * y + c
        loc = pltpu.make_async_copy(v_ref, o_ref.at[me], lsem)
        loc.start()
        copies = []
        for k in range(1, N_DEV):
            px = 1 - x if k & 4 else x
            py = 1 - y if k & 2 else y
            pc = 1 - c if k & 1 else c
            cp = pltpu.make_async_remote_copy(
                src_ref=v_ref, dst_ref=o_ref.at[me], send_sem=send.at[k - 1], recv_sem=recv.at[k - 1],
                device_id=(px, py, pc), device_id_type=MESH)
            cp.start()
            copies.append(cp)
        for cp in copies:
            cp.wait()
        loc.wait()

    vm = pl.BlockSpec(memory_space=pltpu.VMEM)
    return pl.pallas_call(
        body, name=name, in_specs=[vm], out_specs=vm,
        out_shape=jax.ShapeDtypeStruct((N_DEV,) + v.shape, v.dtype),
        scratch_shapes=[pltpu.SemaphoreType.DMA((N_DEV - 1,))] * 2 + [pltpu.SemaphoreType.DMA],
    )(v)


def kernel(x, mem, norm_g, mem_norm_g, w_kv, w_out, pool_w_in, pool_w_grp, pool_scale, hgrn_w_in, hgrn_lb, hgrn_norm_g, final_g, loss_target, m_norm_g, m_mem_norm_g, m_w_kv, m_w_out, m_pool_w_in, m_pool_w_grp, m_pool_scale, m_hgrn_w_in, m_hgrn_lb, m_hgrn_norm_g, m_final_g, v_norm_g, v_mem_norm_g, v_w_kv, v_w_out, v_pool_w_in, v_pool_w_grp, v_pool_scale, v_hgrn_w_in, v_hgrn_lb, v_hgrn_norm_g, v_final_g):
    _, S, D = x.shape
    M = mem.shape[1]
    EB = 2 * D
    ECA = EB // 4
    EMIX = EB - ECA
    PG = EMIX // N_POOL_GROUPS
    NP0 = EMIX + ECA + EB
    NP1 = 3 * EMIX + ECA + EB
    SH0, SH1 = NP0 // N_CHIPS, NP1 // N_CHIPS
    DK, EK = D // N_CHIPS, EB // N_CHIPS
    TNP = 512 if all(v % 512 == 0 for v in (SH0, SH1, ECA, EMIX)) else 256
    TM = _tile(S, 1024)
    TMF = _tile(S, 2048)
    TD = _tile(D, 512)
    TDW = _tile(D, 1024)
    c0, c1 = SH0 // TNP, SH1 // TNP
    qt, et = EMIX // TNP, ECA // TNP
    chip = 2 * lax.axis_index("x") + lax.axis_index("y")

    xs, ms, tgt = x[0], mem[0], loss_target[0]

    flat = lambda w: w.reshape(-1, w.shape[-1])
    bf = lambda w: w.astype(BF16)
    sds = jax.ShapeDtypeStruct
    first = [bf(flat(pool_w_in))]
    wpin, = _gather_finish("gather_pin_finish", first, _copies_now(
        "gather_pin", first, [sds((N_CHIPS,) + first[0].shape, BF16)], _gather_plan, 3))
    stack_of = lambda s: lax.empty((N_CHIPS,) + s.shape, BF16)
    group_b = [bf(w_kv[0]), bf(w_out[0]), bf(flat(pool_w_grp))]
    gather_b, token = _split_start("gather_b_start", group_b, [stack_of(s) for s in group_b], _gather_plan, 9, wpin)
    group_c = [bf(flat(hgrn_w_in))]
    gather_c, token = _split_start("gather_c_start", group_c, [lax.empty((D, NP1), BF16)], _gather_plan, 3, token)
    group_d = [bf(w_kv[1]), bf(w_out[1])]
    gather_d, token = _split_start("gather_d_start", group_d, [stack_of(s) for s in group_d], _gather_plan, 6, token)

    tek, tew = _tile(EK, 512), _tile(EK, 1024)

    mem_n = _rms_fwd("rms_mem", ms, mem_norm_g.reshape(1, D))

    tkw = _tile(2 * ECA, 1024)

    def kv_of(layer, wkv):
        return _matmul(
            f"kv{layer}", mem_n, wkv.reshape(D, 2 * ECA), grid=(1, 2 * ECA // tkw, 1),
            a_spec=pl.BlockSpec((M, D), lambda i, j, k: (0, 0)), b_spec=pl.BlockSpec((D, tkw), lambda i, j, k: (0, j)),
            out_shape=sds((M, 2 * ECA), BF16), out_spec=pl.BlockSpec((M, tkw), lambda i, j, k: (0, j)),
            acc_shape=(M, tkw), dims=NN)

    tko = _tile(EB, 2048)

    def out_proj(layer, branch, wout, resid):
        return _matmul(
            f"out_proj{layer}", branch, wout.reshape(EB, D), grid=(S // TM, D // TDW, EB // tko),
            a_spec=pl.BlockSpec((TM, tko), IK), b_spec=pl.BlockSpec((tko, TDW), KJ),
            out_shape=sds((S, D), F32), out_spec=pl.BlockSpec((TM, TDW), IJ),
            acc_shape=(TM, TDW), dims=NN, add=resid, add_spec=pl.BlockSpec((TM, TDW), IJ))

    ones_ca = jnp.ones((1, ECA), F32)

    h0 = _rms_fwd("rms0", xs, norm_g[0:1] + token[0:1, 0:1])
    proj0 = _matmul(
        "proj0", h0, wpin, grid=(S // TMF, NP0 // TNP, 1),
        a_spec=pl.BlockSpec((TMF, D), lambda i, j, k: (i, 0)),
        b_spec=pl.BlockSpec((None, D, TNP), lambda i, j, k: (j // c0, 0, j % c0)),
        out_shape=sds((S, NP0), BF16), out_spec=pl.BlockSpec((TMF, TNP), IJ),
        acc_shape=(TMF, TNP), dims=NN)
    pooled = _pool_fwd(proj0, S, EMIX)
    wkv0, wout0, g_grp = _gather_finish("gather_b_finish", *_split_wait("gather_b_wait", gather_b, _gather_plan, 3, pooled))
    wgrp = g_grp.reshape(N_CHIPS, N_POOL_GROUPS, PG // N_CHIPS, PG).transpose(1, 0, 2, 3).reshape(N_POOL_GROUPS, PG, PG)
    kv = [kv_of(0, wkv0), None]
    premix0 = _matmul(
        "pool_grp", pooled, wgrp, grid=(S // TM, N_POOL_GROUPS, 1),
        a_spec=pl.BlockSpec((TM, PG), lambda i, j, k: (i, j)),
        b_spec=pl.BlockSpec((None, PG, PG), lambda i, j, k: (j, 0, 0)),
        out_shape=sds((S, EB), BF16), out_spec=pl.BlockSpec((TM, PG), lambda i, j, k: (i, j)),
        acc_shape=(TM, PG), dims=NN)
    premix0 = _ca_fwd("ca_fwd0", proj0, EMIX // ECA, kv[0], premix0, S, ECA, EMIX)
    colscale0 = jnp.concatenate([pool_scale.reshape(1, EMIX), ones_ca], axis=1)
    gblk0 = (EMIX + ECA) // ECA
    branch0 = _gate_fwd("gate_fwd0", premix0, proj0, gblk0, colscale0, S, EB, ECA)
    x1 = out_proj(0, branch0, wout0, xs)

    whin, = _gather_finish("gather_c_finish", *_split_wait("gather_c_wait", gather_c, _gather_plan, 1, x1))
    h1 = _rms_fwd("rms1", x1, norm_g[1:2])

    def proj1_cols(name, ncols, col_of, out_cols, out_dtype, out_col_of):
        return _matmul(
            name, h1, whin, grid=(S // TMF, ncols, 1),
            a_spec=pl.BlockSpec((TMF, D), lambda i, j, k: (i, 0)),
            b_spec=pl.BlockSpec((D, TNP), lambda i, j, k: (0, col_of(j))),
            out_shape=sds((S, out_cols), out_dtype), out_spec=pl.BlockSpec((TMF, TNP), lambda i, j, k: (i, out_col_of(j))),
            acc_shape=(TMF, TNP), dims=NN)

    skip_f = lambda j: jnp.where(j < qt, j, j + qt)
    proj1 = proj1_cols("proj1", NP1 // TNP - qt, skip_f, NP1, BF16, skip_f)
    fgate = proj1_cols("proj1_f", qt, lambda j: j + qt, EMIX, F32, lambda j: j)
    premix1, rstd1, states = _hgrn_fwd(proj1, fgate, hgrn_lb, S, EMIX, EB)
    wkv1, wout1 = _gather_finish("gather_d_finish", *_split_wait("gather_d_wait", gather_d, _gather_plan, 2, rstd1))
    kv[1] = kv_of(1, wkv1)
    premix1 = _ca_fwd("ca_fwd1", proj1, 3 * EMIX // ECA, kv[1], premix1, S, ECA, EMIX)
    norm_tiles = _allgather_small("allgather_norm_g", jnp.pad(hgrn_norm_g, ((0, SMALL_ROWS - 1), (0, 0))))
    hg_norm = norm_tiles[0::2, 0, :].reshape(1, EMIX)
    colscale1 = jnp.concatenate([hg_norm, ones_ca], axis=1)
    gblk1 = (3 * EMIX + ECA) // ECA
    branch1 = _gate_fwd("gate_fwd1", premix1, proj1, gblk1, colscale1, S, EB, ECA)
    x2 = out_proj(1, branch1, wout1, x1)

    dx2, dx2b, d_final_g, loss_part = _loss_head(x2, final_g.reshape(1, D), tgt)

    def out_proj_bwd(layer, dxb, branch, wout, premix, proj, gblk, colscale, dshape, dblk):
        goff, doff = gblk * ECA // tek, dblk * ECA // tek
        dpremix, dgate, dcol = _matmul(
            f"dbranch{layer}", dxb, wout, grid=(S // TMF, EB // tek, 1),
            a_spec=pl.BlockSpec((TMF, D), lambda i, j, k: (i, 0)),
            b_spec=pl.BlockSpec((None, tek, D), lambda i, j, k: (j // (EK // tek), j % (EK // tek), 0)),
            extras=[(premix, pl.BlockSpec((TMF, tek), IJ)), (proj, pl.BlockSpec((TMF, tek), lambda i, j, k: (i, goff + j))),
                    (colscale, pl.BlockSpec((1, tek), lambda i, j, k: (0, j)))],
            epilogue=_gate_bwd_epilogue,
            out_shape=[sds((S, EB), BF16), sds(dshape, BF16), sds((S // TMF, 1, EB), F32)],
            out_spec=[pl.BlockSpec((TMF, tek), IJ), pl.BlockSpec((TMF, tek), lambda i, j, k: (i, doff + j)),
                      pl.BlockSpec((None, 1, tek), lambda i, j, k: (i, 0, j))],
            acc_shape=(TMF, tek), dims=NT)
        dw = _matmul(
            f"dwout{layer}", branch, dxb, grid=(EB // tew, D // TD, 1),
            a_spec=pl.BlockSpec((S, tew), lambda i, j, k: (0, i)), b_spec=pl.BlockSpec((S, TD), lambda i, j, k: (0, j)),
            out_shape=sds((N_CHIPS, EK, D), BF16),
            out_spec=pl.BlockSpec((None, tew, TD), lambda i, j, k: (i // (EK // tew), i % (EK // tew), j)),
            acc_shape=(tew, TD), dims=TN)
        return dpremix, dgate, dcol.reshape(S // TMF, EB), dw

    def kv_bwd(layer, dkv, wkv, dmem_add):
        dkvb = dkv.astype(BF16)
        dmem = _matmul(
            f"dmem{layer}", dkvb, wkv.reshape(D, 2 * ECA), grid=(1, D // TDW, 1),
            a_spec=pl.BlockSpec((M, 2 * ECA), lambda i, j, k: (0, 0)),
            b_spec=pl.BlockSpec((TDW, 2 * ECA), lambda i, j, k: (j, 0)),
            out_shape=sds((M, D), F32), out_spec=pl.BlockSpec((M, TDW), lambda i, j, k: (0, j)), acc_shape=(M, TDW),
            dims=NT, add=dmem_add, add_spec=pl.BlockSpec((M, TDW), lambda i, j, k: (0, j)))
        dw = _matmul(
            f"dwkv{layer}", mem_n, dkvb, grid=(D // TDW, 2 * ECA // tkw, 1),
            a_spec=pl.BlockSpec((M, TDW), lambda i, j, k: (0, i)), b_spec=pl.BlockSpec((M, tkw), lambda i, j, k: (0, j)),
            out_shape=sds((D, 2 * ECA), BF16), out_spec=pl.BlockSpec((TDW, tkw), IJ), acc_shape=(TDW, tkw), dims=TN)
        return dmem, dw.reshape(N_CHIPS, DK, 2 * ECA)

    dpremix1, drest1, dcol1, gw_out1 = out_proj_bwd(1, dx2b, branch1, wout1, premix1, proj1, gblk1, colscale1,
                                                    (S, ECA + EB), 1)
    drest1, dkv1 = _ca_bwd("ca_bwd1", dpremix1, proj1, 3 * EMIX // ECA, kv[1], drest1, 0, S, ECA, EMIX)
    dqfi, dlb = _hgrn_bwd(dpremix1, premix1, rstd1, states, proj1, fgate, hgrn_lb, S, EMIX)
    nq, nr = 3 * qt, (ECA + EB) // TNP
    tkh = _tile(EMIX, 1024) if (ECA + EB) % _tile(EMIX, 1024) == 0 else TNP
    kq = EMIX // tkh
    dh1 = _matmul(
        "dh1_qfi", dqfi, whin, grid=(S // TM, D // TDW, 3),
        a_spec=pl.BlockSpec((None, TM, EMIX), lambda i, j, k: (k, i, 0)),
        b_spec=pl.BlockSpec((TDW, EMIX), lambda i, j, k: (j, k)),
        out_shape=sds((S, D), F32), out_spec=pl.BlockSpec((TM, TDW), IJ), acc_shape=(TM, TDW), dims=NT)
    dh1 = _matmul(
        "dh1_rest", drest1, whin, grid=(S // TM, D // TDW, (ECA + EB) // tkh), a_spec=pl.BlockSpec((TM, tkh), IK),
        b_spec=pl.BlockSpec((TDW, tkh), lambda i, j, k: (j, k + 3 * kq)),
        out_shape=sds((S, D), F32), out_spec=pl.BlockSpec((TM, TDW), IJ), acc_shape=(TM, TDW), dims=NT,
        add=dh1, add_spec=pl.BlockSpec((TM, TDW), IJ))
    gw_hin = _matmul(
        "dwhin_qfi", h1, dqfi, grid=(D // TDW, nq, 1), a_spec=pl.BlockSpec((S, TDW), lambda i, j, k: (0, i)),
        b_spec=pl.BlockSpec((None, S, TNP), lambda i, j, k: (j // qt, 0, j % qt)),
        out_shape=sds((N_CHIPS, D, SH1), BF16), out_spec=pl.BlockSpec((None, TDW, TNP), lambda i, j, k: (j // c1, i, j % c1)),
        acc_shape=(TDW, TNP), dims=TN)
    gw_hin = _matmul(
        "dwhin_rest", h1, drest1, grid=(D // TDW, nr, 1), a_spec=pl.BlockSpec((S, TDW), lambda i, j, k: (0, i)),
        b_spec=pl.BlockSpec((S, TNP), lambda i, j, k: (0, j)), out_shape=sds((N_CHIPS, D, SH1), BF16),
        out_spec=pl.BlockSpec((None, TDW, TNP), lambda i, j, k: ((j + nq) // c1, i, (j + nq) % c1)),
        acc_shape=(TDW, TNP), dims=TN, alias=gw_hin)
    dmem, gw_kv1 = kv_bwd(1, dkv1, wkv1, None)

    core_chip = jnp.stack([lax.axis_index("c"), chip]).astype(jnp.int32)

    def reduce_in_chip(tag, stacks):
        got = _exchange_halves(f"exchange_halves{tag}", stacks)
        pairs = [_add_halves(f"add_halves{tag}_{t}", core_chip, g.reshape(N_CHIPS, 2, g.shape[1] // 2, g.shape[2]), r)
                 for t, (g, r) in enumerate(zip(stacks, got))]
        return [p for p, _ in pairs], [own for _, own in pairs]

    parts1, landed1 = reduce_in_chip(1, [gw_kv1, gw_out1, gw_hin])
    scatter1, token1 = _split_start("scatter1_start", parts1, landed1, _scatter_plan, 3 * len(parts1), None)
    dx1, dx1b, d_ng1 = _rms_bwd("rms_bwd1", dh1, x1, norm_g[1:2] + token1[0:1, 0:1], dx2)

    dpremix0, dproj0, dcol0, gw_out0 = out_proj_bwd(0, dx1b, branch0, wout0, premix0, proj0, gblk0, colscale0,
                                                    (S, NP0), gblk0)
    dproj0, dkv0 = _ca_bwd("ca_bwd0", dpremix0, proj0, EMIX // ECA, kv[0], dproj0, EMIX // ECA, S, ECA, EMIX)
    dmem, gw_kv0 = kv_bwd(0, dkv0, wkv0, dmem)
    parts_a, landed_a = reduce_in_chip("0a", [gw_kv0, gw_out0])
    scatter_a, token_a = _split_start("scatter0a_start", parts_a, landed_a, _scatter_plan, 3 * len(parts_a), None)
    dpooled = _matmul(
        "dpooled", dpremix0, wgrp, grid=(S // TM, N_POOL_GROUPS, 1), a_spec=pl.BlockSpec((TM, PG), IJ),
        b_spec=pl.BlockSpec((None, PG, PG), lambda i, j, k: (j, 0, 0)),
        out_shape=sds((S, EMIX), F32), out_spec=pl.BlockSpec((TM, PG), IJ), acc_shape=(TM, PG), dims=NT, after=token_a)
    dwgrp = _matmul(
        "dwgrp", pooled, dpremix0, grid=(N_POOL_GROUPS, 1, 1), a_spec=pl.BlockSpec((S, PG), lambda i, j, k: (0, i)),
        b_spec=pl.BlockSpec((S, PG), lambda i, j, k: (0, i)), out_shape=sds((N_POOL_GROUPS, PG, PG), F32),
        out_spec=pl.BlockSpec((None, PG, PG), lambda i, j, k: (i, 0, 0)), acc_shape=(PG, PG), dims=TN)
    dproj0 = _pool_bwd(dpooled, dproj0, S, EMIX)
    gw_pin = _matmul(
        "dwpin", h0, dproj0, grid=(D // TDW, NP0 // TNP, 1), a_spec=pl.BlockSpec((S, TDW), lambda i, j, k: (0, i)),
        b_spec=pl.BlockSpec((S, TNP), lambda i, j, k: (0, j)), out_shape=sds((N_CHIPS, D, SH0), BF16),
        out_spec=pl.BlockSpec((None, TDW, TNP), lambda i, j, k: (j // c0, i, j % c0)), acc_shape=(TDW, TNP), dims=TN)
    gw_grp = dwgrp.reshape(N_POOL_GROUPS, N_CHIPS, PG // N_CHIPS, PG).transpose(1, 0, 2, 3).reshape(N_CHIPS, PG, PG)
    parts_b, landed_b = reduce_in_chip("0b", [gw_pin, gw_grp.astype(BF16)])
    scatter_b, token_b = _split_start("scatter0b_start", parts_b, landed_b, _scatter_plan, 3 * len(parts_b), None)
    dh0 = _matmul(
        "dh0", dproj0, wpin, grid=(S // TM, D // TDW, N_CHIPS), a_spec=pl.BlockSpec((TM, SH0), IK),
        b_spec=pl.BlockSpec((None, TDW, SH0), lambda i, j, k: (k, j, 0)),
        out_shape=sds((S, D), F32), out_spec=pl.BlockSpec((TM, TDW), IJ), acc_shape=(TM, TDW), dims=NT, after=token_b)
    grad_x, _, d_ng0 = _rms_bwd("rms_bwd0", dh0, xs, norm_g[0:1], dx1)
    _, _, d_mng = _rms_bwd("rms_bwd_mem", dmem, ms, mem_norm_g.reshape(1, D), jnp.zeros_like(ms))

    _, landed1 = _split_wait("scatter1_wait", scatter1, _scatter_plan, len(parts1), grad_x)
    _, landed_a = _split_wait("scatter0a_wait", scatter_a, _scatter_plan, len(parts_a), grad_x)
    _, landed_b = _split_wait("scatter0b_wait", scatter_b, _scatter_plan, len(parts_b), grad_x)
    landed = [landed_a[0], landed1[0], landed_a[1], landed1[1], landed_b[0], landed_b[1], landed1[2]]
    fulls = _share_halves([_sum_slots(f"sum_slots{t}", core_chip, p) for t, p in enumerate(landed)])
    f2 = [f.reshape(-1, f.shape[-1]) for f in fulls]
    grads, deltas, new_m, new_v = {}, {}, {}, {}
    for n, w, mm, vv, gs in (("w_kv", w_kv, m_w_kv, v_w_kv, f2[0:2]), ("w_out", w_out, m_w_out, v_w_out, f2[2:4])):
        grads[n], deltas[n], new_m[n], new_v[n] = _adamw_layers(f"adamw_{n}", w, gs, mm, vv)
    for n, w, mm, vv, g in (("pool_w_in", pool_w_in, m_pool_w_in, v_pool_w_in, f2[4]),
                            ("pool_w_grp", pool_w_grp, m_pool_w_grp, v_pool_w_grp, f2[5]),
                            ("hgrn_w_in", hgrn_w_in, m_hgrn_w_in, v_hgrn_w_in, f2[6])):
        d, mn, vn = _adamw(f"adamw_{n}", flat(w), g, flat(mm), flat(vv))
        grads[n], deltas[n], new_m[n], new_v[n] = g.reshape(w.shape), d.reshape(w.shape), mn.reshape(w.shape), vn.reshape(w.shape)

    Wd = EMIX
    partial = _pack_rows("pack_partials", [d_ng0, d_ng1, d_mng, dcol0[:, :EMIX], dlb, dcol1[:, :EMIX], d_final_g,
                                           loss_part], Wd)
    summed = _small_sum(_allgather_small("allgather_grads", partial), hgrn_lb, 4)
    row = lambda i, n=Wd: summed[i:i + 1, :n]
    nshard = EMIX // N_CHIPS
    g_hg_norm = lax.dynamic_slice_in_dim(row(5), chip * nshard, nshard, axis=1)
    small_names = ["norm_g0", "norm_g1", "mem_norm_g", "pool_scale", "hgrn_lb0", "hgrn_lb1", "hgrn_norm_g", "final_g"]
    small_w = [norm_g[0:1], norm_g[1:2], mem_norm_g.reshape(1, D), pool_scale, hgrn_lb[0:1], hgrn_lb[1:2], hgrn_norm_g,
               final_g.reshape(1, D)]
    small_m = [m_norm_g[0:1], m_norm_g[1:2], m_mem_norm_g.reshape(1, D), m_pool_scale, m_hgrn_lb[0:1], m_hgrn_lb[1:2],
               m_hgrn_norm_g, m_final_g.reshape(1, D)]
    small_v = [v_norm_g[0:1], v_norm_g[1:2], v_mem_norm_g.reshape(1, D), v_pool_scale, v_hgrn_lb[0:1], v_hgrn_lb[1:2],
               v_hgrn_norm_g, v_final_g.reshape(1, D)]
    g_pack = _pack_rows("pack_small_g", [row(0, D), row(1, D), row(2, D), row(3), row(8), row(9), g_hg_norm, row(6, D)], Wd)
    d_pack, m_pack, v_pack = _adamw("adamw_small", _pack_rows("pack_small_w", small_w, Wd), g_pack,
                                    _pack_rows("pack_small_m", small_m, Wd), _pack_rows("pack_small_v", small_v, Wd))
    widths = [v.shape[1] for v in small_w]
    rows = lambda p: {n: p[i, :widths[i]] for i, n in enumerate(small_names)}

    def assemble(r, out):
        out["norm_g"] = jnp.stack([r["norm_g0"], r["norm_g1"]])
        out["mem_norm_g"] = r["mem_norm_g"]
        out["pool_scale"] = r["pool_scale"].reshape(1, EMIX)
        out["hgrn_lb"] = jnp.stack([r["hgrn_lb0"], r["hgrn_lb1"]])
        out["hgrn_norm_g"] = r["hgrn_norm_g"].reshape(1, nshard)
        out["final_g"] = r["final_g"]

    assemble(rows(g_pack), grads)
    assemble(rows(d_pack), deltas)
    assemble(rows(m_pack), new_m)
    assemble(rows(v_pack), new_v)
    loss = summed[7, 0]

    order = ["norm_g", "mem_norm_g", "w_kv", "w_out", "pool_w_in", "pool_w_grp", "pool_scale", "hgrn_w_in", "hgrn_lb",
             "hgrn_norm_g", "final_g"]
    return (loss, grad_x.reshape(1, S, D), *[grads[n] for n in order], *[deltas[n] for n in order],
            *[new_m[n] for n in order], *[new_v[n] for n in order])
```

```python
import functools

import jax
import jax.numpy as jnp
from jax import lax
from jax.experimental import pallas as pl
from jax.experimental.pallas import tpu as pltpu

F32 = jnp.float32
BF16 = jnp.bfloat16
MESH = pl.DeviceIdType.MESH
ANY = pl.BlockSpec(memory_space=pl.ANY)

EPS = 1e-6
HG_HEAD_DIM = 128
HG_CHUNK = 64
CA_HEADS = 4
N_POOL_GROUPS = 4
POOL_HALO = 128
ADAM_LR = 0.001
ADAM_B1 = 0.9
ADAM_B2 = 0.999
ADAM_EPS = 1e-08
ADAM_WD = 0.01
ADAM_STEP = 10
N_CHIPS = 4
N_DEV = 8
VMEM_LIMIT_BYTES = 56 * 1024 * 1024
SMALL_ROWS = 8
STREAM_CHUNK_BYTES = 2 * 1024 * 1024
STREAM_SLOTS = 3


def _params(*sem):
    return pltpu.CompilerParams(dimension_semantics=sem, vmem_limit_bytes=VMEM_LIMIT_BYTES)


def _tile(n, pref):
    t = pref
    while n % t:
        t //= 2
    return t


def _sigmoid(x):
    return 1.0 / (1.0 + jnp.exp(-x))


def _matmul(name, a, b, *, grid, a_spec, b_spec, out_shape, out_spec, acc_shape, dims,
            add=None, add_spec=None, alias=None, after=None, extras=(), epilogue=None):
    nk = grid[2]
    has_add = add is not None
    has_alias = alias is not None
    has_after = after is not None
    n_out = len(out_shape) if epilogue is not None else 1

    def body(*refs):
        a_ref, b_ref = refs[0], refs[1]
        pos = 2
        add_ref = None
        if has_add:
            add_ref = refs[pos]
            pos += 1
        extra_refs = refs[pos:pos + len(extras)]
        pos += len(extras) + has_alias + has_after
        o_refs = refs[pos:pos + n_out]
        prod = lax.dot_general(a_ref[...], b_ref[...], (dims, ((), ())), preferred_element_type=F32)

        def finish(r):
            if epilogue is not None:
                epilogue(r, extra_refs, o_refs)
                return
            if has_add:
                r = r + add_ref[...].astype(F32)
            o_refs[0][...] = r.astype(o_refs[0].dtype)

        if nk == 1:
            finish(prod)
            return
        acc_ref = refs[pos + n_out]
        k = pl.program_id(2)

        @pl.when(k == 0)
        def _():
            acc_ref[...] = prod

        @pl.when(k > 0)
        def _():
            acc_ref[...] += prod

        @pl.when(k == nk - 1)
        def _():
            finish(acc_ref[...])

    operands = [a, b]
    in_specs = [a_spec, b_spec]
    if has_add:
        operands.append(add)
        in_specs.append(add_spec)
    for arr, spec in extras:
        operands.append(arr)
        in_specs.append(spec)
    aliases = {}
    if has_alias:
        aliases = {len(operands): 0}
        operands.append(alias)
        in_specs.append(ANY)
    if has_after:
        operands.append(after)
        in_specs.append(ANY)
    return pl.pallas_call(
        body, name=name, grid=grid, in_specs=in_specs, out_specs=out_spec, out_shape=out_shape,
        scratch_shapes=[pltpu.VMEM(acc_shape, F32)] if nk > 1 else [], input_output_aliases=aliases,
        compiler_params=_params("parallel", "parallel", "arbitrary"),
    )(*operands)


IJ = lambda i, j, k: (i, j)
IK = lambda i, j, k: (i, k)
KJ = lambda i, j, k: (k, j)
KI = lambda i, j, k: (k, i)
NN = ((1,), (0,))
NT = ((1,), (1,))
TN = ((0,), (0,))


def _rms_fwd(name, x, g):
    R, D = x.shape
    tr = _tile(R, 256)

    def body(x_ref, g_ref, o_ref):
        xf = x_ref[...]
        r = lax.rsqrt(jnp.mean(xf * xf, axis=-1, keepdims=True) + EPS)
        o_ref[...] = (xf * r * g_ref[...]).astype(o_ref.dtype)

    return pl.pallas_call(
        body, name=name, grid=(R // tr,),
        in_specs=[pl.BlockSpec((tr, D), lambda i: (i, 0)), pl.BlockSpec((1, D), lambda i: (0, 0))],
        out_specs=pl.BlockSpec((tr, D), lambda i: (i, 0)),
        out_shape=jax.ShapeDtypeStruct((R, D), BF16), compiler_params=_params("parallel"),
    )(x, g)


def _rms_bwd(name, dh, x, g, dres):
    R, D = x.shape
    tr = _tile(R, 256)

    def body(dh_ref, x_ref, g_ref, dres_ref, dx_ref, dxb_ref, dg_ref):
        xf = x_ref[...]
        r = lax.rsqrt(jnp.mean(xf * xf, axis=-1, keepdims=True) + EPS)
        xn = xf * r
        d = dh_ref[...]
        dyg = d * g_ref[...]
        dx = r * (dyg - xn * jnp.mean(dyg * xn, axis=-1, keepdims=True)) + dres_ref[...]
        dx_ref[...] = dx
        dxb_ref[...] = dx.astype(BF16)

        @pl.when(pl.program_id(0) == 0)
        def _():
            dg_ref[...] = jnp.zeros_like(dg_ref)

        dg_ref[...] += jnp.sum(d * xn, axis=0, keepdims=True)

    row = pl.BlockSpec((tr, D), lambda i: (i, 0))
    vec = pl.BlockSpec((1, D), lambda i: (0, 0))
    return pl.pallas_call(
        body, name=name, grid=(R // tr,), in_specs=[row, row, vec, row], out_specs=[row, row, vec],
        out_shape=[jax.ShapeDtypeStruct((R, D), F32), jax.ShapeDtypeStruct((R, D), BF16),
                   jax.ShapeDtypeStruct((1, D), F32)],
        compiler_params=_params("arbitrary"),
    )(dh, x, g, dres)


def _loss_head(x2, g, target):
    R, D = x2.shape
    tr = _tile(R, 256)

    def body(x_ref, g_ref, t_ref, dx_ref, dxb_ref, dg_ref, loss_ref):
        xf = x_ref[...]
        gg = g_ref[...]
        r = lax.rsqrt(jnp.mean(xf * xf, axis=-1, keepdims=True) + EPS)
        xn = xf * r
        e = xn * gg - t_ref[...]
        part = 0.5 * jnp.sum(jnp.mean(e * e, axis=-1, keepdims=True), axis=0, keepdims=True)
        dy = e * (1.0 / D)
        dyg = dy * gg
        dx = r * (dyg - xn * jnp.mean(dyg * xn, axis=-1, keepdims=True))
        dx_ref[...] = dx
        dxb_ref[...] = dx.astype(BF16)

        @pl.when(pl.program_id(0) == 0)
        def _():
            dg_ref[...] = jnp.zeros_like(dg_ref)
            loss_ref[...] = jnp.zeros_like(loss_ref)

        dg_ref[...] += jnp.sum(dy * xn, axis=0, keepdims=True)
        loss_ref[...] += jnp.broadcast_to(part, loss_ref.shape)

    row = pl.BlockSpec((tr, D), lambda i: (i, 0))
    vec = pl.BlockSpec((1, D), lambda i: (0, 0))
    return pl.pallas_call(
        body, name="loss_head", grid=(R // tr,), in_specs=[row, vec, row],
        out_specs=[row, row, vec, pl.BlockSpec((1, 128), lambda i: (0, 0))],
        out_shape=[jax.ShapeDtypeStruct((R, D), F32), jax.ShapeDtypeStruct((R, D), BF16),
                   jax.ShapeDtypeStruct((1, D), F32), jax.ShapeDtypeStruct((1, 128), F32)],
        compiler_params=_params("arbitrary"),
    )(x2, g, target)


def _pool_band(tr, reverse, w):
    r = lax.broadcasted_iota(jnp.int32, (tr, tr + POOL_HALO), 0)
    c = lax.broadcasted_iota(jnp.int32, (tr, tr + POOL_HALO), 1)
    if reverse:
        inside = (c >= r) & (c < r + w)
    else:
        cc = c - POOL_HALO
        inside = (cc <= r) & (cc > r - w)
    return jnp.where(inside, 1.0, 0.0).astype(BF16)


def _pool_fwd(proj, S, EMIX):
    PG = EMIX // N_POOL_GROUPS
    cb = PG
    tr = _tile(S, 512)
    per_group = PG // cb

    def body(u_ref, o_ref, ext):
        i = pl.program_id(1)
        w = jnp.left_shift(2, pl.program_id(0) // per_group)

        @pl.when(i == 0)
        def _():
            ext[0:POOL_HALO, :] = jnp.zeros((POOL_HALO, cb), BF16)

        u = u_ref[...]
        ext[POOL_HALO:, :] = u
        win = jnp.dot(_pool_band(tr, False, w), ext[...], preferred_element_type=F32)
        pos = i * tr + lax.broadcasted_iota(jnp.int32, (tr, 1), 0)
        cnt = jnp.minimum(pos + 1, w).astype(F32)
        o_ref[...] = (win / cnt - u.astype(F32)).astype(BF16)
        ext[0:POOL_HALO, :] = u[tr - POOL_HALO:, :]

    return pl.pallas_call(
        body, name="pool_fwd", grid=(EMIX // cb, S // tr),
        in_specs=[pl.BlockSpec((tr, cb), lambda j, i: (i, j))],
        out_specs=pl.BlockSpec((tr, cb), lambda j, i: (i, j)),
        out_shape=jax.ShapeDtypeStruct((S, EMIX), BF16),
        scratch_shapes=[pltpu.VMEM((tr + POOL_HALO, cb), BF16)],
        compiler_params=_params("parallel", "arbitrary"),
    )(proj)


def _pool_bwd(dpooled, dproj, S, EMIX):
    PG = EMIX // N_POOL_GROUPS
    cb = PG
    tr = _tile(S, 512)
    per_group = PG // cb
    nrt = S // tr

    def body(d_ref, _, o_ref, ext):
        step = pl.program_id(1)
        i = nrt - 1 - step
        w = jnp.left_shift(2, pl.program_id(0) // per_group)

        @pl.when(step == 0)
        def _():
            ext[tr:, :] = jnp.zeros((POOL_HALO, cb), BF16)

        d = d_ref[...]
        pos = i * tr + lax.broadcasted_iota(jnp.int32, (tr, 1), 0)
        cnt = jnp.minimum(pos + 1, w).astype(F32)
        z = (d / cnt).astype(BF16)
        ext[0:tr, :] = z
        win = jnp.dot(_pool_band(tr, True, w), ext[...], preferred_element_type=F32)
        o_ref[...] = (win - d).astype(BF16)
        ext[tr:, :] = z[0:POOL_HALO, :]

    return pl.pallas_call(
        body, name="pool_bwd", grid=(EMIX // cb, nrt),
        in_specs=[pl.BlockSpec((tr, cb), lambda j, s: (nrt - 1 - s, j)), ANY],
        out_specs=pl.BlockSpec((tr, cb), lambda j, s: (nrt - 1 - s, j)),
        out_shape=jax.ShapeDtypeStruct(dproj.shape, dproj.dtype),
        scratch_shapes=[pltpu.VMEM((tr + POOL_HALO, cb), BF16)],
        input_output_aliases={1: 0},
        compiler_params=_params("parallel", "arbitrary"),
    )(dpooled, dproj)


def _ca_fwd(name, proj, qblk, kv, premix, S, ECA, EMIX):
    M = kv.shape[0]
    hd = ECA // CA_HEADS
    ts = _tile(S, 512)
    scale = hd ** -0.5

    def body(q_ref, kv_ref, _, o_ref):
        for h in range(CA_HEADS):
            q = q_ref[:, h * hd:(h + 1) * hd]
            k = kv_ref[:, h * hd:(h + 1) * hd]
            v = kv_ref[:, ECA + h * hd:ECA + (h + 1) * hd]
            s = lax.dot_general(q, k, (NT, ((), ())), preferred_element_type=F32) * scale
            s = s - jnp.max(s, axis=-1, keepdims=True)
            p = jnp.exp(s)
            p = p / jnp.sum(p, axis=-1, keepdims=True)
            o = jnp.dot(p.astype(BF16), v, preferred_element_type=F32)
            o_ref[:, h * hd:(h + 1) * hd] = o.astype(BF16)

    return pl.pallas_call(
        body, name=name, grid=(S // ts,),
        in_specs=[pl.BlockSpec((ts, ECA), lambda i: (i, qblk)), pl.BlockSpec((M, 2 * ECA), lambda i: (0, 0)), ANY],
        out_specs=pl.BlockSpec((ts, ECA), lambda i: (i, EMIX // ECA)),
        out_shape=jax.ShapeDtypeStruct(premix.shape, premix.dtype),
        input_output_aliases={2: 0}, compiler_params=_params("parallel"),
    )(proj, kv, premix)


def _ca_bwd(name, dpremix, proj, qblk, kv, dbuf, dblk, S, ECA, EMIX):
    M = kv.shape[0]
    hd = ECA // CA_HEADS
    ts = _tile(S, 512)
    scale = hd ** -0.5

    def body(do_ref, q_ref, kv_ref, _, dq_ref, dkv_ref):
        @pl.when(pl.program_id(0) == 0)
        def _():
            dkv_ref[...] = jnp.zeros_like(dkv_ref)

        for h in range(CA_HEADS):
            lo, hi = h * hd, (h + 1) * hd
            q = q_ref[:, lo:hi]
            k = kv_ref[:, lo:hi]
            v = kv_ref[:, ECA + lo:ECA + hi]
            do = do_ref[:, lo:hi]
            s = lax.dot_general(q, k, (NT, ((), ())), preferred_element_type=F32) * scale
            s = s - jnp.max(s, axis=-1, keepdims=True)
            p = jnp.exp(s)
            p = p / jnp.sum(p, axis=-1, keepdims=True)
            pb = p.astype(BF16)
            dkv_ref[:, ECA + lo:ECA + hi] += lax.dot_general(pb, do, (TN, ((), ())), preferred_element_type=F32)
            dp = lax.dot_general(do, v, (NT, ((), ())), preferred_element_type=F32)
            ds = (p * (dp - jnp.sum(p * dp, axis=-1, keepdims=True)) * scale).astype(BF16)
            dq_ref[:, lo:hi] = jnp.dot(ds, k, preferred_element_type=F32).astype(BF16)
            dkv_ref[:, lo:hi] += lax.dot_general(ds, q, (TN, ((), ())), preferred_element_type=F32)

    return pl.pallas_call(
        body, name=name, grid=(S // ts,),
        in_specs=[pl.BlockSpec((ts, ECA), lambda i: (i, EMIX // ECA)), pl.BlockSpec((ts, ECA), lambda i: (i, qblk)),
                  pl.BlockSpec((M, 2 * ECA), lambda i: (0, 0)), ANY],
        out_specs=[pl.BlockSpec((ts, ECA), lambda i: (i, dblk)), pl.BlockSpec((M, 2 * ECA), lambda i: (0, 0))],
        out_shape=[jax.ShapeDtypeStruct(dbuf.shape, dbuf.dtype), jax.ShapeDtypeStruct((M, 2 * ECA), F32)],
        input_output_aliases={3: 0}, compiler_params=_params("arbitrary"),
    )(dpremix, proj, kv, dbuf)


def _gate_fwd(name, premix, proj, gblk, colscale, S, EB, ECA):
    ts = _tile(S, 512)

    def body(p_ref, g_ref, c_ref, o_ref):
        g = g_ref[...].astype(F32)
        o_ref[...] = (p_ref[...].astype(F32) * c_ref[...] * (g * _sigmoid(g))).astype(BF16)

    return pl.pallas_call(
        body, name=name, grid=(S // ts, EB // ECA),
        in_specs=[pl.BlockSpec((ts, ECA), lambda i, j: (i, j)), pl.BlockSpec((ts, ECA), lambda i, j: (i, gblk + j)),
                  pl.BlockSpec((1, ECA), lambda i, j: (0, j))],
        out_specs=pl.BlockSpec((ts, ECA), lambda i, j: (i, j)),
        out_shape=jax.ShapeDtypeStruct((S, EB), BF16), compiler_params=_params("parallel", "parallel"),
    )(premix, proj, colscale)


def _gate_bwd_epilogue(db, extra_refs, out_refs):
    p_ref, g_ref, c_ref = extra_refs
    dp_ref, dg_ref, dc_ref = out_refs
    g = g_ref[...].astype(F32)
    sg = _sigmoid(g)
    si = g * sg
    c = c_ref[...]
    t = db * p_ref[...].astype(F32)
    dp_ref[...] = (db * si * c).astype(BF16)
    dg_ref[...] = (t * c * (sg * (1.0 + g * (1.0 - sg)))).astype(BF16)
    dc_ref[...] = jnp.sum(t * si, axis=0, keepdims=True)


def _hgrn_lb(lb_ref):
    l0 = lb_ref[0:1, :]
    l1 = lb_ref[1:2, :]
    mx = jnp.maximum(l0, l1)
    e0 = jnp.exp(l0 - mx)
    e1 = jnp.exp(l1 - mx)
    return e1 / (e0 + e1)


def _bdot(a, b, ca, cb):
    return lax.dot_general(a, b, (((ca,), (cb,)), ((0,), (0,))), preferred_element_type=F32)


def _tri_sum(tri, x):
    hi = x.astype(BF16)
    lo = (x - hi.astype(F32)).astype(BF16)
    tri = tri.astype(BF16)
    return _bdot(tri, hi, 2, 1) + _bdot(tri, lo, 2, 1)


def _hgrn_chunks(qin, fin, lbh, n):
    C = HG_CHUNK
    row = lax.broadcasted_iota(jnp.int32, (n, C, C), 1)
    col = lax.broadcasted_iota(jnp.int32, (n, C, C), 2)
    causal = row >= col
    sg = _sigmoid(fin)
    f = lbh + (1.0 - lbh) * sg
    k = 1.0 - f
    g = jnp.log(f)
    b = _tri_sum(jnp.where(causal, 1.0, 0.0), g)
    b_last = jnp.sum(g, axis=1, keepdims=True)
    eb = jnp.exp(b)
    einv = jnp.exp(-b)
    eend = jnp.exp(b_last - b)
    sq = _sigmoid(qin)
    a = qin * sq * (HG_HEAD_DIM ** -0.5) * eb
    bm = k * einv
    e = k * eend
    d = jnp.exp(b_last)
    p = jnp.where(causal, _bdot(a.astype(BF16), bm.astype(BF16), 2, 2), 0.0)
    return dict(causal=causal, sg=sg, f=f, eb=eb, einv=einv, eend=eend, sq=sq, a=a, bm=bm, e=e, d=d, p=p)


def _hgrn_fwd(proj, fgate, hgrn_lb, S, EMIX, EB):
    HD, C = HG_HEAD_DIM, HG_CHUNK
    HH = EMIX // HD
    hb = 2 if HH % 2 == 0 else 1
    W = hb * HD
    tr = _tile(S, 512)
    n = tr // C

    def body(q_ref, f_ref, i_ref, lb_ref, o_ref, rstd_ref, st_ref, state):
        @pl.when(pl.program_id(1) == 0)
        def _():
            state[...] = jnp.zeros_like(state)

        lb = _hgrn_lb(lb_ref)
        for h in range(hb):
            cs = slice(h * HD, (h + 1) * HD)
            qin = q_ref[:, cs].astype(F32).reshape(n, C, HD)
            fin = f_ref[:, cs].reshape(n, C, HD)
            v = i_ref[:, cs].reshape(n, C, HD)
            t = _hgrn_chunks(qin, fin, lb[:, cs], n)
            upd = _bdot(v, t["e"].astype(BF16), 1, 1)
            st = state[h]
            for c in range(n):
                st_ref[h, c] = st
                st = st * t["d"][c] + upd[c]
            state[h] = st
            o = _bdot(t["p"].astype(BF16), v, 2, 1) + _bdot(t["a"].astype(BF16), st_ref[h].astype(BF16), 2, 2)
            rstd = lax.rsqrt(jnp.mean(o * o, axis=-1, keepdims=True) + EPS)
            o_ref[:, cs] = (o * rstd).reshape(tr, HD).astype(BF16)
            rstd_ref[:, cs] = jnp.broadcast_to(rstd, (n, C, HD)).reshape(tr, HD)

    blk = lambda off: pl.BlockSpec((tr, W), lambda g, i: (i, off + g))
    return pl.pallas_call(
        body, name="hgrn_fwd", grid=(HH // hb, S // tr),
        in_specs=[blk(0), blk(0), blk(2 * EMIX // W), pl.BlockSpec((2, W), lambda g, i: (0, g))],
        out_specs=[blk(0), blk(0), pl.BlockSpec((hb, n, HD, HD), lambda g, i: (g, i, 0, 0))],
        out_shape=[jax.ShapeDtypeStruct((S, EB), BF16), jax.ShapeDtypeStruct((S, EMIX), F32),
                   jax.ShapeDtypeStruct((HH, S // C, HD, HD), F32)],
        scratch_shapes=[pltpu.VMEM((hb, HD, HD), F32)],
        compiler_params=_params("parallel", "arbitrary"),
    )(proj, fgate, proj, hgrn_lb)


def _hgrn_bwd(dpremix, premix, rstd, states, proj, fgate, hgrn_lb, S, EMIX):
    HD, C = HG_HEAD_DIM, HG_CHUNK
    HH = EMIX // HD
    hb = 2 if HH % 2 == 0 else 1
    W = hb * HD
    tr = _tile(S, 512)
    n = tr // C
    nrt = S // tr

    def body(do_ref, on_ref, rstd_ref, st_ref, q_ref, f_ref, i_ref, lb_ref, d_ref, dlb_ref, dstate, dsbuf):
        @pl.when(pl.program_id(1) == 0)
        def _():
            dstate[...] = jnp.zeros_like(dstate)
            dlb_ref[...] = jnp.zeros_like(dlb_ref)

        lb = _hgrn_lb(lb_ref)
        for h in range(hb):
            cs = slice(h * HD, (h + 1) * HD)
            qin = q_ref[:, cs].astype(F32).reshape(n, C, HD)
            fin = f_ref[:, cs].reshape(n, C, HD)
            v = i_ref[:, cs].reshape(n, C, HD)
            lbh = lb[:, cs]
            t = _hgrn_chunks(qin, fin, lbh, n)
            a, bm, e, d, p = t["a"], t["bm"], t["e"], t["d"], t["p"]
            ab, bmb, eb16 = a.astype(BF16), bm.astype(BF16), e.astype(BF16)
            on = on_ref[:, cs].astype(F32).reshape(n, C, HD)
            dn = do_ref[:, cs].astype(F32).reshape(n, C, HD)
            do = rstd_ref[:, cs].reshape(n, C, HD) * (dn - on * jnp.mean(dn * on, axis=-1, keepdims=True))
            dob = do.astype(BF16)
            grow = _bdot(dob, ab, 1, 1)
            ds = dstate[h]
            for c in reversed(range(n)):
                dsbuf[h, c] = ds
                ds = ds * d[c] + grow[c]
            dstate[h] = ds
            dst = dsbuf[h]
            st = st_ref[h]
            dstb = dst.astype(BF16)
            dp = jnp.where(t["causal"], _bdot(dob, v, 2, 2), 0.0).astype(BF16)
            dv = _bdot(p.astype(BF16), dob, 1, 1) + _bdot(eb16, dstb, 2, 2)
            da = _bdot(dp, bmb, 2, 1) + _bdot(dob, st.astype(BF16), 2, 1)
            dbm = _bdot(dp, ab, 1, 1)
            de = _bdot(v, dstb, 2, 1)
            dd = jnp.sum(dst * st, axis=1, keepdims=True)
            dk = dbm * t["einv"] + de * t["eend"]
            dee = de * e
            db = da * a - dbm * bm - dee
            extra = jnp.sum(dee, axis=1, keepdims=True) + dd * d
            upper = jnp.where(lax.broadcasted_iota(jnp.int32, (n, C, C), 2)
                              >= lax.broadcasted_iota(jnp.int32, (n, C, C), 1), 1.0, 0.0)
            dg = _tri_sum(upper, db) + extra
            df = dg / t["f"] - dk
            sg, sq = t["sg"], t["sq"]
            dq = da * t["eb"] * (HD ** -0.5) * (sq * (1.0 + qin * (1.0 - sq)))
            d_ref[0, :, cs] = dq.reshape(tr, HD).astype(BF16)
            d_ref[1, :, cs] = (df * (1.0 - lbh) * sg * (1.0 - sg)).reshape(tr, HD).astype(BF16)
            d_ref[2, :, cs] = dv.reshape(tr, HD).astype(BF16)
            dlb_ref[:, cs] += jnp.sum((df * (1.0 - sg)).reshape(tr, HD), axis=0, keepdims=True)

    rev = lambda off: pl.BlockSpec((tr, W), lambda g, s: (nrt - 1 - s, off + g))
    return pl.pallas_call(
        body, name="hgrn_bwd", grid=(HH // hb, nrt),
        in_specs=[rev(0), rev(0), rev(0), pl.BlockSpec((hb, n, HD, HD), lambda g, s: (g, nrt - 1 - s, 0, 0)),
                  rev(0), rev(0), rev(2 * EMIX // W), pl.BlockSpec((2, W), lambda g, s: (0, g))],
        out_specs=[pl.BlockSpec((3, tr, W), lambda g, s: (0, nrt - 1 - s, g)), pl.BlockSpec((1, W), lambda g, s: (0, g))],
        out_shape=[jax.ShapeDtypeStruct((3, S, EMIX), BF16), jax.ShapeDtypeStruct((1, EMIX), F32)],
        scratch_shapes=[pltpu.VMEM((hb, HD, HD), F32), pltpu.VMEM((hb, n, HD, HD), F32)],
        compiler_params=_params("parallel", "arbitrary"),
    )(dpremix, premix, rstd, states, proj, fgate, proj, hgrn_lb)


EW_BLOCK_ELEMS = 512 * 1024


def _ew_tiles(R, C):
    tc = C if C <= 4096 else _tile(C, 2048)
    tr = _tile(R, 512)
    while tr * tc > EW_BLOCK_ELEMS and tr % 16 == 0:
        tr //= 2
    return tr, tc


def _add_halves(name, core_chip, grad, got):
    _, _, R, C = grad.shape
    tr, tc = _ew_tiles(R, C)

    def body(c_ref, a_ref, b_ref, o_ref, own_ref):
        r = (a_ref[...].astype(F32) + b_ref[...].astype(F32)).astype(BF16)
        o_ref[...] = r

        @pl.when(pl.program_id(2) == c_ref[1])
        def _():
            own_ref[...] = r

    blk = pl.BlockSpec((None, tr, tc), lambda i, j, s, c: (s, i, j))
    sds = jax.ShapeDtypeStruct(got.shape, BF16)
    return pl.pallas_call(
        body, name=name, out_shape=[sds, sds],
        grid_spec=pltpu.PrefetchScalarGridSpec(
            num_scalar_prefetch=1, grid=(R // tr, C // tc, N_CHIPS),
            in_specs=[pl.BlockSpec((None, None, tr, tc), lambda i, j, s, c: (s, c[0], i, j)), blk],
            out_specs=[blk, pl.BlockSpec((None, tr, tc), lambda i, j, s, c: (c[1], i, j))]),
        compiler_params=_params("parallel", "parallel", "arbitrary"),
    )(core_chip, grad, got)


def _sum_slots(name, core, parts):
    _, R, C = parts.shape
    tr, tc = _ew_tiles(R, C)

    def body(c_ref, p_ref, o_ref):
        acc = p_ref[0].astype(F32)
        for s in range(1, N_CHIPS):
            acc = acc + p_ref[s].astype(F32)
        o_ref[...] = acc

    return pl.pallas_call(
        body, name=name, out_shape=jax.ShapeDtypeStruct((2, R, C), F32),
        grid_spec=pltpu.PrefetchScalarGridSpec(
            num_scalar_prefetch=1, grid=(R // tr, C // tc),
            in_specs=[pl.BlockSpec((N_CHIPS, tr, tc), lambda i, j, c: (0, i, j))],
            out_specs=pl.BlockSpec((None, tr, tc), lambda i, j, c: (c[0], i, j))),
        compiler_params=_params("parallel", "parallel"),
    )(core, parts)


def _adam_step(w, g, m, v):
    mn = ADAM_B1 * m + (1.0 - ADAM_B1) * g
    vn = ADAM_B2 * v + (1.0 - ADAM_B2) * (g * g)
    m_hat = mn / (1.0 - ADAM_B1 ** ADAM_STEP)
    v_hat = vn / (1.0 - ADAM_B2 ** ADAM_STEP)
    return -ADAM_LR * (m_hat / (jnp.sqrt(v_hat) + ADAM_EPS) + ADAM_WD * w), mn, vn


def _adamw(name, w, g, m, v):
    R, C = w.shape
    tr, tc = _ew_tiles(R, C)

    def body(w_ref, g_ref, m_ref, v_ref, d_ref, mo_ref, vo_ref):
        d_ref[...], mo_ref[...], vo_ref[...] = _adam_step(w_ref[...], g_ref[...], m_ref[...], v_ref[...])

    blk = pl.BlockSpec((tr, tc), lambda i, j: (i, j))
    sds = jax.ShapeDtypeStruct((R, C), F32)
    return pl.pallas_call(
        body, name=name, grid=(R // tr, C // tc), in_specs=[blk] * 4, out_specs=[blk] * 3, out_shape=[sds] * 3,
        compiler_params=_params("parallel", "parallel"),
    )(w, g, m, v)


def _adamw_layers(name, w, gs, m, v):
    L, R, C = w.shape
    tr, tc = _ew_tiles(R, C)

    def body(*refs):
        w_ref, m_ref, v_ref = refs[:3]
        g_refs = refs[3:3 + L]
        go_ref, d_ref, mo_ref, vo_ref = refs[3 + L:]
        layer = pl.program_id(0)
        g = g_refs[0][...]
        for n in range(1, L):
            g = jnp.where(layer == n, g_refs[n][...], g)
        go_ref[...] = g
        d_ref[...], mo_ref[...], vo_ref[...] = _adam_step(w_ref[...], g, m_ref[...], v_ref[...])

    blk = pl.BlockSpec((None, tr, tc), lambda l, i, j: (l, i, j))
    of_layer = lambda n: pl.BlockSpec((tr, tc), lambda l, i, j: (jnp.where(l == n, i, 0), jnp.where(l == n, j, 0)))
    sds = jax.ShapeDtypeStruct((L, R, C), F32)
    return pl.pallas_call(
        body, name=name, grid=(L, R // tr, C // tc), in_specs=[blk] * 3 + [of_layer(n) for n in range(L)],
        out_specs=[blk] * 4, out_shape=[sds] * 4, compiler_params=_params("parallel", "parallel", "parallel"),
    )(w, m, v, *gs)


def _pack_rows(name, vecs, W):
    nv = len(vecs)

    def body(*refs):
        o_ref = refs[nv]
        o_ref[...] = jnp.zeros_like(o_ref)
        for i in range(nv):
            o_ref[i:i + 1, 0:vecs[i].shape[1]] = jnp.sum(refs[i][...], axis=0, keepdims=True)

    vm = pl.BlockSpec(memory_space=pltpu.VMEM)
    return pl.pallas_call(
        body, name=name, in_specs=[vm] * nv, out_specs=vm, out_shape=jax.ShapeDtypeStruct((SMALL_ROWS, W), F32),
    )(*vecs)


def _small_sum(gathered, hgrn_lb, lb_row):
    _, T, W = gathered.shape

    def body(g_ref, lb_ref, o_ref):
        acc = g_ref[0]
        for dev in range(1, N_DEV):
            acc = acc + g_ref[dev]
        o_ref[0:T, :] = acc
        lb = _hgrn_lb(lb_ref)
        d1 = o_ref[lb_row:lb_row + 1, :] * (lb * (1.0 - lb))
        o_ref[T:2 * T, :] = jnp.zeros((T, W), F32)
        o_ref[T:T + 1, :] = -d1
        o_ref[T + 1:T + 2, :] = d1

    vm = pl.BlockSpec(memory_space=pltpu.VMEM)
    return pl.pallas_call(
        body, name="small_sum", in_specs=[vm, vm], out_specs=vm, out_shape=jax.ShapeDtypeStruct((2 * T, W), F32),
    )(gathered, hgrn_lb)


def _place():
    return lax.axis_index("x"), lax.axis_index("y"), lax.axis_index("c")


def _other_chips(x, y):
    return [(1 - x, y), (x, 1 - y), (1 - x, 1 - y)]


def _chunk_rows(rows, row_bytes):
    cr = rows
    while cr * row_bytes > STREAM_CHUNK_BYTES and cr % 32 == 0:
        cr //= 2
    return cr


def _stream(pairs, buf, sems, t, peer):
    lsem, ssem, rsem = sems
    n = len(pairs)
    loads, sent = [None] * n, [None] * n

    def load(k):
        slot = k % STREAM_SLOTS
        if k >= STREAM_SLOTS:
            sent[k - STREAM_SLOTS]()
        loads[k] = pltpu.make_async_copy(pairs[k][0], buf.at[slot], lsem.at[t, slot])
        loads[k].start()

    load(0)
    for k in range(n):
        slot = k % STREAM_SLOTS
        if k + 1 < n:
            load(k + 1)
        loads[k].wait()
        if peer is None:
            cp = pltpu.make_async_copy(buf.at[slot], pairs[k][1], ssem.at[t, slot])
            cp.start()
            sent[k] = cp.wait
        else:
            cp = pltpu.make_async_remote_copy(src_ref=buf.at[slot], dst_ref=pairs[k][1], send_sem=ssem.at[t, slot],
                                              recv_sem=rsem.at[t], device_id=peer, device_id_type=MESH)
            cp.start()
            sent[k] = cp.wait_send
    for k in range(max(0, n - STREAM_SLOTS), n):
        sent[k]()


def _stream_scratch(shapes):
    nt = len(shapes)
    return ([pltpu.VMEM((STREAM_SLOTS,) + s, d) for s, d in shapes]
            + [pltpu.SemaphoreType.DMA((nt, STREAM_SLOTS)), pltpu.SemaphoreType.DMA((nt, STREAM_SLOTS)),
               pltpu.SemaphoreType.DMA((nt,))])


def _exchange_halves(name, grads):
    nt = len(grads)
    hs = [g.shape[1] // 2 for g in grads]
    crs = [_chunk_rows(h, g.shape[2] * g.dtype.itemsize) for h, g in zip(hs, grads)]

    def body(*refs):
        ins, gots, bufs, sems = refs[:nt], refs[nt:2 * nt], refs[2 * nt:3 * nt], refs[3 * nt:]
        x, y, c = _place()
        sib = (x, y, 1 - c)
        for t in range(nt):
            h, cr = hs[t], crs[t]
            pairs = [(ins[t].at[b, pl.ds((1 - c) * h + r0, cr)], gots[t].at[b, pl.ds(r0, cr)])
                     for b in range(N_CHIPS) for r0 in range(0, h, cr)]
            _stream(pairs, bufs[t], sems, t, sib)
        for t in range(nt):
            pltpu.make_async_remote_copy(src_ref=gots[t], dst_ref=gots[t], send_sem=sems[1].at[t, 0],
                                         recv_sem=sems[2].at[t], device_id=sib, device_id_type=MESH).wait_recv()

    return pl.pallas_call(
        body, name=name, in_specs=[ANY] * nt, out_specs=[ANY] * nt,
        out_shape=[jax.ShapeDtypeStruct((N_CHIPS, h, g.shape[2]), g.dtype) for h, g in zip(hs, grads)],
        scratch_shapes=_stream_scratch([((cr, g.shape[2]), g.dtype) for cr, g in zip(crs, grads)]),
        compiler_params=pltpu.CompilerParams(vmem_limit_bytes=VMEM_LIMIT_BYTES),
    )(*grads)


def _scatter_plan(srcs, dsts):
    x, y, c = _place()
    me = 2 * x + y
    return [(srcs[t].at[2 * px + py], dsts[t].at[me], (px, py, c))
            for t in range(len(srcs)) for px, py in _other_chips(x, y)]


def _slot(dst, chip, r0, rows, cols):
    if len(dst.shape) == 3:
        return dst.at[chip, pl.ds(r0, rows)]
    return dst.at[pl.ds(r0, rows), pl.ds(pl.multiple_of(chip * cols, 128), cols)]


def _gather_plan(srcs, dsts):
    x, y, c = _place()
    me = 2 * x + y
    plan = []
    for t in range(len(srcs)):
        h, cols = srcs[t].shape[0] // 2, srcs[t].shape[1]
        plan += [(srcs[t].at[pl.ds(c * h, h)], _slot(dsts[t], me, c * h, h, cols), (px, py, c))
                 for px, py in _other_chips(x, y)]
    return plan


HBM_SPEC = pl.BlockSpec(memory_space=pltpu.HBM)
SEM_SPEC = pl.BlockSpec(memory_space=pltpu.SEMAPHORE)


def _split_start(name, srcs, dsts, plan, ncopies, after):
    bufs = [pltpu.with_memory_space_constraint(a, pltpu.HBM) for a in list(srcs) + list(dsts)]
    nb, ns = len(bufs), len(srcs)
    operands = bufs + ([after] if after is not None else [])

    def body(*refs):
        outs = refs[len(operands):]
        send, recv, token = outs[0], outs[1], outs[-1]
        for i, (src, dst, dev) in enumerate(plan(refs[:ns], refs[ns:nb])):
            pltpu.make_async_remote_copy(src_ref=src, dst_ref=dst, send_sem=send.at[i], recv_sem=recv.at[i],
                                         device_id=dev, device_id_type=MESH).start()
        token[...] = jnp.zeros_like(token)

    res = pl.pallas_call(
        body, name=name,
        out_shape=[pltpu.SemaphoreType.DMA((ncopies,)), pltpu.SemaphoreType.DMA((ncopies,))]
        + [pltpu.HBM(a.shape, a.dtype) for a in bufs] + [jax.ShapeDtypeStruct((8, 128), F32)],
        in_specs=[HBM_SPEC] * nb + [ANY] * (len(operands) - nb),
        out_specs=[SEM_SPEC, SEM_SPEC] + [HBM_SPEC] * nb + [pl.BlockSpec(memory_space=pltpu.VMEM)],
        input_output_aliases={i: 2 + i for i in range(nb)},
        compiler_params=pltpu.CompilerParams(has_side_effects=pltpu.SideEffectType.DATAFLOW_SIDE_EFFECTING),
    )(*operands)
    return res[:-1], res[-1]


def _split_wait(name, started, plan, ns, after):
    send, recv, bufs = started[0], started[1], list(started[2:])
    nb = len(bufs)

    def body(*refs):
        send_ref, recv_ref = refs[nb], refs[nb + 1]
        for i, (src, dst, dev) in enumerate(plan(refs[:ns], refs[ns:nb])):
            cp = pltpu.make_async_remote_copy(src_ref=src, dst_ref=dst, send_sem=send_ref.at[i], recv_sem=recv_ref.at[i],
                                              device_id=dev, device_id_type=MESH)
            cp.wait_send()
            cp.wait_recv()

    res = pl.pallas_call(
        body, name=name, out_shape=[pltpu.HBM(a.shape, a.dtype) for a in bufs],
        in_specs=[HBM_SPEC] * nb + [SEM_SPEC, SEM_SPEC, ANY], out_specs=[HBM_SPEC] * nb,
        input_output_aliases={i: i for i in range(nb)},
        compiler_params=pltpu.CompilerParams(has_side_effects=pltpu.SideEffectType.DATAFLOW_SIDE_EFFECTING),
    )(*bufs, send, recv, after)
    return res[:ns], res[ns:]


def _gather_finish(name, shards, stacks):
    nt = len(shards)
    hs = [s.shape[0] // 2 for s in shards]
    crs = [_chunk_rows(h, s.shape[1] * s.dtype.itemsize) for h, s in zip(hs, shards)]

    def body(*refs):
        ins, outs, bufs, sems = refs[:nt], refs[2 * nt:3 * nt], refs[3 * nt:4 * nt], refs[4 * nt:]
        x, y, c = _place()
        me = 2 * x + y
        sib = (x, y, 1 - c)
        for t in range(nt):
            h, cr, cols = hs[t], crs[t], shards[t].shape[1]
            passed = [_slot(outs[t], 2 * px + py, c * h + r0, cr, cols)
                      for px, py in _other_chips(x, y) for r0 in range(0, h, cr)]
            _stream([(r, r) for r in passed], bufs[t], sems, t, sib)
            own = [(ins[t].at[pl.ds(r0, cr)], _slot(outs[t], me, r0, cr, cols)) for r0 in range(0, 2 * h, cr)]
            _stream(own, bufs[t], sems, t, None)
        for t in range(nt):
            if len(stacks[t].shape) == 3:
                three = outs[t].at[pl.ds(0, 3), pl.ds(0, hs[t])]
            else:
                three = outs[t].at[pl.ds(0, hs[t]), pl.ds(0, 3 * shards[t].shape[1])]
            pltpu.make_async_remote_copy(src_ref=three, dst_ref=three, send_sem=sems[1].at[t, 0],
                                         recv_sem=sems[2].at[t], device_id=sib, device_id_type=MESH).wait_recv()

    return pl.pallas_call(
        body, name=name, in_specs=[ANY] * (2 * nt), out_specs=[ANY] * nt,
        out_shape=[jax.ShapeDtypeStruct(s.shape, s.dtype) for s in stacks],
        scratch_shapes=_stream_scratch([((cr, s.shape[1]), s.dtype) for cr, s in zip(crs, shards)]),
        input_output_aliases={nt + t: t for t in range(nt)},
        compiler_params=pltpu.CompilerParams(vmem_limit_bytes=VMEM_LIMIT_BYTES),
    )(*shards, *stacks)


def _share_halves(fulls):
    nt = len(fulls)
    crs = [_chunk_rows(f.shape[1], f.shape[2] * f.dtype.itemsize) for f in fulls]

    def body(*refs):
        outs, bufs, sems = refs[nt:2 * nt], refs[2 * nt:3 * nt], refs[3 * nt:]
        x, y, c = _place()
        sib = (x, y, 1 - c)
        for t in range(nt):
            rows = [outs[t].at[c, pl.ds(r0, crs[t])] for r0 in range(0, fulls[t].shape[1], crs[t])]
            _stream([(r, r) for r in rows], bufs[t], sems, t, sib)
        for t in range(nt):
            other = outs[t].at[1 - c]
            pltpu.make_async_remote_copy(src_ref=other, dst_ref=other, send_sem=sems[1].at[t, 0],
                                         recv_sem=sems[2].at[t], device_id=sib, device_id_type=MESH).wait_recv()

    return pl.pallas_call(
        body, name="share_halves", in_specs=[ANY] * nt, out_specs=[ANY] * nt,
        out_shape=[jax.ShapeDtypeStruct(f.shape, f.dtype) for f in fulls],
        scratch_shapes=_stream_scratch([((cr, f.shape[2]), f.dtype) for cr, f in zip(crs, fulls)]),
        input_output_aliases={t: t for t in range(nt)},
        compiler_params=pltpu.CompilerParams(vmem_limit_bytes=VMEM_LIMIT_BYTES),
    )(*fulls)


def _allgather_small(name, v):
    def body(v_ref, o_ref, send, recv, lsem):
        x, y, c = _place()
        me = 4 * x + 2 * y + c
        loc = pltpu.make_async_copy(v_ref, o_ref.at[me], lsem)
        loc.start()
        copies = []
        for k in range(1, N_DEV):
            px = 1 - x if k & 4 else x
            py = 1 - y if k & 2 else y
            pc = 1 - c if k & 1 else c
            cp = pltpu.make_async_remote_copy(
                src_ref=v_ref, dst_ref=o_ref.at[me], send_sem=send.at[k - 1], recv_sem=recv.at[k - 1],
                device_id=(px, py, pc), device_id_type=MESH)
            cp.start()
            copies.append(cp)
        for cp in copies:
            cp.wait()
        loc.wait()

    vm = pl.BlockSpec(memory_space=pltpu.VMEM)
    return pl.pallas_call(
        body, name=name, in_specs=[vm], out_specs=vm,
        out_shape=jax.ShapeDtypeStruct((N_DEV,) + v.shape, v.dtype),
        scratch_shapes=[pltpu.SemaphoreType.DMA((N_DEV - 1,))] * 2 + [pltpu.SemaphoreType.DMA],
    )(v)


def kernel(x, mem, norm_g, mem_norm_g, w_kv, w_out, pool_w_in, pool_w_grp, pool_scale, hgrn_w_in, hgrn_lb, hgrn_norm_g, final_g, loss_target, m_norm_g, m_mem_norm_g, m_w_kv, m_w_out, m_pool_w_in, m_pool_w_grp, m_pool_scale, m_hgrn_w_in, m_hgrn_lb, m_hgrn_norm_g, m_final_g, v_norm_g, v_mem_norm_g, v_w_kv, v_w_out, v_pool_w_in, v_pool_w_grp, v_pool_scale, v_hgrn_w_in, v_hgrn_lb, v_hgrn_norm_g, v_final_g):
    _, S, D = x.shape
    M = mem.shape[1]
    EB = 2 * D
    ECA = EB // 4
    EMIX = EB - ECA
    PG = EMIX // N_POOL_GROUPS
    NP0 = EMIX + ECA + EB
    NP1 = 3 * EMIX + ECA + EB
    SH0, SH1 = NP0 // N_CHIPS, NP1 // N_CHIPS
    DK, EK = D // N_CHIPS, EB // N_CHIPS
    TNP = 512 if all(v % 512 == 0 for v in (SH0, SH1, ECA, EMIX)) else 256
    TM = _tile(S, 1024)
    TMF = _tile(S, 2048)
    TD = _tile(D, 512)
    TDW = _tile(D, 1024)
    c0, c1 = SH0 // TNP, SH1 // TNP
    qt, et = EMIX // TNP, ECA // TNP
    chip = 2 * lax.axis_index("x") + lax.axis_index("y")

    xs, ms, tgt = x[0], mem[0], loss_target[0]

    flat = lambda w: w.reshape(-1, w.shape[-1])
    sds = jax.ShapeDtypeStruct
    stack_of = lambda s: lax.empty((N_CHIPS,) + s.shape, BF16)
    group_a = [flat(pool_w_in).astype(BF16)]
    gather_a, token = _split_start("gather_a_start", group_a, [stack_of(s) for s in group_a], _gather_plan, 3, None)
    t0 = token[0:1, 0:1]
    bf = lambda w: (w + t0).astype(BF16)
    group_b = [bf(w_kv[0]), bf(w_out[0]), bf(flat(pool_w_grp))]
    gather_b, token = _split_start("gather_b_start", group_b, [stack_of(s) for s in group_b], _gather_plan, 9, token)
    group_c = [bf(flat(hgrn_w_in))]
    gather_c, token = _split_start("gather_c_start", group_c, [lax.empty((D, NP1), BF16)], _gather_plan, 3, token)
    group_d = [bf(w_kv[1]), bf(w_out[1])]
    gather_d, token = _split_start("gather_d_start", group_d, [stack_of(s) for s in group_d], _gather_plan, 6, token)
    started = token[0:1, 0:1]

    tek, tew = _tile(EK, 512), _tile(EK, 1024)

    mem_n = _rms_fwd("rms_mem", ms, mem_norm_g.reshape(1, D) + started)
    norm_tiles = _allgather_small("allgather_norm_g", jnp.pad(hgrn_norm_g + started, ((0, SMALL_ROWS - 1), (0, 0))))
    hg_norm = norm_tiles[0::2, 0, :].reshape(1, EMIX)
    h0 = _rms_fwd("rms0", xs, norm_g[0:1] + started)
    wpin, = _gather_finish("gather_a_finish", *_split_wait("gather_a_wait", gather_a, _gather_plan, 1, h0))

    tkw = _tile(2 * ECA, 1024)

    def kv_of(layer, wkv):
        return _matmul(
            f"kv{layer}", mem_n, wkv.reshape(D, 2 * ECA), grid=(1, 2 * ECA // tkw, 1),
            a_spec=pl.BlockSpec((M, D), lambda i, j, k: (0, 0)), b_spec=pl.BlockSpec((D, tkw), lambda i, j, k: (0, j)),
            out_shape=sds((M, 2 * ECA), BF16), out_spec=pl.BlockSpec((M, tkw), lambda i, j, k: (0, j)),
            acc_shape=(M, tkw), dims=NN)

    tko = _tile(EB, 2048)

    def out_proj(layer, branch, wout, resid):
        return _matmul(
            f"out_proj{layer}", branch, wout.reshape(EB, D), grid=(S // TM, D // TDW, EB // tko),
            a_spec=pl.BlockSpec((TM, tko), IK), b_spec=pl.BlockSpec((tko, TDW), KJ),
            out_shape=sds((S, D), F32), out_spec=pl.BlockSpec((TM, TDW), IJ),
            acc_shape=(TM, TDW), dims=NN, add=resid, add_spec=pl.BlockSpec((TM, TDW), IJ))

    ones_ca = jnp.ones((1, ECA), F32)

    proj0 = _matmul(
        "proj0", h0, wpin, grid=(S // TMF, NP0 // TNP, 1),
        a_spec=pl.BlockSpec((TMF, D), lambda i, j, k: (i, 0)),
        b_spec=pl.BlockSpec((None, D, TNP), lambda i, j, k: (j // c0, 0, j % c0)),
        out_shape=sds((S, NP0), BF16), out_spec=pl.BlockSpec((TMF, TNP), IJ),
        acc_shape=(TMF, TNP), dims=NN)
    pooled = _pool_fwd(proj0, S, EMIX)
    wkv0, wout0, g_grp = _gather_finish("gather_b_finish", *_split_wait("gather_b_wait", gather_b, _gather_plan, 3, pooled))
    wgrp = g_grp.reshape(N_CHIPS, N_POOL_GROUPS, PG // N_CHIPS, PG).transpose(1, 0, 2, 3).reshape(N_POOL_GROUPS, PG, PG)
    kv = [kv_of(0, wkv0), None]
    premix0 = _matmul(
        "pool_grp", pooled, wgrp, grid=(S // TM, N_POOL_GROUPS, 1),
        a_spec=pl.BlockSpec((TM, PG), lambda i, j, k: (i, j)),
        b_spec=pl.BlockSpec((None, PG, PG), lambda i, j, k: (j, 0, 0)),
        out_shape=sds((S, EB), BF16), out_spec=pl.BlockSpec((TM, PG), lambda i, j, k: (i, j)),
        acc_shape=(TM, PG), dims=NN)
    premix0 = _ca_fwd("ca_fwd0", proj0, EMIX // ECA, kv[0], premix0, S, ECA, EMIX)
    colscale0 = jnp.concatenate([pool_scale.reshape(1, EMIX), ones_ca], axis=1)
    gblk0 = (EMIX + ECA) // ECA
    branch0 = _gate_fwd("gate_fwd0", premix0, proj0, gblk0, colscale0, S, EB, ECA)
    x1 = out_proj(0, branch0, wout0, xs)

    whin, = _gather_finish("gather_c_finish", *_split_wait("gather_c_wait", gather_c, _gather_plan, 1, x1))
    h1 = _rms_fwd("rms1", x1, norm_g[1:2])

    def proj1_cols(name, ncols, col_of, out_cols, out_dtype, out_col_of):
        return _matmul(
            name, h1, whin, grid=(S // TMF, ncols, 1),
            a_spec=pl.BlockSpec((TMF, D), lambda i, j, k: (i, 0)),
            b_spec=pl.BlockSpec((D, TNP), lambda i, j, k: (0, col_of(j))),
            out_shape=sds((S, out_cols), out_dtype), out_spec=pl.BlockSpec((TMF, TNP), lambda i, j, k: (i, out_col_of(j))),
            acc_shape=(TMF, TNP), dims=NN)

    skip_f = lambda j: jnp.where(j < qt, j, j + qt)
    proj1 = proj1_cols("proj1", NP1 // TNP - qt, skip_f, NP1, BF16, skip_f)
    fgate = proj1_cols("proj1_f", qt, lambda j: j + qt, EMIX, F32, lambda j: j)
    premix1, rstd1, states = _hgrn_fwd(proj1, fgate, hgrn_lb, S, EMIX, EB)
    wkv1, wout1 = _gather_finish("gather_d_finish", *_split_wait("gather_d_wait", gather_d, _gather_plan, 2, rstd1))
    kv[1] = kv_of(1, wkv1)
    premix1 = _ca_fwd("ca_fwd1", proj1, 3 * EMIX // ECA, kv[1], premix1, S, ECA, EMIX)
    colscale1 = jnp.concatenate([hg_norm, ones_ca], axis=1)
    gblk1 = (3 * EMIX + ECA) // ECA
    branch1 = _gate_fwd("gate_fwd1", premix1, proj1, gblk1, colscale1, S, EB, ECA)
    x2 = out_proj(1, branch1, wout1, x1)

    dx2, dx2b, d_final_g, loss_part = _loss_head(x2, final_g.reshape(1, D), tgt)

    def out_proj_bwd(layer, dxb, branch, wout, premix, proj, gblk, colscale, dshape, dblk):
        goff, doff = gblk * ECA // tek, dblk * ECA // tek
        dpremix, dgate, dcol = _matmul(
            f"dbranch{layer}", dxb, wout, grid=(S // TMF, EB // tek, 1),
            a_spec=pl.BlockSpec((TMF, D), lambda i, j, k: (i, 0)),
            b_spec=pl.BlockSpec((None, tek, D), lambda i, j, k: (j // (EK // tek), j % (EK // tek), 0)),
            extras=[(premix, pl.BlockSpec((TMF, tek), IJ)), (proj, pl.BlockSpec((TMF, tek), lambda i, j, k: (i, goff + j))),
                    (colscale, pl.BlockSpec((1, tek), lambda i, j, k: (0, j)))],
            epilogue=_gate_bwd_epilogue,
            out_shape=[sds((S, EB), BF16), sds(dshape, BF16), sds((S // TMF, 1, EB), F32)],
            out_spec=[pl.BlockSpec((TMF, tek), IJ), pl.BlockSpec((TMF, tek), lambda i, j, k: (i, doff + j)),
                      pl.BlockSpec((None, 1, tek), lambda i, j, k: (i, 0, j))],
            acc_shape=(TMF, tek), dims=NT)
        dw = _matmul(
            f"dwout{layer}", branch, dxb, grid=(EB // tew, D // TD, 1),
            a_spec=pl.BlockSpec((S, tew), lambda i, j, k: (0, i)), b_spec=pl.BlockSpec((S, TD), lambda i, j, k: (0, j)),
            out_shape=sds((N_CHIPS, EK, D), BF16),
            out_spec=pl.BlockSpec((None, tew, TD), lambda i, j, k: (i // (EK // tew), i % (EK // tew), j)),
            acc_shape=(tew, TD), dims=TN)
        return dpremix, dgate, dcol.reshape(S // TMF, EB), dw

    def kv_bwd(layer, dkv, wkv, dmem_add):
        dkvb = dkv.astype(BF16)
        dmem = _matmul(
            f"dmem{layer}", dkvb, wkv.reshape(D, 2 * ECA), grid=(1, D // TDW, 1),
            a_spec=pl.BlockSpec((M, 2 * ECA), lambda i, j, k: (0, 0)),
            b_spec=pl.BlockSpec((TDW, 2 * ECA), lambda i, j, k: (j, 0)),
            out_shape=sds((M, D), F32), out_spec=pl.BlockSpec((M, TDW), lambda i, j, k: (0, j)), acc_shape=(M, TDW),
            dims=NT, add=dmem_add, add_spec=pl.BlockSpec((M, TDW), lambda i, j, k: (0, j)))
        dw = _matmul(
            f"dwkv{layer}", mem_n, dkvb, grid=(D // TDW, 2 * ECA // tkw, 1),
            a_spec=pl.BlockSpec((M, TDW), lambda i, j, k: (0, i)), b_spec=pl.BlockSpec((M, tkw), lambda i, j, k: (0, j)),
            out_shape=sds((D, 2 * ECA), BF16), out_spec=pl.BlockSpec((TDW, tkw), IJ), acc_shape=(TDW, tkw), dims=TN)
        return dmem, dw.reshape(N_CHIPS, DK, 2 * ECA)

    dpremix1, drest1, dcol1, gw_out1 = out_proj_bwd(1, dx2b, branch1, wout1, premix1, proj1, gblk1, colscale1,
                                                    (S, ECA + EB), 1)
    drest1, dkv1 = _ca_bwd("ca_bwd1", dpremix1, proj1, 3 * EMIX // ECA, kv[1], drest1, 0, S, ECA, EMIX)
    dqfi, dlb = _hgrn_bwd(dpremix1, premix1, rstd1, states, proj1, fgate, hgrn_lb, S, EMIX)
    nq, nr = 3 * qt, (ECA + EB) // TNP
    tkh = _tile(EMIX, 1024) if (ECA + EB) % _tile(EMIX, 1024) == 0 else TNP
    kq = EMIX // tkh
    dh1 = _matmul(
        "dh1_qfi", dqfi, whin, grid=(S // TM, D // TDW, 3),
        a_spec=pl.BlockSpec((None, TM, EMIX), lambda i, j, k: (k, i, 0)),
        b_spec=pl.BlockSpec((TDW, EMIX), lambda i, j, k: (j, k)),
        out_shape=sds((S, D), F32), out_spec=pl.BlockSpec((TM, TDW), IJ), acc_shape=(TM, TDW), dims=NT)
    dh1 = _matmul(
        "dh1_rest", drest1, whin, grid=(S // TM, D // TDW, (ECA + EB) // tkh), a_spec=pl.BlockSpec((TM, tkh), IK),
        b_spec=pl.BlockSpec((TDW, tkh), lambda i, j, k: (j, k + 3 * kq)),
        out_shape=sds((S, D), F32), out_spec=pl.BlockSpec((TM, TDW), IJ), acc_shape=(TM, TDW), dims=NT,
        add=dh1, add_spec=pl.BlockSpec((TM, TDW), IJ))
    gw_hin = _matmul(
        "dwhin_qfi", h1, dqfi, grid=(D // TDW, nq, 1), a_spec=pl.BlockSpec((S, TDW), lambda i, j, k: (0, i)),
        b_spec=pl.BlockSpec((None, S, TNP), lambda i, j, k: (j // qt, 0, j % qt)),
        out_shape=sds((N_CHIPS, D, SH1), BF16), out_spec=pl.BlockSpec((None, TDW, TNP), lambda i, j, k: (j // c1, i, j % c1)),
        acc_shape=(TDW, TNP), dims=TN)
    gw_hin = _matmul(
        "dwhin_rest", h1, drest1, grid=(D // TDW, nr, 1), a_spec=pl.BlockSpec((S, TDW), lambda i, j, k: (0, i)),
        b_spec=pl.BlockSpec((S, TNP), lambda i, j, k: (0, j)), out_shape=sds((N_CHIPS, D, SH1), BF16),
        out_spec=pl.BlockSpec((None, TDW, TNP), lambda i, j, k: ((j + nq) // c1, i, (j + nq) % c1)),
        acc_shape=(TDW, TNP), dims=TN, alias=gw_hin)
    dmem, gw_kv1 = kv_bwd(1, dkv1, wkv1, None)

    core_chip = jnp.stack([lax.axis_index("c"), chip]).astype(jnp.int32)

    def reduce_in_chip(tag, stacks):
        got = _exchange_halves(f"exchange_halves{tag}", stacks)
        pairs = [_add_halves(f"add_halves{tag}_{t}", core_chip, g.reshape(N_CHIPS, 2, g.shape[1] // 2, g.shape[2]), r)
                 for t, (g, r) in enumerate(zip(stacks, got))]
        return [p for p, _ in pairs], [own for _, own in pairs]

    parts1, landed1 = reduce_in_chip(1, [gw_kv1, gw_out1, gw_hin])
    scatter1, token1 = _split_start("scatter1_start", parts1, landed1, _scatter_plan, 3 * len(parts1), None)
    dx1, dx1b, d_ng1 = _rms_bwd("rms_bwd1", dh1, x1, norm_g[1:2] + token1[0:1, 0:1], dx2)

    dpremix0, dproj0, dcol0, gw_out0 = out_proj_bwd(0, dx1b, branch0, wout0, premix0, proj0, gblk0, colscale0,
                                                    (S, NP0), gblk0)
    dproj0, dkv0 = _ca_bwd("ca_bwd0", dpremix0, proj0, EMIX // ECA, kv[0], dproj0, EMIX // ECA, S, ECA, EMIX)
    dmem, gw_kv0 = kv_bwd(0, dkv0, wkv0, dmem)
    parts_a, landed_a = reduce_in_chip("0a", [gw_kv0, gw_out0])
    scatter_a, token_a = _split_start("scatter0a_start", parts_a, landed_a, _scatter_plan, 3 * len(parts_a), None)
    dpooled = _matmul(
        "dpooled", dpremix0, wgrp, grid=(S // TM, N_POOL_GROUPS, 1), a_spec=pl.BlockSpec((TM, PG), IJ),
        b_spec=pl.BlockSpec((None, PG, PG), lambda i, j, k: (j, 0, 0)),
        out_shape=sds((S, EMIX), F32), out_spec=pl.BlockSpec((TM, PG), IJ), acc_shape=(TM, PG), dims=NT, after=token_a)
    dwgrp = _matmul(
        "dwgrp", pooled, dpremix0, grid=(N_POOL_GROUPS, 1, 1), a_spec=pl.BlockSpec((S, PG), lambda i, j, k: (0, i)),
        b_spec=pl.BlockSpec((S, PG), lambda i, j, k: (0, i)), out_shape=sds((N_POOL_GROUPS, PG, PG), F32),
        out_spec=pl.BlockSpec((None, PG, PG), lambda i, j, k: (i, 0, 0)), acc_shape=(PG, PG), dims=TN)
    dproj0 = _pool_bwd(dpooled, dproj0, S, EMIX)
    gw_pin = _matmul(
        "dwpin", h0, dproj0, grid=(D // TDW, NP0 // TNP, 1), a_spec=pl.BlockSpec((S, TDW), lambda i, j, k: (0, i)),
        b_spec=pl.BlockSpec((S, TNP), lambda i, j, k: (0, j)), out_shape=sds((N_CHIPS, D, SH0), BF16),
        out_spec=pl.BlockSpec((None, TDW, TNP), lambda i, j, k: (j // c0, i, j % c0)), acc_shape=(TDW, TNP), dims=TN)
    gw_grp = dwgrp.reshape(N_POOL_GROUPS, N_CHIPS, PG // N_CHIPS, PG).transpose(1, 0, 2, 3).reshape(N_CHIPS, PG, PG)
    parts_b, landed_b = reduce_in_chip("0b", [gw_pin, gw_grp.astype(BF16)])
    scatter_b, token_b = _split_start("scatter0b_start", parts_b, landed_b, _scatter_plan, 3 * len(parts_b), None)
    dh0 = _matmul(
        "dh0", dproj0, wpin, grid=(S // TM, D // TDW, N_CHIPS), a_spec=pl.BlockSpec((TM, SH0), IK),
        b_spec=pl.BlockSpec((None, TDW, SH0), lambda i, j, k: (k, j, 0)),
        out_shape=sds((S, D), F32), out_spec=pl.BlockSpec((TM, TDW), IJ), acc_shape=(TM, TDW), dims=NT, after=token_b)
    grad_x, _, d_ng0 = _rms_bwd("rms_bwd0", dh0, xs, norm_g[0:1], dx1)
    _, _, d_mng = _rms_bwd("rms_bwd_mem", dmem, ms, mem_norm_g.reshape(1, D), jnp.zeros_like(ms))

    _, landed1 = _split_wait("scatter1_wait", scatter1, _scatter_plan, len(parts1), grad_x)
    _, landed_a = _split_wait("scatter0a_wait", scatter_a, _scatter_plan, len(parts_a), grad_x)
    _, landed_b = _split_wait("scatter0b_wait", scatter_b, _scatter_plan, len(parts_b), grad_x)
    landed = [landed_a[0], landed1[0], landed_a[1], landed1[1], landed_b[0], landed_b[1], landed1[2]]
    fulls = _share_halves([_sum_slots(f"sum_slots{t}", core_chip, p) for t, p in enumerate(landed)])
    f2 = [f.reshape(-1, f.shape[-1]) for f in fulls]
    grads, deltas, new_m, new_v = {}, {}, {}, {}
    for n, w, mm, vv, gs in (("w_kv", w_kv, m_w_kv, v_w_kv, f2[0:2]), ("w_out", w_out, m_w_out, v_w_out, f2[2:4]),
                             ("pool_w_in", pool_w_in, m_pool_w_in, v_pool_w_in, f2[4:5]),
                             ("pool_w_grp", pool_w_grp, m_pool_w_grp, v_pool_w_grp, f2[5:6]),
                             ("hgrn_w_in", hgrn_w_in, m_hgrn_w_in, v_hgrn_w_in, f2[6:7])):
        as3d = lambda a: a.reshape((a.shape[0], -1, a.shape[-1]))
        outs = _adamw_layers(f"adamw_{n}", as3d(w), gs, as3d(mm), as3d(vv))
        grads[n], deltas[n], new_m[n], new_v[n] = [o.reshape(w.shape) for o in outs]

    Wd = EMIX
    partial = _pack_rows("pack_partials", [d_ng0, d_ng1, d_mng, dcol0[:, :EMIX], dlb, dcol1[:, :EMIX], d_final_g,
                                           loss_part], Wd)
    summed = _small_sum(_allgather_small("allgather_grads", partial), hgrn_lb, 4)
    row = lambda i, n=Wd: summed[i:i + 1, :n]
    nshard = EMIX // N_CHIPS
    g_hg_norm = lax.dynamic_slice_in_dim(row(5), chip * nshard, nshard, axis=1)
    small_names = ["norm_g0", "norm_g1", "mem_norm_g", "pool_scale", "hgrn_lb0", "hgrn_lb1", "hgrn_norm_g", "final_g"]
    small_w = [norm_g[0:1], norm_g[1:2], mem_norm_g.reshape(1, D), pool_scale, hgrn_lb[0:1], hgrn_lb[1:2], hgrn_norm_g,
               final_g.reshape(1, D)]
    small_m = [m_norm_g[0:1], m_norm_g[1:2], m_mem_norm_g.reshape(1, D), m_pool_scale, m_hgrn_lb[0:1], m_hgrn_lb[1:2],
               m_hgrn_norm_g, m_final_g.reshape(1, D)]
    small_v = [v_norm_g[0:1], v_norm_g[1:2], v_mem_norm_g.reshape(1, D), v_pool_scale, v_hgrn_lb[0:1], v_hgrn_lb[1:2],
               v_hgrn_norm_g, v_final_g.reshape(1, D)]
    g_pack = _pack_rows("pack_small_g", [row(0, D), row(1, D), row(2, D), row(3), row(8), row(9), g_hg_norm, row(6, D)], Wd)
    d_pack, m_pack, v_pack = _adamw("adamw_small", _pack_rows("pack_small_w", small_w, Wd), g_pack,
                                    _pack_rows("pack_small_m", small_m, Wd), _pack_rows("pack_small_v", small_v, Wd))
    widths = [v.shape[1] for v in small_w]
    rows = lambda p: {n: p[i, :widths[i]] for i, n in enumerate(small_names)}

    def assemble(r, out):
        out["norm_g"] = jnp.stack([r["norm_g0"], r["norm_g1"]])
        out["mem_norm_g"] = r["mem_norm_g"]
        out["pool_scale"] = r["pool_scale"].reshape(1, EMIX)
        out["hgrn_lb"] = jnp.stack([r["hgrn_lb0"], r["hgrn_lb1"]])
        out["hgrn_norm_g"] = r["hgrn_norm_g"].reshape(1, nshard)
        out["final_g"] = r["final_g"]

    assemble(rows(g_pack), grads)
    assemble(rows(d_pack), deltas)
    assemble(rows(m_pack), new_m)
    assemble(rows(v_pack), new_v)
    loss = summed[7, 0]

    order = ["norm_g", "mem_norm_g", "w_kv", "w_out", "pool_w_in", "pool_w_grp", "pool_scale", "hgrn_w_in", "hgrn_lb",
             "hgrn_norm_g", "final_g"]
    return (loss, grad_x.reshape(1, S, D), *[grads[n] for n in order], *[deltas[n] for n in order],
            *[new_m[n] for n in order], *[new_v[n] for n in order])
```

```python
import functools

import jax
import jax.numpy as jnp
from jax import lax
from jax.experimental import pallas as pl
from jax.experimental.pallas import tpu as pltpu

F32 = jnp.float32
BF16 = jnp.bfloat16
MESH = pl.DeviceIdType.MESH
ANY = pl.BlockSpec(memory_space=pl.ANY)

EPS = 1e-6
HG_HEAD_DIM = 128
HG_CHUNK = 64
CA_HEADS = 4
N_POOL_GROUPS = 4
POOL_HALO = 128
ADAM_LR = 0.001
ADAM_B1 = 0.9
ADAM_B2 = 0.999
ADAM_EPS = 1e-08
ADAM_WD = 0.01
ADAM_STEP = 10
N_CHIPS = 4
N_DEV = 8
VMEM_LIMIT_BYTES = 56 * 1024 * 1024
SMALL_ROWS = 8
STREAM_CHUNK_BYTES = 2 * 1024 * 1024
STREAM_SLOTS = 3
SUM_SLOTS = 2


def _params(*sem):
    return pltpu.CompilerParams(dimension_semantics=sem, vmem_limit_bytes=VMEM_LIMIT_BYTES)


def _tile(n, pref):
    t = pref
    while n % t:
        t //= 2
    return t


def _sigmoid(x):
    return 1.0 / (1.0 + jnp.exp(-x))


def _matmul(name, a, b, *, grid, a_spec, b_spec, out_shape, out_spec, acc_shape, dims,
            add=None, add_spec=None, alias=None, after=None, extras=(), epilogue=None):
    nk = grid[2]
    has_add = add is not None
    has_alias = alias is not None
    has_after = after is not None
    n_out = len(out_shape) if epilogue is not None else 1

    def body(*refs):
        a_ref, b_ref = refs[0], refs[1]
        pos = 2
        add_ref = None
        if has_add:
            add_ref = refs[pos]
            pos += 1
        extra_refs = refs[pos:pos + len(extras)]
        pos += len(extras) + has_alias + has_after
        o_refs = refs[pos:pos + n_out]
        prod = lax.dot_general(a_ref[...], b_ref[...], (dims, ((), ())), preferred_element_type=F32)

        def finish(r):
            if epilogue is not None:
                epilogue(r, extra_refs, o_refs)
                return
            if has_add:
                r = r + add_ref[...].astype(F32)
            o_refs[0][...] = r.astype(o_refs[0].dtype)

        if nk == 1:
            finish(prod)
            return
        acc_ref = refs[pos + n_out]
        k = pl.program_id(2)

        @pl.when(k == 0)
        def _():
            acc_ref[...] = prod

        @pl.when(k > 0)
        def _():
            acc_ref[...] += prod

        @pl.when(k == nk - 1)
        def _():
            finish(acc_ref[...])

    operands = [a, b]
    in_specs = [a_spec, b_spec]
    if has_add:
        operands.append(add)
        in_specs.append(add_spec)
    for arr, spec in extras:
        operands.append(arr)
        in_specs.append(spec)
    aliases = {}
    if has_alias:
        aliases = {len(operands): 0}
        operands.append(alias)
        in_specs.append(ANY)
    if has_after:
        operands.append(after)
        in_specs.append(ANY)
    return pl.pallas_call(
        body, name=name, grid=grid, in_specs=in_specs, out_specs=out_spec, out_shape=out_shape,
        scratch_shapes=[pltpu.VMEM(acc_shape, F32)] if nk > 1 else [], input_output_aliases=aliases,
        compiler_params=_params("parallel", "parallel", "arbitrary"),
    )(*operands)


IJ = lambda i, j, k: (i, j)
IK = lambda i, j, k: (i, k)
KJ = lambda i, j, k: (k, j)
KI = lambda i, j, k: (k, i)
NN = ((1,), (0,))
NT = ((1,), (1,))
TN = ((0,), (0,))


def _rms_fwd(name, x, g):
    R, D = x.shape
    tr = _tile(R, 256)

    def body(x_ref, g_ref, o_ref):
        xf = x_ref[...]
        r = lax.rsqrt(jnp.mean(xf * xf, axis=-1, keepdims=True) + EPS)
        o_ref[...] = (xf * r * g_ref[...]).astype(o_ref.dtype)

    return pl.pallas_call(
        body, name=name, grid=(R // tr,),
        in_specs=[pl.BlockSpec((tr, D), lambda i: (i, 0)), pl.BlockSpec((1, D), lambda i: (0, 0))],
        out_specs=pl.BlockSpec((tr, D), lambda i: (i, 0)),
        out_shape=jax.ShapeDtypeStruct((R, D), BF16), compiler_params=_params("parallel"),
    )(x, g)


def _rms_bwd(name, dh, x, g, dres):
    R, D = x.shape
    tr = _tile(R, 256)

    def body(dh_ref, x_ref, g_ref, dres_ref, dx_ref, dxb_ref, dg_ref):
        xf = x_ref[...]
        r = lax.rsqrt(jnp.mean(xf * xf, axis=-1, keepdims=True) + EPS)
        xn = xf * r
        d = dh_ref[...]
        dyg = d * g_ref[...]
        dx = r * (dyg - xn * jnp.mean(dyg * xn, axis=-1, keepdims=True)) + dres_ref[...]
        dx_ref[...] = dx
        dxb_ref[...] = dx.astype(BF16)

        @pl.when(pl.program_id(0) == 0)
        def _():
            dg_ref[...] = jnp.zeros_like(dg_ref)

        dg_ref[...] += jnp.sum(d * xn, axis=0, keepdims=True)

    row = pl.BlockSpec((tr, D), lambda i: (i, 0))
    vec = pl.BlockSpec((1, D), lambda i: (0, 0))
    return pl.pallas_call(
        body, name=name, grid=(R // tr,), in_specs=[row, row, vec, row], out_specs=[row, row, vec],
        out_shape=[jax.ShapeDtypeStruct((R, D), F32), jax.ShapeDtypeStruct((R, D), BF16),
                   jax.ShapeDtypeStruct((1, D), F32)],
        compiler_params=_params("arbitrary"),
    )(dh, x, g, dres)


def _loss_head(x2, g, target):
    R, D = x2.shape
    tr = _tile(R, 256)

    def body(x_ref, g_ref, t_ref, dx_ref, dxb_ref, dg_ref, loss_ref):
        xf = x_ref[...]
        gg = g_ref[...]
        r = lax.rsqrt(jnp.mean(xf * xf, axis=-1, keepdims=True) + EPS)
        xn = xf * r
        e = xn * gg - t_ref[...]
        part = 0.5 * jnp.sum(jnp.mean(e * e, axis=-1, keepdims=True), axis=0, keepdims=True)
        dy = e * (1.0 / D)
        dyg = dy * gg
        dx = r * (dyg - xn * jnp.mean(dyg * xn, axis=-1, keepdims=True))
        dx_ref[...] = dx
        dxb_ref[...] = dx.astype(BF16)

        @pl.when(pl.program_id(0) == 0)
        def _():
            dg_ref[...] = jnp.zeros_like(dg_ref)
            loss_ref[...] = jnp.zeros_like(loss_ref)

        dg_ref[...] += jnp.sum(dy * xn, axis=0, keepdims=True)
        loss_ref[...] += jnp.broadcast_to(part, loss_ref.shape)

    row = pl.BlockSpec((tr, D), lambda i: (i, 0))
    vec = pl.BlockSpec((1, D), lambda i: (0, 0))
    return pl.pallas_call(
        body, name="loss_head", grid=(R // tr,), in_specs=[row, vec, row],
        out_specs=[row, row, vec, pl.BlockSpec((1, 128), lambda i: (0, 0))],
        out_shape=[jax.ShapeDtypeStruct((R, D), F32), jax.ShapeDtypeStruct((R, D), BF16),
                   jax.ShapeDtypeStruct((1, D), F32), jax.ShapeDtypeStruct((1, 128), F32)],
        compiler_params=_params("arbitrary"),
    )(x2, g, target)


def _pool_band(tr, reverse, w):
    r = lax.broadcasted_iota(jnp.int32, (tr, tr + POOL_HALO), 0)
    c = lax.broadcasted_iota(jnp.int32, (tr, tr + POOL_HALO), 1)
    if reverse:
        inside = (c >= r) & (c < r + w)
    else:
        cc = c - POOL_HALO
        inside = (cc <= r) & (cc > r - w)
    return jnp.where(inside, 1.0, 0.0).astype(BF16)


def _pool_fwd(proj, S, EMIX):
    PG = EMIX // N_POOL_GROUPS
    cb = PG
    tr = _tile(S, 512)
    per_group = PG // cb

    def body(u_ref, o_ref, ext):
        i = pl.program_id(1)
        w = jnp.left_shift(2, pl.program_id(0) // per_group)

        @pl.when(i == 0)
        def _():
            ext[0:POOL_HALO, :] = jnp.zeros((POOL_HALO, cb), BF16)

        u = u_ref[...]
        ext[POOL_HALO:, :] = u
        win = jnp.dot(_pool_band(tr, False, w), ext[...], preferred_element_type=F32)
        pos = i * tr + lax.broadcasted_iota(jnp.int32, (tr, 1), 0)
        cnt = jnp.minimum(pos + 1, w).astype(F32)
        o_ref[...] = (win / cnt - u.astype(F32)).astype(BF16)
        ext[0:POOL_HALO, :] = u[tr - POOL_HALO:, :]

    return pl.pallas_call(
        body, name="pool_fwd", grid=(EMIX // cb, S // tr),
        in_specs=[pl.BlockSpec((tr, cb), lambda j, i: (i, j))],
        out_specs=pl.BlockSpec((tr, cb), lambda j, i: (i, j)),
        out_shape=jax.ShapeDtypeStruct((S, EMIX), BF16),
        scratch_shapes=[pltpu.VMEM((tr + POOL_HALO, cb), BF16)],
        compiler_params=_params("parallel", "arbitrary"),
    )(proj)


def _pool_bwd(dpooled, dproj, S, EMIX):
    PG = EMIX // N_POOL_GROUPS
    cb = PG
    tr = _tile(S, 512)
    per_group = PG // cb
    nrt = S // tr

    def body(d_ref, _, o_ref, ext):
        step = pl.program_id(1)
        i = nrt - 1 - step
        w = jnp.left_shift(2, pl.program_id(0) // per_group)

        @pl.when(step == 0)
        def _():
            ext[tr:, :] = jnp.zeros((POOL_HALO, cb), BF16)

        d = d_ref[...]
        pos = i * tr + lax.broadcasted_iota(jnp.int32, (tr, 1), 0)
        cnt = jnp.minimum(pos + 1, w).astype(F32)
        z = (d / cnt).astype(BF16)
        ext[0:tr, :] = z
        win = jnp.dot(_pool_band(tr, True, w), ext[...], preferred_element_type=F32)
        o_ref[...] = (win - d).astype(BF16)
        ext[tr:, :] = z[0:POOL_HALO, :]

    return pl.pallas_call(
        body, name="pool_bwd", grid=(EMIX // cb, nrt),
        in_specs=[pl.BlockSpec((tr, cb), lambda j, s: (nrt - 1 - s, j)), ANY],
        out_specs=pl.BlockSpec((tr, cb), lambda j, s: (nrt - 1 - s, j)),
        out_shape=jax.ShapeDtypeStruct(dproj.shape, dproj.dtype),
        scratch_shapes=[pltpu.VMEM((tr + POOL_HALO, cb), BF16)],
        input_output_aliases={1: 0},
        compiler_params=_params("parallel", "arbitrary"),
    )(dpooled, dproj)


def _ca_fwd(name, proj, qblk, kv, premix, S, ECA, EMIX):
    M = kv.shape[0]
    hd = ECA // CA_HEADS
    ts = _tile(S, 512)
    scale = hd ** -0.5

    def body(q_ref, kv_ref, _, o_ref):
        for h in range(CA_HEADS):
            q = q_ref[:, h * hd:(h + 1) * hd]
            k = kv_ref[:, h * hd:(h + 1) * hd]
            v = kv_ref[:, ECA + h * hd:ECA + (h + 1) * hd]
            s = lax.dot_general(q, k, (NT, ((), ())), preferred_element_type=F32) * scale
            s = s - jnp.max(s, axis=-1, keepdims=True)
            p = jnp.exp(s)
            p = p / jnp.sum(p, axis=-1, keepdims=True)
            o = jnp.dot(p.astype(BF16), v, preferred_element_type=F32)
            o_ref[:, h * hd:(h + 1) * hd] = o.astype(BF16)

    return pl.pallas_call(
        body, name=name, grid=(S // ts,),
        in_specs=[pl.BlockSpec((ts, ECA), lambda i: (i, qblk)), pl.BlockSpec((M, 2 * ECA), lambda i: (0, 0)), ANY],
        out_specs=pl.BlockSpec((ts, ECA), lambda i: (i, EMIX // ECA)),
        out_shape=jax.ShapeDtypeStruct(premix.shape, premix.dtype),
        input_output_aliases={2: 0}, compiler_params=_params("parallel"),
    )(proj, kv, premix)


def _ca_bwd(name, dpremix, proj, qblk, kv, dbuf, dblk, S, ECA, EMIX):
    M = kv.shape[0]
    hd = ECA // CA_HEADS
    ts = _tile(S, 512)
    scale = hd ** -0.5

    def body(do_ref, q_ref, kv_ref, _, dq_ref, dkv_ref):
        @pl.when(pl.program_id(0) == 0)
        def _():
            dkv_ref[...] = jnp.zeros_like(dkv_ref)

        for h in range(CA_HEADS):
            lo, hi = h * hd, (h + 1) * hd
            q = q_ref[:, lo:hi]
            k = kv_ref[:, lo:hi]
            v = kv_ref[:, ECA + lo:ECA + hi]
            do = do_ref[:, lo:hi]
            s = lax.dot_general(q, k, (NT, ((), ())), preferred_element_type=F32) * scale
            s = s - jnp.max(s, axis=-1, keepdims=True)
            p = jnp.exp(s)
            p = p / jnp.sum(p, axis=-1, keepdims=True)
            pb = p.astype(BF16)
            dkv_ref[:, ECA + lo:ECA + hi] += lax.dot_general(pb, do, (TN, ((), ())), preferred_element_type=F32)
            dp = lax.dot_general(do, v, (NT, ((), ())), preferred_element_type=F32)
            ds = (p * (dp - jnp.sum(p * dp, axis=-1, keepdims=True)) * scale).astype(BF16)
            dq_ref[:, lo:hi] = jnp.dot(ds, k, preferred_element_type=F32).astype(BF16)
            dkv_ref[:, lo:hi] += lax.dot_general(ds, q, (TN, ((), ())), preferred_element_type=F32)

    return pl.pallas_call(
        body, name=name, grid=(S // ts,),
        in_specs=[pl.BlockSpec((ts, ECA), lambda i: (i, EMIX // ECA)), pl.BlockSpec((ts, ECA), lambda i: (i, qblk)),
                  pl.BlockSpec((M, 2 * ECA), lambda i: (0, 0)), ANY],
        out_specs=[pl.BlockSpec((ts, ECA), lambda i: (i, dblk)), pl.BlockSpec((M, 2 * ECA), lambda i: (0, 0))],
        out_shape=[jax.ShapeDtypeStruct(dbuf.shape, dbuf.dtype), jax.ShapeDtypeStruct((M, 2 * ECA), F32)],
        input_output_aliases={3: 0}, compiler_params=_params("arbitrary"),
    )(dpremix, proj, kv, dbuf)


def _gate_fwd(name, premix, proj, gblk, colscale, S, EB, ECA):
    ts = _tile(S, 512)

    def body(p_ref, g_ref, c_ref, o_ref):
        g = g_ref[...].astype(F32)
        o_ref[...] = (p_ref[...].astype(F32) * c_ref[...] * (g * _sigmoid(g))).astype(BF16)

    return pl.pallas_call(
        body, name=name, grid=(S // ts, EB // ECA),
        in_specs=[pl.BlockSpec((ts, ECA), lambda i, j: (i, j)), pl.BlockSpec((ts, ECA), lambda i, j: (i, gblk + j)),
                  pl.BlockSpec((1, ECA), lambda i, j: (0, j))],
        out_specs=pl.BlockSpec((ts, ECA), lambda i, j: (i, j)),
        out_shape=jax.ShapeDtypeStruct((S, EB), BF16), compiler_params=_params("parallel", "parallel"),
    )(premix, proj, colscale)


def _gate_bwd_epilogue(db, extra_refs, out_refs):
    p_ref, g_ref, c_ref = extra_refs
    dp_ref, dg_ref, dc_ref = out_refs
    g = g_ref[...].astype(F32)
    sg = _sigmoid(g)
    si = g * sg
    c = c_ref[...]
    t = db * p_ref[...].astype(F32)
    dp_ref[...] = (db * si * c).astype(BF16)
    dg_ref[...] = (t * c * (sg * (1.0 + g * (1.0 - sg)))).astype(BF16)
    dc_ref[...] = jnp.sum(t * si, axis=0, keepdims=True)


def _hgrn_lb(lb_ref):
    l0 = lb_ref[0:1, :]
    l1 = lb_ref[1:2, :]
    mx = jnp.maximum(l0, l1)
    e0 = jnp.exp(l0 - mx)
    e1 = jnp.exp(l1 - mx)
    return e1 / (e0 + e1)


def _bdot(a, b, ca, cb):
    return lax.dot_general(a, b, (((ca,), (cb,)), ((0,), (0,))), preferred_element_type=F32)


def _tri_sum(tri, x):
    hi = x.astype(BF16)
    lo = (x - hi.astype(F32)).astype(BF16)
    tri = tri.astype(BF16)
    return _bdot(tri, hi, 2, 1) + _bdot(tri, lo, 2, 1)


def _hgrn_chunks(qin, fin, lbh, n):
    C = HG_CHUNK
    row = lax.broadcasted_iota(jnp.int32, (n, C, C), 1)
    col = lax.broadcasted_iota(jnp.int32, (n, C, C), 2)
    causal = row >= col
    sg = _sigmoid(fin)
    f = lbh + (1.0 - lbh) * sg
    k = 1.0 - f
    g = jnp.log(f)
    b = _tri_sum(jnp.where(causal, 1.0, 0.0), g)
    b_last = jnp.sum(g, axis=1, keepdims=True)
    eb = jnp.exp(b)
    einv = jnp.exp(-b)
    eend = jnp.exp(b_last - b)
    sq = _sigmoid(qin)
    a = qin * sq * (HG_HEAD_DIM ** -0.5) * eb
    bm = k * einv
    e = k * eend
    d = jnp.exp(b_last)
    p = jnp.where(causal, _bdot(a.astype(BF16), bm.astype(BF16), 2, 2), 0.0)
    return dict(causal=causal, sg=sg, f=f, eb=eb, einv=einv, eend=eend, sq=sq, a=a, bm=bm, e=e, d=d, p=p)


def _hgrn_fwd(proj, fgate, hgrn_lb, S, EMIX, EB):
    HD, C = HG_HEAD_DIM, HG_CHUNK
    HH = EMIX // HD
    hb = 2 if HH % 2 == 0 else 1
    W = hb * HD
    tr = _tile(S, 512)
    n = tr // C

    def body(q_ref, f_ref, i_ref, lb_ref, o_ref, rstd_ref, st_ref, state):
        @pl.when(pl.program_id(1) == 0)
        def _():
            state[...] = jnp.zeros_like(state)

        lb = _hgrn_lb(lb_ref)
        for h in range(hb):
            cs = slice(h * HD, (h + 1) * HD)
            qin = q_ref[:, cs].astype(F32).reshape(n, C, HD)
            fin = f_ref[:, cs].reshape(n, C, HD)
            v = i_ref[:, cs].reshape(n, C, HD)
            t = _hgrn_chunks(qin, fin, lb[:, cs], n)
            upd = _bdot(v, t["e"].astype(BF16), 1, 1)
            st = state[h]
            for c in range(n):
                st_ref[h, c] = st
                st = st * t["d"][c] + upd[c]
            state[h] = st
            o = _bdot(t["p"].astype(BF16), v, 2, 1) + _bdot(t["a"].astype(BF16), st_ref[h].astype(BF16), 2, 2)
            rstd = lax.rsqrt(jnp.mean(o * o, axis=-1, keepdims=True) + EPS)
            o_ref[:, cs] = (o * rstd).reshape(tr, HD).astype(BF16)
            rstd_ref[:, cs] = jnp.broadcast_to(rstd, (n, C, HD)).reshape(tr, HD)

    blk = lambda off: pl.BlockSpec((tr, W), lambda g, i: (i, off + g))
    return pl.pallas_call(
        body, name="hgrn_fwd", grid=(HH // hb, S // tr),
        in_specs=[blk(0), blk(0), blk(2 * EMIX // W), pl.BlockSpec((2, W), lambda g, i: (0, g))],
        out_specs=[blk(0), blk(0), pl.BlockSpec((hb, n, HD, HD), lambda g, i: (g, i, 0, 0))],
        out_shape=[jax.ShapeDtypeStruct((S, EB), BF16), jax.ShapeDtypeStruct((S, EMIX), F32),
                   jax.ShapeDtypeStruct((HH, S // C, HD, HD), F32)],
        scratch_shapes=[pltpu.VMEM((hb, HD, HD), F32)],
        compiler_params=_params("parallel", "arbitrary"),
    )(proj, fgate, proj, hgrn_lb)


def _hgrn_bwd(dpremix, premix, rstd, states, proj, fgate, hgrn_lb, S, EMIX):
    HD, C = HG_HEAD_DIM, HG_CHUNK
    HH = EMIX // HD
    hb = 2 if HH % 2 == 0 else 1
    W = hb * HD
    tr = _tile(S, 512)
    n = tr // C
    nrt = S // tr

    def body(do_ref, on_ref, rstd_ref, st_ref, q_ref, f_ref, i_ref, lb_ref, d_ref, dlb_ref, dstate, dsbuf):
        @pl.when(pl.program_id(1) == 0)
        def _():
            dstate[...] = jnp.zeros_like(dstate)
            dlb_ref[...] = jnp.zeros_like(dlb_ref)

        lb = _hgrn_lb(lb_ref)
        for h in range(hb):
            cs = slice(h * HD, (h + 1) * HD)
            qin = q_ref[:, cs].astype(F32).reshape(n, C, HD)
            fin = f_ref[:, cs].reshape(n, C, HD)
            v = i_ref[:, cs].reshape(n, C, HD)
            lbh = lb[:, cs]
            t = _hgrn_chunks(qin, fin, lbh, n)
            a, bm, e, d, p = t["a"], t["bm"], t["e"], t["d"], t["p"]
            ab, bmb, eb16 = a.astype(BF16), bm.astype(BF16), e.astype(BF16)
            on = on_ref[:, cs].astype(F32).reshape(n, C, HD)
            dn = do_ref[:, cs].astype(F32).reshape(n, C, HD)
            do = rstd_ref[:, cs].reshape(n, C, HD) * (dn - on * jnp.mean(dn * on, axis=-1, keepdims=True))
            dob = do.astype(BF16)
            grow = _bdot(dob, ab, 1, 1)
            ds = dstate[h]
            for c in reversed(range(n)):
                dsbuf[h, c] = ds
                ds = ds * d[c] + grow[c]
            dstate[h] = ds
            dst = dsbuf[h]
            st = st_ref[h]
            dstb = dst.astype(BF16)
            dp = jnp.where(t["causal"], _bdot(dob, v, 2, 2), 0.0).astype(BF16)
            dv = _bdot(p.astype(BF16), dob, 1, 1) + _bdot(eb16, dstb, 2, 2)
            da = _bdot(dp, bmb, 2, 1) + _bdot(dob, st.astype(BF16), 2, 1)
            dbm = _bdot(dp, ab, 1, 1)
            de = _bdot(v, dstb, 2, 1)
            dd = jnp.sum(dst * st, axis=1, keepdims=True)
            dk = dbm * t["einv"] + de * t["eend"]
            dee = de * e
            db = da * a - dbm * bm - dee
            extra = jnp.sum(dee, axis=1, keepdims=True) + dd * d
            upper = jnp.where(lax.broadcasted_iota(jnp.int32, (n, C, C), 2)
                              >= lax.broadcasted_iota(jnp.int32, (n, C, C), 1), 1.0, 0.0)
            dg = _tri_sum(upper, db) + extra
            df = dg / t["f"] - dk
            sg, sq = t["sg"], t["sq"]
            dq = da * t["eb"] * (HD ** -0.5) * (sq * (1.0 + qin * (1.0 - sq)))
            d_ref[0, :, cs] = dq.reshape(tr, HD).astype(BF16)
            d_ref[1, :, cs] = (df * (1.0 - lbh) * sg * (1.0 - sg)).reshape(tr, HD).astype(BF16)
            d_ref[2, :, cs] = dv.reshape(tr, HD).astype(BF16)
            dlb_ref[:, cs] += jnp.sum((df * (1.0 - sg)).reshape(tr, HD), axis=0, keepdims=True)

    rev = lambda off: pl.BlockSpec((tr, W), lambda g, s: (nrt - 1 - s, off + g))
    return pl.pallas_call(
        body, name="hgrn_bwd", grid=(HH // hb, nrt),
        in_specs=[rev(0), rev(0), rev(0), pl.BlockSpec((hb, n, HD, HD), lambda g, s: (g, nrt - 1 - s, 0, 0)),
                  rev(0), rev(0), rev(2 * EMIX // W), pl.BlockSpec((2, W), lambda g, s: (0, g))],
        out_specs=[pl.BlockSpec((3, tr, W), lambda g, s: (0, nrt - 1 - s, g)), pl.BlockSpec((1, W), lambda g, s: (0, g))],
        out_shape=[jax.ShapeDtypeStruct((3, S, EMIX), BF16), jax.ShapeDtypeStruct((1, EMIX), F32)],
        scratch_shapes=[pltpu.VMEM((hb, HD, HD), F32), pltpu.VMEM((hb, n, HD, HD), F32)],
        compiler_params=_params("parallel", "arbitrary"),
    )(dpremix, premix, rstd, states, proj, fgate, proj, hgrn_lb)


EW_BLOCK_ELEMS = 512 * 1024


def _ew_tiles(R, C):
    tc = C if C <= 4096 else _tile(C, 2048)
    tr = _tile(R, 512)
    while tr * tc > EW_BLOCK_ELEMS and tr % 16 == 0:
        tr //= 2
    return tr, tc


def _add_halves(name, core_chip, grad, got):
    _, _, R, C = grad.shape
    tr, tc = _ew_tiles(R, C)

    def body(c_ref, a_ref, b_ref, o_ref, own_ref):
        r = (a_ref[...].astype(F32) + b_ref[...].astype(F32)).astype(BF16)
        o_ref[...] = r

        @pl.when(pl.program_id(2) == c_ref[1])
        def _():
            own_ref[...] = r

    blk = pl.BlockSpec((None, tr, tc), lambda i, j, s, c: (s, i, j))
    sds = jax.ShapeDtypeStruct(got.shape, BF16)
    return pl.pallas_call(
        body, name=name, out_shape=[sds, sds],
        grid_spec=pltpu.PrefetchScalarGridSpec(
            num_scalar_prefetch=1, grid=(R // tr, C // tc, N_CHIPS),
            in_specs=[pl.BlockSpec((None, None, tr, tc), lambda i, j, s, c: (s, c[0], i, j)), blk],
            out_specs=[blk, pl.BlockSpec((None, tr, tc), lambda i, j, s, c: (c[1], i, j))]),
        compiler_params=_params("parallel", "parallel", "arbitrary"),
    )(core_chip, grad, got)


def _adam_step(w, g, m, v):
    mn = ADAM_B1 * m + (1.0 - ADAM_B1) * g
    vn = ADAM_B2 * v + (1.0 - ADAM_B2) * (g * g)
    m_hat = mn / (1.0 - ADAM_B1 ** ADAM_STEP)
    v_hat = vn / (1.0 - ADAM_B2 ** ADAM_STEP)
    return -ADAM_LR * (m_hat / (jnp.sqrt(v_hat) + ADAM_EPS) + ADAM_WD * w), mn, vn


def _adamw(name, w, g, m, v):
    R, C = w.shape
    tr, tc = _ew_tiles(R, C)

    def body(w_ref, g_ref, m_ref, v_ref, d_ref, mo_ref, vo_ref):
        d_ref[...], mo_ref[...], vo_ref[...] = _adam_step(w_ref[...], g_ref[...], m_ref[...], v_ref[...])

    blk = pl.BlockSpec((tr, tc), lambda i, j: (i, j))
    sds = jax.ShapeDtypeStruct((R, C), F32)
    return pl.pallas_call(
        body, name=name, grid=(R // tr, C // tc), in_specs=[blk] * 4, out_specs=[blk] * 3, out_shape=[sds] * 3,
        compiler_params=_params("parallel", "parallel"),
    )(w, g, m, v)


def _adamw_layers(name, w, gs, m, v):
    L, R, C = w.shape
    tr, tc = _ew_tiles(R, C)

    def body(*refs):
        w_ref, m_ref, v_ref = refs[:3]
        g_refs = refs[3:3 + L]
        go_ref, d_ref, mo_ref, vo_ref = refs[3 + L:]
        layer = pl.program_id(0)
        g = g_refs[0][...]
        for n in range(1, L):
            g = jnp.where(layer == n, g_refs[n][...], g)
        go_ref[...] = g
        d_ref[...], mo_ref[...], vo_ref[...] = _adam_step(w_ref[...], g, m_ref[...], v_ref[...])

    blk = pl.BlockSpec((None, tr, tc), lambda l, i, j: (l, i, j))
    of_layer = lambda n: pl.BlockSpec((tr, tc), lambda l, i, j: (jnp.where(l == n, i, 0), jnp.where(l == n, j, 0)))
    sds = jax.ShapeDtypeStruct((L, R, C), F32)
    return pl.pallas_call(
        body, name=name, grid=(L, R // tr, C // tc), in_specs=[blk] * 3 + [of_layer(n) for n in range(L)],
        out_specs=[blk] * 4, out_shape=[sds] * 4, compiler_params=_params("parallel", "parallel", "parallel"),
    )(w, m, v, *gs)


def _pack_rows(name, vecs, W):
    nv = len(vecs)

    def body(*refs):
        o_ref = refs[nv]
        o_ref[...] = jnp.zeros_like(o_ref)
        for i in range(nv):
            o_ref[i:i + 1, 0:vecs[i].shape[1]] = jnp.sum(refs[i][...], axis=0, keepdims=True)

    vm = pl.BlockSpec(memory_space=pltpu.VMEM)
    return pl.pallas_call(
        body, name=name, in_specs=[vm] * nv, out_specs=vm, out_shape=jax.ShapeDtypeStruct((SMALL_ROWS, W), F32),
    )(*vecs)


def _small_sum(gathered, hgrn_lb, lb_row):
    _, T, W = gathered.shape

    def body(g_ref, lb_ref, o_ref):
        acc = g_ref[0]
        for dev in range(1, N_DEV):
            acc = acc + g_ref[dev]
        o_ref[0:T, :] = acc
        lb = _hgrn_lb(lb_ref)
        d1 = o_ref[lb_row:lb_row + 1, :] * (lb * (1.0 - lb))
        o_ref[T:2 * T, :] = jnp.zeros((T, W), F32)
        o_ref[T:T + 1, :] = -d1
        o_ref[T + 1:T + 2, :] = d1

    vm = pl.BlockSpec(memory_space=pltpu.VMEM)
    return pl.pallas_call(
        body, name="small_sum", in_specs=[vm, vm], out_specs=vm, out_shape=jax.ShapeDtypeStruct((2 * T, W), F32),
    )(gathered, hgrn_lb)


def _place():
    return lax.axis_index("x"), lax.axis_index("y"), lax.axis_index("c")


def _other_chips(x, y):
    return [(1 - x, y), (x, 1 - y), (1 - x, 1 - y)]


def _chunk_rows(rows, row_bytes):
    cr = rows
    while cr * row_bytes > STREAM_CHUNK_BYTES and cr % 32 == 0:
        cr //= 2
    return cr


def _stream(pairs, buf, sems, t, peer):
    lsem, ssem, rsem = sems
    n = len(pairs)
    loads, sent = [None] * n, [None] * n

    def load(k):
        slot = k % STREAM_SLOTS
        if k >= STREAM_SLOTS:
            sent[k - STREAM_SLOTS]()
        loads[k] = pltpu.make_async_copy(pairs[k][0], buf.at[slot], lsem.at[t, slot])
        loads[k].start()

    load(0)
    for k in range(n):
        slot = k % STREAM_SLOTS
        if k + 1 < n:
            load(k + 1)
        loads[k].wait()
        if peer is None:
            cp = pltpu.make_async_copy(buf.at[slot], pairs[k][1], ssem.at[t, slot])
            cp.start()
            sent[k] = cp.wait
        else:
            cp = pltpu.make_async_remote_copy(src_ref=buf.at[slot], dst_ref=pairs[k][1], send_sem=ssem.at[t, slot],
                                              recv_sem=rsem.at[t], device_id=peer, device_id_type=MESH)
            cp.start()
            sent[k] = cp.wait_send
    for k in range(max(0, n - STREAM_SLOTS), n):
        sent[k]()


def _stream_scratch(shapes):
    nt = len(shapes)
    return ([pltpu.VMEM((STREAM_SLOTS,) + s, d) for s, d in shapes]
            + [pltpu.SemaphoreType.DMA((nt, STREAM_SLOTS)), pltpu.SemaphoreType.DMA((nt, STREAM_SLOTS)),
               pltpu.SemaphoreType.DMA((nt,))])


def _exchange_halves(name, grads):
    nt = len(grads)
    hs = [g.shape[1] // 2 for g in grads]
    crs = [_chunk_rows(h, g.shape[2] * g.dtype.itemsize) for h, g in zip(hs, grads)]

    def body(*refs):
        ins, gots, bufs, sems = refs[:nt], refs[nt:2 * nt], refs[2 * nt:3 * nt], refs[3 * nt:]
        x, y, c = _place()
        sib = (x, y, 1 - c)
        for t in range(nt):
            h, cr = hs[t], crs[t]
            pairs = [(ins[t].at[b, pl.ds((1 - c) * h + r0, cr)], gots[t].at[b, pl.ds(r0, cr)])
                     for b in range(N_CHIPS) for r0 in range(0, h, cr)]
            _stream(pairs, bufs[t], sems, t, sib)
        for t in range(nt):
            pltpu.make_async_remote_copy(src_ref=gots[t], dst_ref=gots[t], send_sem=sems[1].at[t, 0],
                                         recv_sem=sems[2].at[t], device_id=sib, device_id_type=MESH).wait_recv()

    return pl.pallas_call(
        body, name=name, in_specs=[ANY] * nt, out_specs=[ANY] * nt,
        out_shape=[jax.ShapeDtypeStruct((N_CHIPS, h, g.shape[2]), g.dtype) for h, g in zip(hs, grads)],
        scratch_shapes=_stream_scratch([((cr, g.shape[2]), g.dtype) for cr, g in zip(crs, grads)]),
        compiler_params=pltpu.CompilerParams(vmem_limit_bytes=VMEM_LIMIT_BYTES),
    )(*grads)


def _scatter_plan(srcs, dsts):
    x, y, c = _place()
    me = 2 * x + y
    return [(srcs[t].at[2 * px + py], dsts[t].at[me], (px, py, c))
            for t in range(len(srcs)) for px, py in _other_chips(x, y)]


def _slot(dst, chip, r0, rows, cols):
    if len(dst.shape) == 3:
        return dst.at[chip, pl.ds(r0, rows)]
    return dst.at[pl.ds(r0, rows), pl.ds(pl.multiple_of(chip * cols, 128), cols)]


def _gather_plan(srcs, dsts):
    x, y, c = _place()
    me = 2 * x + y
    plan = []
    for t in range(len(srcs)):
        h, cols = srcs[t].shape[0] // 2, srcs[t].shape[1]
        plan += [(srcs[t].at[pl.ds(c * h, h)], _slot(dsts[t], me, c * h, h, cols), (px, py, c))
                 for px, py in _other_chips(x, y)]
    return plan


HBM_SPEC = pl.BlockSpec(memory_space=pltpu.HBM)
SEM_SPEC = pl.BlockSpec(memory_space=pltpu.SEMAPHORE)


def _split_start(name, srcs, dsts, plan, ncopies, after):
    bufs = [pltpu.with_memory_space_constraint(a, pltpu.HBM) for a in list(srcs) + list(dsts)]
    nb, ns = len(bufs), len(srcs)
    operands = bufs + ([after] if after is not None else [])

    def body(*refs):
        outs = refs[len(operands):]
        send, recv, token = outs[0], outs[1], outs[-1]
        for i, (src, dst, dev) in enumerate(plan(refs[:ns], refs[ns:nb])):
            pltpu.make_async_remote_copy(src_ref=src, dst_ref=dst, send_sem=send.at[i], recv_sem=recv.at[i],
                                         device_id=dev, device_id_type=MESH).start()
        token[...] = jnp.zeros_like(token)

    res = pl.pallas_call(
        body, name=name,
        out_shape=[pltpu.SemaphoreType.DMA((ncopies,)), pltpu.SemaphoreType.DMA((ncopies,))]
        + [pltpu.HBM(a.shape, a.dtype) for a in bufs] + [jax.ShapeDtypeStruct((8, 128), F32)],
        in_specs=[HBM_SPEC] * nb + [ANY] * (len(operands) - nb),
        out_specs=[SEM_SPEC, SEM_SPEC] + [HBM_SPEC] * nb + [pl.BlockSpec(memory_space=pltpu.VMEM)],
        input_output_aliases={i: 2 + i for i in range(nb)},
        compiler_params=pltpu.CompilerParams(has_side_effects=pltpu.SideEffectType.DATAFLOW_SIDE_EFFECTING),
    )(*operands)
    return res[:-1], res[-1]


def _split_wait(name, started, plan, ns, after):
    send, recv, bufs = started[0], started[1], list(started[2:])
    nb = len(bufs)

    def body(*refs):
        send_ref, recv_ref = refs[nb], refs[nb + 1]
        for i, (src, dst, dev) in enumerate(plan(refs[:ns], refs[ns:nb])):
            cp = pltpu.make_async_remote_copy(src_ref=src, dst_ref=dst, send_sem=send_ref.at[i], recv_sem=recv_ref.at[i],
                                              device_id=dev, device_id_type=MESH)
            cp.wait_send()
            cp.wait_recv()

    res = pl.pallas_call(
        body, name=name, out_shape=[pltpu.HBM(a.shape, a.dtype) for a in bufs],
        in_specs=[HBM_SPEC] * nb + [SEM_SPEC, SEM_SPEC, ANY], out_specs=[HBM_SPEC] * nb,
        input_output_aliases={i: i for i in range(nb)},
        compiler_params=pltpu.CompilerParams(has_side_effects=pltpu.SideEffectType.DATAFLOW_SIDE_EFFECTING),
    )(*bufs, send, recv, after)
    return res[:ns], res[ns:]


def _gather_finish(name, shards, stacks):
    nt = len(shards)
    hs = [s.shape[0] // 2 for s in shards]
    crs = [_chunk_rows(h, s.shape[1] * s.dtype.itemsize) for h, s in zip(hs, shards)]

    def body(*refs):
        ins, outs, bufs, sems = refs[:nt], refs[2 * nt:3 * nt], refs[3 * nt:4 * nt], refs[4 * nt:]
        x, y, c = _place()
        me = 2 * x + y
        sib = (x, y, 1 - c)
        for t in range(nt):
            h, cr, cols = hs[t], crs[t], shards[t].shape[1]
            passed = [_slot(outs[t], 2 * px + py, c * h + r0, cr, cols)
                      for px, py in _other_chips(x, y) for r0 in range(0, h, cr)]
            _stream([(r, r) for r in passed], bufs[t], sems, t, sib)
            own = [(ins[t].at[pl.ds(r0, cr)], _slot(outs[t], me, r0, cr, cols)) for r0 in range(0, 2 * h, cr)]
            _stream(own, bufs[t], sems, t, None)
        for t in range(nt):
            if len(stacks[t].shape) == 3:
                three = outs[t].at[pl.ds(0, 3), pl.ds(0, hs[t])]
            else:
                three = outs[t].at[pl.ds(0, hs[t]), pl.ds(0, 3 * shards[t].shape[1])]
            pltpu.make_async_remote_copy(src_ref=three, dst_ref=three, send_sem=sems[1].at[t, 0],
                                         recv_sem=sems[2].at[t], device_id=sib, device_id_type=MESH).wait_recv()

    return pl.pallas_call(
        body, name=name, in_specs=[ANY] * (2 * nt), out_specs=[ANY] * nt,
        out_shape=[jax.ShapeDtypeStruct(s.shape, s.dtype) for s in stacks],
        scratch_shapes=_stream_scratch([((cr, s.shape[1]), s.dtype) for cr, s in zip(crs, shards)]),
        input_output_aliases={nt + t: t for t in range(nt)},
        compiler_params=pltpu.CompilerParams(vmem_limit_bytes=VMEM_LIMIT_BYTES),
    )(*shards, *stacks)


def _sum_share(landed):
    nt = len(landed)
    hs = [a.shape[1] for a in landed]
    cs = [a.shape[2] for a in landed]
    crs = [_chunk_rows(h, 2 * c * 4) for h, c in zip(hs, cs)]
    shapes = sorted(set(zip(crs, cs)))
    which = [shapes.index(s) for s in zip(crs, cs)]

    def body(*refs):
        ins, outs = refs[:nt], refs[nt:2 * nt]
        inbufs, outbufs = refs[2 * nt:2 * nt + len(shapes)], refs[2 * nt + len(shapes):2 * nt + 2 * len(shapes)]
        lsem, ssem, osem, rsem = refs[2 * nt + 2 * len(shapes):]
        x, y, c = _place()
        sib = (x, y, 1 - c)
        for t in range(nt):
            cr, n, ib, ob = crs[t], hs[t] // crs[t], inbufs[which[t]], outbufs[which[t]]
            loads, gone = [None] * n, [None] * n

            def load(k):
                slot = k % SUM_SLOTS
                if k >= SUM_SLOTS:
                    for cp_wait in gone[k - SUM_SLOTS]:
                        cp_wait()
                loads[k] = pltpu.make_async_copy(ins[t].at[:, pl.ds(k * cr, cr)], ib.at[slot], lsem.at[t, slot])
                loads[k].start()

            load(0)
            for k in range(n):
                slot = k % SUM_SLOTS
                if k + 1 < n:
                    load(k + 1)
                loads[k].wait()
                acc = ib[slot, 0].astype(F32)
                for s in range(1, N_CHIPS):
                    acc = acc + ib[slot, s].astype(F32)
                ob[slot] = acc
                rows = outs[t].at[c, pl.ds(k * cr, cr)]
                away = pltpu.make_async_remote_copy(src_ref=ob.at[slot], dst_ref=rows, send_sem=ssem.at[t, slot],
                                                    recv_sem=rsem.at[t], device_id=sib, device_id_type=MESH)
                away.start()
                home = pltpu.make_async_copy(ob.at[slot], rows, osem.at[t, slot])
                home.start()
                gone[k] = (away.wait_send, home.wait)
            for k in range(max(0, n - SUM_SLOTS), n):
                for cp_wait in gone[k]:
                    cp_wait()
        for t in range(nt):
            other = outs[t].at[1 - c]
            pltpu.make_async_remote_copy(src_ref=other, dst_ref=other, send_sem=ssem.at[t, 0], recv_sem=rsem.at[t],
                                         device_id=sib, device_id_type=MESH).wait_recv()

    slot_sems = pltpu.SemaphoreType.DMA((nt, SUM_SLOTS))
    return pl.pallas_call(
        body, name="sum_share", in_specs=[ANY] * nt, out_specs=[ANY] * nt,
        out_shape=[jax.ShapeDtypeStruct((2, h, c), F32) for h, c in zip(hs, cs)],
        scratch_shapes=[pltpu.VMEM((SUM_SLOTS, N_CHIPS, cr, c), BF16) for cr, c in shapes]
        + [pltpu.VMEM((SUM_SLOTS, cr, c), F32) for cr, c in shapes]
        + [slot_sems, slot_sems, slot_sems, pltpu.SemaphoreType.DMA((nt,))],
        compiler_params=pltpu.CompilerParams(vmem_limit_bytes=VMEM_LIMIT_BYTES),
    )(*landed)


def _allgather_small(name, v):
    def body(v_ref, o_ref, send, recv, lsem):
        x, y, c = _place()
        me = 4 * x + 2 * y + c
        loc = pltpu.make_async_copy(v_ref, o_ref.at[me], lsem)
        loc.start()
        copies = []
        for k in range(1, N_DEV):
            px = 1 - x if k & 4 else x
            py = 1 - y if k & 2 else y
            pc = 1 - c if k & 1 else c
            cp = pltpu.make_async_remote_copy(
                src_ref=v_ref, dst_ref=o_ref.at[me], send_sem=send.at[k - 1], recv_sem=recv.at[k - 1],
                device_id=(px, py, pc), device_id_type=MESH)
            cp.start()
            copies.append(cp)
        for cp in copies:
            cp.wait()
        loc.wait()

    vm = pl.BlockSpec(memory_space=pltpu.VMEM)
    return pl.pallas_call(
        body, name=name, in_specs=[vm], out_specs=vm,
        out_shape=jax.ShapeDtypeStruct((N_DEV,) + v.shape, v.dtype),
        scratch_shapes=[pltpu.SemaphoreType.DMA((N_DEV - 1,))] * 2 + [pltpu.SemaphoreType.DMA],
    )(v)


def kernel(x, mem, norm_g, mem_norm_g, w_kv, w_out, pool_w_in, pool_w_grp, pool_scale, hgrn_w_in, hgrn_lb, hgrn_norm_g, final_g, loss_target, m_norm_g, m_mem_norm_g, m_w_kv, m_w_out, m_pool_w_in, m_pool_w_grp, m_pool_scale, m_hgrn_w_in, m_hgrn_lb, m_hgrn_norm_g, m_final_g, v_norm_g, v_mem_norm_g, v_w_kv, v_w_out, v_pool_w_in, v_pool_w_grp, v_pool_scale, v_hgrn_w_in, v_hgrn_lb, v_hgrn_norm_g, v_final_g):
    _, S, D = x.shape
    M = mem.shape[1]
    EB = 2 * D
    ECA = EB // 4
    EMIX = EB - ECA
    PG = EMIX // N_POOL_GROUPS
    NP0 = EMIX + ECA + EB
    NP1 = 3 * EMIX + ECA + EB
    SH0, SH1 = NP0 // N_CHIPS, NP1 // N_CHIPS
    DK, EK = D // N_CHIPS, EB // N_CHIPS
    TNP = 512 if all(v % 512 == 0 for v in (SH0, SH1, ECA, EMIX)) else 256
    TM = _tile(S, 1024)
    TMF = _tile(S, 2048)
    TD = _tile(D, 512)
    TDW = _tile(D, 1024)
    c0, c1 = SH0 // TNP, SH1 // TNP
    qt, et = EMIX // TNP, ECA // TNP
    chip = 2 * lax.axis_index("x") + lax.axis_index("y")

    xs, ms, tgt = x[0], mem[0], loss_target[0]

    flat = lambda w: w.reshape(-1, w.shape[-1])
    sds = jax.ShapeDtypeStruct
    stack_of = lambda s: lax.empty((N_CHIPS,) + s.shape, BF16)
    group_a = [flat(pool_w_in).astype(BF16)]
    gather_a, token = _split_start("gather_a_start", group_a, [stack_of(s) for s in group_a], _gather_plan, 3, None)
    t0 = token[0:1, 0:1]
    bf = lambda w: (w + t0).astype(BF16)
    group_b = [bf(w_kv[0]), bf(w_out[0]), bf(flat(pool_w_grp))]
    gather_b, token = _split_start("gather_b_start", group_b, [stack_of(s) for s in group_b], _gather_plan, 9, token)
    group_c = [bf(flat(hgrn_w_in))]
    gather_c, token = _split_start("gather_c_start", group_c, [lax.empty((D, NP1), BF16)], _gather_plan, 3, token)
    group_d = [bf(w_kv[1]), bf(w_out[1])]
    gather_d, token = _split_start("gather_d_start", group_d, [stack_of(s) for s in group_d], _gather_plan, 6, token)
    started = token[0:1, 0:1]

    tek, tew = _tile(EK, 512), _tile(EK, 1024)

    mem_n = _rms_fwd("rms_mem", ms, mem_norm_g.reshape(1, D) + started)
    h0 = _rms_fwd("rms0", xs, norm_g[0:1] + started)
    wpin, = _gather_finish("gather_a_finish", *_split_wait("gather_a_wait", gather_a, _gather_plan, 1, h0))

    tkw = _tile(2 * ECA, 1024)

    def kv_of(layer, wkv):
        return _matmul(
            f"kv{layer}", mem_n, wkv.reshape(D, 2 * ECA), grid=(1, 2 * ECA // tkw, 1),
            a_spec=pl.BlockSpec((M, D), lambda i, j, k: (0, 0)), b_spec=pl.BlockSpec((D, tkw), lambda i, j, k: (0, j)),
            out_shape=sds((M, 2 * ECA), BF16), out_spec=pl.BlockSpec((M, tkw), lambda i, j, k: (0, j)),
            acc_shape=(M, tkw), dims=NN)

    tko = _tile(EB, 2048)

    def out_proj(layer, branch, wout, resid):
        return _matmul(
            f"out_proj{layer}", branch, wout.reshape(EB, D), grid=(S // TM, D // TDW, EB // tko),
            a_spec=pl.BlockSpec((TM, tko), IK), b_spec=pl.BlockSpec((tko, TDW), KJ),
            out_shape=sds((S, D), F32), out_spec=pl.BlockSpec((TM, TDW), IJ),
            acc_shape=(TM, TDW), dims=NN, add=resid, add_spec=pl.BlockSpec((TM, TDW), IJ))

    ones_ca = jnp.ones((1, ECA), F32)

    proj0 = _matmul(
        "proj0", h0, wpin, grid=(S // TMF, NP0 // TNP, 1),
        a_spec=pl.BlockSpec((TMF, D), lambda i, j, k: (i, 0)),
        b_spec=pl.BlockSpec((None, D, TNP), lambda i, j, k: (j // c0, 0, j % c0)),
        out_shape=sds((S, NP0), BF16), out_spec=pl.BlockSpec((TMF, TNP), IJ),
        acc_shape=(TMF, TNP), dims=NN)
    pooled = _pool_fwd(proj0, S, EMIX)
    wkv0, wout0, g_grp = _gather_finish("gather_b_finish", *_split_wait("gather_b_wait", gather_b, _gather_plan, 3, pooled))
    wgrp = g_grp.reshape(N_CHIPS, N_POOL_GROUPS, PG // N_CHIPS, PG).transpose(1, 0, 2, 3).reshape(N_POOL_GROUPS, PG, PG)
    kv = [kv_of(0, wkv0), None]
    premix0 = _matmul(
        "pool_grp", pooled, wgrp, grid=(S // TM, N_POOL_GROUPS, 1),
        a_spec=pl.BlockSpec((TM, PG), lambda i, j, k: (i, j)),
        b_spec=pl.BlockSpec((None, PG, PG), lambda i, j, k: (j, 0, 0)),
        out_shape=sds((S, EB), BF16), out_spec=pl.BlockSpec((TM, PG), lambda i, j, k: (i, j)),
        acc_shape=(TM, PG), dims=NN)
    premix0 = _ca_fwd("ca_fwd0", proj0, EMIX // ECA, kv[0], premix0, S, ECA, EMIX)
    colscale0 = jnp.concatenate([pool_scale.reshape(1, EMIX), ones_ca], axis=1)
    gblk0 = (EMIX + ECA) // ECA
    branch0 = _gate_fwd("gate_fwd0", premix0, proj0, gblk0, colscale0, S, EB, ECA)
    x1 = out_proj(0, branch0, wout0, xs)

    whin, = _gather_finish("gather_c_finish", *_split_wait("gather_c_wait", gather_c, _gather_plan, 1, x1))
    h1 = _rms_fwd("rms1", x1, norm_g[1:2])

    def proj1_cols(name, ncols, col_of, out_cols, out_dtype, out_col_of):
        return _matmul(
            name, h1, whin, grid=(S // TMF, ncols, 1),
            a_spec=pl.BlockSpec((TMF, D), lambda i, j, k: (i, 0)),
            b_spec=pl.BlockSpec((D, TNP), lambda i, j, k: (0, col_of(j))),
            out_shape=sds((S, out_cols), out_dtype), out_spec=pl.BlockSpec((TMF, TNP), lambda i, j, k: (i, out_col_of(j))),
            acc_shape=(TMF, TNP), dims=NN)

    skip_f = lambda j: jnp.where(j < qt, j, j + qt)
    proj1 = proj1_cols("proj1", NP1 // TNP - qt, skip_f, NP1, BF16, skip_f)
    fgate = proj1_cols("proj1_f", qt, lambda j: j + qt, EMIX, F32, lambda j: j)
    premix1, rstd1, states = _hgrn_fwd(proj1, fgate, hgrn_lb, S, EMIX, EB)
    wkv1, wout1 = _gather_finish("gather_d_finish", *_split_wait("gather_d_wait", gather_d, _gather_plan, 2, rstd1))
    kv[1] = kv_of(1, wkv1)
    premix1 = _ca_fwd("ca_fwd1", proj1, 3 * EMIX // ECA, kv[1], premix1, S, ECA, EMIX)
    norm_tiles = _allgather_small("allgather_norm_g", jnp.pad(hgrn_norm_g, ((0, SMALL_ROWS - 1), (0, 0))))
    hg_norm = norm_tiles[0::2, 0, :].reshape(1, EMIX)
    colscale1 = jnp.concatenate([hg_norm, ones_ca], axis=1)
    gblk1 = (3 * EMIX + ECA) // ECA
    branch1 = _gate_fwd("gate_fwd1", premix1, proj1, gblk1, colscale1, S, EB, ECA)
    x2 = out_proj(1, branch1, wout1, x1)

    dx2, dx2b, d_final_g, loss_part = _loss_head(x2, final_g.reshape(1, D), tgt)

    def out_proj_bwd(layer, dxb, branch, wout, premix, proj, gblk, colscale, dshape, dblk):
        goff, doff = gblk * ECA // tek, dblk * ECA // tek
        dpremix, dgate, dcol = _matmul(
            f"dbranch{layer}", dxb, wout, grid=(S // TMF, EB // tek, 1),
            a_spec=pl.BlockSpec((TMF, D), lambda i, j, k: (i, 0)),
            b_spec=pl.BlockSpec((None, tek, D), lambda i, j, k: (j // (EK // tek), j % (EK // tek), 0)),
            extras=[(premix, pl.BlockSpec((TMF, tek), IJ)), (proj, pl.BlockSpec((TMF, tek), lambda i, j, k: (i, goff + j))),
                    (colscale, pl.BlockSpec((1, tek), lambda i, j, k: (0, j)))],
            epilogue=_gate_bwd_epilogue,
            out_shape=[sds((S, EB), BF16), sds(dshape, BF16), sds((S // TMF, 1, EB), F32)],
            out_spec=[pl.BlockSpec((TMF, tek), IJ), pl.BlockSpec((TMF, tek), lambda i, j, k: (i, doff + j)),
                      pl.BlockSpec((None, 1, tek), lambda i, j, k: (i, 0, j))],
            acc_shape=(TMF, tek), dims=NT)
        dw = _matmul(
            f"dwout{layer}", branch, dxb, grid=(EB // tew, D // TD, 1),
            a_spec=pl.BlockSpec((S, tew), lambda i, j, k: (0, i)), b_spec=pl.BlockSpec((S, TD), lambda i, j, k: (0, j)),
            out_shape=sds((N_CHIPS, EK, D), BF16),
            out_spec=pl.BlockSpec((None, tew, TD), lambda i, j, k: (i // (EK // tew), i % (EK // tew), j)),
            acc_shape=(tew, TD), dims=TN)
        return dpremix, dgate, dcol.reshape(S // TMF, EB), dw

    def kv_bwd(layer, dkv, wkv, dmem_add):
        dkvb = dkv.astype(BF16)
        dmem = _matmul(
            f"dmem{layer}", dkvb, wkv.reshape(D, 2 * ECA), grid=(1, D // TDW, 1),
            a_spec=pl.BlockSpec((M, 2 * ECA), lambda i, j, k: (0, 0)),
            b_spec=pl.BlockSpec((TDW, 2 * ECA), lambda i, j, k: (j, 0)),
            out_shape=sds((M, D), F32), out_spec=pl.BlockSpec((M, TDW), lambda i, j, k: (0, j)), acc_shape=(M, TDW),
            dims=NT, add=dmem_add, add_spec=pl.BlockSpec((M, TDW), lambda i, j, k: (0, j)))
        dw = _matmul(
            f"dwkv{layer}", mem_n, dkvb, grid=(D // TDW, 2 * ECA // tkw, 1),
            a_spec=pl.BlockSpec((M, TDW), lambda i, j, k: (0, i)), b_spec=pl.BlockSpec((M, tkw), lambda i, j, k: (0, j)),
            out_shape=sds((D, 2 * ECA), BF16), out_spec=pl.BlockSpec((TDW, tkw), IJ), acc_shape=(TDW, tkw), dims=TN)
        return dmem, dw.reshape(N_CHIPS, DK, 2 * ECA)

    dpremix1, drest1, dcol1, gw_out1 = out_proj_bwd(1, dx2b, branch1, wout1, premix1, proj1, gblk1, colscale1,
                                                    (S, ECA + EB), 1)
    drest1, dkv1 = _ca_bwd("ca_bwd1", dpremix1, proj1, 3 * EMIX // ECA, kv[1], drest1, 0, S, ECA, EMIX)
    dqfi, dlb = _hgrn_bwd(dpremix1, premix1, rstd1, states, proj1, fgate, hgrn_lb, S, EMIX)
    nq, nr = 3 * qt, (ECA + EB) // TNP
    tkh = _tile(EMIX, 1024) if (ECA + EB) % _tile(EMIX, 1024) == 0 else TNP
    kq = EMIX // tkh
    dh1 = _matmul(
        "dh1_qfi", dqfi, whin, grid=(S // TM, D // TDW, 3),
        a_spec=pl.BlockSpec((None, TM, EMIX), lambda i, j, k: (k, i, 0)),
        b_spec=pl.BlockSpec((TDW, EMIX), lambda i, j, k: (j, k)),
        out_shape=sds((S, D), F32), out_spec=pl.BlockSpec((TM, TDW), IJ), acc_shape=(TM, TDW), dims=NT)
    dh1 = _matmul(
        "dh1_rest", drest1, whin, grid=(S // TM, D // TDW, (ECA + EB) // tkh), a_spec=pl.BlockSpec((TM, tkh), IK),
        b_spec=pl.BlockSpec((TDW, tkh), lambda i, j, k: (j, k + 3 * kq)),
        out_shape=sds((S, D), F32), out_spec=pl.BlockSpec((TM, TDW), IJ), acc_shape=(TM, TDW), dims=NT,
        add=dh1, add_spec=pl.BlockSpec((TM, TDW), IJ))
    gw_hin = _matmul(
        "dwhin_qfi", h1, dqfi, grid=(D // TDW, nq, 1), a_spec=pl.BlockSpec((S, TDW), lambda i, j, k: (0, i)),
        b_spec=pl.BlockSpec((None, S, TNP), lambda i, j, k: (j // qt, 0, j % qt)),
        out_shape=sds((N_CHIPS, D, SH1), BF16), out_spec=pl.BlockSpec((None, TDW, TNP), lambda i, j, k: (j // c1, i, j % c1)),
        acc_shape=(TDW, TNP), dims=TN)
    gw_hin = _matmul(
        "dwhin_rest", h1, drest1, grid=(D // TDW, nr, 1), a_spec=pl.BlockSpec((S, TDW), lambda i, j, k: (0, i)),
        b_spec=pl.BlockSpec((S, TNP), lambda i, j, k: (0, j)), out_shape=sds((N_CHIPS, D, SH1), BF16),
        out_spec=pl.BlockSpec((None, TDW, TNP), lambda i, j, k: ((j + nq) // c1, i, (j + nq) % c1)),
        acc_shape=(TDW, TNP), dims=TN, alias=gw_hin)
    dmem, gw_kv1 = kv_bwd(1, dkv1, wkv1, None)

    core_chip = jnp.stack([lax.axis_index("c"), chip]).astype(jnp.int32)

    def reduce_in_chip(tag, stacks):
        got = _exchange_halves(f"exchange_halves{tag}", stacks)
        pairs = [_add_halves(f"add_halves{tag}_{t}", core_chip, g.reshape(N_CHIPS, 2, g.shape[1] // 2, g.shape[2]), r)
                 for t, (g, r) in enumerate(zip(stacks, got))]
        return [p for p, _ in pairs], [own for _, own in pairs]

    parts1, landed1 = reduce_in_chip(1, [gw_kv1, gw_out1, gw_hin])
    scatter1, token1 = _split_start("scatter1_start", parts1, landed1, _scatter_plan, 3 * len(parts1), None)
    dx1, dx1b, d_ng1 = _rms_bwd("rms_bwd1", dh1, x1, norm_g[1:2] + token1[0:1, 0:1], dx2)

    dpremix0, dproj0, dcol0, gw_out0 = out_proj_bwd(0, dx1b, branch0, wout0, premix0, proj0, gblk0, colscale0,
                                                    (S, NP0), gblk0)
    dproj0, dkv0 = _ca_bwd("ca_bwd0", dpremix0, proj0, EMIX // ECA, kv[0], dproj0, EMIX // ECA, S, ECA, EMIX)
    dmem, gw_kv0 = kv_bwd(0, dkv0, wkv0, dmem)
    parts_a, landed_a = reduce_in_chip("0a", [gw_kv0, gw_out0])
    scatter_a, token_a = _split_start("scatter0a_start", parts_a, landed_a, _scatter_plan, 3 * len(parts_a), None)
    dpooled = _matmul(
        "dpooled", dpremix0, wgrp, grid=(S // TM, N_POOL_GROUPS, 1), a_spec=pl.BlockSpec((TM, PG), IJ),
        b_spec=pl.BlockSpec((None, PG, PG), lambda i, j, k: (j, 0, 0)),
        out_shape=sds((S, EMIX), F32), out_spec=pl.BlockSpec((TM, PG), IJ), acc_shape=(TM, PG), dims=NT, after=token_a)
    dwgrp = _matmul(
        "dwgrp", pooled, dpremix0, grid=(N_POOL_GROUPS, 1, 1), a_spec=pl.BlockSpec((S, PG), lambda i, j, k: (0, i)),
        b_spec=pl.BlockSpec((S, PG), lambda i, j, k: (0, i)), out_shape=sds((N_POOL_GROUPS, PG, PG), F32),
        out_spec=pl.BlockSpec((None, PG, PG), lambda i, j, k: (i, 0, 0)), acc_shape=(PG, PG), dims=TN)
    dproj0 = _pool_bwd(dpooled, dproj0, S, EMIX)
    gw_pin = _matmul(
        "dwpin", h0, dproj0, grid=(D // TDW, NP0 // TNP, 1), a_spec=pl.BlockSpec((S, TDW), lambda i, j, k: (0, i)),
        b_spec=pl.BlockSpec((S, TNP), lambda i, j, k: (0, j)), out_shape=sds((N_CHIPS, D, SH0), BF16),
        out_spec=pl.BlockSpec((None, TDW, TNP), lambda i, j, k: (j // c0, i, j % c0)), acc_shape=(TDW, TNP), dims=TN)
    gw_grp = dwgrp.reshape(N_POOL_GROUPS, N_CHIPS, PG // N_CHIPS, PG).transpose(1, 0, 2, 3).reshape(N_CHIPS, PG, PG)
    parts_b, landed_b = reduce_in_chip("0b", [gw_pin, gw_grp.astype(BF16)])
    scatter_b, token_b = _split_start("scatter0b_start", parts_b, landed_b, _scatter_plan, 3 * len(parts_b), None)
    dh0 = _matmul(
        "dh0", dproj0, wpin, grid=(S // TM, D // TDW, N_CHIPS), a_spec=pl.BlockSpec((TM, SH0), IK),
        b_spec=pl.BlockSpec((None, TDW, SH0), lambda i, j, k: (k, j, 0)),
        out_shape=sds((S, D), F32), out_spec=pl.BlockSpec((TM, TDW), IJ), acc_shape=(TM, TDW), dims=NT, after=token_b)
    grad_x, _, d_ng0 = _rms_bwd("rms_bwd0", dh0, xs, norm_g[0:1], dx1)
    _, _, d_mng = _rms_bwd("rms_bwd_mem", dmem, ms, mem_norm_g.reshape(1, D), jnp.zeros_like(ms))

    _, landed1 = _split_wait("scatter1_wait", scatter1, _scatter_plan, len(parts1), grad_x)
    _, landed_a = _split_wait("scatter0a_wait", scatter_a, _scatter_plan, len(parts_a), grad_x)
    _, landed_b = _split_wait("scatter0b_wait", scatter_b, _scatter_plan, len(parts_b), grad_x)
    landed = [landed_a[0], landed1[0], landed_a[1], landed1[1], landed_b[0], landed_b[1], landed1[2]]
    fulls = _sum_share(landed)
    f2 = [f.reshape(-1, f.shape[-1]) for f in fulls]
    grads, deltas, new_m, new_v = {}, {}, {}, {}
    for n, w, mm, vv, gs in (("w_kv", w_kv, m_w_kv, v_w_kv, f2[0:2]), ("w_out", w_out, m_w_out, v_w_out, f2[2:4]),
                             ("pool_w_in", pool_w_in, m_pool_w_in, v_pool_w_in, f2[4:5]),
                             ("pool_w_grp", pool_w_grp, m_pool_w_grp, v_pool_w_grp, f2[5:6]),
                             ("hgrn_w_in", hgrn_w_in, m_hgrn_w_in, v_hgrn_w_in, f2[6:7])):
        as3d = lambda a: a.reshape((a.shape[0], -1, a.shape[-1]))
        outs = _adamw_layers(f"adamw_{n}", as3d(w), gs, as3d(mm), as3d(vv))
        grads[n], deltas[n], new_m[n], new_v[n] = [o.reshape(w.shape) for o in outs]

    Wd = EMIX
    partial = _pack_rows("pack_partials", [d_ng0, d_ng1, d_mng, dcol0[:, :EMIX], dlb, dcol1[:, :EMIX], d_final_g,
                                           loss_part], Wd)
    summed = _small_sum(_allgather_small("allgather_grads", partial), hgrn_lb, 4)
    row = lambda i, n=Wd: summed[i:i + 1, :n]
    nshard = EMIX // N_CHIPS
    g_hg_norm = lax.dynamic_slice_in_dim(row(5), chip * nshard, nshard, axis=1)
    small_names = ["norm_g0", "norm_g1", "mem_norm_g", "pool_scale", "hgrn_lb0", "hgrn_lb1", "hgrn_norm_g", "final_g"]
    small_w = [norm_g[0:1], norm_g[1:2], mem_norm_g.reshape(1, D), pool_scale, hgrn_lb[0:1], hgrn_lb[1:2], hgrn_norm_g,
               final_g.reshape(1, D)]
    small_m = [m_norm_g[0:1], m_norm_g[1:2], m_mem_norm_g.reshape(1, D), m_pool_scale, m_hgrn_lb[0:1], m_hgrn_lb[1:2],
               m_hgrn_norm_g, m_final_g.reshape(1, D)]
    small_v = [v_norm_g[0:1], v_norm_g[1:2], v_mem_norm_g.reshape(1, D), v_pool_scale, v_hgrn_lb[0:1], v_hgrn_lb[1:2],
               v_hgrn_norm_g, v_final_g.reshape(1, D)]
    g_pack = _pack_rows("pack_small_g", [row(0, D), row(1, D), row(2, D), row(3), row(8), row(9), g_hg_norm, row(6, D)], Wd)
    d_pack, m_pack, v_pack = _adamw("adamw_small", _pack_rows("pack_small_w", small_w, Wd), g_pack,
                                    _pack_rows("pack_small_m", small_m, Wd), _pack_rows("pack_small_v", small_v, Wd))
    widths = [v.shape[1] for v in small_w]
    rows = lambda p: {n: p[i, :widths[i]] for i, n in enumerate(small_names)}

    def assemble(r, out):
        out["norm_g"] = jnp.stack([r["norm_g0"], r["norm_g1"]])
        out["mem_norm_g"] = r["mem_norm_g"]
        out["pool_scale"] = r["pool_scale"].reshape(1, EMIX)
        out["hgrn_lb"] = jnp.stack([r["hgrn_lb0"], r["hgrn_lb1"]])
        out["hgrn_norm_g"] = r["hgrn_norm_g"].reshape(1, nshard)
        out["final_g"] = r["final_g"]

    assemble(rows(g_pack), grads)
    assemble(rows(d_pack), deltas)
    assemble(rows(m_pack), new_m)
    assemble(rows(v_pack), new_v)
    loss = summed[7, 0]

    order = ["norm_g", "mem_norm_g", "w_kv", "w_out", "pool_w_in", "pool_w_grp", "pool_scale", "hgrn_w_in", "hgrn_lb",
             "hgrn_norm_g", "final_g"]
    return (loss, grad_x.reshape(1, S, D), *[grads[n] for n in order], *[deltas[n] for n in order],
            *[new_m[n] for n in order], *[new_v[n] for n in order])
```

```python
import functools

import jax
import jax.numpy as jnp
from jax import lax
from jax.experimental import pallas as pl
from jax.experimental.pallas import tpu as pltpu

F32 = jnp.float32
BF16 = jnp.bfloat16
MESH = pl.DeviceIdType.MESH
ANY = pl.BlockSpec(memory_space=pl.ANY)

EPS = 1e-6
HG_HEAD_DIM = 128
HG_CHUNK = 64
CA_HEADS = 4
N_POOL_GROUPS = 4
POOL_HALO = 128
ADAM_LR = 0.001
ADAM_B1 = 0.9
ADAM_B2 = 0.999
ADAM_EPS = 1e-08
ADAM_WD = 0.01
ADAM_STEP = 10
N_CHIPS = 4
N_DEV = 8
VMEM_LIMIT_BYTES = 56 * 1024 * 1024
SMALL_ROWS = 8
STREAM_CHUNK_BYTES = 2 * 1024 * 1024
STREAM_SLOTS = 3
SUM_SLOTS = 2


def _params(*sem):
    return pltpu.CompilerParams(dimension_semantics=sem, vmem_limit_bytes=VMEM_LIMIT_BYTES)


def _tile(n, pref):
    t = pref
    while n % t:
        t //= 2
    return t


def _sigmoid(x):
    return 1.0 / (1.0 + jnp.exp(-x))


def _matmul(name, a, b, *, grid, a_spec, b_spec, out_shape, out_spec, acc_shape, dims,
            add=None, add_spec=None, alias=None, after=None, extras=(), epilogue=None):
    nk = grid[2]
    has_add = add is not None
    has_alias = alias is not None
    has_after = after is not None
    n_out = len(out_shape) if epilogue is not None else 1

    def body(*refs):
        a_ref, b_ref = refs[0], refs[1]
        pos = 2
        add_ref = None
        if has_add:
            add_ref = refs[pos]
            pos += 1
        extra_refs = refs[pos:pos + len(extras)]
        pos += len(extras) + has_alias + has_after
        o_refs = refs[pos:pos + n_out]
        prod = lax.dot_general(a_ref[...], b_ref[...], (dims, ((), ())), preferred_element_type=F32)

        def finish(r):
            if epilogue is not None:
                epilogue(r, extra_refs, o_refs)
                return
            if has_add:
                r = r + add_ref[...].astype(F32)
            o_refs[0][...] = r.astype(o_refs[0].dtype)

        if nk == 1:
            finish(prod)
            return
        acc_ref = refs[pos + n_out]
        k = pl.program_id(2)

        @pl.when(k == 0)
        def _():
            acc_ref[...] = prod

        @pl.when(k > 0)
        def _():
            acc_ref[...] += prod

        @pl.when(k == nk - 1)
        def _():
            finish(acc_ref[...])

    operands = [a, b]
    in_specs = [a_spec, b_spec]
    if has_add:
        operands.append(add)
        in_specs.append(add_spec)
    for arr, spec in extras:
        operands.append(arr)
        in_specs.append(spec)
    aliases = {}
    if has_alias:
        aliases = {len(operands): 0}
        operands.append(alias)
        in_specs.append(ANY)
    if has_after:
        operands.append(after)
        in_specs.append(ANY)
    return pl.pallas_call(
        body, name=name, grid=grid, in_specs=in_specs, out_specs=out_spec, out_shape=out_shape,
        scratch_shapes=[pltpu.VMEM(acc_shape, F32)] if nk > 1 else [], input_output_aliases=aliases,
        compiler_params=_params("parallel", "parallel", "arbitrary"),
    )(*operands)


IJ = lambda i, j, k: (i, j)
IK = lambda i, j, k: (i, k)
KJ = lambda i, j, k: (k, j)
KI = lambda i, j, k: (k, i)
NN = ((1,), (0,))
NT = ((1,), (1,))
TN = ((0,), (0,))


def _rms_fwd(name, x, g):
    R, D = x.shape
    tr = _tile(R, 256)

    def body(x_ref, g_ref, o_ref):
        xf = x_ref[...]
        r = lax.rsqrt(jnp.mean(xf * xf, axis=-1, keepdims=True) + EPS)
        o_ref[...] = (xf * r * g_ref[...]).astype(o_ref.dtype)

    return pl.pallas_call(
        body, name=name, grid=(R // tr,),
        in_specs=[pl.BlockSpec((tr, D), lambda i: (i, 0)), pl.BlockSpec((1, D), lambda i: (0, 0))],
        out_specs=pl.BlockSpec((tr, D), lambda i: (i, 0)),
        out_shape=jax.ShapeDtypeStruct((R, D), BF16), compiler_params=_params("parallel"),
    )(x, g)


def _rms_bwd(name, dh, x, g, dres):
    R, D = x.shape
    tr = _tile(R, 256)

    def body(dh_ref, x_ref, g_ref, dres_ref, dx_ref, dxb_ref, dg_ref):
        xf = x_ref[...]
        r = lax.rsqrt(jnp.mean(xf * xf, axis=-1, keepdims=True) + EPS)
        xn = xf * r
        d = dh_ref[...]
        dyg = d * g_ref[...]
        dx = r * (dyg - xn * jnp.mean(dyg * xn, axis=-1, keepdims=True)) + dres_ref[...]
        dx_ref[...] = dx
        dxb_ref[...] = dx.astype(BF16)

        @pl.when(pl.program_id(0) == 0)
        def _():
            dg_ref[...] = jnp.zeros_like(dg_ref)

        dg_ref[...] += jnp.sum(d * xn, axis=0, keepdims=True)

    row = pl.BlockSpec((tr, D), lambda i: (i, 0))
    vec = pl.BlockSpec((1, D), lambda i: (0, 0))
    return pl.pallas_call(
        body, name=name, grid=(R // tr,), in_specs=[row, row, vec, row], out_specs=[row, row, vec],
        out_shape=[jax.ShapeDtypeStruct((R, D), F32), jax.ShapeDtypeStruct((R, D), BF16),
                   jax.ShapeDtypeStruct((1, D), F32)],
        compiler_params=_params("arbitrary"),
    )(dh, x, g, dres)


def _loss_head(x2, g, target):
    R, D = x2.shape
    tr = _tile(R, 256)

    def body(x_ref, g_ref, t_ref, dx_ref, dxb_ref, dg_ref, loss_ref):
        xf = x_ref[...]
        gg = g_ref[...]
        r = lax.rsqrt(jnp.mean(xf * xf, axis=-1, keepdims=True) + EPS)
        xn = xf * r
        e = xn * gg - t_ref[...]
        part = 0.5 * jnp.sum(jnp.mean(e * e, axis=-1, keepdims=True), axis=0, keepdims=True)
        dy = e * (1.0 / D)
        dyg = dy * gg
        dx = r * (dyg - xn * jnp.mean(dyg * xn, axis=-1, keepdims=True))
        dx_ref[...] = dx
        dxb_ref[...] = dx.astype(BF16)

        @pl.when(pl.program_id(0) == 0)
        def _():
            dg_ref[...] = jnp.zeros_like(dg_ref)
            loss_ref[...] = jnp.zeros_like(loss_ref)

        dg_ref[...] += jnp.sum(dy * xn, axis=0, keepdims=True)
        loss_ref[...] += jnp.broadcast_to(part, loss_ref.shape)

    row = pl.BlockSpec((tr, D), lambda i: (i, 0))
    vec = pl.BlockSpec((1, D), lambda i: (0, 0))
    return pl.pallas_call(
        body, name="loss_head", grid=(R // tr,), in_specs=[row, vec, row],
        out_specs=[row, row, vec, pl.BlockSpec((1, 128), lambda i: (0, 0))],
        out_shape=[jax.ShapeDtypeStruct((R, D), F32), jax.ShapeDtypeStruct((R, D), BF16),
                   jax.ShapeDtypeStruct((1, D), F32), jax.ShapeDtypeStruct((1, 128), F32)],
        compiler_params=_params("arbitrary"),
    )(x2, g, target)


def _pool_band(tr, reverse, w):
    r = lax.broadcasted_iota(jnp.int32, (tr, tr + POOL_HALO), 0)
    c = lax.broadcasted_iota(jnp.int32, (tr, tr + POOL_HALO), 1)
    if reverse:
        inside = (c >= r) & (c < r + w)
    else:
        cc = c - POOL_HALO
        inside = (cc <= r) & (cc > r - w)
    return jnp.where(inside, 1.0, 0.0).astype(BF16)


def _pool_fwd(proj, S, EMIX):
    PG = EMIX // N_POOL_GROUPS
    cb = PG
    tr = _tile(S, 512)
    per_group = PG // cb

    def body(u_ref, o_ref, ext):
        i = pl.program_id(1)
        w = jnp.left_shift(2, pl.program_id(0) // per_group)

        @pl.when(i == 0)
        def _():
            ext[0:POOL_HALO, :] = jnp.zeros((POOL_HALO, cb), BF16)

        u = u_ref[...]
        ext[POOL_HALO:, :] = u
        win = jnp.dot(_pool_band(tr, False, w), ext[...], preferred_element_type=F32)
        pos = i * tr + lax.broadcasted_iota(jnp.int32, (tr, 1), 0)
        cnt = jnp.minimum(pos + 1, w).astype(F32)
        o_ref[...] = (win / cnt - u.astype(F32)).astype(BF16)
        ext[0:POOL_HALO, :] = u[tr - POOL_HALO:, :]

    return pl.pallas_call(
        body, name="pool_fwd", grid=(EMIX // cb, S // tr),
        in_specs=[pl.BlockSpec((tr, cb), lambda j, i: (i, j))],
        out_specs=pl.BlockSpec((tr, cb), lambda j, i: (i, j)),
        out_shape=jax.ShapeDtypeStruct((S, EMIX), BF16),
        scratch_shapes=[pltpu.VMEM((tr + POOL_HALO, cb), BF16)],
        compiler_params=_params("parallel", "arbitrary"),
    )(proj)


def _pool_bwd(dpooled, dproj, S, EMIX):
    PG = EMIX // N_POOL_GROUPS
    cb = PG
    tr = _tile(S, 512)
    per_group = PG // cb
    nrt = S // tr

    def body(d_ref, _, o_ref, ext):
        step = pl.program_id(1)
        i = nrt - 1 - step
        w = jnp.left_shift(2, pl.program_id(0) // per_group)

        @pl.when(step == 0)
        def _():
            ext[tr:, :] = jnp.zeros((POOL_HALO, cb), BF16)

        d = d_ref[...]
        pos = i * tr + lax.broadcasted_iota(jnp.int32, (tr, 1), 0)
        cnt = jnp.minimum(pos + 1, w).astype(F32)
        z = (d / cnt).astype(BF16)
        ext[0:tr, :] = z
        win = jnp.dot(_pool_band(tr, True, w), ext[...], preferred_element_type=F32)
        o_ref[...] = (win - d).astype(BF16)
        ext[tr:, :] = z[0:POOL_HALO, :]

    return pl.pallas_call(
        body, name="pool_bwd", grid=(EMIX // cb, nrt),
        in_specs=[pl.BlockSpec((tr, cb), lambda j, s: (nrt - 1 - s, j)), ANY],
        out_specs=pl.BlockSpec((tr, cb), lambda j, s: (nrt - 1 - s, j)),
        out_shape=jax.ShapeDtypeStruct(dproj.shape, dproj.dtype),
        scratch_shapes=[pltpu.VMEM((tr + POOL_HALO, cb), BF16)],
        input_output_aliases={1: 0},
        compiler_params=_params("parallel", "arbitrary"),
    )(dpooled, dproj)


def _ca_fwd(name, proj, qblk, kv, premix, S, ECA, EMIX):
    M = kv.shape[0]
    hd = ECA // CA_HEADS
    ts = _tile(S, 512)
    scale = hd ** -0.5

    def body(q_ref, kv_ref, _, o_ref):
        for h in range(CA_HEADS):
            q = q_ref[:, h * hd:(h + 1) * hd]
            k = kv_ref[:, h * hd:(h + 1) * hd]
            v = kv_ref[:, ECA + h * hd:ECA + (h + 1) * hd]
            s = lax.dot_general(q, k, (NT, ((), ())), preferred_element_type=F32) * scale
            s = s - jnp.max(s, axis=-1, keepdims=True)
            p = jnp.exp(s)
            p = p / jnp.sum(p, axis=-1, keepdims=True)
            o = jnp.dot(p.astype(BF16), v, preferred_element_type=F32)
            o_ref[:, h * hd:(h + 1) * hd] = o.astype(BF16)

    return pl.pallas_call(
        body, name=name, grid=(S // ts,),
        in_specs=[pl.BlockSpec((ts, ECA), lambda i: (i, qblk)), pl.BlockSpec((M, 2 * ECA), lambda i: (0, 0)), ANY],
        out_specs=pl.BlockSpec((ts, ECA), lambda i: (i, EMIX // ECA)),
        out_shape=jax.ShapeDtypeStruct(premix.shape, premix.dtype),
        input_output_aliases={2: 0}, compiler_params=_params("parallel"),
    )(proj, kv, premix)


def _ca_bwd(name, dpremix, proj, qblk, kv, dbuf, dblk, S, ECA, EMIX):
    M = kv.shape[0]
    hd = ECA // CA_HEADS
    ts = _tile(S, 512)
    scale = hd ** -0.5

    def body(do_ref, q_ref, kv_ref, _, dq_ref, dkv_ref):
        @pl.when(pl.program_id(0) == 0)
        def _():
            dkv_ref[...] = jnp.zeros_like(dkv_ref)

        for h in range(CA_HEADS):
            lo, hi = h * hd, (h + 1) * hd
            q = q_ref[:, lo:hi]
            k = kv_ref[:, lo:hi]
            v = kv_ref[:, ECA + lo:ECA + hi]
            do = do_ref[:, lo:hi]
            s = lax.dot_general(q, k, (NT, ((), ())), preferred_element_type=F32) * scale
            s = s - jnp.max(s, axis=-1, keepdims=True)
            p = jnp.exp(s)
            p = p / jnp.sum(p, axis=-1, keepdims=True)
            pb = p.astype(BF16)
            dkv_ref[:, ECA + lo:ECA + hi] += lax.dot_general(pb, do, (TN, ((), ())), preferred_element_type=F32)
            dp = lax.dot_general(do, v, (NT, ((), ())), preferred_element_type=F32)
            ds = (p * (dp - jnp.sum(p * dp, axis=-1, keepdims=True)) * scale).astype(BF16)
            dq_ref[:, lo:hi] = jnp.dot(ds, k, preferred_element_type=F32).astype(BF16)
            dkv_ref[:, lo:hi] += lax.dot_general(ds, q, (TN, ((), ())), preferred_element_type=F32)

    return pl.pallas_call(
        body, name=name, grid=(S // ts,),
        in_specs=[pl.BlockSpec((ts, ECA), lambda i: (i, EMIX // ECA)), pl.BlockSpec((ts, ECA), lambda i: (i, qblk)),
                  pl.BlockSpec((M, 2 * ECA), lambda i: (0, 0)), ANY],
        out_specs=[pl.BlockSpec((ts, ECA), lambda i: (i, dblk)), pl.BlockSpec((M, 2 * ECA), lambda i: (0, 0))],
        out_shape=[jax.ShapeDtypeStruct(dbuf.shape, dbuf.dtype), jax.ShapeDtypeStruct((M, 2 * ECA), F32)],
        input_output_aliases={3: 0}, compiler_params=_params("arbitrary"),
    )(dpremix, proj, kv, dbuf)


def _gate_fwd(name, premix, proj, gblk, colscale, S, EB, ECA):
    ts = _tile(S, 512)

    def body(p_ref, g_ref, c_ref, o_ref):
        g = g_ref[...].astype(F32)
        o_ref[...] = (p_ref[...].astype(F32) * c_ref[...] * (g * _sigmoid(g))).astype(BF16)

    return pl.pallas_call(
        body, name=name, grid=(S // ts, EB // ECA),
        in_specs=[pl.BlockSpec((ts, ECA), lambda i, j: (i, j)), pl.BlockSpec((ts, ECA), lambda i, j: (i, gblk + j)),
                  pl.BlockSpec((1, ECA), lambda i, j: (0, j))],
        out_specs=pl.BlockSpec((ts, ECA), lambda i, j: (i, j)),
        out_shape=jax.ShapeDtypeStruct((S, EB), BF16), compiler_params=_params("parallel", "parallel"),
    )(premix, proj, colscale)


def _gate_bwd_epilogue(db, extra_refs, out_refs):
    p_ref, g_ref, c_ref = extra_refs
    dp_ref, dg_ref, dc_ref = out_refs
    g = g_ref[...].astype(F32)
    sg = _sigmoid(g)
    si = g * sg
    c = c_ref[...]
    t = db * p_ref[...].astype(F32)
    dp_ref[...] = (db * si * c).astype(BF16)
    dg_ref[...] = (t * c * (sg * (1.0 + g * (1.0 - sg)))).astype(BF16)
    dc_ref[...] = jnp.sum(t * si, axis=0, keepdims=True)


def _hgrn_lb(lb_ref):
    l0 = lb_ref[0:1, :]
    l1 = lb_ref[1:2, :]
    mx = jnp.maximum(l0, l1)
    e0 = jnp.exp(l0 - mx)
    e1 = jnp.exp(l1 - mx)
    return e1 / (e0 + e1)


def _bdot(a, b, ca, cb):
    return lax.dot_general(a, b, (((ca,), (cb,)), ((0,), (0,))), preferred_element_type=F32)


def _tri_sum(tri, x):
    hi = x.astype(BF16)
    lo = (x - hi.astype(F32)).astype(BF16)
    tri = tri.astype(BF16)
    return _bdot(tri, hi, 2, 1) + _bdot(tri, lo, 2, 1)


def _hgrn_chunks(qin, fin, lbh, n):
    C = HG_CHUNK
    row = lax.broadcasted_iota(jnp.int32, (n, C, C), 1)
    col = lax.broadcasted_iota(jnp.int32, (n, C, C), 2)
    causal = row >= col
    sg = _sigmoid(fin)
    f = lbh + (1.0 - lbh) * sg
    k = 1.0 - f
    g = jnp.log(f)
    b = _tri_sum(jnp.where(causal, 1.0, 0.0), g)
    b_last = jnp.sum(g, axis=1, keepdims=True)
    eb = jnp.exp(b)
    einv = jnp.exp(-b)
    eend = jnp.exp(b_last - b)
    sq = _sigmoid(qin)
    a = qin * sq * (HG_HEAD_DIM ** -0.5) * eb
    bm = k * einv
    e = k * eend
    d = jnp.exp(b_last)
    p = jnp.where(causal, _bdot(a.astype(BF16), bm.astype(BF16), 2, 2), 0.0)
    return dict(causal=causal, sg=sg, f=f, eb=eb, einv=einv, eend=eend, sq=sq, a=a, bm=bm, e=e, d=d, p=p)


def _hgrn_fwd(proj, fgate, hgrn_lb, S, EMIX, EB):
    HD, C = HG_HEAD_DIM, HG_CHUNK
    HH = EMIX // HD
    hb = 2 if HH % 2 == 0 else 1
    W = hb * HD
    tr = _tile(S, 512)
    n = tr // C

    def body(q_ref, f_ref, i_ref, lb_ref, o_ref, rstd_ref, st_ref, state):
        @pl.when(pl.program_id(1) == 0)
        def _():
            state[...] = jnp.zeros_like(state)

        lb = _hgrn_lb(lb_ref)
        for h in range(hb):
            cs = slice(h * HD, (h + 1) * HD)
            qin = q_ref[:, cs].astype(F32).reshape(n, C, HD)
            fin = f_ref[:, cs].reshape(n, C, HD)
            v = i_ref[:, cs].reshape(n, C, HD)
            t = _hgrn_chunks(qin, fin, lb[:, cs], n)
            upd = _bdot(v, t["e"].astype(BF16), 1, 1)
            st = state[h]
            for c in range(n):
                st_ref[h, c] = st
                st = st * t["d"][c] + upd[c]
            state[h] = st
            o = _bdot(t["p"].astype(BF16), v, 2, 1) + _bdot(t["a"].astype(BF16), st_ref[h].astype(BF16), 2, 2)
            rstd = lax.rsqrt(jnp.mean(o * o, axis=-1, keepdims=True) + EPS)
            o_ref[:, cs] = (o * rstd).reshape(tr, HD).astype(BF16)
            rstd_ref[:, cs] = jnp.broadcast_to(rstd, (n, C, HD)).reshape(tr, HD)

    blk = lambda off: pl.BlockSpec((tr, W), lambda g, i: (i, off + g))
    return pl.pallas_call(
        body, name="hgrn_fwd", grid=(HH // hb, S // tr),
        in_specs=[blk(0), blk(0), blk(2 * EMIX // W), pl.BlockSpec((2, W), lambda g, i: (0, g))],
        out_specs=[blk(0), blk(0), pl.BlockSpec((hb, n, HD, HD), lambda g, i: (g, i, 0, 0))],
        out_shape=[jax.ShapeDtypeStruct((S, EB), BF16), jax.ShapeDtypeStruct((S, EMIX), F32),
                   jax.ShapeDtypeStruct((HH, S // C, HD, HD), F32)],
        scratch_shapes=[pltpu.VMEM((hb, HD, HD), F32)],
        compiler_params=_params("parallel", "arbitrary"),
    )(proj, fgate, proj, hgrn_lb)


def _hgrn_bwd(dpremix, premix, rstd, states, proj, fgate, hgrn_lb, S, EMIX):
    HD, C = HG_HEAD_DIM, HG_CHUNK
    HH = EMIX // HD
    hb = 2 if HH % 2 == 0 else 1
    W = hb * HD
    tr = _tile(S, 512)
    n = tr // C
    nrt = S // tr

    def body(do_ref, on_ref, rstd_ref, st_ref, q_ref, f_ref, i_ref, lb_ref, d_ref, dlb_ref, dstate, dsbuf):
        @pl.when(pl.program_id(1) == 0)
        def _():
            dstate[...] = jnp.zeros_like(dstate)
            dlb_ref[...] = jnp.zeros_like(dlb_ref)

        lb = _hgrn_lb(lb_ref)
        for h in range(hb):
            cs = slice(h * HD, (h + 1) * HD)
            qin = q_ref[:, cs].astype(F32).reshape(n, C, HD)
            fin = f_ref[:, cs].reshape(n, C, HD)
            v = i_ref[:, cs].reshape(n, C, HD)
            lbh = lb[:, cs]
            t = _hgrn_chunks(qin, fin, lbh, n)
            a, bm, e, d, p = t["a"], t["bm"], t["e"], t["d"], t["p"]
            ab, bmb, eb16 = a.astype(BF16), bm.astype(BF16), e.astype(BF16)
            on = on_ref[:, cs].astype(F32).reshape(n, C, HD)
            dn = do_ref[:, cs].astype(F32).reshape(n, C, HD)
            do = rstd_ref[:, cs].reshape(n, C, HD) * (dn - on * jnp.mean(dn * on, axis=-1, keepdims=True))
            dob = do.astype(BF16)
            grow = _bdot(dob, ab, 1, 1)
            ds = dstate[h]
            for c in reversed(range(n)):
                dsbuf[h, c] = ds
                ds = ds * d[c] + grow[c]
            dstate[h] = ds
            dst = dsbuf[h]
            st = st_ref[h]
            dstb = dst.astype(BF16)
            dp = jnp.where(t["causal"], _bdot(dob, v, 2, 2), 0.0).astype(BF16)
            dv = _bdot(p.astype(BF16), dob, 1, 1) + _bdot(eb16, dstb, 2, 2)
            da = _bdot(dp, bmb, 2, 1) + _bdot(dob, st.astype(BF16), 2, 1)
            dbm = _bdot(dp, ab, 1, 1)
            de = _bdot(v, dstb, 2, 1)
            dd = jnp.sum(dst * st, axis=1, keepdims=True)
            dk = dbm * t["einv"] + de * t["eend"]
            dee = de * e
            db = da * a - dbm * bm - dee
            extra = jnp.sum(dee, axis=1, keepdims=True) + dd * d
            upper = jnp.where(lax.broadcasted_iota(jnp.int32, (n, C, C), 2)
                              >= lax.broadcasted_iota(jnp.int32, (n, C, C), 1), 1.0, 0.0)
            dg = _tri_sum(upper, db) + extra
            df = dg / t["f"] - dk
            sg, sq = t["sg"], t["sq"]
            dq = da * t["eb"] * (HD ** -0.5) * (sq * (1.0 + qin * (1.0 - sq)))
            d_ref[0, :, cs] = dq.reshape(tr, HD).astype(BF16)
            d_ref[1, :, cs] = (df * (1.0 - lbh) * sg * (1.0 - sg)).reshape(tr, HD).astype(BF16)
            d_ref[2, :, cs] = dv.reshape(tr, HD).astype(BF16)
            dlb_ref[:, cs] += jnp.sum((df * (1.0 - sg)).reshape(tr, HD), axis=0, keepdims=True)

    rev = lambda off: pl.BlockSpec((tr, W), lambda g, s: (nrt - 1 - s, off + g))
    return pl.pallas_call(
        body, name="hgrn_bwd", grid=(HH // hb, nrt),
        in_specs=[rev(0), rev(0), rev(0), pl.BlockSpec((hb, n, HD, HD), lambda g, s: (g, nrt - 1 - s, 0, 0)),
                  rev(0), rev(0), rev(2 * EMIX // W), pl.BlockSpec((2, W), lambda g, s: (0, g))],
        out_specs=[pl.BlockSpec((3, tr, W), lambda g, s: (0, nrt - 1 - s, g)), pl.BlockSpec((1, W), lambda g, s: (0, g))],
        out_shape=[jax.ShapeDtypeStruct((3, S, EMIX), BF16), jax.ShapeDtypeStruct((1, EMIX), F32)],
        scratch_shapes=[pltpu.VMEM((hb, HD, HD), F32), pltpu.VMEM((hb, n, HD, HD), F32)],
        compiler_params=_params("parallel", "arbitrary"),
    )(dpremix, premix, rstd, states, proj, fgate, proj, hgrn_lb)


EW_BLOCK_ELEMS = 512 * 1024


def _ew_tiles(R, C):
    tc = C if C <= 4096 else _tile(C, 2048)
    tr = _tile(R, 512)
    while tr * tc > EW_BLOCK_ELEMS and tr % 16 == 0:
        tr //= 2
    return tr, tc


def _add_halves(name, core_chip, grad, got):
    _, _, R, C = grad.shape
    tr, tc = _ew_tiles(R, C)

    def body(c_ref, a_ref, b_ref, o_ref, own_ref):
        r = (a_ref[...].astype(F32) + b_ref[...].astype(F32)).astype(BF16)
        o_ref[...] = r

        @pl.when(pl.program_id(2) == c_ref[1])
        def _():
            own_ref[...] = r

    blk = pl.BlockSpec((None, tr, tc), lambda i, j, s, c: (s, i, j))
    sds = jax.ShapeDtypeStruct(got.shape, BF16)
    return pl.pallas_call(
        body, name=name, out_shape=[sds, sds],
        grid_spec=pltpu.PrefetchScalarGridSpec(
            num_scalar_prefetch=1, grid=(R // tr, C // tc, N_CHIPS),
            in_specs=[pl.BlockSpec((None, None, tr, tc), lambda i, j, s, c: (s, c[0], i, j)), blk],
            out_specs=[blk, pl.BlockSpec((None, tr, tc), lambda i, j, s, c: (c[1], i, j))]),
        compiler_params=_params("parallel", "parallel", "arbitrary"),
    )(core_chip, grad, got)


def _adam_step(w, g, m, v):
    mn = ADAM_B1 * m + (1.0 - ADAM_B1) * g
    vn = ADAM_B2 * v + (1.0 - ADAM_B2) * (g * g)
    m_hat = mn / (1.0 - ADAM_B1 ** ADAM_STEP)
    v_hat = vn / (1.0 - ADAM_B2 ** ADAM_STEP)
    return -ADAM_LR * (m_hat / (jnp.sqrt(v_hat) + ADAM_EPS) + ADAM_WD * w), mn, vn


def _adamw(name, w, g, m, v):
    R, C = w.shape
    tr, tc = _ew_tiles(R, C)

    def body(w_ref, g_ref, m_ref, v_ref, d_ref, mo_ref, vo_ref):
        d_ref[...], mo_ref[...], vo_ref[...] = _adam_step(w_ref[...], g_ref[...], m_ref[...], v_ref[...])

    blk = pl.BlockSpec((tr, tc), lambda i, j: (i, j))
    sds = jax.ShapeDtypeStruct((R, C), F32)
    return pl.pallas_call(
        body, name=name, grid=(R // tr, C // tc), in_specs=[blk] * 4, out_specs=[blk] * 3, out_shape=[sds] * 3,
        compiler_params=_params("parallel", "parallel"),
    )(w, g, m, v)


def _adamw_layers(name, w, gs, m, v):
    L, R, C = w.shape
    tr, tc = _ew_tiles(R, C)

    def body(*refs):
        w_ref, m_ref, v_ref = refs[:3]
        g_refs = refs[3:3 + L]
        go_ref, d_ref, mo_ref, vo_ref = refs[3 + L:]
        layer = pl.program_id(0)
        g = g_refs[0][...]
        for n in range(1, L):
            g = jnp.where(layer == n, g_refs[n][...], g)
        go_ref[...] = g
        d_ref[...], mo_ref[...], vo_ref[...] = _adam_step(w_ref[...], g, m_ref[...], v_ref[...])

    blk = pl.BlockSpec((None, tr, tc), lambda l, i, j: (l, i, j))
    of_layer = lambda n: pl.BlockSpec((tr, tc), lambda l, i, j: (jnp.where(l == n, i, 0), jnp.where(l == n, j, 0)))
    sds = jax.ShapeDtypeStruct((L, R, C), F32)
    return pl.pallas_call(
        body, name=name, grid=(L, R // tr, C // tc), in_specs=[blk] * 3 + [of_layer(n) for n in range(L)],
        out_specs=[blk] * 4, out_shape=[sds] * 4, compiler_params=_params("parallel", "parallel", "parallel"),
    )(w, m, v, *gs)


def _pack_rows(name, vecs, W):
    nv = len(vecs)

    def body(*refs):
        o_ref = refs[nv]
        o_ref[...] = jnp.zeros_like(o_ref)
        for i in range(nv):
            o_ref[i:i + 1, 0:vecs[i].shape[1]] = jnp.sum(refs[i][...], axis=0, keepdims=True)

    vm = pl.BlockSpec(memory_space=pltpu.VMEM)
    return pl.pallas_call(
        body, name=name, in_specs=[vm] * nv, out_specs=vm, out_shape=jax.ShapeDtypeStruct((SMALL_ROWS, W), F32),
    )(*vecs)


def _small_sum(gathered, hgrn_lb, lb_row):
    _, T, W = gathered.shape

    def body(g_ref, lb_ref, o_ref):
        acc = g_ref[0]
        for dev in range(1, N_DEV):
            acc = acc + g_ref[dev]
        o_ref[0:T, :] = acc
        lb = _hgrn_lb(lb_ref)
        d1 = o_ref[lb_row:lb_row + 1, :] * (lb * (1.0 - lb))
        o_ref[T:2 * T, :] = jnp.zeros((T, W), F32)
        o_ref[T:T + 1, :] = -d1
        o_ref[T + 1:T + 2, :] = d1

    vm = pl.BlockSpec(memory_space=pltpu.VMEM)
    return pl.pallas_call(
        body, name="small_sum", in_specs=[vm, vm], out_specs=vm, out_shape=jax.ShapeDtypeStruct((2 * T, W), F32),
    )(gathered, hgrn_lb)


def _place():
    return lax.axis_index("x"), lax.axis_index("y"), lax.axis_index("c")


def _other_chips(x, y):
    return [(1 - x, y), (x, 1 - y), (1 - x, 1 - y)]


def _chunk_rows(rows, row_bytes):
    cr = rows
    while cr * row_bytes > STREAM_CHUNK_BYTES and cr % 32 == 0:
        cr //= 2
    return cr


def _stream(pairs, buf, sems, t, peer):
    lsem, ssem, rsem = sems
    n = len(pairs)
    loads, sent = [None] * n, [None] * n

    def load(k):
        slot = k % STREAM_SLOTS
        if k >= STREAM_SLOTS:
            sent[k - STREAM_SLOTS]()
        loads[k] = pltpu.make_async_copy(pairs[k][0], buf.at[slot], lsem.at[t, slot])
        loads[k].start()

    load(0)
    for k in range(n):
        slot = k % STREAM_SLOTS
        if k + 1 < n:
            load(k + 1)
        loads[k].wait()
        if peer is None:
            cp = pltpu.make_async_copy(buf.at[slot], pairs[k][1], ssem.at[t, slot])
            cp.start()
            sent[k] = cp.wait
        else:
            cp = pltpu.make_async_remote_copy(src_ref=buf.at[slot], dst_ref=pairs[k][1], send_sem=ssem.at[t, slot],
                                              recv_sem=rsem.at[t], device_id=peer, device_id_type=MESH)
            cp.start()
            sent[k] = cp.wait_send
    for k in range(max(0, n - STREAM_SLOTS), n):
        sent[k]()


def _stream_scratch(shapes):
    nt = len(shapes)
    return ([pltpu.VMEM((STREAM_SLOTS,) + s, d) for s, d in shapes]
            + [pltpu.SemaphoreType.DMA((nt, STREAM_SLOTS)), pltpu.SemaphoreType.DMA((nt, STREAM_SLOTS)),
               pltpu.SemaphoreType.DMA((nt,))])


def _exchange_halves(name, grads):
    nt = len(grads)
    hs = [g.shape[1] // 2 for g in grads]
    crs = [_chunk_rows(h, g.shape[2] * g.dtype.itemsize) for h, g in zip(hs, grads)]

    def body(*refs):
        ins, gots, bufs, sems = refs[:nt], refs[nt:2 * nt], refs[2 * nt:3 * nt], refs[3 * nt:]
        x, y, c = _place()
        sib = (x, y, 1 - c)
        for t in range(nt):
            h, cr = hs[t], crs[t]
            pairs = [(ins[t].at[b, pl.ds((1 - c) * h + r0, cr)], gots[t].at[b, pl.ds(r0, cr)])
                     for b in range(N_CHIPS) for r0 in range(0, h, cr)]
            _stream(pairs, bufs[t], sems, t, sib)
        for t in range(nt):
            pltpu.make_async_remote_copy(src_ref=gots[t], dst_ref=gots[t], send_sem=sems[1].at[t, 0],
                                         recv_sem=sems[2].at[t], device_id=sib, device_id_type=MESH).wait_recv()

    return pl.pallas_call(
        body, name=name, in_specs=[ANY] * nt, out_specs=[ANY] * nt,
        out_shape=[jax.ShapeDtypeStruct((N_CHIPS, h, g.shape[2]), g.dtype) for h, g in zip(hs, grads)],
        scratch_shapes=_stream_scratch([((cr, g.shape[2]), g.dtype) for cr, g in zip(crs, grads)]),
        compiler_params=pltpu.CompilerParams(vmem_limit_bytes=VMEM_LIMIT_BYTES),
    )(*grads)


def _scatter_plan(srcs, dsts):
    x, y, c = _place()
    me = 2 * x + y
    return [(srcs[t].at[2 * px + py], dsts[t].at[me], (px, py, c))
            for t in range(len(srcs)) for px, py in _other_chips(x, y)]


def _slot(dst, chip, r0, rows, cols):
    if len(dst.shape) == 3:
        return dst.at[chip, pl.ds(r0, rows)]
    return dst.at[pl.ds(r0, rows), pl.ds(pl.multiple_of(chip * cols, 128), cols)]


def _gather_plan(srcs, dsts):
    x, y, c = _place()
    me = 2 * x + y
    plan = []
    for t in range(len(srcs)):
        h, cols = srcs[t].shape[0] // 2, srcs[t].shape[1]
        plan += [(srcs[t].at[pl.ds(c * h, h)], _slot(dsts[t], me, c * h, h, cols), (px, py, c))
                 for px, py in _other_chips(x, y)]
    return plan


HBM_SPEC = pl.BlockSpec(memory_space=pltpu.HBM)
SEM_SPEC = pl.BlockSpec(memory_space=pltpu.SEMAPHORE)


def _split_start(name, srcs, dsts, plan, ncopies, after):
    bufs = [pltpu.with_memory_space_constraint(a, pltpu.HBM) for a in list(srcs) + list(dsts)]
    nb, ns = len(bufs), len(srcs)
    operands = bufs + ([after] if after is not None else [])

    def body(*refs):
        outs = refs[len(operands):]
        send, recv, token = outs[0], outs[1], outs[-1]
        for i, (src, dst, dev) in enumerate(plan(refs[:ns], refs[ns:nb])):
            pltpu.make_async_remote_copy(src_ref=src, dst_ref=dst, send_sem=send.at[i], recv_sem=recv.at[i],
                                         device_id=dev, device_id_type=MESH).start()
        token[...] = jnp.zeros_like(token)

    res = pl.pallas_call(
        body, name=name,
        out_shape=[pltpu.SemaphoreType.DMA((ncopies,)), pltpu.SemaphoreType.DMA((ncopies,))]
        + [pltpu.HBM(a.shape, a.dtype) for a in bufs] + [jax.ShapeDtypeStruct((8, 128), F32)],
        in_specs=[HBM_SPEC] * nb + [ANY] * (len(operands) - nb),
        out_specs=[SEM_SPEC, SEM_SPEC] + [HBM_SPEC] * nb + [pl.BlockSpec(memory_space=pltpu.VMEM)],
        input_output_aliases={i: 2 + i for i in range(nb)},
        compiler_params=pltpu.CompilerParams(has_side_effects=pltpu.SideEffectType.DATAFLOW_SIDE_EFFECTING),
    )(*operands)
    return res[:-1], res[-1]


def _split_wait(name, started, plan, ns, after):
    send, recv, bufs = started[0], started[1], list(started[2:])
    nb = len(bufs)

    def body(*refs):
        send_ref, recv_ref = refs[nb], refs[nb + 1]
        for i, (src, dst, dev) in enumerate(plan(refs[:ns], refs[ns:nb])):
            cp = pltpu.make_async_remote_copy(src_ref=src, dst_ref=dst, send_sem=send_ref.at[i], recv_sem=recv_ref.at[i],
                                              device_id=dev, device_id_type=MESH)
            cp.wait_send()
            cp.wait_recv()

    res = pl.pallas_call(
        body, name=name, out_shape=[pltpu.HBM(a.shape, a.dtype) for a in bufs],
        in_specs=[HBM_SPEC] * nb + [SEM_SPEC, SEM_SPEC, ANY], out_specs=[HBM_SPEC] * nb,
        input_output_aliases={i: i for i in range(nb)},
        compiler_params=pltpu.CompilerParams(has_side_effects=pltpu.SideEffectType.DATAFLOW_SIDE_EFFECTING),
    )(*bufs, send, recv, after)
    return res[:ns], res[ns:]


def _gather_finish(name, shards, stacks):
    nt = len(shards)
    hs = [s.shape[0] // 2 for s in shards]
    crs = [_chunk_rows(h, s.shape[1] * s.dtype.itemsize) for h, s in zip(hs, shards)]

    def body(*refs):
        ins, outs, bufs, sems = refs[:nt], refs[2 * nt:3 * nt], refs[3 * nt:4 * nt], refs[4 * nt:]
        x, y, c = _place()
        me = 2 * x + y
        sib = (x, y, 1 - c)
        for t in range(nt):
            h, cr, cols = hs[t], crs[t], shards[t].shape[1]
            passed = [_slot(outs[t], 2 * px + py, c * h + r0, cr, cols)
                      for px, py in _other_chips(x, y) for r0 in range(0, h, cr)]
            _stream([(r, r) for r in passed], bufs[t], sems, t, sib)
            own = [(ins[t].at[pl.ds(r0, cr)], _slot(outs[t], me, r0, cr, cols)) for r0 in range(0, 2 * h, cr)]
            _stream(own, bufs[t], sems, t, None)
        for t in range(nt):
            if len(stacks[t].shape) == 3:
                three = outs[t].at[pl.ds(0, 3), pl.ds(0, hs[t])]
            else:
                three = outs[t].at[pl.ds(0, hs[t]), pl.ds(0, 3 * shards[t].shape[1])]
            pltpu.make_async_remote_copy(src_ref=three, dst_ref=three, send_sem=sems[1].at[t, 0],
                                         recv_sem=sems[2].at[t], device_id=sib, device_id_type=MESH).wait_recv()

    return pl.pallas_call(
        body, name=name, in_specs=[ANY] * (2 * nt), out_specs=[ANY] * nt,
        out_shape=[jax.ShapeDtypeStruct(s.shape, s.dtype) for s in stacks],
        scratch_shapes=_stream_scratch([((cr, s.shape[1]), s.dtype) for cr, s in zip(crs, shards)]),
        input_output_aliases={nt + t: t for t in range(nt)},
        compiler_params=pltpu.CompilerParams(vmem_limit_bytes=VMEM_LIMIT_BYTES),
    )(*shards, *stacks)


def _sum_share(landed):
    nt = len(landed)
    hs = [a.shape[1] for a in landed]
    cs = [a.shape[2] for a in landed]
    crs = [_chunk_rows(h, 2 * c * 4) for h, c in zip(hs, cs)]
    shapes = sorted(set(zip(crs, cs)))
    which = [shapes.index(s) for s in zip(crs, cs)]

    def body(*refs):
        ins, outs = refs[:nt], refs[nt:2 * nt]
        inbufs, outbufs = refs[2 * nt:2 * nt + len(shapes)], refs[2 * nt + len(shapes):2 * nt + 2 * len(shapes)]
        lsem, ssem, osem, rsem = refs[2 * nt + 2 * len(shapes):]
        x, y, c = _place()
        sib = (x, y, 1 - c)
        for t in range(nt):
            cr, n, ib, ob = crs[t], hs[t] // crs[t], inbufs[which[t]], outbufs[which[t]]
            loads, gone = [None] * n, [None] * n

            def load(k):
                slot = k % SUM_SLOTS
                if k >= SUM_SLOTS:
                    for cp_wait in gone[k - SUM_SLOTS]:
                        cp_wait()
                loads[k] = pltpu.make_async_copy(ins[t].at[:, pl.ds(k * cr, cr)], ib.at[slot], lsem.at[t, slot])
                loads[k].start()

            load(0)
            for k in range(n):
                slot = k % SUM_SLOTS
                if k + 1 < n:
                    load(k + 1)
                loads[k].wait()
                acc = ib[slot, 0].astype(F32)
                for s in range(1, N_CHIPS):
                    acc = acc + ib[slot, s].astype(F32)
                ob[slot] = acc
                rows = outs[t].at[c, pl.ds(k * cr, cr)]
                away = pltpu.make_async_remote_copy(src_ref=ob.at[slot], dst_ref=rows, send_sem=ssem.at[t, slot],
                                                    recv_sem=rsem.at[t], device_id=sib, device_id_type=MESH)
                away.start()
                home = pltpu.make_async_copy(ob.at[slot], rows, osem.at[t, slot])
                home.start()
                gone[k] = (away.wait_send, home.wait)
            for k in range(max(0, n - SUM_SLOTS), n):
                for cp_wait in gone[k]:
                    cp_wait()
        for t in range(nt):
            other = outs[t].at[1 - c]
            pltpu.make_async_remote_copy(src_ref=other, dst_ref=other, send_sem=ssem.at[t, 0], recv_sem=rsem.at[t],
                                         device_id=sib, device_id_type=MESH).wait_recv()

    slot_sems = pltpu.SemaphoreType.DMA((nt, SUM_SLOTS))
    return pl.pallas_call(
        body, name="sum_share", in_specs=[ANY] * nt, out_specs=[ANY] * nt,
        out_shape=[jax.ShapeDtypeStruct((2, h, c), F32) for h, c in zip(hs, cs)],
        scratch_shapes=[pltpu.VMEM((SUM_SLOTS, N_CHIPS, cr, c), BF16) for cr, c in shapes]
        + [pltpu.VMEM((SUM_SLOTS, cr, c), F32) for cr, c in shapes]
        + [slot_sems, slot_sems, slot_sems, pltpu.SemaphoreType.DMA((nt,))],
        compiler_params=pltpu.CompilerParams(vmem_limit_bytes=VMEM_LIMIT_BYTES),
    )(*landed)


def _allgather_small(name, v):
    def body(v_ref, o_ref, send, recv, lsem):
        x, y, c = _place()
        me = 4 * x + 2 * y + c
        loc = pltpu.make_async_copy(v_ref, o_ref.at[me], lsem)
        loc.start()
        copies = []
        for k in range(1, N_DEV):
            px = 1 - x if k & 4 else x
            py = 1 - y if k & 2 else y
            pc = 1 - c if k & 1 else c
            cp = pltpu.make_async_remote_copy(
                src_ref=v_ref, dst_ref=o_ref.at[me], send_sem=send.at[k - 1], recv_sem=recv.at[k - 1],
                device_id=(px, py, pc), device_id_type=MESH)
            cp.start()
            copies.append(cp)
        for cp in copies:
            cp.wait()
        loc.wait()

    vm = pl.BlockSpec(memory_space=pltpu.VMEM)
    return pl.pallas_call(
        body, name=name, in_specs=[vm], out_specs=vm,
        out_shape=jax.ShapeDtypeStruct((N_DEV,) + v.shape, v.dtype),
        scratch_shapes=[pltpu.SemaphoreType.DMA((N_DEV - 1,))] * 2 + [pltpu.SemaphoreType.DMA],
    )(v)


def kernel(x, mem, norm_g, mem_norm_g, w_kv, w_out, pool_w_in, pool_w_grp, pool_scale, hgrn_w_in, hgrn_lb, hgrn_norm_g, final_g, loss_target, m_norm_g, m_mem_norm_g, m_w_kv, m_w_out, m_pool_w_in, m_pool_w_grp, m_pool_scale, m_hgrn_w_in, m_hgrn_lb, m_hgrn_norm_g, m_final_g, v_norm_g, v_mem_norm_g, v_w_kv, v_w_out, v_pool_w_in, v_pool_w_grp, v_pool_scale, v_hgrn_w_in, v_hgrn_lb, v_hgrn_norm_g, v_final_g):
    _, S, D = x.shape
    M = mem.shape[1]
    EB = 2 * D
    ECA = EB // 4
    EMIX = EB - ECA
    PG = EMIX // N_POOL_GROUPS
    NP0 = EMIX + ECA + EB
    NP1 = 3 * EMIX + ECA + EB
    SH0, SH1 = NP0 // N_CHIPS, NP1 // N_CHIPS
    DK, EK = D // N_CHIPS, EB // N_CHIPS
    TNP = next(t for t in (512, 256, 128) if all(v % t == 0 for v in (SH0, SH1, ECA, EMIX)) and (SH0 // t) % 2 == 0)
    TM = _tile(S, 1024)
    TMF = _tile(S, 2048)
    TD = _tile(D, 512)
    TDW = _tile(D, 1024)
    c0, c1 = SH0 // TNP, SH1 // TNP
    qt, et = EMIX // TNP, ECA // TNP
    chip = 2 * lax.axis_index("x") + lax.axis_index("y")

    xs, ms, tgt = x[0], mem[0], loss_target[0]

    flat = lambda w: w.reshape(-1, w.shape[-1])
    sds = jax.ShapeDtypeStruct
    stack_of = lambda s: lax.empty((N_CHIPS,) + s.shape, BF16)
    pin = flat(pool_w_in).astype(BF16)
    group_a, group_a2 = [pin[:, :SH0 // 2]], [pin[:, SH0 // 2:]]
    gather_a, token = _split_start("gather_a_start", group_a, [stack_of(s) for s in group_a], _gather_plan, 3, None)
    gather_a2, token = _split_start("gather_a2_start", group_a2, [stack_of(s) for s in group_a2], _gather_plan, 3, token)
    t0 = token[0:1, 0:1]
    bf = lambda w: (w + t0).astype(BF16)
    group_b = [bf(w_kv[0]), bf(w_out[0]), bf(flat(pool_w_grp))]
    gather_b, token = _split_start("gather_b_start", group_b, [stack_of(s) for s in group_b], _gather_plan, 9, token)
    group_c = [bf(flat(hgrn_w_in))]
    gather_c, token = _split_start("gather_c_start", group_c, [lax.empty((D, NP1), BF16)], _gather_plan, 3, token)
    group_d = [bf(w_kv[1]), bf(w_out[1])]
    gather_d, token = _split_start("gather_d_start", group_d, [stack_of(s) for s in group_d], _gather_plan, 6, token)
    started = token[0:1, 0:1]

    tek, tew = _tile(EK, 512), _tile(EK, 1024)

    mem_n = _rms_fwd("rms_mem", ms, mem_norm_g.reshape(1, D) + started)
    h0 = _rms_fwd("rms0", xs, norm_g[0:1] + started)
    wpin_lo, = _gather_finish("gather_a_finish", *_split_wait("gather_a_wait", gather_a, _gather_plan, 1, h0))

    tkw = _tile(2 * ECA, 1024)

    def kv_of(layer, wkv):
        return _matmul(
            f"kv{layer}", mem_n, wkv.reshape(D, 2 * ECA), grid=(1, 2 * ECA // tkw, 1),
            a_spec=pl.BlockSpec((M, D), lambda i, j, k: (0, 0)), b_spec=pl.BlockSpec((D, tkw), lambda i, j, k: (0, j)),
            out_shape=sds((M, 2 * ECA), BF16), out_spec=pl.BlockSpec((M, tkw), lambda i, j, k: (0, j)),
            acc_shape=(M, tkw), dims=NN)

    tko = _tile(EB, 2048)

    def out_proj(layer, branch, wout, resid):
        return _matmul(
            f"out_proj{layer}", branch, wout.reshape(EB, D), grid=(S // TM, D // TDW, EB // tko),
            a_spec=pl.BlockSpec((TM, tko), IK), b_spec=pl.BlockSpec((tko, TDW), KJ),
            out_shape=sds((S, D), F32), out_spec=pl.BlockSpec((TM, TDW), IJ),
            acc_shape=(TM, TDW), dims=NN, add=resid, add_spec=pl.BlockSpec((TM, TDW), IJ))

    ones_ca = jnp.ones((1, ECA), F32)

    ch = c0 // 2

    def proj0_half(name, w_half, first_tile, buf):
        return _matmul(
            name, h0, w_half, grid=(S // TMF, N_CHIPS * ch, 1),
            a_spec=pl.BlockSpec((TMF, D), lambda i, j, k: (i, 0)),
            b_spec=pl.BlockSpec((None, D, TNP), lambda i, j, k: (j // ch, 0, j % ch)),
            out_shape=sds((S, NP0), BF16),
            out_spec=pl.BlockSpec((TMF, TNP), lambda i, j, k: (i, (j // ch) * c0 + first_tile + j % ch)),
            acc_shape=(TMF, TNP), dims=NN, alias=buf)

    proj0 = proj0_half("proj0_lo", wpin_lo, 0, None)
    wpin_hi, = _gather_finish("gather_a2_finish", *_split_wait("gather_a2_wait", gather_a2, _gather_plan, 1, proj0))
    proj0 = proj0_half("proj0_hi", wpin_hi, ch, proj0)
    wpin = jnp.concatenate([wpin_lo, wpin_hi], axis=2)
    pooled = _pool_fwd(proj0, S, EMIX)
    wkv0, wout0, g_grp = _gather_finish("gather_b_finish", *_split_wait("gather_b_wait", gather_b, _gather_plan, 3, pooled))
    wgrp = g_grp.reshape(N_CHIPS, N_POOL_GROUPS, PG // N_CHIPS, PG).transpose(1, 0, 2, 3).reshape(N_POOL_GROUPS, PG, PG)
    kv = [kv_of(0, wkv0), None]
    premix0 = _matmul(
        "pool_grp", pooled, wgrp, grid=(S // TM, N_POOL_GROUPS, 1),
        a_spec=pl.BlockSpec((TM, PG), lambda i, j, k: (i, j)),
        b_spec=pl.BlockSpec((None, PG, PG), lambda i, j, k: (j, 0, 0)),
        out_shape=sds((S, EB), BF16), out_spec=pl.BlockSpec((TM, PG), lambda i, j, k: (i, j)),
        acc_shape=(TM, PG), dims=NN)
    premix0 = _ca_fwd("ca_fwd0", proj0, EMIX // ECA, kv[0], premix0, S, ECA, EMIX)
    colscale0 = jnp.concatenate([pool_scale.reshape(1, EMIX), ones_ca], axis=1)
    gblk0 = (EMIX + ECA) // ECA
    branch0 = _gate_fwd("gate_fwd0", premix0, proj0, gblk0, colscale0, S, EB, ECA)
    x1 = out_proj(0, branch0, wout0, xs)

    whin, = _gather_finish("gather_c_finish", *_split_wait("gather_c_wait", gather_c, _gather_plan, 1, x1))
    h1 = _rms_fwd("rms1", x1, norm_g[1:2])

    def proj1_cols(name, ncols, col_of, out_cols, out_dtype, out_col_of):
        return _matmul(
            name, h1, whin, grid=(S // TMF, ncols, 1),
            a_spec=pl.BlockSpec((TMF, D), lambda i, j, k: (i, 0)),
            b_spec=pl.BlockSpec((D, TNP), lambda i, j, k: (0, col_of(j))),
            out_shape=sds((S, out_cols), out_dtype), out_spec=pl.BlockSpec((TMF, TNP), lambda i, j, k: (i, out_col_of(j))),
            acc_shape=(TMF, TNP), dims=NN)

    skip_f = lambda j: jnp.where(j < qt, j, j + qt)
    proj1 = proj1_cols("proj1", NP1 // TNP - qt, skip_f, NP1, BF16, skip_f)
    fgate = proj1_cols("proj1_f", qt, lambda j: j + qt, EMIX, F32, lambda j: j)
    premix1, rstd1, states = _hgrn_fwd(proj1, fgate, hgrn_lb, S, EMIX, EB)
    wkv1, wout1 = _gather_finish("gather_d_finish", *_split_wait("gather_d_wait", gather_d, _gather_plan, 2, rstd1))
    kv[1] = kv_of(1, wkv1)
    premix1 = _ca_fwd("ca_fwd1", proj1, 3 * EMIX // ECA, kv[1], premix1, S, ECA, EMIX)
    norm_tiles = _allgather_small("allgather_norm_g", jnp.pad(hgrn_norm_g, ((0, SMALL_ROWS - 1), (0, 0))))
    hg_norm = norm_tiles[0::2, 0, :].reshape(1, EMIX)
    colscale1 = jnp.concatenate([hg_norm, ones_ca], axis=1)
    gblk1 = (3 * EMIX + ECA) // ECA
    branch1 = _gate_fwd("gate_fwd1", premix1, proj1, gblk1, colscale1, S, EB, ECA)
    x2 = out_proj(1, branch1, wout1, x1)

    dx2, dx2b, d_final_g, loss_part = _loss_head(x2, final_g.reshape(1, D), tgt)

    def out_proj_bwd(layer, dxb, branch, wout, premix, proj, gblk, colscale, dshape, dblk):
        goff, doff = gblk * ECA // tek, dblk * ECA // tek
        dpremix, dgate, dcol = _matmul(
            f"dbranch{layer}", dxb, wout, grid=(S // TMF, EB // tek, 1),
            a_spec=pl.BlockSpec((TMF, D), lambda i, j, k: (i, 0)),
            b_spec=pl.BlockSpec((None, tek, D), lambda i, j, k: (j // (EK // tek), j % (EK // tek), 0)),
            extras=[(premix, pl.BlockSpec((TMF, tek), IJ)), (proj, pl.BlockSpec((TMF, tek), lambda i, j, k: (i, goff + j))),
                    (colscale, pl.BlockSpec((1, tek), lambda i, j, k: (0, j)))],
            epilogue=_gate_bwd_epilogue,
            out_shape=[sds((S, EB), BF16), sds(dshape, BF16), sds((S // TMF, 1, EB), F32)],
            out_spec=[pl.BlockSpec((TMF, tek), IJ), pl.BlockSpec((TMF, tek), lambda i, j, k: (i, doff + j)),
                      pl.BlockSpec((None, 1, tek), lambda i, j, k: (i, 0, j))],
            acc_shape=(TMF, tek), dims=NT)
        dw = _matmul(
            f"dwout{layer}", branch, dxb, grid=(EB // tew, D // TD, 1),
            a_spec=pl.BlockSpec((S, tew), lambda i, j, k: (0, i)), b_spec=pl.BlockSpec((S, TD), lambda i, j, k: (0, j)),
            out_shape=sds((N_CHIPS, EK, D), BF16),
            out_spec=pl.BlockSpec((None, tew, TD), lambda i, j, k: (i // (EK // tew), i % (EK // tew), j)),
            acc_shape=(tew, TD), dims=TN)
        return dpremix, dgate, dcol.reshape(S // TMF, EB), dw

    def kv_bwd(layer, dkv, wkv, dmem_add):
        dkvb = dkv.astype(BF16)
        dmem = _matmul(
            f"dmem{layer}", dkvb, wkv.reshape(D, 2 * ECA), grid=(1, D // TDW, 1),
            a_spec=pl.BlockSpec((M, 2 * ECA), lambda i, j, k: (0, 0)),
            b_spec=pl.BlockSpec((TDW, 2 * ECA), lambda i, j, k: (j, 0)),
            out_shape=sds((M, D), F32), out_spec=pl.BlockSpec((M, TDW), lambda i, j, k: (0, j)), acc_shape=(M, TDW),
            dims=NT, add=dmem_add, add_spec=pl.BlockSpec((M, TDW), lambda i, j, k: (0, j)))
        dw = _matmul(
            f"dwkv{layer}", mem_n, dkvb, grid=(D // TDW, 2 * ECA // tkw, 1),
            a_spec=pl.BlockSpec((M, TDW), lambda i, j, k: (0, i)), b_spec=pl.BlockSpec((M, tkw), lambda i, j, k: (0, j)),
            out_shape=sds((D, 2 * ECA), BF16), out_spec=pl.BlockSpec((TDW, tkw), IJ), acc_shape=(TDW, tkw), dims=TN)
        return dmem, dw.reshape(N_CHIPS, DK, 2 * ECA)

    dpremix1, drest1, dcol1, gw_out1 = out_proj_bwd(1, dx2b, branch1, wout1, premix1, proj1, gblk1, colscale1,
                                                    (S, ECA + EB), 1)
    drest1, dkv1 = _ca_bwd("ca_bwd1", dpremix1, proj1, 3 * EMIX // ECA, kv[1], drest1, 0, S, ECA, EMIX)
    dqfi, dlb = _hgrn_bwd(dpremix1, premix1, rstd1, states, proj1, fgate, hgrn_lb, S, EMIX)
    nq, nr = 3 * qt, (ECA + EB) // TNP
    tkh = _tile(EMIX, 1024) if (ECA + EB) % _tile(EMIX, 1024) == 0 else TNP
    kq = EMIX // tkh
    dh1 = _matmul(
        "dh1_qfi", dqfi, whin, grid=(S // TM, D // TDW, 3),
        a_spec=pl.BlockSpec((None, TM, EMIX), lambda i, j, k: (k, i, 0)),
        b_spec=pl.BlockSpec((TDW, EMIX), lambda i, j, k: (j, k)),
        out_shape=sds((S, D), F32), out_spec=pl.BlockSpec((TM, TDW), IJ), acc_shape=(TM, TDW), dims=NT)
    dh1 = _matmul(
        "dh1_rest", drest1, whin, grid=(S // TM, D // TDW, (ECA + EB) // tkh), a_spec=pl.BlockSpec((TM, tkh), IK),
        b_spec=pl.BlockSpec((TDW, tkh), lambda i, j, k: (j, k + 3 * kq)),
        out_shape=sds((S, D), F32), out_spec=pl.BlockSpec((TM, TDW), IJ), acc_shape=(TM, TDW), dims=NT,
        add=dh1, add_spec=pl.BlockSpec((TM, TDW), IJ))
    gw_hin = _matmul(
        "dwhin_qfi", h1, dqfi, grid=(D // TDW, nq, 1), a_spec=pl.BlockSpec((S, TDW), lambda i, j, k: (0, i)),
        b_spec=pl.BlockSpec((None, S, TNP), lambda i, j, k: (j // qt, 0, j % qt)),
        out_shape=sds((N_CHIPS, D, SH1), BF16), out_spec=pl.BlockSpec((None, TDW, TNP), lambda i, j, k: (j // c1, i, j % c1)),
        acc_shape=(TDW, TNP), dims=TN)
    gw_hin = _matmul(
        "dwhin_rest", h1, drest1, grid=(D // TDW, nr, 1), a_spec=pl.BlockSpec((S, TDW), lambda i, j, k: (0, i)),
        b_spec=pl.BlockSpec((S, TNP), lambda i, j, k: (0, j)), out_shape=sds((N_CHIPS, D, SH1), BF16),
        out_spec=pl.BlockSpec((None, TDW, TNP), lambda i, j, k: ((j + nq) // c1, i, (j + nq) % c1)),
        acc_shape=(TDW, TNP), dims=TN, alias=gw_hin)
    dmem, gw_kv1 = kv_bwd(1, dkv1, wkv1, None)

    core_chip = jnp.stack([lax.axis_index("c"), chip]).astype(jnp.int32)

    def reduce_in_chip(tag, stacks):
        got = _exchange_halves(f"exchange_halves{tag}", stacks)
        pairs = [_add_halves(f"add_halves{tag}_{t}", core_chip, g.reshape(N_CHIPS, 2, g.shape[1] // 2, g.shape[2]), r)
                 for t, (g, r) in enumerate(zip(stacks, got))]
        return [p for p, _ in pairs], [own for _, own in pairs]

    parts1, landed1 = reduce_in_chip(1, [gw_kv1, gw_out1, gw_hin])
    scatter1, token1 = _split_start("scatter1_start", parts1, landed1, _scatter_plan, 3 * len(parts1), None)
    dx1, dx1b, d_ng1 = _rms_bwd("rms_bwd1", dh1, x1, norm_g[1:2] + token1[0:1, 0:1], dx2)

    dpremix0, dproj0, dcol0, gw_out0 = out_proj_bwd(0, dx1b, branch0, wout0, premix0, proj0, gblk0, colscale0,
                                                    (S, NP0), gblk0)
    dproj0, dkv0 = _ca_bwd("ca_bwd0", dpremix0, proj0, EMIX // ECA, kv[0], dproj0, EMIX // ECA, S, ECA, EMIX)
    dmem, gw_kv0 = kv_bwd(0, dkv0, wkv0, dmem)
    parts_a, landed_a = reduce_in_chip("0a", [gw_kv0, gw_out0])
    scatter_a, token_a = _split_start("scatter0a_start", parts_a, landed_a, _scatter_plan, 3 * len(parts_a), None)
    dpooled = _matmul(
        "dpooled", dpremix0, wgrp, grid=(S // TM, N_POOL_GROUPS, 1), a_spec=pl.BlockSpec((TM, PG), IJ),
        b_spec=pl.BlockSpec((None, PG, PG), lambda i, j, k: (j, 0, 0)),
        out_shape=sds((S, EMIX), F32), out_spec=pl.BlockSpec((TM, PG), IJ), acc_shape=(TM, PG), dims=NT, after=token_a)
    dwgrp = _matmul(
        "dwgrp", pooled, dpremix0, grid=(N_POOL_GROUPS, 1, 1), a_spec=pl.BlockSpec((S, PG), lambda i, j, k: (0, i)),
        b_spec=pl.BlockSpec((S, PG), lambda i, j, k: (0, i)), out_shape=sds((N_POOL_GROUPS, PG, PG), F32),
        out_spec=pl.BlockSpec((None, PG, PG), lambda i, j, k: (i, 0, 0)), acc_shape=(PG, PG), dims=TN)
    dproj0 = _pool_bwd(dpooled, dproj0, S, EMIX)
    gw_pin = _matmul(
        "dwpin", h0, dproj0, grid=(D // TDW, NP0 // TNP, 1), a_spec=pl.BlockSpec((S, TDW), lambda i, j, k: (0, i)),
        b_spec=pl.BlockSpec((S, TNP), lambda i, j, k: (0, j)), out_shape=sds((N_CHIPS, D, SH0), BF16),
        out_spec=pl.BlockSpec((None, TDW, TNP), lambda i, j, k: (j // c0, i, j % c0)), acc_shape=(TDW, TNP), dims=TN)
    gw_grp = dwgrp.reshape(N_POOL_GROUPS, N_CHIPS, PG // N_CHIPS, PG).transpose(1, 0, 2, 3).reshape(N_CHIPS, PG, PG)
    parts_b, landed_b = reduce_in_chip("0b", [gw_pin, gw_grp.astype(BF16)])
    scatter_b, token_b = _split_start("scatter0b_start", parts_b, landed_b, _scatter_plan, 3 * len(parts_b), None)
    dh0 = _matmul(
        "dh0", dproj0, wpin, grid=(S // TM, D // TDW, N_CHIPS), a_spec=pl.BlockSpec((TM, SH0), IK),
        b_spec=pl.BlockSpec((None, TDW, SH0), lambda i, j, k: (k, j, 0)),
        out_shape=sds((S, D), F32), out_spec=pl.BlockSpec((TM, TDW), IJ), acc_shape=(TM, TDW), dims=NT, after=token_b)
    grad_x, _, d_ng0 = _rms_bwd("rms_bwd0", dh0, xs, norm_g[0:1], dx1)
    _, _, d_mng = _rms_bwd("rms_bwd_mem", dmem, ms, mem_norm_g.reshape(1, D), jnp.zeros_like(ms))

    _, landed1 = _split_wait("scatter1_wait", scatter1, _scatter_plan, len(parts1), grad_x)
    _, landed_a = _split_wait("scatter0a_wait", scatter_a, _scatter_plan, len(parts_a), grad_x)
    _, landed_b = _split_wait("scatter0b_wait", scatter_b, _scatter_plan, len(parts_b), grad_x)
    landed = [landed_a[0], landed1[0], landed_a[1], landed1[1], landed_b[0], landed_b[1], landed1[2]]
    fulls = _sum_share(landed)
    f2 = [f.reshape(-1, f.shape[-1]) for f in fulls]
    grads, deltas, new_m, new_v = {}, {}, {}, {}
    for n, w, mm, vv, gs in (("w_kv", w_kv, m_w_kv, v_w_kv, f2[0:2]), ("w_out", w_out, m_w_out, v_w_out, f2[2:4]),
                             ("pool_w_in", pool_w_in, m_pool_w_in, v_pool_w_in, f2[4:5]),
                             ("pool_w_grp", pool_w_grp, m_pool_w_grp, v_pool_w_grp, f2[5:6]),
                             ("hgrn_w_in", hgrn_w_in, m_hgrn_w_in, v_hgrn_w_in, f2[6:7])):
        as3d = lambda a: a.reshape((a.shape[0], -1, a.shape[-1]))
        outs = _adamw_layers(f"adamw_{n}", as3d(w), gs, as3d(mm), as3d(vv))
        grads[n], deltas[n], new_m[n], new_v[n] = [o.reshape(w.shape) for o in outs]

    Wd = EMIX
    partial = _pack_rows("pack_partials", [d_ng0, d_ng1, d_mng, dcol0[:, :EMIX], dlb, dcol1[:, :EMIX], d_final_g,
                                           loss_part], Wd)
    summed = _small_sum(_allgather_small("allgather_grads", partial), hgrn_lb, 4)
    row = lambda i, n=Wd: summed[i:i + 1, :n]
    nshard = EMIX // N_CHIPS
    g_hg_norm = lax.dynamic_slice_in_dim(row(5), chip * nshard, nshard, axis=1)
    small_names = ["norm_g0", "norm_g1", "mem_norm_g", "pool_scale", "hgrn_lb0", "hgrn_lb1", "hgrn_norm_g", "final_g"]
    small_w = [norm_g[0:1], norm_g[1:2], mem_norm_g.reshape(1, D), pool_scale, hgrn_lb[0:1], hgrn_lb[1:2], hgrn_norm_g,
               final_g.reshape(1, D)]
    small_m = [m_norm_g[0:1], m_norm_g[1:2], m_mem_norm_g.reshape(1, D), m_pool_scale, m_hgrn_lb[0:1], m_hgrn_lb[1:2],
               m_hgrn_norm_g, m_final_g.reshape(1, D)]
    small_v = [v_norm_g[0:1], v_norm_g[1:2], v_mem_norm_g.reshape(1, D), v_pool_scale, v_hgrn_lb[0:1], v_hgrn_lb[1:2],
               v_hgrn_norm_g, v_final_g.reshape(1, D)]
    g_pack = _pack_rows("pack_small_g", [row(0, D), row(1, D), row(2, D), row(3), row(8), row(9), g_hg_norm, row(6, D)], Wd)
    d_pack, m_pack, v_pack = _adamw("adamw_small", _pack_rows("pack_small_w", small_w, Wd), g_pack,
                                    _pack_rows("pack_small_m", small_m, Wd), _pack_rows("pack_small_v", small_v, Wd))
    widths = [v.shape[1] for v in small_w]
    rows = lambda p: {n: p[i, :widths[i]] for i, n in enumerate(small_names)}

    def assemble(r, out):
        out["norm_g"] = jnp.stack([r["norm_g0"], r["norm_g1"]])
        out["mem_norm_g"] = r["mem_norm_g"]
        out["pool_scale"] = r["pool_scale"].reshape(1, EMIX)
        out["hgrn_lb"] = jnp.stack([r["hgrn_lb0"], r["hgrn_lb1"]])
        out["hgrn_norm_g"] = r["hgrn_norm_g"].reshape(1, nshard)
        out["final_g"] = r["final_g"]

    assemble(rows(g_pack), grads)
    assemble(rows(d_pack), deltas)
    assemble(rows(m_pack), new_m)
    assemble(rows(v_pack), new_v)
    loss = summed[7, 0]

    order = ["norm_g", "mem_norm_g", "w_kv", "w_out", "pool_w_in", "pool_w_grp", "pool_scale", "hgrn_w_in", "hgrn_lb",
             "hgrn_norm_g", "final_g"]
    return (loss, grad_x.reshape(1, S, D), *[grads[n] for n in order], *[deltas[n] for n in order],
            *[new_m[n] for n in order], *[new_v[n] for n in order])
```

```python
import functools

import jax
import jax.numpy as jnp
from jax import lax
from jax.experimental import pallas as pl
from jax.experimental.pallas import tpu as pltpu

F32 = jnp.float32
BF16 = jnp.bfloat16
MESH = pl.DeviceIdType.MESH
ANY = pl.BlockSpec(memory_space=pl.ANY)

EPS = 1e-6
HG_HEAD_DIM = 128
HG_CHUNK = 64
CA_HEADS = 4
N_POOL_GROUPS = 4
POOL_HALO = 128
ADAM_LR = 0.001
ADAM_B1 = 0.9
ADAM_B2 = 0.999
ADAM_EPS = 1e-08
ADAM_WD = 0.01
ADAM_STEP = 10
N_CHIPS = 4
N_DEV = 8
VMEM_LIMIT_BYTES = 56 * 1024 * 1024
SMALL_ROWS = 8
STREAM_CHUNK_BYTES = 2 * 1024 * 1024
STREAM_SLOTS = 3
SUM_SLOTS = 2


def _params(*sem):
    return pltpu.CompilerParams(dimension_semantics=sem, vmem_limit_bytes=VMEM_LIMIT_BYTES)


def _tile(n, pref):
    t = pref
    while n % t:
        t //= 2
    return t


def _sigmoid(x):
    return 1.0 / (1.0 + jnp.exp(-x))


def _matmul(name, a, b, *, grid, a_spec, b_spec, out_shape, out_spec, acc_shape, dims,
            add=None, add_spec=None, alias=None, after=None, extras=(), epilogue=None):
    nk = grid[2]
    has_add = add is not None
    has_alias = alias is not None
    has_after = after is not None
    n_out = len(out_shape) if epilogue is not None else 1

    def body(*refs):
        a_ref, b_ref = refs[0], refs[1]
        pos = 2
        add_ref = None
        if has_add:
            add_ref = refs[pos]
            pos += 1
        extra_refs = refs[pos:pos + len(extras)]
        pos += len(extras) + has_alias + has_after
        o_refs = refs[pos:pos + n_out]
        prod = lax.dot_general(a_ref[...], b_ref[...], (dims, ((), ())), preferred_element_type=F32)

        def finish(r):
            if epilogue is not None:
                epilogue(r, extra_refs, o_refs)
                return
            if has_add:
                r = r + add_ref[...].astype(F32)
            o_refs[0][...] = r.astype(o_refs[0].dtype)

        if nk == 1:
            finish(prod)
            return
        acc_ref = refs[pos + n_out]
        k = pl.program_id(2)

        @pl.when(k == 0)
        def _():
            acc_ref[...] = prod

        @pl.when(k > 0)
        def _():
            acc_ref[...] += prod

        @pl.when(k == nk - 1)
        def _():
            finish(acc_ref[...])

    operands = [a, b]
    in_specs = [a_spec, b_spec]
    if has_add:
        operands.append(add)
        in_specs.append(add_spec)
    for arr, spec in extras:
        operands.append(arr)
        in_specs.append(spec)
    aliases = {}
    if has_alias:
        aliases = {len(operands): 0}
        operands.append(alias)
        in_specs.append(ANY)
    if has_after:
        operands.append(after)
        in_specs.append(ANY)
    return pl.pallas_call(
        body, name=name, grid=grid, in_specs=in_specs, out_specs=out_spec, out_shape=out_shape,
        scratch_shapes=[pltpu.VMEM(acc_shape, F32)] if nk > 1 else [], input_output_aliases=aliases,
        compiler_params=_params("parallel", "parallel", "arbitrary"),
    )(*operands)


IJ = lambda i, j, k: (i, j)
IK = lambda i, j, k: (i, k)
KJ = lambda i, j, k: (k, j)
KI = lambda i, j, k: (k, i)
NN = ((1,), (0,))
NT = ((1,), (1,))
TN = ((0,), (0,))


def _rms_fwd(name, x, g):
    R, D = x.shape
    tr = _tile(R, 256)

    def body(x_ref, g_ref, o_ref):
        xf = x_ref[...]
        r = lax.rsqrt(jnp.mean(xf * xf, axis=-1, keepdims=True) + EPS)
        o_ref[...] = (xf * r * g_ref[...]).astype(o_ref.dtype)

    return pl.pallas_call(
        body, name=name, grid=(R // tr,),
        in_specs=[pl.BlockSpec((tr, D), lambda i: (i, 0)), pl.BlockSpec((1, D), lambda i: (0, 0))],
        out_specs=pl.BlockSpec((tr, D), lambda i: (i, 0)),
        out_shape=jax.ShapeDtypeStruct((R, D), BF16), compiler_params=_params("parallel"),
    )(x, g)


def _rms_bwd(name, dh, x, g, dres):
    R, D = x.shape
    tr = _tile(R, 256)

    def body(dh_ref, x_ref, g_ref, dres_ref, dx_ref, dxb_ref, dg_ref):
        xf = x_ref[...]
        r = lax.rsqrt(jnp.mean(xf * xf, axis=-1, keepdims=True) + EPS)
        xn = xf * r
        d = dh_ref[...]
        dyg = d * g_ref[...]
        dx = r * (dyg - xn * jnp.mean(dyg * xn, axis=-1, keepdims=True)) + dres_ref[...]
        dx_ref[...] = dx
        dxb_ref[...] = dx.astype(BF16)

        @pl.when(pl.program_id(0) == 0)
        def _():
            dg_ref[...] = jnp.zeros_like(dg_ref)

        dg_ref[...] += jnp.sum(d * xn, axis=0, keepdims=True)

    row = pl.BlockSpec((tr, D), lambda i: (i, 0))
    vec = pl.BlockSpec((1, D), lambda i: (0, 0))
    return pl.pallas_call(
        body, name=name, grid=(R // tr,), in_specs=[row, row, vec, row], out_specs=[row, row, vec],
        out_shape=[jax.ShapeDtypeStruct((R, D), F32), jax.ShapeDtypeStruct((R, D), BF16),
                   jax.ShapeDtypeStruct((1, D), F32)],
        compiler_params=_params("arbitrary"),
    )(dh, x, g, dres)


def _loss_head(x2, g, target):
    R, D = x2.shape
    tr = _tile(R, 256)

    def body(x_ref, g_ref, t_ref, dx_ref, dxb_ref, dg_ref, loss_ref):
        xf = x_ref[...]
        gg = g_ref[...]
        r = lax.rsqrt(jnp.mean(xf * xf, axis=-1, keepdims=True) + EPS)
        xn = xf * r
        e = xn * gg - t_ref[...]
        part = 0.5 * jnp.sum(jnp.mean(e * e, axis=-1, keepdims=True), axis=0, keepdims=True)
        dy = e * (1.0 / D)
        dyg = dy * gg
        dx = r * (dyg - xn * jnp.mean(dyg * xn, axis=-1, keepdims=True))
        dx_ref[...] = dx
        dxb_ref[...] = dx.astype(BF16)

        @pl.when(pl.program_id(0) == 0)
        def _():
            dg_ref[...] = jnp.zeros_like(dg_ref)
            loss_ref[...] = jnp.zeros_like(loss_ref)

        dg_ref[...] += jnp.sum(dy * xn, axis=0, keepdims=True)
        loss_ref[...] += jnp.broadcast_to(part, loss_ref.shape)

    row = pl.BlockSpec((tr, D), lambda i: (i, 0))
    vec = pl.BlockSpec((1, D), lambda i: (0, 0))
    return pl.pallas_call(
        body, name="loss_head", grid=(R // tr,), in_specs=[row, vec, row],
        out_specs=[row, row, vec, pl.BlockSpec((1, 128), lambda i: (0, 0))],
        out_shape=[jax.ShapeDtypeStruct((R, D), F32), jax.ShapeDtypeStruct((R, D), BF16),
                   jax.ShapeDtypeStruct((1, D), F32), jax.ShapeDtypeStruct((1, 128), F32)],
        compiler_params=_params("arbitrary"),
    )(x2, g, target)


def _pool_band(tr, reverse, w):
    r = lax.broadcasted_iota(jnp.int32, (tr, tr + POOL_HALO), 0)
    c = lax.broadcasted_iota(jnp.int32, (tr, tr + POOL_HALO), 1)
    if reverse:
        inside = (c >= r) & (c < r + w)
    else:
        cc = c - POOL_HALO
        inside = (cc <= r) & (cc > r - w)
    return jnp.where(inside, 1.0, 0.0).astype(BF16)


def _pool_fwd(proj, S, EMIX):
    PG = EMIX // N_POOL_GROUPS
    cb = PG
    tr = _tile(S, 512)
    per_group = PG // cb

    def body(u_ref, o_ref, ext):
        i = pl.program_id(1)
        w = jnp.left_shift(2, pl.program_id(0) // per_group)

        @pl.when(i == 0)
        def _():
            ext[0:POOL_HALO, :] = jnp.zeros((POOL_HALO, cb), BF16)

        u = u_ref[...]
        ext[POOL_HALO:, :] = u
        win = jnp.dot(_pool_band(tr, False, w), ext[...], preferred_element_type=F32)
        pos = i * tr + lax.broadcasted_iota(jnp.int32, (tr, 1), 0)
        cnt = jnp.minimum(pos + 1, w).astype(F32)
        o_ref[...] = (win / cnt - u.astype(F32)).astype(BF16)
        ext[0:POOL_HALO, :] = u[tr - POOL_HALO:, :]

    return pl.pallas_call(
        body, name="pool_fwd", grid=(EMIX // cb, S // tr),
        in_specs=[pl.BlockSpec((tr, cb), lambda j, i: (i, j))],
        out_specs=pl.BlockSpec((tr, cb), lambda j, i: (i, j)),
        out_shape=jax.ShapeDtypeStruct((S, EMIX), BF16),
        scratch_shapes=[pltpu.VMEM((tr + POOL_HALO, cb), BF16)],
        compiler_params=_params("parallel", "arbitrary"),
    )(proj)


def _pool_bwd(dpooled, dproj, S, EMIX):
    PG = EMIX // N_POOL_GROUPS
    cb = PG
    tr = _tile(S, 512)
    per_group = PG // cb
    nrt = S // tr

    def body(d_ref, _, o_ref, ext):
        step = pl.program_id(1)
        i = nrt - 1 - step
        w = jnp.left_shift(2, pl.program_id(0) // per_group)

        @pl.when(step == 0)
        def _():
            ext[tr:, :] = jnp.zeros((POOL_HALO, cb), BF16)

        d = d_ref[...]
        pos = i * tr + lax.broadcasted_iota(jnp.int32, (tr, 1), 0)
        cnt = jnp.minimum(pos + 1, w).astype(F32)
        z = (d / cnt).astype(BF16)
        ext[0:tr, :] = z
        win = jnp.dot(_pool_band(tr, True, w), ext[...], preferred_element_type=F32)
        o_ref[...] = (win - d).astype(BF16)
        ext[tr:, :] = z[0:POOL_HALO, :]

    return pl.pallas_call(
        body, name="pool_bwd", grid=(EMIX // cb, nrt),
        in_specs=[pl.BlockSpec((tr, cb), lambda j, s: (nrt - 1 - s, j)), ANY],
        out_specs=pl.BlockSpec((tr, cb), lambda j, s: (nrt - 1 - s, j)),
        out_shape=jax.ShapeDtypeStruct(dproj.shape, dproj.dtype),
        scratch_shapes=[pltpu.VMEM((tr + POOL_HALO, cb), BF16)],
        input_output_aliases={1: 0},
        compiler_params=_params("parallel", "arbitrary"),
    )(dpooled, dproj)


def _ca_fwd(name, proj, qblk, kv, premix, S, ECA, EMIX):
    M = kv.shape[0]
    hd = ECA // CA_HEADS
    ts = _tile(S, 512)
    scale = hd ** -0.5

    def body(q_ref, kv_ref, _, o_ref):
        for h in range(CA_HEADS):
            q = q_ref[:, h * hd:(h + 1) * hd]
            k = kv_ref[:, h * hd:(h + 1) * hd]
            v = kv_ref[:, ECA + h * hd:ECA + (h + 1) * hd]
            s = lax.dot_general(q, k, (NT, ((), ())), preferred_element_type=F32) * scale
            s = s - jnp.max(s, axis=-1, keepdims=True)
            p = jnp.exp(s)
            p = p / jnp.sum(p, axis=-1, keepdims=True)
            o = jnp.dot(p.astype(BF16), v, preferred_element_type=F32)
            o_ref[:, h * hd:(h + 1) * hd] = o.astype(BF16)

    return pl.pallas_call(
        body, name=name, grid=(S // ts,),
        in_specs=[pl.BlockSpec((ts, ECA), lambda i: (i, qblk)), pl.BlockSpec((M, 2 * ECA), lambda i: (0, 0)), ANY],
        out_specs=pl.BlockSpec((ts, ECA), lambda i: (i, EMIX // ECA)),
        out_shape=jax.ShapeDtypeStruct(premix.shape, premix.dtype),
        input_output_aliases={2: 0}, compiler_params=_params("parallel"),
    )(proj, kv, premix)


def _ca_bwd(name, dpremix, proj, qblk, kv, dbuf, dblk, S, ECA, EMIX):
    M = kv.shape[0]
    hd = ECA // CA_HEADS
    ts = _tile(S, 512)
    scale = hd ** -0.5

    def body(do_ref, q_ref, kv_ref, _, dq_ref, dkv_ref):
        @pl.when(pl.program_id(0) == 0)
        def _():
            dkv_ref[...] = jnp.zeros_like(dkv_ref)

        for h in range(CA_HEADS):
            lo, hi = h * hd, (h + 1) * hd
            q = q_ref[:, lo:hi]
            k = kv_ref[:, lo:hi]
            v = kv_ref[:, ECA + lo:ECA + hi]
            do = do_ref[:, lo:hi]
            s = lax.dot_general(q, k, (NT, ((), ())), preferred_element_type=F32) * scale
            s = s - jnp.max(s, axis=-1, keepdims=True)
            p = jnp.exp(s)
            p = p / jnp.sum(p, axis=-1, keepdims=True)
            pb = p.astype(BF16)
            dkv_ref[:, ECA + lo:ECA + hi] += lax.dot_general(pb, do, (TN, ((), ())), preferred_element_type=F32)
            dp = lax.dot_general(do, v, (NT, ((), ())), preferred_element_type=F32)
            ds = (p * (dp - jnp.sum(p * dp, axis=-1, keepdims=True)) * scale).astype(BF16)
            dq_ref[:, lo:hi] = jnp.dot(ds, k, preferred_element_type=F32).astype(BF16)
            dkv_ref[:, lo:hi] += lax.dot_general(ds, q, (TN, ((), ())), preferred_element_type=F32)

    return pl.pallas_call(
        body, name=name, grid=(S // ts,),
        in_specs=[pl.BlockSpec((ts, ECA), lambda i: (i, EMIX // ECA)), pl.BlockSpec((ts, ECA), lambda i: (i, qblk)),
                  pl.BlockSpec((M, 2 * ECA), lambda i: (0, 0)), ANY],
        out_specs=[pl.BlockSpec((ts, ECA), lambda i: (i, dblk)), pl.BlockSpec((M, 2 * ECA), lambda i: (0, 0))],
        out_shape=[jax.ShapeDtypeStruct(dbuf.shape, dbuf.dtype), jax.ShapeDtypeStruct((M, 2 * ECA), F32)],
        input_output_aliases={3: 0}, compiler_params=_params("arbitrary"),
    )(dpremix, proj, kv, dbuf)


def _gate_fwd(name, premix, proj, gblk, colscale, S, EB, ECA):
    ts = _tile(S, 512)

    def body(p_ref, g_ref, c_ref, o_ref):
        g = g_ref[...].astype(F32)
        o_ref[...] = (p_ref[...].astype(F32) * c_ref[...] * (g * _sigmoid(g))).astype(BF16)

    return pl.pallas_call(
        body, name=name, grid=(S // ts, EB // ECA),
        in_specs=[pl.BlockSpec((ts, ECA), lambda i, j: (i, j)), pl.BlockSpec((ts, ECA), lambda i, j: (i, gblk + j)),
                  pl.BlockSpec((1, ECA), lambda i, j: (0, j))],
        out_specs=pl.BlockSpec((ts, ECA), lambda i, j: (i, j)),
        out_shape=jax.ShapeDtypeStruct((S, EB), BF16), compiler_params=_params("parallel", "parallel"),
    )(premix, proj, colscale)


def _gate_bwd_epilogue(db, extra_refs, out_refs):
    p_ref, g_ref, c_ref = extra_refs
    dp_ref, dg_ref, dc_ref = out_refs
    g = g_ref[...].astype(F32)
    sg = _sigmoid(g)
    si = g * sg
    c = c_ref[...]
    t = db * p_ref[...].astype(F32)
    dp_ref[...] = (db * si * c).astype(BF16)
    dg_ref[...] = (t * c * (sg * (1.0 + g * (1.0 - sg)))).astype(BF16)
    dc_ref[...] = jnp.sum(t * si, axis=0, keepdims=True)


def _hgrn_lb(lb_ref):
    l0 = lb_ref[0:1, :]
    l1 = lb_ref[1:2, :]
    mx = jnp.maximum(l0, l1)
    e0 = jnp.exp(l0 - mx)
    e1 = jnp.exp(l1 - mx)
    return e1 / (e0 + e1)


def _bdot(a, b, ca, cb):
    return lax.dot_general(a, b, (((ca,), (cb,)), ((0,), (0,))), preferred_element_type=F32)


def _tri_sum(tri, x):
    hi = x.astype(BF16)
    lo = (x - hi.astype(F32)).astype(BF16)
    tri = tri.astype(BF16)
    return _bdot(tri, hi, 2, 1) + _bdot(tri, lo, 2, 1)


def _hgrn_chunks(qin, fin, lbh, n):
    C = HG_CHUNK
    row = lax.broadcasted_iota(jnp.int32, (n, C, C), 1)
    col = lax.broadcasted_iota(jnp.int32, (n, C, C), 2)
    causal = row >= col
    sg = _sigmoid(fin)
    f = lbh + (1.0 - lbh) * sg
    k = 1.0 - f
    g = jnp.log(f)
    b = _tri_sum(jnp.where(causal, 1.0, 0.0), g)
    b_last = jnp.sum(g, axis=1, keepdims=True)
    eb = jnp.exp(b)
    einv = jnp.exp(-b)
    eend = jnp.exp(b_last - b)
    sq = _sigmoid(qin)
    a = qin * sq * (HG_HEAD_DIM ** -0.5) * eb
    bm = k * einv
    e = k * eend
    d = jnp.exp(b_last)
    p = jnp.where(causal, _bdot(a.astype(BF16), bm.astype(BF16), 2, 2), 0.0)
    return dict(causal=causal, sg=sg, f=f, eb=eb, einv=einv, eend=eend, sq=sq, a=a, bm=bm, e=e, d=d, p=p)


def _hgrn_fwd(proj, fgate, hgrn_lb, S, EMIX, EB):
    HD, C = HG_HEAD_DIM, HG_CHUNK
    HH = EMIX // HD
    hb = 6 if HH % 6 == 0 else 1
    W = hb * HD
    tr = _tile(S, 512)
    n = tr // C

    def body(q_ref, f_ref, i_ref, lb_ref, o_ref, rstd_ref, st_ref, state):
        @pl.when(pl.program_id(1) == 0)
        def _():
            state[...] = jnp.zeros_like(state)

        lb = _hgrn_lb(lb_ref)
        for h in range(hb):
            cs = slice(h * HD, (h + 1) * HD)
            qin = q_ref[:, cs].astype(F32).reshape(n, C, HD)
            fin = f_ref[:, cs].reshape(n, C, HD)
            v = i_ref[:, cs].reshape(n, C, HD)
            t = _hgrn_chunks(qin, fin, lb[:, cs], n)
            upd = _bdot(v, t["e"].astype(BF16), 1, 1)
            st = state[h]
            for c in range(n):
                st_ref[h, c] = st
                st = st * t["d"][c] + upd[c]
            state[h] = st
            o = _bdot(t["p"].astype(BF16), v, 2, 1) + _bdot(t["a"].astype(BF16), st_ref[h].astype(BF16), 2, 2)
            rstd = lax.rsqrt(jnp.mean(o * o, axis=-1, keepdims=True) + EPS)
            o_ref[:, cs] = (o * rstd).reshape(tr, HD).astype(BF16)
            rstd_ref[:, cs] = jnp.broadcast_to(rstd, (n, C, HD)).reshape(tr, HD)

    blk = lambda off: pl.BlockSpec((tr, W), lambda g, i: (i, off + g))
    return pl.pallas_call(
        body, name="hgrn_fwd", grid=(HH // hb, S // tr),
        in_specs=[blk(0), blk(0), blk(2 * EMIX // W), pl.BlockSpec((2, W), lambda g, i: (0, g))],
        out_specs=[blk(0), blk(0), pl.BlockSpec((hb, n, HD, HD), lambda g, i: (g, i, 0, 0))],
        out_shape=[jax.ShapeDtypeStruct((S, EB), BF16), jax.ShapeDtypeStruct((S, EMIX), F32),
                   jax.ShapeDtypeStruct((HH, S // C, HD, HD), F32)],
        scratch_shapes=[pltpu.VMEM((hb, HD, HD), F32)],
        compiler_params=_params("parallel", "arbitrary"),
    )(proj, fgate, proj, hgrn_lb)


def _hgrn_bwd(dpremix, premix, rstd, states, proj, fgate, hgrn_lb, S, EMIX):
    HD, C = HG_HEAD_DIM, HG_CHUNK
    HH = EMIX // HD
    hb = 6 if HH % 6 == 0 else 1
    W = hb * HD
    tr = _tile(S, 512)
    n = tr // C
    nrt = S // tr

    def body(do_ref, on_ref, rstd_ref, st_ref, q_ref, f_ref, i_ref, lb_ref, d_ref, dlb_ref, dstate, dsbuf):
        @pl.when(pl.program_id(1) == 0)
        def _():
            dstate[...] = jnp.zeros_like(dstate)
            dlb_ref[...] = jnp.zeros_like(dlb_ref)

        lb = _hgrn_lb(lb_ref)
        for h in range(hb):
            cs = slice(h * HD, (h + 1) * HD)
            qin = q_ref[:, cs].astype(F32).reshape(n, C, HD)
            fin = f_ref[:, cs].reshape(n, C, HD)
            v = i_ref[:, cs].reshape(n, C, HD)
            lbh = lb[:, cs]
            t = _hgrn_chunks(qin, fin, lbh, n)
            a, bm, e, d, p = t["a"], t["bm"], t["e"], t["d"], t["p"]
            ab, bmb, eb16 = a.astype(BF16), bm.astype(BF16), e.astype(BF16)
            on = on_ref[:, cs].astype(F32).reshape(n, C, HD)
            dn = do_ref[:, cs].astype(F32).reshape(n, C, HD)
            do = rstd_ref[:, cs].reshape(n, C, HD) * (dn - on * jnp.mean(dn * on, axis=-1, keepdims=True))
            dob = do.astype(BF16)
            grow = _bdot(dob, ab, 1, 1)
            ds = dstate[h]
            for c in reversed(range(n)):
                dsbuf[h, c] = ds
                ds = ds * d[c] + grow[c]
            dstate[h] = ds
            dst = dsbuf[h]
            st = st_ref[h]
            dstb = dst.astype(BF16)
            dp = jnp.where(t["causal"], _bdot(dob, v, 2, 2), 0.0).astype(BF16)
            dv = _bdot(p.astype(BF16), dob, 1, 1) + _bdot(eb16, dstb, 2, 2)
            da = _bdot(dp, bmb, 2, 1) + _bdot(dob, st.astype(BF16), 2, 1)
            dbm = _bdot(dp, ab, 1, 1)
            de = _bdot(v, dstb, 2, 1)
            dd = jnp.sum(dst * st, axis=1, keepdims=True)
            dk = dbm * t["einv"] + de * t["eend"]
            dee = de * e
            db = da * a - dbm * bm - dee
            extra = jnp.sum(dee, axis=1, keepdims=True) + dd * d
            upper = jnp.where(lax.broadcasted_iota(jnp.int32, (n, C, C), 2)
                              >= lax.broadcasted_iota(jnp.int32, (n, C, C), 1), 1.0, 0.0)
            dg = _tri_sum(upper, db) + extra
            df = dg / t["f"] - dk
            sg, sq = t["sg"], t["sq"]
            dq = da * t["eb"] * (HD ** -0.5) * (sq * (1.0 + qin * (1.0 - sq)))
            d_ref[0, :, cs] = dq.reshape(tr, HD).astype(BF16)
            d_ref[1, :, cs] = (df * (1.0 - lbh) * sg * (1.0 - sg)).reshape(tr, HD).astype(BF16)
            d_ref[2, :, cs] = dv.reshape(tr, HD).astype(BF16)
            dlb_ref[:, cs] += jnp.sum((df * (1.0 - sg)).reshape(tr, HD), axis=0, keepdims=True)

    rev = lambda off: pl.BlockSpec((tr, W), lambda g, s: (nrt - 1 - s, off + g))
    return pl.pallas_call(
        body, name="hgrn_bwd", grid=(HH // hb, nrt),
        in_specs=[rev(0), rev(0), rev(0), pl.BlockSpec((hb, n, HD, HD), lambda g, s: (g, nrt - 1 - s, 0, 0)),
                  rev(0), rev(0), rev(2 * EMIX // W), pl.BlockSpec((2, W), lambda g, s: (0, g))],
        out_specs=[pl.BlockSpec((3, tr, W), lambda g, s: (0, nrt - 1 - s, g)), pl.BlockSpec((1, W), lambda g, s: (0, g))],
        out_shape=[jax.ShapeDtypeStruct((3, S, EMIX), BF16), jax.ShapeDtypeStruct((1, EMIX), F32)],
        scratch_shapes=[pltpu.VMEM((hb, HD, HD), F32), pltpu.VMEM((hb, n, HD, HD), F32)],
        compiler_params=_params("parallel", "arbitrary"),
    )(dpremix, premix, rstd, states, proj, fgate, proj, hgrn_lb)


EW_BLOCK_ELEMS = 512 * 1024


def _ew_tiles(R, C):
    tc = C if C <= 4096 else _tile(C, 2048)
    tr = _tile(R, 512)
    while tr * tc > EW_BLOCK_ELEMS and tr % 16 == 0:
        tr //= 2
    return tr, tc


def _add_halves(name, core_chip, grad, got):
    _, _, R, C = grad.shape
    tr, tc = _ew_tiles(R, C)

    def body(c_ref, a_ref, b_ref, o_ref, own_ref):
        r = (a_ref[...].astype(F32) + b_ref[...].astype(F32)).astype(BF16)
        o_ref[...] = r

        @pl.when(pl.program_id(2) == c_ref[1])
        def _():
            own_ref[...] = r

    blk = pl.BlockSpec((None, tr, tc), lambda i, j, s, c: (s, i, j))
    sds = jax.ShapeDtypeStruct(got.shape, BF16)
    return pl.pallas_call(
        body, name=name, out_shape=[sds, sds],
        grid_spec=pltpu.PrefetchScalarGridSpec(
            num_scalar_prefetch=1, grid=(R // tr, C // tc, N_CHIPS),
            in_specs=[pl.BlockSpec((None, None, tr, tc), lambda i, j, s, c: (s, c[0], i, j)), blk],
            out_specs=[blk, pl.BlockSpec((None, tr, tc), lambda i, j, s, c: (c[1], i, j))]),
        compiler_params=_params("parallel", "parallel", "arbitrary"),
    )(core_chip, grad, got)


def _adam_step(w, g, m, v):
    mn = ADAM_B1 * m + (1.0 - ADAM_B1) * g
    vn = ADAM_B2 * v + (1.0 - ADAM_B2) * (g * g)
    m_hat = mn / (1.0 - ADAM_B1 ** ADAM_STEP)
    v_hat = vn / (1.0 - ADAM_B2 ** ADAM_STEP)
    return -ADAM_LR * (m_hat / (jnp.sqrt(v_hat) + ADAM_EPS) + ADAM_WD * w), mn, vn


def _adamw(name, w, g, m, v):
    R, C = w.shape
    tr, tc = _ew_tiles(R, C)

    def body(w_ref, g_ref, m_ref, v_ref, d_ref, mo_ref, vo_ref):
        d_ref[...], mo_ref[...], vo_ref[...] = _adam_step(w_ref[...], g_ref[...], m_ref[...], v_ref[...])

    blk = pl.BlockSpec((tr, tc), lambda i, j: (i, j))
    sds = jax.ShapeDtypeStruct((R, C), F32)
    return pl.pallas_call(
        body, name=name, grid=(R // tr, C // tc), in_specs=[blk] * 4, out_specs=[blk] * 3, out_shape=[sds] * 3,
        compiler_params=_params("parallel", "parallel"),
    )(w, g, m, v)


def _adamw_layers(name, w, gs, m, v):
    L, R, C = w.shape
    tr, tc = _ew_tiles(R, C)

    def body(*refs):
        w_ref, m_ref, v_ref = refs[:3]
        g_refs = refs[3:3 + L]
        go_ref, d_ref, mo_ref, vo_ref = refs[3 + L:]
        layer = pl.program_id(0)
        g = g_refs[0][...]
        for n in range(1, L):
            g = jnp.where(layer == n, g_refs[n][...], g)
        go_ref[...] = g
        d_ref[...], mo_ref[...], vo_ref[...] = _adam_step(w_ref[...], g, m_ref[...], v_ref[...])

    blk = pl.BlockSpec((None, tr, tc), lambda l, i, j: (l, i, j))
    of_layer = lambda n: pl.BlockSpec((tr, tc), lambda l, i, j: (jnp.where(l == n, i, 0), jnp.where(l == n, j, 0)))
    sds = jax.ShapeDtypeStruct((L, R, C), F32)
    return pl.pallas_call(
        body, name=name, grid=(L, R // tr, C // tc), in_specs=[blk] * 3 + [of_layer(n) for n in range(L)],
        out_specs=[blk] * 4, out_shape=[sds] * 4, compiler_params=_params("parallel", "parallel", "parallel"),
    )(w, m, v, *gs)


def _pack_rows(name, vecs, W):
    nv = len(vecs)

    def body(*refs):
        o_ref = refs[nv]
        o_ref[...] = jnp.zeros_like(o_ref)
        for i in range(nv):
            o_ref[i:i + 1, 0:vecs[i].shape[1]] = jnp.sum(refs[i][...], axis=0, keepdims=True)

    vm = pl.BlockSpec(memory_space=pltpu.VMEM)
    return pl.pallas_call(
        body, name=name, in_specs=[vm] * nv, out_specs=vm, out_shape=jax.ShapeDtypeStruct((SMALL_ROWS, W), F32),
    )(*vecs)


def _small_sum(gathered, hgrn_lb, lb_row):
    _, T, W = gathered.shape

    def body(g_ref, lb_ref, o_ref):
        acc = g_ref[0]
        for dev in range(1, N_DEV):
            acc = acc + g_ref[dev]
        o_ref[0:T, :] = acc
        lb = _hgrn_lb(lb_ref)
        d1 = o_ref[lb_row:lb_row + 1, :] * (lb * (1.0 - lb))
        o_ref[T:2 * T, :] = jnp.zeros((T, W), F32)
        o_ref[T:T + 1, :] = -d1
        o_ref[T + 1:T + 2, :] = d1

    vm = pl.BlockSpec(memory_space=pltpu.VMEM)
    return pl.pallas_call(
        body, name="small_sum", in_specs=[vm, vm], out_specs=vm, out_shape=jax.ShapeDtypeStruct((2 * T, W), F32),
    )(gathered, hgrn_lb)


def _place():
    return lax.axis_index("x"), lax.axis_index("y"), lax.axis_index("c")


def _other_chips(x, y):
    return [(1 - x, y), (x, 1 - y), (1 - x, 1 - y)]


def _chunk_rows(rows, row_bytes):
    cr = rows
    while cr * row_bytes > STREAM_CHUNK_BYTES and cr % 32 == 0:
        cr //= 2
    return cr


def _stream(pairs, buf, sems, t, peer):
    lsem, ssem, rsem = sems
    n = len(pairs)
    loads, sent = [None] * n, [None] * n

    def load(k):
        slot = k % STREAM_SLOTS
        if k >= STREAM_SLOTS:
            sent[k - STREAM_SLOTS]()
        loads[k] = pltpu.make_async_copy(pairs[k][0], buf.at[slot], lsem.at[t, slot])
        loads[k].start()

    load(0)
    for k in range(n):
        slot = k % STREAM_SLOTS
        if k + 1 < n:
            load(k + 1)
        loads[k].wait()
        if peer is None:
            cp = pltpu.make_async_copy(buf.at[slot], pairs[k][1], ssem.at[t, slot])
            cp.start()
            sent[k] = cp.wait
        else:
            cp = pltpu.make_async_remote_copy(src_ref=buf.at[slot], dst_ref=pairs[k][1], send_sem=ssem.at[t, slot],
                                              recv_sem=rsem.at[t], device_id=peer, device_id_type=MESH)
            cp.start()
            sent[k] = cp.wait_send
    for k in range(max(0, n - STREAM_SLOTS), n):
        sent[k]()


def _stream_scratch(shapes):
    nt = len(shapes)
    return ([pltpu.VMEM((STREAM_SLOTS,) + s, d) for s, d in shapes]
            + [pltpu.SemaphoreType.DMA((nt, STREAM_SLOTS)), pltpu.SemaphoreType.DMA((nt, STREAM_SLOTS)),
               pltpu.SemaphoreType.DMA((nt,))])


def _exchange_halves(name, grads):
    nt = len(grads)
    hs = [g.shape[1] // 2 for g in grads]
    crs = [_chunk_rows(h, g.shape[2] * g.dtype.itemsize) for h, g in zip(hs, grads)]

    def body(*refs):
        ins, gots, bufs, sems = refs[:nt], refs[nt:2 * nt], refs[2 * nt:3 * nt], refs[3 * nt:]
        x, y, c = _place()
        sib = (x, y, 1 - c)
        for t in range(nt):
            h, cr = hs[t], crs[t]
            pairs = [(ins[t].at[b, pl.ds((1 - c) * h + r0, cr)], gots[t].at[b, pl.ds(r0, cr)])
                     for b in range(N_CHIPS) for r0 in range(0, h, cr)]
            _stream(pairs, bufs[t], sems, t, sib)
        for t in range(nt):
            pltpu.make_async_remote_copy(src_ref=gots[t], dst_ref=gots[t], send_sem=sems[1].at[t, 0],
                                         recv_sem=sems[2].at[t], device_id=sib, device_id_type=MESH).wait_recv()

    return pl.pallas_call(
        body, name=name, in_specs=[ANY] * nt, out_specs=[ANY] * nt,
        out_shape=[jax.ShapeDtypeStruct((N_CHIPS, h, g.shape[2]), g.dtype) for h, g in zip(hs, grads)],
        scratch_shapes=_stream_scratch([((cr, g.shape[2]), g.dtype) for cr, g in zip(crs, grads)]),
        compiler_params=pltpu.CompilerParams(vmem_limit_bytes=VMEM_LIMIT_BYTES),
    )(*grads)


def _scatter_plan(srcs, dsts):
    x, y, c = _place()
    me = 2 * x + y
    return [(srcs[t].at[2 * px + py], dsts[t].at[me], (px, py, c))
            for t in range(len(srcs)) for px, py in _other_chips(x, y)]


def _slot(dst, chip, r0, rows, cols):
    if len(dst.shape) == 3:
        return dst.at[chip, pl.ds(r0, rows)]
    return dst.at[pl.ds(r0, rows), pl.ds(pl.multiple_of(chip * cols, 128), cols)]


def _gather_plan(srcs, dsts):
    x, y, c = _place()
    me = 2 * x + y
    plan = []
    for t in range(len(srcs)):
        h, cols = srcs[t].shape[0] // 2, srcs[t].shape[1]
        plan += [(srcs[t].at[pl.ds(c * h, h)], _slot(dsts[t], me, c * h, h, cols), (px, py, c))
                 for px, py in _other_chips(x, y)]
    return plan


HBM_SPEC = pl.BlockSpec(memory_space=pltpu.HBM)
SEM_SPEC = pl.BlockSpec(memory_space=pltpu.SEMAPHORE)


def _split_start(name, srcs, dsts, plan, ncopies, after):
    bufs = [pltpu.with_memory_space_constraint(a, pltpu.HBM) for a in list(srcs) + list(dsts)]
    nb, ns = len(bufs), len(srcs)
    operands = bufs + ([after] if after is not None else [])

    def body(*refs):
        outs = refs[len(operands):]
        send, recv, token = outs[0], outs[1], outs[-1]
        for i, (src, dst, dev) in enumerate(plan(refs[:ns], refs[ns:nb])):
            pltpu.make_async_remote_copy(src_ref=src, dst_ref=dst, send_sem=send.at[i], recv_sem=recv.at[i],
                                         device_id=dev, device_id_type=MESH).start()
        token[...] = jnp.zeros_like(token)

    res = pl.pallas_call(
        body, name=name,
        out_shape=[pltpu.SemaphoreType.DMA((ncopies,)), pltpu.SemaphoreType.DMA((ncopies,))]
        + [pltpu.HBM(a.shape, a.dtype) for a in bufs] + [jax.ShapeDtypeStruct((8, 128), F32)],
        in_specs=[HBM_SPEC] * nb + [ANY] * (len(operands) - nb),
        out_specs=[SEM_SPEC, SEM_SPEC] + [HBM_SPEC] * nb + [pl.BlockSpec(memory_space=pltpu.VMEM)],
        input_output_aliases={i: 2 + i for i in range(nb)},
        compiler_params=pltpu.CompilerParams(has_side_effects=pltpu.SideEffectType.DATAFLOW_SIDE_EFFECTING),
    )(*operands)
    return res[:-1], res[-1]


def _split_wait(name, started, plan, ns, after):
    send, recv, bufs = started[0], started[1], list(started[2:])
    nb = len(bufs)

    def body(*refs):
        send_ref, recv_ref = refs[nb], refs[nb + 1]
        for i, (src, dst, dev) in enumerate(plan(refs[:ns], refs[ns:nb])):
            cp = pltpu.make_async_remote_copy(src_ref=src, dst_ref=dst, send_sem=send_ref.at[i], recv_sem=recv_ref.at[i],
                                              device_id=dev, device_id_type=MESH)
            cp.wait_send()
            cp.wait_recv()

    res = pl.pallas_call(
        body, name=name, out_shape=[pltpu.HBM(a.shape, a.dtype) for a in bufs],
        in_specs=[HBM_SPEC] * nb + [SEM_SPEC, SEM_SPEC, ANY], out_specs=[HBM_SPEC] * nb,
        input_output_aliases={i: i for i in range(nb)},
        compiler_params=pltpu.CompilerParams(has_side_effects=pltpu.SideEffectType.DATAFLOW_SIDE_EFFECTING),
    )(*bufs, send, recv, after)
    return res[:ns], res[ns:]


def _gather_finish(name, shards, stacks):
    nt = len(shards)
    hs = [s.shape[0] // 2 for s in shards]
    crs = [_chunk_rows(h, s.shape[1] * s.dtype.itemsize) for h, s in zip(hs, shards)]

    def body(*refs):
        ins, outs, bufs, sems = refs[:nt], refs[2 * nt:3 * nt], refs[3 * nt:4 * nt], refs[4 * nt:]
        x, y, c = _place()
        me = 2 * x + y
        sib = (x, y, 1 - c)
        for t in range(nt):
            h, cr, cols = hs[t], crs[t], shards[t].shape[1]
            passed = [_slot(outs[t], 2 * px + py, c * h + r0, cr, cols)
                      for px, py in _other_chips(x, y) for r0 in range(0, h, cr)]
            _stream([(r, r) for r in passed], bufs[t], sems, t, sib)
            own = [(ins[t].at[pl.ds(r0, cr)], _slot(outs[t], me, r0, cr, cols)) for r0 in range(0, 2 * h, cr)]
            _stream(own, bufs[t], sems, t, None)
        for t in range(nt):
            if len(stacks[t].shape) == 3:
                three = outs[t].at[pl.ds(0, 3), pl.ds(0, hs[t])]
            else:
                three = outs[t].at[pl.ds(0, hs[t]), pl.ds(0, 3 * shards[t].shape[1])]
            pltpu.make_async_remote_copy(src_ref=three, dst_ref=three, send_sem=sems[1].at[t, 0],
                                         recv_sem=sems[2].at[t], device_id=sib, device_id_type=MESH).wait_recv()

    return pl.pallas_call(
        body, name=name, in_specs=[ANY] * (2 * nt), out_specs=[ANY] * nt,
        out_shape=[jax.ShapeDtypeStruct(s.shape, s.dtype) for s in stacks],
        scratch_shapes=_stream_scratch([((cr, s.shape[1]), s.dtype) for cr, s in zip(crs, shards)]),
        input_output_aliases={nt + t: t for t in range(nt)},
        compiler_params=pltpu.CompilerParams(vmem_limit_bytes=VMEM_LIMIT_BYTES),
    )(*shards, *stacks)


def _sum_share(landed):
    nt = len(landed)
    hs = [a.shape[1] for a in landed]
    cs = [a.shape[2] for a in landed]
    crs = [_chunk_rows(h, 2 * c * 4) for h, c in zip(hs, cs)]
    shapes = sorted(set(zip(crs, cs)))
    which = [shapes.index(s) for s in zip(crs, cs)]

    def body(*refs):
        ins, outs = refs[:nt], refs[nt:2 * nt]
        inbufs, outbufs = refs[2 * nt:2 * nt + len(shapes)], refs[2 * nt + len(shapes):2 * nt + 2 * len(shapes)]
        lsem, ssem, osem, rsem = refs[2 * nt + 2 * len(shapes):]
        x, y, c = _place()
        sib = (x, y, 1 - c)
        for t in range(nt):
            cr, n, ib, ob = crs[t], hs[t] // crs[t], inbufs[which[t]], outbufs[which[t]]
            loads, gone = [None] * n, [None] * n

            def load(k):
                slot = k % SUM_SLOTS
                if k >= SUM_SLOTS:
                    for cp_wait in gone[k - SUM_SLOTS]:
                        cp_wait()
                loads[k] = pltpu.make_async_copy(ins[t].at[:, pl.ds(k * cr, cr)], ib.at[slot], lsem.at[t, slot])
                loads[k].start()

            load(0)
            for k in range(n):
                slot = k % SUM_SLOTS
                if k + 1 < n:
                    load(k + 1)
                loads[k].wait()
                acc = ib[slot, 0].astype(F32)
                for s in range(1, N_CHIPS):
                    acc = acc + ib[slot, s].astype(F32)
                ob[slot] = acc
                rows = outs[t].at[c, pl.ds(k * cr, cr)]
                away = pltpu.make_async_remote_copy(src_ref=ob.at[slot], dst_ref=rows, send_sem=ssem.at[t, slot],
                                                    recv_sem=rsem.at[t], device_id=sib, device_id_type=MESH)
                away.start()
                home = pltpu.make_async_copy(ob.at[slot], rows, osem.at[t, slot])
                home.start()
                gone[k] = (away.wait_send, home.wait)
            for k in range(max(0, n - SUM_SLOTS), n):
                for cp_wait in gone[k]:
                    cp_wait()
        for t in range(nt):
            other = outs[t].at[1 - c]
            pltpu.make_async_remote_copy(src_ref=other, dst_ref=other, send_sem=ssem.at[t, 0], recv_sem=rsem.at[t],
                                         device_id=sib, device_id_type=MESH).wait_recv()

    slot_sems = pltpu.SemaphoreType.DMA((nt, SUM_SLOTS))
    return pl.pallas_call(
        body, name="sum_share", in_specs=[ANY] * nt, out_specs=[ANY] * nt,
        out_shape=[jax.ShapeDtypeStruct((2, h, c), F32) for h, c in zip(hs, cs)],
        scratch_shapes=[pltpu.VMEM((SUM_SLOTS, N_CHIPS, cr, c), BF16) for cr, c in shapes]
        + [pltpu.VMEM((SUM_SLOTS, cr, c), F32) for cr, c in shapes]
        + [slot_sems, slot_sems, slot_sems, pltpu.SemaphoreType.DMA((nt,))],
        compiler_params=pltpu.CompilerParams(vmem_limit_bytes=VMEM_LIMIT_BYTES),
    )(*landed)


def _allgather_small(name, v):
    def body(v_ref, o_ref, send, recv, lsem):
        x, y, c = _place()
        me = 4 * x + 2 * y + c
        loc = pltpu.make_async_copy(v_ref, o_ref.at[me], lsem)
        loc.start()
        copies = []
        for k in range(1, N_DEV):
            px = 1 - x if k & 4 else x
            py = 1 - y if k & 2 else y
            pc = 1 - c if k & 1 else c
            cp = pltpu.make_async_remote_copy(
                src_ref=v_ref, dst_ref=o_ref.at[me], send_sem=send.at[k - 1], recv_sem=recv.at[k - 1],
                device_id=(px, py, pc), device_id_type=MESH)
            cp.start()
            copies.append(cp)
        for cp in copies:
            cp.wait()
        loc.wait()

    vm = pl.BlockSpec(memory_space=pltpu.VMEM)
    return pl.pallas_call(
        body, name=name, in_specs=[vm], out_specs=vm,
        out_shape=jax.ShapeDtypeStruct((N_DEV,) + v.shape, v.dtype),
        scratch_shapes=[pltpu.SemaphoreType.DMA((N_DEV - 1,))] * 2 + [pltpu.SemaphoreType.DMA],
    )(v)


def kernel(x, mem, norm_g, mem_norm_g, w_kv, w_out, pool_w_in, pool_w_grp, pool_scale, hgrn_w_in, hgrn_lb, hgrn_norm_g, final_g, loss_target, m_norm_g, m_mem_norm_g, m_w_kv, m_w_out, m_pool_w_in, m_pool_w_grp, m_pool_scale, m_hgrn_w_in, m_hgrn_lb, m_hgrn_norm_g, m_final_g, v_norm_g, v_mem_norm_g, v_w_kv, v_w_out, v_pool_w_in, v_pool_w_grp, v_pool_scale, v_hgrn_w_in, v_hgrn_lb, v_hgrn_norm_g, v_final_g):
    _, S, D = x.shape
    M = mem.shape[1]
    EB = 2 * D
    ECA = EB // 4
    EMIX = EB - ECA
    PG = EMIX // N_POOL_GROUPS
    NP0 = EMIX + ECA + EB
    NP1 = 3 * EMIX + ECA + EB
    SH0, SH1 = NP0 // N_CHIPS, NP1 // N_CHIPS
    DK, EK = D // N_CHIPS, EB // N_CHIPS
    TNP = 512 if all(v % 512 == 0 for v in (SH0, SH1, ECA, EMIX)) else 256
    TM = _tile(S, 1024)
    TMF = _tile(S, 2048)
    TD = _tile(D, 512)
    TDW = _tile(D, 1024)
    c0, c1 = SH0 // TNP, SH1 // TNP
    qt, et = EMIX // TNP, ECA // TNP
    chip = 2 * lax.axis_index("x") + lax.axis_index("y")

    xs, ms, tgt = x[0], mem[0], loss_target[0]

    flat = lambda w: w.reshape(-1, w.shape[-1])
    sds = jax.ShapeDtypeStruct
    stack_of = lambda s: lax.empty((N_CHIPS,) + s.shape, BF16)
    group_a = [flat(pool_w_in).astype(BF16)]
    gather_a, token = _split_start("gather_a_start", group_a, [stack_of(s) for s in group_a], _gather_plan, 3, None)
    t0 = token[0:1, 0:1]
    bf = lambda w: (w + t0).astype(BF16)
    group_b = [bf(w_kv[0]), bf(w_out[0]), bf(flat(pool_w_grp))]
    gather_b, token = _split_start("gather_b_start", group_b, [stack_of(s) for s in group_b], _gather_plan, 9, token)
    group_c = [bf(flat(hgrn_w_in))]
    gather_c, token = _split_start("gather_c_start", group_c, [lax.empty((D, NP1), BF16)], _gather_plan, 3, token)
    group_d = [bf(w_kv[1]), bf(w_out[1])]
    gather_d, token = _split_start("gather_d_start", group_d, [stack_of(s) for s in group_d], _gather_plan, 6, token)
    started = token[0:1, 0:1]

    tek, tew = _tile(EK, 512), _tile(EK, 1024)

    mem_n = _rms_fwd("rms_mem", ms, mem_norm_g.reshape(1, D) + started)
    h0 = _rms_fwd("rms0", xs, norm_g[0:1] + started)
    wpin, = _gather_finish("gather_a_finish", *_split_wait("gather_a_wait", gather_a, _gather_plan, 1, h0))

    tkw = _tile(2 * ECA, 1024)

    def kv_of(layer, wkv):
        return _matmul(
            f"kv{layer}", mem_n, wkv.reshape(D, 2 * ECA), grid=(1, 2 * ECA // tkw, 1),
            a_spec=pl.BlockSpec((M, D), lambda i, j, k: (0, 0)), b_spec=pl.BlockSpec((D, tkw), lambda i, j, k: (0, j)),
            out_shape=sds((M, 2 * ECA), BF16), out_spec=pl.BlockSpec((M, tkw), lambda i, j, k: (0, j)),
            acc_shape=(M, tkw), dims=NN)

    tko = _tile(EB, 2048)

    def out_proj(layer, branch, wout, resid):
        return _matmul(
            f"out_proj{layer}", branch, wout.reshape(EB, D), grid=(S // TM, D // TDW, EB // tko),
            a_spec=pl.BlockSpec((TM, tko), IK), b_spec=pl.BlockSpec((tko, TDW), KJ),
            out_shape=sds((S, D), F32), out_spec=pl.BlockSpec((TM, TDW), IJ),
            acc_shape=(TM, TDW), dims=NN, add=resid, add_spec=pl.BlockSpec((TM, TDW), IJ))

    ones_ca = jnp.ones((1, ECA), F32)

    proj0 = _matmul(
        "proj0", h0, wpin, grid=(S // TMF, NP0 // TNP, 1),
        a_spec=pl.BlockSpec((TMF, D), lambda i, j, k: (i, 0)),
        b_spec=pl.BlockSpec((None, D, TNP), lambda i, j, k: (j // c0, 0, j % c0)),
        out_shape=sds((S, NP0), BF16), out_spec=pl.BlockSpec((TMF, TNP), IJ),
        acc_shape=(TMF, TNP), dims=NN)
    pooled = _pool_fwd(proj0, S, EMIX)
    wkv0, wout0, g_grp = _gather_finish("gather_b_finish", *_split_wait("gather_b_wait", gather_b, _gather_plan, 3, pooled))
    wgrp = g_grp.reshape(N_CHIPS, N_POOL_GROUPS, PG // N_CHIPS, PG).transpose(1, 0, 2, 3).reshape(N_POOL_GROUPS, PG, PG)
    kv = [kv_of(0, wkv0), None]
    premix0 = _matmul(
        "pool_grp", pooled, wgrp, grid=(S // TM, N_POOL_GROUPS, 1),
        a_spec=pl.BlockSpec((TM, PG), lambda i, j, k: (i, j)),
        b_spec=pl.BlockSpec((None, PG, PG), lambda i, j, k: (j, 0, 0)),
        out_shape=sds((S, EB), BF16), out_spec=pl.BlockSpec((TM, PG), lambda i, j, k: (i, j)),
        acc_shape=(TM, PG), dims=NN)
    premix0 = _ca_fwd("ca_fwd0", proj0, EMIX // ECA, kv[0], premix0, S, ECA, EMIX)
    colscale0 = jnp.concatenate([pool_scale.reshape(1, EMIX), ones_ca], axis=1)
    gblk0 = (EMIX + ECA) // ECA
    branch0 = _gate_fwd("gate_fwd0", premix0, proj0, gblk0, colscale0, S, EB, ECA)
    x1 = out_proj(0, branch0, wout0, xs)

    whin, = _gather_finish("gather_c_finish", *_split_wait("gather_c_wait", gather_c, _gather_plan, 1, x1))
    h1 = _rms_fwd("rms1", x1, norm_g[1:2])

    def proj1_cols(name, ncols, col_of, out_cols, out_dtype, out_col_of):
        return _matmul(
            name, h1, whin, grid=(S // TMF, ncols, 1),
            a_spec=pl.BlockSpec((TMF, D), lambda i, j, k: (i, 0)),
            b_spec=pl.BlockSpec((D, TNP), lambda i, j, k: (0, col_of(j))),
            out_shape=sds((S, out_cols), out_dtype), out_spec=pl.BlockSpec((TMF, TNP), lambda i, j, k: (i, out_col_of(j))),
            acc_shape=(TMF, TNP), dims=NN)

    skip_f = lambda j: jnp.where(j < qt, j, j + qt)
    proj1 = proj1_cols("proj1", NP1 // TNP - qt, skip_f, NP1, BF16, skip_f)
    fgate = proj1_cols("proj1_f", qt, lambda j: j + qt, EMIX, F32, lambda j: j)
    premix1, rstd1, states = _hgrn_fwd(proj1, fgate, hgrn_lb, S, EMIX, EB)
    wkv1, wout1 = _gather_finish("gather_d_finish", *_split_wait("gather_d_wait", gather_d, _gather_plan, 2, rstd1))
    kv[1] = kv_of(1, wkv1)
    premix1 = _ca_fwd("ca_fwd1", proj1, 3 * EMIX // ECA, kv[1], premix1, S, ECA, EMIX)
    norm_tiles = _allgather_small("allgather_norm_g", jnp.pad(hgrn_norm_g, ((0, SMALL_ROWS - 1), (0, 0))))
    hg_norm = norm_tiles[0::2, 0, :].reshape(1, EMIX)
    colscale1 = jnp.concatenate([hg_norm, ones_ca], axis=1)
    gblk1 = (3 * EMIX + ECA) // ECA
    branch1 = _gate_fwd("gate_fwd1", premix1, proj1, gblk1, colscale1, S, EB, ECA)
    x2 = out_proj(1, branch1, wout1, x1)

    dx2, dx2b, d_final_g, loss_part = _loss_head(x2, final_g.reshape(1, D), tgt)

    def out_proj_bwd(layer, dxb, branch, wout, premix, proj, gblk, colscale, dshape, dblk):
        goff, doff = gblk * ECA // tek, dblk * ECA // tek
        dpremix, dgate, dcol = _matmul(
            f"dbranch{layer}", dxb, wout, grid=(S // TMF, EB // tek, 1),
            a_spec=pl.BlockSpec((TMF, D), lambda i, j, k: (i, 0)),
            b_spec=pl.BlockSpec((None, tek, D), lambda i, j, k: (j // (EK // tek), j % (EK // tek), 0)),
            extras=[(premix, pl.BlockSpec((TMF, tek), IJ)), (proj, pl.BlockSpec((TMF, tek), lambda i, j, k: (i, goff + j))),
                    (colscale, pl.BlockSpec((1, tek), lambda i, j, k: (0, j)))],
            epilogue=_gate_bwd_epilogue,
            out_shape=[sds((S, EB), BF16), sds(dshape, BF16), sds((S // TMF, 1, EB), F32)],
            out_spec=[pl.BlockSpec((TMF, tek), IJ), pl.BlockSpec((TMF, tek), lambda i, j, k: (i, doff + j)),
                      pl.BlockSpec((None, 1, tek), lambda i, j, k: (i, 0, j))],
            acc_shape=(TMF, tek), dims=NT)
        dw = _matmul(
            f"dwout{layer}", branch, dxb, grid=(EB // tew, D // TD, 1),
            a_spec=pl.BlockSpec((S, tew), lambda i, j, k: (0, i)), b_spec=pl.BlockSpec((S, TD), lambda i, j, k: (0, j)),
            out_shape=sds((N_CHIPS, EK, D), BF16),
            out_spec=pl.BlockSpec((None, tew, TD), lambda i, j, k: (i // (EK // tew), i % (EK // tew), j)),
            acc_shape=(tew, TD), dims=TN)
        return dpremix, dgate, dcol.reshape(S // TMF, EB), dw

    def kv_bwd(layer, dkv, wkv, dmem_add):
        dkvb = dkv.astype(BF16)
        dmem = _matmul(
            f"dmem{layer}", dkvb, wkv.reshape(D, 2 * ECA), grid=(1, D // TDW, 1),
            a_spec=pl.BlockSpec((M, 2 * ECA), lambda i, j, k: (0, 0)),
            b_spec=pl.BlockSpec((TDW, 2 * ECA), lambda i, j, k: (j, 0)),
            out_shape=sds((M, D), F32), out_spec=pl.BlockSpec((M, TDW), lambda i, j, k: (0, j)), acc_shape=(M, TDW),
            dims=NT, add=dmem_add, add_spec=pl.BlockSpec((M, TDW), lambda i, j, k: (0, j)))
        dw = _matmul(
            f"dwkv{layer}", mem_n, dkvb, grid=(D // TDW, 2 * ECA // tkw, 1),
            a_spec=pl.BlockSpec((M, TDW), lambda i, j, k: (0, i)), b_spec=pl.BlockSpec((M, tkw), lambda i, j, k: (0, j)),
            out_shape=sds((D, 2 * ECA), BF16), out_spec=pl.BlockSpec((TDW, tkw), IJ), acc_shape=(TDW, tkw), dims=TN)
        return dmem, dw.reshape(N_CHIPS, DK, 2 * ECA)

    dpremix1, drest1, dcol1, gw_out1 = out_proj_bwd(1, dx2b, branch1, wout1, premix1, proj1, gblk1, colscale1,
                                                    (S, ECA + EB), 1)
    drest1, dkv1 = _ca_bwd("ca_bwd1", dpremix1, proj1, 3 * EMIX // ECA, kv[1], drest1, 0, S, ECA, EMIX)
    dqfi, dlb = _hgrn_bwd(dpremix1, premix1, rstd1, states, proj1, fgate, hgrn_lb, S, EMIX)
    nq, nr = 3 * qt, (ECA + EB) // TNP
    tkh = _tile(EMIX, 1024) if (ECA + EB) % _tile(EMIX, 1024) == 0 else TNP
    kq = EMIX // tkh
    dh1 = _matmul(
        "dh1_qfi", dqfi, whin, grid=(S // TM, D // TDW, 3),
        a_spec=pl.BlockSpec((None, TM, EMIX), lambda i, j, k: (k, i, 0)),
        b_spec=pl.BlockSpec((TDW, EMIX), lambda i, j, k: (j, k)),
        out_shape=sds((S, D), F32), out_spec=pl.BlockSpec((TM, TDW), IJ), acc_shape=(TM, TDW), dims=NT)
    dh1 = _matmul(
        "dh1_rest", drest1, whin, grid=(S // TM, D // TDW, (ECA + EB) // tkh), a_spec=pl.BlockSpec((TM, tkh), IK),
        b_spec=pl.BlockSpec((TDW, tkh), lambda i, j, k: (j, k + 3 * kq)),
        out_shape=sds((S, D), F32), out_spec=pl.BlockSpec((TM, TDW), IJ), acc_shape=(TM, TDW), dims=NT,
        add=dh1, add_spec=pl.BlockSpec((TM, TDW), IJ))
    gw_hin = _matmul(
        "dwhin_qfi", h1, dqfi, grid=(D // TDW, nq, 1), a_spec=pl.BlockSpec((S, TDW), lambda i, j, k: (0, i)),
        b_spec=pl.BlockSpec((None, S, TNP), lambda i, j, k: (j // qt, 0, j % qt)),
        out_shape=sds((N_CHIPS, D, SH1), BF16), out_spec=pl.BlockSpec((None, TDW, TNP), lambda i, j, k: (j // c1, i, j % c1)),
        acc_shape=(TDW, TNP), dims=TN)
    gw_hin = _matmul(
        "dwhin_rest", h1, drest1, grid=(D // TDW, nr, 1), a_spec=pl.BlockSpec((S, TDW), lambda i, j, k: (0, i)),
        b_spec=pl.BlockSpec((S, TNP), lambda i, j, k: (0, j)), out_shape=sds((N_CHIPS, D, SH1), BF16),
        out_spec=pl.BlockSpec((None, TDW, TNP), lambda i, j, k: ((j + nq) // c1, i, (j + nq) % c1)),
        acc_shape=(TDW, TNP), dims=TN, alias=gw_hin)
    dmem, gw_kv1 = kv_bwd(1, dkv1, wkv1, None)

    core_chip = jnp.stack([lax.axis_index("c"), chip]).astype(jnp.int32)

    def reduce_in_chip(tag, stacks):
        got = _exchange_halves(f"exchange_halves{tag}", stacks)
        pairs = [_add_halves(f"add_halves{tag}_{t}", core_chip, g.reshape(N_CHIPS, 2, g.shape[1] // 2, g.shape[2]), r)
                 for t, (g, r) in enumerate(zip(stacks, got))]
        return [p for p, _ in pairs], [own for _, own in pairs]

    parts1, landed1 = reduce_in_chip(1, [gw_kv1, gw_out1, gw_hin])
    scatter1, token1 = _split_start("scatter1_start", parts1, landed1, _scatter_plan, 3 * len(parts1), None)
    dx1, dx1b, d_ng1 = _rms_bwd("rms_bwd1", dh1, x1, norm_g[1:2] + token1[0:1, 0:1], dx2)

    dpremix0, dproj0, dcol0, gw_out0 = out_proj_bwd(0, dx1b, branch0, wout0, premix0, proj0, gblk0, colscale0,
                                                    (S, NP0), gblk0)
    dproj0, dkv0 = _ca_bwd("ca_bwd0", dpremix0, proj0, EMIX // ECA, kv[0], dproj0, EMIX // ECA, S, ECA, EMIX)
    dmem, gw_kv0 = kv_bwd(0, dkv0, wkv0, dmem)
    parts_a, landed_a = reduce_in_chip("0a", [gw_kv0, gw_out0])
    scatter_a, token_a = _split_start("scatter0a_start", parts_a, landed_a, _scatter_plan, 3 * len(parts_a), None)
    dpooled = _matmul(
        "dpooled", dpremix0, wgrp, grid=(S // TM, N_POOL_GROUPS, 1), a_spec=pl.BlockSpec((TM, PG), IJ),
        b_spec=pl.BlockSpec((None, PG, PG), lambda i, j, k: (j, 0, 0)),
        out_shape=sds((S, EMIX), F32), out_spec=pl.BlockSpec((TM, PG), IJ), acc_shape=(TM, PG), dims=NT, after=token_a)
    dwgrp = _matmul(
        "dwgrp", pooled, dpremix0, grid=(N_POOL_GROUPS, 1, 1), a_spec=pl.BlockSpec((S, PG), lambda i, j, k: (0, i)),
        b_spec=pl.BlockSpec((S, PG), lambda i, j, k: (0, i)), out_shape=sds((N_POOL_GROUPS, PG, PG), F32),
        out_spec=pl.BlockSpec((None, PG, PG), lambda i, j, k: (i, 0, 0)), acc_shape=(PG, PG), dims=TN)
    dproj0 = _pool_bwd(dpooled, dproj0, S, EMIX)
    gw_pin = _matmul(
        "dwpin", h0, dproj0, grid=(D // TDW, NP0 // TNP, 1), a_spec=pl.BlockSpec((S, TDW), lambda i, j, k: (0, i)),
        b_spec=pl.BlockSpec((S, TNP), lambda i, j, k: (0, j)), out_shape=sds((N_CHIPS, D, SH0), BF16),
        out_spec=pl.BlockSpec((None, TDW, TNP), lambda i, j, k: (j // c0, i, j % c0)), acc_shape=(TDW, TNP), dims=TN)
    gw_grp = dwgrp.reshape(N_POOL_GROUPS, N_CHIPS, PG // N_CHIPS, PG).transpose(1, 0, 2, 3).reshape(N_CHIPS, PG, PG)
    parts_b, landed_b = reduce_in_chip("0b", [gw_pin, gw_grp.astype(BF16)])
    scatter_b, token_b = _split_start("scatter0b_start", parts_b, landed_b, _scatter_plan, 3 * len(parts_b), None)
    dh0 = _matmul(
        "dh0", dproj0, wpin, grid=(S // TM, D // TDW, N_CHIPS), a_spec=pl.BlockSpec((TM, SH0), IK),
        b_spec=pl.BlockSpec((None, TDW, SH0), lambda i, j, k: (k, j, 0)),
        out_shape=sds((S, D), F32), out_spec=pl.BlockSpec((TM, TDW), IJ), acc_shape=(TM, TDW), dims=NT, after=token_b)
    grad_x, _, d_ng0 = _rms_bwd("rms_bwd0", dh0, xs, norm_g[0:1], dx1)
    _, _, d_mng = _rms_bwd("rms_bwd_mem", dmem, ms, mem_norm_g.reshape(1, D), jnp.zeros_like(ms))

    _, landed1 = _split_wait("scatter1_wait", scatter1, _scatter_plan, len(parts1), grad_x)
    _, landed_a = _split_wait("scatter0a_wait", scatter_a, _scatter_plan, len(parts_a), grad_x)
    _, landed_b = _split_wait("scatter0b_wait", scatter_b, _scatter_plan, len(parts_b), grad_x)
    landed = [landed_a[0], landed1[0], landed_a[1], landed1[1], landed_b[0], landed_b[1], landed1[2]]
    fulls = _sum_share(landed)
    f2 = [f.reshape(-1, f.shape[-1]) for f in fulls]
    grads, deltas, new_m, new_v = {}, {}, {}, {}
    for n, w, mm, vv, gs in (("w_kv", w_kv, m_w_kv, v_w_kv, f2[0:2]), ("w_out", w_out, m_w_out, v_w_out, f2[2:4]),
                             ("pool_w_in", pool_w_in, m_pool_w_in, v_pool_w_in, f2[4:5]),
                             ("pool_w_grp", pool_w_grp, m_pool_w_grp, v_pool_w_grp, f2[5:6]),
                             ("hgrn_w_in", hgrn_w_in, m_hgrn_w_in, v_hgrn_w_in, f2[6:7])):
        as3d = lambda a: a.reshape((a.shape[0], -1, a.shape[-1]))
        outs = _adamw_layers(f"adamw_{n}", as3d(w), gs, as3d(mm), as3d(vv))
        grads[n], deltas[n], new_m[n], new_v[n] = [o.reshape(w.shape) for o in outs]

    Wd = EMIX
    partial = _pack_rows("pack_partials", [d_ng0, d_ng1, d_mng, dcol0[:, :EMIX], dlb, dcol1[:, :EMIX], d_final_g,
                                           loss_part], Wd)
    summed = _small_sum(_allgather_small("allgather_grads", partial), hgrn_lb, 4)
    row = lambda i, n=Wd: summed[i:i + 1, :n]
    nshard = EMIX // N_CHIPS
    g_hg_norm = lax.dynamic_slice_in_dim(row(5), chip * nshard, nshard, axis=1)
    small_names = ["norm_g0", "norm_g1", "mem_norm_g", "pool_scale", "hgrn_lb0", "hgrn_lb1", "hgrn_norm_g", "final_g"]
    small_w = [norm_g[0:1], norm_g[1:2], mem_norm_g.reshape(1, D), pool_scale, hgrn_lb[0:1], hgrn_lb[1:2], hgrn_norm_g,
               final_g.reshape(1, D)]
    small_m = [m_norm_g[0:1], m_norm_g[1:2], m_mem_norm_g.reshape(1, D), m_pool_scale, m_hgrn_lb[0:1], m_hgrn_lb[1:2],
               m_hgrn_norm_g, m_final_g.reshape(1, D)]
    small_v = [v_norm_g[0:1], v_norm_g[1:2], v_mem_norm_g.reshape(1, D), v_pool_scale, v_hgrn_lb[0:1], v_hgrn_lb[1:2],
               v_hgrn_norm_g, v_final_g.reshape(1, D)]
    g_pack = _pack_rows("pack_small_g", [row(0, D), row(1, D), row(2, D), row(3), row(8), row(9), g_hg_norm, row(6, D)], Wd)
    d_pack, m_pack, v_pack = _adamw("adamw_small", _pack_rows("pack_small_w", small_w, Wd), g_pack,
                                    _pack_rows("pack_small_m", small_m, Wd), _pack_rows("pack_small_v", small_v, Wd))
    widths = [v.shape[1] for v in small_w]
    rows = lambda p: {n: p[i, :widths[i]] for i, n in enumerate(small_names)}

    def assemble(r, out):
        out["norm_g"] = jnp.stack([r["norm_g0"], r["norm_g1"]])
        out["mem_norm_g"] = r["mem_norm_g"]
        out["pool_scale"] = r["pool_scale"].reshape(1, EMIX)
        out["hgrn_lb"] = jnp.stack([r["hgrn_lb0"], r["hgrn_lb1"]])
        out["hgrn_norm_g"] = r["hgrn_norm_g"].reshape(1, nshard)
        out["final_g"] = r["final_g"]

    assemble(rows(g_pack), grads)
    assemble(rows(d_pack), deltas)
    assemble(rows(m_pack), new_m)
    assemble(rows(v_pack), new_v)
    loss = summed[7, 0]

    order = ["norm_g", "mem_norm_g", "w_kv", "w_out", "pool_w_in", "pool_w_grp", "pool_scale", "hgrn_w_in", "hgrn_lb",
             "hgrn_norm_g", "final_g"]
    return (loss, grad_x.reshape(1, S, D), *[grads[n] for n in order], *[deltas[n] for n in order],
            *[new_m[n] for n in order], *[new_v[n] for n in order])
```

```python
import functools

import jax
import jax.numpy as jnp
from jax import lax
from jax.experimental import pallas as pl
from jax.experimental.pallas import tpu as pltpu

F32 = jnp.float32
BF16 = jnp.bfloat16
MESH = pl.DeviceIdType.MESH
ANY = pl.BlockSpec(memory_space=pl.ANY)

EPS = 1e-6
HG_HEAD_DIM = 128
HG_CHUNK = 64
CA_HEADS = 4
N_POOL_GROUPS = 4
POOL_HALO = 128
ADAM_LR = 0.001
ADAM_B1 = 0.9
ADAM_B2 = 0.999
ADAM_EPS = 1e-08
ADAM_WD = 0.01
ADAM_STEP = 10
N_CHIPS = 4
N_DEV = 8
VMEM_LIMIT_BYTES = 56 * 1024 * 1024
SMALL_ROWS = 8
STREAM_CHUNK_BYTES = 2 * 1024 * 1024
STREAM_SLOTS = 3
SUM_SLOTS = 2


def _params(*sem):
    return pltpu.CompilerParams(dimension_semantics=sem, vmem_limit_bytes=VMEM_LIMIT_BYTES)


def _tile(n, pref):
    t = pref
    while n % t:
        t //= 2
    return t


def _sigmoid(x):
    return 1.0 / (1.0 + jnp.exp(-x))


def _matmul(name, a, b, *, grid, a_spec, b_spec, out_shape, out_spec, acc_shape, dims,
            add=None, add_spec=None, alias=None, after=None, extras=(), epilogue=None):
    nk = grid[2]
    has_add = add is not None
    has_alias = alias is not None
    has_after = after is not None
    n_out = len(out_shape) if epilogue is not None else 1

    def body(*refs):
        a_ref, b_ref = refs[0], refs[1]
        pos = 2
        add_ref = None
        if has_add:
            add_ref = refs[pos]
            pos += 1
        extra_refs = refs[pos:pos + len(extras)]
        pos += len(extras) + has_alias + has_after
        o_refs = refs[pos:pos + n_out]
        prod = lax.dot_general(a_ref[...], b_ref[...], (dims, ((), ())), preferred_element_type=F32)

        def finish(r):
            if epilogue is not None:
                epilogue(r, extra_refs, o_refs)
                return
            if has_add:
                r = r + add_ref[...].astype(F32)
            o_refs[0][...] = r.astype(o_refs[0].dtype)

        if nk == 1:
            finish(prod)
            return
        acc_ref = refs[pos + n_out]
        k = pl.program_id(2)

        @pl.when(k == 0)
        def _():
            acc_ref[...] = prod

        @pl.when(k > 0)
        def _():
            acc_ref[...] += prod

        @pl.when(k == nk - 1)
        def _():
            finish(acc_ref[...])

    operands = [a, b]
    in_specs = [a_spec, b_spec]
    if has_add:
        operands.append(add)
        in_specs.append(add_spec)
    for arr, spec in extras:
        operands.append(arr)
        in_specs.append(spec)
    aliases = {}
    if has_alias:
        aliases = {len(operands): 0}
        operands.append(alias)
        in_specs.append(ANY)
    if has_after:
        operands.append(after)
        in_specs.append(ANY)
    return pl.pallas_call(
        body, name=name, grid=grid, in_specs=in_specs, out_specs=out_spec, out_shape=out_shape,
        scratch_shapes=[pltpu.VMEM(acc_shape, F32)] if nk > 1 else [], input_output_aliases=aliases,
        compiler_params=_params("parallel", "parallel", "arbitrary"),
    )(*operands)


IJ = lambda i, j, k: (i, j)
IK = lambda i, j, k: (i, k)
KJ = lambda i, j, k: (k, j)
KI = lambda i, j, k: (k, i)
NN = ((1,), (0,))
NT = ((1,), (1,))
TN = ((0,), (0,))


def _rms_fwd(name, x, g):
    R, D = x.shape
    tr = _tile(R, 256)

    def body(x_ref, g_ref, o_ref):
        xf = x_ref[...]
        r = lax.rsqrt(jnp.mean(xf * xf, axis=-1, keepdims=True) + EPS)
        o_ref[...] = (xf * r * g_ref[...]).astype(o_ref.dtype)

    return pl.pallas_call(
        body, name=name, grid=(R // tr,),
        in_specs=[pl.BlockSpec((tr, D), lambda i: (i, 0)), pl.BlockSpec((1, D), lambda i: (0, 0))],
        out_specs=pl.BlockSpec((tr, D), lambda i: (i, 0)),
        out_shape=jax.ShapeDtypeStruct((R, D), BF16), compiler_params=_params("parallel"),
    )(x, g)


def _rms_bwd(name, dh, x, g, dres):
    R, D = x.shape
    tr = _tile(R, 256)

    def body(dh_ref, x_ref, g_ref, dres_ref, dx_ref, dxb_ref, dg_ref):
        xf = x_ref[...]
        r = lax.rsqrt(jnp.mean(xf * xf, axis=-1, keepdims=True) + EPS)
        xn = xf * r
        d = dh_ref[...]
        dyg = d * g_ref[...]
        dx = r * (dyg - xn * jnp.mean(dyg * xn, axis=-1, keepdims=True)) + dres_ref[...]
        dx_ref[...] = dx
        dxb_ref[...] = dx.astype(BF16)

        @pl.when(pl.program_id(0) == 0)
        def _():
            dg_ref[...] = jnp.zeros_like(dg_ref)

        dg_ref[...] += jnp.sum(d * xn, axis=0, keepdims=True)

    row = pl.BlockSpec((tr, D), lambda i: (i, 0))
    vec = pl.BlockSpec((1, D), lambda i: (0, 0))
    return pl.pallas_call(
        body, name=name, grid=(R // tr,), in_specs=[row, row, vec, row], out_specs=[row, row, vec],
        out_shape=[jax.ShapeDtypeStruct((R, D), F32), jax.ShapeDtypeStruct((R, D), BF16),
                   jax.ShapeDtypeStruct((1, D), F32)],
        compiler_params=_params("arbitrary"),
    )(dh, x, g, dres)


def _loss_head(x2, g, target):
    R, D = x2.shape
    tr = _tile(R, 256)

    def body(x_ref, g_ref, t_ref, dx_ref, dxb_ref, dg_ref, loss_ref):
        xf = x_ref[...]
        gg = g_ref[...]
        r = lax.rsqrt(jnp.mean(xf * xf, axis=-1, keepdims=True) + EPS)
        xn = xf * r
        e = xn * gg - t_ref[...]
        part = 0.5 * jnp.sum(jnp.mean(e * e, axis=-1, keepdims=True), axis=0, keepdims=True)
        dy = e * (1.0 / D)
        dyg = dy * gg
        dx = r * (dyg - xn * jnp.mean(dyg * xn, axis=-1, keepdims=True))
        dx_ref[...] = dx
        dxb_ref[...] = dx.astype(BF16)

        @pl.when(pl.program_id(0) == 0)
        def _():
            dg_ref[...] = jnp.zeros_like(dg_ref)
            loss_ref[...] = jnp.zeros_like(loss_ref)

        dg_ref[...] += jnp.sum(dy * xn, axis=0, keepdims=True)
        loss_ref[...] += jnp.broadcast_to(part, loss_ref.shape)

    row = pl.BlockSpec((tr, D), lambda i: (i, 0))
    vec = pl.BlockSpec((1, D), lambda i: (0, 0))
    return pl.pallas_call(
        body, name="loss_head", grid=(R // tr,), in_specs=[row, vec, row],
        out_specs=[row, row, vec, pl.BlockSpec((1, 128), lambda i: (0, 0))],
        out_shape=[jax.ShapeDtypeStruct((R, D), F32), jax.ShapeDtypeStruct((R, D), BF16),
                   jax.ShapeDtypeStruct((1, D), F32), jax.ShapeDtypeStruct((1, 128), F32)],
        compiler_params=_params("arbitrary"),
    )(x2, g, target)


def _pool_band(tr, reverse, w):
    r = lax.broadcasted_iota(jnp.int32, (tr, tr + POOL_HALO), 0)
    c = lax.broadcasted_iota(jnp.int32, (tr, tr + POOL_HALO), 1)
    if reverse:
        inside = (c >= r) & (c < r + w)
    else:
        cc = c - POOL_HALO
        inside = (cc <= r) & (cc > r - w)
    return jnp.where(inside, 1.0, 0.0).astype(BF16)


def _pool_fwd(proj, S, EMIX):
    PG = EMIX // N_POOL_GROUPS
    cb = PG
    tr = _tile(S, 512)
    per_group = PG // cb

    def body(u_ref, o_ref, ext):
        i = pl.program_id(1)
        w = jnp.left_shift(2, pl.program_id(0) // per_group)

        @pl.when(i == 0)
        def _():
            ext[0:POOL_HALO, :] = jnp.zeros((POOL_HALO, cb), BF16)

        u = u_ref[...]
        ext[POOL_HALO:, :] = u
        win = jnp.dot(_pool_band(tr, False, w), ext[...], preferred_element_type=F32)
        pos = i * tr + lax.broadcasted_iota(jnp.int32, (tr, 1), 0)
        cnt = jnp.minimum(pos + 1, w).astype(F32)
        o_ref[...] = (win / cnt - u.astype(F32)).astype(BF16)
        ext[0:POOL_HALO, :] = u[tr - POOL_HALO:, :]

    return pl.pallas_call(
        body, name="pool_fwd", grid=(EMIX // cb, S // tr),
        in_specs=[pl.BlockSpec((tr, cb), lambda j, i: (i, j))],
        out_specs=pl.BlockSpec((tr, cb), lambda j, i: (i, j)),
        out_shape=jax.ShapeDtypeStruct((S, EMIX), BF16),
        scratch_shapes=[pltpu.VMEM((tr + POOL_HALO, cb), BF16)],
        compiler_params=_params("parallel", "arbitrary"),
    )(proj)


def _pool_bwd(dpooled, dproj, S, EMIX):
    PG = EMIX // N_POOL_GROUPS
    cb = PG
    tr = _tile(S, 512)
    per_group = PG // cb
    nrt = S // tr

    def body(d_ref, _, o_ref, ext):
        step = pl.program_id(1)
        i = nrt - 1 - step
        w = jnp.left_shift(2, pl.program_id(0) // per_group)

        @pl.when(step == 0)
        def _():
            ext[tr:, :] = jnp.zeros((POOL_HALO, cb), BF16)

        d = d_ref[...]
        pos = i * tr + lax.broadcasted_iota(jnp.int32, (tr, 1), 0)
        cnt = jnp.minimum(pos + 1, w).astype(F32)
        z = (d / cnt).astype(BF16)
        ext[0:tr, :] = z
        win = jnp.dot(_pool_band(tr, True, w), ext[...], preferred_element_type=F32)
        o_ref[...] = (win - d).astype(BF16)
        ext[tr:, :] = z[0:POOL_HALO, :]

    return pl.pallas_call(
        body, name="pool_bwd", grid=(EMIX // cb, nrt),
        in_specs=[pl.BlockSpec((tr, cb), lambda j, s: (nrt - 1 - s, j)), ANY],
        out_specs=pl.BlockSpec((tr, cb), lambda j, s: (nrt - 1 - s, j)),
        out_shape=jax.ShapeDtypeStruct(dproj.shape, dproj.dtype),
        scratch_shapes=[pltpu.VMEM((tr + POOL_HALO, cb), BF16)],
        input_output_aliases={1: 0},
        compiler_params=_params("parallel", "arbitrary"),
    )(dpooled, dproj)


def _ca_fwd(name, proj, qblk, kv, premix, S, ECA, EMIX):
    M = kv.shape[0]
    hd = ECA // CA_HEADS
    ts = _tile(S, 512)
    scale = hd ** -0.5

    def body(q_ref, kv_ref, _, o_ref):
        for h in range(CA_HEADS):
            q = q_ref[:, h * hd:(h + 1) * hd]
            k = kv_ref[:, h * hd:(h + 1) * hd]
            v = kv_ref[:, ECA + h * hd:ECA + (h + 1) * hd]
            s = lax.dot_general(q, k, (NT, ((), ())), preferred_element_type=F32) * scale
            s = s - jnp.max(s, axis=-1, keepdims=True)
            p = jnp.exp(s)
            p = p / jnp.sum(p, axis=-1, keepdims=True)
            o = jnp.dot(p.astype(BF16), v, preferred_element_type=F32)
            o_ref[:, h * hd:(h + 1) * hd] = o.astype(BF16)

    return pl.pallas_call(
        body, name=name, grid=(S // ts,),
        in_specs=[pl.BlockSpec((ts, ECA), lambda i: (i, qblk)), pl.BlockSpec((M, 2 * ECA), lambda i: (0, 0)), ANY],
        out_specs=pl.BlockSpec((ts, ECA), lambda i: (i, EMIX // ECA)),
        out_shape=jax.ShapeDtypeStruct(premix.shape, premix.dtype),
        input_output_aliases={2: 0}, compiler_params=_params("parallel"),
    )(proj, kv, premix)


def _ca_bwd(name, dpremix, proj, qblk, kv, dbuf, dblk, S, ECA, EMIX):
    M = kv.shape[0]
    hd = ECA // CA_HEADS
    ts = _tile(S, 512)
    scale = hd ** -0.5

    def body(do_ref, q_ref, kv_ref, _, dq_ref, dkv_ref):
        @pl.when(pl.program_id(0) == 0)
        def _():
            dkv_ref[...] = jnp.zeros_like(dkv_ref)

        for h in range(CA_HEADS):
            lo, hi = h * hd, (h + 1) * hd
            q = q_ref[:, lo:hi]
            k = kv_ref[:, lo:hi]
            v = kv_ref[:, ECA + lo:ECA + hi]
            do = do_ref[:, lo:hi]
            s = lax.dot_general(q, k, (NT, ((), ())), preferred_element_type=F32) * scale
            s = s - jnp.max(s, axis=-1, keepdims=True)
            p = jnp.exp(s)
            p = p / jnp.sum(p, axis=-1, keepdims=True)
            pb = p.astype(BF16)
            dkv_ref[:, ECA + lo:ECA + hi] += lax.dot_general(pb, do, (TN, ((), ())), preferred_element_type=F32)
            dp = lax.dot_general(do, v, (NT, ((), ())), preferred_element_type=F32)
            ds = (p * (dp - jnp.sum(p * dp, axis=-1, keepdims=True)) * scale).astype(BF16)
            dq_ref[:, lo:hi] = jnp.dot(ds, k, preferred_element_type=F32).astype(BF16)
            dkv_ref[:, lo:hi] += lax.dot_general(ds, q, (TN, ((), ())), preferred_element_type=F32)

    return pl.pallas_call(
        body, name=name, grid=(S // ts,),
        in_specs=[pl.BlockSpec((ts, ECA), lambda i: (i, EMIX // ECA)), pl.BlockSpec((ts, ECA), lambda i: (i, qblk)),
                  pl.BlockSpec((M, 2 * ECA), lambda i: (0, 0)), ANY],
        out_specs=[pl.BlockSpec((ts, ECA), lambda i: (i, dblk)), pl.BlockSpec((M, 2 * ECA), lambda i: (0, 0))],
        out_shape=[jax.ShapeDtypeStruct(dbuf.shape, dbuf.dtype), jax.ShapeDtypeStruct((M, 2 * ECA), F32)],
        input_output_aliases={3: 0}, compiler_params=_params("arbitrary"),
    )(dpremix, proj, kv, dbuf)


def _gate_fwd(name, premix, proj, gblk, colscale, S, EB, ECA):
    ts = _tile(S, 512)

    def body(p_ref, g_ref, c_ref, o_ref):
        g = g_ref[...].astype(F32)
        o_ref[...] = (p_ref[...].astype(F32) * c_ref[...] * (g * _sigmoid(g))).astype(BF16)

    return pl.pallas_call(
        body, name=name, grid=(S // ts, EB // ECA),
        in_specs=[pl.BlockSpec((ts, ECA), lambda i, j: (i, j)), pl.BlockSpec((ts, ECA), lambda i, j: (i, gblk + j)),
                  pl.BlockSpec((1, ECA), lambda i, j: (0, j))],
        out_specs=pl.BlockSpec((ts, ECA), lambda i, j: (i, j)),
        out_shape=jax.ShapeDtypeStruct((S, EB), BF16), compiler_params=_params("parallel", "parallel"),
    )(premix, proj, colscale)


def _gate_bwd_epilogue(db, extra_refs, out_refs):
    p_ref, g_ref, c_ref = extra_refs
    dp_ref, dg_ref, dc_ref = out_refs
    g = g_ref[...].astype(F32)
    sg = _sigmoid(g)
    si = g * sg
    c = c_ref[...]
    t = db * p_ref[...].astype(F32)
    dp_ref[...] = (db * si * c).astype(BF16)
    dg_ref[...] = (t * c * (sg * (1.0 + g * (1.0 - sg)))).astype(BF16)
    dc_ref[...] = jnp.sum(t * si, axis=0, keepdims=True)


def _hgrn_lb(lb_ref):
    l0 = lb_ref[0:1, :]
    l1 = lb_ref[1:2, :]
    mx = jnp.maximum(l0, l1)
    e0 = jnp.exp(l0 - mx)
    e1 = jnp.exp(l1 - mx)
    return e1 / (e0 + e1)


def _bdot(a, b, ca, cb):
    return lax.dot_general(a, b, (((ca,), (cb,)), ((0,), (0,))), preferred_element_type=F32)


def _tri_sum(tri, x):
    hi = x.astype(BF16)
    lo = (x - hi.astype(F32)).astype(BF16)
    tri = tri.astype(BF16)
    return _bdot(tri, hi, 2, 1) + _bdot(tri, lo, 2, 1)


def _hgrn_chunks(qin, fin, lbh, n):
    C = HG_CHUNK
    row = lax.broadcasted_iota(jnp.int32, (n, C, C), 1)
    col = lax.broadcasted_iota(jnp.int32, (n, C, C), 2)
    causal = row >= col
    sg = _sigmoid(fin)
    f = lbh + (1.0 - lbh) * sg
    k = 1.0 - f
    g = jnp.log(f)
    b = _tri_sum(jnp.where(causal, 1.0, 0.0), g)
    b_last = jnp.sum(g, axis=1, keepdims=True)
    eb = jnp.exp(b)
    einv = jnp.exp(-b)
    eend = jnp.exp(b_last - b)
    sq = _sigmoid(qin)
    a = qin * sq * (HG_HEAD_DIM ** -0.5) * eb
    bm = k * einv
    e = k * eend
    d = jnp.exp(b_last)
    p = jnp.where(causal, _bdot(a.astype(BF16), bm.astype(BF16), 2, 2), 0.0)
    return dict(causal=causal, sg=sg, f=f, eb=eb, einv=einv, eend=eend, sq=sq, a=a, bm=bm, e=e, d=d, p=p)


def _hgrn_fwd(proj, fgate, hgrn_lb, S, EMIX, EB):
    HD, C = HG_HEAD_DIM, HG_CHUNK
    HH = EMIX // HD
    hb = 6 if HH % 6 == 0 else 1
    W = hb * HD
    tr = _tile(S, 512)
    n = tr // C

    def body(q_ref, f_ref, i_ref, lb_ref, o_ref, rstd_ref, st_ref, state):
        @pl.when(pl.program_id(1) == 0)
        def _():
            state[...] = jnp.zeros_like(state)

        lb = _hgrn_lb(lb_ref)
        for h in range(hb):
            cs = slice(h * HD, (h + 1) * HD)
            qin = q_ref[:, cs].astype(F32).reshape(n, C, HD)
            fin = f_ref[:, cs].reshape(n, C, HD)
            v = i_ref[:, cs].reshape(n, C, HD)
            t = _hgrn_chunks(qin, fin, lb[:, cs], n)
            upd = _bdot(v, t["e"].astype(BF16), 1, 1)
            st = state[h]
            for c in range(n):
                st_ref[h, c] = st
                st = st * t["d"][c] + upd[c]
            state[h] = st
            o = _bdot(t["p"].astype(BF16), v, 2, 1) + _bdot(t["a"].astype(BF16), st_ref[h].astype(BF16), 2, 2)
            rstd = lax.rsqrt(jnp.mean(o * o, axis=-1, keepdims=True) + EPS)
            o_ref[:, cs] = (o * rstd).reshape(tr, HD).astype(BF16)
            rstd_ref[:, cs] = jnp.broadcast_to(rstd, (n, C, HD)).reshape(tr, HD)

    blk = lambda off: pl.BlockSpec((tr, W), lambda g, i: (i, off + g))
    return pl.pallas_call(
        body, name="hgrn_fwd", grid=(HH // hb, S // tr),
        in_specs=[blk(0), blk(0), blk(2 * EMIX // W), pl.BlockSpec((2, W), lambda g, i: (0, g))],
        out_specs=[blk(0), blk(0), pl.BlockSpec((hb, n, HD, HD), lambda g, i: (g, i, 0, 0))],
        out_shape=[jax.ShapeDtypeStruct((S, EB), BF16), jax.ShapeDtypeStruct((S, EMIX), F32),
                   jax.ShapeDtypeStruct((HH, S // C, HD, HD), F32)],
        scratch_shapes=[pltpu.VMEM((hb, HD, HD), F32)],
        compiler_params=_params("parallel", "arbitrary"),
    )(proj, fgate, proj, hgrn_lb)


def _hgrn_bwd(dpremix, premix, rstd, states, proj, fgate, hgrn_lb, S, EMIX):
    HD, C = HG_HEAD_DIM, HG_CHUNK
    HH = EMIX // HD
    hb = 6 if HH % 6 == 0 else 1
    W = hb * HD
    tr = _tile(S, 512)
    n = tr // C
    nrt = S // tr

    def body(do_ref, on_ref, rstd_ref, st_ref, q_ref, f_ref, i_ref, lb_ref, d_ref, dlb_ref, dstate, dsbuf):
        @pl.when(pl.program_id(1) == 0)
        def _():
            dstate[...] = jnp.zeros_like(dstate)
            dlb_ref[...] = jnp.zeros_like(dlb_ref)

        lb = _hgrn_lb(lb_ref)
        for h in range(hb):
            cs = slice(h * HD, (h + 1) * HD)
            qin = q_ref[:, cs].astype(F32).reshape(n, C, HD)
            fin = f_ref[:, cs].reshape(n, C, HD)
            v = i_ref[:, cs].reshape(n, C, HD)
            lbh = lb[:, cs]
            t = _hgrn_chunks(qin, fin, lbh, n)
            a, bm, e, d, p = t["a"], t["bm"], t["e"], t["d"], t["p"]
            ab, bmb, eb16 = a.astype(BF16), bm.astype(BF16), e.astype(BF16)
            on = on_ref[:, cs].astype(F32).reshape(n, C, HD)
            dn = do_ref[:, cs].astype(F32).reshape(n, C, HD)
            do = rstd_ref[:, cs].reshape(n, C, HD) * (dn - on * jnp.mean(dn * on, axis=-1, keepdims=True))
            dob = do.astype(BF16)
            grow = _bdot(dob, ab, 1, 1)
            ds = dstate[h]
            for c in reversed(range(n)):
                dsbuf[h, c] = ds
                ds = ds * d[c] + grow[c]
            dstate[h] = ds
            dst = dsbuf[h]
            st = st_ref[h]
            dstb = dst.astype(BF16)
            dp = jnp.where(t["causal"], _bdot(dob, v, 2, 2), 0.0).astype(BF16)
            dv = _bdot(p.astype(BF16), dob, 1, 1) + _bdot(eb16, dstb, 2, 2)
            da = _bdot(dp, bmb, 2, 1) + _bdot(dob, st.astype(BF16), 2, 1)
            dbm = _bdot(dp, ab, 1, 1)
            de = _bdot(v, dstb, 2, 1)
            dd = jnp.sum(dst * st, axis=1, keepdims=True)
            dk = dbm * t["einv"] + de * t["eend"]
            dee = de * e
            db = da * a - dbm * bm - dee
            extra = jnp.sum(dee, axis=1, keepdims=True) + dd * d
            upper = jnp.where(lax.broadcasted_iota(jnp.int32, (n, C, C), 2)
                              >= lax.broadcasted_iota(jnp.int32, (n, C, C), 1), 1.0, 0.0)
            dg = _tri_sum(upper, db) + extra
            df = dg / t["f"] - dk
            sg, sq = t["sg"], t["sq"]
            dq = da * t["eb"] * (HD ** -0.5) * (sq * (1.0 + qin * (1.0 - sq)))
            d_ref[0, :, cs] = dq.reshape(tr, HD).astype(BF16)
            d_ref[1, :, cs] = (df * (1.0 - lbh) * sg * (1.0 - sg)).reshape(tr, HD).astype(BF16)
            d_ref[2, :, cs] = dv.reshape(tr, HD).astype(BF16)
            dlb_ref[:, cs] += jnp.sum((df * (1.0 - sg)).reshape(tr, HD), axis=0, keepdims=True)

    rev = lambda off: pl.BlockSpec((tr, W), lambda g, s: (nrt - 1 - s, off + g))
    return pl.pallas_call(
        body, name="hgrn_bwd", grid=(HH // hb, nrt),
        in_specs=[rev(0), rev(0), rev(0), pl.BlockSpec((hb, n, HD, HD), lambda g, s: (g, nrt - 1 - s, 0, 0)),
                  rev(0), rev(0), rev(2 * EMIX // W), pl.BlockSpec((2, W), lambda g, s: (0, g))],
        out_specs=[pl.BlockSpec((3, tr, W), lambda g, s: (0, nrt - 1 - s, g)), pl.BlockSpec((1, W), lambda g, s: (0, g))],
        out_shape=[jax.ShapeDtypeStruct((3, S, EMIX), BF16), jax.ShapeDtypeStruct((1, EMIX), F32)],
        scratch_shapes=[pltpu.VMEM((hb, HD, HD), F32), pltpu.VMEM((hb, n, HD, HD), F32)],
        compiler_params=_params("parallel", "arbitrary"),
    )(dpremix, premix, rstd, states, proj, fgate, proj, hgrn_lb)


EW_BLOCK_ELEMS = 512 * 1024


def _ew_tiles(R, C):
    tc = C if C <= 4096 else _tile(C, 2048)
    tr = _tile(R, 512)
    while tr * tc > EW_BLOCK_ELEMS and tr % 16 == 0:
        tr //= 2
    return tr, tc


def _add_halves(name, core_chip, grad, got):
    _, _, R, C = grad.shape
    tr, tc = _ew_tiles(R, C)

    def body(c_ref, a_ref, b_ref, o_ref, own_ref):
        r = (a_ref[...].astype(F32) + b_ref[...].astype(F32)).astype(BF16)
        o_ref[...] = r

        @pl.when(pl.program_id(2) == c_ref[1])
        def _():
            own_ref[...] = r

    blk = pl.BlockSpec((None, tr, tc), lambda i, j, s, c: (s, i, j))
    sds = jax.ShapeDtypeStruct(got.shape, BF16)
    return pl.pallas_call(
        body, name=name, out_shape=[sds, sds],
        grid_spec=pltpu.PrefetchScalarGridSpec(
            num_scalar_prefetch=1, grid=(R // tr, C // tc, N_CHIPS),
            in_specs=[pl.BlockSpec((None, None, tr, tc), lambda i, j, s, c: (s, c[0], i, j)), blk],
            out_specs=[blk, pl.BlockSpec((None, tr, tc), lambda i, j, s, c: (c[1], i, j))]),
        compiler_params=_params("parallel", "parallel", "arbitrary"),
    )(core_chip, grad, got)


def _adam_step(w, g, m, v):
    mn = ADAM_B1 * m + (1.0 - ADAM_B1) * g
    vn = ADAM_B2 * v + (1.0 - ADAM_B2) * (g * g)
    m_hat = mn / (1.0 - ADAM_B1 ** ADAM_STEP)
    v_hat = vn / (1.0 - ADAM_B2 ** ADAM_STEP)
    return -ADAM_LR * (m_hat / (jnp.sqrt(v_hat) + ADAM_EPS) + ADAM_WD * w), mn, vn


def _adamw(name, w, g, m, v):
    R, C = w.shape
    tr, tc = _ew_tiles(R, C)

    def body(w_ref, g_ref, m_ref, v_ref, d_ref, mo_ref, vo_ref):
        d_ref[...], mo_ref[...], vo_ref[...] = _adam_step(w_ref[...], g_ref[...], m_ref[...], v_ref[...])

    blk = pl.BlockSpec((tr, tc), lambda i, j: (i, j))
    sds = jax.ShapeDtypeStruct((R, C), F32)
    return pl.pallas_call(
        body, name=name, grid=(R // tr, C // tc), in_specs=[blk] * 4, out_specs=[blk] * 3, out_shape=[sds] * 3,
        compiler_params=_params("parallel", "parallel"),
    )(w, g, m, v)


def _adamw_layers(name, w, gs, m, v):
    L, R, C = w.shape
    tr, tc = _ew_tiles(R, C)

    def body(*refs):
        w_ref, m_ref, v_ref = refs[:3]
        g_refs = refs[3:3 + L]
        go_ref, d_ref, mo_ref, vo_ref = refs[3 + L:]
        layer = pl.program_id(0)
        g = g_refs[0][...]
        for n in range(1, L):
            g = jnp.where(layer == n, g_refs[n][...], g)
        go_ref[...] = g
        d_ref[...], mo_ref[...], vo_ref[...] = _adam_step(w_ref[...], g, m_ref[...], v_ref[...])

    blk = pl.BlockSpec((None, tr, tc), lambda l, i, j: (l, i, j))
    of_layer = lambda n: pl.BlockSpec((tr, tc), lambda l, i, j: (jnp.where(l == n, i, 0), jnp.where(l == n, j, 0)))
    sds = jax.ShapeDtypeStruct((L, R, C), F32)
    return pl.pallas_call(
        body, name=name, grid=(L, R // tr, C // tc), in_specs=[blk] * 3 + [of_layer(n) for n in range(L)],
        out_specs=[blk] * 4, out_shape=[sds] * 4, compiler_params=_params("parallel", "parallel", "parallel"),
    )(w, m, v, *gs)


def _pack_rows(name, vecs, W):
    nv = len(vecs)

    def body(*refs):
        o_ref = refs[nv]
        o_ref[...] = jnp.zeros_like(o_ref)
        for i in range(nv):
            o_ref[i:i + 1, 0:vecs[i].shape[1]] = jnp.sum(refs[i][...], axis=0, keepdims=True)

    vm = pl.BlockSpec(memory_space=pltpu.VMEM)
    return pl.pallas_call(
        body, name=name, in_specs=[vm] * nv, out_specs=vm, out_shape=jax.ShapeDtypeStruct((SMALL_ROWS, W), F32),
    )(*vecs)


def _small_sum(gathered, hgrn_lb, lb_row):
    _, T, W = gathered.shape

    def body(g_ref, lb_ref, o_ref):
        acc = g_ref[0]
        for dev in range(1, N_DEV):
            acc = acc + g_ref[dev]
        o_ref[0:T, :] = acc
        lb = _hgrn_lb(lb_ref)
        d1 = o_ref[lb_row:lb_row + 1, :] * (lb * (1.0 - lb))
        o_ref[T:2 * T, :] = jnp.zeros((T, W), F32)
        o_ref[T:T + 1, :] = -d1
        o_ref[T + 1:T + 2, :] = d1

    vm = pl.BlockSpec(memory_space=pltpu.VMEM)
    return pl.pallas_call(
        body, name="small_sum", in_specs=[vm, vm], out_specs=vm, out_shape=jax.ShapeDtypeStruct((2 * T, W), F32),
    )(gathered, hgrn_lb)


def _place():
    return lax.axis_index("x"), lax.axis_index("y"), lax.axis_index("c")


def _other_chips(x, y):
    return [(1 - x, y), (x, 1 - y), (1 - x, 1 - y)]


def _chunk_rows(rows, row_bytes):
    cr = rows
    while cr * row_bytes > STREAM_CHUNK_BYTES and cr % 32 == 0:
        cr //= 2
    return cr


def _stream(pairs, buf, sems, t, peer):
    lsem, ssem, rsem = sems
    n = len(pairs)
    loads, sent = [None] * n, [None] * n

    def load(k):
        slot = k % STREAM_SLOTS
        if k >= STREAM_SLOTS:
            sent[k - STREAM_SLOTS]()
        loads[k] = pltpu.make_async_copy(pairs[k][0], buf.at[slot], lsem.at[t, slot])
        loads[k].start()

    load(0)
    for k in range(n):
        slot = k % STREAM_SLOTS
        if k + 1 < n:
            load(k + 1)
        loads[k].wait()
        cp = pltpu.make_async_remote_copy(src_ref=buf.at[slot], dst_ref=pairs[k][1], send_sem=ssem.at[t, slot],
                                          recv_sem=rsem.at[t], device_id=peer, device_id_type=MESH)
        cp.start()
        sent[k] = cp.wait_send
    for k in range(max(0, n - STREAM_SLOTS), n):
        sent[k]()


def _stream_scratch(shapes):
    nt = len(shapes)
    return ([pltpu.VMEM((STREAM_SLOTS,) + s, d) for s, d in shapes]
            + [pltpu.SemaphoreType.DMA((nt, STREAM_SLOTS)), pltpu.SemaphoreType.DMA((nt, STREAM_SLOTS)),
               pltpu.SemaphoreType.DMA((nt,))])


def _exchange_halves(name, grads):
    nt = len(grads)
    hs = [g.shape[1] // 2 for g in grads]
    crs = [_chunk_rows(h, g.shape[2] * g.dtype.itemsize) for h, g in zip(hs, grads)]

    def body(*refs):
        ins, gots, bufs, sems = refs[:nt], refs[nt:2 * nt], refs[2 * nt:3 * nt], refs[3 * nt:]
        x, y, c = _place()
        sib = (x, y, 1 - c)
        for t in range(nt):
            h, cr = hs[t], crs[t]
            pairs = [(ins[t].at[b, pl.ds((1 - c) * h + r0, cr)], gots[t].at[b, pl.ds(r0, cr)])
                     for b in range(N_CHIPS) for r0 in range(0, h, cr)]
            _stream(pairs, bufs[t], sems, t, sib)
        for t in range(nt):
            pltpu.make_async_remote_copy(src_ref=gots[t], dst_ref=gots[t], send_sem=sems[1].at[t, 0],
                                         recv_sem=sems[2].at[t], device_id=sib, device_id_type=MESH).wait_recv()

    return pl.pallas_call(
        body, name=name, in_specs=[ANY] * nt, out_specs=[ANY] * nt,
        out_shape=[jax.ShapeDtypeStruct((N_CHIPS, h, g.shape[2]), g.dtype) for h, g in zip(hs, grads)],
        scratch_shapes=_stream_scratch([((cr, g.shape[2]), g.dtype) for cr, g in zip(crs, grads)]),
        compiler_params=pltpu.CompilerParams(vmem_limit_bytes=VMEM_LIMIT_BYTES),
    )(*grads)


def _scatter_plan(srcs, dsts):
    x, y, c = _place()
    me = 2 * x + y
    return [(srcs[t].at[2 * px + py], dsts[t].at[me], (px, py, c))
            for t in range(len(srcs)) for px, py in _other_chips(x, y)]


def _slot(dst, chip, r0, rows, cols):
    if len(dst.shape) == 3:
        return dst.at[chip, pl.ds(r0, rows)]
    return dst.at[pl.ds(r0, rows), pl.ds(pl.multiple_of(chip * cols, 128), cols)]


def _shard_dims(gathered):
    s = gathered.shape
    return (s[1], s[2]) if len(s) == 3 else (s[0], s[1] // N_CHIPS)


def _gather_plan(_, bufs):
    x, y, c = _place()
    me = 2 * x + y
    plan = []
    for buf in bufs:
        rows, cols = _shard_dims(buf)
        mine = _slot(buf, me, c * (rows // 2), rows // 2, cols)
        plan += [(mine, mine, (px, py, c)) for px, py in _other_chips(x, y)]
    return plan


def _cast_to_slot(name, chip, w, gathered_shape, after):
    R, C = w.shape
    tr, tc = _ew_tiles(R, C)

    def body(c_ref, w_ref, *rest):
        rest[-1][...] = w_ref[...].astype(BF16)

    if len(gathered_shape) == 3:
        out_spec = pl.BlockSpec((None, tr, tc), lambda i, j, c: (c[0], i, j))
    else:
        out_spec = pl.BlockSpec((tr, tc), lambda i, j, c: (i, c[0] * (C // tc) + j))
    extra = [] if after is None else [after]
    return pl.pallas_call(
        body, name=name, out_shape=jax.ShapeDtypeStruct(gathered_shape, BF16),
        grid_spec=pltpu.PrefetchScalarGridSpec(
            num_scalar_prefetch=1, grid=(R // tr, C // tc),
            in_specs=[pl.BlockSpec((tr, tc), lambda i, j, c: (i, j))] + [ANY] * len(extra), out_specs=out_spec),
        compiler_params=_params("parallel", "parallel"),
    )(chip, w, *extra)


HBM_SPEC = pl.BlockSpec(memory_space=pltpu.HBM)
SEM_SPEC = pl.BlockSpec(memory_space=pltpu.SEMAPHORE)


def _split_start(name, srcs, dsts, plan, ncopies, after):
    bufs = [pltpu.with_memory_space_constraint(a, pltpu.HBM) for a in list(srcs) + list(dsts)]
    nb, ns = len(bufs), len(srcs)
    operands = bufs + ([after] if after is not None else [])

    def body(*refs):
        outs = refs[len(operands):]
        send, recv, token = outs[0], outs[1], outs[-1]
        for i, (src, dst, dev) in enumerate(plan(refs[:ns], refs[ns:nb])):
            pltpu.make_async_remote_copy(src_ref=src, dst_ref=dst, send_sem=send.at[i], recv_sem=recv.at[i],
                                         device_id=dev, device_id_type=MESH).start()
        token[...] = jnp.zeros_like(token)

    res = pl.pallas_call(
        body, name=name,
        out_shape=[pltpu.SemaphoreType.DMA((ncopies,)), pltpu.SemaphoreType.DMA((ncopies,))]
        + [pltpu.HBM(a.shape, a.dtype) for a in bufs] + [jax.ShapeDtypeStruct((8, 128), F32)],
        in_specs=[HBM_SPEC] * nb + [ANY] * (len(operands) - nb),
        out_specs=[SEM_SPEC, SEM_SPEC] + [HBM_SPEC] * nb + [pl.BlockSpec(memory_space=pltpu.VMEM)],
        input_output_aliases={i: 2 + i for i in range(nb)},
        compiler_params=pltpu.CompilerParams(has_side_effects=pltpu.SideEffectType.DATAFLOW_SIDE_EFFECTING),
    )(*operands)
    return res[:-1], res[-1]


def _split_wait(name, started, plan, ns, after):
    send, recv, bufs = started[0], started[1], list(started[2:])
    nb = len(bufs)

    def body(*refs):
        send_ref, recv_ref = refs[nb], refs[nb + 1]
        for i, (src, dst, dev) in enumerate(plan(refs[:ns], refs[ns:nb])):
            cp = pltpu.make_async_remote_copy(src_ref=src, dst_ref=dst, send_sem=send_ref.at[i], recv_sem=recv_ref.at[i],
                                              device_id=dev, device_id_type=MESH)
            cp.wait_send()
            cp.wait_recv()

    res = pl.pallas_call(
        body, name=name, out_shape=[pltpu.HBM(a.shape, a.dtype) for a in bufs],
        in_specs=[HBM_SPEC] * nb + [SEM_SPEC, SEM_SPEC, ANY], out_specs=[HBM_SPEC] * nb,
        input_output_aliases={i: i for i in range(nb)},
        compiler_params=pltpu.CompilerParams(has_side_effects=pltpu.SideEffectType.DATAFLOW_SIDE_EFFECTING),
    )(*bufs, send, recv, after)
    return res[:ns], res[ns:]


def _gather_finish(name, _, gathered):
    nt = len(gathered)
    dims = [_shard_dims(g) for g in gathered]
    hs = [rows // 2 for rows, _ in dims]
    crs = [_chunk_rows(h, cols * 2) for h, (_, cols) in zip(hs, dims)]

    def body(*refs):
        outs, bufs, sems = refs[nt:2 * nt], refs[2 * nt:3 * nt], refs[3 * nt:]
        x, y, c = _place()
        sib = (x, y, 1 - c)
        for t in range(nt):
            h, cr, cols = hs[t], crs[t], dims[t][1]
            passed = [_slot(outs[t], 2 * px + py, c * h + r0, cr, cols)
                      for px, py in _other_chips(x, y) for r0 in range(0, h, cr)]
            _stream([(r, r) for r in passed], bufs[t], sems, t, sib)
        for t in range(nt):
            if len(gathered[t].shape) == 3:
                three = outs[t].at[pl.ds(0, 3), pl.ds(0, hs[t])]
            else:
                three = outs[t].at[pl.ds(0, hs[t]), pl.ds(0, 3 * dims[t][1])]
            pltpu.make_async_remote_copy(src_ref=three, dst_ref=three, send_sem=sems[1].at[t, 0],
                                         recv_sem=sems[2].at[t], device_id=sib, device_id_type=MESH).wait_recv()

    return pl.pallas_call(
        body, name=name, in_specs=[ANY] * nt, out_specs=[ANY] * nt,
        out_shape=[jax.ShapeDtypeStruct(g.shape, g.dtype) for g in gathered],
        scratch_shapes=_stream_scratch([((cr, cols), BF16) for cr, (_, cols) in zip(crs, dims)]),
        input_output_aliases={t: t for t in range(nt)},
        compiler_params=pltpu.CompilerParams(vmem_limit_bytes=VMEM_LIMIT_BYTES),
    )(*gathered)


def _sum_share(landed):
    nt = len(landed)
    hs = [a.shape[1] for a in landed]
    cs = [a.shape[2] for a in landed]
    crs = [_chunk_rows(h, 2 * c * 4) for h, c in zip(hs, cs)]
    shapes = sorted(set(zip(crs, cs)))
    which = [shapes.index(s) for s in zip(crs, cs)]

    def body(*refs):
        ins, outs = refs[:nt], refs[nt:2 * nt]
        inbufs, outbufs = refs[2 * nt:2 * nt + len(shapes)], refs[2 * nt + len(shapes):2 * nt + 2 * len(shapes)]
        lsem, ssem, osem, rsem = refs[2 * nt + 2 * len(shapes):]
        x, y, c = _place()
        sib = (x, y, 1 - c)
        for t in range(nt):
            cr, n, ib, ob = crs[t], hs[t] // crs[t], inbufs[which[t]], outbufs[which[t]]
            loads, gone = [None] * n, [None] * n

            def load(k):
                slot = k % SUM_SLOTS
                if k >= SUM_SLOTS:
                    for cp_wait in gone[k - SUM_SLOTS]:
                        cp_wait()
                loads[k] = pltpu.make_async_copy(ins[t].at[:, pl.ds(k * cr, cr)], ib.at[slot], lsem.at[t, slot])
                loads[k].start()

            load(0)
            for k in range(n):
                slot = k % SUM_SLOTS
                if k + 1 < n:
                    load(k + 1)
                loads[k].wait()
                acc = ib[slot, 0].astype(F32)
                for s in range(1, N_CHIPS):
                    acc = acc + ib[slot, s].astype(F32)
                ob[slot] = acc
                rows = outs[t].at[c, pl.ds(k * cr, cr)]
                away = pltpu.make_async_remote_copy(src_ref=ob.at[slot], dst_ref=rows, send_sem=ssem.at[t, slot],
                                                    recv_sem=rsem.at[t], device_id=sib, device_id_type=MESH)
                away.start()
                home = pltpu.make_async_copy(ob.at[slot], rows, osem.at[t, slot])
                home.start()
                gone[k] = (away.wait_send, home.wait)
            for k in range(max(0, n - SUM_SLOTS), n):
                for cp_wait in gone[k]:
                    cp_wait()
        for t in range(nt):
            other = outs[t].at[1 - c]
            pltpu.make_async_remote_copy(src_ref=other, dst_ref=other, send_sem=ssem.at[t, 0], recv_sem=rsem.at[t],
                                         device_id=sib, device_id_type=MESH).wait_recv()

    slot_sems = pltpu.SemaphoreType.DMA((nt, SUM_SLOTS))
    return pl.pallas_call(
        body, name="sum_share", in_specs=[ANY] * nt, out_specs=[ANY] * nt,
        out_shape=[jax.ShapeDtypeStruct((2, h, c), F32) for h, c in zip(hs, cs)],
        scratch_shapes=[pltpu.VMEM((SUM_SLOTS, N_CHIPS, cr, c), BF16) for cr, c in shapes]
        + [pltpu.VMEM((SUM_SLOTS, cr, c), F32) for cr, c in shapes]
        + [slot_sems, slot_sems, slot_sems, pltpu.SemaphoreType.DMA((nt,))],
        compiler_params=pltpu.CompilerParams(vmem_limit_bytes=VMEM_LIMIT_BYTES),
    )(*landed)


def _allgather_small(name, v):
    def body(v_ref, o_ref, send, recv, lsem):
        x, y, c = _place()
        me = 4 * x + 2 * y + c
        loc = pltpu.make_async_copy(v_ref, o_ref.at[me], lsem)
        loc.start()
        copies = []
        for k in range(1, N_DEV):
            px = 1 - x if k & 4 else x
            py = 1 - y if k & 2 else y
            pc = 1 - c if k & 1 else c
            cp = pltpu.make_async_remote_copy(
                src_ref=v_ref, dst_ref=o_ref.at[me], send_sem=send.at[k - 1], recv_sem=recv.at[k - 1],
                device_id=(px, py, pc), device_id_type=MESH)
            cp.start()
            copies.append(cp)
        for cp in copies:
            cp.wait()
        loc.wait()

    vm = pl.BlockSpec(memory_space=pltpu.VMEM)
    return pl.pallas_call(
        body, name=name, in_specs=[vm], out_specs=vm,
        out_shape=jax.ShapeDtypeStruct((N_DEV,) + v.shape, v.dtype),
        scratch_shapes=[pltpu.SemaphoreType.DMA((N_DEV - 1,))] * 2 + [pltpu.SemaphoreType.DMA],
    )(v)


def kernel(x, mem, norm_g, mem_norm_g, w_kv, w_out, pool_w_in, pool_w_grp, pool_scale, hgrn_w_in, hgrn_lb, hgrn_norm_g, final_g, loss_target, m_norm_g, m_mem_norm_g, m_w_kv, m_w_out, m_pool_w_in, m_pool_w_grp, m_pool_scale, m_hgrn_w_in, m_hgrn_lb, m_hgrn_norm_g, m_final_g, v_norm_g, v_mem_norm_g, v_w_kv, v_w_out, v_pool_w_in, v_pool_w_grp, v_pool_scale, v_hgrn_w_in, v_hgrn_lb, v_hgrn_norm_g, v_final_g):
    _, S, D = x.shape
    M = mem.shape[1]
    EB = 2 * D
    ECA = EB // 4
    EMIX = EB - ECA
    PG = EMIX // N_POOL_GROUPS
    NP0 = EMIX + ECA + EB
    NP1 = 3 * EMIX + ECA + EB
    SH0, SH1 = NP0 // N_CHIPS, NP1 // N_CHIPS
    DK, EK = D // N_CHIPS, EB // N_CHIPS
    TNP = 512 if all(v % 512 == 0 for v in (SH0, SH1, ECA, EMIX)) else 256
    TM = _tile(S, 1024)
    TMF = _tile(S, 2048)
    TD = _tile(D, 512)
    TDW = _tile(D, 1024)
    c0, c1 = SH0 // TNP, SH1 // TNP
    qt = EMIX // TNP
    chip = 2 * lax.axis_index("x") + lax.axis_index("y")

    xs, ms, tgt = x[0], mem[0], loss_target[0]

    flat = lambda w: w.reshape(-1, w.shape[-1])
    sds = jax.ShapeDtypeStruct
    chip1 = chip.astype(jnp.int32).reshape(1)

    def start_gather(tag, shards, after):
        shapes = [(D, NP1) if s is hgrn_w_in else (N_CHIPS,) + flat(s).shape for s in shards]
        bufs = [_cast_to_slot(f"cast_{tag}{t}", chip1, flat(s), shape, after) for t, (s, shape) in enumerate(zip(shards, shapes))]
        return _split_start(f"gather_{tag}_start", [], bufs, _gather_plan, 3 * len(bufs), after)

    gather_a, token = start_gather("a", [pool_w_in], None)
    gather_b, token = start_gather("b", [w_kv[0], w_out[0], pool_w_grp], token)
    gather_c, token = start_gather("c", [hgrn_w_in], token)
    gather_d, token = start_gather("d", [w_kv[1], w_out[1]], token)
    started = token[0:1, 0:1]

    tek, tew = _tile(EK, 512), _tile(EK, 1024)

    mem_n = _rms_fwd("rms_mem", ms, mem_norm_g.reshape(1, D) + started)
    h0 = _rms_fwd("rms0", xs, norm_g[0:1] + started)
    wpin, = _gather_finish("gather_a_finish", *_split_wait("gather_a_wait", gather_a, _gather_plan, 0, h0))

    tkw = _tile(2 * ECA, 1024)

    def kv_of(layer, wkv):
        return _matmul(
            f"kv{layer}", mem_n, wkv.reshape(D, 2 * ECA), grid=(1, 2 * ECA // tkw, 1),
            a_spec=pl.BlockSpec((M, D), lambda i, j, k: (0, 0)), b_spec=pl.BlockSpec((D, tkw), lambda i, j, k: (0, j)),
            out_shape=sds((M, 2 * ECA), BF16), out_spec=pl.BlockSpec((M, tkw), lambda i, j, k: (0, j)),
            acc_shape=(M, tkw), dims=NN)

    tko = _tile(EB, 2048)

    def out_proj(layer, branch, wout, resid):
        return _matmul(
            f"out_proj{layer}", branch, wout.reshape(EB, D), grid=(S // TM, D // TDW, EB // tko),
            a_spec=pl.BlockSpec((TM, tko), IK), b_spec=pl.BlockSpec((tko, TDW), KJ),
            out_shape=sds((S, D), F32), out_spec=pl.BlockSpec((TM, TDW), IJ),
            acc_shape=(TM, TDW), dims=NN, add=resid, add_spec=pl.BlockSpec((TM, TDW), IJ))

    ones_ca = jnp.ones((1, ECA), F32)

    proj0 = _matmul(
        "proj0", h0, wpin, grid=(S // TMF, NP0 // TNP, 1),
        a_spec=pl.BlockSpec((TMF, D), lambda i, j, k: (i, 0)),
        b_spec=pl.BlockSpec((None, D, TNP), lambda i, j, k: (j // c0, 0, j % c0)),
        out_shape=sds((S, NP0), BF16), out_spec=pl.BlockSpec((TMF, TNP), IJ),
        acc_shape=(TMF, TNP), dims=NN)
    pooled = _pool_fwd(proj0, S, EMIX)
    wkv0, wout0, g_grp = _gather_finish("gather_b_finish", *_split_wait("gather_b_wait", gather_b, _gather_plan, 0, pooled))
    wgrp = g_grp.reshape(N_CHIPS, N_POOL_GROUPS, PG // N_CHIPS, PG).transpose(1, 0, 2, 3).reshape(N_POOL_GROUPS, PG, PG)
    kv = [kv_of(0, wkv0), None]
    premix0 = _matmul(
        "pool_grp", pooled, wgrp, grid=(S // TM, N_POOL_GROUPS, 1),
        a_spec=pl.BlockSpec((TM, PG), lambda i, j, k: (i, j)),
        b_spec=pl.BlockSpec((None, PG, PG), lambda i, j, k: (j, 0, 0)),
        out_shape=sds((S, EB), BF16), out_spec=pl.BlockSpec((TM, PG), lambda i, j, k: (i, j)),
        acc_shape=(TM, PG), dims=NN)
    premix0 = _ca_fwd("ca_fwd0", proj0, EMIX // ECA, kv[0], premix0, S, ECA, EMIX)
    colscale0 = jnp.concatenate([pool_scale.reshape(1, EMIX), ones_ca], axis=1)
    gblk0 = (EMIX + ECA) // ECA
    branch0 = _gate_fwd("gate_fwd0", premix0, proj0, gblk0, colscale0, S, EB, ECA)
    x1 = out_proj(0, branch0, wout0, xs)

    whin, = _gather_finish("gather_c_finish", *_split_wait("gather_c_wait", gather_c, _gather_plan, 0, x1))
    h1 = _rms_fwd("rms1", x1, norm_g[1:2])

    def proj1_cols(name, ncols, col_of, out_cols, out_dtype, out_col_of):
        return _matmul(
            name, h1, whin, grid=(S // TMF, ncols, 1),
            a_spec=pl.BlockSpec((TMF, D), lambda i, j, k: (i, 0)),
            b_spec=pl.BlockSpec((D, TNP), lambda i, j, k: (0, col_of(j))),
            out_shape=sds((S, out_cols), out_dtype), out_spec=pl.BlockSpec((TMF, TNP), lambda i, j, k: (i, out_col_of(j))),
            acc_shape=(TMF, TNP), dims=NN)

    skip_f = lambda j: jnp.where(j < qt, j, j + qt)
    proj1 = proj1_cols("proj1", NP1 // TNP - qt, skip_f, NP1, BF16, skip_f)
    fgate = proj1_cols("proj1_f", qt, lambda j: j + qt, EMIX, F32, lambda j: j)
    premix1, rstd1, states = _hgrn_fwd(proj1, fgate, hgrn_lb, S, EMIX, EB)
    wkv1, wout1 = _gather_finish("gather_d_finish", *_split_wait("gather_d_wait", gather_d, _gather_plan, 0, rstd1))
    kv[1] = kv_of(1, wkv1)
    premix1 = _ca_fwd("ca_fwd1", proj1, 3 * EMIX // ECA, kv[1], premix1, S, ECA, EMIX)
    norm_tiles = _allgather_small("allgather_norm_g", jnp.pad(hgrn_norm_g, ((0, SMALL_ROWS - 1), (0, 0))))
    hg_norm = norm_tiles[0::2, 0, :].reshape(1, EMIX)
    colscale1 = jnp.concatenate([hg_norm, ones_ca], axis=1)
    gblk1 = (3 * EMIX + ECA) // ECA
    branch1 = _gate_fwd("gate_fwd1", premix1, proj1, gblk1, colscale1, S, EB, ECA)
    x2 = out_proj(1, branch1, wout1, x1)

    dx2, dx2b, d_final_g, loss_part = _loss_head(x2, final_g.reshape(1, D), tgt)

    def out_proj_bwd(layer, dxb, branch, wout, premix, proj, gblk, colscale, dshape, dblk):
        goff, doff = gblk * ECA // tek, dblk * ECA // tek
        dpremix, dgate, dcol = _matmul(
            f"dbranch{layer}", dxb, wout, grid=(S // TMF, EB // tek, 1),
            a_spec=pl.BlockSpec((TMF, D), lambda i, j, k: (i, 0)),
            b_spec=pl.BlockSpec((None, tek, D), lambda i, j, k: (j // (EK // tek), j % (EK // tek), 0)),
            extras=[(premix, pl.BlockSpec((TMF, tek), IJ)), (proj, pl.BlockSpec((TMF, tek), lambda i, j, k: (i, goff + j))),
                    (colscale, pl.BlockSpec((1, tek), lambda i, j, k: (0, j)))],
            epilogue=_gate_bwd_epilogue,
            out_shape=[sds((S, EB), BF16), sds(dshape, BF16), sds((S // TMF, 1, EB), F32)],
            out_spec=[pl.BlockSpec((TMF, tek), IJ), pl.BlockSpec((TMF, tek), lambda i, j, k: (i, doff + j)),
                      pl.BlockSpec((None, 1, tek), lambda i, j, k: (i, 0, j))],
            acc_shape=(TMF, tek), dims=NT)
        dw = _matmul(
            f"dwout{layer}", branch, dxb, grid=(EB // tew, D // TD, 1),
            a_spec=pl.BlockSpec((S, tew), lambda i, j, k: (0, i)), b_spec=pl.BlockSpec((S, TD), lambda i, j, k: (0, j)),
            out_shape=sds((N_CHIPS, EK, D), BF16),
            out_spec=pl.BlockSpec((None, tew, TD), lambda i, j, k: (i // (EK // tew), i % (EK // tew), j)),
            acc_shape=(tew, TD), dims=TN)
        return dpremix, dgate, dcol.reshape(S // TMF, EB), dw

    def kv_bwd(layer, dkv, wkv, dmem_add):
        dkvb = dkv.astype(BF16)
        dmem = _matmul(
            f"dmem{layer}", dkvb, wkv.reshape(D, 2 * ECA), grid=(1, D // TDW, 1),
            a_spec=pl.BlockSpec((M, 2 * ECA), lambda i, j, k: (0, 0)),
            b_spec=pl.BlockSpec((TDW, 2 * ECA), lambda i, j, k: (j, 0)),
            out_shape=sds((M, D), F32), out_spec=pl.BlockSpec((M, TDW), lambda i, j, k: (0, j)), acc_shape=(M, TDW),
            dims=NT, add=dmem_add, add_spec=pl.BlockSpec((M, TDW), lambda i, j, k: (0, j)))
        dw = _matmul(
            f"dwkv{layer}", mem_n, dkvb, grid=(D // TDW, 2 * ECA // tkw, 1),
            a_spec=pl.BlockSpec((M, TDW), lambda i, j, k: (0, i)), b_spec=pl.BlockSpec((M, tkw), lambda i, j, k: (0, j)),
            out_shape=sds((D, 2 * ECA), BF16), out_spec=pl.BlockSpec((TDW, tkw), IJ), acc_shape=(TDW, tkw), dims=TN)
        return dmem, dw.reshape(N_CHIPS, DK, 2 * ECA)

    dpremix1, drest1, dcol1, gw_out1 = out_proj_bwd(1, dx2b, branch1, wout1, premix1, proj1, gblk1, colscale1,
                                                    (S, ECA + EB), 1)
    drest1, dkv1 = _ca_bwd("ca_bwd1", dpremix1, proj1, 3 * EMIX // ECA, kv[1], drest1, 0, S, ECA, EMIX)
    dqfi, dlb = _hgrn_bwd(dpremix1, premix1, rstd1, states, proj1, fgate, hgrn_lb, S, EMIX)
    nq, nr = 3 * qt, (ECA + EB) // TNP
    tkh = _tile(EMIX, 1024) if (ECA + EB) % _tile(EMIX, 1024) == 0 else TNP
    kq = EMIX // tkh
    dh1 = _matmul(
        "dh1_qfi", dqfi, whin, grid=(S // TM, D // TDW, 3),
        a_spec=pl.BlockSpec((None, TM, EMIX), lambda i, j, k: (k, i, 0)),
        b_spec=pl.BlockSpec((TDW, EMIX), lambda i, j, k: (j, k)),
        out_shape=sds((S, D), F32), out_spec=pl.BlockSpec((TM, TDW), IJ), acc_shape=(TM, TDW), dims=NT)
    dh1 = _matmul(
        "dh1_rest", drest1, whin, grid=(S // TM, D // TDW, (ECA + EB) // tkh), a_spec=pl.BlockSpec((TM, tkh), IK),
        b_spec=pl.BlockSpec((TDW, tkh), lambda i, j, k: (j, k + 3 * kq)),
        out_shape=sds((S, D), F32), out_spec=pl.BlockSpec((TM, TDW), IJ), acc_shape=(TM, TDW), dims=NT,
        add=dh1, add_spec=pl.BlockSpec((TM, TDW), IJ))
    gw_hin = _matmul(
        "dwhin_qfi", h1, dqfi, grid=(D // TDW, nq, 1), a_spec=pl.BlockSpec((S, TDW), lambda i, j, k: (0, i)),
        b_spec=pl.BlockSpec((None, S, TNP), lambda i, j, k: (j // qt, 0, j % qt)),
        out_shape=sds((N_CHIPS, D, SH1), BF16), out_spec=pl.BlockSpec((None, TDW, TNP), lambda i, j, k: (j // c1, i, j % c1)),
        acc_shape=(TDW, TNP), dims=TN)
    gw_hin = _matmul(
        "dwhin_rest", h1, drest1, grid=(D // TDW, nr, 1), a_spec=pl.BlockSpec((S, TDW), lambda i, j, k: (0, i)),
        b_spec=pl.BlockSpec((S, TNP), lambda i, j, k: (0, j)), out_shape=sds((N_CHIPS, D, SH1), BF16),
        out_spec=pl.BlockSpec((None, TDW, TNP), lambda i, j, k: ((j + nq) // c1, i, (j + nq) % c1)),
        acc_shape=(TDW, TNP), dims=TN, alias=gw_hin)
    dmem, gw_kv1 = kv_bwd(1, dkv1, wkv1, None)

    core_chip = jnp.stack([lax.axis_index("c"), chip]).astype(jnp.int32)

    def reduce_in_chip(tag, stacks):
        got = _exchange_halves(f"exchange_halves{tag}", stacks)
        pairs = [_add_halves(f"add_halves{tag}_{t}", core_chip, g.reshape(N_CHIPS, 2, g.shape[1] // 2, g.shape[2]), r)
                 for t, (g, r) in enumerate(zip(stacks, got))]
        return [p for p, _ in pairs], [own for _, own in pairs]

    parts1, landed1 = reduce_in_chip(1, [gw_kv1, gw_out1, gw_hin])
    scatter1, token1 = _split_start("scatter1_start", parts1, landed1, _scatter_plan, 3 * len(parts1), None)
    dx1, dx1b, d_ng1 = _rms_bwd("rms_bwd1", dh1, x1, norm_g[1:2] + token1[0:1, 0:1], dx2)

    dpremix0, dproj0, dcol0, gw_out0 = out_proj_bwd(0, dx1b, branch0, wout0, premix0, proj0, gblk0, colscale0,
                                                    (S, NP0), gblk0)
    dproj0, dkv0 = _ca_bwd("ca_bwd0", dpremix0, proj0, EMIX // ECA, kv[0], dproj0, EMIX // ECA, S, ECA, EMIX)
    dmem, gw_kv0 = kv_bwd(0, dkv0, wkv0, dmem)
    parts_a, landed_a = reduce_in_chip("0a", [gw_kv0, gw_out0])
    scatter_a, token_a = _split_start("scatter0a_start", parts_a, landed_a, _scatter_plan, 3 * len(parts_a), None)
    dpooled = _matmul(
        "dpooled", dpremix0, wgrp, grid=(S // TM, N_POOL_GROUPS, 1), a_spec=pl.BlockSpec((TM, PG), IJ),
        b_spec=pl.BlockSpec((None, PG, PG), lambda i, j, k: (j, 0, 0)),
        out_shape=sds((S, EMIX), F32), out_spec=pl.BlockSpec((TM, PG), IJ), acc_shape=(TM, PG), dims=NT, after=token_a)
    dwgrp = _matmul(
        "dwgrp", pooled, dpremix0, grid=(N_POOL_GROUPS, 1, 1), a_spec=pl.BlockSpec((S, PG), lambda i, j, k: (0, i)),
        b_spec=pl.BlockSpec((S, PG), lambda i, j, k: (0, i)), out_shape=sds((N_POOL_GROUPS, PG, PG), F32),
        out_spec=pl.BlockSpec((None, PG, PG), lambda i, j, k: (i, 0, 0)), acc_shape=(PG, PG), dims=TN)
    dproj0 = _pool_bwd(dpooled, dproj0, S, EMIX)
    gw_pin = _matmul(
        "dwpin", h0, dproj0, grid=(D // TDW, NP0 // TNP, 1), a_spec=pl.BlockSpec((S, TDW), lambda i, j, k: (0, i)),
        b_spec=pl.BlockSpec((S, TNP), lambda i, j, k: (0, j)), out_shape=sds((N_CHIPS, D, SH0), BF16),
        out_spec=pl.BlockSpec((None, TDW, TNP), lambda i, j, k: (j // c0, i, j % c0)), acc_shape=(TDW, TNP), dims=TN)
    gw_grp = dwgrp.reshape(N_POOL_GROUPS, N_CHIPS, PG // N_CHIPS, PG).transpose(1, 0, 2, 3).reshape(N_CHIPS, PG, PG)
    parts_b, landed_b = reduce_in_chip("0b", [gw_pin, gw_grp.astype(BF16)])
    scatter_b, token_b = _split_start("scatter0b_start", parts_b, landed_b, _scatter_plan, 3 * len(parts_b), None)
    dh0 = _matmul(
        "dh0", dproj0, wpin, grid=(S // TM, D // TDW, N_CHIPS), a_spec=pl.BlockSpec((TM, SH0), IK),
        b_spec=pl.BlockSpec((None, TDW, SH0), lambda i, j, k: (k, j, 0)),
        out_shape=sds((S, D), F32), out_spec=pl.BlockSpec((TM, TDW), IJ), acc_shape=(TM, TDW), dims=NT, after=token_b)
    grad_x, _, d_ng0 = _rms_bwd("rms_bwd0", dh0, xs, norm_g[0:1], dx1)
    _, _, d_mng = _rms_bwd("rms_bwd_mem", dmem, ms, mem_norm_g.reshape(1, D), jnp.zeros_like(ms))

    _, landed1 = _split_wait("scatter1_wait", scatter1, _scatter_plan, len(parts1), grad_x)
    _, landed_a = _split_wait("scatter0a_wait", scatter_a, _scatter_plan, len(parts_a), grad_x)
    _, landed_b = _split_wait("scatter0b_wait", scatter_b, _scatter_plan, len(parts_b), grad_x)
    landed = [landed_a[0], landed1[0], landed_a[1], landed1[1], landed_b[0], landed_b[1], landed1[2]]
    fulls = _sum_share(landed)
    f2 = [f.reshape(-1, f.shape[-1]) for f in fulls]
    grads, deltas, new_m, new_v = {}, {}, {}, {}
    for n, w, mm, vv, gs in (("w_kv", w_kv, m_w_kv, v_w_kv, f2[0:2]), ("w_out", w_out, m_w_out, v_w_out, f2[2:4]),
                             ("pool_w_in", pool_w_in, m_pool_w_in, v_pool_w_in, f2[4:5]),
                             ("pool_w_grp", pool_w_grp, m_pool_w_grp, v_pool_w_grp, f2[5:6]),
                             ("hgrn_w_in", hgrn_w_in, m_hgrn_w_in, v_hgrn_w_in, f2[6:7])):
        as3d = lambda a: a.reshape((a.shape[0], -1, a.shape[-1]))
        outs = _adamw_layers(f"adamw_{n}", as3d(w), gs, as3d(mm), as3d(vv))
        grads[n], deltas[n], new_m[n], new_v[n] = [o.reshape(w.shape) for o in outs]

    Wd = EMIX
    partial = _pack_rows("pack_partials", [d_ng0, d_ng1, d_mng, dcol0[:, :EMIX], dlb, dcol1[:, :EMIX], d_final_g,
                                           loss_part], Wd)
    summed = _small_sum(_allgather_small("allgather_grads", partial), hgrn_lb, 4)
    row = lambda i, n=Wd: summed[i:i + 1, :n]
    nshard = EMIX // N_CHIPS
    g_hg_norm = lax.dynamic_slice_in_dim(row(5), chip * nshard, nshard, axis=1)
    small_names = ["norm_g0", "norm_g1", "mem_norm_g", "pool_scale", "hgrn_lb0", "hgrn_lb1", "hgrn_norm_g", "final_g"]
    small_w = [norm_g[0:1], norm_g[1:2], mem_norm_g.reshape(1, D), pool_scale, hgrn_lb[0:1], hgrn_lb[1:2], hgrn_norm_g,
               final_g.reshape(1, D)]
    small_m = [m_norm_g[0:1], m_norm_g[1:2], m_mem_norm_g.reshape(1, D), m_pool_scale, m_hgrn_lb[0:1], m_hgrn_lb[1:2],
               m_hgrn_norm_g, m_final_g.reshape(1, D)]
    small_v = [v_norm_g[0:1], v_norm_g[1:2], v_mem_norm_g.reshape(1, D), v_pool_scale, v_hgrn_lb[0:1], v_hgrn_lb[1:2],
               v_hgrn_norm_g, v_final_g.reshape(1, D)]
    g_pack = _pack_rows("pack_small_g", [row(0, D), row(1, D), row(2, D), row(3), row(8), row(9), g_hg_norm, row(6, D)], Wd)
    d_pack, m_pack, v_pack = _adamw("adamw_small", _pack_rows("pack_small_w", small_w, Wd), g_pack,
                                    _pack_rows("pack_small_m", small_m, Wd), _pack_rows("pack_small_v", small_v, Wd))
    widths = [v.shape[1] for v in small_w]
    rows = lambda p: {n: p[i, :widths[i]] for i, n in enumerate(small_names)}

    def assemble(r, out):
        out["norm_g"] = jnp.stack([r["norm_g0"], r["norm_g1"]])
        out["mem_norm_g"] = r["mem_norm_g"]
        out["pool_scale"] = r["pool_scale"].reshape(1, EMIX)
        out["hgrn_lb"] = jnp.stack([r["hgrn_lb0"], r["hgrn_lb1"]])
        out["hgrn_norm_g"] = r["hgrn_norm_g"].reshape(1, nshard)
        out["final_g"] = r["final_g"]

    assemble(rows(g_pack), grads)
    assemble(rows(d_pack), deltas)
    assemble(rows(m_pack), new_m)
    assemble(rows(v_pack), new_v)
    loss = summed[7, 0]

    order = ["norm_g", "mem_norm_g", "w_kv", "w_out", "pool_w_in", "pool_w_grp", "pool_scale", "hgrn_w_in", "hgrn_lb",
             "hgrn_norm_g", "final_g"]
    return (loss, grad_x.reshape(1, S, D), *[grads[n] for n in order], *[deltas[n] for n in order],
            *[new_m[n] for n in order], *[new_v[n] for n in order])
```

```python
import functools

import jax
import jax.numpy as jnp
from jax import lax
from jax.experimental import pallas as pl
from jax.experimental.pallas import tpu as pltpu

F32 = jnp.float32
BF16 = jnp.bfloat16
MESH = pl.DeviceIdType.MESH
ANY = pl.BlockSpec(memory_space=pl.ANY)

EPS = 1e-6
HG_HEAD_DIM = 128
HG_CHUNK = 64
CA_HEADS = 4
N_POOL_GROUPS = 4
POOL_HALO = 128
ADAM_LR = 0.001
ADAM_B1 = 0.9
ADAM_B2 = 0.999
ADAM_EPS = 1e-08
ADAM_WD = 0.01
ADAM_STEP = 10
N_CHIPS = 4
N_DEV = 8
VMEM_LIMIT_BYTES = 56 * 1024 * 1024
SMALL_ROWS = 8
STREAM_CHUNK_BYTES = 2 * 1024 * 1024
STREAM_SLOTS = 4
SUM_SLOTS = 2


def _params(*sem):
    return pltpu.CompilerParams(dimension_semantics=sem, vmem_limit_bytes=VMEM_LIMIT_BYTES)


def _tile(n, pref):
    t = pref
    while n % t:
        t //= 2
    return t


def _sigmoid(x):
    return 1.0 / (1.0 + jnp.exp(-x))


def _matmul(name, a, b, *, grid, a_spec, b_spec, out_shape, out_spec, acc_shape, dims,
            add=None, add_spec=None, alias=None, after=None, extras=(), epilogue=None):
    nk = grid[2]
    has_add = add is not None
    has_alias = alias is not None
    has_after = after is not None
    n_out = len(out_shape) if epilogue is not None else 1

    def body(*refs):
        a_ref, b_ref = refs[0], refs[1]
        pos = 2
        add_ref = None
        if has_add:
            add_ref = refs[pos]
            pos += 1
        extra_refs = refs[pos:pos + len(extras)]
        pos += len(extras) + has_alias + has_after
        o_refs = refs[pos:pos + n_out]
        prod = lax.dot_general(a_ref[...], b_ref[...], (dims, ((), ())), preferred_element_type=F32)

        def finish(r):
            if epilogue is not None:
                epilogue(r, extra_refs, o_refs)
                return
            if has_add:
                r = r + add_ref[...].astype(F32)
            o_refs[0][...] = r.astype(o_refs[0].dtype)

        if nk == 1:
            finish(prod)
            return
        acc_ref = refs[pos + n_out]
        k = pl.program_id(2)

        @pl.when(k == 0)
        def _():
            acc_ref[...] = prod

        @pl.when(k > 0)
        def _():
            acc_ref[...] += prod

        @pl.when(k == nk - 1)
        def _():
            finish(acc_ref[...])

    operands = [a, b]
    in_specs = [a_spec, b_spec]
    if has_add:
        operands.append(add)
        in_specs.append(add_spec)
    for arr, spec in extras:
        operands.append(arr)
        in_specs.append(spec)
    aliases = {}
    if has_alias:
        aliases = {len(operands): 0}
        operands.append(alias)
        in_specs.append(ANY)
    if has_after:
        operands.append(after)
        in_specs.append(ANY)
    return pl.pallas_call(
        body, name=name, grid=grid, in_specs=in_specs, out_specs=out_spec, out_shape=out_shape,
        scratch_shapes=[pltpu.VMEM(acc_shape, F32)] if nk > 1 else [], input_output_aliases=aliases,
        compiler_params=_params("parallel", "parallel", "arbitrary"),
    )(*operands)


IJ = lambda i, j, k: (i, j)
IK = lambda i, j, k: (i, k)
KJ = lambda i, j, k: (k, j)
KI = lambda i, j, k: (k, i)
NN = ((1,), (0,))
NT = ((1,), (1,))
TN = ((0,), (0,))


def _rms_fwd(name, x, g):
    R, D = x.shape
    tr = _tile(R, 256)

    def body(x_ref, g_ref, o_ref):
        xf = x_ref[...]
        r = lax.rsqrt(jnp.mean(xf * xf, axis=-1, keepdims=True) + EPS)
        o_ref[...] = (xf * r * g_ref[...]).astype(o_ref.dtype)

    return pl.pallas_call(
        body, name=name, grid=(R // tr,),
        in_specs=[pl.BlockSpec((tr, D), lambda i: (i, 0)), pl.BlockSpec((1, D), lambda i: (0, 0))],
        out_specs=pl.BlockSpec((tr, D), lambda i: (i, 0)),
        out_shape=jax.ShapeDtypeStruct((R, D), BF16), compiler_params=_params("parallel"),
    )(x, g)


def _rms_bwd(name, dh, x, g, dres):
    R, D = x.shape
    tr = _tile(R, 256)

    def body(dh_ref, x_ref, g_ref, dres_ref, dx_ref, dxb_ref, dg_ref):
        xf = x_ref[...]
        r = lax.rsqrt(jnp.mean(xf * xf, axis=-1, keepdims=True) + EPS)
        xn = xf * r
        d = dh_ref[...]
        dyg = d * g_ref[...]
        dx = r * (dyg - xn * jnp.mean(dyg * xn, axis=-1, keepdims=True)) + dres_ref[...]
        dx_ref[...] = dx
        dxb_ref[...] = dx.astype(BF16)

        @pl.when(pl.program_id(0) == 0)
        def _():
            dg_ref[...] = jnp.zeros_like(dg_ref)

        dg_ref[...] += jnp.sum(d * xn, axis=0, keepdims=True)

    row = pl.BlockSpec((tr, D), lambda i: (i, 0))
    vec = pl.BlockSpec((1, D), lambda i: (0, 0))
    return pl.pallas_call(
        body, name=name, grid=(R // tr,), in_specs=[row, row, vec, row], out_specs=[row, row, vec],
        out_shape=[jax.ShapeDtypeStruct((R, D), F32), jax.ShapeDtypeStruct((R, D), BF16),
                   jax.ShapeDtypeStruct((1, D), F32)],
        compiler_params=_params("arbitrary"),
    )(dh, x, g, dres)


def _loss_head(x2, g, target):
    R, D = x2.shape
    tr = _tile(R, 256)

    def body(x_ref, g_ref, t_ref, dx_ref, dxb_ref, dg_ref, loss_ref):
        xf = x_ref[...]
        gg = g_ref[...]
        r = lax.rsqrt(jnp.mean(xf * xf, axis=-1, keepdims=True) + EPS)
        xn = xf * r
        e = xn * gg - t_ref[...]
        part = 0.5 * jnp.sum(jnp.mean(e * e, axis=-1, keepdims=True), axis=0, keepdims=True)
        dy = e * (1.0 / D)
        dyg = dy * gg
        dx = r * (dyg - xn * jnp.mean(dyg * xn, axis=-1, keepdims=True))
        dx_ref[...] = dx
        dxb_ref[...] = dx.astype(BF16)

        @pl.when(pl.program_id(0) == 0)
        def _():
            dg_ref[...] = jnp.zeros_like(dg_ref)
            loss_ref[...] = jnp.zeros_like(loss_ref)

        dg_ref[...] += jnp.sum(dy * xn, axis=0, keepdims=True)
        loss_ref[...] += jnp.broadcast_to(part, loss_ref.shape)

    row = pl.BlockSpec((tr, D), lambda i: (i, 0))
    vec = pl.BlockSpec((1, D), lambda i: (0, 0))
    return pl.pallas_call(
        body, name="loss_head", grid=(R // tr,), in_specs=[row, vec, row],
        out_specs=[row, row, vec, pl.BlockSpec((1, 128), lambda i: (0, 0))],
        out_shape=[jax.ShapeDtypeStruct((R, D), F32), jax.ShapeDtypeStruct((R, D), BF16),
                   jax.ShapeDtypeStruct((1, D), F32), jax.ShapeDtypeStruct((1, 128), F32)],
        compiler_params=_params("arbitrary"),
    )(x2, g, target)


def _pool_band(tr, reverse, w):
    r = lax.broadcasted_iota(jnp.int32, (tr, tr + POOL_HALO), 0)
    c = lax.broadcasted_iota(jnp.int32, (tr, tr + POOL_HALO), 1)
    if reverse:
        inside = (c >= r) & (c < r + w)
    else:
        cc = c - POOL_HALO
        inside = (cc <= r) & (cc > r - w)
    return jnp.where(inside, 1.0, 0.0).astype(BF16)


def _pool_fwd(proj, S, EMIX):
    PG = EMIX // N_POOL_GROUPS
    cb = PG
    tr = _tile(S, 512)
    per_group = PG // cb

    def body(u_ref, o_ref, ext):
        i = pl.program_id(1)
        w = jnp.left_shift(2, pl.program_id(0) // per_group)

        @pl.when(i == 0)
        def _():
            ext[0:POOL_HALO, :] = jnp.zeros((POOL_HALO, cb), BF16)

        u = u_ref[...]
        ext[POOL_HALO:, :] = u
        win = jnp.dot(_pool_band(tr, False, w), ext[...], preferred_element_type=F32)
        pos = i * tr + lax.broadcasted_iota(jnp.int32, (tr, 1), 0)
        cnt = jnp.minimum(pos + 1, w).astype(F32)
        o_ref[...] = (win / cnt - u.astype(F32)).astype(BF16)
        ext[0:POOL_HALO, :] = u[tr - POOL_HALO:, :]

    return pl.pallas_call(
        body, name="pool_fwd", grid=(EMIX // cb, S // tr),
        in_specs=[pl.BlockSpec((tr, cb), lambda j, i: (i, j))],
        out_specs=pl.BlockSpec((tr, cb), lambda j, i: (i, j)),
        out_shape=jax.ShapeDtypeStruct((S, EMIX), BF16),
        scratch_shapes=[pltpu.VMEM((tr + POOL_HALO, cb), BF16)],
        compiler_params=_params("parallel", "arbitrary"),
    )(proj)


def _pool_bwd(dpooled, dproj, S, EMIX):
    PG = EMIX // N_POOL_GROUPS
    cb = PG
    tr = _tile(S, 512)
    per_group = PG // cb
    nrt = S // tr

    def body(d_ref, _, o_ref, ext):
        step = pl.program_id(1)
        i = nrt - 1 - step
        w = jnp.left_shift(2, pl.program_id(0) // per_group)

        @pl.when(step == 0)
        def _():
            ext[tr:, :] = jnp.zeros((POOL_HALO, cb), BF16)

        d = d_ref[...]
        pos = i * tr + lax.broadcasted_iota(jnp.int32, (tr, 1), 0)
        cnt = jnp.minimum(pos + 1, w).astype(F32)
        z = (d / cnt).astype(BF16)
        ext[0:tr, :] = z
        win = jnp.dot(_pool_band(tr, True, w), ext[...], preferred_element_type=F32)
        o_ref[...] = (win - d).astype(BF16)
        ext[tr:, :] = z[0:POOL_HALO, :]

    return pl.pallas_call(
        body, name="pool_bwd", grid=(EMIX // cb, nrt),
        in_specs=[pl.BlockSpec((tr, cb), lambda j, s: (nrt - 1 - s, j)), ANY],
        out_specs=pl.BlockSpec((tr, cb), lambda j, s: (nrt - 1 - s, j)),
        out_shape=jax.ShapeDtypeStruct(dproj.shape, dproj.dtype),
        scratch_shapes=[pltpu.VMEM((tr + POOL_HALO, cb), BF16)],
        input_output_aliases={1: 0},
        compiler_params=_params("parallel", "arbitrary"),
    )(dpooled, dproj)


def _ca_fwd(name, proj, qblk, kv, premix, S, ECA, EMIX):
    M = kv.shape[0]
    hd = ECA // CA_HEADS
    ts = _tile(S, 512)
    scale = hd ** -0.5

    def body(q_ref, kv_ref, _, o_ref):
        for h in range(CA_HEADS):
            q = q_ref[:, h * hd:(h + 1) * hd]
            k = kv_ref[:, h * hd:(h + 1) * hd]
            v = kv_ref[:, ECA + h * hd:ECA + (h + 1) * hd]
            s = lax.dot_general(q, k, (NT, ((), ())), preferred_element_type=F32) * scale
            s = s - jnp.max(s, axis=-1, keepdims=True)
            p = jnp.exp(s)
            p = p / jnp.sum(p, axis=-1, keepdims=True)
            o = jnp.dot(p.astype(BF16), v, preferred_element_type=F32)
            o_ref[:, h * hd:(h + 1) * hd] = o.astype(BF16)

    return pl.pallas_call(
        body, name=name, grid=(S // ts,),
        in_specs=[pl.BlockSpec((ts, ECA), lambda i: (i, qblk)), pl.BlockSpec((M, 2 * ECA), lambda i: (0, 0)), ANY],
        out_specs=pl.BlockSpec((ts, ECA), lambda i: (i, EMIX // ECA)),
        out_shape=jax.ShapeDtypeStruct(premix.shape, premix.dtype),
        input_output_aliases={2: 0}, compiler_params=_params("parallel"),
    )(proj, kv, premix)


def _ca_bwd(name, dpremix, proj, qblk, kv, dbuf, dblk, S, ECA, EMIX):
    M = kv.shape[0]
    hd = ECA // CA_HEADS
    ts = _tile(S, 512)
    scale = hd ** -0.5

    def body(do_ref, q_ref, kv_ref, _, dq_ref, dkv_ref):
        @pl.when(pl.program_id(0) == 0)
        def _():
            dkv_ref[...] = jnp.zeros_like(dkv_ref)

        for h in range(CA_HEADS):
            lo, hi = h * hd, (h + 1) * hd
            q = q_ref[:, lo:hi]
            k = kv_ref[:, lo:hi]
            v = kv_ref[:, ECA + lo:ECA + hi]
            do = do_ref[:, lo:hi]
            s = lax.dot_general(q, k, (NT, ((), ())), preferred_element_type=F32) * scale
            s = s - jnp.max(s, axis=-1, keepdims=True)
            p = jnp.exp(s)
            p = p / jnp.sum(p, axis=-1, keepdims=True)
            pb = p.astype(BF16)
            dkv_ref[:, ECA + lo:ECA + hi] += lax.dot_general(pb, do, (TN, ((), ())), preferred_element_type=F32)
            dp = lax.dot_general(do, v, (NT, ((), ())), preferred_element_type=F32)
            ds = (p * (dp - jnp.sum(p * dp, axis=-1, keepdims=True)) * scale).astype(BF16)
            dq_ref[:, lo:hi] = jnp.dot(ds, k, preferred_element_type=F32).astype(BF16)
            dkv_ref[:, lo:hi] += lax.dot_general(ds, q, (TN, ((), ())), preferred_element_type=F32)

    return pl.pallas_call(
        body, name=name, grid=(S // ts,),
        in_specs=[pl.BlockSpec((ts, ECA), lambda i: (i, EMIX // ECA)), pl.BlockSpec((ts, ECA), lambda i: (i, qblk)),
                  pl.BlockSpec((M, 2 * ECA), lambda i: (0, 0)), ANY],
        out_specs=[pl.BlockSpec((ts, ECA), lambda i: (i, dblk)), pl.BlockSpec((M, 2 * ECA), lambda i: (0, 0))],
        out_shape=[jax.ShapeDtypeStruct(dbuf.shape, dbuf.dtype), jax.ShapeDtypeStruct((M, 2 * ECA), F32)],
        input_output_aliases={3: 0}, compiler_params=_params("arbitrary"),
    )(dpremix, proj, kv, dbuf)


def _gate_fwd(name, premix, proj, gblk, colscale, S, EB, ECA):
    ts = _tile(S, 512)

    def body(p_ref, g_ref, c_ref, o_ref):
        g = g_ref[...].astype(F32)
        o_ref[...] = (p_ref[...].astype(F32) * c_ref[...] * (g * _sigmoid(g))).astype(BF16)

    return pl.pallas_call(
        body, name=name, grid=(S // ts, EB // ECA),
        in_specs=[pl.BlockSpec((ts, ECA), lambda i, j: (i, j)), pl.BlockSpec((ts, ECA), lambda i, j: (i, gblk + j)),
                  pl.BlockSpec((1, ECA), lambda i, j: (0, j))],
        out_specs=pl.BlockSpec((ts, ECA), lambda i, j: (i, j)),
        out_shape=jax.ShapeDtypeStruct((S, EB), BF16), compiler_params=_params("parallel", "parallel"),
    )(premix, proj, colscale)


def _gate_bwd_epilogue(db, extra_refs, out_refs):
    p_ref, g_ref, c_ref = extra_refs
    dp_ref, dg_ref, dc_ref = out_refs
    g = g_ref[...].astype(F32)
    sg = _sigmoid(g)
    si = g * sg
    c = c_ref[...]
    t = db * p_ref[...].astype(F32)
    dp_ref[...] = (db * si * c).astype(BF16)
    dg_ref[...] = (t * c * (sg * (1.0 + g * (1.0 - sg)))).astype(BF16)
    dc_ref[...] = jnp.sum(t * si, axis=0, keepdims=True)


def _hgrn_lb(lb_ref):
    l0 = lb_ref[0:1, :]
    l1 = lb_ref[1:2, :]
    mx = jnp.maximum(l0, l1)
    e0 = jnp.exp(l0 - mx)
    e1 = jnp.exp(l1 - mx)
    return e1 / (e0 + e1)


def _bdot(a, b, ca, cb):
    return lax.dot_general(a, b, (((ca,), (cb,)), ((0,), (0,))), preferred_element_type=F32)


def _tri_sum(tri, x):
    hi = x.astype(BF16)
    lo = (x - hi.astype(F32)).astype(BF16)
    tri = tri.astype(BF16)
    return _bdot(tri, hi, 2, 1) + _bdot(tri, lo, 2, 1)


def _hgrn_chunks(qin, fin, lbh, n):
    C = HG_CHUNK
    row = lax.broadcasted_iota(jnp.int32, (n, C, C), 1)
    col = lax.broadcasted_iota(jnp.int32, (n, C, C), 2)
    causal = row >= col
    sg = _sigmoid(fin)
    f = lbh + (1.0 - lbh) * sg
    k = 1.0 - f
    g = jnp.log(f)
    b = _tri_sum(jnp.where(causal, 1.0, 0.0), g)
    b_last = jnp.sum(g, axis=1, keepdims=True)
    eb = jnp.exp(b)
    einv = jnp.exp(-b)
    eend = jnp.exp(b_last - b)
    sq = _sigmoid(qin)
    a = qin * sq * (HG_HEAD_DIM ** -0.5) * eb
    bm = k * einv
    e = k * eend
    d = jnp.exp(b_last)
    p = jnp.where(causal, _bdot(a.astype(BF16), bm.astype(BF16), 2, 2), 0.0)
    return dict(causal=causal, sg=sg, f=f, eb=eb, einv=einv, eend=eend, sq=sq, a=a, bm=bm, e=e, d=d, p=p)


def _hgrn_fwd(proj, fgate, hgrn_lb, S, EMIX, EB):
    HD, C = HG_HEAD_DIM, HG_CHUNK
    HH = EMIX // HD
    hb = 6 if HH % 6 == 0 else 1
    W = hb * HD
    tr = _tile(S, 512)
    n = tr // C

    def body(q_ref, f_ref, i_ref, lb_ref, o_ref, rstd_ref, st_ref, state):
        @pl.when(pl.program_id(1) == 0)
        def _():
            state[...] = jnp.zeros_like(state)

        lb = _hgrn_lb(lb_ref)
        for h in range(hb):
            cs = slice(h * HD, (h + 1) * HD)
            qin = q_ref[:, cs].astype(F32).reshape(n, C, HD)
            fin = f_ref[:, cs].reshape(n, C, HD)
            v = i_ref[:, cs].reshape(n, C, HD)
            t = _hgrn_chunks(qin, fin, lb[:, cs], n)
            upd = _bdot(v, t["e"].astype(BF16), 1, 1)
            st = state[h]
            for c in range(n):
                st_ref[h, c] = st
                st = st * t["d"][c] + upd[c]
            state[h] = st
            o = _bdot(t["p"].astype(BF16), v, 2, 1) + _bdot(t["a"].astype(BF16), st_ref[h].astype(BF16), 2, 2)
            rstd = lax.rsqrt(jnp.mean(o * o, axis=-1, keepdims=True) + EPS)
            o_ref[:, cs] = (o * rstd).reshape(tr, HD).astype(BF16)
            rstd_ref[:, cs] = jnp.broadcast_to(rstd, (n, C, HD)).reshape(tr, HD)

    blk = lambda off: pl.BlockSpec((tr, W), lambda g, i: (i, off + g))
    return pl.pallas_call(
        body, name="hgrn_fwd", grid=(HH // hb, S // tr),
        in_specs=[blk(0), blk(0), blk(2 * EMIX // W), pl.BlockSpec((2, W), lambda g, i: (0, g))],
        out_specs=[blk(0), blk(0), pl.BlockSpec((hb, n, HD, HD), lambda g, i: (g, i, 0, 0))],
        out_shape=[jax.ShapeDtypeStruct((S, EB), BF16), jax.ShapeDtypeStruct((S, EMIX), F32),
                   jax.ShapeDtypeStruct((HH, S // C, HD, HD), F32)],
        scratch_shapes=[pltpu.VMEM((hb, HD, HD), F32)],
        compiler_params=_params("parallel", "arbitrary"),
    )(proj, fgate, proj, hgrn_lb)


def _hgrn_bwd(dpremix, premix, rstd, states, proj, fgate, hgrn_lb, S, EMIX):
    HD, C = HG_HEAD_DIM, HG_CHUNK
    HH = EMIX // HD
    hb = 6 if HH % 6 == 0 else 1
    W = hb * HD
    tr = _tile(S, 512)
    n = tr // C
    nrt = S // tr

    def body(do_ref, on_ref, rstd_ref, st_ref, q_ref, f_ref, i_ref, lb_ref, d_ref, dlb_ref, dstate, dsbuf):
        @pl.when(pl.program_id(1) == 0)
        def _():
            dstate[...] = jnp.zeros_like(dstate)
            dlb_ref[...] = jnp.zeros_like(dlb_ref)

        lb = _hgrn_lb(lb_ref)
        for h in range(hb):
            cs = slice(h * HD, (h + 1) * HD)
            qin = q_ref[:, cs].astype(F32).reshape(n, C, HD)
            fin = f_ref[:, cs].reshape(n, C, HD)
            v = i_ref[:, cs].reshape(n, C, HD)
            lbh = lb[:, cs]
            t = _hgrn_chunks(qin, fin, lbh, n)
            a, bm, e, d, p = t["a"], t["bm"], t["e"], t["d"], t["p"]
            ab, bmb, eb16 = a.astype(BF16), bm.astype(BF16), e.astype(BF16)
            on = on_ref[:, cs].astype(F32).reshape(n, C, HD)
            dn = do_ref[:, cs].astype(F32).reshape(n, C, HD)
            do = rstd_ref[:, cs].reshape(n, C, HD) * (dn - on * jnp.mean(dn * on, axis=-1, keepdims=True))
            dob = do.astype(BF16)
            grow = _bdot(dob, ab, 1, 1)
            ds = dstate[h]
            for c in reversed(range(n)):
                dsbuf[h, c] = ds
                ds = ds * d[c] + grow[c]
            dstate[h] = ds
            dst = dsbuf[h]
            st = st_ref[h]
            dstb = dst.astype(BF16)
            dp = jnp.where(t["causal"], _bdot(dob, v, 2, 2), 0.0).astype(BF16)
            dv = _bdot(p.astype(BF16), dob, 1, 1) + _bdot(eb16, dstb, 2, 2)
            da = _bdot(dp, bmb, 2, 1) + _bdot(dob, st.astype(BF16), 2, 1)
            dbm = _bdot(dp, ab, 1, 1)
            de = _bdot(v, dstb, 2, 1)
            dd = jnp.sum(dst * st, axis=1, keepdims=True)
            dk = dbm * t["einv"] + de * t["eend"]
            dee = de * e
            db = da * a - dbm * bm - dee
            extra = jnp.sum(dee, axis=1, keepdims=True) + dd * d
            upper = jnp.where(lax.broadcasted_iota(jnp.int32, (n, C, C), 2)
                              >= lax.broadcasted_iota(jnp.int32, (n, C, C), 1), 1.0, 0.0)
            dg = _tri_sum(upper, db) + extra
            df = dg / t["f"] - dk
            sg, sq = t["sg"], t["sq"]
            dq = da * t["eb"] * (HD ** -0.5) * (sq * (1.0 + qin * (1.0 - sq)))
            d_ref[0, :, cs] = dq.reshape(tr, HD).astype(BF16)
            d_ref[1, :, cs] = (df * (1.0 - lbh) * sg * (1.0 - sg)).reshape(tr, HD).astype(BF16)
            d_ref[2, :, cs] = dv.reshape(tr, HD).astype(BF16)
            dlb_ref[:, cs] += jnp.sum((df * (1.0 - sg)).reshape(tr, HD), axis=0, keepdims=True)

    rev = lambda off: pl.BlockSpec((tr, W), lambda g, s: (nrt - 1 - s, off + g))
    return pl.pallas_call(
        body, name="hgrn_bwd", grid=(HH // hb, nrt),
        in_specs=[rev(0), rev(0), rev(0), pl.BlockSpec((hb, n, HD, HD), lambda g, s: (g, nrt - 1 - s, 0, 0)),
                  rev(0), rev(0), rev(2 * EMIX // W), pl.BlockSpec((2, W), lambda g, s: (0, g))],
        out_specs=[pl.BlockSpec((3, tr, W), lambda g, s: (0, nrt - 1 - s, g)), pl.BlockSpec((1, W), lambda g, s: (0, g))],
        out_shape=[jax.ShapeDtypeStruct((3, S, EMIX), BF16), jax.ShapeDtypeStruct((1, EMIX), F32)],
        scratch_shapes=[pltpu.VMEM((hb, HD, HD), F32), pltpu.VMEM((hb, n, HD, HD), F32)],
        compiler_params=_params("parallel", "arbitrary"),
    )(dpremix, premix, rstd, states, proj, fgate, proj, hgrn_lb)


EW_BLOCK_ELEMS = 512 * 1024


def _ew_tiles(R, C):
    tc = C if C <= 4096 else _tile(C, 2048)
    tr = _tile(R, 512)
    while tr * tc > EW_BLOCK_ELEMS and tr % 16 == 0:
        tr //= 2
    return tr, tc


def _add_halves(name, core_chip, grad, got):
    _, _, R, C = grad.shape
    tr, tc = _ew_tiles(R, C)

    def body(c_ref, a_ref, b_ref, o_ref, own_ref):
        r = (a_ref[...].astype(F32) + b_ref[...].astype(F32)).astype(BF16)
        o_ref[...] = r

        @pl.when(pl.program_id(2) == c_ref[1])
        def _():
            own_ref[...] = r

    blk = pl.BlockSpec((None, tr, tc), lambda i, j, s, c: (s, i, j))
    sds = jax.ShapeDtypeStruct(got.shape, BF16)
    return pl.pallas_call(
        body, name=name, out_shape=[sds, sds],
        grid_spec=pltpu.PrefetchScalarGridSpec(
            num_scalar_prefetch=1, grid=(R // tr, C // tc, N_CHIPS),
            in_specs=[pl.BlockSpec((None, None, tr, tc), lambda i, j, s, c: (s, c[0], i, j)), blk],
            out_specs=[blk, pl.BlockSpec((None, tr, tc), lambda i, j, s, c: (c[1], i, j))]),
        compiler_params=_params("parallel", "parallel", "arbitrary"),
    )(core_chip, grad, got)


def _adam_step(w, g, m, v):
    mn = ADAM_B1 * m + (1.0 - ADAM_B1) * g
    vn = ADAM_B2 * v + (1.0 - ADAM_B2) * (g * g)
    m_hat = mn / (1.0 - ADAM_B1 ** ADAM_STEP)
    v_hat = vn / (1.0 - ADAM_B2 ** ADAM_STEP)
    return -ADAM_LR * (m_hat / (jnp.sqrt(v_hat) + ADAM_EPS) + ADAM_WD * w), mn, vn


def _adamw(name, w, g, m, v):
    R, C = w.shape
    tr, tc = _ew_tiles(R, C)

    def body(w_ref, g_ref, m_ref, v_ref, d_ref, mo_ref, vo_ref):
        d_ref[...], mo_ref[...], vo_ref[...] = _adam_step(w_ref[...], g_ref[...], m_ref[...], v_ref[...])

    blk = pl.BlockSpec((tr, tc), lambda i, j: (i, j))
    sds = jax.ShapeDtypeStruct((R, C), F32)
    return pl.pallas_call(
        body, name=name, grid=(R // tr, C // tc), in_specs=[blk] * 4, out_specs=[blk] * 3, out_shape=[sds] * 3,
        compiler_params=_params("parallel", "parallel"),
    )(w, g, m, v)


def _adamw_layers(name, w, gs, m, v):
    L, R, C = w.shape
    tr, tc = _ew_tiles(R, C)

    def body(*refs):
        w_ref, m_ref, v_ref = refs[:3]
        g_refs = refs[3:3 + L]
        go_ref, d_ref, mo_ref, vo_ref = refs[3 + L:]
        layer = pl.program_id(0)
        g = g_refs[0][...]
        for n in range(1, L):
            g = jnp.where(layer == n, g_refs[n][...], g)
        go_ref[...] = g
        d_ref[...], mo_ref[...], vo_ref[...] = _adam_step(w_ref[...], g, m_ref[...], v_ref[...])

    blk = pl.BlockSpec((None, tr, tc), lambda l, i, j: (l, i, j))
    of_layer = lambda n: pl.BlockSpec((tr, tc), lambda l, i, j: (jnp.where(l == n, i, 0), jnp.where(l == n, j, 0)))
    sds = jax.ShapeDtypeStruct((L, R, C), F32)
    return pl.pallas_call(
        body, name=name, grid=(L, R // tr, C // tc), in_specs=[blk] * 3 + [of_layer(n) for n in range(L)],
        out_specs=[blk] * 4, out_shape=[sds] * 4, compiler_params=_params("parallel", "parallel", "parallel"),
    )(w, m, v, *gs)


def _pack_rows(name, vecs, W):
    nv = len(vecs)

    def body(*refs):
        o_ref = refs[nv]
        o_ref[...] = jnp.zeros_like(o_ref)
        for i in range(nv):
            o_ref[i:i + 1, 0:vecs[i].shape[1]] = jnp.sum(refs[i][...], axis=0, keepdims=True)

    vm = pl.BlockSpec(memory_space=pltpu.VMEM)
    return pl.pallas_call(
        body, name=name, in_specs=[vm] * nv, out_specs=vm, out_shape=jax.ShapeDtypeStruct((SMALL_ROWS, W), F32),
    )(*vecs)


def _small_sum(gathered, hgrn_lb, lb_row):
    _, T, W = gathered.shape

    def body(g_ref, lb_ref, o_ref):
        acc = g_ref[0]
        for dev in range(1, N_DEV):
            acc = acc + g_ref[dev]
        o_ref[0:T, :] = acc
        lb = _hgrn_lb(lb_ref)
        d1 = o_ref[lb_row:lb_row + 1, :] * (lb * (1.0 - lb))
        o_ref[T:2 * T, :] = jnp.zeros((T, W), F32)
        o_ref[T:T + 1, :] = -d1
        o_ref[T + 1:T + 2, :] = d1

    vm = pl.BlockSpec(memory_space=pltpu.VMEM)
    return pl.pallas_call(
        body, name="small_sum", in_specs=[vm, vm], out_specs=vm, out_shape=jax.ShapeDtypeStruct((2 * T, W), F32),
    )(gathered, hgrn_lb)


def _place():
    return lax.axis_index("x"), lax.axis_index("y"), lax.axis_index("c")


def _other_chips(x, y):
    return [(1 - x, y), (x, 1 - y), (1 - x, 1 - y)]


def _chunk_rows(rows, row_bytes):
    cr = rows
    while cr * row_bytes > STREAM_CHUNK_BYTES and cr % 32 == 0:
        cr //= 2
    return cr


def _stream(pairs, buf, sems, t, peer):
    lsem, ssem, rsem = sems
    n = len(pairs)
    loads, sent = [None] * n, [None] * n

    def load(k):
        slot = k % STREAM_SLOTS
        if k >= STREAM_SLOTS:
            sent[k - STREAM_SLOTS]()
        loads[k] = pltpu.make_async_copy(pairs[k][0], buf.at[slot], lsem.at[t, slot])
        loads[k].start()

    load(0)
    for k in range(n):
        slot = k % STREAM_SLOTS
        if k + 1 < n:
            load(k + 1)
        loads[k].wait()
        cp = pltpu.make_async_remote_copy(src_ref=buf.at[slot], dst_ref=pairs[k][1], send_sem=ssem.at[t, slot],
                                          recv_sem=rsem.at[t], device_id=peer, device_id_type=MESH)
        cp.start()
        sent[k] = cp.wait_send
    for k in range(max(0, n - STREAM_SLOTS), n):
        sent[k]()


def _stream_scratch(shapes):
    nt = len(shapes)
    return ([pltpu.VMEM((STREAM_SLOTS,) + s, d) for s, d in shapes]
            + [pltpu.SemaphoreType.DMA((nt, STREAM_SLOTS)), pltpu.SemaphoreType.DMA((nt, STREAM_SLOTS)),
               pltpu.SemaphoreType.DMA((nt,))])


def _exchange_halves(name, grads):
    nt = len(grads)
    hs = [g.shape[1] // 2 for g in grads]
    crs = [_chunk_rows(h, g.shape[2] * g.dtype.itemsize) for h, g in zip(hs, grads)]

    def body(*refs):
        ins, gots, bufs, sems = refs[:nt], refs[nt:2 * nt], refs[2 * nt:3 * nt], refs[3 * nt:]
        x, y, c = _place()
        sib = (x, y, 1 - c)
        for t in range(nt):
            h, cr = hs[t], crs[t]
            pairs = [(ins[t].at[b, pl.ds((1 - c) * h + r0, cr)], gots[t].at[b, pl.ds(r0, cr)])
                     for b in range(N_CHIPS) for r0 in range(0, h, cr)]
            _stream(pairs, bufs[t], sems, t, sib)
        for t in range(nt):
            pltpu.make_async_remote_copy(src_ref=gots[t], dst_ref=gots[t], send_sem=sems[1].at[t, 0],
                                         recv_sem=sems[2].at[t], device_id=sib, device_id_type=MESH).wait_recv()

    return pl.pallas_call(
        body, name=name, in_specs=[ANY] * nt, out_specs=[ANY] * nt,
        out_shape=[jax.ShapeDtypeStruct((N_CHIPS, h, g.shape[2]), g.dtype) for h, g in zip(hs, grads)],
        scratch_shapes=_stream_scratch([((cr, g.shape[2]), g.dtype) for cr, g in zip(crs, grads)]),
        compiler_params=pltpu.CompilerParams(vmem_limit_bytes=VMEM_LIMIT_BYTES),
    )(*grads)


def _scatter_plan(srcs, dsts):
    x, y, c = _place()
    me = 2 * x + y
    return [(srcs[t].at[2 * px + py], dsts[t].at[me], (px, py, c))
            for t in range(len(srcs)) for px, py in _other_chips(x, y)]


def _slot(dst, chip, r0, rows, cols):
    if len(dst.shape) == 3:
        return dst.at[chip, pl.ds(r0, rows)]
    return dst.at[pl.ds(r0, rows), pl.ds(pl.multiple_of(chip * cols, 128), cols)]


def _shard_dims(gathered):
    s = gathered.shape
    return (s[1], s[2]) if len(s) == 3 else (s[0], s[1] // N_CHIPS)


def _gather_plan(_, bufs):
    x, y, c = _place()
    me = 2 * x + y
    plan = []
    for buf in bufs:
        rows, cols = _shard_dims(buf)
        mine = _slot(buf, me, c * (rows // 2), rows // 2, cols)
        plan += [(mine, mine, (px, py, c)) for px, py in _other_chips(x, y)]
    return plan


def _cast_to_slot(name, chip, w, layer, gathered_shape, after):
    _, R, C = w.shape
    tr, tc = _ew_tiles(R, C)

    def body(c_ref, w_ref, *rest):
        rest[-1][...] = w_ref[...].astype(BF16)

    if len(gathered_shape) == 3:
        out_spec = pl.BlockSpec((None, tr, tc), lambda i, j, c: (c[0], i, j))
    else:
        out_spec = pl.BlockSpec((tr, tc), lambda i, j, c: (i, c[0] * (C // tc) + j))
    extra = [] if after is None else [after]
    return pl.pallas_call(
        body, name=name, out_shape=jax.ShapeDtypeStruct(gathered_shape, BF16),
        grid_spec=pltpu.PrefetchScalarGridSpec(
            num_scalar_prefetch=1, grid=(R // tr, C // tc),
            in_specs=[pl.BlockSpec((None, tr, tc), lambda i, j, c: (layer, i, j))] + [ANY] * len(extra),
            out_specs=out_spec),
        compiler_params=_params("parallel", "parallel"),
    )(chip, w, *extra)


HBM_SPEC = pl.BlockSpec(memory_space=pltpu.HBM)
SEM_SPEC = pl.BlockSpec(memory_space=pltpu.SEMAPHORE)


def _split_start(name, srcs, dsts, plan, ncopies, after):
    bufs = [pltpu.with_memory_space_constraint(a, pltpu.HBM) for a in list(srcs) + list(dsts)]
    nb, ns = len(bufs), len(srcs)
    operands = bufs + ([after] if after is not None else [])

    def body(*refs):
        outs = refs[len(operands):]
        send, recv, token = outs[0], outs[1], outs[-1]
        for i, (src, dst, dev) in enumerate(plan(refs[:ns], refs[ns:nb])):
            pltpu.make_async_remote_copy(src_ref=src, dst_ref=dst, send_sem=send.at[i], recv_sem=recv.at[i],
                                         device_id=dev, device_id_type=MESH).start()
        token[...] = jnp.zeros_like(token)

    res = pl.pallas_call(
        body, name=name,
        out_shape=[pltpu.SemaphoreType.DMA((ncopies,)), pltpu.SemaphoreType.DMA((ncopies,))]
        + [pltpu.HBM(a.shape, a.dtype) for a in bufs] + [jax.ShapeDtypeStruct((8, 128), F32)],
        in_specs=[HBM_SPEC] * nb + [ANY] * (len(operands) - nb),
        out_specs=[SEM_SPEC, SEM_SPEC] + [HBM_SPEC] * nb + [pl.BlockSpec(memory_space=pltpu.VMEM)],
        input_output_aliases={i: 2 + i for i in range(nb)},
        compiler_params=pltpu.CompilerParams(has_side_effects=pltpu.SideEffectType.DATAFLOW_SIDE_EFFECTING),
    )(*operands)
    return res[:-1], res[-1]


def _split_wait(name, started, plan, ns, after):
    send, recv, bufs = started[0], started[1], list(started[2:])
    nb = len(bufs)

    def body(*refs):
        send_ref, recv_ref = refs[nb], refs[nb + 1]
        for i, (src, dst, dev) in enumerate(plan(refs[:ns], refs[ns:nb])):
            cp = pltpu.make_async_remote_copy(src_ref=src, dst_ref=dst, send_sem=send_ref.at[i], recv_sem=recv_ref.at[i],
                                              device_id=dev, device_id_type=MESH)
            cp.wait_send()
            cp.wait_recv()

    res = pl.pallas_call(
        body, name=name, out_shape=[pltpu.HBM(a.shape, a.dtype) for a in bufs],
        in_specs=[HBM_SPEC] * nb + [SEM_SPEC, SEM_SPEC, ANY], out_specs=[HBM_SPEC] * nb,
        input_output_aliases={i: i for i in range(nb)},
        compiler_params=pltpu.CompilerParams(has_side_effects=pltpu.SideEffectType.DATAFLOW_SIDE_EFFECTING),
    )(*bufs, send, recv, after)
    return res[:ns], res[ns:]


def _gather_finish(name, _, gathered):
    nt = len(gathered)
    dims = [_shard_dims(g) for g in gathered]
    hs = [rows // 2 for rows, _ in dims]
    crs = [_chunk_rows(h, cols * 2) for h, (_, cols) in zip(hs, dims)]

    def body(*refs):
        outs, bufs, sems = refs[nt:2 * nt], refs[2 * nt:3 * nt], refs[3 * nt:]
        x, y, c = _place()
        sib = (x, y, 1 - c)
        for t in range(nt):
            h, cr, cols = hs[t], crs[t], dims[t][1]
            passed = [_slot(outs[t], 2 * px + py, c * h + r0, cr, cols)
                      for px, py in _other_chips(x, y) for r0 in range(0, h, cr)]
            _stream([(r, r) for r in passed], bufs[t], sems, t, sib)
        for t in range(nt):
            if len(gathered[t].shape) == 3:
                three = outs[t].at[pl.ds(0, 3), pl.ds(0, hs[t])]
            else:
                three = outs[t].at[pl.ds(0, hs[t]), pl.ds(0, 3 * dims[t][1])]
            pltpu.make_async_remote_copy(src_ref=three, dst_ref=three, send_sem=sems[1].at[t, 0],
                                         recv_sem=sems[2].at[t], device_id=sib, device_id_type=MESH).wait_recv()

    return pl.pallas_call(
        body, name=name, in_specs=[ANY] * nt, out_specs=[ANY] * nt,
        out_shape=[jax.ShapeDtypeStruct(g.shape, g.dtype) for g in gathered],
        scratch_shapes=_stream_scratch([((cr, cols), BF16) for cr, (_, cols) in zip(crs, dims)]),
        input_output_aliases={t: t for t in range(nt)},
        compiler_params=pltpu.CompilerParams(vmem_limit_bytes=VMEM_LIMIT_BYTES),
    )(*gathered)


def _sum_share(landed):
    nt = len(landed)
    hs = [a.shape[1] for a in landed]
    cs = [a.shape[2] for a in landed]
    crs = [_chunk_rows(h, 2 * c * 4) for h, c in zip(hs, cs)]
    shapes = sorted(set(zip(crs, cs)))
    which = [shapes.index(s) for s in zip(crs, cs)]

    def body(*refs):
        ins, outs = refs[:nt], refs[nt:2 * nt]
        inbufs, outbufs = refs[2 * nt:2 * nt + len(shapes)], refs[2 * nt + len(shapes):2 * nt + 2 * len(shapes)]
        lsem, ssem, osem, rsem = refs[2 * nt + 2 * len(shapes):]
        x, y, c = _place()
        sib = (x, y, 1 - c)
        for t in range(nt):
            cr, n, ib, ob = crs[t], hs[t] // crs[t], inbufs[which[t]], outbufs[which[t]]
            loads, gone = [None] * n, [None] * n

            def load(k):
                slot = k % SUM_SLOTS
                if k >= SUM_SLOTS:
                    for cp_wait in gone[k - SUM_SLOTS]:
                        cp_wait()
                loads[k] = pltpu.make_async_copy(ins[t].at[:, pl.ds(k * cr, cr)], ib.at[slot], lsem.at[t, slot])
                loads[k].start()

            load(0)
            for k in range(n):
                slot = k % SUM_SLOTS
                if k + 1 < n:
                    load(k + 1)
                loads[k].wait()
                acc = ib[slot, 0].astype(F32)
                for s in range(1, N_CHIPS):
                    acc = acc + ib[slot, s].astype(F32)
                ob[slot] = acc
                rows = outs[t].at[c, pl.ds(k * cr, cr)]
                away = pltpu.make_async_remote_copy(src_ref=ob.at[slot], dst_ref=rows, send_sem=ssem.at[t, slot],
                                                    recv_sem=rsem.at[t], device_id=sib, device_id_type=MESH)
                away.start()
                home = pltpu.make_async_copy(ob.at[slot], rows, osem.at[t, slot])
                home.start()
                gone[k] = (away.wait_send, home.wait)
            for k in range(max(0, n - SUM_SLOTS), n):
                for cp_wait in gone[k]:
                    cp_wait()
        for t in range(nt):
            other = outs[t].at[1 - c]
            pltpu.make_async_remote_copy(src_ref=other, dst_ref=other, send_sem=ssem.at[t, 0], recv_sem=rsem.at[t],
                                         device_id=sib, device_id_type=MESH).wait_recv()

    slot_sems = pltpu.SemaphoreType.DMA((nt, SUM_SLOTS))
    return pl.pallas_call(
        body, name="sum_share", in_specs=[ANY] * nt, out_specs=[ANY] * nt,
        out_shape=[jax.ShapeDtypeStruct((2, h, c), F32) for h, c in zip(hs, cs)],
        scratch_shapes=[pltpu.VMEM((SUM_SLOTS, N_CHIPS, cr, c), BF16) for cr, c in shapes]
        + [pltpu.VMEM((SUM_SLOTS, cr, c), F32) for cr, c in shapes]
        + [slot_sems, slot_sems, slot_sems, pltpu.SemaphoreType.DMA((nt,))],
        compiler_params=pltpu.CompilerParams(vmem_limit_bytes=VMEM_LIMIT_BYTES),
    )(*landed)


def _allgather_small(name, v):
    def body(v_ref, o_ref, send, recv, lsem):
        x, y, c = _place()
        me = 4 * x + 2 * y + c
        loc = pltpu.make_async_copy(v_ref, o_ref.at[me], lsem)
        loc.start()
        copies = []
        for k in range(1, N_DEV):
            px = 1 - x if k & 4 else x
            py = 1 - y if k & 2 else y
            pc = 1 - c if k & 1 else c
            cp = pltpu.make_async_remote_copy(
                src_ref=v_ref, dst_ref=o_ref.at[me], send_sem=send.at[k - 1], recv_sem=recv.at[k - 1],
                device_id=(px, py, pc), device_id_type=MESH)
            cp.start()
            copies.append(cp)
        for cp in copies:
            cp.wait()
        loc.wait()

    vm = pl.BlockSpec(memory_space=pltpu.VMEM)
    return pl.pallas_call(
        body, name=name, in_specs=[vm], out_specs=vm,
        out_shape=jax.ShapeDtypeStruct((N_DEV,) + v.shape, v.dtype),
        scratch_shapes=[pltpu.SemaphoreType.DMA((N_DEV - 1,))] * 2 + [pltpu.SemaphoreType.DMA],
    )(v)


def kernel(x, mem, norm_g, mem_norm_g, w_kv, w_out, pool_w_in, pool_w_grp, pool_scale, hgrn_w_in, hgrn_lb, hgrn_norm_g, final_g, loss_target, m_norm_g, m_mem_norm_g, m_w_kv, m_w_out, m_pool_w_in, m_pool_w_grp, m_pool_scale, m_hgrn_w_in, m_hgrn_lb, m_hgrn_norm_g, m_final_g, v_norm_g, v_mem_norm_g, v_w_kv, v_w_out, v_pool_w_in, v_pool_w_grp, v_pool_scale, v_hgrn_w_in, v_hgrn_lb, v_hgrn_norm_g, v_final_g):
    _, S, D = x.shape
    M = mem.shape[1]
    EB = 2 * D
    ECA = EB // 4
    EMIX = EB - ECA
    PG = EMIX // N_POOL_GROUPS
    NP0 = EMIX + ECA + EB
    NP1 = 3 * EMIX + ECA + EB
    SH0, SH1 = NP0 // N_CHIPS, NP1 // N_CHIPS
    DK, EK = D // N_CHIPS, EB // N_CHIPS
    TNP = 512 if all(v % 512 == 0 for v in (SH0, SH1, ECA, EMIX)) else 256
    TM = _tile(S, 1024)
    TMF = _tile(S, 2048)
    TD = _tile(D, 512)
    TDW = _tile(D, 1024)
    c0, c1 = SH0 // TNP, SH1 // TNP
    qt = EMIX // TNP
    chip = 2 * lax.axis_index("x") + lax.axis_index("y")

    xs, ms, tgt = x[0], mem[0], loss_target[0]

    sds = jax.ShapeDtypeStruct
    chip1 = chip.astype(jnp.int32).reshape(1)

    def start_gather(tag, layers, after):
        bufs = []
        for t, (w, layer) in enumerate(layers):
            w3 = w.reshape((w.shape[0], -1, w.shape[-1]))
            shape = (D, NP1) if w is hgrn_w_in else (N_CHIPS,) + w3.shape[1:]
            bufs.append(_cast_to_slot(f"cast_{tag}{t}", chip1, w3, layer, shape, after))
        return _split_start(f"gather_{tag}_start", [], bufs, _gather_plan, 3 * len(bufs), after)

    gather_a, token = start_gather("a", [(pool_w_in, 0)], None)
    gather_b, token = start_gather("b", [(w_kv, 0), (w_out, 0), (pool_w_grp, 0)], token)
    gather_c, token = start_gather("c", [(hgrn_w_in, 0)], token)
    gather_d, token = start_gather("d", [(w_kv, 1), (w_out, 1)], token)
    started = token[0:1, 0:1]

    tek, tew = _tile(EK, 512), _tile(EK, 1024)

    mem_n = _rms_fwd("rms_mem", ms, mem_norm_g.reshape(1, D) + started)
    h0 = _rms_fwd("rms0", xs, norm_g[0:1] + started)
    wpin, = _gather_finish("gather_a_finish", *_split_wait("gather_a_wait", gather_a, _gather_plan, 0, h0))

    tkw = _tile(2 * ECA, 1024)

    def kv_of(layer, wkv):
        return _matmul(
            f"kv{layer}", mem_n, wkv.reshape(D, 2 * ECA), grid=(1, 2 * ECA // tkw, 1),
            a_spec=pl.BlockSpec((M, D), lambda i, j, k: (0, 0)), b_spec=pl.BlockSpec((D, tkw), lambda i, j, k: (0, j)),
            out_shape=sds((M, 2 * ECA), BF16), out_spec=pl.BlockSpec((M, tkw), lambda i, j, k: (0, j)),
            acc_shape=(M, tkw), dims=NN)

    tko = _tile(EB, 2048)

    def out_proj(layer, branch, wout, resid):
        return _matmul(
            f"out_proj{layer}", branch, wout.reshape(EB, D), grid=(S // TM, D // TDW, EB // tko),
            a_spec=pl.BlockSpec((TM, tko), IK), b_spec=pl.BlockSpec((tko, TDW), KJ),
            out_shape=sds((S, D), F32), out_spec=pl.BlockSpec((TM, TDW), IJ),
            acc_shape=(TM, TDW), dims=NN, add=resid, add_spec=pl.BlockSpec((TM, TDW), IJ))

    ones_ca = jnp.ones((1, ECA), F32)

    proj0 = _matmul(
        "proj0", h0, wpin, grid=(S // TMF, NP0 // TNP, 1),
        a_spec=pl.BlockSpec((TMF, D), lambda i, j, k: (i, 0)),
        b_spec=pl.BlockSpec((None, D, TNP), lambda i, j, k: (j // c0, 0, j % c0)),
        out_shape=sds((S, NP0), BF16), out_spec=pl.BlockSpec((TMF, TNP), IJ),
        acc_shape=(TMF, TNP), dims=NN)
    pooled = _pool_fwd(proj0, S, EMIX)
    wkv0, wout0, g_grp = _gather_finish("gather_b_finish", *_split_wait("gather_b_wait", gather_b, _gather_plan, 0, pooled))
    wgrp = g_grp.reshape(N_CHIPS, N_POOL_GROUPS, PG // N_CHIPS, PG).transpose(1, 0, 2, 3).reshape(N_POOL_GROUPS, PG, PG)
    kv = [kv_of(0, wkv0), None]
    premix0 = _matmul(
        "pool_grp", pooled, wgrp, grid=(S // TM, N_POOL_GROUPS, 1),
        a_spec=pl.BlockSpec((TM, PG), lambda i, j, k: (i, j)),
        b_spec=pl.BlockSpec((None, PG, PG), lambda i, j, k: (j, 0, 0)),
        out_shape=sds((S, EB), BF16), out_spec=pl.BlockSpec((TM, PG), lambda i, j, k: (i, j)),
        acc_shape=(TM, PG), dims=NN)
    premix0 = _ca_fwd("ca_fwd0", proj0, EMIX // ECA, kv[0], premix0, S, ECA, EMIX)
    colscale0 = jnp.concatenate([pool_scale.reshape(1, EMIX), ones_ca], axis=1)
    gblk0 = (EMIX + ECA) // ECA
    branch0 = _gate_fwd("gate_fwd0", premix0, proj0, gblk0, colscale0, S, EB, ECA)
    x1 = out_proj(0, branch0, wout0, xs)

    whin, = _gather_finish("gather_c_finish", *_split_wait("gather_c_wait", gather_c, _gather_plan, 0, x1))
    h1 = _rms_fwd("rms1", x1, norm_g[1:2])

    def proj1_cols(name, ncols, col_of, out_cols, out_dtype, out_col_of):
        return _matmul(
            name, h1, whin, grid=(S // TMF, ncols, 1),
            a_spec=pl.BlockSpec((TMF, D), lambda i, j, k: (i, 0)),
            b_spec=pl.BlockSpec((D, TNP), lambda i, j, k: (0, col_of(j))),
            out_shape=sds((S, out_cols), out_dtype), out_spec=pl.BlockSpec((TMF, TNP), lambda i, j, k: (i, out_col_of(j))),
            acc_shape=(TMF, TNP), dims=NN)

    skip_f = lambda j: jnp.where(j < qt, j, j + qt)
    proj1 = proj1_cols("proj1", NP1 // TNP - qt, skip_f, NP1, BF16, skip_f)
    fgate = proj1_cols("proj1_f", qt, lambda j: j + qt, EMIX, F32, lambda j: j)
    premix1, rstd1, states = _hgrn_fwd(proj1, fgate, hgrn_lb, S, EMIX, EB)
    wkv1, wout1 = _gather_finish("gather_d_finish", *_split_wait("gather_d_wait", gather_d, _gather_plan, 0, rstd1))
    kv[1] = kv_of(1, wkv1)
    premix1 = _ca_fwd("ca_fwd1", proj1, 3 * EMIX // ECA, kv[1], premix1, S, ECA, EMIX)
    norm_tiles = _allgather_small("allgather_norm_g", jnp.pad(hgrn_norm_g, ((0, SMALL_ROWS - 1), (0, 0))))
    hg_norm = norm_tiles[0::2, 0, :].reshape(1, EMIX)
    colscale1 = jnp.concatenate([hg_norm, ones_ca], axis=1)
    gblk1 = (3 * EMIX + ECA) // ECA
    branch1 = _gate_fwd("gate_fwd1", premix1, proj1, gblk1, colscale1, S, EB, ECA)
    x2 = out_proj(1, branch1, wout1, x1)

    dx2, dx2b, d_final_g, loss_part = _loss_head(x2, final_g.reshape(1, D), tgt)

    def out_proj_bwd(layer, dxb, branch, wout, premix, proj, gblk, colscale, dshape, dblk):
        goff, doff = gblk * ECA // tek, dblk * ECA // tek
        dpremix, dgate, dcol = _matmul(
            f"dbranch{layer}", dxb, wout, grid=(S // TMF, EB // tek, 1),
            a_spec=pl.BlockSpec((TMF, D), lambda i, j, k: (i, 0)),
            b_spec=pl.BlockSpec((None, tek, D), lambda i, j, k: (j // (EK // tek), j % (EK // tek), 0)),
            extras=[(premix, pl.BlockSpec((TMF, tek), IJ)), (proj, pl.BlockSpec((TMF, tek), lambda i, j, k: (i, goff + j))),
                    (colscale, pl.BlockSpec((1, tek), lambda i, j, k: (0, j)))],
            epilogue=_gate_bwd_epilogue,
            out_shape=[sds((S, EB), BF16), sds(dshape, BF16), sds((S // TMF, 1, EB), F32)],
            out_spec=[pl.BlockSpec((TMF, tek), IJ), pl.BlockSpec((TMF, tek), lambda i, j, k: (i, doff + j)),
                      pl.BlockSpec((None, 1, tek), lambda i, j, k: (i, 0, j))],
            acc_shape=(TMF, tek), dims=NT)
        dw = _matmul(
            f"dwout{layer}", branch, dxb, grid=(EB // tew, D // TD, 1),
            a_spec=pl.BlockSpec((S, tew), lambda i, j, k: (0, i)), b_spec=pl.BlockSpec((S, TD), lambda i, j, k: (0, j)),
            out_shape=sds((N_CHIPS, EK, D), BF16),
            out_spec=pl.BlockSpec((None, tew, TD), lambda i, j, k: (i // (EK // tew), i % (EK // tew), j)),
            acc_shape=(tew, TD), dims=TN)
        return dpremix, dgate, dcol.reshape(S // TMF, EB), dw

    def kv_bwd(layer, dkv, wkv, dmem_add):
        dkvb = dkv.astype(BF16)
        dmem = _matmul(
            f"dmem{layer}", dkvb, wkv.reshape(D, 2 * ECA), grid=(1, D // TDW, 1),
            a_spec=pl.BlockSpec((M, 2 * ECA), lambda i, j, k: (0, 0)),
            b_spec=pl.BlockSpec((TDW, 2 * ECA), lambda i, j, k: (j, 0)),
            out_shape=sds((M, D), F32), out_spec=pl.BlockSpec((M, TDW), lambda i, j, k: (0, j)), acc_shape=(M, TDW),
            dims=NT, add=dmem_add, add_spec=pl.BlockSpec((M, TDW), lambda i, j, k: (0, j)))
        dw = _matmul(
            f"dwkv{layer}", mem_n, dkvb, grid=(D // TDW, 2 * ECA // tkw, 1),
            a_spec=pl.BlockSpec((M, TDW), lambda i, j, k: (0, i)), b_spec=pl.BlockSpec((M, tkw), lambda i, j, k: (0, j)),
            out_shape=sds((D, 2 * ECA), BF16), out_spec=pl.BlockSpec((TDW, tkw), IJ), acc_shape=(TDW, tkw), dims=TN)
        return dmem, dw.reshape(N_CHIPS, DK, 2 * ECA)

    dpremix1, drest1, dcol1, gw_out1 = out_proj_bwd(1, dx2b, branch1, wout1, premix1, proj1, gblk1, colscale1,
                                                    (S, ECA + EB), 1)
    drest1, dkv1 = _ca_bwd("ca_bwd1", dpremix1, proj1, 3 * EMIX // ECA, kv[1], drest1, 0, S, ECA, EMIX)
    dqfi, dlb = _hgrn_bwd(dpremix1, premix1, rstd1, states, proj1, fgate, hgrn_lb, S, EMIX)
    nq, nr = 3 * qt, (ECA + EB) // TNP
    tkh = _tile(EMIX, 1024) if (ECA + EB) % _tile(EMIX, 1024) == 0 else TNP
    kq = EMIX // tkh
    dh1 = _matmul(
        "dh1_qfi", dqfi, whin, grid=(S // TM, D // TDW, 3),
        a_spec=pl.BlockSpec((None, TM, EMIX), lambda i, j, k: (k, i, 0)),
        b_spec=pl.BlockSpec((TDW, EMIX), lambda i, j, k: (j, k)),
        out_shape=sds((S, D), F32), out_spec=pl.BlockSpec((TM, TDW), IJ), acc_shape=(TM, TDW), dims=NT)
    dh1 = _matmul(
        "dh1_rest", drest1, whin, grid=(S // TM, D // TDW, (ECA + EB) // tkh), a_spec=pl.BlockSpec((TM, tkh), IK),
        b_spec=pl.BlockSpec((TDW, tkh), lambda i, j, k: (j, k + 3 * kq)),
        out_shape=sds((S, D), F32), out_spec=pl.BlockSpec((TM, TDW), IJ), acc_shape=(TM, TDW), dims=NT,
        add=dh1, add_spec=pl.BlockSpec((TM, TDW), IJ))
    gw_hin = _matmul(
        "dwhin_qfi", h1, dqfi, grid=(D // TDW, nq, 1), a_spec=pl.BlockSpec((S, TDW), lambda i, j, k: (0, i)),
        b_spec=pl.BlockSpec((None, S, TNP), lambda i, j, k: (j // qt, 0, j % qt)),
        out_shape=sds((N_CHIPS, D, SH1), BF16), out_spec=pl.BlockSpec((None, TDW, TNP), lambda i, j, k: (j // c1, i, j % c1)),
        acc_shape=(TDW, TNP), dims=TN)
    gw_hin = _matmul(
        "dwhin_rest", h1, drest1, grid=(D // TDW, nr, 1), a_spec=pl.BlockSpec((S, TDW), lambda i, j, k: (0, i)),
        b_spec=pl.BlockSpec((S, TNP), lambda i, j, k: (0, j)), out_shape=sds((N_CHIPS, D, SH1), BF16),
        out_spec=pl.BlockSpec((None, TDW, TNP), lambda i, j, k: ((j + nq) // c1, i, (j + nq) % c1)),
        acc_shape=(TDW, TNP), dims=TN, alias=gw_hin)
    dmem, gw_kv1 = kv_bwd(1, dkv1, wkv1, None)

    core_chip = jnp.stack([lax.axis_index("c"), chip]).astype(jnp.int32)

    def reduce_in_chip(tag, stacks):
        got = _exchange_halves(f"exchange_halves{tag}", stacks)
        pairs = [_add_halves(f"add_halves{tag}_{t}", core_chip, g.reshape(N_CHIPS, 2, g.shape[1] // 2, g.shape[2]), r)
                 for t, (g, r) in enumerate(zip(stacks, got))]
        return [p for p, _ in pairs], [own for _, own in pairs]

    parts1, landed1 = reduce_in_chip(1, [gw_kv1, gw_out1, gw_hin])
    scatter1, token1 = _split_start("scatter1_start", parts1, landed1, _scatter_plan, 3 * len(parts1), None)
    dx1, dx1b, d_ng1 = _rms_bwd("rms_bwd1", dh1, x1, norm_g[1:2] + token1[0:1, 0:1], dx2)

    dpremix0, dproj0, dcol0, gw_out0 = out_proj_bwd(0, dx1b, branch0, wout0, premix0, proj0, gblk0, colscale0,
                                                    (S, NP0), gblk0)
    dproj0, dkv0 = _ca_bwd("ca_bwd0", dpremix0, proj0, EMIX // ECA, kv[0], dproj0, EMIX // ECA, S, ECA, EMIX)
    dmem, gw_kv0 = kv_bwd(0, dkv0, wkv0, dmem)
    parts_a, landed_a = reduce_in_chip("0a", [gw_kv0, gw_out0])
    scatter_a, token_a = _split_start("scatter0a_start", parts_a, landed_a, _scatter_plan, 3 * len(parts_a), None)
    dpooled = _matmul(
        "dpooled", dpremix0, wgrp, grid=(S // TM, N_POOL_GROUPS, 1), a_spec=pl.BlockSpec((TM, PG), IJ),
        b_spec=pl.BlockSpec((None, PG, PG), lambda i, j, k: (j, 0, 0)),
        out_shape=sds((S, EMIX), F32), out_spec=pl.BlockSpec((TM, PG), IJ), acc_shape=(TM, PG), dims=NT, after=token_a)
    dwgrp = _matmul(
        "dwgrp", pooled, dpremix0, grid=(N_POOL_GROUPS, 1, 1), a_spec=pl.BlockSpec((S, PG), lambda i, j, k: (0, i)),
        b_spec=pl.BlockSpec((S, PG), lambda i, j, k: (0, i)), out_shape=sds((N_POOL_GROUPS, PG, PG), F32),
        out_spec=pl.BlockSpec((None, PG, PG), lambda i, j, k: (i, 0, 0)), acc_shape=(PG, PG), dims=TN)
    dproj0 = _pool_bwd(dpooled, dproj0, S, EMIX)
    gw_pin = _matmul(
        "dwpin", h0, dproj0, grid=(D // TDW, NP0 // TNP, 1), a_spec=pl.BlockSpec((S, TDW), lambda i, j, k: (0, i)),
        b_spec=pl.BlockSpec((S, TNP), lambda i, j, k: (0, j)), out_shape=sds((N_CHIPS, D, SH0), BF16),
        out_spec=pl.BlockSpec((None, TDW, TNP), lambda i, j, k: (j // c0, i, j % c0)), acc_shape=(TDW, TNP), dims=TN)
    gw_grp = dwgrp.reshape(N_POOL_GROUPS, N_CHIPS, PG // N_CHIPS, PG).transpose(1, 0, 2, 3).reshape(N_CHIPS, PG, PG)
    parts_b, landed_b = reduce_in_chip("0b", [gw_pin, gw_grp.astype(BF16)])
    scatter_b, token_b = _split_start("scatter0b_start", parts_b, landed_b, _scatter_plan, 3 * len(parts_b), None)
    dh0 = _matmul(
        "dh0", dproj0, wpin, grid=(S // TM, D // TDW, N_CHIPS), a_spec=pl.BlockSpec((TM, SH0), IK),
        b_spec=pl.BlockSpec((None, TDW, SH0), lambda i, j, k: (k, j, 0)),
        out_shape=sds((S, D), F32), out_spec=pl.BlockSpec((TM, TDW), IJ), acc_shape=(TM, TDW), dims=NT, after=token_b)
    grad_x, _, d_ng0 = _rms_bwd("rms_bwd0", dh0, xs, norm_g[0:1], dx1)
    _, _, d_mng = _rms_bwd("rms_bwd_mem", dmem, ms, mem_norm_g.reshape(1, D), jnp.zeros_like(ms))

    _, landed1 = _split_wait("scatter1_wait", scatter1, _scatter_plan, len(parts1), grad_x)
    _, landed_a = _split_wait("scatter0a_wait", scatter_a, _scatter_plan, len(parts_a), grad_x)
    _, landed_b = _split_wait("scatter0b_wait", scatter_b, _scatter_plan, len(parts_b), grad_x)
    landed = [landed_a[0], landed1[0], landed_a[1], landed1[1], landed_b[0], landed_b[1], landed1[2]]
    fulls = _sum_share(landed)
    f2 = [f.reshape(-1, f.shape[-1]) for f in fulls]
    grads, deltas, new_m, new_v = {}, {}, {}, {}
    for n, w, mm, vv, gs in (("w_kv", w_kv, m_w_kv, v_w_kv, f2[0:2]), ("w_out", w_out, m_w_out, v_w_out, f2[2:4]),
                             ("pool_w_in", pool_w_in, m_pool_w_in, v_pool_w_in, f2[4:5]),
                             ("pool_w_grp", pool_w_grp, m_pool_w_grp, v_pool_w_grp, f2[5:6]),
                             ("hgrn_w_in", hgrn_w_in, m_hgrn_w_in, v_hgrn_w_in, f2[6:7])):
        as3d = lambda a: a.reshape((a.shape[0], -1, a.shape[-1]))
        outs = _adamw_layers(f"adamw_{n}", as3d(w), gs, as3d(mm), as3d(vv))
        grads[n], deltas[n], new_m[n], new_v[n] = [o.reshape(w.shape) for o in outs]

    Wd = EMIX
    partial = _pack_rows("pack_partials", [d_ng0, d_ng1, d_mng, dcol0[:, :EMIX], dlb, dcol1[:, :EMIX], d_final_g,
                                           loss_part], Wd)
    summed = _small_sum(_allgather_small("allgather_grads", partial), hgrn_lb, 4)
    row = lambda i, n=Wd: summed[i:i + 1, :n]
    nshard = EMIX // N_CHIPS
    g_hg_norm = lax.dynamic_slice_in_dim(row(5), chip * nshard, nshard, axis=1)
    small_names = ["norm_g0", "norm_g1", "mem_norm_g", "pool_scale", "hgrn_lb0", "hgrn_lb1", "hgrn_norm_g", "final_g"]
    small_w = [norm_g[0:1], norm_g[1:2], mem_norm_g.reshape(1, D), pool_scale, hgrn_lb[0:1], hgrn_lb[1:2], hgrn_norm_g,
               final_g.reshape(1, D)]
    small_m = [m_norm_g[0:1], m_norm_g[1:2], m_mem_norm_g.reshape(1, D), m_pool_scale, m_hgrn_lb[0:1], m_hgrn_lb[1:2],
               m_hgrn_norm_g, m_final_g.reshape(1, D)]
    small_v = [v_norm_g[0:1], v_norm_g[1:2], v_mem_norm_g.reshape(1, D), v_pool_scale, v_hgrn_lb[0:1], v_hgrn_lb[1:2],
               v_hgrn_norm_g, v_final_g.reshape(1, D)]
    g_pack = _pack_rows("pack_small_g", [row(0, D), row(1, D), row(2, D), row(3), row(8), row(9), g_hg_norm, row(6, D)], Wd)
    d_pack, m_pack, v_pack = _adamw("adamw_small", _pack_rows("pack_small_w", small_w, Wd), g_pack,
                                    _pack_rows("pack_small_m", small_m, Wd), _pack_rows("pack_small_v", small_v, Wd))
    widths = [v.shape[1] for v in small_w]
    rows = lambda p: {n: p[i, :widths[i]] for i, n in enumerate(small_names)}

    def assemble(r, out):
        out["norm_g"] = jnp.stack([r["norm_g0"], r["norm_g1"]])
        out["mem_norm_g"] = r["mem_norm_g"]
        out["pool_scale"] = r["pool_scale"].reshape(1, EMIX)
        out["hgrn_lb"] = jnp.stack([r["hgrn_lb0"], r["hgrn_lb1"]])
        out["hgrn_norm_g"] = r["hgrn_norm_g"].reshape(1, nshard)
        out["final_g"] = r["final_g"]

    assemble(rows(g_pack), grads)
    assemble(rows(d_pack), deltas)
    assemble(rows(m_pack), new_m)
    assemble(rows(v_pack), new_v)
    loss = summed[7, 0]

    order = ["norm_g", "mem_norm_g", "w_kv", "w_out", "pool_w_in", "pool_w_grp", "pool_scale", "hgrn_w_in", "hgrn_lb",
             "hgrn_norm_g", "final_g"]
    return (loss, grad_x.reshape(1, S, D), *[grads[n] for n in order], *[deltas[n] for n in order],
            *[new_m[n] for n in order], *[new_v[n] for n in order])
```

```python
import functools

import jax
import jax.numpy as jnp
from jax import lax
from jax.experimental import pallas as pl
from jax.experimental.pallas import tpu as pltpu

F32 = jnp.float32
BF16 = jnp.bfloat16
MESH = pl.DeviceIdType.MESH
ANY = pl.BlockSpec(memory_space=pl.ANY)

EPS = 1e-6
HG_HEAD_DIM = 128
HG_CHUNK = 64
CA_HEADS = 4
N_POOL_GROUPS = 4
POOL_HALO = 128
ADAM_LR = 0.001
ADAM_B1 = 0.9
ADAM_B2 = 0.999
ADAM_EPS = 1e-08
ADAM_WD = 0.01
ADAM_STEP = 10
N_CHIPS = 4
N_DEV = 8
VMEM_LIMIT_BYTES = 56 * 1024 * 1024
SMALL_ROWS = 8
STREAM_CHUNK_BYTES = 2 * 1024 * 1024
STREAM_SLOTS = 4
SUM_SLOTS = 2


def _params(*sem):
    return pltpu.CompilerParams(dimension_semantics=sem, vmem_limit_bytes=VMEM_LIMIT_BYTES)


def _tile(n, pref):
    t = pref
    while n % t:
        t //= 2
    return t


def _sigmoid(x):
    return 1.0 / (1.0 + jnp.exp(-x))


def _matmul(name, a, b, *, grid, a_spec, b_spec, out_shape, out_spec, acc_shape, dims,
            add=None, add_spec=None, alias=None, after=None, extras=(), epilogue=None):
    nk = grid[2]
    has_add = add is not None
    has_alias = alias is not None
    has_after = after is not None
    n_out = len(out_shape) if epilogue is not None else 1

    def body(*refs):
        a_ref, b_ref = refs[0], refs[1]
        pos = 2
        add_ref = None
        if has_add:
            add_ref = refs[pos]
            pos += 1
        extra_refs = refs[pos:pos + len(extras)]
        pos += len(extras) + has_alias + has_after
        o_refs = refs[pos:pos + n_out]
        prod = lax.dot_general(a_ref[...], b_ref[...], (dims, ((), ())), preferred_element_type=F32)

        def finish(r):
            if epilogue is not None:
                epilogue(r, extra_refs, o_refs)
                return
            if has_add:
                r = r + add_ref[...].astype(F32)
            o_refs[0][...] = r.astype(o_refs[0].dtype)

        if nk == 1:
            finish(prod)
            return
        acc_ref = refs[pos + n_out]
        k = pl.program_id(2)

        @pl.when(k == 0)
        def _():
            acc_ref[...] = prod

        @pl.when(k > 0)
        def _():
            acc_ref[...] += prod

        @pl.when(k == nk - 1)
        def _():
            finish(acc_ref[...])

    operands = [a, b]
    in_specs = [a_spec, b_spec]
    if has_add:
        operands.append(add)
        in_specs.append(add_spec)
    for arr, spec in extras:
        operands.append(arr)
        in_specs.append(spec)
    aliases = {}
    if has_alias:
        aliases = {len(operands): 0}
        operands.append(alias)
        in_specs.append(ANY)
    if has_after:
        operands.append(after)
        in_specs.append(ANY)
    return pl.pallas_call(
        body, name=name, grid=grid, in_specs=in_specs, out_specs=out_spec, out_shape=out_shape,
        scratch_shapes=[pltpu.VMEM(acc_shape, F32)] if nk > 1 else [], input_output_aliases=aliases,
        compiler_params=_params("parallel", "parallel", "arbitrary"),
    )(*operands)


IJ = lambda i, j, k: (i, j)
IK = lambda i, j, k: (i, k)
KJ = lambda i, j, k: (k, j)
KI = lambda i, j, k: (k, i)
NN = ((1,), (0,))
NT = ((1,), (1,))
TN = ((0,), (0,))


def _rms_fwd(name, x, g):
    R, D = x.shape
    tr = _tile(R, 256)

    def body(x_ref, g_ref, o_ref):
        xf = x_ref[...]
        r = lax.rsqrt(jnp.mean(xf * xf, axis=-1, keepdims=True) + EPS)
        o_ref[...] = (xf * r * g_ref[...]).astype(o_ref.dtype)

    return pl.pallas_call(
        body, name=name, grid=(R // tr,),
        in_specs=[pl.BlockSpec((tr, D), lambda i: (i, 0)), pl.BlockSpec((1, D), lambda i: (0, 0))],
        out_specs=pl.BlockSpec((tr, D), lambda i: (i, 0)),
        out_shape=jax.ShapeDtypeStruct((R, D), BF16), compiler_params=_params("parallel"),
    )(x, g)


def _rms_bwd(name, dh, x, g, dres):
    R, D = x.shape
    tr = _tile(R, 256)

    def body(dh_ref, x_ref, g_ref, dres_ref, dx_ref, dxb_ref, dg_ref):
        xf = x_ref[...]
        r = lax.rsqrt(jnp.mean(xf * xf, axis=-1, keepdims=True) + EPS)
        xn = xf * r
        d = dh_ref[...]
        dyg = d * g_ref[...]
        dx = r * (dyg - xn * jnp.mean(dyg * xn, axis=-1, keepdims=True)) + dres_ref[...]
        dx_ref[...] = dx
        dxb_ref[...] = dx.astype(BF16)

        @pl.when(pl.program_id(0) == 0)
        def _():
            dg_ref[...] = jnp.zeros_like(dg_ref)

        dg_ref[...] += jnp.sum(d * xn, axis=0, keepdims=True)

    row = pl.BlockSpec((tr, D), lambda i: (i, 0))
    vec = pl.BlockSpec((1, D), lambda i: (0, 0))
    return pl.pallas_call(
        body, name=name, grid=(R // tr,), in_specs=[row, row, vec, row], out_specs=[row, row, vec],
        out_shape=[jax.ShapeDtypeStruct((R, D), F32), jax.ShapeDtypeStruct((R, D), BF16),
                   jax.ShapeDtypeStruct((1, D), F32)],
        compiler_params=_params("arbitrary"),
    )(dh, x, g, dres)


def _loss_head(x2, g, target):
    R, D = x2.shape
    tr = _tile(R, 256)

    def body(x_ref, g_ref, t_ref, dx_ref, dxb_ref, dg_ref, loss_ref):
        xf = x_ref[...]
        gg = g_ref[...]
        r = lax.rsqrt(jnp.mean(xf * xf, axis=-1, keepdims=True) + EPS)
        xn = xf * r
        e = xn * gg - t_ref[...]
        part = 0.5 * jnp.sum(jnp.mean(e * e, axis=-1, keepdims=True), axis=0, keepdims=True)
        dy = e * (1.0 / D)
        dyg = dy * gg
        dx = r * (dyg - xn * jnp.mean(dyg * xn, axis=-1, keepdims=True))
        dx_ref[...] = dx
        dxb_ref[...] = dx.astype(BF16)

        @pl.when(pl.program_id(0) == 0)
        def _():
            dg_ref[...] = jnp.zeros_like(dg_ref)
            loss_ref[...] = jnp.zeros_like(loss_ref)

        dg_ref[...] += jnp.sum(dy * xn, axis=0, keepdims=True)
        loss_ref[...] += jnp.broadcast_to(part, loss_ref.shape)

    row = pl.BlockSpec((tr, D), lambda i: (i, 0))
    vec = pl.BlockSpec((1, D), lambda i: (0, 0))
    return pl.pallas_call(
        body, name="loss_head", grid=(R // tr,), in_specs=[row, vec, row],
        out_specs=[row, row, vec, pl.BlockSpec((1, 128), lambda i: (0, 0))],
        out_shape=[jax.ShapeDtypeStruct((R, D), F32), jax.ShapeDtypeStruct((R, D), BF16),
                   jax.ShapeDtypeStruct((1, D), F32), jax.ShapeDtypeStruct((1, 128), F32)],
        compiler_params=_params("arbitrary"),
    )(x2, g, target)


def _pool_band(tr, reverse, w):
    r = lax.broadcasted_iota(jnp.int32, (tr, tr + POOL_HALO), 0)
    c = lax.broadcasted_iota(jnp.int32, (tr, tr + POOL_HALO), 1)
    if reverse:
        inside = (c >= r) & (c < r + w)
    else:
        cc = c - POOL_HALO
        inside = (cc <= r) & (cc > r - w)
    return jnp.where(inside, 1.0, 0.0).astype(BF16)


def _pool_fwd(proj, S, EMIX):
    PG = EMIX // N_POOL_GROUPS
    cb = PG
    tr = _tile(S, 512)
    per_group = PG // cb

    def body(u_ref, o_ref, ext):
        i = pl.program_id(1)
        w = jnp.left_shift(2, pl.program_id(0) // per_group)

        @pl.when(i == 0)
        def _():
            ext[0:POOL_HALO, :] = jnp.zeros((POOL_HALO, cb), BF16)

        u = u_ref[...]
        ext[POOL_HALO:, :] = u
        win = jnp.dot(_pool_band(tr, False, w), ext[...], preferred_element_type=F32)
        pos = i * tr + lax.broadcasted_iota(jnp.int32, (tr, 1), 0)
        cnt = jnp.minimum(pos + 1, w).astype(F32)
        o_ref[...] = (win / cnt - u.astype(F32)).astype(BF16)
        ext[0:POOL_HALO, :] = u[tr - POOL_HALO:, :]

    return pl.pallas_call(
        body, name="pool_fwd", grid=(EMIX // cb, S // tr),
        in_specs=[pl.BlockSpec((tr, cb), lambda j, i: (i, j))],
        out_specs=pl.BlockSpec((tr, cb), lambda j, i: (i, j)),
        out_shape=jax.ShapeDtypeStruct((S, EMIX), BF16),
        scratch_shapes=[pltpu.VMEM((tr + POOL_HALO, cb), BF16)],
        compiler_params=_params("parallel", "arbitrary"),
    )(proj)


def _pool_bwd(dpooled, dproj, S, EMIX):
    PG = EMIX // N_POOL_GROUPS
    cb = PG
    tr = _tile(S, 512)
    per_group = PG // cb
    nrt = S // tr

    def body(d_ref, _, o_ref, ext):
        step = pl.program_id(1)
        i = nrt - 1 - step
        w = jnp.left_shift(2, pl.program_id(0) // per_group)

        @pl.when(step == 0)
        def _():
            ext[tr:, :] = jnp.zeros((POOL_HALO, cb), BF16)

        d = d_ref[...]
        pos = i * tr + lax.broadcasted_iota(jnp.int32, (tr, 1), 0)
        cnt = jnp.minimum(pos + 1, w).astype(F32)
        z = (d / cnt).astype(BF16)
        ext[0:tr, :] = z
        win = jnp.dot(_pool_band(tr, True, w), ext[...], preferred_element_type=F32)
        o_ref[...] = (win - d).astype(BF16)
        ext[tr:, :] = z[0:POOL_HALO, :]

    return pl.pallas_call(
        body, name="pool_bwd", grid=(EMIX // cb, nrt),
        in_specs=[pl.BlockSpec((tr, cb), lambda j, s: (nrt - 1 - s, j)), ANY],
        out_specs=pl.BlockSpec((tr, cb), lambda j, s: (nrt - 1 - s, j)),
        out_shape=jax.ShapeDtypeStruct(dproj.shape, dproj.dtype),
        scratch_shapes=[pltpu.VMEM((tr + POOL_HALO, cb), BF16)],
        input_output_aliases={1: 0},
        compiler_params=_params("parallel", "arbitrary"),
    )(dpooled, dproj)


def _ca_fwd(name, proj, qblk, kv, premix, S, ECA, EMIX):
    M = kv.shape[0]
    hd = ECA // CA_HEADS
    ts = _tile(S, 512)
    scale = hd ** -0.5

    def body(q_ref, kv_ref, _, o_ref):
        for h in range(CA_HEADS):
            q = q_ref[:, h * hd:(h + 1) * hd]
            k = kv_ref[:, h * hd:(h + 1) * hd]
            v = kv_ref[:, ECA + h * hd:ECA + (h + 1) * hd]
            s = lax.dot_general(q, k, (NT, ((), ())), preferred_element_type=F32) * scale
            s = s - jnp.max(s, axis=-1, keepdims=True)
            p = jnp.exp(s)
            p = p / jnp.sum(p, axis=-1, keepdims=True)
            o = jnp.dot(p.astype(BF16), v, preferred_element_type=F32)
            o_ref[:, h * hd:(h + 1) * hd] = o.astype(BF16)

    return pl.pallas_call(
        body, name=name, grid=(S // ts,),
        in_specs=[pl.BlockSpec((ts, ECA), lambda i: (i, qblk)), pl.BlockSpec((M, 2 * ECA), lambda i: (0, 0)), ANY],
        out_specs=pl.BlockSpec((ts, ECA), lambda i: (i, EMIX // ECA)),
        out_shape=jax.ShapeDtypeStruct(premix.shape, premix.dtype),
        input_output_aliases={2: 0}, compiler_params=_params("parallel"),
    )(proj, kv, premix)


def _ca_bwd(name, dpremix, proj, qblk, kv, dbuf, dblk, S, ECA, EMIX):
    M = kv.shape[0]
    hd = ECA // CA_HEADS
    ts = _tile(S, 512)
    scale = hd ** -0.5

    def body(do_ref, q_ref, kv_ref, _, dq_ref, dkv_ref):
        @pl.when(pl.program_id(0) == 0)
        def _():
            dkv_ref[...] = jnp.zeros_like(dkv_ref)

        for h in range(CA_HEADS):
            lo, hi = h * hd, (h + 1) * hd
            q = q_ref[:, lo:hi]
            k = kv_ref[:, lo:hi]
            v = kv_ref[:, ECA + lo:ECA + hi]
            do = do_ref[:, lo:hi]
            s = lax.dot_general(q, k, (NT, ((), ())), preferred_element_type=F32) * scale
            s = s - jnp.max(s, axis=-1, keepdims=True)
            p = jnp.exp(s)
            p = p / jnp.sum(p, axis=-1, keepdims=True)
            pb = p.astype(BF16)
            dkv_ref[:, ECA + lo:ECA + hi] += lax.dot_general(pb, do, (TN, ((), ())), preferred_element_type=F32)
            dp = lax.dot_general(do, v, (NT, ((), ())), preferred_element_type=F32)
            ds = (p * (dp - jnp.sum(p * dp, axis=-1, keepdims=True)) * scale).astype(BF16)
            dq_ref[:, lo:hi] = jnp.dot(ds, k, preferred_element_type=F32).astype(BF16)
            dkv_ref[:, lo:hi] += lax.dot_general(ds, q, (TN, ((), ())), preferred_element_type=F32)

    return pl.pallas_call(
        body, name=name, grid=(S // ts,),
        in_specs=[pl.BlockSpec((ts, ECA), lambda i: (i, EMIX // ECA)), pl.BlockSpec((ts, ECA), lambda i: (i, qblk)),
                  pl.BlockSpec((M, 2 * ECA), lambda i: (0, 0)), ANY],
        out_specs=[pl.BlockSpec((ts, ECA), lambda i: (i, dblk)), pl.BlockSpec((M, 2 * ECA), lambda i: (0, 0))],
        out_shape=[jax.ShapeDtypeStruct(dbuf.shape, dbuf.dtype), jax.ShapeDtypeStruct((M, 2 * ECA), F32)],
        input_output_aliases={3: 0}, compiler_params=_params("arbitrary"),
    )(dpremix, proj, kv, dbuf)


def _gate_fwd(name, premix, proj, gblk, colscale, S, EB, ECA):
    ts = _tile(S, 512)

    def body(p_ref, g_ref, c_ref, o_ref):
        g = g_ref[...].astype(F32)
        o_ref[...] = (p_ref[...].astype(F32) * c_ref[...] * (g * _sigmoid(g))).astype(BF16)

    return pl.pallas_call(
        body, name=name, grid=(S // ts, EB // ECA),
        in_specs=[pl.BlockSpec((ts, ECA), lambda i, j: (i, j)), pl.BlockSpec((ts, ECA), lambda i, j: (i, gblk + j)),
                  pl.BlockSpec((1, ECA), lambda i, j: (0, j))],
        out_specs=pl.BlockSpec((ts, ECA), lambda i, j: (i, j)),
        out_shape=jax.ShapeDtypeStruct((S, EB), BF16), compiler_params=_params("parallel", "parallel"),
    )(premix, proj, colscale)


def _gate_bwd_epilogue(db, extra_refs, out_refs):
    p_ref, g_ref, c_ref = extra_refs
    dp_ref, dg_ref, dc_ref = out_refs
    g = g_ref[...].astype(F32)
    sg = _sigmoid(g)
    si = g * sg
    c = c_ref[...]
    t = db * p_ref[...].astype(F32)
    dp_ref[...] = (db * si * c).astype(BF16)
    dg_ref[...] = (t * c * (sg * (1.0 + g * (1.0 - sg)))).astype(BF16)
    dc_ref[...] = jnp.sum(t * si, axis=0, keepdims=True)


def _hgrn_lb(lb_ref):
    l0 = lb_ref[0:1, :]
    l1 = lb_ref[1:2, :]
    mx = jnp.maximum(l0, l1)
    e0 = jnp.exp(l0 - mx)
    e1 = jnp.exp(l1 - mx)
    return e1 / (e0 + e1)


def _bdot(a, b, ca, cb):
    return lax.dot_general(a, b, (((ca,), (cb,)), ((0,), (0,))), preferred_element_type=F32)


def _tri_sum(tri, x):
    hi = x.astype(BF16)
    lo = (x - hi.astype(F32)).astype(BF16)
    tri = tri.astype(BF16)
    return _bdot(tri, hi, 2, 1) + _bdot(tri, lo, 2, 1)


def _hgrn_chunks(qin, fin, lbh, n):
    C = HG_CHUNK
    row = lax.broadcasted_iota(jnp.int32, (n, C, C), 1)
    col = lax.broadcasted_iota(jnp.int32, (n, C, C), 2)
    causal = row >= col
    sg = _sigmoid(fin)
    f = lbh + (1.0 - lbh) * sg
    k = 1.0 - f
    g = jnp.log(f)
    b = _tri_sum(jnp.where(causal, 1.0, 0.0), g)
    b_last = jnp.sum(g, axis=1, keepdims=True)
    eb = jnp.exp(b)
    einv = jnp.exp(-b)
    eend = jnp.exp(b_last - b)
    sq = _sigmoid(qin)
    a = qin * sq * (HG_HEAD_DIM ** -0.5) * eb
    bm = k * einv
    e = k * eend
    d = jnp.exp(b_last)
    p = jnp.where(causal, _bdot(a.astype(BF16), bm.astype(BF16), 2, 2), 0.0)
    return dict(causal=causal, sg=sg, f=f, eb=eb, einv=einv, eend=eend, sq=sq, a=a, bm=bm, e=e, d=d, p=p)


def _hgrn_fwd(proj, fgate, hgrn_lb, S, EMIX, EB):
    HD, C = HG_HEAD_DIM, HG_CHUNK
    HH = EMIX // HD
    hb = 6 if HH % 6 == 0 else 1
    W = hb * HD
    tr = _tile(S, 512)
    n = tr // C

    def body(q_ref, f_ref, i_ref, lb_ref, o_ref, rstd_ref, st_ref, state):
        @pl.when(pl.program_id(1) == 0)
        def _():
            state[...] = jnp.zeros_like(state)

        lb = _hgrn_lb(lb_ref)
        for h in range(hb):
            cs = slice(h * HD, (h + 1) * HD)
            qin = q_ref[:, cs].astype(F32).reshape(n, C, HD)
            fin = f_ref[:, cs].reshape(n, C, HD)
            v = i_ref[:, cs].reshape(n, C, HD)
            t = _hgrn_chunks(qin, fin, lb[:, cs], n)
            upd = _bdot(v, t["e"].astype(BF16), 1, 1)
            st = state[h]
            for c in range(n):
                st_ref[h, c] = st
                st = st * t["d"][c] + upd[c]
            state[h] = st
            o = _bdot(t["p"].astype(BF16), v, 2, 1) + _bdot(t["a"].astype(BF16), st_ref[h].astype(BF16), 2, 2)
            rstd = lax.rsqrt(jnp.mean(o * o, axis=-1, keepdims=True) + EPS)
            o_ref[:, cs] = (o * rstd).reshape(tr, HD).astype(BF16)
            rstd_ref[:, cs] = jnp.broadcast_to(rstd, (n, C, HD)).reshape(tr, HD)

    blk = lambda off: pl.BlockSpec((tr, W), lambda g, i: (i, off + g))
    return pl.pallas_call(
        body, name="hgrn_fwd", grid=(HH // hb, S // tr),
        in_specs=[blk(0), blk(0), blk(2 * EMIX // W), pl.BlockSpec((2, W), lambda g, i: (0, g))],
        out_specs=[blk(0), blk(0), pl.BlockSpec((hb, n, HD, HD), lambda g, i: (g, i, 0, 0))],
        out_shape=[jax.ShapeDtypeStruct((S, EB), BF16), jax.ShapeDtypeStruct((S, EMIX), F32),
                   jax.ShapeDtypeStruct((HH, S // C, HD, HD), F32)],
        scratch_shapes=[pltpu.VMEM((hb, HD, HD), F32)],
        compiler_params=_params("parallel", "arbitrary"),
    )(proj, fgate, proj, hgrn_lb)


def _hgrn_bwd(dpremix, premix, rstd, states, proj, fgate, hgrn_lb, S, EMIX):
    HD, C = HG_HEAD_DIM, HG_CHUNK
    HH = EMIX // HD
    hb = 6 if HH % 6 == 0 else 1
    W = hb * HD
    tr = _tile(S, 512)
    n = tr // C
    nrt = S // tr

    def body(do_ref, on_ref, rstd_ref, st_ref, q_ref, f_ref, i_ref, lb_ref, d_ref, dlb_ref, dstate, dsbuf):
        @pl.when(pl.program_id(1) == 0)
        def _():
            dstate[...] = jnp.zeros_like(dstate)
            dlb_ref[...] = jnp.zeros_like(dlb_ref)

        lb = _hgrn_lb(lb_ref)
        for h in range(hb):
            cs = slice(h * HD, (h + 1) * HD)
            qin = q_ref[:, cs].astype(F32).reshape(n, C, HD)
            fin = f_ref[:, cs].reshape(n, C, HD)
            v = i_ref[:, cs].reshape(n, C, HD)
            lbh = lb[:, cs]
            t = _hgrn_chunks(qin, fin, lbh, n)
            a, bm, e, d, p = t["a"], t["bm"], t["e"], t["d"], t["p"]
            ab, bmb, eb16 = a.astype(BF16), bm.astype(BF16), e.astype(BF16)
            on = on_ref[:, cs].astype(F32).reshape(n, C, HD)
            dn = do_ref[:, cs].astype(F32).reshape(n, C, HD)
            do = rstd_ref[:, cs].reshape(n, C, HD) * (dn - on * jnp.mean(dn * on, axis=-1, keepdims=True))
            dob = do.astype(BF16)
            grow = _bdot(dob, ab, 1, 1)
            ds = dstate[h]
            for c in reversed(range(n)):
                dsbuf[h, c] = ds
                ds = ds * d[c] + grow[c]
            dstate[h] = ds
            dst = dsbuf[h]
            st = st_ref[h]
            dstb = dst.astype(BF16)
            dp = jnp.where(t["causal"], _bdot(dob, v, 2, 2), 0.0).astype(BF16)
            dv = _bdot(p.astype(BF16), dob, 1, 1) + _bdot(eb16, dstb, 2, 2)
            da = _bdot(dp, bmb, 2, 1) + _bdot(dob, st.astype(BF16), 2, 1)
            dbm = _bdot(dp, ab, 1, 1)
            de = _bdot(v, dstb, 2, 1)
            dd = jnp.sum(dst * st, axis=1, keepdims=True)
            dk = dbm * t["einv"] + de * t["eend"]
            dee = de * e
            db = da * a - dbm * bm - dee
            extra = jnp.sum(dee, axis=1, keepdims=True) + dd * d
            upper = jnp.where(lax.broadcasted_iota(jnp.int32, (n, C, C), 2)
                              >= lax.broadcasted_iota(jnp.int32, (n, C, C), 1), 1.0, 0.0)
            dg = _tri_sum(upper, db) + extra
            df = dg / t["f"] - dk
            sg, sq = t["sg"], t["sq"]
            dq = da * t["eb"] * (HD ** -0.5) * (sq * (1.0 + qin * (1.0 - sq)))
            d_ref[0, :, cs] = dq.reshape(tr, HD).astype(BF16)
            d_ref[1, :, cs] = (df * (1.0 - lbh) * sg * (1.0 - sg)).reshape(tr, HD).astype(BF16)
            d_ref[2, :, cs] = dv.reshape(tr, HD).astype(BF16)
            dlb_ref[:, cs] += jnp.sum((df * (1.0 - sg)).reshape(tr, HD), axis=0, keepdims=True)

    rev = lambda off: pl.BlockSpec((tr, W), lambda g, s: (nrt - 1 - s, off + g))
    return pl.pallas_call(
        body, name="hgrn_bwd", grid=(HH // hb, nrt),
        in_specs=[rev(0), rev(0), rev(0), pl.BlockSpec((hb, n, HD, HD), lambda g, s: (g, nrt - 1 - s, 0, 0)),
                  rev(0), rev(0), rev(2 * EMIX // W), pl.BlockSpec((2, W), lambda g, s: (0, g))],
        out_specs=[pl.BlockSpec((3, tr, W), lambda g, s: (0, nrt - 1 - s, g)), pl.BlockSpec((1, W), lambda g, s: (0, g))],
        out_shape=[jax.ShapeDtypeStruct((3, S, EMIX), BF16), jax.ShapeDtypeStruct((1, EMIX), F32)],
        scratch_shapes=[pltpu.VMEM((hb, HD, HD), F32), pltpu.VMEM((hb, n, HD, HD), F32)],
        compiler_params=_params("parallel", "arbitrary"),
    )(dpremix, premix, rstd, states, proj, fgate, proj, hgrn_lb)


EW_BLOCK_ELEMS = 512 * 1024


def _ew_tiles(R, C):
    tc = C if C <= 4096 else _tile(C, 2048)
    tr = _tile(R, 512)
    while tr * tc > EW_BLOCK_ELEMS and tr % 16 == 0:
        tr //= 2
    return tr, tc


def _add_halves(name, core_chip, grad, got):
    _, _, R, C = grad.shape
    tr, tc = _ew_tiles(R, C)

    def body(c_ref, a_ref, b_ref, o_ref, own_ref):
        r = (a_ref[...].astype(F32) + b_ref[...].astype(F32)).astype(BF16)
        o_ref[...] = r

        @pl.when(pl.program_id(2) == c_ref[1])
        def _():
            own_ref[...] = r

    blk = pl.BlockSpec((None, tr, tc), lambda i, j, s, c: (s, i, j))
    sds = jax.ShapeDtypeStruct(got.shape, BF16)
    return pl.pallas_call(
        body, name=name, out_shape=[sds, sds],
        grid_spec=pltpu.PrefetchScalarGridSpec(
            num_scalar_prefetch=1, grid=(R // tr, C // tc, N_CHIPS),
            in_specs=[pl.BlockSpec((None, None, tr, tc), lambda i, j, s, c: (s, c[0], i, j)), blk],
            out_specs=[blk, pl.BlockSpec((None, tr, tc), lambda i, j, s, c: (c[1], i, j))]),
        compiler_params=_params("parallel", "parallel", "arbitrary"),
    )(core_chip, grad, got)


def _adam_step(w, g, m, v):
    mn = ADAM_B1 * m + (1.0 - ADAM_B1) * g
    vn = ADAM_B2 * v + (1.0 - ADAM_B2) * (g * g)
    m_hat = mn / (1.0 - ADAM_B1 ** ADAM_STEP)
    v_hat = vn / (1.0 - ADAM_B2 ** ADAM_STEP)
    return -ADAM_LR * (m_hat / (jnp.sqrt(v_hat) + ADAM_EPS) + ADAM_WD * w), mn, vn


def _adamw(name, w, g, m, v):
    R, C = w.shape
    tr, tc = _ew_tiles(R, C)

    def body(w_ref, g_ref, m_ref, v_ref, d_ref, mo_ref, vo_ref):
        d_ref[...], mo_ref[...], vo_ref[...] = _adam_step(w_ref[...], g_ref[...], m_ref[...], v_ref[...])

    blk = pl.BlockSpec((tr, tc), lambda i, j: (i, j))
    sds = jax.ShapeDtypeStruct((R, C), F32)
    return pl.pallas_call(
        body, name=name, grid=(R // tr, C // tc), in_specs=[blk] * 4, out_specs=[blk] * 3, out_shape=[sds] * 3,
        compiler_params=_params("parallel", "parallel"),
    )(w, g, m, v)


def _adamw_layers(name, w, gs, m, v):
    L, R, C = w.shape
    tr, tc = _ew_tiles(R, C)

    def body(*refs):
        w_ref, m_ref, v_ref = refs[:3]
        g_refs = refs[3:3 + L]
        go_ref, d_ref, mo_ref, vo_ref = refs[3 + L:]
        layer = pl.program_id(0)
        g = g_refs[0][...]
        for n in range(1, L):
            g = jnp.where(layer == n, g_refs[n][...], g)
        go_ref[...] = g
        d_ref[...], mo_ref[...], vo_ref[...] = _adam_step(w_ref[...], g, m_ref[...], v_ref[...])

    blk = pl.BlockSpec((None, tr, tc), lambda l, i, j: (l, i, j))
    of_layer = lambda n: pl.BlockSpec((tr, tc), lambda l, i, j: (jnp.where(l == n, i, 0), jnp.where(l == n, j, 0)))
    sds = jax.ShapeDtypeStruct((L, R, C), F32)
    return pl.pallas_call(
        body, name=name, grid=(L, R // tr, C // tc), in_specs=[blk] * 3 + [of_layer(n) for n in range(L)],
        out_specs=[blk] * 4, out_shape=[sds] * 4, compiler_params=_params("parallel", "parallel", "parallel"),
    )(w, m, v, *gs)


def _pack_rows(name, items, W):
    nv = len(items)
    first, row = [], 0
    for a, add, _ in items:
        first.append(row)
        row += 1 if add else a.shape[0]
    assert row <= SMALL_ROWS

    def body(*refs):
        o_ref = refs[nv]
        o_ref[...] = jnp.zeros_like(o_ref)
        for i, (a, add, width) in enumerate(items):
            val = refs[i][:, 0:width]
            if add:
                val = jnp.sum(val, axis=0, keepdims=True)
            o_ref[first[i]:first[i] + val.shape[0], 0:width] = val

    vm = pl.BlockSpec(memory_space=pltpu.VMEM)
    return pl.pallas_call(
        body, name=name, in_specs=[vm] * nv, out_specs=vm, out_shape=jax.ShapeDtypeStruct((SMALL_ROWS, W), F32),
    )(*[a for a, _, _ in items])


def _unpack_rows(name, pack, layout):
    def body(p_ref, *o_refs):
        row = 0
        for o_ref, (k, n) in zip(o_refs, layout):
            o_ref[...] = p_ref[row:row + k, 0:n]
            row += k

    vm = pl.BlockSpec(memory_space=pltpu.VMEM)
    return pl.pallas_call(
        body, name=name, in_specs=[vm], out_specs=[vm] * len(layout),
        out_shape=[jax.ShapeDtypeStruct(s, F32) for s in layout],
    )(pack)


def _small_sum(gathered, hgrn_lb, chip, nshard):
    _, T, W = gathered.shape

    def body(c_ref, g_ref, lb_ref, o_ref, loss_ref, tmp):
        acc = g_ref[0]
        for dev in range(1, N_DEV):
            acc = acc + g_ref[dev]
        tmp[...] = acc
        lb = _hgrn_lb(lb_ref)
        d1 = tmp[4:5, :] * (lb * (1.0 - lb))
        o_ref[...] = jnp.zeros_like(o_ref)
        o_ref[0:4, :] = tmp[0:4, :]
        o_ref[4:5, :] = -d1
        o_ref[5:6, :] = d1
        mine = tmp[5:6, 0:nshard]
        for b in range(1, N_CHIPS):
            mine = jnp.where(c_ref[0] == b, tmp[5:6, b * nshard:(b + 1) * nshard], mine)
        o_ref[6:7, 0:nshard] = mine
        o_ref[7:8, :] = tmp[6:7, :]
        loss_ref[...] = tmp[7:8, 0:128]

    vm = pl.BlockSpec(memory_space=pltpu.VMEM)
    return pl.pallas_call(
        body, name="small_sum", in_specs=[pl.BlockSpec(memory_space=pltpu.SMEM), vm, vm], out_specs=[vm, vm],
        out_shape=[jax.ShapeDtypeStruct((T, W), F32), jax.ShapeDtypeStruct((1, 128), F32)],
        scratch_shapes=[pltpu.VMEM((T, W), F32)],
    )(chip, gathered, hgrn_lb)


def _place():
    return lax.axis_index("x"), lax.axis_index("y"), lax.axis_index("c")


def _other_chips(x, y):
    return [(1 - x, y), (x, 1 - y), (1 - x, 1 - y)]


def _chunk_rows(rows, row_bytes):
    cr = rows
    while cr * row_bytes > STREAM_CHUNK_BYTES and cr % 32 == 0:
        cr //= 2
    return cr


def _stream(pairs, buf, sems, t, peer):
    lsem, ssem, rsem = sems
    n = len(pairs)
    loads, sent = [None] * n, [None] * n

    def load(k):
        slot = k % STREAM_SLOTS
        if k >= STREAM_SLOTS:
            sent[k - STREAM_SLOTS]()
        loads[k] = pltpu.make_async_copy(pairs[k][0], buf.at[slot], lsem.at[t, slot])
        loads[k].start()

    load(0)
    for k in range(n):
        slot = k % STREAM_SLOTS
        if k + 1 < n:
            load(k + 1)
        loads[k].wait()
        cp = pltpu.make_async_remote_copy(src_ref=buf.at[slot], dst_ref=pairs[k][1], send_sem=ssem.at[t, slot],
                                          recv_sem=rsem.at[t], device_id=peer, device_id_type=MESH)
        cp.start()
        sent[k] = cp.wait_send
    for k in range(max(0, n - STREAM_SLOTS), n):
        sent[k]()


def _stream_scratch(shapes):
    nt = len(shapes)
    return ([pltpu.VMEM((STREAM_SLOTS,) + s, d) for s, d in shapes]
            + [pltpu.SemaphoreType.DMA((nt, STREAM_SLOTS)), pltpu.SemaphoreType.DMA((nt, STREAM_SLOTS)),
               pltpu.SemaphoreType.DMA((nt,))])


def _exchange_halves(name, grads):
    nt = len(grads)
    hs = [g.shape[1] // 2 for g in grads]
    crs = [_chunk_rows(h, g.shape[2] * g.dtype.itemsize) for h, g in zip(hs, grads)]

    def body(*refs):
        ins, gots, bufs, sems = refs[:nt], refs[nt:2 * nt], refs[2 * nt:3 * nt], refs[3 * nt:]
        x, y, c = _place()
        sib = (x, y, 1 - c)
        for t in range(nt):
            h, cr = hs[t], crs[t]
            pairs = [(ins[t].at[b, pl.ds((1 - c) * h + r0, cr)], gots[t].at[b, pl.ds(r0, cr)])
                     for b in range(N_CHIPS) for r0 in range(0, h, cr)]
            _stream(pairs, bufs[t], sems, t, sib)
        for t in range(nt):
            pltpu.make_async_remote_copy(src_ref=gots[t], dst_ref=gots[t], send_sem=sems[1].at[t, 0],
                                         recv_sem=sems[2].at[t], device_id=sib, device_id_type=MESH).wait_recv()

    return pl.pallas_call(
        body, name=name, in_specs=[ANY] * nt, out_specs=[ANY] * nt,
        out_shape=[jax.ShapeDtypeStruct((N_CHIPS, h, g.shape[2]), g.dtype) for h, g in zip(hs, grads)],
        scratch_shapes=_stream_scratch([((cr, g.shape[2]), g.dtype) for cr, g in zip(crs, grads)]),
        compiler_params=pltpu.CompilerParams(vmem_limit_bytes=VMEM_LIMIT_BYTES),
    )(*grads)


def _scatter_plan(srcs, dsts):
    x, y, c = _place()
    me = 2 * x + y
    return [(srcs[t].at[2 * px + py], dsts[t].at[me], (px, py, c))
            for t in range(len(srcs)) for px, py in _other_chips(x, y)]


def _slot(dst, chip, r0, rows, cols):
    if len(dst.shape) == 3:
        return dst.at[chip, pl.ds(r0, rows)]
    return dst.at[pl.ds(r0, rows), pl.ds(pl.multiple_of(chip * cols, 128), cols)]


def _shard_dims(gathered):
    s = gathered.shape
    return (s[1], s[2]) if len(s) == 3 else (s[0], s[1] // N_CHIPS)


def _gather_plan(_, bufs):
    x, y, c = _place()
    me = 2 * x + y
    plan = []
    for buf in bufs:
        rows, cols = _shard_dims(buf)
        mine = _slot(buf, me, c * (rows // 2), rows // 2, cols)
        plan += [(mine, mine, (px, py, c)) for px, py in _other_chips(x, y)]
    return plan


def _cast_to_slot(name, chip, w, layer, gathered_shape, after):
    _, R, C = w.shape
    tr, tc = _ew_tiles(R, C)

    def body(c_ref, w_ref, *rest):
        rest[-1][...] = w_ref[...].astype(BF16)

    if len(gathered_shape) == 3:
        out_spec = pl.BlockSpec((None, tr, tc), lambda i, j, c: (c[0], i, j))
    else:
        out_spec = pl.BlockSpec((tr, tc), lambda i, j, c: (i, c[0] * (C // tc) + j))
    extra = [] if after is None else [after]
    return pl.pallas_call(
        body, name=name, out_shape=jax.ShapeDtypeStruct(gathered_shape, BF16),
        grid_spec=pltpu.PrefetchScalarGridSpec(
            num_scalar_prefetch=1, grid=(R // tr, C // tc),
            in_specs=[pl.BlockSpec((None, tr, tc), lambda i, j, c: (layer, i, j))] + [ANY] * len(extra),
            out_specs=out_spec),
        compiler_params=_params("parallel", "parallel"),
    )(chip, w, *extra)


HBM_SPEC = pl.BlockSpec(memory_space=pltpu.HBM)
SEM_SPEC = pl.BlockSpec(memory_space=pltpu.SEMAPHORE)


def _split_start(name, srcs, dsts, plan, ncopies, after):
    bufs = [pltpu.with_memory_space_constraint(a, pltpu.HBM) for a in list(srcs) + list(dsts)]
    nb, ns = len(bufs), len(srcs)
    operands = bufs + ([after] if after is not None else [])

    def body(*refs):
        outs = refs[len(operands):]
        send, recv, token = outs[0], outs[1], outs[-1]
        for i, (src, dst, dev) in enumerate(plan(refs[:ns], refs[ns:nb])):
            pltpu.make_async_remote_copy(src_ref=src, dst_ref=dst, send_sem=send.at[i], recv_sem=recv.at[i],
                                         device_id=dev, device_id_type=MESH).start()
        token[...] = jnp.zeros_like(token)

    res = pl.pallas_call(
        body, name=name,
        out_shape=[pltpu.SemaphoreType.DMA((ncopies,)), pltpu.SemaphoreType.DMA((ncopies,))]
        + [pltpu.HBM(a.shape, a.dtype) for a in bufs] + [jax.ShapeDtypeStruct((8, 128), F32)],
        in_specs=[HBM_SPEC] * nb + [ANY] * (len(operands) - nb),
        out_specs=[SEM_SPEC, SEM_SPEC] + [HBM_SPEC] * nb + [pl.BlockSpec(memory_space=pltpu.VMEM)],
        input_output_aliases={i: 2 + i for i in range(nb)},
        compiler_params=pltpu.CompilerParams(has_side_effects=pltpu.SideEffectType.DATAFLOW_SIDE_EFFECTING),
    )(*operands)
    return res[:-1], res[-1]


def _split_wait(name, started, plan, ns, after):
    send, recv, bufs = started[0], started[1], list(started[2:])
    nb = len(bufs)

    def body(*refs):
        send_ref, recv_ref = refs[nb], refs[nb + 1]
        for i, (src, dst, dev) in enumerate(plan(refs[:ns], refs[ns:nb])):
            cp = pltpu.make_async_remote_copy(src_ref=src, dst_ref=dst, send_sem=send_ref.at[i], recv_sem=recv_ref.at[i],
                                              device_id=dev, device_id_type=MESH)
            cp.wait_send()
            cp.wait_recv()

    res = pl.pallas_call(
        body, name=name, out_shape=[pltpu.HBM(a.shape, a.dtype) for a in bufs],
        in_specs=[HBM_SPEC] * nb + [SEM_SPEC, SEM_SPEC, ANY], out_specs=[HBM_SPEC] * nb,
        input_output_aliases={i: i for i in range(nb)},
        compiler_params=pltpu.CompilerParams(has_side_effects=pltpu.SideEffectType.DATAFLOW_SIDE_EFFECTING),
    )(*bufs, send, recv, after)
    return res[:ns], res[ns:]


def _gather_finish(name, _, gathered):
    nt = len(gathered)
    dims = [_shard_dims(g) for g in gathered]
    hs = [rows // 2 for rows, _ in dims]
    crs = [_chunk_rows(h, cols * 2) for h, (_, cols) in zip(hs, dims)]

    def body(*refs):
        outs, bufs, sems = refs[nt:2 * nt], refs[2 * nt:3 * nt], refs[3 * nt:]
        x, y, c = _place()
        sib = (x, y, 1 - c)
        for t in range(nt):
            h, cr, cols = hs[t], crs[t], dims[t][1]
            passed = [_slot(outs[t], 2 * px + py, c * h + r0, cr, cols)
                      for px, py in _other_chips(x, y) for r0 in range(0, h, cr)]
            _stream([(r, r) for r in passed], bufs[t], sems, t, sib)
        for t in range(nt):
            if len(gathered[t].shape) == 3:
                three = outs[t].at[pl.ds(0, 3), pl.ds(0, hs[t])]
            else:
                three = outs[t].at[pl.ds(0, hs[t]), pl.ds(0, 3 * dims[t][1])]
            pltpu.make_async_remote_copy(src_ref=three, dst_ref=three, send_sem=sems[1].at[t, 0],
                                         recv_sem=sems[2].at[t], device_id=sib, device_id_type=MESH).wait_recv()

    return pl.pallas_call(
        body, name=name, in_specs=[ANY] * nt, out_specs=[ANY] * nt,
        out_shape=[jax.ShapeDtypeStruct(g.shape, g.dtype) for g in gathered],
        scratch_shapes=_stream_scratch([((cr, cols), BF16) for cr, (_, cols) in zip(crs, dims)]),
        input_output_aliases={t: t for t in range(nt)},
        compiler_params=pltpu.CompilerParams(vmem_limit_bytes=VMEM_LIMIT_BYTES),
    )(*gathered)


def _sum_share(landed):
    nt = len(landed)
    hs = [a.shape[1] for a in landed]
    cs = [a.shape[2] for a in landed]
    crs = [_chunk_rows(h, 2 * c * 4) for h, c in zip(hs, cs)]
    shapes = sorted(set(zip(crs, cs)))
    which = [shapes.index(s) for s in zip(crs, cs)]

    def body(*refs):
        ins, outs = refs[:nt], refs[nt:2 * nt]
        inbufs, outbufs = refs[2 * nt:2 * nt + len(shapes)], refs[2 * nt + len(shapes):2 * nt + 2 * len(shapes)]
        lsem, ssem, osem, rsem = refs[2 * nt + 2 * len(shapes):]
        x, y, c = _place()
        sib = (x, y, 1 - c)
        for t in range(nt):
            cr, n, ib, ob = crs[t], hs[t] // crs[t], inbufs[which[t]], outbufs[which[t]]
            loads, gone = [None] * n, [None] * n

            def load(k):
                slot = k % SUM_SLOTS
                if k >= SUM_SLOTS:
                    for cp_wait in gone[k - SUM_SLOTS]:
                        cp_wait()
                loads[k] = pltpu.make_async_copy(ins[t].at[:, pl.ds(k * cr, cr)], ib.at[slot], lsem.at[t, slot])
                loads[k].start()

            load(0)
            for k in range(n):
                slot = k % SUM_SLOTS
                if k + 1 < n:
                    load(k + 1)
                loads[k].wait()
                acc = ib[slot, 0].astype(F32)
                for s in range(1, N_CHIPS):
                    acc = acc + ib[slot, s].astype(F32)
                ob[slot] = acc
                rows = outs[t].at[c, pl.ds(k * cr, cr)]
                away = pltpu.make_async_remote_copy(src_ref=ob.at[slot], dst_ref=rows, send_sem=ssem.at[t, slot],
                                                    recv_sem=rsem.at[t], device_id=sib, device_id_type=MESH)
                away.start()
                home = pltpu.make_async_copy(ob.at[slot], rows, osem.at[t, slot])
                home.start()
                gone[k] = (away.wait_send, home.wait)
            for k in range(max(0, n - SUM_SLOTS), n):
                for cp_wait in gone[k]:
                    cp_wait()
        for t in range(nt):
            other = outs[t].at[1 - c]
            pltpu.make_async_remote_copy(src_ref=other, dst_ref=other, send_sem=ssem.at[t, 0], recv_sem=rsem.at[t],
                                         device_id=sib, device_id_type=MESH).wait_recv()

    slot_sems = pltpu.SemaphoreType.DMA((nt, SUM_SLOTS))
    return pl.pallas_call(
        body, name="sum_share", in_specs=[ANY] * nt, out_specs=[ANY] * nt,
        out_shape=[jax.ShapeDtypeStruct((2, h, c), F32) for h, c in zip(hs, cs)],
        scratch_shapes=[pltpu.VMEM((SUM_SLOTS, N_CHIPS, cr, c), BF16) for cr, c in shapes]
        + [pltpu.VMEM((SUM_SLOTS, cr, c), F32) for cr, c in shapes]
        + [slot_sems, slot_sems, slot_sems, pltpu.SemaphoreType.DMA((nt,))],
        compiler_params=pltpu.CompilerParams(vmem_limit_bytes=VMEM_LIMIT_BYTES),
    )(*landed)


def _allgather_small(name, v):
    def body(v_ref, o_ref, send, recv, lsem):
        x, y, c = _place()
        me = 4 * x + 2 * y + c
        loc = pltpu.make_async_copy(v_ref, o_ref.at[me], lsem)
        loc.start()
        copies = []
        for k in range(1, N_DEV):
            px = 1 - x if k & 4 else x
            py = 1 - y if k & 2 else y
            pc = 1 - c if k & 1 else c
            cp = pltpu.make_async_remote_copy(
                src_ref=v_ref, dst_ref=o_ref.at[me], send_sem=send.at[k - 1], recv_sem=recv.at[k - 1],
                device_id=(px, py, pc), device_id_type=MESH)
            cp.start()
            copies.append(cp)
        for cp in copies:
            cp.wait()
        loc.wait()

    vm = pl.BlockSpec(memory_space=pltpu.VMEM)
    return pl.pallas_call(
        body, name=name, in_specs=[vm], out_specs=vm,
        out_shape=jax.ShapeDtypeStruct((N_DEV,) + v.shape, v.dtype),
        scratch_shapes=[pltpu.SemaphoreType.DMA((N_DEV - 1,))] * 2 + [pltpu.SemaphoreType.DMA],
    )(v)


def kernel(x, mem, norm_g, mem_norm_g, w_kv, w_out, pool_w_in, pool_w_grp, pool_scale, hgrn_w_in, hgrn_lb, hgrn_norm_g, final_g, loss_target, m_norm_g, m_mem_norm_g, m_w_kv, m_w_out, m_pool_w_in, m_pool_w_grp, m_pool_scale, m_hgrn_w_in, m_hgrn_lb, m_hgrn_norm_g, m_final_g, v_norm_g, v_mem_norm_g, v_w_kv, v_w_out, v_pool_w_in, v_pool_w_grp, v_pool_scale, v_hgrn_w_in, v_hgrn_lb, v_hgrn_norm_g, v_final_g):
    _, S, D = x.shape
    M = mem.shape[1]
    EB = 2 * D
    ECA = EB // 4
    EMIX = EB - ECA
    PG = EMIX // N_POOL_GROUPS
    NP0 = EMIX + ECA + EB
    NP1 = 3 * EMIX + ECA + EB
    SH0, SH1 = NP0 // N_CHIPS, NP1 // N_CHIPS
    DK, EK = D // N_CHIPS, EB // N_CHIPS
    TNP = 512 if all(v % 512 == 0 for v in (SH0, SH1, ECA, EMIX)) else 256
    TM = _tile(S, 1024)
    TMF = _tile(S, 2048)
    TD = _tile(D, 512)
    TDW = _tile(D, 1024)
    c0, c1 = SH0 // TNP, SH1 // TNP
    qt = EMIX // TNP
    chip = 2 * lax.axis_index("x") + lax.axis_index("y")

    xs, ms, tgt = x[0], mem[0], loss_target[0]

    sds = jax.ShapeDtypeStruct
    chip1 = chip.astype(jnp.int32).reshape(1)

    def start_gather(tag, layers, after):
        bufs = []
        for t, (w, layer) in enumerate(layers):
            w3 = w.reshape((w.shape[0], -1, w.shape[-1]))
            shape = (D, NP1) if w is hgrn_w_in else (N_CHIPS,) + w3.shape[1:]
            bufs.append(_cast_to_slot(f"cast_{tag}{t}", chip1, w3, layer, shape, after))
        return _split_start(f"gather_{tag}_start", [], bufs, _gather_plan, 3 * len(bufs), after)

    gather_a, token = start_gather("a", [(pool_w_in, 0)], None)
    gather_b, token = start_gather("b", [(w_kv, 0), (w_out, 0), (pool_w_grp, 0)], token)
    gather_c, token = start_gather("c", [(hgrn_w_in, 0)], token)
    gather_d, token = start_gather("d", [(w_kv, 1), (w_out, 1)], token)
    started = token[0:1, 0:1]

    tek, tew = _tile(EK, 512), _tile(EK, 1024)

    mem_n = _rms_fwd("rms_mem", ms, mem_norm_g.reshape(1, D) + started)
    h0 = _rms_fwd("rms0", xs, norm_g[0:1] + started)
    wpin, = _gather_finish("gather_a_finish", *_split_wait("gather_a_wait", gather_a, _gather_plan, 0, h0))

    tkw = _tile(2 * ECA, 1024)

    def kv_of(layer, wkv):
        return _matmul(
            f"kv{layer}", mem_n, wkv.reshape(D, 2 * ECA), grid=(1, 2 * ECA // tkw, 1),
            a_spec=pl.BlockSpec((M, D), lambda i, j, k: (0, 0)), b_spec=pl.BlockSpec((D, tkw), lambda i, j, k: (0, j)),
            out_shape=sds((M, 2 * ECA), BF16), out_spec=pl.BlockSpec((M, tkw), lambda i, j, k: (0, j)),
            acc_shape=(M, tkw), dims=NN)

    tko = _tile(EB, 2048)

    def out_proj(layer, branch, wout, resid):
        return _matmul(
            f"out_proj{layer}", branch, wout.reshape(EB, D), grid=(S // TM, D // TDW, EB // tko),
            a_spec=pl.BlockSpec((TM, tko), IK), b_spec=pl.BlockSpec((tko, TDW), KJ),
            out_shape=sds((S, D), F32), out_spec=pl.BlockSpec((TM, TDW), IJ),
            acc_shape=(TM, TDW), dims=NN, add=resid, add_spec=pl.BlockSpec((TM, TDW), IJ))

    ones_ca = jnp.ones((1, ECA), F32)

    proj0 = _matmul(
        "proj0", h0, wpin, grid=(S // TMF, NP0 // TNP, 1),
        a_spec=pl.BlockSpec((TMF, D), lambda i, j, k: (i, 0)),
        b_spec=pl.BlockSpec((None, D, TNP), lambda i, j, k: (j // c0, 0, j % c0)),
        out_shape=sds((S, NP0), BF16), out_spec=pl.BlockSpec((TMF, TNP), IJ),
        acc_shape=(TMF, TNP), dims=NN)
    pooled = _pool_fwd(proj0, S, EMIX)
    wkv0, wout0, g_grp = _gather_finish("gather_b_finish", *_split_wait("gather_b_wait", gather_b, _gather_plan, 0, pooled))
    wgrp = g_grp.reshape(N_CHIPS, N_POOL_GROUPS, PG // N_CHIPS, PG).transpose(1, 0, 2, 3).reshape(N_POOL_GROUPS, PG, PG)
    kv = [kv_of(0, wkv0), None]
    premix0 = _matmul(
        "pool_grp", pooled, wgrp, grid=(S // TM, N_POOL_GROUPS, 1),
        a_spec=pl.BlockSpec((TM, PG), lambda i, j, k: (i, j)),
        b_spec=pl.BlockSpec((None, PG, PG), lambda i, j, k: (j, 0, 0)),
        out_shape=sds((S, EB), BF16), out_spec=pl.BlockSpec((TM, PG), lambda i, j, k: (i, j)),
        acc_shape=(TM, PG), dims=NN)
    premix0 = _ca_fwd("ca_fwd0", proj0, EMIX // ECA, kv[0], premix0, S, ECA, EMIX)
    colscale0 = jnp.concatenate([pool_scale.reshape(1, EMIX), ones_ca], axis=1)
    gblk0 = (EMIX + ECA) // ECA
    branch0 = _gate_fwd("gate_fwd0", premix0, proj0, gblk0, colscale0, S, EB, ECA)
    x1 = out_proj(0, branch0, wout0, xs)

    whin, = _gather_finish("gather_c_finish", *_split_wait("gather_c_wait", gather_c, _gather_plan, 0, x1))
    h1 = _rms_fwd("rms1", x1, norm_g[1:2])

    def proj1_cols(name, ncols, col_of, out_cols, out_dtype, out_col_of):
        return _matmul(
            name, h1, whin, grid=(S // TMF, ncols, 1),
            a_spec=pl.BlockSpec((TMF, D), lambda i, j, k: (i, 0)),
            b_spec=pl.BlockSpec((D, TNP), lambda i, j, k: (0, col_of(j))),
            out_shape=sds((S, out_cols), out_dtype), out_spec=pl.BlockSpec((TMF, TNP), lambda i, j, k: (i, out_col_of(j))),
            acc_shape=(TMF, TNP), dims=NN)

    skip_f = lambda j: jnp.where(j < qt, j, j + qt)
    proj1 = proj1_cols("proj1", NP1 // TNP - qt, skip_f, NP1, BF16, skip_f)
    fgate = proj1_cols("proj1_f", qt, lambda j: j + qt, EMIX, F32, lambda j: j)
    premix1, rstd1, states = _hgrn_fwd(proj1, fgate, hgrn_lb, S, EMIX, EB)
    wkv1, wout1 = _gather_finish("gather_d_finish", *_split_wait("gather_d_wait", gather_d, _gather_plan, 0, rstd1))
    kv[1] = kv_of(1, wkv1)
    premix1 = _ca_fwd("ca_fwd1", proj1, 3 * EMIX // ECA, kv[1], premix1, S, ECA, EMIX)
    norm_tiles = _allgather_small("allgather_norm_g", jnp.pad(hgrn_norm_g, ((0, SMALL_ROWS - 1), (0, 0))))
    hg_norm = norm_tiles[0::2, 0, :].reshape(1, EMIX)
    colscale1 = jnp.concatenate([hg_norm, ones_ca], axis=1)
    gblk1 = (3 * EMIX + ECA) // ECA
    branch1 = _gate_fwd("gate_fwd1", premix1, proj1, gblk1, colscale1, S, EB, ECA)
    x2 = out_proj(1, branch1, wout1, x1)

    dx2, dx2b, d_final_g, loss_part = _loss_head(x2, final_g.reshape(1, D), tgt)

    def out_proj_bwd(layer, dxb, branch, wout, premix, proj, gblk, colscale, dshape, dblk):
        goff, doff = gblk * ECA // tek, dblk * ECA // tek
        dpremix, dgate, dcol = _matmul(
            f"dbranch{layer}", dxb, wout, grid=(S // TMF, EB // tek, 1),
            a_spec=pl.BlockSpec((TMF, D), lambda i, j, k: (i, 0)),
            b_spec=pl.BlockSpec((None, tek, D), lambda i, j, k: (j // (EK // tek), j % (EK // tek), 0)),
            extras=[(premix, pl.BlockSpec((TMF, tek), IJ)), (proj, pl.BlockSpec((TMF, tek), lambda i, j, k: (i, goff + j))),
                    (colscale, pl.BlockSpec((1, tek), lambda i, j, k: (0, j)))],
            epilogue=_gate_bwd_epilogue,
            out_shape=[sds((S, EB), BF16), sds(dshape, BF16), sds((S // TMF, 1, EB), F32)],
            out_spec=[pl.BlockSpec((TMF, tek), IJ), pl.BlockSpec((TMF, tek), lambda i, j, k: (i, doff + j)),
                      pl.BlockSpec((None, 1, tek), lambda i, j, k: (i, 0, j))],
            acc_shape=(TMF, tek), dims=NT)
        dw = _matmul(
            f"dwout{layer}", branch, dxb, grid=(EB // tew, D // TD, 1),
            a_spec=pl.BlockSpec((S, tew), lambda i, j, k: (0, i)), b_spec=pl.BlockSpec((S, TD), lambda i, j, k: (0, j)),
            out_shape=sds((N_CHIPS, EK, D), BF16),
            out_spec=pl.BlockSpec((None, tew, TD), lambda i, j, k: (i // (EK // tew), i % (EK // tew), j)),
            acc_shape=(tew, TD), dims=TN)
        return dpremix, dgate, dcol.reshape(S // TMF, EB), dw

    def kv_bwd(layer, dkv, wkv, dmem_add):
        dkvb = dkv.astype(BF16)
        dmem = _matmul(
            f"dmem{layer}", dkvb, wkv.reshape(D, 2 * ECA), grid=(1, D // TDW, 1),
            a_spec=pl.BlockSpec((M, 2 * ECA), lambda i, j, k: (0, 0)),
            b_spec=pl.BlockSpec((TDW, 2 * ECA), lambda i, j, k: (j, 0)),
            out_shape=sds((M, D), F32), out_spec=pl.BlockSpec((M, TDW), lambda i, j, k: (0, j)), acc_shape=(M, TDW),
            dims=NT, add=dmem_add, add_spec=pl.BlockSpec((M, TDW), lambda i, j, k: (0, j)))
        dw = _matmul(
            f"dwkv{layer}", mem_n, dkvb, grid=(D // TDW, 2 * ECA // tkw, 1),
            a_spec=pl.BlockSpec((M, TDW), lambda i, j, k: (0, i)), b_spec=pl.BlockSpec((M, tkw), lambda i, j, k: (0, j)),
            out_shape=sds((D, 2 * ECA), BF16), out_spec=pl.BlockSpec((TDW, tkw), IJ), acc_shape=(TDW, tkw), dims=TN)
        return dmem, dw.reshape(N_CHIPS, DK, 2 * ECA)

    dpremix1, drest1, dcol1, gw_out1 = out_proj_bwd(1, dx2b, branch1, wout1, premix1, proj1, gblk1, colscale1,
                                                    (S, ECA + EB), 1)
    drest1, dkv1 = _ca_bwd("ca_bwd1", dpremix1, proj1, 3 * EMIX // ECA, kv[1], drest1, 0, S, ECA, EMIX)
    dqfi, dlb = _hgrn_bwd(dpremix1, premix1, rstd1, states, proj1, fgate, hgrn_lb, S, EMIX)
    nq, nr = 3 * qt, (ECA + EB) // TNP
    tkh = _tile(EMIX, 1024) if (ECA + EB) % _tile(EMIX, 1024) == 0 else TNP
    kq = EMIX // tkh
    dh1 = _matmul(
        "dh1_qfi", dqfi, whin, grid=(S // TM, D // TDW, 3),
        a_spec=pl.BlockSpec((None, TM, EMIX), lambda i, j, k: (k, i, 0)),
        b_spec=pl.BlockSpec((TDW, EMIX), lambda i, j, k: (j, k)),
        out_shape=sds((S, D), F32), out_spec=pl.BlockSpec((TM, TDW), IJ), acc_shape=(TM, TDW), dims=NT)
    dh1 = _matmul(
        "dh1_rest", drest1, whin, grid=(S // TM, D // TDW, (ECA + EB) // tkh), a_spec=pl.BlockSpec((TM, tkh), IK),
        b_spec=pl.BlockSpec((TDW, tkh), lambda i, j, k: (j, k + 3 * kq)),
        out_shape=sds((S, D), F32), out_spec=pl.BlockSpec((TM, TDW), IJ), acc_shape=(TM, TDW), dims=NT,
        add=dh1, add_spec=pl.BlockSpec((TM, TDW), IJ))
    gw_hin = _matmul(
        "dwhin_qfi", h1, dqfi, grid=(D // TDW, nq, 1), a_spec=pl.BlockSpec((S, TDW), lambda i, j, k: (0, i)),
        b_spec=pl.BlockSpec((None, S, TNP), lambda i, j, k: (j // qt, 0, j % qt)),
        out_shape=sds((N_CHIPS, D, SH1), BF16), out_spec=pl.BlockSpec((None, TDW, TNP), lambda i, j, k: (j // c1, i, j % c1)),
        acc_shape=(TDW, TNP), dims=TN)
    gw_hin = _matmul(
        "dwhin_rest", h1, drest1, grid=(D // TDW, nr, 1), a_spec=pl.BlockSpec((S, TDW), lambda i, j, k: (0, i)),
        b_spec=pl.BlockSpec((S, TNP), lambda i, j, k: (0, j)), out_shape=sds((N_CHIPS, D, SH1), BF16),
        out_spec=pl.BlockSpec((None, TDW, TNP), lambda i, j, k: ((j + nq) // c1, i, (j + nq) % c1)),
        acc_shape=(TDW, TNP), dims=TN, alias=gw_hin)
    dmem, gw_kv1 = kv_bwd(1, dkv1, wkv1, None)

    core_chip = jnp.stack([lax.axis_index("c"), chip]).astype(jnp.int32)

    def reduce_in_chip(tag, stacks):
        got = _exchange_halves(f"exchange_halves{tag}", stacks)
        pairs = [_add_halves(f"add_halves{tag}_{t}", core_chip, g.reshape(N_CHIPS, 2, g.shape[1] // 2, g.shape[2]), r)
                 for t, (g, r) in enumerate(zip(stacks, got))]
        return [p for p, _ in pairs], [own for _, own in pairs]

    parts1, landed1 = reduce_in_chip(1, [gw_kv1, gw_out1, gw_hin])
    scatter1, token1 = _split_start("scatter1_start", parts1, landed1, _scatter_plan, 3 * len(parts1), None)
    dx1, dx1b, d_ng1 = _rms_bwd("rms_bwd1", dh1, x1, norm_g[1:2] + token1[0:1, 0:1], dx2)

    dpremix0, dproj0, dcol0, gw_out0 = out_proj_bwd(0, dx1b, branch0, wout0, premix0, proj0, gblk0, colscale0,
                                                    (S, NP0), gblk0)
    dproj0, dkv0 = _ca_bwd("ca_bwd0", dpremix0, proj0, EMIX // ECA, kv[0], dproj0, EMIX // ECA, S, ECA, EMIX)
    dmem, gw_kv0 = kv_bwd(0, dkv0, wkv0, dmem)
    parts_a, landed_a = reduce_in_chip("0a", [gw_kv0, gw_out0])
    scatter_a, token_a = _split_start("scatter0a_start", parts_a, landed_a, _scatter_plan, 3 * len(parts_a), None)
    dpooled = _matmul(
        "dpooled", dpremix0, wgrp, grid=(S // TM, N_POOL_GROUPS, 1), a_spec=pl.BlockSpec((TM, PG), IJ),
        b_spec=pl.BlockSpec((None, PG, PG), lambda i, j, k: (j, 0, 0)),
        out_shape=sds((S, EMIX), F32), out_spec=pl.BlockSpec((TM, PG), IJ), acc_shape=(TM, PG), dims=NT, after=token_a)
    dwgrp = _matmul(
        "dwgrp", pooled, dpremix0, grid=(N_POOL_GROUPS, 1, 1), a_spec=pl.BlockSpec((S, PG), lambda i, j, k: (0, i)),
        b_spec=pl.BlockSpec((S, PG), lambda i, j, k: (0, i)), out_shape=sds((N_POOL_GROUPS, PG, PG), F32),
        out_spec=pl.BlockSpec((None, PG, PG), lambda i, j, k: (i, 0, 0)), acc_shape=(PG, PG), dims=TN)
    dproj0 = _pool_bwd(dpooled, dproj0, S, EMIX)
    gw_pin = _matmul(
        "dwpin", h0, dproj0, grid=(D // TDW, NP0 // TNP, 1), a_spec=pl.BlockSpec((S, TDW), lambda i, j, k: (0, i)),
        b_spec=pl.BlockSpec((S, TNP), lambda i, j, k: (0, j)), out_shape=sds((N_CHIPS, D, SH0), BF16),
        out_spec=pl.BlockSpec((None, TDW, TNP), lambda i, j, k: (j // c0, i, j % c0)), acc_shape=(TDW, TNP), dims=TN)
    gw_grp = dwgrp.reshape(N_POOL_GROUPS, N_CHIPS, PG // N_CHIPS, PG).transpose(1, 0, 2, 3).reshape(N_CHIPS, PG, PG)
    parts_b, landed_b = reduce_in_chip("0b", [gw_pin, gw_grp.astype(BF16)])
    scatter_b, token_b = _split_start("scatter0b_start", parts_b, landed_b, _scatter_plan, 3 * len(parts_b), None)
    dh0 = _matmul(
        "dh0", dproj0, wpin, grid=(S // TM, D // TDW, N_CHIPS), a_spec=pl.BlockSpec((TM, SH0), IK),
        b_spec=pl.BlockSpec((None, TDW, SH0), lambda i, j, k: (k, j, 0)),
        out_shape=sds((S, D), F32), out_spec=pl.BlockSpec((TM, TDW), IJ), acc_shape=(TM, TDW), dims=NT, after=token_b)
    grad_x, _, d_ng0 = _rms_bwd("rms_bwd0", dh0, xs, norm_g[0:1], dx1)
    _, _, d_mng = _rms_bwd("rms_bwd_mem", dmem, ms, mem_norm_g.reshape(1, D), jnp.zeros_like(ms))

    _, landed1 = _split_wait("scatter1_wait", scatter1, _scatter_plan, len(parts1), grad_x)
    _, landed_a = _split_wait("scatter0a_wait", scatter_a, _scatter_plan, len(parts_a), grad_x)
    _, landed_b = _split_wait("scatter0b_wait", scatter_b, _scatter_plan, len(parts_b), grad_x)
    landed = [landed_a[0], landed1[0], landed_a[1], landed1[1], landed_b[0], landed_b[1], landed1[2]]
    fulls = _sum_share(landed)
    f2 = [f.reshape(-1, f.shape[-1]) for f in fulls]
    grads, deltas, new_m, new_v = {}, {}, {}, {}
    for n, w, mm, vv, gs in (("w_kv", w_kv, m_w_kv, v_w_kv, f2[0:2]), ("w_out", w_out, m_w_out, v_w_out, f2[2:4]),
                             ("pool_w_in", pool_w_in, m_pool_w_in, v_pool_w_in, f2[4:5]),
                             ("pool_w_grp", pool_w_grp, m_pool_w_grp, v_pool_w_grp, f2[5:6]),
                             ("hgrn_w_in", hgrn_w_in, m_hgrn_w_in, v_hgrn_w_in, f2[6:7])):
        as3d = lambda a: a.reshape((a.shape[0], -1, a.shape[-1]))
        outs = _adamw_layers(f"adamw_{n}", as3d(w), gs, as3d(mm), as3d(vv))
        grads[n], deltas[n], new_m[n], new_v[n] = [o.reshape(w.shape) for o in outs]

    Wd = EMIX
    nshard = EMIX // N_CHIPS
    summed_rows = [(v, True, v.shape[1]) for v in (d_ng0, d_ng1, d_mng)] + [
        (dcol0, True, EMIX), (dlb, True, EMIX), (dcol1, True, EMIX), (d_final_g, True, D), (loss_part, True, 128)]
    partial = _pack_rows("pack_partials", summed_rows, Wd)
    g_pack, loss = _small_sum(_allgather_small("allgather_grads", partial), hgrn_lb, chip1, nshard)

    def pack_small(name, ng, mng, ps, lb_, hn, fg):
        return _pack_rows(name, [(ng, False, D), (mng.reshape(1, D), False, D), (ps, False, EMIX), (lb_, False, EMIX),
                                 (hn, False, nshard), (fg.reshape(1, D), False, D)], Wd)

    d_pack, m_pack, v_pack = _adamw(
        "adamw_small", pack_small("pack_small_w", norm_g, mem_norm_g, pool_scale, hgrn_lb, hgrn_norm_g, final_g), g_pack,
        pack_small("pack_small_m", m_norm_g, m_mem_norm_g, m_pool_scale, m_hgrn_lb, m_hgrn_norm_g, m_final_g),
        pack_small("pack_small_v", v_norm_g, v_mem_norm_g, v_pool_scale, v_hgrn_lb, v_hgrn_norm_g, v_final_g))
    layout = [(2, D), (1, D), (1, EMIX), (2, EMIX), (1, nshard), (1, D)]
    for tag, pack, out in (("g", g_pack, grads), ("d", d_pack, deltas), ("m", m_pack, new_m), ("v", v_pack, new_v)):
        ng, mng, ps, lb_, hn, fg = _unpack_rows(f"unpack_small_{tag}", pack, layout)
        out.update(norm_g=ng, mem_norm_g=mng.reshape(D), pool_scale=ps, hgrn_lb=lb_, hgrn_norm_g=hn, final_g=fg.reshape(D))
    loss = loss[0, 0]

    order = ["norm_g", "mem_norm_g", "w_kv", "w_out", "pool_w_in", "pool_w_grp", "pool_scale", "hgrn_w_in", "hgrn_lb",
             "hgrn_norm_g", "final_g"]
    return (loss, grad_x.reshape(1, S, D), *[grads[n] for n in order], *[deltas[n] for n in order],
            *[new_m[n] for n in order], *[new_v[n] for n in order])
```

```python
import functools

import jax
import jax.numpy as jnp
from jax import lax
from jax.experimental import pallas as pl
from jax.experimental.pallas import tpu as pltpu

F32 = jnp.float32
BF16 = jnp.bfloat16
MESH = pl.DeviceIdType.MESH
ANY = pl.BlockSpec(memory_space=pl.ANY)

EPS = 1e-6
HG_HEAD_DIM = 128
HG_CHUNK = 64
CA_HEADS = 4
N_POOL_GROUPS = 4
POOL_HALO = 128
ADAM_LR = 0.001
ADAM_B1 = 0.9
ADAM_B2 = 0.999
ADAM_EPS = 1e-08
ADAM_WD = 0.01
ADAM_STEP = 10
N_CHIPS = 4
N_DEV = 8
VMEM_LIMIT_BYTES = 56 * 1024 * 1024
SMALL_ROWS = 8
STREAM_CHUNK_BYTES = 2 * 1024 * 1024
STREAM_SLOTS = 4
SUM_SLOTS = 2


def _params(*sem):
    return pltpu.CompilerParams(dimension_semantics=sem, vmem_limit_bytes=VMEM_LIMIT_BYTES)


def _tile(n, pref):
    t = pref
    while n % t:
        t //= 2
    return t


def _sigmoid(x):
    return 1.0 / (1.0 + jnp.exp(-x))


def _matmul(name, a, b, *, grid, a_spec, b_spec, out_shape, out_spec, acc_shape, dims,
            add=None, add_spec=None, alias=None, after=None, extras=(), epilogue=None):
    nk = grid[2]
    has_add = add is not None
    has_alias = alias is not None
    has_after = after is not None
    n_out = len(out_shape) if epilogue is not None else 1

    def body(*refs):
        a_ref, b_ref = refs[0], refs[1]
        pos = 2
        add_ref = None
        if has_add:
            add_ref = refs[pos]
            pos += 1
        extra_refs = refs[pos:pos + len(extras)]
        pos += len(extras) + has_alias + has_after
        o_refs = refs[pos:pos + n_out]
        prod = lax.dot_general(a_ref[...], b_ref[...], (dims, ((), ())), preferred_element_type=F32)

        def finish(r):
            if epilogue is not None:
                epilogue(r, extra_refs, o_refs)
                return
            if has_add:
                r = r + add_ref[...].astype(F32)
            o_refs[0][...] = r.astype(o_refs[0].dtype)

        if nk == 1:
            finish(prod)
            return
        acc_ref = refs[pos + n_out]
        k = pl.program_id(2)

        @pl.when(k == 0)
        def _():
            acc_ref[...] = prod

        @pl.when(k > 0)
        def _():
            acc_ref[...] += prod

        @pl.when(k == nk - 1)
        def _():
            finish(acc_ref[...])

    operands = [a, b]
    in_specs = [a_spec, b_spec]
    if has_add:
        operands.append(add)
        in_specs.append(add_spec)
    for arr, spec in extras:
        operands.append(arr)
        in_specs.append(spec)
    aliases = {}
    if has_alias:
        aliases = {len(operands): 0}
        operands.append(alias)
        in_specs.append(ANY)
    if has_after:
        operands.append(after)
        in_specs.append(ANY)
    return pl.pallas_call(
        body, name=name, grid=grid, in_specs=in_specs, out_specs=out_spec, out_shape=out_shape,
        scratch_shapes=[pltpu.VMEM(acc_shape, F32)] if nk > 1 else [], input_output_aliases=aliases,
        compiler_params=_params("parallel", "parallel", "arbitrary"),
    )(*operands)


IJ = lambda i, j, k: (i, j)
IK = lambda i, j, k: (i, k)
KJ = lambda i, j, k: (k, j)
KI = lambda i, j, k: (k, i)
NN = ((1,), (0,))
NT = ((1,), (1,))
TN = ((0,), (0,))


def _rms_fwd(name, x, g, after=None):
    R, D = x.shape
    tr = _tile(R, 256)
    extra = [] if after is None else [after]

    def body(x_ref, g_ref, *rest):
        xf = x_ref[...]
        r = lax.rsqrt(jnp.mean(xf * xf, axis=-1, keepdims=True) + EPS)
        rest[-1][...] = (xf * r * g_ref[...]).astype(BF16)

    return pl.pallas_call(
        body, name=name, grid=(R // tr,),
        in_specs=[pl.BlockSpec((tr, D), lambda i: (i, 0)), pl.BlockSpec((1, D), lambda i: (0, 0))] + [ANY] * len(extra),
        out_specs=pl.BlockSpec((tr, D), lambda i: (i, 0)),
        out_shape=jax.ShapeDtypeStruct((R, D), BF16), compiler_params=_params("parallel"),
    )(x, g, *extra)


def _rms_bwd(name, dh, x, g, dres, after=None):
    R, D = x.shape
    tr = _tile(R, 256)
    extra = [] if after is None else [after]

    def body(dh_ref, x_ref, g_ref, dres_ref, *rest):
        dx_ref, dxb_ref, dg_ref = rest[len(extra):]
        xf = x_ref[...]
        r = lax.rsqrt(jnp.mean(xf * xf, axis=-1, keepdims=True) + EPS)
        xn = xf * r
        d = dh_ref[...]
        dyg = d * g_ref[...]
        dx = r * (dyg - xn * jnp.mean(dyg * xn, axis=-1, keepdims=True)) + dres_ref[...]
        dx_ref[...] = dx
        dxb_ref[...] = dx.astype(BF16)

        @pl.when(pl.program_id(0) == 0)
        def _():
            dg_ref[...] = jnp.zeros_like(dg_ref)

        dg_ref[...] += jnp.sum(d * xn, axis=0, keepdims=True)

    row = pl.BlockSpec((tr, D), lambda i: (i, 0))
    vec = pl.BlockSpec((1, D), lambda i: (0, 0))
    return pl.pallas_call(
        body, name=name, grid=(R // tr,), in_specs=[row, row, vec, row] + [ANY] * len(extra), out_specs=[row, row, vec],
        out_shape=[jax.ShapeDtypeStruct((R, D), F32), jax.ShapeDtypeStruct((R, D), BF16),
                   jax.ShapeDtypeStruct((1, D), F32)],
        compiler_params=_params("arbitrary"),
    )(dh, x, g, dres, *extra)


def _loss_head(x2, g, target):
    R, D = x2.shape
    tr = _tile(R, 256)

    def body(x_ref, g_ref, t_ref, dx_ref, dxb_ref, dg_ref, loss_ref):
        xf = x_ref[...]
        gg = g_ref[...]
        r = lax.rsqrt(jnp.mean(xf * xf, axis=-1, keepdims=True) + EPS)
        xn = xf * r
        e = xn * gg - t_ref[...]
        part = 0.5 * jnp.sum(jnp.mean(e * e, axis=-1, keepdims=True), axis=0, keepdims=True)
        dy = e * (1.0 / D)
        dyg = dy * gg
        dx = r * (dyg - xn * jnp.mean(dyg * xn, axis=-1, keepdims=True))
        dx_ref[...] = dx
        dxb_ref[...] = dx.astype(BF16)

        @pl.when(pl.program_id(0) == 0)
        def _():
            dg_ref[...] = jnp.zeros_like(dg_ref)
            loss_ref[...] = jnp.zeros_like(loss_ref)

        dg_ref[...] += jnp.sum(dy * xn, axis=0, keepdims=True)
        loss_ref[...] += jnp.broadcast_to(part, loss_ref.shape)

    row = pl.BlockSpec((tr, D), lambda i: (i, 0))
    vec = pl.BlockSpec((1, D), lambda i: (0, 0))
    return pl.pallas_call(
        body, name="loss_head", grid=(R // tr,), in_specs=[row, vec, row],
        out_specs=[row, row, vec, pl.BlockSpec((1, 128), lambda i: (0, 0))],
        out_shape=[jax.ShapeDtypeStruct((R, D), F32), jax.ShapeDtypeStruct((R, D), BF16),
                   jax.ShapeDtypeStruct((1, D), F32), jax.ShapeDtypeStruct((1, 128), F32)],
        compiler_params=_params("arbitrary"),
    )(x2, g, target)


def _pool_band(tr, reverse, w):
    r = lax.broadcasted_iota(jnp.int32, (tr, tr + POOL_HALO), 0)
    c = lax.broadcasted_iota(jnp.int32, (tr, tr + POOL_HALO), 1)
    if reverse:
        inside = (c >= r) & (c < r + w)
    else:
        cc = c - POOL_HALO
        inside = (cc <= r) & (cc > r - w)
    return jnp.where(inside, 1.0, 0.0).astype(BF16)


def _pool_fwd(proj, S, EMIX):
    PG = EMIX // N_POOL_GROUPS
    cb = PG
    tr = _tile(S, 512)
    per_group = PG // cb

    def body(u_ref, o_ref, ext):
        i = pl.program_id(1)
        w = jnp.left_shift(2, pl.program_id(0) // per_group)

        @pl.when(i == 0)
        def _():
            ext[0:POOL_HALO, :] = jnp.zeros((POOL_HALO, cb), BF16)

        u = u_ref[...]
        ext[POOL_HALO:, :] = u
        win = jnp.dot(_pool_band(tr, False, w), ext[...], preferred_element_type=F32)
        pos = i * tr + lax.broadcasted_iota(jnp.int32, (tr, 1), 0)
        cnt = jnp.minimum(pos + 1, w).astype(F32)
        o_ref[...] = (win / cnt - u.astype(F32)).astype(BF16)
        ext[0:POOL_HALO, :] = u[tr - POOL_HALO:, :]

    return pl.pallas_call(
        body, name="pool_fwd", grid=(EMIX // cb, S // tr),
        in_specs=[pl.BlockSpec((tr, cb), lambda j, i: (i, j))],
        out_specs=pl.BlockSpec((tr, cb), lambda j, i: (i, j)),
        out_shape=jax.ShapeDtypeStruct((S, EMIX), BF16),
        scratch_shapes=[pltpu.VMEM((tr + POOL_HALO, cb), BF16)],
        compiler_params=_params("parallel", "arbitrary"),
    )(proj)


def _pool_bwd(dpooled, dproj, S, EMIX):
    PG = EMIX // N_POOL_GROUPS
    cb = PG
    tr = _tile(S, 512)
    per_group = PG // cb
    nrt = S // tr

    def body(d_ref, _, o_ref, ext):
        step = pl.program_id(1)
        i = nrt - 1 - step
        w = jnp.left_shift(2, pl.program_id(0) // per_group)

        @pl.when(step == 0)
        def _():
            ext[tr:, :] = jnp.zeros((POOL_HALO, cb), BF16)

        d = d_ref[...]
        pos = i * tr + lax.broadcasted_iota(jnp.int32, (tr, 1), 0)
        cnt = jnp.minimum(pos + 1, w).astype(F32)
        z = (d / cnt).astype(BF16)
        ext[0:tr, :] = z
        win = jnp.dot(_pool_band(tr, True, w), ext[...], preferred_element_type=F32)
        o_ref[...] = (win - d).astype(BF16)
        ext[tr:, :] = z[0:POOL_HALO, :]

    return pl.pallas_call(
        body, name="pool_bwd", grid=(EMIX // cb, nrt),
        in_specs=[pl.BlockSpec((tr, cb), lambda j, s: (nrt - 1 - s, j)), ANY],
        out_specs=pl.BlockSpec((tr, cb), lambda j, s: (nrt - 1 - s, j)),
        out_shape=jax.ShapeDtypeStruct(dproj.shape, dproj.dtype),
        scratch_shapes=[pltpu.VMEM((tr + POOL_HALO, cb), BF16)],
        input_output_aliases={1: 0},
        compiler_params=_params("parallel", "arbitrary"),
    )(dpooled, dproj)


def _ca_fwd(name, proj, qblk, kv, premix, S, ECA, EMIX):
    M = kv.shape[0]
    hd = ECA // CA_HEADS
    ts = _tile(S, 512)
    scale = hd ** -0.5

    def body(q_ref, kv_ref, _, o_ref):
        for h in range(CA_HEADS):
            q = q_ref[:, h * hd:(h + 1) * hd]
            k = kv_ref[:, h * hd:(h + 1) * hd]
            v = kv_ref[:, ECA + h * hd:ECA + (h + 1) * hd]
            s = lax.dot_general(q, k, (NT, ((), ())), preferred_element_type=F32) * scale
            s = s - jnp.max(s, axis=-1, keepdims=True)
            p = jnp.exp(s)
            p = p / jnp.sum(p, axis=-1, keepdims=True)
            o = jnp.dot(p.astype(BF16), v, preferred_element_type=F32)
            o_ref[:, h * hd:(h + 1) * hd] = o.astype(BF16)

    return pl.pallas_call(
        body, name=name, grid=(S // ts,),
        in_specs=[pl.BlockSpec((ts, ECA), lambda i: (i, qblk)), pl.BlockSpec((M, 2 * ECA), lambda i: (0, 0)), ANY],
        out_specs=pl.BlockSpec((ts, ECA), lambda i: (i, EMIX // ECA)),
        out_shape=jax.ShapeDtypeStruct(premix.shape, premix.dtype),
        input_output_aliases={2: 0}, compiler_params=_params("parallel"),
    )(proj, kv, premix)


def _ca_bwd(name, dpremix, proj, qblk, kv, dbuf, dblk, S, ECA, EMIX):
    M = kv.shape[0]
    hd = ECA // CA_HEADS
    ts = _tile(S, 512)
    scale = hd ** -0.5

    def body(do_ref, q_ref, kv_ref, _, dq_ref, dkv_ref):
        @pl.when(pl.program_id(0) == 0)
        def _():
            dkv_ref[...] = jnp.zeros_like(dkv_ref)

        for h in range(CA_HEADS):
            lo, hi = h * hd, (h + 1) * hd
            q = q_ref[:, lo:hi]
            k = kv_ref[:, lo:hi]
            v = kv_ref[:, ECA + lo:ECA + hi]
            do = do_ref[:, lo:hi]
            s = lax.dot_general(q, k, (NT, ((), ())), preferred_element_type=F32) * scale
            s = s - jnp.max(s, axis=-1, keepdims=True)
            p = jnp.exp(s)
            p = p / jnp.sum(p, axis=-1, keepdims=True)
            pb = p.astype(BF16)
            dkv_ref[:, ECA + lo:ECA + hi] += lax.dot_general(pb, do, (TN, ((), ())), preferred_element_type=F32)
            dp = lax.dot_general(do, v, (NT, ((), ())), preferred_element_type=F32)
            ds = (p * (dp - jnp.sum(p * dp, axis=-1, keepdims=True)) * scale).astype(BF16)
            dq_ref[:, lo:hi] = jnp.dot(ds, k, preferred_element_type=F32).astype(BF16)
            dkv_ref[:, lo:hi] += lax.dot_general(ds, q, (TN, ((), ())), preferred_element_type=F32)

    return pl.pallas_call(
        body, name=name, grid=(S // ts,),
        in_specs=[pl.BlockSpec((ts, ECA), lambda i: (i, EMIX // ECA)), pl.BlockSpec((ts, ECA), lambda i: (i, qblk)),
                  pl.BlockSpec((M, 2 * ECA), lambda i: (0, 0)), ANY],
        out_specs=[pl.BlockSpec((ts, ECA), lambda i: (i, dblk)), pl.BlockSpec((M, 2 * ECA), lambda i: (0, 0))],
        out_shape=[jax.ShapeDtypeStruct(dbuf.shape, dbuf.dtype), jax.ShapeDtypeStruct((M, 2 * ECA), F32)],
        input_output_aliases={3: 0}, compiler_params=_params("arbitrary"),
    )(dpremix, proj, kv, dbuf)


def _gate_fwd(name, premix, proj, gblk, colscale, S, EB, ECA):
    ts = _tile(S, 512)

    def body(p_ref, g_ref, c_ref, o_ref):
        g = g_ref[...].astype(F32)
        o_ref[...] = (p_ref[...].astype(F32) * c_ref[...] * (g * _sigmoid(g))).astype(BF16)

    return pl.pallas_call(
        body, name=name, grid=(S // ts, EB // ECA),
        in_specs=[pl.BlockSpec((ts, ECA), lambda i, j: (i, j)), pl.BlockSpec((ts, ECA), lambda i, j: (i, gblk + j)),
                  pl.BlockSpec((1, ECA), lambda i, j: (0, j))],
        out_specs=pl.BlockSpec((ts, ECA), lambda i, j: (i, j)),
        out_shape=jax.ShapeDtypeStruct((S, EB), BF16), compiler_params=_params("parallel", "parallel"),
    )(premix, proj, colscale)


def _gate_bwd_epilogue(db, extra_refs, out_refs):
    p_ref, g_ref, c_ref = extra_refs
    dp_ref, dg_ref, dc_ref = out_refs
    g = g_ref[...].astype(F32)
    sg = _sigmoid(g)
    si = g * sg
    c = c_ref[...]
    t = db * p_ref[...].astype(F32)
    dp_ref[...] = (db * si * c).astype(BF16)
    dg_ref[...] = (t * c * (sg * (1.0 + g * (1.0 - sg)))).astype(BF16)
    dc_ref[...] = jnp.sum(t * si, axis=0, keepdims=True)


def _hgrn_lb(lb_ref):
    l0 = lb_ref[0:1, :]
    l1 = lb_ref[1:2, :]
    mx = jnp.maximum(l0, l1)
    e0 = jnp.exp(l0 - mx)
    e1 = jnp.exp(l1 - mx)
    return e1 / (e0 + e1)


def _bdot(a, b, ca, cb):
    return lax.dot_general(a, b, (((ca,), (cb,)), ((0,), (0,))), preferred_element_type=F32)


def _tri_sum(tri, x):
    hi = x.astype(BF16)
    lo = (x - hi.astype(F32)).astype(BF16)
    tri = tri.astype(BF16)
    return _bdot(tri, hi, 2, 1) + _bdot(tri, lo, 2, 1)


def _hgrn_chunks(qin, fin, lbh, n):
    C = HG_CHUNK
    row = lax.broadcasted_iota(jnp.int32, (n, C, C), 1)
    col = lax.broadcasted_iota(jnp.int32, (n, C, C), 2)
    causal = row >= col
    sg = _sigmoid(fin)
    f = lbh + (1.0 - lbh) * sg
    k = 1.0 - f
    g = jnp.log(f)
    b = _tri_sum(jnp.where(causal, 1.0, 0.0), g)
    b_last = jnp.sum(g, axis=1, keepdims=True)
    eb = jnp.exp(b)
    einv = jnp.exp(-b)
    eend = jnp.exp(b_last - b)
    sq = _sigmoid(qin)
    a = qin * sq * (HG_HEAD_DIM ** -0.5) * eb
    bm = k * einv
    e = k * eend
    d = jnp.exp(b_last)
    p = jnp.where(causal, _bdot(a.astype(BF16), bm.astype(BF16), 2, 2), 0.0)
    return dict(causal=causal, sg=sg, f=f, eb=eb, einv=einv, eend=eend, sq=sq, a=a, bm=bm, e=e, d=d, p=p)


def _hgrn_fwd(proj, fgate, hgrn_lb, S, EMIX, EB):
    HD, C = HG_HEAD_DIM, HG_CHUNK
    HH = EMIX // HD
    hb = 6 if HH % 6 == 0 else 1
    W = hb * HD
    tr = _tile(S, 512)
    n = tr // C

    def body(q_ref, f_ref, i_ref, lb_ref, o_ref, rstd_ref, st_ref, state):
        @pl.when(pl.program_id(1) == 0)
        def _():
            state[...] = jnp.zeros_like(state)

        lb = _hgrn_lb(lb_ref)
        for h in range(hb):
            cs = slice(h * HD, (h + 1) * HD)
            qin = q_ref[:, cs].astype(F32).reshape(n, C, HD)
            fin = f_ref[:, cs].reshape(n, C, HD)
            v = i_ref[:, cs].reshape(n, C, HD)
            t = _hgrn_chunks(qin, fin, lb[:, cs], n)
            upd = _bdot(v, t["e"].astype(BF16), 1, 1)
            st = state[h]
            for c in range(n):
                st_ref[h, c] = st
                st = st * t["d"][c] + upd[c]
            state[h] = st
            o = _bdot(t["p"].astype(BF16), v, 2, 1) + _bdot(t["a"].astype(BF16), st_ref[h].astype(BF16), 2, 2)
            rstd = lax.rsqrt(jnp.mean(o * o, axis=-1, keepdims=True) + EPS)
            o_ref[:, cs] = (o * rstd).reshape(tr, HD).astype(BF16)
            rstd_ref[:, cs] = jnp.broadcast_to(rstd, (n, C, HD)).reshape(tr, HD)

    blk = lambda off: pl.BlockSpec((tr, W), lambda g, i: (i, off + g))
    return pl.pallas_call(
        body, name="hgrn_fwd", grid=(HH // hb, S // tr),
        in_specs=[blk(0), blk(0), blk(2 * EMIX // W), pl.BlockSpec((2, W), lambda g, i: (0, g))],
        out_specs=[blk(0), blk(0), pl.BlockSpec((hb, n, HD, HD), lambda g, i: (g, i, 0, 0))],
        out_shape=[jax.ShapeDtypeStruct((S, EB), BF16), jax.ShapeDtypeStruct((S, EMIX), F32),
                   jax.ShapeDtypeStruct((HH, S // C, HD, HD), F32)],
        scratch_shapes=[pltpu.VMEM((hb, HD, HD), F32)],
        compiler_params=_params("parallel", "arbitrary"),
    )(proj, fgate, proj, hgrn_lb)


def _hgrn_bwd(dpremix, premix, rstd, states, proj, fgate, hgrn_lb, S, EMIX):
    HD, C = HG_HEAD_DIM, HG_CHUNK
    HH = EMIX // HD
    hb = 6 if HH % 6 == 0 else 1
    W = hb * HD
    tr = _tile(S, 512)
    n = tr // C
    nrt = S // tr

    def body(do_ref, on_ref, rstd_ref, st_ref, q_ref, f_ref, i_ref, lb_ref, d_ref, dlb_ref, dstate, dsbuf):
        @pl.when(pl.program_id(1) == 0)
        def _():
            dstate[...] = jnp.zeros_like(dstate)
            dlb_ref[...] = jnp.zeros_like(dlb_ref)

        lb = _hgrn_lb(lb_ref)
        for h in range(hb):
            cs = slice(h * HD, (h + 1) * HD)
            qin = q_ref[:, cs].astype(F32).reshape(n, C, HD)
            fin = f_ref[:, cs].reshape(n, C, HD)
            v = i_ref[:, cs].reshape(n, C, HD)
            lbh = lb[:, cs]
            t = _hgrn_chunks(qin, fin, lbh, n)
            a, bm, e, d, p = t["a"], t["bm"], t["e"], t["d"], t["p"]
            ab, bmb, eb16 = a.astype(BF16), bm.astype(BF16), e.astype(BF16)
            on = on_ref[:, cs].astype(F32).reshape(n, C, HD)
            dn = do_ref[:, cs].astype(F32).reshape(n, C, HD)
            do = rstd_ref[:, cs].reshape(n, C, HD) * (dn - on * jnp.mean(dn * on, axis=-1, keepdims=True))
            dob = do.astype(BF16)
            grow = _bdot(dob, ab, 1, 1)
            ds = dstate[h]
            for c in reversed(range(n)):
                dsbuf[h, c] = ds
                ds = ds * d[c] + grow[c]
            dstate[h] = ds
            dst = dsbuf[h]
            st = st_ref[h]
            dstb = dst.astype(BF16)
            dp = jnp.where(t["causal"], _bdot(dob, v, 2, 2), 0.0).astype(BF16)
            dv = _bdot(p.astype(BF16), dob, 1, 1) + _bdot(eb16, dstb, 2, 2)
            da = _bdot(dp, bmb, 2, 1) + _bdot(dob, st.astype(BF16), 2, 1)
            dbm = _bdot(dp, ab, 1, 1)
            de = _bdot(v, dstb, 2, 1)
            dd = jnp.sum(dst * st, axis=1, keepdims=True)
            dk = dbm * t["einv"] + de * t["eend"]
            dee = de * e
            db = da * a - dbm * bm - dee
            extra = jnp.sum(dee, axis=1, keepdims=True) + dd * d
            upper = jnp.where(lax.broadcasted_iota(jnp.int32, (n, C, C), 2)
                              >= lax.broadcasted_iota(jnp.int32, (n, C, C), 1), 1.0, 0.0)
            dg = _tri_sum(upper, db) + extra
            df = dg / t["f"] - dk
            sg, sq = t["sg"], t["sq"]
            dq = da * t["eb"] * (HD ** -0.5) * (sq * (1.0 + qin * (1.0 - sq)))
            d_ref[0, :, cs] = dq.reshape(tr, HD).astype(BF16)
            d_ref[1, :, cs] = (df * (1.0 - lbh) * sg * (1.0 - sg)).reshape(tr, HD).astype(BF16)
            d_ref[2, :, cs] = dv.reshape(tr, HD).astype(BF16)
            dlb_ref[:, cs] += jnp.sum((df * (1.0 - sg)).reshape(tr, HD), axis=0, keepdims=True)

    rev = lambda off: pl.BlockSpec((tr, W), lambda g, s: (nrt - 1 - s, off + g))
    return pl.pallas_call(
        body, name="hgrn_bwd", grid=(HH // hb, nrt),
        in_specs=[rev(0), rev(0), rev(0), pl.BlockSpec((hb, n, HD, HD), lambda g, s: (g, nrt - 1 - s, 0, 0)),
                  rev(0), rev(0), rev(2 * EMIX // W), pl.BlockSpec((2, W), lambda g, s: (0, g))],
        out_specs=[pl.BlockSpec((3, tr, W), lambda g, s: (0, nrt - 1 - s, g)), pl.BlockSpec((1, W), lambda g, s: (0, g))],
        out_shape=[jax.ShapeDtypeStruct((3, S, EMIX), BF16), jax.ShapeDtypeStruct((1, EMIX), F32)],
        scratch_shapes=[pltpu.VMEM((hb, HD, HD), F32), pltpu.VMEM((hb, n, HD, HD), F32)],
        compiler_params=_params("parallel", "arbitrary"),
    )(dpremix, premix, rstd, states, proj, fgate, proj, hgrn_lb)


EW_BLOCK_ELEMS = 512 * 1024


def _ew_tiles(R, C):
    tc = C if C <= 4096 else _tile(C, 2048)
    tr = _tile(R, 512)
    while tr * tc > EW_BLOCK_ELEMS and tr % 16 == 0:
        tr //= 2
    return tr, tc


def _add_halves(name, core_chip, grad, got):
    _, _, R, C = grad.shape
    tr, tc = _ew_tiles(R, C)

    def body(c_ref, a_ref, b_ref, o_ref, own_ref):
        r = (a_ref[...].astype(F32) + b_ref[...].astype(F32)).astype(BF16)
        o_ref[...] = r

        @pl.when(pl.program_id(2) == c_ref[1])
        def _():
            own_ref[...] = r

    blk = pl.BlockSpec((None, tr, tc), lambda i, j, s, c: (s, i, j))
    sds = jax.ShapeDtypeStruct(got.shape, BF16)
    return pl.pallas_call(
        body, name=name, out_shape=[sds, sds],
        grid_spec=pltpu.PrefetchScalarGridSpec(
            num_scalar_prefetch=1, grid=(R // tr, C // tc, N_CHIPS),
            in_specs=[pl.BlockSpec((None, None, tr, tc), lambda i, j, s, c: (s, c[0], i, j)), blk],
            out_specs=[blk, pl.BlockSpec((None, tr, tc), lambda i, j, s, c: (c[1], i, j))]),
        compiler_params=_params("parallel", "parallel", "arbitrary"),
    )(core_chip, grad, got)


def _adam_step(w, g, m, v):
    mn = ADAM_B1 * m + (1.0 - ADAM_B1) * g
    vn = ADAM_B2 * v + (1.0 - ADAM_B2) * (g * g)
    m_hat = mn / (1.0 - ADAM_B1 ** ADAM_STEP)
    v_hat = vn / (1.0 - ADAM_B2 ** ADAM_STEP)
    return -ADAM_LR * (m_hat / (jnp.sqrt(v_hat) + ADAM_EPS) + ADAM_WD * w), mn, vn


def _adamw(name, w, g, m, v):
    R, C = w.shape
    tr, tc = _ew_tiles(R, C)

    def body(w_ref, g_ref, m_ref, v_ref, d_ref, mo_ref, vo_ref):
        d_ref[...], mo_ref[...], vo_ref[...] = _adam_step(w_ref[...], g_ref[...], m_ref[...], v_ref[...])

    blk = pl.BlockSpec((tr, tc), lambda i, j: (i, j))
    sds = jax.ShapeDtypeStruct((R, C), F32)
    return pl.pallas_call(
        body, name=name, grid=(R // tr, C // tc), in_specs=[blk] * 4, out_specs=[blk] * 3, out_shape=[sds] * 3,
        compiler_params=_params("parallel", "parallel"),
    )(w, g, m, v)


def _adamw_layers(name, w, gs, m, v):
    L, R, C = w.shape
    tr, tc = _ew_tiles(R, C)

    def body(*refs):
        w_ref, m_ref, v_ref = refs[:3]
        g_refs = refs[3:3 + L]
        go_ref, d_ref, mo_ref, vo_ref = refs[3 + L:]
        layer = pl.program_id(0)
        g = g_refs[0][...]
        for n in range(1, L):
            g = jnp.where(layer == n, g_refs[n][...], g)
        go_ref[...] = g
        d_ref[...], mo_ref[...], vo_ref[...] = _adam_step(w_ref[...], g, m_ref[...], v_ref[...])

    blk = pl.BlockSpec((None, tr, tc), lambda l, i, j: (l, i, j))
    of_layer = lambda n: pl.BlockSpec((tr, tc), lambda l, i, j: (jnp.where(l == n, i, 0), jnp.where(l == n, j, 0)))
    sds = jax.ShapeDtypeStruct((L, R, C), F32)
    return pl.pallas_call(
        body, name=name, grid=(L, R // tr, C // tc), in_specs=[blk] * 3 + [of_layer(n) for n in range(L)],
        out_specs=[blk] * 4, out_shape=[sds] * 4, compiler_params=_params("parallel", "parallel", "parallel"),
    )(w, m, v, *gs)


def _pack_rows(name, items, W):
    nv = len(items)
    first, row = [], 0
    for a, add, _ in items:
        first.append(row)
        row += 1 if add else a.shape[0]
    assert row <= SMALL_ROWS

    def body(*refs):
        o_ref = refs[nv]
        o_ref[...] = jnp.zeros_like(o_ref)
        for i, (a, add, width) in enumerate(items):
            val = refs[i][:, 0:width]
            if add:
                val = jnp.sum(val, axis=0, keepdims=True)
            o_ref[first[i]:first[i] + val.shape[0], 0:width] = val

    vm = pl.BlockSpec(memory_space=pltpu.VMEM)
    return pl.pallas_call(
        body, name=name, in_specs=[vm] * nv, out_specs=vm, out_shape=jax.ShapeDtypeStruct((SMALL_ROWS, W), F32),
    )(*[a for a, _, _ in items])


def _unpack_rows(name, pack, layout):
    def body(p_ref, *o_refs):
        row = 0
        for o_ref, (k, n) in zip(o_refs, layout):
            o_ref[...] = p_ref[row:row + k, 0:n]
            row += k

    vm = pl.BlockSpec(memory_space=pltpu.VMEM)
    return pl.pallas_call(
        body, name=name, in_specs=[vm], out_specs=[vm] * len(layout),
        out_shape=[jax.ShapeDtypeStruct(s, F32) for s in layout],
    )(pack)


def _small_sum(gathered, hgrn_lb, chip, nshard):
    _, T, W = gathered.shape

    def body(c_ref, g_ref, lb_ref, o_ref, loss_ref, tmp):
        acc = g_ref[0]
        for dev in range(1, N_DEV):
            acc = acc + g_ref[dev]
        tmp[...] = acc
        lb = _hgrn_lb(lb_ref)
        d1 = tmp[4:5, :] * (lb * (1.0 - lb))
        o_ref[...] = jnp.zeros_like(o_ref)
        o_ref[0:4, :] = tmp[0:4, :]
        o_ref[4:5, :] = -d1
        o_ref[5:6, :] = d1
        mine = tmp[5:6, 0:nshard]
        for b in range(1, N_CHIPS):
            mine = jnp.where(c_ref[0] == b, tmp[5:6, b * nshard:(b + 1) * nshard], mine)
        o_ref[6:7, 0:nshard] = mine
        o_ref[7:8, :] = tmp[6:7, :]
        loss_ref[...] = tmp[7:8, 0:128]

    vm = pl.BlockSpec(memory_space=pltpu.VMEM)
    return pl.pallas_call(
        body, name="small_sum", in_specs=[pl.BlockSpec(memory_space=pltpu.SMEM), vm, vm], out_specs=[vm, vm],
        out_shape=[jax.ShapeDtypeStruct((T, W), F32), jax.ShapeDtypeStruct((1, 128), F32)],
        scratch_shapes=[pltpu.VMEM((T, W), F32)],
    )(chip, gathered, hgrn_lb)


def _place():
    return lax.axis_index("x"), lax.axis_index("y"), lax.axis_index("c")


def _other_chips(x, y):
    return [(1 - x, y), (x, 1 - y), (1 - x, 1 - y)]


def _chunk_rows(rows, row_bytes):
    cr = rows
    while cr * row_bytes > STREAM_CHUNK_BYTES and cr % 32 == 0:
        cr //= 2
    return cr


def _stream(pairs, buf, sems, t, peer):
    lsem, ssem, rsem = sems
    n = len(pairs)
    loads, sent = [None] * n, [None] * n

    def load(k):
        slot = k % STREAM_SLOTS
        if k >= STREAM_SLOTS:
            sent[k - STREAM_SLOTS]()
        loads[k] = pltpu.make_async_copy(pairs[k][0], buf.at[slot], lsem.at[t, slot])
        loads[k].start()

    load(0)
    for k in range(n):
        slot = k % STREAM_SLOTS
        if k + 1 < n:
            load(k + 1)
        loads[k].wait()
        cp = pltpu.make_async_remote_copy(src_ref=buf.at[slot], dst_ref=pairs[k][1], send_sem=ssem.at[t, slot],
                                          recv_sem=rsem.at[t], device_id=peer, device_id_type=MESH)
        cp.start()
        sent[k] = cp.wait_send
    for k in range(max(0, n - STREAM_SLOTS), n):
        sent[k]()


def _stream_scratch(shapes):
    nt = len(shapes)
    return ([pltpu.VMEM((STREAM_SLOTS,) + s, d) for s, d in shapes]
            + [pltpu.SemaphoreType.DMA((nt, STREAM_SLOTS)), pltpu.SemaphoreType.DMA((nt, STREAM_SLOTS)),
               pltpu.SemaphoreType.DMA((nt,))])


def _exchange_halves(name, grads):
    nt = len(grads)
    hs = [g.shape[1] // 2 for g in grads]
    crs = [_chunk_rows(h, g.shape[2] * g.dtype.itemsize) for h, g in zip(hs, grads)]

    def body(*refs):
        ins, gots, bufs, sems = refs[:nt], refs[nt:2 * nt], refs[2 * nt:3 * nt], refs[3 * nt:]
        x, y, c = _place()
        sib = (x, y, 1 - c)
        for t in range(nt):
            h, cr = hs[t], crs[t]
            pairs = [(ins[t].at[b, pl.ds((1 - c) * h + r0, cr)], gots[t].at[b, pl.ds(r0, cr)])
                     for b in range(N_CHIPS) for r0 in range(0, h, cr)]
            _stream(pairs, bufs[t], sems, t, sib)
        for t in range(nt):
            pltpu.make_async_remote_copy(src_ref=gots[t], dst_ref=gots[t], send_sem=sems[1].at[t, 0],
                                         recv_sem=sems[2].at[t], device_id=sib, device_id_type=MESH).wait_recv()

    return pl.pallas_call(
        body, name=name, in_specs=[ANY] * nt, out_specs=[ANY] * nt,
        out_shape=[jax.ShapeDtypeStruct((N_CHIPS, h, g.shape[2]), g.dtype) for h, g in zip(hs, grads)],
        scratch_shapes=_stream_scratch([((cr, g.shape[2]), g.dtype) for cr, g in zip(crs, grads)]),
        compiler_params=pltpu.CompilerParams(vmem_limit_bytes=VMEM_LIMIT_BYTES),
    )(*grads)


def _scatter_plan(srcs, dsts):
    x, y, c = _place()
    me = 2 * x + y
    return [(srcs[t].at[2 * px + py], dsts[t].at[me], (px, py, c))
            for t in range(len(srcs)) for px, py in _other_chips(x, y)]


def _slot(dst, chip, r0, rows, cols):
    if len(dst.shape) == 3:
        return dst.at[chip, pl.ds(r0, rows)]
    return dst.at[pl.ds(r0, rows), pl.ds(pl.multiple_of(chip * cols, 128), cols)]


def _shard_dims(gathered):
    s = gathered.shape
    return (s[1], s[2]) if len(s) == 3 else (s[0], s[1] // N_CHIPS)


def _gather_plan(_, bufs):
    x, y, c = _place()
    me = 2 * x + y
    plan = []
    for buf in bufs:
        rows, cols = _shard_dims(buf)
        mine = _slot(buf, me, c * (rows // 2), rows // 2, cols)
        plan += [(mine, mine, (px, py, c)) for px, py in _other_chips(x, y)]
    return plan


def _cast_to_slot(name, chip, w, layer, gathered_shape, after):
    _, R, C = w.shape
    tr, tc = _ew_tiles(R, C)

    def body(c_ref, w_ref, *rest):
        rest[-1][...] = w_ref[...].astype(BF16)

    if len(gathered_shape) == 3:
        out_spec = pl.BlockSpec((None, tr, tc), lambda i, j, c: (c[0], i, j))
    else:
        out_spec = pl.BlockSpec((tr, tc), lambda i, j, c: (i, c[0] * (C // tc) + j))
    extra = [] if after is None else [after]
    return pl.pallas_call(
        body, name=name, out_shape=jax.ShapeDtypeStruct(gathered_shape, BF16),
        grid_spec=pltpu.PrefetchScalarGridSpec(
            num_scalar_prefetch=1, grid=(R // tr, C // tc),
            in_specs=[pl.BlockSpec((None, tr, tc), lambda i, j, c: (layer, i, j))] + [ANY] * len(extra),
            out_specs=out_spec),
        compiler_params=_params("parallel", "parallel"),
    )(chip, w, *extra)


HBM_SPEC = pl.BlockSpec(memory_space=pltpu.HBM)
SEM_SPEC = pl.BlockSpec(memory_space=pltpu.SEMAPHORE)


def _split_start(name, srcs, dsts, plan, ncopies, after):
    bufs = [pltpu.with_memory_space_constraint(a, pltpu.HBM) for a in list(srcs) + list(dsts)]
    nb, ns = len(bufs), len(srcs)
    operands = bufs + ([after] if after is not None else [])

    def body(*refs):
        outs = refs[len(operands):]
        send, recv, token = outs[0], outs[1], outs[-1]
        for i, (src, dst, dev) in enumerate(plan(refs[:ns], refs[ns:nb])):
            pltpu.make_async_remote_copy(src_ref=src, dst_ref=dst, send_sem=send.at[i], recv_sem=recv.at[i],
                                         device_id=dev, device_id_type=MESH).start()
        token[...] = jnp.zeros_like(token)

    res = pl.pallas_call(
        body, name=name,
        out_shape=[pltpu.SemaphoreType.DMA((ncopies,)), pltpu.SemaphoreType.DMA((ncopies,))]
        + [pltpu.HBM(a.shape, a.dtype) for a in bufs] + [jax.ShapeDtypeStruct((8, 128), F32)],
        in_specs=[HBM_SPEC] * nb + [ANY] * (len(operands) - nb),
        out_specs=[SEM_SPEC, SEM_SPEC] + [HBM_SPEC] * nb + [pl.BlockSpec(memory_space=pltpu.VMEM)],
        input_output_aliases={i: 2 + i for i in range(nb)},
        compiler_params=pltpu.CompilerParams(has_side_effects=pltpu.SideEffectType.DATAFLOW_SIDE_EFFECTING),
    )(*operands)
    return res[:-1], res[-1]


def _split_wait(name, started, plan, ns, after):
    send, recv, bufs = started[0], started[1], list(started[2:])
    nb = len(bufs)

    def body(*refs):
        send_ref, recv_ref = refs[nb], refs[nb + 1]
        for i, (src, dst, dev) in enumerate(plan(refs[:ns], refs[ns:nb])):
            cp = pltpu.make_async_remote_copy(src_ref=src, dst_ref=dst, send_sem=send_ref.at[i], recv_sem=recv_ref.at[i],
                                              device_id=dev, device_id_type=MESH)
            cp.wait_send()
            cp.wait_recv()

    res = pl.pallas_call(
        body, name=name, out_shape=[pltpu.HBM(a.shape, a.dtype) for a in bufs],
        in_specs=[HBM_SPEC] * nb + [SEM_SPEC, SEM_SPEC, ANY], out_specs=[HBM_SPEC] * nb,
        input_output_aliases={i: i for i in range(nb)},
        compiler_params=pltpu.CompilerParams(has_side_effects=pltpu.SideEffectType.DATAFLOW_SIDE_EFFECTING),
    )(*bufs, send, recv, after)
    return res[:ns], res[ns:]


def _gather_finish(name, _, gathered):
    nt = len(gathered)
    dims = [_shard_dims(g) for g in gathered]
    hs = [rows // 2 for rows, _ in dims]
    crs = [_chunk_rows(h, cols * 2) for h, (_, cols) in zip(hs, dims)]

    def body(*refs):
        outs, bufs, sems = refs[nt:2 * nt], refs[2 * nt:3 * nt], refs[3 * nt:]
        x, y, c = _place()
        sib = (x, y, 1 - c)
        for t in range(nt):
            h, cr, cols = hs[t], crs[t], dims[t][1]
            passed = [_slot(outs[t], 2 * px + py, c * h + r0, cr, cols)
                      for px, py in _other_chips(x, y) for r0 in range(0, h, cr)]
            _stream([(r, r) for r in passed], bufs[t], sems, t, sib)
        for t in range(nt):
            if len(gathered[t].shape) == 3:
                three = outs[t].at[pl.ds(0, 3), pl.ds(0, hs[t])]
            else:
                three = outs[t].at[pl.ds(0, hs[t]), pl.ds(0, 3 * dims[t][1])]
            pltpu.make_async_remote_copy(src_ref=three, dst_ref=three, send_sem=sems[1].at[t, 0],
                                         recv_sem=sems[2].at[t], device_id=sib, device_id_type=MESH).wait_recv()

    return pl.pallas_call(
        body, name=name, in_specs=[ANY] * nt, out_specs=[ANY] * nt,
        out_shape=[jax.ShapeDtypeStruct(g.shape, g.dtype) for g in gathered],
        scratch_shapes=_stream_scratch([((cr, cols), BF16) for cr, (_, cols) in zip(crs, dims)]),
        input_output_aliases={t: t for t in range(nt)},
        compiler_params=pltpu.CompilerParams(vmem_limit_bytes=VMEM_LIMIT_BYTES),
    )(*gathered)


def _sum_share(landed):
    nt = len(landed)
    hs = [a.shape[1] for a in landed]
    cs = [a.shape[2] for a in landed]
    crs = [_chunk_rows(h, 2 * c * 4) for h, c in zip(hs, cs)]
    shapes = sorted(set(zip(crs, cs)))
    which = [shapes.index(s) for s in zip(crs, cs)]

    def body(*refs):
        ins, outs = refs[:nt], refs[nt:2 * nt]
        inbufs, outbufs = refs[2 * nt:2 * nt + len(shapes)], refs[2 * nt + len(shapes):2 * nt + 2 * len(shapes)]
        lsem, ssem, osem, rsem = refs[2 * nt + 2 * len(shapes):]
        x, y, c = _place()
        sib = (x, y, 1 - c)
        for t in range(nt):
            cr, n, ib, ob = crs[t], hs[t] // crs[t], inbufs[which[t]], outbufs[which[t]]
            loads, gone = [None] * n, [None] * n

            def load(k):
                slot = k % SUM_SLOTS
                if k >= SUM_SLOTS:
                    for cp_wait in gone[k - SUM_SLOTS]:
                        cp_wait()
                loads[k] = pltpu.make_async_copy(ins[t].at[:, pl.ds(k * cr, cr)], ib.at[slot], lsem.at[t, slot])
                loads[k].start()

            load(0)
            for k in range(n):
                slot = k % SUM_SLOTS
                if k + 1 < n:
                    load(k + 1)
                loads[k].wait()
                acc = ib[slot, 0].astype(F32)
                for s in range(1, N_CHIPS):
                    acc = acc + ib[slot, s].astype(F32)
                ob[slot] = acc
                rows = outs[t].at[c, pl.ds(k * cr, cr)]
                away = pltpu.make_async_remote_copy(src_ref=ob.at[slot], dst_ref=rows, send_sem=ssem.at[t, slot],
                                                    recv_sem=rsem.at[t], device_id=sib, device_id_type=MESH)
                away.start()
                home = pltpu.make_async_copy(ob.at[slot], rows, osem.at[t, slot])
                home.start()
                gone[k] = (away.wait_send, home.wait)
            for k in range(max(0, n - SUM_SLOTS), n):
                for cp_wait in gone[k]:
                    cp_wait()
        for t in range(nt):
            other = outs[t].at[1 - c]
            pltpu.make_async_remote_copy(src_ref=other, dst_ref=other, send_sem=ssem.at[t, 0], recv_sem=rsem.at[t],
                                         device_id=sib, device_id_type=MESH).wait_recv()

    slot_sems = pltpu.SemaphoreType.DMA((nt, SUM_SLOTS))
    return pl.pallas_call(
        body, name="sum_share", in_specs=[ANY] * nt, out_specs=[ANY] * nt,
        out_shape=[jax.ShapeDtypeStruct((2, h, c), F32) for h, c in zip(hs, cs)],
        scratch_shapes=[pltpu.VMEM((SUM_SLOTS, N_CHIPS, cr, c), BF16) for cr, c in shapes]
        + [pltpu.VMEM((SUM_SLOTS, cr, c), F32) for cr, c in shapes]
        + [slot_sems, slot_sems, slot_sems, pltpu.SemaphoreType.DMA((nt,))],
        compiler_params=pltpu.CompilerParams(vmem_limit_bytes=VMEM_LIMIT_BYTES),
    )(*landed)


def _allgather_small(name, v):
    def body(v_ref, o_ref, send, recv, lsem):
        x, y, c = _place()
        me = 4 * x + 2 * y + c
        loc = pltpu.make_async_copy(v_ref, o_ref.at[me], lsem)
        loc.start()
        copies = []
        for k in range(1, N_DEV):
            px = 1 - x if k & 4 else x
            py = 1 - y if k & 2 else y
            pc = 1 - c if k & 1 else c
            cp = pltpu.make_async_remote_copy(
                src_ref=v_ref, dst_ref=o_ref.at[me], send_sem=send.at[k - 1], recv_sem=recv.at[k - 1],
                device_id=(px, py, pc), device_id_type=MESH)
            cp.start()
            copies.append(cp)
        for cp in copies:
            cp.wait()
        loc.wait()

    vm = pl.BlockSpec(memory_space=pltpu.VMEM)
    return pl.pallas_call(
        body, name=name, in_specs=[vm], out_specs=vm,
        out_shape=jax.ShapeDtypeStruct((N_DEV,) + v.shape, v.dtype),
        scratch_shapes=[pltpu.SemaphoreType.DMA((N_DEV - 1,))] * 2 + [pltpu.SemaphoreType.DMA],
    )(v)


def kernel(x, mem, norm_g, mem_norm_g, w_kv, w_out, pool_w_in, pool_w_grp, pool_scale, hgrn_w_in, hgrn_lb, hgrn_norm_g, final_g, loss_target, m_norm_g, m_mem_norm_g, m_w_kv, m_w_out, m_pool_w_in, m_pool_w_grp, m_pool_scale, m_hgrn_w_in, m_hgrn_lb, m_hgrn_norm_g, m_final_g, v_norm_g, v_mem_norm_g, v_w_kv, v_w_out, v_pool_w_in, v_pool_w_grp, v_pool_scale, v_hgrn_w_in, v_hgrn_lb, v_hgrn_norm_g, v_final_g):
    _, S, D = x.shape
    M = mem.shape[1]
    EB = 2 * D
    ECA = EB // 4
    EMIX = EB - ECA
    PG = EMIX // N_POOL_GROUPS
    NP0 = EMIX + ECA + EB
    NP1 = 3 * EMIX + ECA + EB
    SH0, SH1 = NP0 // N_CHIPS, NP1 // N_CHIPS
    DK, EK = D // N_CHIPS, EB // N_CHIPS
    TNP = 512 if all(v % 512 == 0 for v in (SH0, SH1, ECA, EMIX)) else 256
    TM = _tile(S, 1024)
    TMF = _tile(S, 2048)
    TD = _tile(D, 512)
    TDW = _tile(D, 1024)
    c0, c1 = SH0 // TNP, SH1 // TNP
    qt = EMIX // TNP
    chip = 2 * lax.axis_index("x") + lax.axis_index("y")

    xs, ms, tgt = x[0], mem[0], loss_target[0]

    sds = jax.ShapeDtypeStruct
    chip1 = chip.astype(jnp.int32).reshape(1)

    def start_gather(tag, layers, after):
        bufs = []
        for t, (w, layer) in enumerate(layers):
            w3 = w.reshape((w.shape[0], -1, w.shape[-1]))
            shape = (D, NP1) if w is hgrn_w_in else (N_CHIPS,) + w3.shape[1:]
            bufs.append(_cast_to_slot(f"cast_{tag}{t}", chip1, w3, layer, shape, after))
        return _split_start(f"gather_{tag}_start", [], bufs, _gather_plan, 3 * len(bufs), after)

    gather_a, token = start_gather("a", [(pool_w_in, 0)], None)
    gather_b, token = start_gather("b", [(w_kv, 0), (w_out, 0), (pool_w_grp, 0)], token)
    gather_c, token = start_gather("c", [(hgrn_w_in, 0)], token)
    gather_d, token = start_gather("d", [(w_kv, 1), (w_out, 1)], token)

    tek, tew = _tile(EK, 512), _tile(EK, 1024)

    mem_n = _rms_fwd("rms_mem", ms, mem_norm_g.reshape(1, D), token)
    h0 = _rms_fwd("rms0", xs, norm_g[0:1], token)
    wpin, = _gather_finish("gather_a_finish", *_split_wait("gather_a_wait", gather_a, _gather_plan, 0, h0))

    tkw = _tile(2 * ECA, 1024)

    def kv_of(layer, wkv):
        return _matmul(
            f"kv{layer}", mem_n, wkv.reshape(D, 2 * ECA), grid=(1, 2 * ECA // tkw, 1),
            a_spec=pl.BlockSpec((M, D), lambda i, j, k: (0, 0)), b_spec=pl.BlockSpec((D, tkw), lambda i, j, k: (0, j)),
            out_shape=sds((M, 2 * ECA), BF16), out_spec=pl.BlockSpec((M, tkw), lambda i, j, k: (0, j)),
            acc_shape=(M, tkw), dims=NN)

    tko = _tile(EB, 2048)

    def out_proj(layer, branch, wout, resid):
        return _matmul(
            f"out_proj{layer}", branch, wout.reshape(EB, D), grid=(S // TM, D // TDW, EB // tko),
            a_spec=pl.BlockSpec((TM, tko), IK), b_spec=pl.BlockSpec((tko, TDW), KJ),
            out_shape=sds((S, D), F32), out_spec=pl.BlockSpec((TM, TDW), IJ),
            acc_shape=(TM, TDW), dims=NN, add=resid, add_spec=pl.BlockSpec((TM, TDW), IJ))

    ones_ca = jnp.ones((1, ECA), F32)

    proj0 = _matmul(
        "proj0", h0, wpin, grid=(S // TMF, NP0 // TNP, 1),
        a_spec=pl.BlockSpec((TMF, D), lambda i, j, k: (i, 0)),
        b_spec=pl.BlockSpec((None, D, TNP), lambda i, j, k: (j // c0, 0, j % c0)),
        out_shape=sds((S, NP0), BF16), out_spec=pl.BlockSpec((TMF, TNP), IJ),
        acc_shape=(TMF, TNP), dims=NN)
    pooled = _pool_fwd(proj0, S, EMIX)
    wkv0, wout0, g_grp = _gather_finish("gather_b_finish", *_split_wait("gather_b_wait", gather_b, _gather_plan, 0, pooled))
    wgrp = g_grp.reshape(N_CHIPS, N_POOL_GROUPS, PG // N_CHIPS, PG).transpose(1, 0, 2, 3).reshape(N_POOL_GROUPS, PG, PG)
    kv = [kv_of(0, wkv0), None]
    premix0 = _matmul(
        "pool_grp", pooled, wgrp, grid=(S // TM, N_POOL_GROUPS, 1),
        a_spec=pl.BlockSpec((TM, PG), lambda i, j, k: (i, j)),
        b_spec=pl.BlockSpec((None, PG, PG), lambda i, j, k: (j, 0, 0)),
        out_shape=sds((S, EB), BF16), out_spec=pl.BlockSpec((TM, PG), lambda i, j, k: (i, j)),
        acc_shape=(TM, PG), dims=NN)
    premix0 = _ca_fwd("ca_fwd0", proj0, EMIX // ECA, kv[0], premix0, S, ECA, EMIX)
    colscale0 = jnp.concatenate([pool_scale.reshape(1, EMIX), ones_ca], axis=1)
    gblk0 = (EMIX + ECA) // ECA
    branch0 = _gate_fwd("gate_fwd0", premix0, proj0, gblk0, colscale0, S, EB, ECA)
    x1 = out_proj(0, branch0, wout0, xs)

    whin, = _gather_finish("gather_c_finish", *_split_wait("gather_c_wait", gather_c, _gather_plan, 0, x1))
    h1 = _rms_fwd("rms1", x1, norm_g[1:2])

    def proj1_cols(name, ncols, col_of, out_cols, out_dtype, out_col_of):
        return _matmul(
            name, h1, whin, grid=(S // TMF, ncols, 1),
            a_spec=pl.BlockSpec((TMF, D), lambda i, j, k: (i, 0)),
            b_spec=pl.BlockSpec((D, TNP), lambda i, j, k: (0, col_of(j))),
            out_shape=sds((S, out_cols), out_dtype), out_spec=pl.BlockSpec((TMF, TNP), lambda i, j, k: (i, out_col_of(j))),
            acc_shape=(TMF, TNP), dims=NN)

    skip_f = lambda j: jnp.where(j < qt, j, j + qt)
    proj1 = proj1_cols("proj1", NP1 // TNP - qt, skip_f, NP1, BF16, skip_f)
    fgate = proj1_cols("proj1_f", qt, lambda j: j + qt, EMIX, F32, lambda j: j)
    premix1, rstd1, states = _hgrn_fwd(proj1, fgate, hgrn_lb, S, EMIX, EB)
    wkv1, wout1 = _gather_finish("gather_d_finish", *_split_wait("gather_d_wait", gather_d, _gather_plan, 0, rstd1))
    kv[1] = kv_of(1, wkv1)
    premix1 = _ca_fwd("ca_fwd1", proj1, 3 * EMIX // ECA, kv[1], premix1, S, ECA, EMIX)
    norm_tiles = _allgather_small("allgather_norm_g", jnp.pad(hgrn_norm_g, ((0, SMALL_ROWS - 1), (0, 0))))
    hg_norm = norm_tiles[0::2, 0, :].reshape(1, EMIX)
    colscale1 = jnp.concatenate([hg_norm, ones_ca], axis=1)
    gblk1 = (3 * EMIX + ECA) // ECA
    branch1 = _gate_fwd("gate_fwd1", premix1, proj1, gblk1, colscale1, S, EB, ECA)
    x2 = out_proj(1, branch1, wout1, x1)

    dx2, dx2b, d_final_g, loss_part = _loss_head(x2, final_g.reshape(1, D), tgt)

    def out_proj_bwd(layer, dxb, branch, wout, premix, proj, gblk, colscale, dshape, dblk):
        goff, doff = gblk * ECA // tek, dblk * ECA // tek
        dpremix, dgate, dcol = _matmul(
            f"dbranch{layer}", dxb, wout, grid=(S // TMF, EB // tek, 1),
            a_spec=pl.BlockSpec((TMF, D), lambda i, j, k: (i, 0)),
            b_spec=pl.BlockSpec((None, tek, D), lambda i, j, k: (j // (EK // tek), j % (EK // tek), 0)),
            extras=[(premix, pl.BlockSpec((TMF, tek), IJ)), (proj, pl.BlockSpec((TMF, tek), lambda i, j, k: (i, goff + j))),
                    (colscale, pl.BlockSpec((1, tek), lambda i, j, k: (0, j)))],
            epilogue=_gate_bwd_epilogue,
            out_shape=[sds((S, EB), BF16), sds(dshape, BF16), sds((S // TMF, 1, EB), F32)],
            out_spec=[pl.BlockSpec((TMF, tek), IJ), pl.BlockSpec((TMF, tek), lambda i, j, k: (i, doff + j)),
                      pl.BlockSpec((None, 1, tek), lambda i, j, k: (i, 0, j))],
            acc_shape=(TMF, tek), dims=NT)
        dw = _matmul(
            f"dwout{layer}", branch, dxb, grid=(EB // tew, D // TD, 1),
            a_spec=pl.BlockSpec((S, tew), lambda i, j, k: (0, i)), b_spec=pl.BlockSpec((S, TD), lambda i, j, k: (0, j)),
            out_shape=sds((N_CHIPS, EK, D), BF16),
            out_spec=pl.BlockSpec((None, tew, TD), lambda i, j, k: (i // (EK // tew), i % (EK // tew), j)),
            acc_shape=(tew, TD), dims=TN)
        return dpremix, dgate, dcol.reshape(S // TMF, EB), dw

    def kv_bwd(layer, dkv, wkv, dmem_add):
        dkvb = dkv.astype(BF16)
        dmem = _matmul(
            f"dmem{layer}", dkvb, wkv.reshape(D, 2 * ECA), grid=(1, D // TDW, 1),
            a_spec=pl.BlockSpec((M, 2 * ECA), lambda i, j, k: (0, 0)),
            b_spec=pl.BlockSpec((TDW, 2 * ECA), lambda i, j, k: (j, 0)),
            out_shape=sds((M, D), F32), out_spec=pl.BlockSpec((M, TDW), lambda i, j, k: (0, j)), acc_shape=(M, TDW),
            dims=NT, add=dmem_add, add_spec=pl.BlockSpec((M, TDW), lambda i, j, k: (0, j)))
        dw = _matmul(
            f"dwkv{layer}", mem_n, dkvb, grid=(D // TDW, 2 * ECA // tkw, 1),
            a_spec=pl.BlockSpec((M, TDW), lambda i, j, k: (0, i)), b_spec=pl.BlockSpec((M, tkw), lambda i, j, k: (0, j)),
            out_shape=sds((D, 2 * ECA), BF16), out_spec=pl.BlockSpec((TDW, tkw), IJ), acc_shape=(TDW, tkw), dims=TN)
        return dmem, dw.reshape(N_CHIPS, DK, 2 * ECA)

    dpremix1, drest1, dcol1, gw_out1 = out_proj_bwd(1, dx2b, branch1, wout1, premix1, proj1, gblk1, colscale1,
                                                    (S, ECA + EB), 1)
    drest1, dkv1 = _ca_bwd("ca_bwd1", dpremix1, proj1, 3 * EMIX // ECA, kv[1], drest1, 0, S, ECA, EMIX)
    dqfi, dlb = _hgrn_bwd(dpremix1, premix1, rstd1, states, proj1, fgate, hgrn_lb, S, EMIX)
    nq, nr = 3 * qt, (ECA + EB) // TNP
    tkh = _tile(EMIX, 1024) if (ECA + EB) % _tile(EMIX, 1024) == 0 else TNP
    kq = EMIX // tkh
    dh1 = _matmul(
        "dh1_qfi", dqfi, whin, grid=(S // TM, D // TDW, 3),
        a_spec=pl.BlockSpec((None, TM, EMIX), lambda i, j, k: (k, i, 0)),
        b_spec=pl.BlockSpec((TDW, EMIX), lambda i, j, k: (j, k)),
        out_shape=sds((S, D), F32), out_spec=pl.BlockSpec((TM, TDW), IJ), acc_shape=(TM, TDW), dims=NT)
    dh1 = _matmul(
        "dh1_rest", drest1, whin, grid=(S // TM, D // TDW, (ECA + EB) // tkh), a_spec=pl.BlockSpec((TM, tkh), IK),
        b_spec=pl.BlockSpec((TDW, tkh), lambda i, j, k: (j, k + 3 * kq)),
        out_shape=sds((S, D), F32), out_spec=pl.BlockSpec((TM, TDW), IJ), acc_shape=(TM, TDW), dims=NT,
        add=dh1, add_spec=pl.BlockSpec((TM, TDW), IJ))
    gw_hin = _matmul(
        "dwhin_qfi", h1, dqfi, grid=(D // TDW, nq, 1), a_spec=pl.BlockSpec((S, TDW), lambda i, j, k: (0, i)),
        b_spec=pl.BlockSpec((None, S, TNP), lambda i, j, k: (j // qt, 0, j % qt)),
        out_shape=sds((N_CHIPS, D, SH1), BF16), out_spec=pl.BlockSpec((None, TDW, TNP), lambda i, j, k: (j // c1, i, j % c1)),
        acc_shape=(TDW, TNP), dims=TN)
    gw_hin = _matmul(
        "dwhin_rest", h1, drest1, grid=(D // TDW, nr, 1), a_spec=pl.BlockSpec((S, TDW), lambda i, j, k: (0, i)),
        b_spec=pl.BlockSpec((S, TNP), lambda i, j, k: (0, j)), out_shape=sds((N_CHIPS, D, SH1), BF16),
        out_spec=pl.BlockSpec((None, TDW, TNP), lambda i, j, k: ((j + nq) // c1, i, (j + nq) % c1)),
        acc_shape=(TDW, TNP), dims=TN, alias=gw_hin)
    dmem, gw_kv1 = kv_bwd(1, dkv1, wkv1, None)

    core_chip = jnp.stack([lax.axis_index("c"), chip]).astype(jnp.int32)

    def reduce_in_chip(tag, stacks):
        got = _exchange_halves(f"exchange_halves{tag}", stacks)
        pairs = [_add_halves(f"add_halves{tag}_{t}", core_chip, g.reshape(N_CHIPS, 2, g.shape[1] // 2, g.shape[2]), r)
                 for t, (g, r) in enumerate(zip(stacks, got))]
        return [p for p, _ in pairs], [own for _, own in pairs]

    parts1, landed1 = reduce_in_chip(1, [gw_kv1, gw_out1, gw_hin])
    scatter1, token1 = _split_start("scatter1_start", parts1, landed1, _scatter_plan, 3 * len(parts1), None)
    dx1, dx1b, d_ng1 = _rms_bwd("rms_bwd1", dh1, x1, norm_g[1:2], dx2, token1)

    dpremix0, dproj0, dcol0, gw_out0 = out_proj_bwd(0, dx1b, branch0, wout0, premix0, proj0, gblk0, colscale0,
                                                    (S, NP0), gblk0)
    dproj0, dkv0 = _ca_bwd("ca_bwd0", dpremix0, proj0, EMIX // ECA, kv[0], dproj0, EMIX // ECA, S, ECA, EMIX)
    dmem, gw_kv0 = kv_bwd(0, dkv0, wkv0, dmem)
    parts_a, landed_a = reduce_in_chip("0a", [gw_kv0, gw_out0])
    scatter_a, token_a = _split_start("scatter0a_start", parts_a, landed_a, _scatter_plan, 3 * len(parts_a), None)
    dpooled = _matmul(
        "dpooled", dpremix0, wgrp, grid=(S // TM, N_POOL_GROUPS, 1), a_spec=pl.BlockSpec((TM, PG), IJ),
        b_spec=pl.BlockSpec((None, PG, PG), lambda i, j, k: (j, 0, 0)),
        out_shape=sds((S, EMIX), F32), out_spec=pl.BlockSpec((TM, PG), IJ), acc_shape=(TM, PG), dims=NT, after=token_a)
    def rows_by_chip(r, _, outs):
        outs[0][...] = r.reshape(N_CHIPS, PG // N_CHIPS, PG).astype(BF16)

    gw_grp, = _matmul(
        "dwgrp", pooled, dpremix0, grid=(N_POOL_GROUPS, 1, 1), a_spec=pl.BlockSpec((S, PG), lambda i, j, k: (0, i)),
        b_spec=pl.BlockSpec((S, PG), lambda i, j, k: (0, i)), epilogue=rows_by_chip,
        out_shape=[sds((N_CHIPS, N_POOL_GROUPS, PG // N_CHIPS, PG), BF16)],
        out_spec=[pl.BlockSpec((N_CHIPS, None, PG // N_CHIPS, PG), lambda i, j, k: (0, i, 0, 0))],
        acc_shape=(PG, PG), dims=TN)
    dproj0 = _pool_bwd(dpooled, dproj0, S, EMIX)
    gw_pin = _matmul(
        "dwpin", h0, dproj0, grid=(D // TDW, NP0 // TNP, 1), a_spec=pl.BlockSpec((S, TDW), lambda i, j, k: (0, i)),
        b_spec=pl.BlockSpec((S, TNP), lambda i, j, k: (0, j)), out_shape=sds((N_CHIPS, D, SH0), BF16),
        out_spec=pl.BlockSpec((None, TDW, TNP), lambda i, j, k: (j // c0, i, j % c0)), acc_shape=(TDW, TNP), dims=TN)
    parts_b, landed_b = reduce_in_chip("0b", [gw_pin, gw_grp.reshape(N_CHIPS, PG, PG)])
    scatter_b, token_b = _split_start("scatter0b_start", parts_b, landed_b, _scatter_plan, 3 * len(parts_b), None)
    dh0 = _matmul(
        "dh0", dproj0, wpin, grid=(S // TM, D // TDW, N_CHIPS), a_spec=pl.BlockSpec((TM, SH0), IK),
        b_spec=pl.BlockSpec((None, TDW, SH0), lambda i, j, k: (k, j, 0)),
        out_shape=sds((S, D), F32), out_spec=pl.BlockSpec((TM, TDW), IJ), acc_shape=(TM, TDW), dims=NT, after=token_b)
    grad_x, _, d_ng0 = _rms_bwd("rms_bwd0", dh0, xs, norm_g[0:1], dx1)
    _, _, d_mng = _rms_bwd("rms_bwd_mem", dmem, ms, mem_norm_g.reshape(1, D), jnp.zeros_like(ms))

    _, landed1 = _split_wait("scatter1_wait", scatter1, _scatter_plan, len(parts1), grad_x)
    _, landed_a = _split_wait("scatter0a_wait", scatter_a, _scatter_plan, len(parts_a), grad_x)
    _, landed_b = _split_wait("scatter0b_wait", scatter_b, _scatter_plan, len(parts_b), grad_x)
    landed = [landed_a[0], landed1[0], landed_a[1], landed1[1], landed_b[0], landed_b[1], landed1[2]]
    fulls = _sum_share(landed)
    f2 = [f.reshape(-1, f.shape[-1]) for f in fulls]
    grads, deltas, new_m, new_v = {}, {}, {}, {}
    for n, w, mm, vv, gs in (("w_kv", w_kv, m_w_kv, v_w_kv, f2[0:2]), ("w_out", w_out, m_w_out, v_w_out, f2[2:4]),
                             ("pool_w_in", pool_w_in, m_pool_w_in, v_pool_w_in, f2[4:5]),
                             ("pool_w_grp", pool_w_grp, m_pool_w_grp, v_pool_w_grp, f2[5:6]),
                             ("hgrn_w_in", hgrn_w_in, m_hgrn_w_in, v_hgrn_w_in, f2[6:7])):
        as3d = lambda a: a.reshape((a.shape[0], -1, a.shape[-1]))
        outs = _adamw_layers(f"adamw_{n}", as3d(w), gs, as3d(mm), as3d(vv))
        grads[n], deltas[n], new_m[n], new_v[n] = [o.reshape(w.shape) for o in outs]

    Wd = EMIX
    nshard = EMIX // N_CHIPS
    summed_rows = [(v, True, v.shape[1]) for v in (d_ng0, d_ng1, d_mng)] + [
        (dcol0, True, EMIX), (dlb, True, EMIX), (dcol1, True, EMIX), (d_final_g, True, D), (loss_part, True, 128)]
    partial = _pack_rows("pack_partials", summed_rows, Wd)
    g_pack, loss = _small_sum(_allgather_small("allgather_grads", partial), hgrn_lb, chip1, nshard)

    def pack_small(name, ng, mng, ps, lb_, hn, fg):
        return _pack_rows(name, [(ng, False, D), (mng.reshape(1, D), False, D), (ps, False, EMIX), (lb_, False, EMIX),
                                 (hn, False, nshard), (fg.reshape(1, D), False, D)], Wd)

    d_pack, m_pack, v_pack = _adamw(
        "adamw_small", pack_small("pack_small_w", norm_g, mem_norm_g, pool_scale, hgrn_lb, hgrn_norm_g, final_g), g_pack,
        pack_small("pack_small_m", m_norm_g, m_mem_norm_g, m_pool_scale, m_hgrn_lb, m_hgrn_norm_g, m_final_g),
        pack_small("pack_small_v", v_norm_g, v_mem_norm_g, v_pool_scale, v_hgrn_lb, v_hgrn_norm_g, v_final_g))
    layout = [(2, D), (1, D), (1, EMIX), (2, EMIX), (1, nshard), (1, D)]
    for tag, pack, out in (("g", g_pack, grads), ("d", d_pack, deltas), ("m", m_pack, new_m), ("v", v_pack, new_v)):
        ng, mng, ps, lb_, hn, fg = _unpack_rows(f"unpack_small_{tag}", pack, layout)
        out.update(norm_g=ng, mem_norm_g=mng.reshape(D), pool_scale=ps, hgrn_lb=lb_, hgrn_norm_g=hn, final_g=fg.reshape(D))
    loss = loss[0, 0]

    order = ["norm_g", "mem_norm_g", "w_kv", "w_out", "pool_w_in", "pool_w_grp", "pool_scale", "hgrn_w_in", "hgrn_lb",
             "hgrn_norm_g", "final_g"]
    return (loss, grad_x.reshape(1, S, D), *[grads[n] for n in order], *[deltas[n] for n in order],
            *[new_m[n] for n in order], *[new_v[n] for n in order])
```

```python
import functools

import jax
import jax.numpy as jnp
from jax import lax
from jax.experimental import pallas as pl
from jax.experimental.pallas import tpu as pltpu

F32 = jnp.float32
BF16 = jnp.bfloat16
MESH = pl.DeviceIdType.MESH
ANY = pl.BlockSpec(memory_space=pl.ANY)

EPS = 1e-6
HG_HEAD_DIM = 128
HG_CHUNK = 64
CA_HEADS = 4
N_POOL_GROUPS = 4
POOL_HALO = 128
ADAM_LR = 0.001
ADAM_B1 = 0.9
ADAM_B2 = 0.999
ADAM_EPS = 1e-08
ADAM_WD = 0.01
ADAM_STEP = 10
N_CHIPS = 4
N_DEV = 8
VMEM_LIMIT_BYTES = 56 * 1024 * 1024
SMALL_ROWS = 8
STREAM_CHUNK_BYTES = 4 * 1024 * 1024
STREAM_SLOTS = 3
SUM_SLOTS = 2


def _params(*sem):
    return pltpu.CompilerParams(dimension_semantics=sem, vmem_limit_bytes=VMEM_LIMIT_BYTES)


def _tile(n, pref):
    t = pref
    while n % t:
        t //= 2
    return t


def _sigmoid(x):
    return 1.0 / (1.0 + jnp.exp(-x))


def _matmul(name, a, b, *, grid, a_spec, b_spec, out_shape, out_spec, acc_shape, dims,
            add=None, add_spec=None, alias=None, after=None, extras=(), epilogue=None):
    nk = grid[2]
    has_add = add is not None
    has_alias = alias is not None
    has_after = after is not None
    n_out = len(out_shape) if epilogue is not None else 1

    def body(*refs):
        a_ref, b_ref = refs[0], refs[1]
        pos = 2
        add_ref = None
        if has_add:
            add_ref = refs[pos]
            pos += 1
        extra_refs = refs[pos:pos + len(extras)]
        pos += len(extras) + has_alias + has_after
        o_refs = refs[pos:pos + n_out]
        prod = lax.dot_general(a_ref[...], b_ref[...], (dims, ((), ())), preferred_element_type=F32)

        def finish(r):
            if epilogue is not None:
                epilogue(r, extra_refs, o_refs)
                return
            if has_add:
                r = r + add_ref[...].astype(F32)
            o_refs[0][...] = r.astype(o_refs[0].dtype)

        if nk == 1:
            finish(prod)
            return
        acc_ref = refs[pos + n_out]
        k = pl.program_id(2)

        @pl.when(k == 0)
        def _():
            acc_ref[...] = prod

        @pl.when(k > 0)
        def _():
            acc_ref[...] += prod

        @pl.when(k == nk - 1)
        def _():
            finish(acc_ref[...])

    operands = [a, b]
    in_specs = [a_spec, b_spec]
    if has_add:
        operands.append(add)
        in_specs.append(add_spec)
    for arr, spec in extras:
        operands.append(arr)
        in_specs.append(spec)
    aliases = {}
    if has_alias:
        aliases = {len(operands): 0}
        operands.append(alias)
        in_specs.append(ANY)
    if has_after:
        operands.append(after)
        in_specs.append(ANY)
    return pl.pallas_call(
        body, name=name, grid=grid, in_specs=in_specs, out_specs=out_spec, out_shape=out_shape,
        scratch_shapes=[pltpu.VMEM(acc_shape, F32)] if nk > 1 else [], input_output_aliases=aliases,
        compiler_params=_params("parallel", "parallel", "arbitrary"),
    )(*operands)


IJ = lambda i, j, k: (i, j)
IK = lambda i, j, k: (i, k)
KJ = lambda i, j, k: (k, j)
KI = lambda i, j, k: (k, i)
NN = ((1,), (0,))
NT = ((1,), (1,))
TN = ((0,), (0,))


def _rms_fwd(name, x, g, after=None):
    R, D = x.shape
    tr = _tile(R, 256)
    extra = [] if after is None else [after]

    def body(x_ref, g_ref, *rest):
        xf = x_ref[...]
        r = lax.rsqrt(jnp.mean(xf * xf, axis=-1, keepdims=True) + EPS)
        rest[-1][...] = (xf * r * g_ref[...]).astype(BF16)

    return pl.pallas_call(
        body, name=name, grid=(R // tr,),
        in_specs=[pl.BlockSpec((tr, D), lambda i: (i, 0)), pl.BlockSpec((1, D), lambda i: (0, 0))] + [ANY] * len(extra),
        out_specs=pl.BlockSpec((tr, D), lambda i: (i, 0)),
        out_shape=jax.ShapeDtypeStruct((R, D), BF16), compiler_params=_params("parallel"),
    )(x, g, *extra)


def _rms_bwd(name, dh, x, g, dres, after=None):
    R, D = x.shape
    tr = _tile(R, 256)
    extra = [] if after is None else [after]

    def body(dh_ref, x_ref, g_ref, dres_ref, *rest):
        dx_ref, dxb_ref, dg_ref = rest[len(extra):]
        xf = x_ref[...]
        r = lax.rsqrt(jnp.mean(xf * xf, axis=-1, keepdims=True) + EPS)
        xn = xf * r
        d = dh_ref[...]
        dyg = d * g_ref[...]
        dx = r * (dyg - xn * jnp.mean(dyg * xn, axis=-1, keepdims=True)) + dres_ref[...]
        dx_ref[...] = dx
        dxb_ref[...] = dx.astype(BF16)

        @pl.when(pl.program_id(0) == 0)
        def _():
            dg_ref[...] = jnp.zeros_like(dg_ref)

        dg_ref[...] += jnp.sum(d * xn, axis=0, keepdims=True)

    row = pl.BlockSpec((tr, D), lambda i: (i, 0))
    vec = pl.BlockSpec((1, D), lambda i: (0, 0))
    return pl.pallas_call(
        body, name=name, grid=(R // tr,), in_specs=[row, row, vec, row] + [ANY] * len(extra), out_specs=[row, row, vec],
        out_shape=[jax.ShapeDtypeStruct((R, D), F32), jax.ShapeDtypeStruct((R, D), BF16),
                   jax.ShapeDtypeStruct((1, D), F32)],
        compiler_params=_params("arbitrary"),
    )(dh, x, g, dres, *extra)


def _loss_head(x2, g, target):
    R, D = x2.shape
    tr = _tile(R, 256)

    def body(x_ref, g_ref, t_ref, dx_ref, dxb_ref, dg_ref, loss_ref):
        xf = x_ref[...]
        gg = g_ref[...]
        r = lax.rsqrt(jnp.mean(xf * xf, axis=-1, keepdims=True) + EPS)
        xn = xf * r
        e = xn * gg - t_ref[...]
        part = 0.5 * jnp.sum(jnp.mean(e * e, axis=-1, keepdims=True), axis=0, keepdims=True)
        dy = e * (1.0 / D)
        dyg = dy * gg
        dx = r * (dyg - xn * jnp.mean(dyg * xn, axis=-1, keepdims=True))
        dx_ref[...] = dx
        dxb_ref[...] = dx.astype(BF16)

        @pl.when(pl.program_id(0) == 0)
        def _():
            dg_ref[...] = jnp.zeros_like(dg_ref)
            loss_ref[...] = jnp.zeros_like(loss_ref)

        dg_ref[...] += jnp.sum(dy * xn, axis=0, keepdims=True)
        loss_ref[...] += jnp.broadcast_to(part, loss_ref.shape)

    row = pl.BlockSpec((tr, D), lambda i: (i, 0))
    vec = pl.BlockSpec((1, D), lambda i: (0, 0))
    return pl.pallas_call(
        body, name="loss_head", grid=(R // tr,), in_specs=[row, vec, row],
        out_specs=[row, row, vec, pl.BlockSpec((1, 128), lambda i: (0, 0))],
        out_shape=[jax.ShapeDtypeStruct((R, D), F32), jax.ShapeDtypeStruct((R, D), BF16),
                   jax.ShapeDtypeStruct((1, D), F32), jax.ShapeDtypeStruct((1, 128), F32)],
        compiler_params=_params("arbitrary"),
    )(x2, g, target)


def _pool_band(tr, reverse, w):
    r = lax.broadcasted_iota(jnp.int32, (tr, tr + POOL_HALO), 0)
    c = lax.broadcasted_iota(jnp.int32, (tr, tr + POOL_HALO), 1)
    if reverse:
        inside = (c >= r) & (c < r + w)
    else:
        cc = c - POOL_HALO
        inside = (cc <= r) & (cc > r - w)
    return jnp.where(inside, 1.0, 0.0).astype(BF16)


def _pool_fwd(proj, S, EMIX):
    PG = EMIX // N_POOL_GROUPS
    cb = PG
    tr = _tile(S, 512)
    per_group = PG // cb

    def body(u_ref, o_ref, ext):
        i = pl.program_id(1)
        w = jnp.left_shift(2, pl.program_id(0) // per_group)

        @pl.when(i == 0)
        def _():
            ext[0:POOL_HALO, :] = jnp.zeros((POOL_HALO, cb), BF16)

        u = u_ref[...]
        ext[POOL_HALO:, :] = u
        win = jnp.dot(_pool_band(tr, False, w), ext[...], preferred_element_type=F32)
        pos = i * tr + lax.broadcasted_iota(jnp.int32, (tr, 1), 0)
        cnt = jnp.minimum(pos + 1, w).astype(F32)
        o_ref[...] = (win / cnt - u.astype(F32)).astype(BF16)
        ext[0:POOL_HALO, :] = u[tr - POOL_HALO:, :]

    return pl.pallas_call(
        body, name="pool_fwd", grid=(EMIX // cb, S // tr),
        in_specs=[pl.BlockSpec((tr, cb), lambda j, i: (i, j))],
        out_specs=pl.BlockSpec((tr, cb), lambda j, i: (i, j)),
        out_shape=jax.ShapeDtypeStruct((S, EMIX), BF16),
        scratch_shapes=[pltpu.VMEM((tr + POOL_HALO, cb), BF16)],
        compiler_params=_params("parallel", "arbitrary"),
    )(proj)


def _pool_bwd(dpooled, dproj, S, EMIX):
    PG = EMIX // N_POOL_GROUPS
    cb = PG
    tr = _tile(S, 512)
    per_group = PG // cb
    nrt = S // tr

    def body(d_ref, _, o_ref, ext):
        step = pl.program_id(1)
        i = nrt - 1 - step
        w = jnp.left_shift(2, pl.program_id(0) // per_group)

        @pl.when(step == 0)
        def _():
            ext[tr:, :] = jnp.zeros((POOL_HALO, cb), BF16)

        d = d_ref[...]
        pos = i * tr + lax.broadcasted_iota(jnp.int32, (tr, 1), 0)
        cnt = jnp.minimum(pos + 1, w).astype(F32)
        z = (d / cnt).astype(BF16)
        ext[0:tr, :] = z
        win = jnp.dot(_pool_band(tr, True, w), ext[...], preferred_element_type=F32)
        o_ref[...] = (win - d).astype(BF16)
        ext[tr:, :] = z[0:POOL_HALO, :]

    return pl.pallas_call(
        body, name="pool_bwd", grid=(EMIX // cb, nrt),
        in_specs=[pl.BlockSpec((tr, cb), lambda j, s: (nrt - 1 - s, j)), ANY],
        out_specs=pl.BlockSpec((tr, cb), lambda j, s: (nrt - 1 - s, j)),
        out_shape=jax.ShapeDtypeStruct(dproj.shape, dproj.dtype),
        scratch_shapes=[pltpu.VMEM((tr + POOL_HALO, cb), BF16)],
        input_output_aliases={1: 0},
        compiler_params=_params("parallel", "arbitrary"),
    )(dpooled, dproj)


def _ca_fwd(name, proj, qblk, kv, premix, S, ECA, EMIX):
    M = kv.shape[0]
    hd = ECA // CA_HEADS
    ts = _tile(S, 512)
    scale = hd ** -0.5

    def body(q_ref, kv_ref, _, o_ref):
        for h in range(CA_HEADS):
            q = q_ref[:, h * hd:(h + 1) * hd]
            k = kv_ref[:, h * hd:(h + 1) * hd]
            v = kv_ref[:, ECA + h * hd:ECA + (h + 1) * hd]
            s = lax.dot_general(q, k, (NT, ((), ())), preferred_element_type=F32) * scale
            s = s - jnp.max(s, axis=-1, keepdims=True)
            p = jnp.exp(s)
            p = p / jnp.sum(p, axis=-1, keepdims=True)
            o = jnp.dot(p.astype(BF16), v, preferred_element_type=F32)
            o_ref[:, h * hd:(h + 1) * hd] = o.astype(BF16)

    return pl.pallas_call(
        body, name=name, grid=(S // ts,),
        in_specs=[pl.BlockSpec((ts, ECA), lambda i: (i, qblk)), pl.BlockSpec((M, 2 * ECA), lambda i: (0, 0)), ANY],
        out_specs=pl.BlockSpec((ts, ECA), lambda i: (i, EMIX // ECA)),
        out_shape=jax.ShapeDtypeStruct(premix.shape, premix.dtype),
        input_output_aliases={2: 0}, compiler_params=_params("parallel"),
    )(proj, kv, premix)


def _ca_bwd(name, dpremix, proj, qblk, kv, dbuf, dblk, S, ECA, EMIX):
    M = kv.shape[0]
    hd = ECA // CA_HEADS
    ts = _tile(S, 512)
    scale = hd ** -0.5

    def body(do_ref, q_ref, kv_ref, _, dq_ref, dkv_ref):
        @pl.when(pl.program_id(0) == 0)
        def _():
            dkv_ref[...] = jnp.zeros_like(dkv_ref)

        for h in range(CA_HEADS):
            lo, hi = h * hd, (h + 1) * hd
            q = q_ref[:, lo:hi]
            k = kv_ref[:, lo:hi]
            v = kv_ref[:, ECA + lo:ECA + hi]
            do = do_ref[:, lo:hi]
            s = lax.dot_general(q, k, (NT, ((), ())), preferred_element_type=F32) * scale
            s = s - jnp.max(s, axis=-1, keepdims=True)
            p = jnp.exp(s)
            p = p / jnp.sum(p, axis=-1, keepdims=True)
            pb = p.astype(BF16)
            dkv_ref[:, ECA + lo:ECA + hi] += lax.dot_general(pb, do, (TN, ((), ())), preferred_element_type=F32)
            dp = lax.dot_general(do, v, (NT, ((), ())), preferred_element_type=F32)
            ds = (p * (dp - jnp.sum(p * dp, axis=-1, keepdims=True)) * scale).astype(BF16)
            dq_ref[:, lo:hi] = jnp.dot(ds, k, preferred_element_type=F32).astype(BF16)
            dkv_ref[:, lo:hi] += lax.dot_general(ds, q, (TN, ((), ())), preferred_element_type=F32)

    return pl.pallas_call(
        body, name=name, grid=(S // ts,),
        in_specs=[pl.BlockSpec((ts, ECA), lambda i: (i, EMIX // ECA)), pl.BlockSpec((ts, ECA), lambda i: (i, qblk)),
                  pl.BlockSpec((M, 2 * ECA), lambda i: (0, 0)), ANY],
        out_specs=[pl.BlockSpec((ts, ECA), lambda i: (i, dblk)), pl.BlockSpec((M, 2 * ECA), lambda i: (0, 0))],
        out_shape=[jax.ShapeDtypeStruct(dbuf.shape, dbuf.dtype), jax.ShapeDtypeStruct((M, 2 * ECA), F32)],
        input_output_aliases={3: 0}, compiler_params=_params("arbitrary"),
    )(dpremix, proj, kv, dbuf)


def _gate_fwd(name, premix, proj, gblk, colscale, S, EB, ECA):
    ts = _tile(S, 512)

    def body(p_ref, g_ref, c_ref, o_ref):
        g = g_ref[...].astype(F32)
        o_ref[...] = (p_ref[...].astype(F32) * c_ref[...] * (g * _sigmoid(g))).astype(BF16)

    return pl.pallas_call(
        body, name=name, grid=(S // ts, EB // ECA),
        in_specs=[pl.BlockSpec((ts, ECA), lambda i, j: (i, j)), pl.BlockSpec((ts, ECA), lambda i, j: (i, gblk + j)),
                  pl.BlockSpec((1, ECA), lambda i, j: (0, j))],
        out_specs=pl.BlockSpec((ts, ECA), lambda i, j: (i, j)),
        out_shape=jax.ShapeDtypeStruct((S, EB), BF16), compiler_params=_params("parallel", "parallel"),
    )(premix, proj, colscale)


def _gate_bwd_epilogue(db, extra_refs, out_refs):
    p_ref, g_ref, c_ref = extra_refs
    dp_ref, dg_ref, dc_ref = out_refs
    g = g_ref[...].astype(F32)
    sg = _sigmoid(g)
    si = g * sg
    c = c_ref[...]
    t = db * p_ref[...].astype(F32)
    dp_ref[...] = (db * si * c).astype(BF16)
    dg_ref[...] = (t * c * (sg * (1.0 + g * (1.0 - sg)))).astype(BF16)
    dc_ref[...] = jnp.sum(t * si, axis=0, keepdims=True)


def _hgrn_lb(lb_ref):
    l0 = lb_ref[0:1, :]
    l1 = lb_ref[1:2, :]
    mx = jnp.maximum(l0, l1)
    e0 = jnp.exp(l0 - mx)
    e1 = jnp.exp(l1 - mx)
    return e1 / (e0 + e1)


def _bdot(a, b, ca, cb):
    return lax.dot_general(a, b, (((ca,), (cb,)), ((0,), (0,))), preferred_element_type=F32)


def _tri_sum(tri, x):
    hi = x.astype(BF16)
    lo = (x - hi.astype(F32)).astype(BF16)
    tri = tri.astype(BF16)
    return _bdot(tri, hi, 2, 1) + _bdot(tri, lo, 2, 1)


def _hgrn_chunks(qin, fin, lbh, n):
    C = HG_CHUNK
    row = lax.broadcasted_iota(jnp.int32, (n, C, C), 1)
    col = lax.broadcasted_iota(jnp.int32, (n, C, C), 2)
    causal = row >= col
    sg = _sigmoid(fin)
    f = lbh + (1.0 - lbh) * sg
    k = 1.0 - f
    g = jnp.log(f)
    b = _tri_sum(jnp.where(causal, 1.0, 0.0), g)
    b_last = jnp.sum(g, axis=1, keepdims=True)
    eb = jnp.exp(b)
    einv = jnp.exp(-b)
    eend = jnp.exp(b_last - b)
    sq = _sigmoid(qin)
    a = qin * sq * (HG_HEAD_DIM ** -0.5) * eb
    bm = k * einv
    e = k * eend
    d = jnp.exp(b_last)
    p = jnp.where(causal, _bdot(a.astype(BF16), bm.astype(BF16), 2, 2), 0.0)
    return dict(causal=causal, sg=sg, f=f, eb=eb, einv=einv, eend=eend, sq=sq, a=a, bm=bm, e=e, d=d, p=p)


def _hgrn_fwd(proj, fgate, hgrn_lb, S, EMIX, EB):
    HD, C = HG_HEAD_DIM, HG_CHUNK
    HH = EMIX // HD
    hb = 6 if HH % 6 == 0 else 1
    W = hb * HD
    tr = _tile(S, 512)
    n = tr // C

    def body(q_ref, f_ref, i_ref, lb_ref, o_ref, rstd_ref, st_ref, state):
        @pl.when(pl.program_id(1) == 0)
        def _():
            state[...] = jnp.zeros_like(state)

        lb = _hgrn_lb(lb_ref)
        for h in range(hb):
            cs = slice(h * HD, (h + 1) * HD)
            qin = q_ref[:, cs].astype(F32).reshape(n, C, HD)
            fin = f_ref[:, cs].reshape(n, C, HD)
            v = i_ref[:, cs].reshape(n, C, HD)
            t = _hgrn_chunks(qin, fin, lb[:, cs], n)
            upd = _bdot(v, t["e"].astype(BF16), 1, 1)
            st = state[h]
            for c in range(n):
                st_ref[h, c] = st
                st = st * t["d"][c] + upd[c]
            state[h] = st
            o = _bdot(t["p"].astype(BF16), v, 2, 1) + _bdot(t["a"].astype(BF16), st_ref[h].astype(BF16), 2, 2)
            rstd = lax.rsqrt(jnp.mean(o * o, axis=-1, keepdims=True) + EPS)
            o_ref[:, cs] = (o * rstd).reshape(tr, HD).astype(BF16)
            rstd_ref[:, cs] = jnp.broadcast_to(rstd, (n, C, HD)).reshape(tr, HD)

    blk = lambda off: pl.BlockSpec((tr, W), lambda g, i: (i, off + g))
    return pl.pallas_call(
        body, name="hgrn_fwd", grid=(HH // hb, S // tr),
        in_specs=[blk(0), blk(0), blk(2 * EMIX // W), pl.BlockSpec((2, W), lambda g, i: (0, g))],
        out_specs=[blk(0), blk(0), pl.BlockSpec((hb, n, HD, HD), lambda g, i: (g, i, 0, 0))],
        out_shape=[jax.ShapeDtypeStruct((S, EB), BF16), jax.ShapeDtypeStruct((S, EMIX), F32),
                   jax.ShapeDtypeStruct((HH, S // C, HD, HD), F32)],
        scratch_shapes=[pltpu.VMEM((hb, HD, HD), F32)],
        compiler_params=_params("parallel", "arbitrary"),
    )(proj, fgate, proj, hgrn_lb)


def _hgrn_bwd(dpremix, premix, rstd, states, proj, fgate, hgrn_lb, S, EMIX):
    HD, C = HG_HEAD_DIM, HG_CHUNK
    HH = EMIX // HD
    hb = 6 if HH % 6 == 0 else 1
    W = hb * HD
    tr = _tile(S, 512)
    n = tr // C
    nrt = S // tr

    def body(do_ref, on_ref, rstd_ref, st_ref, q_ref, f_ref, i_ref, lb_ref, d_ref, dlb_ref, dstate, dsbuf):
        @pl.when(pl.program_id(1) == 0)
        def _():
            dstate[...] = jnp.zeros_like(dstate)
            dlb_ref[...] = jnp.zeros_like(dlb_ref)

        lb = _hgrn_lb(lb_ref)
        for h in range(hb):
            cs = slice(h * HD, (h + 1) * HD)
            qin = q_ref[:, cs].astype(F32).reshape(n, C, HD)
            fin = f_ref[:, cs].reshape(n, C, HD)
            v = i_ref[:, cs].reshape(n, C, HD)
            lbh = lb[:, cs]
            t = _hgrn_chunks(qin, fin, lbh, n)
            a, bm, e, d, p = t["a"], t["bm"], t["e"], t["d"], t["p"]
            ab, bmb, eb16 = a.astype(BF16), bm.astype(BF16), e.astype(BF16)
            on = on_ref[:, cs].astype(F32).reshape(n, C, HD)
            dn = do_ref[:, cs].astype(F32).reshape(n, C, HD)
            do = rstd_ref[:, cs].reshape(n, C, HD) * (dn - on * jnp.mean(dn * on, axis=-1, keepdims=True))
            dob = do.astype(BF16)
            grow = _bdot(dob, ab, 1, 1)
            ds = dstate[h]
            for c in reversed(range(n)):
                dsbuf[h, c] = ds
                ds = ds * d[c] + grow[c]
            dstate[h] = ds
            dst = dsbuf[h]
            st = st_ref[h]
            dstb = dst.astype(BF16)
            dp = jnp.where(t["causal"], _bdot(dob, v, 2, 2), 0.0).astype(BF16)
            dv = _bdot(p.astype(BF16), dob, 1, 1) + _bdot(eb16, dstb, 2, 2)
            da = _bdot(dp, bmb, 2, 1) + _bdot(dob, st.astype(BF16), 2, 1)
            dbm = _bdot(dp, ab, 1, 1)
            de = _bdot(v, dstb, 2, 1)
            dd = jnp.sum(dst * st, axis=1, keepdims=True)
            dk = dbm * t["einv"] + de * t["eend"]
            dee = de * e
            db = da * a - dbm * bm - dee
            extra = jnp.sum(dee, axis=1, keepdims=True) + dd * d
            upper = jnp.where(lax.broadcasted_iota(jnp.int32, (n, C, C), 2)
                              >= lax.broadcasted_iota(jnp.int32, (n, C, C), 1), 1.0, 0.0)
            dg = _tri_sum(upper, db) + extra
            df = dg / t["f"] - dk
            sg, sq = t["sg"], t["sq"]
            dq = da * t["eb"] * (HD ** -0.5) * (sq * (1.0 + qin * (1.0 - sq)))
            d_ref[0, :, cs] = dq.reshape(tr, HD).astype(BF16)
            d_ref[1, :, cs] = (df * (1.0 - lbh) * sg * (1.0 - sg)).reshape(tr, HD).astype(BF16)
            d_ref[2, :, cs] = dv.reshape(tr, HD).astype(BF16)
            dlb_ref[:, cs] += jnp.sum((df * (1.0 - sg)).reshape(tr, HD), axis=0, keepdims=True)

    rev = lambda off: pl.BlockSpec((tr, W), lambda g, s: (nrt - 1 - s, off + g))
    return pl.pallas_call(
        body, name="hgrn_bwd", grid=(HH // hb, nrt),
        in_specs=[rev(0), rev(0), rev(0), pl.BlockSpec((hb, n, HD, HD), lambda g, s: (g, nrt - 1 - s, 0, 0)),
                  rev(0), rev(0), rev(2 * EMIX // W), pl.BlockSpec((2, W), lambda g, s: (0, g))],
        out_specs=[pl.BlockSpec((3, tr, W), lambda g, s: (0, nrt - 1 - s, g)), pl.BlockSpec((1, W), lambda g, s: (0, g))],
        out_shape=[jax.ShapeDtypeStruct((3, S, EMIX), BF16), jax.ShapeDtypeStruct((1, EMIX), F32)],
        scratch_shapes=[pltpu.VMEM((hb, HD, HD), F32), pltpu.VMEM((hb, n, HD, HD), F32)],
        compiler_params=_params("parallel", "arbitrary"),
    )(dpremix, premix, rstd, states, proj, fgate, proj, hgrn_lb)


EW_BLOCK_ELEMS = 512 * 1024


def _ew_tiles(R, C):
    tc = C if C <= 4096 else _tile(C, 2048)
    tr = _tile(R, 512)
    while tr * tc > EW_BLOCK_ELEMS and tr % 16 == 0:
        tr //= 2
    return tr, tc


def _add_halves(name, core_chip, grad, got):
    _, _, R, C = grad.shape
    tr, tc = _ew_tiles(R, C)

    def body(c_ref, a_ref, b_ref, o_ref, own_ref):
        r = (a_ref[...].astype(F32) + b_ref[...].astype(F32)).astype(BF16)
        o_ref[...] = r

        @pl.when(pl.program_id(2) == c_ref[1])
        def _():
            own_ref[...] = r

    blk = pl.BlockSpec((None, tr, tc), lambda i, j, s, c: (s, i, j))
    sds = jax.ShapeDtypeStruct(got.shape, BF16)
    return pl.pallas_call(
        body, name=name, out_shape=[sds, sds],
        grid_spec=pltpu.PrefetchScalarGridSpec(
            num_scalar_prefetch=1, grid=(R // tr, C // tc, N_CHIPS),
            in_specs=[pl.BlockSpec((None, None, tr, tc), lambda i, j, s, c: (s, c[0], i, j)), blk],
            out_specs=[blk, pl.BlockSpec((None, tr, tc), lambda i, j, s, c: (c[1], i, j))]),
        compiler_params=_params("parallel", "parallel", "arbitrary"),
    )(core_chip, grad, got)


def _adam_step(w, g, m, v):
    mn = ADAM_B1 * m + (1.0 - ADAM_B1) * g
    vn = ADAM_B2 * v + (1.0 - ADAM_B2) * (g * g)
    m_hat = mn / (1.0 - ADAM_B1 ** ADAM_STEP)
    v_hat = vn / (1.0 - ADAM_B2 ** ADAM_STEP)
    return -ADAM_LR * (m_hat / (jnp.sqrt(v_hat) + ADAM_EPS) + ADAM_WD * w), mn, vn


def _adamw(name, w, g, m, v):
    R, C = w.shape
    tr, tc = _ew_tiles(R, C)

    def body(w_ref, g_ref, m_ref, v_ref, d_ref, mo_ref, vo_ref):
        d_ref[...], mo_ref[...], vo_ref[...] = _adam_step(w_ref[...], g_ref[...], m_ref[...], v_ref[...])

    blk = pl.BlockSpec((tr, tc), lambda i, j: (i, j))
    sds = jax.ShapeDtypeStruct((R, C), F32)
    return pl.pallas_call(
        body, name=name, grid=(R // tr, C // tc), in_specs=[blk] * 4, out_specs=[blk] * 3, out_shape=[sds] * 3,
        compiler_params=_params("parallel", "parallel"),
    )(w, g, m, v)


def _adamw_layers(name, w, gs, m, v):
    L, R, C = w.shape
    tr, tc = _ew_tiles(R, C)

    def body(*refs):
        w_ref, m_ref, v_ref = refs[:3]
        g_refs = refs[3:3 + L]
        go_ref, d_ref, mo_ref, vo_ref = refs[3 + L:]
        layer = pl.program_id(0)
        g = g_refs[0][...]
        for n in range(1, L):
            g = jnp.where(layer == n, g_refs[n][...], g)
        go_ref[...] = g
        d_ref[...], mo_ref[...], vo_ref[...] = _adam_step(w_ref[...], g, m_ref[...], v_ref[...])

    blk = pl.BlockSpec((None, tr, tc), lambda l, i, j: (l, i, j))
    of_layer = lambda n: pl.BlockSpec((tr, tc), lambda l, i, j: (jnp.where(l == n, i, 0), jnp.where(l == n, j, 0)))
    sds = jax.ShapeDtypeStruct((L, R, C), F32)
    return pl.pallas_call(
        body, name=name, grid=(L, R // tr, C // tc), in_specs=[blk] * 3 + [of_layer(n) for n in range(L)],
        out_specs=[blk] * 4, out_shape=[sds] * 4, compiler_params=_params("parallel", "parallel", "parallel"),
    )(w, m, v, *gs)


def _pack_rows(name, items, W):
    nv = len(items)
    first, row = [], 0
    for a, add, _ in items:
        first.append(row)
        row += 1 if add else a.shape[0]
    assert row <= SMALL_ROWS

    def body(*refs):
        o_ref = refs[nv]
        o_ref[...] = jnp.zeros_like(o_ref)
        for i, (a, add, width) in enumerate(items):
            val = refs[i][:, 0:width]
            if add:
                val = jnp.sum(val, axis=0, keepdims=True)
            o_ref[first[i]:first[i] + val.shape[0], 0:width] = val

    vm = pl.BlockSpec(memory_space=pltpu.VMEM)
    return pl.pallas_call(
        body, name=name, in_specs=[vm] * nv, out_specs=vm, out_shape=jax.ShapeDtypeStruct((SMALL_ROWS, W), F32),
    )(*[a for a, _, _ in items])


def _unpack_rows(name, pack, layout):
    def body(p_ref, *o_refs):
        row = 0
        for o_ref, (k, n) in zip(o_refs, layout):
            o_ref[...] = p_ref[row:row + k, 0:n]
            row += k

    vm = pl.BlockSpec(memory_space=pltpu.VMEM)
    return pl.pallas_call(
        body, name=name, in_specs=[vm], out_specs=[vm] * len(layout),
        out_shape=[jax.ShapeDtypeStruct(s, F32) for s in layout],
    )(pack)


def _small_sum(gathered, hgrn_lb, chip, nshard):
    _, T, W = gathered.shape

    def body(c_ref, g_ref, lb_ref, o_ref, loss_ref, tmp):
        acc = g_ref[0]
        for dev in range(1, N_DEV):
            acc = acc + g_ref[dev]
        tmp[...] = acc
        lb = _hgrn_lb(lb_ref)
        d1 = tmp[4:5, :] * (lb * (1.0 - lb))
        o_ref[...] = jnp.zeros_like(o_ref)
        o_ref[0:4, :] = tmp[0:4, :]
        o_ref[4:5, :] = -d1
        o_ref[5:6, :] = d1
        mine = tmp[5:6, 0:nshard]
        for b in range(1, N_CHIPS):
            mine = jnp.where(c_ref[0] == b, tmp[5:6, b * nshard:(b + 1) * nshard], mine)
        o_ref[6:7, 0:nshard] = mine
        o_ref[7:8, :] = tmp[6:7, :]
        loss_ref[...] = tmp[7:8, 0:128]

    vm = pl.BlockSpec(memory_space=pltpu.VMEM)
    return pl.pallas_call(
        body, name="small_sum", in_specs=[pl.BlockSpec(memory_space=pltpu.SMEM), vm, vm], out_specs=[vm, vm],
        out_shape=[jax.ShapeDtypeStruct((T, W), F32), jax.ShapeDtypeStruct((1, 128), F32)],
        scratch_shapes=[pltpu.VMEM((T, W), F32)],
    )(chip, gathered, hgrn_lb)


def _place():
    return lax.axis_index("x"), lax.axis_index("y"), lax.axis_index("c")


def _other_chips(x, y):
    return [(1 - x, y), (x, 1 - y), (1 - x, 1 - y)]


def _chunk_rows(rows, row_bytes):
    cr = rows
    while cr * row_bytes > STREAM_CHUNK_BYTES and cr % 32 == 0:
        cr //= 2
    return cr


def _stream(pairs, buf, sems, t, peer):
    lsem, ssem, rsem = sems
    n = len(pairs)
    loads, sent = [None] * n, [None] * n

    def load(k):
        slot = k % STREAM_SLOTS
        if k >= STREAM_SLOTS:
            sent[k - STREAM_SLOTS]()
        loads[k] = pltpu.make_async_copy(pairs[k][0], buf.at[slot], lsem.at[t, slot])
        loads[k].start()

    load(0)
    for k in range(n):
        slot = k % STREAM_SLOTS
        if k + 1 < n:
            load(k + 1)
        loads[k].wait()
        cp = pltpu.make_async_remote_copy(src_ref=buf.at[slot], dst_ref=pairs[k][1], send_sem=ssem.at[t, slot],
                                          recv_sem=rsem.at[t], device_id=peer, device_id_type=MESH)
        cp.start()
        sent[k] = cp.wait_send
    for k in range(max(0, n - STREAM_SLOTS), n):
        sent[k]()


def _stream_scratch(shapes):
    nt = len(shapes)
    return ([pltpu.VMEM((STREAM_SLOTS,) + s, d) for s, d in shapes]
            + [pltpu.SemaphoreType.DMA((nt, STREAM_SLOTS)), pltpu.SemaphoreType.DMA((nt, STREAM_SLOTS)),
               pltpu.SemaphoreType.DMA((nt,))])


def _exchange_halves(name, grads):
    nt = len(grads)
    hs = [g.shape[1] // 2 for g in grads]
    crs = [_chunk_rows(h, g.shape[2] * g.dtype.itemsize) for h, g in zip(hs, grads)]

    def body(*refs):
        ins, gots, bufs, sems = refs[:nt], refs[nt:2 * nt], refs[2 * nt:3 * nt], refs[3 * nt:]
        x, y, c = _place()
        sib = (x, y, 1 - c)
        for t in range(nt):
            h, cr = hs[t], crs[t]
            pairs = [(ins[t].at[b, pl.ds((1 - c) * h + r0, cr)], gots[t].at[b, pl.ds(r0, cr)])
                     for b in range(N_CHIPS) for r0 in range(0, h, cr)]
            _stream(pairs, bufs[t], sems, t, sib)
        for t in range(nt):
            pltpu.make_async_remote_copy(src_ref=gots[t], dst_ref=gots[t], send_sem=sems[1].at[t, 0],
                                         recv_sem=sems[2].at[t], device_id=sib, device_id_type=MESH).wait_recv()

    return pl.pallas_call(
        body, name=name, in_specs=[ANY] * nt, out_specs=[ANY] * nt,
        out_shape=[jax.ShapeDtypeStruct((N_CHIPS, h, g.shape[2]), g.dtype) for h, g in zip(hs, grads)],
        scratch_shapes=_stream_scratch([((cr, g.shape[2]), g.dtype) for cr, g in zip(crs, grads)]),
        compiler_params=pltpu.CompilerParams(vmem_limit_bytes=VMEM_LIMIT_BYTES),
    )(*grads)


def _scatter_plan(srcs, dsts):
    x, y, c = _place()
    me = 2 * x + y
    return [(srcs[t].at[2 * px + py], dsts[t].at[me], (px, py, c))
            for t in range(len(srcs)) for px, py in _other_chips(x, y)]


def _slot(dst, chip, r0, rows, cols):
    if len(dst.shape) == 3:
        return dst.at[chip, pl.ds(r0, rows)]
    return dst.at[pl.ds(r0, rows), pl.ds(pl.multiple_of(chip * cols, 128), cols)]


def _shard_dims(gathered):
    s = gathered.shape
    return (s[1], s[2]) if len(s) == 3 else (s[0], s[1] // N_CHIPS)


def _gather_plan(_, bufs):
    x, y, c = _place()
    me = 2 * x + y
    plan = []
    for buf in bufs:
        rows, cols = _shard_dims(buf)
        mine = _slot(buf, me, c * (rows // 2), rows // 2, cols)
        plan += [(mine, mine, (px, py, c)) for px, py in _other_chips(x, y)]
    return plan


def _cast_to_slot(name, chip, w, layer, gathered_shape, after):
    _, R, C = w.shape
    tr, tc = _ew_tiles(R, C)

    def body(c_ref, w_ref, *rest):
        rest[-1][...] = w_ref[...].astype(BF16)

    if len(gathered_shape) == 3:
        out_spec = pl.BlockSpec((None, tr, tc), lambda i, j, c: (c[0], i, j))
    else:
        out_spec = pl.BlockSpec((tr, tc), lambda i, j, c: (i, c[0] * (C // tc) + j))
    extra = [] if after is None else [after]
    return pl.pallas_call(
        body, name=name, out_shape=jax.ShapeDtypeStruct(gathered_shape, BF16),
        grid_spec=pltpu.PrefetchScalarGridSpec(
            num_scalar_prefetch=1, grid=(R // tr, C // tc),
            in_specs=[pl.BlockSpec((None, tr, tc), lambda i, j, c: (layer, i, j))] + [ANY] * len(extra),
            out_specs=out_spec),
        compiler_params=_params("parallel", "parallel"),
    )(chip, w, *extra)


HBM_SPEC = pl.BlockSpec(memory_space=pltpu.HBM)
SEM_SPEC = pl.BlockSpec(memory_space=pltpu.SEMAPHORE)


def _split_start(name, srcs, dsts, plan, ncopies, after):
    bufs = [pltpu.with_memory_space_constraint(a, pltpu.HBM) for a in list(srcs) + list(dsts)]
    nb, ns = len(bufs), len(srcs)
    operands = bufs + ([after] if after is not None else [])

    def body(*refs):
        outs = refs[len(operands):]
        send, recv, token = outs[0], outs[1], outs[-1]
        for i, (src, dst, dev) in enumerate(plan(refs[:ns], refs[ns:nb])):
            pltpu.make_async_remote_copy(src_ref=src, dst_ref=dst, send_sem=send.at[i], recv_sem=recv.at[i],
                                         device_id=dev, device_id_type=MESH).start()
        token[...] = jnp.zeros_like(token)

    res = pl.pallas_call(
        body, name=name,
        out_shape=[pltpu.SemaphoreType.DMA((ncopies,)), pltpu.SemaphoreType.DMA((ncopies,))]
        + [pltpu.HBM(a.shape, a.dtype) for a in bufs] + [jax.ShapeDtypeStruct((8, 128), F32)],
        in_specs=[HBM_SPEC] * nb + [ANY] * (len(operands) - nb),
        out_specs=[SEM_SPEC, SEM_SPEC] + [HBM_SPEC] * nb + [pl.BlockSpec(memory_space=pltpu.VMEM)],
        input_output_aliases={i: 2 + i for i in range(nb)},
        compiler_params=pltpu.CompilerParams(has_side_effects=pltpu.SideEffectType.DATAFLOW_SIDE_EFFECTING),
    )(*operands)
    return res[:-1], res[-1]


def _split_wait(name, started, plan, ns, after):
    send, recv, bufs = started[0], started[1], list(started[2:])
    nb = len(bufs)

    def body(*refs):
        send_ref, recv_ref = refs[nb], refs[nb + 1]
        for i, (src, dst, dev) in enumerate(plan(refs[:ns], refs[ns:nb])):
            cp = pltpu.make_async_remote_copy(src_ref=src, dst_ref=dst, send_sem=send_ref.at[i], recv_sem=recv_ref.at[i],
                                              device_id=dev, device_id_type=MESH)
            cp.wait_send()
            cp.wait_recv()

    res = pl.pallas_call(
        body, name=name, out_shape=[pltpu.HBM(a.shape, a.dtype) for a in bufs],
        in_specs=[HBM_SPEC] * nb + [SEM_SPEC, SEM_SPEC, ANY], out_specs=[HBM_SPEC] * nb,
        input_output_aliases={i: i for i in range(nb)},
        compiler_params=pltpu.CompilerParams(has_side_effects=pltpu.SideEffectType.DATAFLOW_SIDE_EFFECTING),
    )(*bufs, send, recv, after)
    return res[:ns], res[ns:]


def _gather_finish(name, _, gathered):
    nt = len(gathered)
    dims = [_shard_dims(g) for g in gathered]
    hs = [rows // 2 for rows, _ in dims]
    crs = [_chunk_rows(h, cols * 2) for h, (_, cols) in zip(hs, dims)]

    def body(*refs):
        outs, bufs, sems = refs[nt:2 * nt], refs[2 * nt:3 * nt], refs[3 * nt:]
        x, y, c = _place()
        sib = (x, y, 1 - c)
        for t in range(nt):
            h, cr, cols = hs[t], crs[t], dims[t][1]
            passed = [_slot(outs[t], 2 * px + py, c * h + r0, cr, cols)
                      for px, py in _other_chips(x, y) for r0 in range(0, h, cr)]
            _stream([(r, r) for r in passed], bufs[t], sems, t, sib)
        for t in range(nt):
            if len(gathered[t].shape) == 3:
                three = outs[t].at[pl.ds(0, 3), pl.ds(0, hs[t])]
            else:
                three = outs[t].at[pl.ds(0, hs[t]), pl.ds(0, 3 * dims[t][1])]
            pltpu.make_async_remote_copy(src_ref=three, dst_ref=three, send_sem=sems[1].at[t, 0],
                                         recv_sem=sems[2].at[t], device_id=sib, device_id_type=MESH).wait_recv()

    return pl.pallas_call(
        body, name=name, in_specs=[ANY] * nt, out_specs=[ANY] * nt,
        out_shape=[jax.ShapeDtypeStruct(g.shape, g.dtype) for g in gathered],
        scratch_shapes=_stream_scratch([((cr, cols), BF16) for cr, (_, cols) in zip(crs, dims)]),
        input_output_aliases={t: t for t in range(nt)},
        compiler_params=pltpu.CompilerParams(vmem_limit_bytes=VMEM_LIMIT_BYTES),
    )(*gathered)


def _sum_share(landed):
    nt = len(landed)
    hs = [a.shape[1] for a in landed]
    cs = [a.shape[2] for a in landed]
    crs = [_chunk_rows(h, 2 * c * 4) for h, c in zip(hs, cs)]
    shapes = sorted(set(zip(crs, cs)))
    which = [shapes.index(s) for s in zip(crs, cs)]

    def body(*refs):
        ins, outs = refs[:nt], refs[nt:2 * nt]
        inbufs, outbufs = refs[2 * nt:2 * nt + len(shapes)], refs[2 * nt + len(shapes):2 * nt + 2 * len(shapes)]
        lsem, ssem, osem, rsem = refs[2 * nt + 2 * len(shapes):]
        x, y, c = _place()
        sib = (x, y, 1 - c)
        for t in range(nt):
            cr, n, ib, ob = crs[t], hs[t] // crs[t], inbufs[which[t]], outbufs[which[t]]
            loads, gone = [None] * n, [None] * n

            def load(k):
                slot = k % SUM_SLOTS
                if k >= SUM_SLOTS:
                    for cp_wait in gone[k - SUM_SLOTS]:
                        cp_wait()
                loads[k] = pltpu.make_async_copy(ins[t].at[:, pl.ds(k * cr, cr)], ib.at[slot], lsem.at[t, slot])
                loads[k].start()

            load(0)
            for k in range(n):
                slot = k % SUM_SLOTS
                if k + 1 < n:
                    load(k + 1)
                loads[k].wait()
                acc = ib[slot, 0].astype(F32)
                for s in range(1, N_CHIPS):
                    acc = acc + ib[slot, s].astype(F32)
                ob[slot] = acc
                rows = outs[t].at[c, pl.ds(k * cr, cr)]
                away = pltpu.make_async_remote_copy(src_ref=ob.at[slot], dst_ref=rows, send_sem=ssem.at[t, slot],
                                                    recv_sem=rsem.at[t], device_id=sib, device_id_type=MESH)
                away.start()
                home = pltpu.make_async_copy(ob.at[slot], rows, osem.at[t, slot])
                home.start()
                gone[k] = (away.wait_send, home.wait)
            for k in range(max(0, n - SUM_SLOTS), n):
                for cp_wait in gone[k]:
                    cp_wait()
        for t in range(nt):
            other = outs[t].at[1 - c]
            pltpu.make_async_remote_copy(src_ref=other, dst_ref=other, send_sem=ssem.at[t, 0], recv_sem=rsem.at[t],
                                         device_id=sib, device_id_type=MESH).wait_recv()

    slot_sems = pltpu.SemaphoreType.DMA((nt, SUM_SLOTS))
    return pl.pallas_call(
        body, name="sum_share", in_specs=[ANY] * nt, out_specs=[ANY] * nt,
        out_shape=[jax.ShapeDtypeStruct((2, h, c), F32) for h, c in zip(hs, cs)],
        scratch_shapes=[pltpu.VMEM((SUM_SLOTS, N_CHIPS, cr, c), BF16) for cr, c in shapes]
        + [pltpu.VMEM((SUM_SLOTS, cr, c), F32) for cr, c in shapes]
        + [slot_sems, slot_sems, slot_sems, pltpu.SemaphoreType.DMA((nt,))],
        compiler_params=pltpu.CompilerParams(vmem_limit_bytes=VMEM_LIMIT_BYTES),
    )(*landed)


def _allgather_small(name, v):
    def body(v_ref, o_ref, send, recv, lsem):
        x, y, c = _place()
        me = 4 * x + 2 * y + c
        loc = pltpu.make_async_copy(v_ref, o_ref.at[me], lsem)
        loc.start()
        copies = []
        for k in range(1, N_DEV):
            px = 1 - x if k & 4 else x
            py = 1 - y if k & 2 else y
            pc = 1 - c if k & 1 else c
            cp = pltpu.make_async_remote_copy(
                src_ref=v_ref, dst_ref=o_ref.at[me], send_sem=send.at[k - 1], recv_sem=recv.at[k - 1],
                device_id=(px, py, pc), device_id_type=MESH)
            cp.start()
            copies.append(cp)
        for cp in copies:
            cp.wait()
        loc.wait()

    vm = pl.BlockSpec(memory_space=pltpu.VMEM)
    return pl.pallas_call(
        body, name=name, in_specs=[vm], out_specs=vm,
        out_shape=jax.ShapeDtypeStruct((N_DEV,) + v.shape, v.dtype),
        scratch_shapes=[pltpu.SemaphoreType.DMA((N_DEV - 1,))] * 2 + [pltpu.SemaphoreType.DMA],
    )(v)


def kernel(x, mem, norm_g, mem_norm_g, w_kv, w_out, pool_w_in, pool_w_grp, pool_scale, hgrn_w_in, hgrn_lb, hgrn_norm_g, final_g, loss_target, m_norm_g, m_mem_norm_g, m_w_kv, m_w_out, m_pool_w_in, m_pool_w_grp, m_pool_scale, m_hgrn_w_in, m_hgrn_lb, m_hgrn_norm_g, m_final_g, v_norm_g, v_mem_norm_g, v_w_kv, v_w_out, v_pool_w_in, v_pool_w_grp, v_pool_scale, v_hgrn_w_in, v_hgrn_lb, v_hgrn_norm_g, v_final_g):
    _, S, D = x.shape
    M = mem.shape[1]
    EB = 2 * D
    ECA = EB // 4
    EMIX = EB - ECA
    PG = EMIX // N_POOL_GROUPS
    NP0 = EMIX + ECA + EB
    NP1 = 3 * EMIX + ECA + EB
    SH0, SH1 = NP0 // N_CHIPS, NP1 // N_CHIPS
    DK, EK = D // N_CHIPS, EB // N_CHIPS
    TNP = 512 if all(v % 512 == 0 for v in (SH0, SH1, ECA, EMIX)) else 256
    TM = _tile(S, 1024)
    TMF = _tile(S, 2048)
    TD = _tile(D, 512)
    TDW = _tile(D, 1024)
    c0, c1 = SH0 // TNP, SH1 // TNP
    qt = EMIX // TNP
    chip = 2 * lax.axis_index("x") + lax.axis_index("y")

    xs, ms, tgt = x[0], mem[0], loss_target[0]

    sds = jax.ShapeDtypeStruct
    chip1 = chip.astype(jnp.int32).reshape(1)

    def start_gather(tag, layers, after):
        bufs = []
        for t, (w, layer) in enumerate(layers):
            w3 = w.reshape((w.shape[0], -1, w.shape[-1]))
            shape = (D, NP1) if w is hgrn_w_in else (N_CHIPS,) + w3.shape[1:]
            bufs.append(_cast_to_slot(f"cast_{tag}{t}", chip1, w3, layer, shape, after))
        return _split_start(f"gather_{tag}_start", [], bufs, _gather_plan, 3 * len(bufs), after)

    gather_a, token = start_gather("a", [(pool_w_in, 0)], None)
    gather_b, token = start_gather("b", [(w_kv, 0), (w_out, 0), (pool_w_grp, 0)], token)
    gather_c, token = start_gather("c", [(hgrn_w_in, 0)], token)
    gather_d, token = start_gather("d", [(w_kv, 1), (w_out, 1)], token)

    tek, tew = _tile(EK, 512), _tile(EK, 1024)

    mem_n = _rms_fwd("rms_mem", ms, mem_norm_g.reshape(1, D), token)
    h0 = _rms_fwd("rms0", xs, norm_g[0:1], token)
    wpin, = _gather_finish("gather_a_finish", *_split_wait("gather_a_wait", gather_a, _gather_plan, 0, h0))

    tkw = _tile(2 * ECA, 1024)

    def kv_of(layer, wkv):
        return _matmul(
            f"kv{layer}", mem_n, wkv.reshape(D, 2 * ECA), grid=(1, 2 * ECA // tkw, 1),
            a_spec=pl.BlockSpec((M, D), lambda i, j, k: (0, 0)), b_spec=pl.BlockSpec((D, tkw), lambda i, j, k: (0, j)),
            out_shape=sds((M, 2 * ECA), BF16), out_spec=pl.BlockSpec((M, tkw), lambda i, j, k: (0, j)),
            acc_shape=(M, tkw), dims=NN)

    tko = _tile(EB, 2048)

    def out_proj(layer, branch, wout, resid):
        return _matmul(
            f"out_proj{layer}", branch, wout.reshape(EB, D), grid=(S // TM, D // TDW, EB // tko),
            a_spec=pl.BlockSpec((TM, tko), IK), b_spec=pl.BlockSpec((tko, TDW), KJ),
            out_shape=sds((S, D), F32), out_spec=pl.BlockSpec((TM, TDW), IJ),
            acc_shape=(TM, TDW), dims=NN, add=resid, add_spec=pl.BlockSpec((TM, TDW), IJ))

    ones_ca = jnp.ones((1, ECA), F32)

    proj0 = _matmul(
        "proj0", h0, wpin, grid=(S // TMF, NP0 // TNP, 1),
        a_spec=pl.BlockSpec((TMF, D), lambda i, j, k: (i, 0)),
        b_spec=pl.BlockSpec((None, D, TNP), lambda i, j, k: (j // c0, 0, j % c0)),
        out_shape=sds((S, NP0), BF16), out_spec=pl.BlockSpec((TMF, TNP), IJ),
        acc_shape=(TMF, TNP), dims=NN)
    pooled = _pool_fwd(proj0, S, EMIX)
    wkv0, wout0, g_grp = _gather_finish("gather_b_finish", *_split_wait("gather_b_wait", gather_b, _gather_plan, 0, pooled))
    wgrp = g_grp.reshape(N_CHIPS, N_POOL_GROUPS, PG // N_CHIPS, PG).transpose(1, 0, 2, 3).reshape(N_POOL_GROUPS, PG, PG)
    kv = [kv_of(0, wkv0), None]
    premix0 = _matmul(
        "pool_grp", pooled, wgrp, grid=(S // TM, N_POOL_GROUPS, 1),
        a_spec=pl.BlockSpec((TM, PG), lambda i, j, k: (i, j)),
        b_spec=pl.BlockSpec((None, PG, PG), lambda i, j, k: (j, 0, 0)),
        out_shape=sds((S, EB), BF16), out_spec=pl.BlockSpec((TM, PG), lambda i, j, k: (i, j)),
        acc_shape=(TM, PG), dims=NN)
    premix0 = _ca_fwd("ca_fwd0", proj0, EMIX // ECA, kv[0], premix0, S, ECA, EMIX)
    colscale0 = jnp.concatenate([pool_scale.reshape(1, EMIX), ones_ca], axis=1)
    gblk0 = (EMIX + ECA) // ECA
    branch0 = _gate_fwd("gate_fwd0", premix0, proj0, gblk0, colscale0, S, EB, ECA)
    x1 = out_proj(0, branch0, wout0, xs)

    whin, = _gather_finish("gather_c_finish", *_split_wait("gather_c_wait", gather_c, _gather_plan, 0, x1))
    h1 = _rms_fwd("rms1", x1, norm_g[1:2])

    def proj1_cols(name, ncols, col_of, out_cols, out_dtype, out_col_of):
        return _matmul(
            name, h1, whin, grid=(S // TMF, ncols, 1),
            a_spec=pl.BlockSpec((TMF, D), lambda i, j, k: (i, 0)),
            b_spec=pl.BlockSpec((D, TNP), lambda i, j, k: (0, col_of(j))),
            out_shape=sds((S, out_cols), out_dtype), out_spec=pl.BlockSpec((TMF, TNP), lambda i, j, k: (i, out_col_of(j))),
            acc_shape=(TMF, TNP), dims=NN)

    skip_f = lambda j: jnp.where(j < qt, j, j + qt)
    proj1 = proj1_cols("proj1", NP1 // TNP - qt, skip_f, NP1, BF16, skip_f)
    fgate = proj1_cols("proj1_f", qt, lambda j: j + qt, EMIX, F32, lambda j: j)
    premix1, rstd1, states = _hgrn_fwd(proj1, fgate, hgrn_lb, S, EMIX, EB)
    wkv1, wout1 = _gather_finish("gather_d_finish", *_split_wait("gather_d_wait", gather_d, _gather_plan, 0, rstd1))
    kv[1] = kv_of(1, wkv1)
    premix1 = _ca_fwd("ca_fwd1", proj1, 3 * EMIX // ECA, kv[1], premix1, S, ECA, EMIX)
    norm_tiles = _allgather_small("allgather_norm_g", jnp.pad(hgrn_norm_g, ((0, SMALL_ROWS - 1), (0, 0))))
    hg_norm = norm_tiles[0::2, 0, :].reshape(1, EMIX)
    colscale1 = jnp.concatenate([hg_norm, ones_ca], axis=1)
    gblk1 = (3 * EMIX + ECA) // ECA
    branch1 = _gate_fwd("gate_fwd1", premix1, proj1, gblk1, colscale1, S, EB, ECA)
    x2 = out_proj(1, branch1, wout1, x1)

    dx2, dx2b, d_final_g, loss_part = _loss_head(x2, final_g.reshape(1, D), tgt)

    def out_proj_bwd(layer, dxb, branch, wout, premix, proj, gblk, colscale, dshape, dblk):
        goff, doff = gblk * ECA // tek, dblk * ECA // tek
        dpremix, dgate, dcol = _matmul(
            f"dbranch{layer}", dxb, wout, grid=(S // TMF, EB // tek, 1),
            a_spec=pl.BlockSpec((TMF, D), lambda i, j, k: (i, 0)),
            b_spec=pl.BlockSpec((None, tek, D), lambda i, j, k: (j // (EK // tek), j % (EK // tek), 0)),
            extras=[(premix, pl.BlockSpec((TMF, tek), IJ)), (proj, pl.BlockSpec((TMF, tek), lambda i, j, k: (i, goff + j))),
                    (colscale, pl.BlockSpec((1, tek), lambda i, j, k: (0, j)))],
            epilogue=_gate_bwd_epilogue,
            out_shape=[sds((S, EB), BF16), sds(dshape, BF16), sds((S // TMF, 1, EB), F32)],
            out_spec=[pl.BlockSpec((TMF, tek), IJ), pl.BlockSpec((TMF, tek), lambda i, j, k: (i, doff + j)),
                      pl.BlockSpec((None, 1, tek), lambda i, j, k: (i, 0, j))],
            acc_shape=(TMF, tek), dims=NT)
        dw = _matmul(
            f"dwout{layer}", branch, dxb, grid=(EB // tew, D // TD, 1),
            a_spec=pl.BlockSpec((S, tew), lambda i, j, k: (0, i)), b_spec=pl.BlockSpec((S, TD), lambda i, j, k: (0, j)),
            out_shape=sds((N_CHIPS, EK, D), BF16),
            out_spec=pl.BlockSpec((None, tew, TD), lambda i, j, k: (i // (EK // tew), i % (EK // tew), j)),
            acc_shape=(tew, TD), dims=TN)
        return dpremix, dgate, dcol.reshape(S // TMF, EB), dw

    def kv_bwd(layer, dkv, wkv, dmem_add):
        dkvb = dkv.astype(BF16)
        dmem = _matmul(
            f"dmem{layer}", dkvb, wkv.reshape(D, 2 * ECA), grid=(1, D // TDW, 1),
            a_spec=pl.BlockSpec((M, 2 * ECA), lambda i, j, k: (0, 0)),
            b_spec=pl.BlockSpec((TDW, 2 * ECA), lambda i, j, k: (j, 0)),
            out_shape=sds((M, D), F32), out_spec=pl.BlockSpec((M, TDW), lambda i, j, k: (0, j)), acc_shape=(M, TDW),
            dims=NT, add=dmem_add, add_spec=pl.BlockSpec((M, TDW), lambda i, j, k: (0, j)))
        dw = _matmul(
            f"dwkv{layer}", mem_n, dkvb, grid=(D // TDW, 2 * ECA // tkw, 1),
            a_spec=pl.BlockSpec((M, TDW), lambda i, j, k: (0, i)), b_spec=pl.BlockSpec((M, tkw), lambda i, j, k: (0, j)),
            out_shape=sds((D, 2 * ECA), BF16), out_spec=pl.BlockSpec((TDW, tkw), IJ), acc_shape=(TDW, tkw), dims=TN)
        return dmem, dw.reshape(N_CHIPS, DK, 2 * ECA)

    dpremix1, drest1, dcol1, gw_out1 = out_proj_bwd(1, dx2b, branch1, wout1, premix1, proj1, gblk1, colscale1,
                                                    (S, ECA + EB), 1)
    drest1, dkv1 = _ca_bwd("ca_bwd1", dpremix1, proj1, 3 * EMIX // ECA, kv[1], drest1, 0, S, ECA, EMIX)
    dqfi, dlb = _hgrn_bwd(dpremix1, premix1, rstd1, states, proj1, fgate, hgrn_lb, S, EMIX)
    nq, nr = 3 * qt, (ECA + EB) // TNP
    tkh = _tile(EMIX, 1024) if (ECA + EB) % _tile(EMIX, 1024) == 0 else TNP
    kq = EMIX // tkh
    dh1 = _matmul(
        "dh1_qfi", dqfi, whin, grid=(S // TM, D // TDW, 3),
        a_spec=pl.BlockSpec((None, TM, EMIX), lambda i, j, k: (k, i, 0)),
        b_spec=pl.BlockSpec((TDW, EMIX), lambda i, j, k: (j, k)),
        out_shape=sds((S, D), F32), out_spec=pl.BlockSpec((TM, TDW), IJ), acc_shape=(TM, TDW), dims=NT)
    dh1 = _matmul(
        "dh1_rest", drest1, whin, grid=(S // TM, D // TDW, (ECA + EB) // tkh), a_spec=pl.BlockSpec((TM, tkh), IK),
        b_spec=pl.BlockSpec((TDW, tkh), lambda i, j, k: (j, k + 3 * kq)),
        out_shape=sds((S, D), F32), out_spec=pl.BlockSpec((TM, TDW), IJ), acc_shape=(TM, TDW), dims=NT,
        add=dh1, add_spec=pl.BlockSpec((TM, TDW), IJ))
    gw_hin = _matmul(
        "dwhin_qfi", h1, dqfi, grid=(D // TDW, nq, 1), a_spec=pl.BlockSpec((S, TDW), lambda i, j, k: (0, i)),
        b_spec=pl.BlockSpec((None, S, TNP), lambda i, j, k: (j // qt, 0, j % qt)),
        out_shape=sds((N_CHIPS, D, SH1), BF16), out_spec=pl.BlockSpec((None, TDW, TNP), lambda i, j, k: (j // c1, i, j % c1)),
        acc_shape=(TDW, TNP), dims=TN)
    gw_hin = _matmul(
        "dwhin_rest", h1, drest1, grid=(D // TDW, nr, 1), a_spec=pl.BlockSpec((S, TDW), lambda i, j, k: (0, i)),
        b_spec=pl.BlockSpec((S, TNP), lambda i, j, k: (0, j)), out_shape=sds((N_CHIPS, D, SH1), BF16),
        out_spec=pl.BlockSpec((None, TDW, TNP), lambda i, j, k: ((j + nq) // c1, i, (j + nq) % c1)),
        acc_shape=(TDW, TNP), dims=TN, alias=gw_hin)
    dmem, gw_kv1 = kv_bwd(1, dkv1, wkv1, None)

    core_chip = jnp.stack([lax.axis_index("c"), chip]).astype(jnp.int32)

    def reduce_in_chip(tag, stacks):
        got = _exchange_halves(f"exchange_halves{tag}", stacks)
        pairs = [_add_halves(f"add_halves{tag}_{t}", core_chip, g.reshape(N_CHIPS, 2, g.shape[1] // 2, g.shape[2]), r)
                 for t, (g, r) in enumerate(zip(stacks, got))]
        return [p for p, _ in pairs], [own for _, own in pairs]

    parts1, landed1 = reduce_in_chip(1, [gw_kv1, gw_out1, gw_hin])
    scatter1, token1 = _split_start("scatter1_start", parts1, landed1, _scatter_plan, 3 * len(parts1), None)
    dx1, dx1b, d_ng1 = _rms_bwd("rms_bwd1", dh1, x1, norm_g[1:2], dx2, token1)

    dpremix0, dproj0, dcol0, gw_out0 = out_proj_bwd(0, dx1b, branch0, wout0, premix0, proj0, gblk0, colscale0,
                                                    (S, NP0), gblk0)
    dproj0, dkv0 = _ca_bwd("ca_bwd0", dpremix0, proj0, EMIX // ECA, kv[0], dproj0, EMIX // ECA, S, ECA, EMIX)
    dmem, gw_kv0 = kv_bwd(0, dkv0, wkv0, dmem)
    parts_a, landed_a = reduce_in_chip("0a", [gw_kv0, gw_out0])
    scatter_a, token_a = _split_start("scatter0a_start", parts_a, landed_a, _scatter_plan, 3 * len(parts_a), None)
    dpooled = _matmul(
        "dpooled", dpremix0, wgrp, grid=(S // TM, N_POOL_GROUPS, 1), a_spec=pl.BlockSpec((TM, PG), IJ),
        b_spec=pl.BlockSpec((None, PG, PG), lambda i, j, k: (j, 0, 0)),
        out_shape=sds((S, EMIX), F32), out_spec=pl.BlockSpec((TM, PG), IJ), acc_shape=(TM, PG), dims=NT, after=token_a)
    def rows_by_chip(r, _, outs):
        outs[0][...] = r.reshape(N_CHIPS, PG // N_CHIPS, PG).astype(BF16)

    gw_grp, = _matmul(
        "dwgrp", pooled, dpremix0, grid=(N_POOL_GROUPS, 1, 1), a_spec=pl.BlockSpec((S, PG), lambda i, j, k: (0, i)),
        b_spec=pl.BlockSpec((S, PG), lambda i, j, k: (0, i)), epilogue=rows_by_chip,
        out_shape=[sds((N_CHIPS, N_POOL_GROUPS, PG // N_CHIPS, PG), BF16)],
        out_spec=[pl.BlockSpec((N_CHIPS, None, PG // N_CHIPS, PG), lambda i, j, k: (0, i, 0, 0))],
        acc_shape=(PG, PG), dims=TN)
    dproj0 = _pool_bwd(dpooled, dproj0, S, EMIX)
    gw_pin = _matmul(
        "dwpin", h0, dproj0, grid=(D // TDW, NP0 // TNP, 1), a_spec=pl.BlockSpec((S, TDW), lambda i, j, k: (0, i)),
        b_spec=pl.BlockSpec((S, TNP), lambda i, j, k: (0, j)), out_shape=sds((N_CHIPS, D, SH0), BF16),
        out_spec=pl.BlockSpec((None, TDW, TNP), lambda i, j, k: (j // c0, i, j % c0)), acc_shape=(TDW, TNP), dims=TN)
    parts_b, landed_b = reduce_in_chip("0b", [gw_pin, gw_grp.reshape(N_CHIPS, PG, PG)])
    scatter_b, token_b = _split_start("scatter0b_start", parts_b, landed_b, _scatter_plan, 3 * len(parts_b), None)
    dh0 = _matmul(
        "dh0", dproj0, wpin, grid=(S // TM, D // TDW, N_CHIPS), a_spec=pl.BlockSpec((TM, SH0), IK),
        b_spec=pl.BlockSpec((None, TDW, SH0), lambda i, j, k: (k, j, 0)),
        out_shape=sds((S, D), F32), out_spec=pl.BlockSpec((TM, TDW), IJ), acc_shape=(TM, TDW), dims=NT, after=token_b)
    grad_x, _, d_ng0 = _rms_bwd("rms_bwd0", dh0, xs, norm_g[0:1], dx1)
    _, _, d_mng = _rms_bwd("rms_bwd_mem", dmem, ms, mem_norm_g.reshape(1, D), jnp.zeros_like(ms))

    _, landed1 = _split_wait("scatter1_wait", scatter1, _scatter_plan, len(parts1), grad_x)
    _, landed_a = _split_wait("scatter0a_wait", scatter_a, _scatter_plan, len(parts_a), grad_x)
    _, landed_b = _split_wait("scatter0b_wait", scatter_b, _scatter_plan, len(parts_b), grad_x)
    landed = [landed_a[0], landed1[0], landed_a[1], landed1[1], landed_b[0], landed_b[1], landed1[2]]
    fulls = _sum_share(landed)
    f2 = [f.reshape(-1, f.shape[-1]) for f in fulls]
    grads, deltas, new_m, new_v = {}, {}, {}, {}
    for n, w, mm, vv, gs in (("w_kv", w_kv, m_w_kv, v_w_kv, f2[0:2]), ("w_out", w_out, m_w_out, v_w_out, f2[2:4]),
                             ("pool_w_in", pool_w_in, m_pool_w_in, v_pool_w_in, f2[4:5]),
                             ("pool_w_grp", pool_w_grp, m_pool_w_grp, v_pool_w_grp, f2[5:6]),
                             ("hgrn_w_in", hgrn_w_in, m_hgrn_w_in, v_hgrn_w_in, f2[6:7])):
        as3d = lambda a: a.reshape((a.shape[0], -1, a.shape[-1]))
        outs = _adamw_layers(f"adamw_{n}", as3d(w), gs, as3d(mm), as3d(vv))
        grads[n], deltas[n], new_m[n], new_v[n] = [o.reshape(w.shape) for o in outs]

    Wd = EMIX
    nshard = EMIX // N_CHIPS
    summed_rows = [(v, True, v.shape[1]) for v in (d_ng0, d_ng1, d_mng)] + [
        (dcol0, True, EMIX), (dlb, True, EMIX), (dcol1, True, EMIX), (d_final_g, True, D), (loss_part, True, 128)]
    partial = _pack_rows("pack_partials", summed_rows, Wd)
    g_pack, loss = _small_sum(_allgather_small("allgather_grads", partial), hgrn_lb, chip1, nshard)

    def pack_small(name, ng, mng, ps, lb_, hn, fg):
        return _pack_rows(name, [(ng, False, D), (mng.reshape(1, D), False, D), (ps, False, EMIX), (lb_, False, EMIX),
                                 (hn, False, nshard), (fg.reshape(1, D), False, D)], Wd)

    d_pack, m_pack, v_pack = _adamw(
        "adamw_small", pack_small("pack_small_w", norm_g, mem_norm_g, pool_scale, hgrn_lb, hgrn_norm_g, final_g), g_pack,
        pack_small("pack_small_m", m_norm_g, m_mem_norm_g, m_pool_scale, m_hgrn_lb, m_hgrn_norm_g, m_final_g),
        pack_small("pack_small_v", v_norm_g, v_mem_norm_g, v_pool_scale, v_hgrn_lb, v_hgrn_norm_g, v_final_g))
    layout = [(2, D), (1, D), (1, EMIX), (2, EMIX), (1, nshard), (1, D)]
    for tag, pack, out in (("g", g_pack, grads), ("d", d_pack, deltas), ("m", m_pack, new_m), ("v", v_pack, new_v)):
        ng, mng, ps, lb_, hn, fg = _unpack_rows(f"unpack_small_{tag}", pack, layout)
        out.update(norm_g=ng, mem_norm_g=mng.reshape(D), pool_scale=ps, hgrn_lb=lb_, hgrn_norm_g=hn, final_g=fg.reshape(D))
    loss = loss[0, 0]

    order = ["norm_g", "mem_norm_g", "w_kv", "w_out", "pool_w_in", "pool_w_grp", "pool_scale", "hgrn_w_in", "hgrn_lb",
             "hgrn_norm_g", "final_g"]
    return (loss, grad_x.reshape(1, S, D), *[grads[n] for n in order], *[deltas[n] for n in order],
            *[new_m[n] for n in order], *[new_v[n] for n in order])
```

```python
import functools

import jax
import jax.numpy as jnp
from jax import lax
from jax.experimental import pallas as pl
from jax.experimental.pallas import tpu as pltpu

F32 = jnp.float32
BF16 = jnp.bfloat16
MESH = pl.DeviceIdType.MESH
ANY = pl.BlockSpec(memory_space=pl.ANY)

EPS = 1e-6
HG_HEAD_DIM = 128
HG_CHUNK = 64
CA_HEADS = 4
N_POOL_GROUPS = 4
POOL_HALO = 128
ADAM_LR = 0.001
ADAM_B1 = 0.9
ADAM_B2 = 0.999
ADAM_EPS = 1e-08
ADAM_WD = 0.01
ADAM_STEP = 10
N_CHIPS = 4
N_DEV = 8
VMEM_LIMIT_BYTES = 56 * 1024 * 1024
SMALL_ROWS = 8
STREAM_CHUNK_BYTES = 4 * 1024 * 1024
STREAM_SLOTS = 3
SUM_SLOTS = 2
RING_SLOTS = 3


def _params(*sem):
    return pltpu.CompilerParams(dimension_semantics=sem, vmem_limit_bytes=VMEM_LIMIT_BYTES)


def _tile(n, pref):
    t = pref
    while n % t:
        t //= 2
    return t


def _sigmoid(x):
    return 1.0 / (1.0 + jnp.exp(-x))


def _matmul(name, a, b, *, grid, a_spec, b_spec, out_shape, out_spec, acc_shape, dims,
            add=None, add_spec=None, alias=None, after=None, extras=(), epilogue=None):
    nk = grid[2]
    has_add = add is not None
    has_alias = alias is not None
    has_after = after is not None
    n_out = len(out_shape) if epilogue is not None else 1

    def body(*refs):
        a_ref, b_ref = refs[0], refs[1]
        pos = 2
        add_ref = None
        if has_add:
            add_ref = refs[pos]
            pos += 1
        extra_refs = refs[pos:pos + len(extras)]
        pos += len(extras) + has_alias + has_after
        o_refs = refs[pos:pos + n_out]
        prod = lax.dot_general(a_ref[...], b_ref[...], (dims, ((), ())), preferred_element_type=F32)

        def finish(r):
            if epilogue is not None:
                epilogue(r, extra_refs, o_refs)
                return
            if has_add:
                r = r + add_ref[...].astype(F32)
            o_refs[0][...] = r.astype(o_refs[0].dtype)

        if nk == 1:
            finish(prod)
            return
        acc_ref = refs[pos + n_out]
        k = pl.program_id(2)

        @pl.when(k == 0)
        def _():
            acc_ref[...] = prod

        @pl.when(k > 0)
        def _():
            acc_ref[...] += prod

        @pl.when(k == nk - 1)
        def _():
            finish(acc_ref[...])

    operands = [a, b]
    in_specs = [a_spec, b_spec]
    if has_add:
        operands.append(add)
        in_specs.append(add_spec)
    for arr, spec in extras:
        operands.append(arr)
        in_specs.append(spec)
    aliases = {}
    if has_alias:
        aliases = {len(operands): 0}
        operands.append(alias)
        in_specs.append(ANY)
    if has_after:
        operands.append(after)
        in_specs.append(ANY)
    return pl.pallas_call(
        body, name=name, grid=grid, in_specs=in_specs, out_specs=out_spec, out_shape=out_shape,
        scratch_shapes=[pltpu.VMEM(acc_shape, F32)] if nk > 1 else [], input_output_aliases=aliases,
        compiler_params=_params("parallel", "parallel", "arbitrary"),
    )(*operands)


IJ = lambda i, j, k: (i, j)
IK = lambda i, j, k: (i, k)
KJ = lambda i, j, k: (k, j)
KI = lambda i, j, k: (k, i)
NN = ((1,), (0,))
NT = ((1,), (1,))
TN = ((0,), (0,))


def _rms_fwd(name, x, g, after=None):
    R, D = x.shape
    tr = _tile(R, 256)
    extra = [] if after is None else [after]

    def body(x_ref, g_ref, *rest):
        xf = x_ref[...]
        r = lax.rsqrt(jnp.mean(xf * xf, axis=-1, keepdims=True) + EPS)
        rest[-1][...] = (xf * r * g_ref[...]).astype(BF16)

    return pl.pallas_call(
        body, name=name, grid=(R // tr,),
        in_specs=[pl.BlockSpec((tr, D), lambda i: (i, 0)), pl.BlockSpec((1, D), lambda i: (0, 0))] + [ANY] * len(extra),
        out_specs=pl.BlockSpec((tr, D), lambda i: (i, 0)),
        out_shape=jax.ShapeDtypeStruct((R, D), BF16), compiler_params=_params("parallel"),
    )(x, g, *extra)


def _rms_bwd(name, dh, x, g, dres, after=None):
    R, D = x.shape
    tr = _tile(R, 256)
    extra = [] if after is None else [after]

    def body(dh_ref, x_ref, g_ref, dres_ref, *rest):
        dx_ref, dxb_ref, dg_ref = rest[len(extra):]
        xf = x_ref[...]
        r = lax.rsqrt(jnp.mean(xf * xf, axis=-1, keepdims=True) + EPS)
        xn = xf * r
        d = dh_ref[...]
        dyg = d * g_ref[...]
        dx = r * (dyg - xn * jnp.mean(dyg * xn, axis=-1, keepdims=True)) + dres_ref[...]
        dx_ref[...] = dx
        dxb_ref[...] = dx.astype(BF16)

        @pl.when(pl.program_id(0) == 0)
        def _():
            dg_ref[...] = jnp.zeros_like(dg_ref)

        dg_ref[...] += jnp.sum(d * xn, axis=0, keepdims=True)

    row = pl.BlockSpec((tr, D), lambda i: (i, 0))
    vec = pl.BlockSpec((1, D), lambda i: (0, 0))
    return pl.pallas_call(
        body, name=name, grid=(R // tr,), in_specs=[row, row, vec, row] + [ANY] * len(extra), out_specs=[row, row, vec],
        out_shape=[jax.ShapeDtypeStruct((R, D), F32), jax.ShapeDtypeStruct((R, D), BF16),
                   jax.ShapeDtypeStruct((1, D), F32)],
        compiler_params=_params("arbitrary"),
    )(dh, x, g, dres, *extra)


def _loss_head(x2, g, target):
    R, D = x2.shape
    tr = _tile(R, 256)

    def body(x_ref, g_ref, t_ref, dx_ref, dxb_ref, dg_ref, loss_ref):
        xf = x_ref[...]
        gg = g_ref[...]
        r = lax.rsqrt(jnp.mean(xf * xf, axis=-1, keepdims=True) + EPS)
        xn = xf * r
        e = xn * gg - t_ref[...]
        part = 0.5 * jnp.sum(jnp.mean(e * e, axis=-1, keepdims=True), axis=0, keepdims=True)
        dy = e * (1.0 / D)
        dyg = dy * gg
        dx = r * (dyg - xn * jnp.mean(dyg * xn, axis=-1, keepdims=True))
        dx_ref[...] = dx
        dxb_ref[...] = dx.astype(BF16)

        @pl.when(pl.program_id(0) == 0)
        def _():
            dg_ref[...] = jnp.zeros_like(dg_ref)
            loss_ref[...] = jnp.zeros_like(loss_ref)

        dg_ref[...] += jnp.sum(dy * xn, axis=0, keepdims=True)
        loss_ref[...] += jnp.broadcast_to(part, loss_ref.shape)

    row = pl.BlockSpec((tr, D), lambda i: (i, 0))
    vec = pl.BlockSpec((1, D), lambda i: (0, 0))
    return pl.pallas_call(
        body, name="loss_head", grid=(R // tr,), in_specs=[row, vec, row],
        out_specs=[row, row, vec, pl.BlockSpec((1, 128), lambda i: (0, 0))],
        out_shape=[jax.ShapeDtypeStruct((R, D), F32), jax.ShapeDtypeStruct((R, D), BF16),
                   jax.ShapeDtypeStruct((1, D), F32), jax.ShapeDtypeStruct((1, 128), F32)],
        compiler_params=_params("arbitrary"),
    )(x2, g, target)


def _pool_band(tr, reverse, w):
    r = lax.broadcasted_iota(jnp.int32, (tr, tr + POOL_HALO), 0)
    c = lax.broadcasted_iota(jnp.int32, (tr, tr + POOL_HALO), 1)
    if reverse:
        inside = (c >= r) & (c < r + w)
    else:
        cc = c - POOL_HALO
        inside = (cc <= r) & (cc > r - w)
    return jnp.where(inside, 1.0, 0.0).astype(BF16)


def _pool_fwd(proj, S, EMIX):
    PG = EMIX // N_POOL_GROUPS
    cb = PG
    tr = _tile(S, 512)
    per_group = PG // cb

    def body(u_ref, o_ref, ext):
        i = pl.program_id(1)
        w = jnp.left_shift(2, pl.program_id(0) // per_group)

        @pl.when(i == 0)
        def _():
            ext[0:POOL_HALO, :] = jnp.zeros((POOL_HALO, cb), BF16)

        u = u_ref[...]
        ext[POOL_HALO:, :] = u
        win = jnp.dot(_pool_band(tr, False, w), ext[...], preferred_element_type=F32)
        pos = i * tr + lax.broadcasted_iota(jnp.int32, (tr, 1), 0)
        cnt = jnp.minimum(pos + 1, w).astype(F32)
        o_ref[...] = (win / cnt - u.astype(F32)).astype(BF16)
        ext[0:POOL_HALO, :] = u[tr - POOL_HALO:, :]

    return pl.pallas_call(
        body, name="pool_fwd", grid=(EMIX // cb, S // tr),
        in_specs=[pl.BlockSpec((tr, cb), lambda j, i: (i, j))],
        out_specs=pl.BlockSpec((tr, cb), lambda j, i: (i, j)),
        out_shape=jax.ShapeDtypeStruct((S, EMIX), BF16),
        scratch_shapes=[pltpu.VMEM((tr + POOL_HALO, cb), BF16)],
        compiler_params=_params("parallel", "arbitrary"),
    )(proj)


def _pool_bwd(dpooled, dproj, S, EMIX):
    PG = EMIX // N_POOL_GROUPS
    cb = PG
    tr = _tile(S, 512)
    per_group = PG // cb
    nrt = S // tr

    def body(d_ref, _, o_ref, ext):
        step = pl.program_id(1)
        i = nrt - 1 - step
        w = jnp.left_shift(2, pl.program_id(0) // per_group)

        @pl.when(step == 0)
        def _():
            ext[tr:, :] = jnp.zeros((POOL_HALO, cb), BF16)

        d = d_ref[...]
        pos = i * tr + lax.broadcasted_iota(jnp.int32, (tr, 1), 0)
        cnt = jnp.minimum(pos + 1, w).astype(F32)
        z = (d / cnt).astype(BF16)
        ext[0:tr, :] = z
        win = jnp.dot(_pool_band(tr, True, w), ext[...], preferred_element_type=F32)
        o_ref[...] = (win - d).astype(BF16)
        ext[tr:, :] = z[0:POOL_HALO, :]

    return pl.pallas_call(
        body, name="pool_bwd", grid=(EMIX // cb, nrt),
        in_specs=[pl.BlockSpec((tr, cb), lambda j, s: (nrt - 1 - s, j)), ANY],
        out_specs=pl.BlockSpec((tr, cb), lambda j, s: (nrt - 1 - s, j)),
        out_shape=jax.ShapeDtypeStruct(dproj.shape, dproj.dtype),
        scratch_shapes=[pltpu.VMEM((tr + POOL_HALO, cb), BF16)],
        input_output_aliases={1: 0},
        compiler_params=_params("parallel", "arbitrary"),
    )(dpooled, dproj)


def _ca_fwd(name, proj, qblk, kv, premix, S, ECA, EMIX):
    M = kv.shape[0]
    hd = ECA // CA_HEADS
    ts = _tile(S, 512)
    scale = hd ** -0.5

    def body(q_ref, kv_ref, _, o_ref):
        for h in range(CA_HEADS):
            q = q_ref[:, h * hd:(h + 1) * hd]
            k = kv_ref[:, h * hd:(h + 1) * hd]
            v = kv_ref[:, ECA + h * hd:ECA + (h + 1) * hd]
            s = lax.dot_general(q, k, (NT, ((), ())), preferred_element_type=F32) * scale
            s = s - jnp.max(s, axis=-1, keepdims=True)
            p = jnp.exp(s)
            p = p / jnp.sum(p, axis=-1, keepdims=True)
            o = jnp.dot(p.astype(BF16), v, preferred_element_type=F32)
            o_ref[:, h * hd:(h + 1) * hd] = o.astype(BF16)

    return pl.pallas_call(
        body, name=name, grid=(S // ts,),
        in_specs=[pl.BlockSpec((ts, ECA), lambda i: (i, qblk)), pl.BlockSpec((M, 2 * ECA), lambda i: (0, 0)), ANY],
        out_specs=pl.BlockSpec((ts, ECA), lambda i: (i, EMIX // ECA)),
        out_shape=jax.ShapeDtypeStruct(premix.shape, premix.dtype),
        input_output_aliases={2: 0}, compiler_params=_params("parallel"),
    )(proj, kv, premix)


def _ca_bwd(name, dpremix, proj, qblk, kv, dbuf, dblk, S, ECA, EMIX):
    M = kv.shape[0]
    hd = ECA // CA_HEADS
    ts = _tile(S, 512)
    scale = hd ** -0.5

    def body(do_ref, q_ref, kv_ref, _, dq_ref, dkv_ref):
        @pl.when(pl.program_id(0) == 0)
        def _():
            dkv_ref[...] = jnp.zeros_like(dkv_ref)

        for h in range(CA_HEADS):
            lo, hi = h * hd, (h + 1) * hd
            q = q_ref[:, lo:hi]
            k = kv_ref[:, lo:hi]
            v = kv_ref[:, ECA + lo:ECA + hi]
            do = do_ref[:, lo:hi]
            s = lax.dot_general(q, k, (NT, ((), ())), preferred_element_type=F32) * scale
            s = s - jnp.max(s, axis=-1, keepdims=True)
            p = jnp.exp(s)
            p = p / jnp.sum(p, axis=-1, keepdims=True)
            pb = p.astype(BF16)
            dkv_ref[:, ECA + lo:ECA + hi] += lax.dot_general(pb, do, (TN, ((), ())), preferred_element_type=F32)
            dp = lax.dot_general(do, v, (NT, ((), ())), preferred_element_type=F32)
            ds = (p * (dp - jnp.sum(p * dp, axis=-1, keepdims=True)) * scale).astype(BF16)
            dq_ref[:, lo:hi] = jnp.dot(ds, k, preferred_element_type=F32).astype(BF16)
            dkv_ref[:, lo:hi] += lax.dot_general(ds, q, (TN, ((), ())), preferred_element_type=F32)

    return pl.pallas_call(
        body, name=name, grid=(S // ts,),
        in_specs=[pl.BlockSpec((ts, ECA), lambda i: (i, EMIX // ECA)), pl.BlockSpec((ts, ECA), lambda i: (i, qblk)),
                  pl.BlockSpec((M, 2 * ECA), lambda i: (0, 0)), ANY],
        out_specs=[pl.BlockSpec((ts, ECA), lambda i: (i, dblk)), pl.BlockSpec((M, 2 * ECA), lambda i: (0, 0))],
        out_shape=[jax.ShapeDtypeStruct(dbuf.shape, dbuf.dtype), jax.ShapeDtypeStruct((M, 2 * ECA), F32)],
        input_output_aliases={3: 0}, compiler_params=_params("arbitrary"),
    )(dpremix, proj, kv, dbuf)


def _gate_fwd(name, premix, proj, gblk, colscale, S, EB, ECA):
    ts = _tile(S, 512)
    nj = EB // ECA
    nsteps = (S // ts) * nj

    def body(p_hbm, g_hbm, c_ref, o_ref, pbuf, gbuf, sems):
        s = pl.program_id(0) * nj + pl.program_id(1)

        def tiles(step):
            slot = step % RING_SLOTS
            rows = pl.ds(pl.multiple_of((step // nj) * ts, ts), ts)
            col = pl.multiple_of((step % nj) * ECA, 128)
            return (pltpu.make_async_copy(p_hbm.at[rows, pl.ds(col, ECA)], pbuf.at[slot], sems.at[0, slot]),
                    pltpu.make_async_copy(g_hbm.at[rows, pl.ds(gblk * ECA + col, ECA)], gbuf.at[slot], sems.at[1, slot]))

        @pl.when(s == 0)
        def _():
            for first in range(min(RING_SLOTS - 1, nsteps)):
                for cp in tiles(s + first):
                    cp.start()

        @pl.when(s + RING_SLOTS - 1 < nsteps)
        def _():
            for cp in tiles(s + RING_SLOTS - 1):
                cp.start()

        for cp in tiles(s):
            cp.wait()
        slot = s % RING_SLOTS
        g = gbuf[slot].astype(F32)
        o_ref[...] = (pbuf[slot].astype(F32) * c_ref[...] * (g * _sigmoid(g))).astype(BF16)

    return pl.pallas_call(
        body, name=name, grid=(S // ts, nj),
        in_specs=[ANY, ANY, pl.BlockSpec((1, ECA), lambda i, j: (0, j))],
        out_specs=pl.BlockSpec((ts, ECA), lambda i, j: (i, j)),
        out_shape=jax.ShapeDtypeStruct((S, EB), BF16),
        scratch_shapes=[pltpu.VMEM((RING_SLOTS, ts, ECA), BF16), pltpu.VMEM((RING_SLOTS, ts, ECA), BF16),
                        pltpu.SemaphoreType.DMA((2, RING_SLOTS))],
        compiler_params=_params("arbitrary", "arbitrary"),
    )(premix, proj, colscale)


def _gate_bwd_epilogue(db, extra_refs, out_refs):
    p_ref, g_ref, c_ref = extra_refs
    dp_ref, dg_ref, dc_ref = out_refs
    g = g_ref[...].astype(F32)
    sg = _sigmoid(g)
    si = g * sg
    c = c_ref[...]
    t = db * p_ref[...].astype(F32)
    dp_ref[...] = (db * si * c).astype(BF16)
    dg_ref[...] = (t * c * (sg * (1.0 + g * (1.0 - sg)))).astype(BF16)
    dc_ref[...] = jnp.sum(t * si, axis=0, keepdims=True)


def _hgrn_lb(lb_ref):
    l0 = lb_ref[0:1, :]
    l1 = lb_ref[1:2, :]
    mx = jnp.maximum(l0, l1)
    e0 = jnp.exp(l0 - mx)
    e1 = jnp.exp(l1 - mx)
    return e1 / (e0 + e1)


def _bdot(a, b, ca, cb):
    return lax.dot_general(a, b, (((ca,), (cb,)), ((0,), (0,))), preferred_element_type=F32)


def _tri_sum(tri, x):
    hi = x.astype(BF16)
    lo = (x - hi.astype(F32)).astype(BF16)
    tri = tri.astype(BF16)
    return _bdot(tri, hi, 2, 1) + _bdot(tri, lo, 2, 1)


def _hgrn_chunks(qin, fin, lbh, n):
    C = HG_CHUNK
    row = lax.broadcasted_iota(jnp.int32, (n, C, C), 1)
    col = lax.broadcasted_iota(jnp.int32, (n, C, C), 2)
    causal = row >= col
    sg = _sigmoid(fin)
    f = lbh + (1.0 - lbh) * sg
    k = 1.0 - f
    g = jnp.log(f)
    b = _tri_sum(jnp.where(causal, 1.0, 0.0), g)
    b_last = jnp.sum(g, axis=1, keepdims=True)
    eb = jnp.exp(b)
    einv = jnp.exp(-b)
    eend = jnp.exp(b_last - b)
    sq = _sigmoid(qin)
    a = qin * sq * (HG_HEAD_DIM ** -0.5) * eb
    bm = k * einv
    e = k * eend
    d = jnp.exp(b_last)
    p = jnp.where(causal, _bdot(a.astype(BF16), bm.astype(BF16), 2, 2), 0.0)
    return dict(causal=causal, sg=sg, f=f, eb=eb, einv=einv, eend=eend, sq=sq, a=a, bm=bm, e=e, d=d, p=p)


def _hgrn_fwd(proj, fgate, hgrn_lb, S, EMIX, EB):
    HD, C = HG_HEAD_DIM, HG_CHUNK
    HH = EMIX // HD
    hb = 6 if HH % 6 == 0 else 1
    W = hb * HD
    tr = _tile(S, 512)
    n = tr // C

    def body(q_ref, f_ref, i_ref, lb_ref, o_ref, rstd_ref, st_ref, state):
        @pl.when(pl.program_id(1) == 0)
        def _():
            state[...] = jnp.zeros_like(state)

        lb = _hgrn_lb(lb_ref)
        for h in range(hb):
            cs = slice(h * HD, (h + 1) * HD)
            qin = q_ref[:, cs].astype(F32).reshape(n, C, HD)
            fin = f_ref[:, cs].reshape(n, C, HD)
            v = i_ref[:, cs].reshape(n, C, HD)
            t = _hgrn_chunks(qin, fin, lb[:, cs], n)
            upd = _bdot(v, t["e"].astype(BF16), 1, 1)
            st = state[h]
            for c in range(n):
                st_ref[h, c] = st
                st = st * t["d"][c] + upd[c]
            state[h] = st
            o = _bdot(t["p"].astype(BF16), v, 2, 1) + _bdot(t["a"].astype(BF16), st_ref[h].astype(BF16), 2, 2)
            rstd = lax.rsqrt(jnp.mean(o * o, axis=-1, keepdims=True) + EPS)
            o_ref[:, cs] = (o * rstd).reshape(tr, HD).astype(BF16)
            rstd_ref[:, cs] = jnp.broadcast_to(rstd, (n, C, HD)).reshape(tr, HD)

    blk = lambda off: pl.BlockSpec((tr, W), lambda g, i: (i, off + g))
    return pl.pallas_call(
        body, name="hgrn_fwd", grid=(HH // hb, S // tr),
        in_specs=[blk(0), blk(0), blk(2 * EMIX // W), pl.BlockSpec((2, W), lambda g, i: (0, g))],
        out_specs=[blk(0), blk(0), pl.BlockSpec((hb, n, HD, HD), lambda g, i: (g, i, 0, 0))],
        out_shape=[jax.ShapeDtypeStruct((S, EB), BF16), jax.ShapeDtypeStruct((S, EMIX), F32),
                   jax.ShapeDtypeStruct((HH, S // C, HD, HD), F32)],
        scratch_shapes=[pltpu.VMEM((hb, HD, HD), F32)],
        compiler_params=_params("parallel", "arbitrary"),
    )(proj, fgate, proj, hgrn_lb)


def _hgrn_bwd(dpremix, premix, rstd, states, proj, fgate, hgrn_lb, S, EMIX):
    HD, C = HG_HEAD_DIM, HG_CHUNK
    HH = EMIX // HD
    hb = 6 if HH % 6 == 0 else 1
    W = hb * HD
    tr = _tile(S, 512)
    n = tr // C
    nrt = S // tr

    def body(do_ref, on_ref, rstd_ref, st_ref, q_ref, f_ref, i_ref, lb_ref, d_ref, dlb_ref, dstate, dsbuf):
        @pl.when(pl.program_id(1) == 0)
        def _():
            dstate[...] = jnp.zeros_like(dstate)
            dlb_ref[...] = jnp.zeros_like(dlb_ref)

        lb = _hgrn_lb(lb_ref)
        for h in range(hb):
            cs = slice(h * HD, (h + 1) * HD)
            qin = q_ref[:, cs].astype(F32).reshape(n, C, HD)
            fin = f_ref[:, cs].reshape(n, C, HD)
            v = i_ref[:, cs].reshape(n, C, HD)
            lbh = lb[:, cs]
            t = _hgrn_chunks(qin, fin, lbh, n)
            a, bm, e, d, p = t["a"], t["bm"], t["e"], t["d"], t["p"]
            ab, bmb, eb16 = a.astype(BF16), bm.astype(BF16), e.astype(BF16)
            on = on_ref[:, cs].astype(F32).reshape(n, C, HD)
            dn = do_ref[:, cs].astype(F32).reshape(n, C, HD)
            do = rstd_ref[:, cs].reshape(n, C, HD) * (dn - on * jnp.mean(dn * on, axis=-1, keepdims=True))
            dob = do.astype(BF16)
            grow = _bdot(dob, ab, 1, 1)
            ds = dstate[h]
            for c in reversed(range(n)):
                dsbuf[h, c] = ds
                ds = ds * d[c] + grow[c]
            dstate[h] = ds
            dst = dsbuf[h]
            st = st_ref[h]
            dstb = dst.astype(BF16)
            dp = jnp.where(t["causal"], _bdot(dob, v, 2, 2), 0.0).astype(BF16)
            dv = _bdot(p.astype(BF16), dob, 1, 1) + _bdot(eb16, dstb, 2, 2)
            da = _bdot(dp, bmb, 2, 1) + _bdot(dob, st.astype(BF16), 2, 1)
            dbm = _bdot(dp, ab, 1, 1)
            de = _bdot(v, dstb, 2, 1)
            dd = jnp.sum(dst * st, axis=1, keepdims=True)
            dk = dbm * t["einv"] + de * t["eend"]
            dee = de * e
            db = da * a - dbm * bm - dee
            extra = jnp.sum(dee, axis=1, keepdims=True) + dd * d
            upper = jnp.where(lax.broadcasted_iota(jnp.int32, (n, C, C), 2)
                              >= lax.broadcasted_iota(jnp.int32, (n, C, C), 1), 1.0, 0.0)
            dg = _tri_sum(upper, db) + extra
            df = dg / t["f"] - dk
            sg, sq = t["sg"], t["sq"]
            dq = da * t["eb"] * (HD ** -0.5) * (sq * (1.0 + qin * (1.0 - sq)))
            d_ref[0, :, cs] = dq.reshape(tr, HD).astype(BF16)
            d_ref[1, :, cs] = (df * (1.0 - lbh) * sg * (1.0 - sg)).reshape(tr, HD).astype(BF16)
            d_ref[2, :, cs] = dv.reshape(tr, HD).astype(BF16)
            dlb_ref[:, cs] += jnp.sum((df * (1.0 - sg)).reshape(tr, HD), axis=0, keepdims=True)

    rev = lambda off: pl.BlockSpec((tr, W), lambda g, s: (nrt - 1 - s, off + g))
    return pl.pallas_call(
        body, name="hgrn_bwd", grid=(HH // hb, nrt),
        in_specs=[rev(0), rev(0), rev(0), pl.BlockSpec((hb, n, HD, HD), lambda g, s: (g, nrt - 1 - s, 0, 0)),
                  rev(0), rev(0), rev(2 * EMIX // W), pl.BlockSpec((2, W), lambda g, s: (0, g))],
        out_specs=[pl.BlockSpec((3, tr, W), lambda g, s: (0, nrt - 1 - s, g)), pl.BlockSpec((1, W), lambda g, s: (0, g))],
        out_shape=[jax.ShapeDtypeStruct((3, S, EMIX), BF16), jax.ShapeDtypeStruct((1, EMIX), F32)],
        scratch_shapes=[pltpu.VMEM((hb, HD, HD), F32), pltpu.VMEM((hb, n, HD, HD), F32)],
        compiler_params=_params("parallel", "arbitrary"),
    )(dpremix, premix, rstd, states, proj, fgate, proj, hgrn_lb)


EW_BLOCK_ELEMS = 512 * 1024


def _ew_tiles(R, C):
    tc = C if C <= 4096 else _tile(C, 2048)
    tr = _tile(R, 512)
    while tr * tc > EW_BLOCK_ELEMS and tr % 16 == 0:
        tr //= 2
    return tr, tc


def _add_halves(name, core_chip, grad, got):
    _, _, R, C = grad.shape
    tr, tc = _ew_tiles(R, C)

    def body(c_ref, a_ref, b_ref, o_ref, own_ref):
        r = (a_ref[...].astype(F32) + b_ref[...].astype(F32)).astype(BF16)
        o_ref[...] = r

        @pl.when(pl.program_id(2) == c_ref[1])
        def _():
            own_ref[...] = r

    blk = pl.BlockSpec((None, tr, tc), lambda i, j, s, c: (s, i, j))
    sds = jax.ShapeDtypeStruct(got.shape, BF16)
    return pl.pallas_call(
        body, name=name, out_shape=[sds, sds],
        grid_spec=pltpu.PrefetchScalarGridSpec(
            num_scalar_prefetch=1, grid=(R // tr, C // tc, N_CHIPS),
            in_specs=[pl.BlockSpec((None, None, tr, tc), lambda i, j, s, c: (s, c[0], i, j)), blk],
            out_specs=[blk, pl.BlockSpec((None, tr, tc), lambda i, j, s, c: (c[1], i, j))]),
        compiler_params=_params("parallel", "parallel", "arbitrary"),
    )(core_chip, grad, got)


def _adam_step(w, g, m, v):
    mn = ADAM_B1 * m + (1.0 - ADAM_B1) * g
    vn = ADAM_B2 * v + (1.0 - ADAM_B2) * (g * g)
    m_hat = mn / (1.0 - ADAM_B1 ** ADAM_STEP)
    v_hat = vn / (1.0 - ADAM_B2 ** ADAM_STEP)
    return -ADAM_LR * (m_hat / (jnp.sqrt(v_hat) + ADAM_EPS) + ADAM_WD * w), mn, vn


def _adamw(name, w, g, m, v):
    R, C = w.shape
    tr, tc = _ew_tiles(R, C)

    def body(w_ref, g_ref, m_ref, v_ref, d_ref, mo_ref, vo_ref):
        d_ref[...], mo_ref[...], vo_ref[...] = _adam_step(w_ref[...], g_ref[...], m_ref[...], v_ref[...])

    blk = pl.BlockSpec((tr, tc), lambda i, j: (i, j))
    sds = jax.ShapeDtypeStruct((R, C), F32)
    return pl.pallas_call(
        body, name=name, grid=(R // tr, C // tc), in_specs=[blk] * 4, out_specs=[blk] * 3, out_shape=[sds] * 3,
        compiler_params=_params("parallel", "parallel"),
    )(w, g, m, v)


def _adamw_layers(name, w, gs, m, v):
    L, R, C = w.shape
    tr, tc = _ew_tiles(R, C)

    def body(*refs):
        w_ref, m_ref, v_ref = refs[:3]
        g_refs = refs[3:3 + L]
        go_ref, d_ref, mo_ref, vo_ref = refs[3 + L:]
        layer = pl.program_id(0)
        g = g_refs[0][...]
        for n in range(1, L):
            g = jnp.where(layer == n, g_refs[n][...], g)
        go_ref[...] = g
        d_ref[...], mo_ref[...], vo_ref[...] = _adam_step(w_ref[...], g, m_ref[...], v_ref[...])

    blk = pl.BlockSpec((None, tr, tc), lambda l, i, j: (l, i, j))
    of_layer = lambda n: pl.BlockSpec((tr, tc), lambda l, i, j: (jnp.where(l == n, i, 0), jnp.where(l == n, j, 0)))
    sds = jax.ShapeDtypeStruct((L, R, C), F32)
    return pl.pallas_call(
        body, name=name, grid=(L, R // tr, C // tc), in_specs=[blk] * 3 + [of_layer(n) for n in range(L)],
        out_specs=[blk] * 4, out_shape=[sds] * 4, compiler_params=_params("parallel", "parallel", "parallel"),
    )(w, m, v, *gs)


def _pack_rows(name, items, W):
    nv = len(items)
    first, row = [], 0
    for a, add, _ in items:
        first.append(row)
        row += 1 if add else a.shape[0]
    assert row <= SMALL_ROWS

    def body(*refs):
        o_ref = refs[nv]
        o_ref[...] = jnp.zeros_like(o_ref)
        for i, (a, add, width) in enumerate(items):
            val = refs[i][:, 0:width]
            if add:
                val = jnp.sum(val, axis=0, keepdims=True)
            o_ref[first[i]:first[i] + val.shape[0], 0:width] = val

    vm = pl.BlockSpec(memory_space=pltpu.VMEM)
    return pl.pallas_call(
        body, name=name, in_specs=[vm] * nv, out_specs=vm, out_shape=jax.ShapeDtypeStruct((SMALL_ROWS, W), F32),
    )(*[a for a, _, _ in items])


def _unpack_rows(name, pack, layout):
    def body(p_ref, *o_refs):
        row = 0
        for o_ref, (k, n) in zip(o_refs, layout):
            o_ref[...] = p_ref[row:row + k, 0:n]
            row += k

    vm = pl.BlockSpec(memory_space=pltpu.VMEM)
    return pl.pallas_call(
        body, name=name, in_specs=[vm], out_specs=[vm] * len(layout),
        out_shape=[jax.ShapeDtypeStruct(s, F32) for s in layout],
    )(pack)


def _small_sum(gathered, hgrn_lb, chip, nshard):
    _, T, W = gathered.shape

    def body(c_ref, g_ref, lb_ref, o_ref, loss_ref, tmp):
        acc = g_ref[0]
        for dev in range(1, N_DEV):
            acc = acc + g_ref[dev]
        tmp[...] = acc
        lb = _hgrn_lb(lb_ref)
        d1 = tmp[4:5, :] * (lb * (1.0 - lb))
        o_ref[...] = jnp.zeros_like(o_ref)
        o_ref[0:4, :] = tmp[0:4, :]
        o_ref[4:5, :] = -d1
        o_ref[5:6, :] = d1
        mine = tmp[5:6, 0:nshard]
        for b in range(1, N_CHIPS):
            mine = jnp.where(c_ref[0] == b, tmp[5:6, b * nshard:(b + 1) * nshard], mine)
        o_ref[6:7, 0:nshard] = mine
        o_ref[7:8, :] = tmp[6:7, :]
        loss_ref[...] = tmp[7:8, 0:128]

    vm = pl.BlockSpec(memory_space=pltpu.VMEM)
    return pl.pallas_call(
        body, name="small_sum", in_specs=[pl.BlockSpec(memory_space=pltpu.SMEM), vm, vm], out_specs=[vm, vm],
        out_shape=[jax.ShapeDtypeStruct((T, W), F32), jax.ShapeDtypeStruct((1, 128), F32)],
        scratch_shapes=[pltpu.VMEM((T, W), F32)],
    )(chip, gathered, hgrn_lb)


def _place():
    return lax.axis_index("x"), lax.axis_index("y"), lax.axis_index("c")


def _other_chips(x, y):
    return [(1 - x, y), (x, 1 - y), (1 - x, 1 - y)]


def _chunk_rows(rows, row_bytes):
    cr = rows
    while cr * row_bytes > STREAM_CHUNK_BYTES and cr % 32 == 0:
        cr //= 2
    return cr


def _stream(pairs, buf, sems, t, peer):
    lsem, ssem, rsem = sems
    n = len(pairs)
    loads, sent = [None] * n, [None] * n

    def load(k):
        slot = k % STREAM_SLOTS
        if k >= STREAM_SLOTS:
            sent[k - STREAM_SLOTS]()
        loads[k] = pltpu.make_async_copy(pairs[k][0], buf.at[slot], lsem.at[t, slot])
        loads[k].start()

    load(0)
    for k in range(n):
        slot = k % STREAM_SLOTS
        if k + 1 < n:
            load(k + 1)
        loads[k].wait()
        cp = pltpu.make_async_remote_copy(src_ref=buf.at[slot], dst_ref=pairs[k][1], send_sem=ssem.at[t, slot],
                                          recv_sem=rsem.at[t], device_id=peer, device_id_type=MESH)
        cp.start()
        sent[k] = cp.wait_send
    for k in range(max(0, n - STREAM_SLOTS), n):
        sent[k]()


def _stream_scratch(shapes):
    nt = len(shapes)
    return ([pltpu.VMEM((STREAM_SLOTS,) + s, d) for s, d in shapes]
            + [pltpu.SemaphoreType.DMA((nt, STREAM_SLOTS)), pltpu.SemaphoreType.DMA((nt, STREAM_SLOTS)),
               pltpu.SemaphoreType.DMA((nt,))])


def _exchange_halves(name, grads):
    nt = len(grads)
    hs = [g.shape[1] // 2 for g in grads]
    crs = [_chunk_rows(h, g.shape[2] * g.dtype.itemsize) for h, g in zip(hs, grads)]

    def body(*refs):
        ins, gots, bufs, sems = refs[:nt], refs[nt:2 * nt], refs[2 * nt:3 * nt], refs[3 * nt:]
        x, y, c = _place()
        sib = (x, y, 1 - c)
        for t in range(nt):
            h, cr = hs[t], crs[t]
            pairs = [(ins[t].at[b, pl.ds((1 - c) * h + r0, cr)], gots[t].at[b, pl.ds(r0, cr)])
                     for b in range(N_CHIPS) for r0 in range(0, h, cr)]
            _stream(pairs, bufs[t], sems, t, sib)
        for t in range(nt):
            pltpu.make_async_remote_copy(src_ref=gots[t], dst_ref=gots[t], send_sem=sems[1].at[t, 0],
                                         recv_sem=sems[2].at[t], device_id=sib, device_id_type=MESH).wait_recv()

    return pl.pallas_call(
        body, name=name, in_specs=[ANY] * nt, out_specs=[ANY] * nt,
        out_shape=[jax.ShapeDtypeStruct((N_CHIPS, h, g.shape[2]), g.dtype) for h, g in zip(hs, grads)],
        scratch_shapes=_stream_scratch([((cr, g.shape[2]), g.dtype) for cr, g in zip(crs, grads)]),
        compiler_params=pltpu.CompilerParams(vmem_limit_bytes=VMEM_LIMIT_BYTES),
    )(*grads)


def _scatter_plan(srcs, dsts):
    x, y, c = _place()
    me = 2 * x + y
    return [(srcs[t].at[2 * px + py], dsts[t].at[me], (px, py, c))
            for t in range(len(srcs)) for px, py in _other_chips(x, y)]


def _slot(dst, chip, r0, rows, cols):
    if len(dst.shape) == 3:
        return dst.at[chip, pl.ds(r0, rows)]
    return dst.at[pl.ds(r0, rows), pl.ds(pl.multiple_of(chip * cols, 128), cols)]


def _shard_dims(gathered):
    s = gathered.shape
    return (s[1], s[2]) if len(s) == 3 else (s[0], s[1] // N_CHIPS)


def _gather_plan(_, bufs):
    x, y, c = _place()
    me = 2 * x + y
    plan = []
    for buf in bufs:
        rows, cols = _shard_dims(buf)
        mine = _slot(buf, me, c * (rows // 2), rows // 2, cols)
        plan += [(mine, mine, (px, py, c)) for px, py in _other_chips(x, y)]
    return plan


def _cast_to_slot(name, chip, w, layer, gathered_shape, after):
    _, R, C = w.shape
    tr, tc = _ew_tiles(R, C)

    def body(c_ref, w_ref, *rest):
        rest[-1][...] = w_ref[...].astype(BF16)

    if len(gathered_shape) == 3:
        out_spec = pl.BlockSpec((None, tr, tc), lambda i, j, c: (c[0], i, j))
    else:
        out_spec = pl.BlockSpec((tr, tc), lambda i, j, c: (i, c[0] * (C // tc) + j))
    extra = [] if after is None else [after]
    return pl.pallas_call(
        body, name=name, out_shape=jax.ShapeDtypeStruct(gathered_shape, BF16),
        grid_spec=pltpu.PrefetchScalarGridSpec(
            num_scalar_prefetch=1, grid=(R // tr, C // tc),
            in_specs=[pl.BlockSpec((None, tr, tc), lambda i, j, c: (layer, i, j))] + [ANY] * len(extra),
            out_specs=out_spec),
        compiler_params=_params("parallel", "parallel"),
    )(chip, w, *extra)


HBM_SPEC = pl.BlockSpec(memory_space=pltpu.HBM)
SEM_SPEC = pl.BlockSpec(memory_space=pltpu.SEMAPHORE)


def _split_start(name, srcs, dsts, plan, ncopies, after):
    bufs = [pltpu.with_memory_space_constraint(a, pltpu.HBM) for a in list(srcs) + list(dsts)]
    nb, ns = len(bufs), len(srcs)
    operands = bufs + ([after] if after is not None else [])

    def body(*refs):
        outs = refs[len(operands):]
        send, recv, token = outs[0], outs[1], outs[-1]
        for i, (src, dst, dev) in enumerate(plan(refs[:ns], refs[ns:nb])):
            pltpu.make_async_remote_copy(src_ref=src, dst_ref=dst, send_sem=send.at[i], recv_sem=recv.at[i],
                                         device_id=dev, device_id_type=MESH).start()
        token[...] = jnp.zeros_like(token)

    res = pl.pallas_call(
        body, name=name,
        out_shape=[pltpu.SemaphoreType.DMA((ncopies,)), pltpu.SemaphoreType.DMA((ncopies,))]
        + [pltpu.HBM(a.shape, a.dtype) for a in bufs] + [jax.ShapeDtypeStruct((8, 128), F32)],
        in_specs=[HBM_SPEC] * nb + [ANY] * (len(operands) - nb),
        out_specs=[SEM_SPEC, SEM_SPEC] + [HBM_SPEC] * nb + [pl.BlockSpec(memory_space=pltpu.VMEM)],
        input_output_aliases={i: 2 + i for i in range(nb)},
        compiler_params=pltpu.CompilerParams(has_side_effects=pltpu.SideEffectType.DATAFLOW_SIDE_EFFECTING),
    )(*operands)
    return res[:-1], res[-1]


def _split_wait(name, started, plan, ns, after):
    send, recv, bufs = started[0], started[1], list(started[2:])
    nb = len(bufs)

    def body(*refs):
        send_ref, recv_ref = refs[nb], refs[nb + 1]
        for i, (src, dst, dev) in enumerate(plan(refs[:ns], refs[ns:nb])):
            cp = pltpu.make_async_remote_copy(src_ref=src, dst_ref=dst, send_sem=send_ref.at[i], recv_sem=recv_ref.at[i],
                                              device_id=dev, device_id_type=MESH)
            cp.wait_send()
            cp.wait_recv()

    res = pl.pallas_call(
        body, name=name, out_shape=[pltpu.HBM(a.shape, a.dtype) for a in bufs],
        in_specs=[HBM_SPEC] * nb + [SEM_SPEC, SEM_SPEC, ANY], out_specs=[HBM_SPEC] * nb,
        input_output_aliases={i: i for i in range(nb)},
        compiler_params=pltpu.CompilerParams(has_side_effects=pltpu.SideEffectType.DATAFLOW_SIDE_EFFECTING),
    )(*bufs, send, recv, after)
    return res[:ns], res[ns:]


def _gather_finish(name, _, gathered):
    nt = len(gathered)
    dims = [_shard_dims(g) for g in gathered]
    hs = [rows // 2 for rows, _ in dims]
    crs = [_chunk_rows(h, cols * 2) for h, (_, cols) in zip(hs, dims)]

    def body(*refs):
        outs, bufs, sems = refs[nt:2 * nt], refs[2 * nt:3 * nt], refs[3 * nt:]
        x, y, c = _place()
        sib = (x, y, 1 - c)
        for t in range(nt):
            h, cr, cols = hs[t], crs[t], dims[t][1]
            passed = [_slot(outs[t], 2 * px + py, c * h + r0, cr, cols)
                      for px, py in _other_chips(x, y) for r0 in range(0, h, cr)]
            _stream([(r, r) for r in passed], bufs[t], sems, t, sib)
        for t in range(nt):
            if len(gathered[t].shape) == 3:
                three = outs[t].at[pl.ds(0, 3), pl.ds(0, hs[t])]
            else:
                three = outs[t].at[pl.ds(0, hs[t]), pl.ds(0, 3 * dims[t][1])]
            pltpu.make_async_remote_copy(src_ref=three, dst_ref=three, send_sem=sems[1].at[t, 0],
                                         recv_sem=sems[2].at[t], device_id=sib, device_id_type=MESH).wait_recv()

    return pl.pallas_call(
        body, name=name, in_specs=[ANY] * nt, out_specs=[ANY] * nt,
        out_shape=[jax.ShapeDtypeStruct(g.shape, g.dtype) for g in gathered],
        scratch_shapes=_stream_scratch([((cr, cols), BF16) for cr, (_, cols) in zip(crs, dims)]),
        input_output_aliases={t: t for t in range(nt)},
        compiler_params=pltpu.CompilerParams(vmem_limit_bytes=VMEM_LIMIT_BYTES),
    )(*gathered)


def _sum_share(landed):
    nt = len(landed)
    hs = [a.shape[1] for a in landed]
    cs = [a.shape[2] for a in landed]
    crs = [_chunk_rows(h, 2 * c * 4) for h, c in zip(hs, cs)]
    shapes = sorted(set(zip(crs, cs)))
    which = [shapes.index(s) for s in zip(crs, cs)]

    def body(*refs):
        ins, outs = refs[:nt], refs[nt:2 * nt]
        inbufs, outbufs = refs[2 * nt:2 * nt + len(shapes)], refs[2 * nt + len(shapes):2 * nt + 2 * len(shapes)]
        lsem, ssem, osem, rsem = refs[2 * nt + 2 * len(shapes):]
        x, y, c = _place()
        sib = (x, y, 1 - c)
        for t in range(nt):
            cr, n, ib, ob = crs[t], hs[t] // crs[t], inbufs[which[t]], outbufs[which[t]]
            loads, gone = [None] * n, [None] * n

            def load(k):
                slot = k % SUM_SLOTS
                if k >= SUM_SLOTS:
                    for cp_wait in gone[k - SUM_SLOTS]:
                        cp_wait()
                loads[k] = pltpu.make_async_copy(ins[t].at[:, pl.ds(k * cr, cr)], ib.at[slot], lsem.at[t, slot])
                loads[k].start()

            load(0)
            for k in range(n):
                slot = k % SUM_SLOTS
                if k + 1 < n:
                    load(k + 1)
                loads[k].wait()
                acc = ib[slot, 0].astype(F32)
                for s in range(1, N_CHIPS):
                    acc = acc + ib[slot, s].astype(F32)
                ob[slot] = acc
                rows = outs[t].at[c, pl.ds(k * cr, cr)]
                away = pltpu.make_async_remote_copy(src_ref=ob.at[slot], dst_ref=rows, send_sem=ssem.at[t, slot],
                                                    recv_sem=rsem.at[t], device_id=sib, device_id_type=MESH)
                away.start()
                home = pltpu.make_async_copy(ob.at[slot], rows, osem.at[t, slot])
                home.start()
                gone[k] = (away.wait_send, home.wait)
            for k in range(max(0, n - SUM_SLOTS), n):
                for cp_wait in gone[k]:
                    cp_wait()
        for t in range(nt):
            other = outs[t].at[1 - c]
            pltpu.make_async_remote_copy(src_ref=other, dst_ref=other, send_sem=ssem.at[t, 0], recv_sem=rsem.at[t],
                                         device_id=sib, device_id_type=MESH).wait_recv()

    slot_sems = pltpu.SemaphoreType.DMA((nt, SUM_SLOTS))
    return pl.pallas_call(
        body, name="sum_share", in_specs=[ANY] * nt, out_specs=[ANY] * nt,
        out_shape=[jax.ShapeDtypeStruct((2, h, c), F32) for h, c in zip(hs, cs)],
        scratch_shapes=[pltpu.VMEM((SUM_SLOTS, N_CHIPS, cr, c), BF16) for cr, c in shapes]
        + [pltpu.VMEM((SUM_SLOTS, cr, c), F32) for cr, c in shapes]
        + [slot_sems, slot_sems, slot_sems, pltpu.SemaphoreType.DMA((nt,))],
        compiler_params=pltpu.CompilerParams(vmem_limit_bytes=VMEM_LIMIT_BYTES),
    )(*landed)


def _allgather_small(name, v):
    def body(v_ref, o_ref, send, recv, lsem):
        x, y, c = _place()
        me = 4 * x + 2 * y + c
        loc = pltpu.make_async_copy(v_ref, o_ref.at[me], lsem)
        loc.start()
        copies = []
        for k in range(1, N_DEV):
            px = 1 - x if k & 4 else x
            py = 1 - y if k & 2 else y
            pc = 1 - c if k & 1 else c
            cp = pltpu.make_async_remote_copy(
                src_ref=v_ref, dst_ref=o_ref.at[me], send_sem=send.at[k - 1], recv_sem=recv.at[k - 1],
                device_id=(px, py, pc), device_id_type=MESH)
            cp.start()
            copies.append(cp)
        for cp in copies:
            cp.wait()
        loc.wait()

    vm = pl.BlockSpec(memory_space=pltpu.VMEM)
    return pl.pallas_call(
        body, name=name, in_specs=[vm], out_specs=vm,
        out_shape=jax.ShapeDtypeStruct((N_DEV,) + v.shape, v.dtype),
        scratch_shapes=[pltpu.SemaphoreType.DMA((N_DEV - 1,))] * 2 + [pltpu.SemaphoreType.DMA],
    )(v)


def kernel(x, mem, norm_g, mem_norm_g, w_kv, w_out, pool_w_in, pool_w_grp, pool_scale, hgrn_w_in, hgrn_lb, hgrn_norm_g, final_g, loss_target, m_norm_g, m_mem_norm_g, m_w_kv, m_w_out, m_pool_w_in, m_pool_w_grp, m_pool_scale, m_hgrn_w_in, m_hgrn_lb, m_hgrn_norm_g, m_final_g, v_norm_g, v_mem_norm_g, v_w_kv, v_w_out, v_pool_w_in, v_pool_w_grp, v_pool_scale, v_hgrn_w_in, v_hgrn_lb, v_hgrn_norm_g, v_final_g):
    _, S, D = x.shape
    M = mem.shape[1]
    EB = 2 * D
    ECA = EB // 4
    EMIX = EB - ECA
    PG = EMIX // N_POOL_GROUPS
    NP0 = EMIX + ECA + EB
    NP1 = 3 * EMIX + ECA + EB
    SH0, SH1 = NP0 // N_CHIPS, NP1 // N_CHIPS
    DK, EK = D // N_CHIPS, EB // N_CHIPS
    TNP = 512 if all(v % 512 == 0 for v in (SH0, SH1, ECA, EMIX)) else 256
    TM = _tile(S, 1024)
    TMF = _tile(S, 2048)
    TD = _tile(D, 512)
    TDW = _tile(D, 1024)
    c0, c1 = SH0 // TNP, SH1 // TNP
    qt = EMIX // TNP
    chip = 2 * lax.axis_index("x") + lax.axis_index("y")

    xs, ms, tgt = x[0], mem[0], loss_target[0]

    sds = jax.ShapeDtypeStruct
    chip1 = chip.astype(jnp.int32).reshape(1)

    def start_gather(tag, layers, after):
        bufs = []
        for t, (w, layer) in enumerate(layers):
            w3 = w.reshape((w.shape[0], -1, w.shape[-1]))
            shape = (D, NP1) if w is hgrn_w_in else (N_CHIPS,) + w3.shape[1:]
            bufs.append(_cast_to_slot(f"cast_{tag}{t}", chip1, w3, layer, shape, after))
        return _split_start(f"gather_{tag}_start", [], bufs, _gather_plan, 3 * len(bufs), after)

    gather_a, token = start_gather("a", [(pool_w_in, 0)], None)
    gather_b, token = start_gather("b", [(w_kv, 0), (w_out, 0), (pool_w_grp, 0)], token)
    gather_c, token = start_gather("c", [(hgrn_w_in, 0)], token)
    gather_d, token = start_gather("d", [(w_kv, 1), (w_out, 1)], token)

    tek, tew = _tile(EK, 512), _tile(EK, 1024)

    mem_n = _rms_fwd("rms_mem", ms, mem_norm_g.reshape(1, D), token)
    h0 = _rms_fwd("rms0", xs, norm_g[0:1], token)
    wpin, = _gather_finish("gather_a_finish", *_split_wait("gather_a_wait", gather_a, _gather_plan, 0, h0))

    tkw = _tile(2 * ECA, 1024)

    def kv_of(layer, wkv):
        return _matmul(
            f"kv{layer}", mem_n, wkv.reshape(D, 2 * ECA), grid=(1, 2 * ECA // tkw, 1),
            a_spec=pl.BlockSpec((M, D), lambda i, j, k: (0, 0)), b_spec=pl.BlockSpec((D, tkw), lambda i, j, k: (0, j)),
            out_shape=sds((M, 2 * ECA), BF16), out_spec=pl.BlockSpec((M, tkw), lambda i, j, k: (0, j)),
            acc_shape=(M, tkw), dims=NN)

    tko = _tile(EB, 2048)

    def out_proj(layer, branch, wout, resid):
        return _matmul(
            f"out_proj{layer}", branch, wout.reshape(EB, D), grid=(S // TM, D // TDW, EB // tko),
            a_spec=pl.BlockSpec((TM, tko), IK), b_spec=pl.BlockSpec((tko, TDW), KJ),
            out_shape=sds((S, D), F32), out_spec=pl.BlockSpec((TM, TDW), IJ),
            acc_shape=(TM, TDW), dims=NN, add=resid, add_spec=pl.BlockSpec((TM, TDW), IJ))

    ones_ca = jnp.ones((1, ECA), F32)

    proj0 = _matmul(
        "proj0", h0, wpin, grid=(S // TMF, NP0 // TNP, 1),
        a_spec=pl.BlockSpec((TMF, D), lambda i, j, k: (i, 0)),
        b_spec=pl.BlockSpec((None, D, TNP), lambda i, j, k: (j // c0, 0, j % c0)),
        out_shape=sds((S, NP0), BF16), out_spec=pl.BlockSpec((TMF, TNP), IJ),
        acc_shape=(TMF, TNP), dims=NN)
    pooled = _pool_fwd(proj0, S, EMIX)
    wkv0, wout0, g_grp = _gather_finish("gather_b_finish", *_split_wait("gather_b_wait", gather_b, _gather_plan, 0, pooled))
    wgrp = g_grp.reshape(N_CHIPS, N_POOL_GROUPS, PG // N_CHIPS, PG).transpose(1, 0, 2, 3).reshape(N_POOL_GROUPS, PG, PG)
    kv = [kv_of(0, wkv0), None]
    premix0 = _matmul(
        "pool_grp", pooled, wgrp, grid=(S // TM, N_POOL_GROUPS, 1),
        a_spec=pl.BlockSpec((TM, PG), lambda i, j, k: (i, j)),
        b_spec=pl.BlockSpec((None, PG, PG), lambda i, j, k: (j, 0, 0)),
        out_shape=sds((S, EB), BF16), out_spec=pl.BlockSpec((TM, PG), lambda i, j, k: (i, j)),
        acc_shape=(TM, PG), dims=NN)
    premix0 = _ca_fwd("ca_fwd0", proj0, EMIX // ECA, kv[0], premix0, S, ECA, EMIX)
    colscale0 = jnp.concatenate([pool_scale.reshape(1, EMIX), ones_ca], axis=1)
    gblk0 = (EMIX + ECA) // ECA
    branch0 = _gate_fwd("gate_fwd0", premix0, proj0, gblk0, colscale0, S, EB, ECA)
    x1 = out_proj(0, branch0, wout0, xs)

    whin, = _gather_finish("gather_c_finish", *_split_wait("gather_c_wait", gather_c, _gather_plan, 0, x1))
    h1 = _rms_fwd("rms1", x1, norm_g[1:2])

    def proj1_cols(name, ncols, col_of, out_cols, out_dtype, out_col_of):
        return _matmul(
            name, h1, whin, grid=(S // TMF, ncols, 1),
            a_spec=pl.BlockSpec((TMF, D), lambda i, j, k: (i, 0)),
            b_spec=pl.BlockSpec((D, TNP), lambda i, j, k: (0, col_of(j))),
            out_shape=sds((S, out_cols), out_dtype), out_spec=pl.BlockSpec((TMF, TNP), lambda i, j, k: (i, out_col_of(j))),
            acc_shape=(TMF, TNP), dims=NN)

    skip_f = lambda j: jnp.where(j < qt, j, j + qt)
    proj1 = proj1_cols("proj1", NP1 // TNP - qt, skip_f, NP1, BF16, skip_f)
    fgate = proj1_cols("proj1_f", qt, lambda j: j + qt, EMIX, F32, lambda j: j)
    premix1, rstd1, states = _hgrn_fwd(proj1, fgate, hgrn_lb, S, EMIX, EB)
    wkv1, wout1 = _gather_finish("gather_d_finish", *_split_wait("gather_d_wait", gather_d, _gather_plan, 0, rstd1))
    kv[1] = kv_of(1, wkv1)
    premix1 = _ca_fwd("ca_fwd1", proj1, 3 * EMIX // ECA, kv[1], premix1, S, ECA, EMIX)
    norm_tiles = _allgather_small("allgather_norm_g", jnp.pad(hgrn_norm_g, ((0, SMALL_ROWS - 1), (0, 0))))
    hg_norm = norm_tiles[0::2, 0, :].reshape(1, EMIX)
    colscale1 = jnp.concatenate([hg_norm, ones_ca], axis=1)
    gblk1 = (3 * EMIX + ECA) // ECA
    branch1 = _gate_fwd("gate_fwd1", premix1, proj1, gblk1, colscale1, S, EB, ECA)
    x2 = out_proj(1, branch1, wout1, x1)

    dx2, dx2b, d_final_g, loss_part = _loss_head(x2, final_g.reshape(1, D), tgt)

    def out_proj_bwd(layer, dxb, branch, wout, premix, proj, gblk, colscale, dshape, dblk):
        goff, doff = gblk * ECA // tek, dblk * ECA // tek
        dpremix, dgate, dcol = _matmul(
            f"dbranch{layer}", dxb, wout, grid=(S // TMF, EB // tek, 1),
            a_spec=pl.BlockSpec((TMF, D), lambda i, j, k: (i, 0)),
            b_spec=pl.BlockSpec((None, tek, D), lambda i, j, k: (j // (EK // tek), j % (EK // tek), 0)),
            extras=[(premix, pl.BlockSpec((TMF, tek), IJ)), (proj, pl.BlockSpec((TMF, tek), lambda i, j, k: (i, goff + j))),
                    (colscale, pl.BlockSpec((1, tek), lambda i, j, k: (0, j)))],
            epilogue=_gate_bwd_epilogue,
            out_shape=[sds((S, EB), BF16), sds(dshape, BF16), sds((S // TMF, 1, EB), F32)],
            out_spec=[pl.BlockSpec((TMF, tek), IJ), pl.BlockSpec((TMF, tek), lambda i, j, k: (i, doff + j)),
                      pl.BlockSpec((None, 1, tek), lambda i, j, k: (i, 0, j))],
            acc_shape=(TMF, tek), dims=NT)
        dw = _matmul(
            f"dwout{layer}", branch, dxb, grid=(EB // tew, D // TD, 1),
            a_spec=pl.BlockSpec((S, tew), lambda i, j, k: (0, i)), b_spec=pl.BlockSpec((S, TD), lambda i, j, k: (0, j)),
            out_shape=sds((N_CHIPS, EK, D), BF16),
            out_spec=pl.BlockSpec((None, tew, TD), lambda i, j, k: (i // (EK // tew), i % (EK // tew), j)),
            acc_shape=(tew, TD), dims=TN)
        return dpremix, dgate, dcol.reshape(S // TMF, EB), dw

    def kv_bwd(layer, dkv, wkv, dmem_add):
        dkvb = dkv.astype(BF16)
        dmem = _matmul(
            f"dmem{layer}", dkvb, wkv.reshape(D, 2 * ECA), grid=(1, D // TDW, 1),
            a_spec=pl.BlockSpec((M, 2 * ECA), lambda i, j, k: (0, 0)),
            b_spec=pl.BlockSpec((TDW, 2 * ECA), lambda i, j, k: (j, 0)),
            out_shape=sds((M, D), F32), out_spec=pl.BlockSpec((M, TDW), lambda i, j, k: (0, j)), acc_shape=(M, TDW),
            dims=NT, add=dmem_add, add_spec=pl.BlockSpec((M, TDW), lambda i, j, k: (0, j)))
        dw = _matmul(
            f"dwkv{layer}", mem_n, dkvb, grid=(D // TDW, 2 * ECA // tkw, 1),
            a_spec=pl.BlockSpec((M, TDW), lambda i, j, k: (0, i)), b_spec=pl.BlockSpec((M, tkw), lambda i, j, k: (0, j)),
            out_shape=sds((D, 2 * ECA), BF16), out_spec=pl.BlockSpec((TDW, tkw), IJ), acc_shape=(TDW, tkw), dims=TN)
        return dmem, dw.reshape(N_CHIPS, DK, 2 * ECA)

    dpremix1, drest1, dcol1, gw_out1 = out_proj_bwd(1, dx2b, branch1, wout1, premix1, proj1, gblk1, colscale1,
                                                    (S, ECA + EB), 1)
    drest1, dkv1 = _ca_bwd("ca_bwd1", dpremix1, proj1, 3 * EMIX // ECA, kv[1], drest1, 0, S, ECA, EMIX)
    dqfi, dlb = _hgrn_bwd(dpremix1, premix1, rstd1, states, proj1, fgate, hgrn_lb, S, EMIX)
    nq, nr = 3 * qt, (ECA + EB) // TNP
    tkh = _tile(EMIX, 1024) if (ECA + EB) % _tile(EMIX, 1024) == 0 else TNP
    kq = EMIX // tkh
    dh1 = _matmul(
        "dh1_qfi", dqfi, whin, grid=(S // TM, D // TDW, 3),
        a_spec=pl.BlockSpec((None, TM, EMIX), lambda i, j, k: (k, i, 0)),
        b_spec=pl.BlockSpec((TDW, EMIX), lambda i, j, k: (j, k)),
        out_shape=sds((S, D), F32), out_spec=pl.BlockSpec((TM, TDW), IJ), acc_shape=(TM, TDW), dims=NT)
    dh1 = _matmul(
        "dh1_rest", drest1, whin, grid=(S // TM, D // TDW, (ECA + EB) // tkh), a_spec=pl.BlockSpec((TM, tkh), IK),
        b_spec=pl.BlockSpec((TDW, tkh), lambda i, j, k: (j, k + 3 * kq)),
        out_shape=sds((S, D), F32), out_spec=pl.BlockSpec((TM, TDW), IJ), acc_shape=(TM, TDW), dims=NT,
        add=dh1, add_spec=pl.BlockSpec((TM, TDW), IJ))
    gw_hin = _matmul(
        "dwhin_qfi", h1, dqfi, grid=(D // TDW, nq, 1), a_spec=pl.BlockSpec((S, TDW), lambda i, j, k: (0, i)),
        b_spec=pl.BlockSpec((None, S, TNP), lambda i, j, k: (j // qt, 0, j % qt)),
        out_shape=sds((N_CHIPS, D, SH1), BF16), out_spec=pl.BlockSpec((None, TDW, TNP), lambda i, j, k: (j // c1, i, j % c1)),
        acc_shape=(TDW, TNP), dims=TN)
    gw_hin = _matmul(
        "dwhin_rest", h1, drest1, grid=(D // TDW, nr, 1), a_spec=pl.BlockSpec((S, TDW), lambda i, j, k: (0, i)),
        b_spec=pl.BlockSpec((S, TNP), lambda i, j, k: (0, j)), out_shape=sds((N_CHIPS, D, SH1), BF16),
        out_spec=pl.BlockSpec((None, TDW, TNP), lambda i, j, k: ((j + nq) // c1, i, (j + nq) % c1)),
        acc_shape=(TDW, TNP), dims=TN, alias=gw_hin)
    dmem, gw_kv1 = kv_bwd(1, dkv1, wkv1, None)

    core_chip = jnp.stack([lax.axis_index("c"), chip]).astype(jnp.int32)

    def reduce_in_chip(tag, stacks):
        got = _exchange_halves(f"exchange_halves{tag}", stacks)
        pairs = [_add_halves(f"add_halves{tag}_{t}", core_chip, g.reshape(N_CHIPS, 2, g.shape[1] // 2, g.shape[2]), r)
                 for t, (g, r) in enumerate(zip(stacks, got))]
        return [p for p, _ in pairs], [own for _, own in pairs]

    parts1, landed1 = reduce_in_chip(1, [gw_kv1, gw_out1, gw_hin])
    scatter1, token1 = _split_start("scatter1_start", parts1, landed1, _scatter_plan, 3 * len(parts1), None)
    dx1, dx1b, d_ng1 = _rms_bwd("rms_bwd1", dh1, x1, norm_g[1:2], dx2, token1)

    dpremix0, dproj0, dcol0, gw_out0 = out_proj_bwd(0, dx1b, branch0, wout0, premix0, proj0, gblk0, colscale0,
                                                    (S, NP0), gblk0)
    dproj0, dkv0 = _ca_bwd("ca_bwd0", dpremix0, proj0, EMIX // ECA, kv[0], dproj0, EMIX // ECA, S, ECA, EMIX)
    dmem, gw_kv0 = kv_bwd(0, dkv0, wkv0, dmem)
    parts_a, landed_a = reduce_in_chip("0a", [gw_kv0, gw_out0])
    scatter_a, token_a = _split_start("scatter0a_start", parts_a, landed_a, _scatter_plan, 3 * len(parts_a), None)
    dpooled = _matmul(
        "dpooled", dpremix0, wgrp, grid=(S // TM, N_POOL_GROUPS, 1), a_spec=pl.BlockSpec((TM, PG), IJ),
        b_spec=pl.BlockSpec((None, PG, PG), lambda i, j, k: (j, 0, 0)),
        out_shape=sds((S, EMIX), F32), out_spec=pl.BlockSpec((TM, PG), IJ), acc_shape=(TM, PG), dims=NT, after=token_a)
    def rows_by_chip(r, _, outs):
        outs[0][...] = r.reshape(N_CHIPS, PG // N_CHIPS, PG).astype(BF16)

    gw_grp, = _matmul(
        "dwgrp", pooled, dpremix0, grid=(N_POOL_GROUPS, 1, 1), a_spec=pl.BlockSpec((S, PG), lambda i, j, k: (0, i)),
        b_spec=pl.BlockSpec((S, PG), lambda i, j, k: (0, i)), epilogue=rows_by_chip,
        out_shape=[sds((N_CHIPS, N_POOL_GROUPS, PG // N_CHIPS, PG), BF16)],
        out_spec=[pl.BlockSpec((N_CHIPS, None, PG // N_CHIPS, PG), lambda i, j, k: (0, i, 0, 0))],
        acc_shape=(PG, PG), dims=TN)
    dproj0 = _pool_bwd(dpooled, dproj0, S, EMIX)
    gw_pin = _matmul(
        "dwpin", h0, dproj0, grid=(D // TDW, NP0 // TNP, 1), a_spec=pl.BlockSpec((S, TDW), lambda i, j, k: (0, i)),
        b_spec=pl.BlockSpec((S, TNP), lambda i, j, k: (0, j)), out_shape=sds((N_CHIPS, D, SH0), BF16),
        out_spec=pl.BlockSpec((None, TDW, TNP), lambda i, j, k: (j // c0, i, j % c0)), acc_shape=(TDW, TNP), dims=TN)
    parts_b, landed_b = reduce_in_chip("0b", [gw_pin, gw_grp.reshape(N_CHIPS, PG, PG)])
    scatter_b, token_b = _split_start("scatter0b_start", parts_b, landed_b, _scatter_plan, 3 * len(parts_b), None)
    dh0 = _matmul(
        "dh0", dproj0, wpin, grid=(S // TM, D // TDW, N_CHIPS), a_spec=pl.BlockSpec((TM, SH0), IK),
        b_spec=pl.BlockSpec((None, TDW, SH0), lambda i, j, k: (k, j, 0)),
        out_shape=sds((S, D), F32), out_spec=pl.BlockSpec((TM, TDW), IJ), acc_shape=(TM, TDW), dims=NT, after=token_b)
    grad_x, _, d_ng0 = _rms_bwd("rms_bwd0", dh0, xs, norm_g[0:1], dx1)
    _, _, d_mng = _rms_bwd("rms_bwd_mem", dmem, ms, mem_norm_g.reshape(1, D), jnp.zeros_like(ms))

    _, landed1 = _split_wait("scatter1_wait", scatter1, _scatter_plan, len(parts1), grad_x)
    _, landed_a = _split_wait("scatter0a_wait", scatter_a, _scatter_plan, len(parts_a), grad_x)
    _, landed_b = _split_wait("scatter0b_wait", scatter_b, _scatter_plan, len(parts_b), grad_x)
    landed = [landed_a[0], landed1[0], landed_a[1], landed1[1], landed_b[0], landed_b[1], landed1[2]]
    fulls = _sum_share(landed)
    f2 = [f.reshape(-1, f.shape[-1]) for f in fulls]
    grads, deltas, new_m, new_v = {}, {}, {}, {}
    for n, w, mm, vv, gs in (("w_kv", w_kv, m_w_kv, v_w_kv, f2[0:2]), ("w_out", w_out, m_w_out, v_w_out, f2[2:4]),
                             ("pool_w_in", pool_w_in, m_pool_w_in, v_pool_w_in, f2[4:5]),
                             ("pool_w_grp", pool_w_grp, m_pool_w_grp, v_pool_w_grp, f2[5:6]),
                             ("hgrn_w_in", hgrn_w_in, m_hgrn_w_in, v_hgrn_w_in, f2[6:7])):
        as3d = lambda a: a.reshape((a.shape[0], -1, a.shape[-1]))
        outs = _adamw_layers(f"adamw_{n}", as3d(w), gs, as3d(mm), as3d(vv))
        grads[n], deltas[n], new_m[n], new_v[n] = [o.reshape(w.shape) for o in outs]

    Wd = EMIX
    nshard = EMIX // N_CHIPS
    summed_rows = [(v, True, v.shape[1]) for v in (d_ng0, d_ng1, d_mng)] + [
        (dcol0, True, EMIX), (dlb, True, EMIX), (dcol1, True, EMIX), (d_final_g, True, D), (loss_part, True, 128)]
    partial = _pack_rows("pack_partials", summed_rows, Wd)
    g_pack, loss = _small_sum(_allgather_small("allgather_grads", partial), hgrn_lb, chip1, nshard)

    def pack_small(name, ng, mng, ps, lb_, hn, fg):
        return _pack_rows(name, [(ng, False, D), (mng.reshape(1, D), False, D), (ps, False, EMIX), (lb_, False, EMIX),
                                 (hn, False, nshard), (fg.reshape(1, D), False, D)], Wd)

    d_pack, m_pack, v_pack = _adamw(
        "adamw_small", pack_small("pack_small_w", norm_g, mem_norm_g, pool_scale, hgrn_lb, hgrn_norm_g, final_g), g_pack,
        pack_small("pack_small_m", m_norm_g, m_mem_norm_g, m_pool_scale, m_hgrn_lb, m_hgrn_norm_g, m_final_g),
        pack_small("pack_small_v", v_norm_g, v_mem_norm_g, v_pool_scale, v_hgrn_lb, v_hgrn_norm_g, v_final_g))
    layout = [(2, D), (1, D), (1, EMIX), (2, EMIX), (1, nshard), (1, D)]
    for tag, pack, out in (("g", g_pack, grads), ("d", d_pack, deltas), ("m", m_pack, new_m), ("v", v_pack, new_v)):
        ng, mng, ps, lb_, hn, fg = _unpack_rows(f"unpack_small_{tag}", pack, layout)
        out.update(norm_g=ng, mem_norm_g=mng.reshape(D), pool_scale=ps, hgrn_lb=lb_, hgrn_norm_g=hn, final_g=fg.reshape(D))
    loss = loss[0, 0]

    order = ["norm_g", "mem_norm_g", "w_kv", "w_out", "pool_w_in", "pool_w_grp", "pool_scale", "hgrn_w_in", "hgrn_lb",
             "hgrn_norm_g", "final_g"]
    return (loss, grad_x.reshape(1, S, D), *[grads[n] for n in order], *[deltas[n] for n in order],
            *[new_m[n] for n in order], *[new_v[n] for n in order])
```

```python
import functools

import jax
import jax.numpy as jnp
from jax import lax
from jax.experimental import pallas as pl
from jax.experimental.pallas import tpu as pltpu

F32 = jnp.float32
BF16 = jnp.bfloat16
MESH = pl.DeviceIdType.MESH
ANY = pl.BlockSpec(memory_space=pl.ANY)

EPS = 1e-6
HG_HEAD_DIM = 128
HG_CHUNK = 64
CA_HEADS = 4
N_POOL_GROUPS = 4
POOL_HALO = 128
ADAM_LR = 0.001
ADAM_B1 = 0.9
ADAM_B2 = 0.999
ADAM_EPS = 1e-08
ADAM_WD = 0.01
ADAM_STEP = 10
N_CHIPS = 4
N_DEV = 8
VMEM_LIMIT_BYTES = 56 * 1024 * 1024
SMALL_ROWS = 8
STREAM_CHUNK_BYTES = 4 * 1024 * 1024
STREAM_SLOTS = 3
SUM_SLOTS = 2
RING_SLOTS = 3


def _params(*sem):
    return pltpu.CompilerParams(dimension_semantics=sem, vmem_limit_bytes=VMEM_LIMIT_BYTES)


def _tile(n, pref):
    t = pref
    while n % t:
        t //= 2
    return t


def _sigmoid(x):
    return 1.0 / (1.0 + jnp.exp(-x))


def _matmul(name, a, b, *, grid, a_spec, b_spec, out_shape, out_spec, acc_shape, dims,
            add=None, add_spec=None, alias=None, after=None, extras=(), epilogue=None):
    nk = grid[2]
    has_add = add is not None
    has_alias = alias is not None
    has_after = after is not None
    n_out = len(out_shape) if epilogue is not None else 1

    def body(*refs):
        a_ref, b_ref = refs[0], refs[1]
        pos = 2
        add_ref = None
        if has_add:
            add_ref = refs[pos]
            pos += 1
        extra_refs = refs[pos:pos + len(extras)]
        pos += len(extras) + has_alias + has_after
        o_refs = refs[pos:pos + n_out]
        prod = lax.dot_general(a_ref[...], b_ref[...], (dims, ((), ())), preferred_element_type=F32)

        def finish(r):
            if epilogue is not None:
                epilogue(r, extra_refs, o_refs)
                return
            if has_add:
                r = r + add_ref[...].astype(F32)
            o_refs[0][...] = r.astype(o_refs[0].dtype)

        if nk == 1:
            finish(prod)
            return
        acc_ref = refs[pos + n_out]
        k = pl.program_id(2)

        @pl.when(k == 0)
        def _():
            acc_ref[...] = prod

        @pl.when(k > 0)
        def _():
            acc_ref[...] += prod

        @pl.when(k == nk - 1)
        def _():
            finish(acc_ref[...])

    operands = [a, b]
    in_specs = [a_spec, b_spec]
    if has_add:
        operands.append(add)
        in_specs.append(add_spec)
    for arr, spec in extras:
        operands.append(arr)
        in_specs.append(spec)
    aliases = {}
    if has_alias:
        aliases = {len(operands): 0}
        operands.append(alias)
        in_specs.append(ANY)
    if has_after:
        operands.append(after)
        in_specs.append(ANY)
    return pl.pallas_call(
        body, name=name, grid=grid, in_specs=in_specs, out_specs=out_spec, out_shape=out_shape,
        scratch_shapes=[pltpu.VMEM(acc_shape, F32)] if nk > 1 else [], input_output_aliases=aliases,
        compiler_params=_params("parallel", "parallel", "arbitrary"),
    )(*operands)


IJ = lambda i, j, k: (i, j)
IK = lambda i, j, k: (i, k)
KJ = lambda i, j, k: (k, j)
KI = lambda i, j, k: (k, i)
NN = ((1,), (0,))
NT = ((1,), (1,))
TN = ((0,), (0,))


def _rms_fwd(name, x, g, after=None):
    R, D = x.shape
    tr = _tile(R, 256)
    extra = [] if after is None else [after]

    def body(x_ref, g_ref, *rest):
        xf = x_ref[...]
        r = lax.rsqrt(jnp.mean(xf * xf, axis=-1, keepdims=True) + EPS)
        rest[-1][...] = (xf * r * g_ref[...]).astype(BF16)

    return pl.pallas_call(
        body, name=name, grid=(R // tr,),
        in_specs=[pl.BlockSpec((tr, D), lambda i: (i, 0)), pl.BlockSpec((1, D), lambda i: (0, 0))] + [ANY] * len(extra),
        out_specs=pl.BlockSpec((tr, D), lambda i: (i, 0)),
        out_shape=jax.ShapeDtypeStruct((R, D), BF16), compiler_params=_params("parallel"),
    )(x, g, *extra)


def _rms_bwd(name, dh, x, g, dres, after=None):
    R, D = x.shape
    tr = _tile(R, 256)
    extra = [] if after is None else [after]

    def body(dh_ref, x_ref, g_ref, dres_ref, *rest):
        dx_ref, dxb_ref, dg_ref = rest[len(extra):]
        xf = x_ref[...]
        r = lax.rsqrt(jnp.mean(xf * xf, axis=-1, keepdims=True) + EPS)
        xn = xf * r
        d = dh_ref[...]
        dyg = d * g_ref[...]
        dx = r * (dyg - xn * jnp.mean(dyg * xn, axis=-1, keepdims=True)) + dres_ref[...]
        dx_ref[...] = dx
        dxb_ref[...] = dx.astype(BF16)

        @pl.when(pl.program_id(0) == 0)
        def _():
            dg_ref[...] = jnp.zeros_like(dg_ref)

        dg_ref[...] += jnp.sum(d * xn, axis=0, keepdims=True)

    row = pl.BlockSpec((tr, D), lambda i: (i, 0))
    vec = pl.BlockSpec((1, D), lambda i: (0, 0))
    return pl.pallas_call(
        body, name=name, grid=(R // tr,), in_specs=[row, row, vec, row] + [ANY] * len(extra), out_specs=[row, row, vec],
        out_shape=[jax.ShapeDtypeStruct((R, D), F32), jax.ShapeDtypeStruct((R, D), BF16),
                   jax.ShapeDtypeStruct((1, D), F32)],
        compiler_params=_params("arbitrary"),
    )(dh, x, g, dres, *extra)


def _loss_head(x2, g, target):
    R, D = x2.shape
    tr = _tile(R, 256)

    def body(x_ref, g_ref, t_ref, dx_ref, dxb_ref, dg_ref, loss_ref):
        xf = x_ref[...]
        gg = g_ref[...]
        r = lax.rsqrt(jnp.mean(xf * xf, axis=-1, keepdims=True) + EPS)
        xn = xf * r
        e = xn * gg - t_ref[...]
        part = 0.5 * jnp.sum(jnp.mean(e * e, axis=-1, keepdims=True), axis=0, keepdims=True)
        dy = e * (1.0 / D)
        dyg = dy * gg
        dx = r * (dyg - xn * jnp.mean(dyg * xn, axis=-1, keepdims=True))
        dx_ref[...] = dx
        dxb_ref[...] = dx.astype(BF16)

        @pl.when(pl.program_id(0) == 0)
        def _():
            dg_ref[...] = jnp.zeros_like(dg_ref)
            loss_ref[...] = jnp.zeros_like(loss_ref)

        dg_ref[...] += jnp.sum(dy * xn, axis=0, keepdims=True)
        loss_ref[...] += jnp.broadcast_to(part, loss_ref.shape)

    row = pl.BlockSpec((tr, D), lambda i: (i, 0))
    vec = pl.BlockSpec((1, D), lambda i: (0, 0))
    return pl.pallas_call(
        body, name="loss_head", grid=(R // tr,), in_specs=[row, vec, row],
        out_specs=[row, row, vec, pl.BlockSpec((1, 128), lambda i: (0, 0))],
        out_shape=[jax.ShapeDtypeStruct((R, D), F32), jax.ShapeDtypeStruct((R, D), BF16),
                   jax.ShapeDtypeStruct((1, D), F32), jax.ShapeDtypeStruct((1, 128), F32)],
        compiler_params=_params("arbitrary"),
    )(x2, g, target)


def _pool_band(tr, reverse, w):
    r = lax.broadcasted_iota(jnp.int32, (tr, tr + POOL_HALO), 0)
    c = lax.broadcasted_iota(jnp.int32, (tr, tr + POOL_HALO), 1)
    if reverse:
        inside = (c >= r) & (c < r + w)
    else:
        cc = c - POOL_HALO
        inside = (cc <= r) & (cc > r - w)
    return jnp.where(inside, 1.0, 0.0).astype(BF16)


def _pool_fwd(proj, S, EMIX):
    PG = EMIX // N_POOL_GROUPS
    cb = PG
    tr = _tile(S, 512)
    per_group = PG // cb

    def body(u_ref, o_ref, ext):
        i = pl.program_id(1)
        w = jnp.left_shift(2, pl.program_id(0) // per_group)

        @pl.when(i == 0)
        def _():
            ext[0:POOL_HALO, :] = jnp.zeros((POOL_HALO, cb), BF16)

        u = u_ref[...]
        ext[POOL_HALO:, :] = u
        win = jnp.dot(_pool_band(tr, False, w), ext[...], preferred_element_type=F32)
        pos = i * tr + lax.broadcasted_iota(jnp.int32, (tr, 1), 0)
        cnt = jnp.minimum(pos + 1, w).astype(F32)
        o_ref[...] = (win / cnt - u.astype(F32)).astype(BF16)
        ext[0:POOL_HALO, :] = u[tr - POOL_HALO:, :]

    return pl.pallas_call(
        body, name="pool_fwd", grid=(EMIX // cb, S // tr),
        in_specs=[pl.BlockSpec((tr, cb), lambda j, i: (i, j))],
        out_specs=pl.BlockSpec((tr, cb), lambda j, i: (i, j)),
        out_shape=jax.ShapeDtypeStruct((S, EMIX), BF16),
        scratch_shapes=[pltpu.VMEM((tr + POOL_HALO, cb), BF16)],
        compiler_params=_params("parallel", "arbitrary"),
    )(proj)


def _pool_bwd(dpooled, dproj, S, EMIX):
    PG = EMIX // N_POOL_GROUPS
    cb = PG
    tr = _tile(S, 512)
    per_group = PG // cb
    nrt = S // tr

    def body(d_ref, _, o_ref, ext):
        step = pl.program_id(1)
        i = nrt - 1 - step
        w = jnp.left_shift(2, pl.program_id(0) // per_group)

        @pl.when(step == 0)
        def _():
            ext[tr:, :] = jnp.zeros((POOL_HALO, cb), BF16)

        d = d_ref[...]
        pos = i * tr + lax.broadcasted_iota(jnp.int32, (tr, 1), 0)
        cnt = jnp.minimum(pos + 1, w).astype(F32)
        z = (d / cnt).astype(BF16)
        ext[0:tr, :] = z
        win = jnp.dot(_pool_band(tr, True, w), ext[...], preferred_element_type=F32)
        o_ref[...] = (win - d).astype(BF16)
        ext[tr:, :] = z[0:POOL_HALO, :]

    return pl.pallas_call(
        body, name="pool_bwd", grid=(EMIX // cb, nrt),
        in_specs=[pl.BlockSpec((tr, cb), lambda j, s: (nrt - 1 - s, j)), ANY],
        out_specs=pl.BlockSpec((tr, cb), lambda j, s: (nrt - 1 - s, j)),
        out_shape=jax.ShapeDtypeStruct(dproj.shape, dproj.dtype),
        scratch_shapes=[pltpu.VMEM((tr + POOL_HALO, cb), BF16)],
        input_output_aliases={1: 0},
        compiler_params=_params("parallel", "arbitrary"),
    )(dpooled, dproj)


def _ca_fwd(name, proj, qblk, kv, premix, S, ECA, EMIX):
    M = kv.shape[0]
    hd = ECA // CA_HEADS
    ts = _tile(S, 512)
    scale = hd ** -0.5

    def body(q_ref, kv_ref, _, o_ref):
        for h in range(CA_HEADS):
            q = q_ref[:, h * hd:(h + 1) * hd]
            k = kv_ref[:, h * hd:(h + 1) * hd]
            v = kv_ref[:, ECA + h * hd:ECA + (h + 1) * hd]
            s = lax.dot_general(q, k, (NT, ((), ())), preferred_element_type=F32) * scale
            s = s - jnp.max(s, axis=-1, keepdims=True)
            p = jnp.exp(s)
            p = p / jnp.sum(p, axis=-1, keepdims=True)
            o = jnp.dot(p.astype(BF16), v, preferred_element_type=F32)
            o_ref[:, h * hd:(h + 1) * hd] = o.astype(BF16)

    return pl.pallas_call(
        body, name=name, grid=(S // ts,),
        in_specs=[pl.BlockSpec((ts, ECA), lambda i: (i, qblk)), pl.BlockSpec((M, 2 * ECA), lambda i: (0, 0)), ANY],
        out_specs=pl.BlockSpec((ts, ECA), lambda i: (i, EMIX // ECA)),
        out_shape=jax.ShapeDtypeStruct(premix.shape, premix.dtype),
        input_output_aliases={2: 0}, compiler_params=_params("parallel"),
    )(proj, kv, premix)


def _ca_bwd(name, dpremix, proj, qblk, kv, dbuf, dblk, S, ECA, EMIX):
    M = kv.shape[0]
    hd = ECA // CA_HEADS
    ts = _tile(S, 512)
    scale = hd ** -0.5

    def body(do_ref, q_ref, kv_ref, _, dq_ref, dkv_ref):
        @pl.when(pl.program_id(0) == 0)
        def _():
            dkv_ref[...] = jnp.zeros_like(dkv_ref)

        for h in range(CA_HEADS):
            lo, hi = h * hd, (h + 1) * hd
            q = q_ref[:, lo:hi]
            k = kv_ref[:, lo:hi]
            v = kv_ref[:, ECA + lo:ECA + hi]
            do = do_ref[:, lo:hi]
            s = lax.dot_general(q, k, (NT, ((), ())), preferred_element_type=F32) * scale
            s = s - jnp.max(s, axis=-1, keepdims=True)
            p = jnp.exp(s)
            p = p / jnp.sum(p, axis=-1, keepdims=True)
            pb = p.astype(BF16)
            dkv_ref[:, ECA + lo:ECA + hi] += lax.dot_general(pb, do, (TN, ((), ())), preferred_element_type=F32)
            dp = lax.dot_general(do, v, (NT, ((), ())), preferred_element_type=F32)
            ds = (p * (dp - jnp.sum(p * dp, axis=-1, keepdims=True)) * scale).astype(BF16)
            dq_ref[:, lo:hi] = jnp.dot(ds, k, preferred_element_type=F32).astype(BF16)
            dkv_ref[:, lo:hi] += lax.dot_general(ds, q, (TN, ((), ())), preferred_element_type=F32)

    return pl.pallas_call(
        body, name=name, grid=(S // ts,),
        in_specs=[pl.BlockSpec((ts, ECA), lambda i: (i, EMIX // ECA)), pl.BlockSpec((ts, ECA), lambda i: (i, qblk)),
                  pl.BlockSpec((M, 2 * ECA), lambda i: (0, 0)), ANY],
        out_specs=[pl.BlockSpec((ts, ECA), lambda i: (i, dblk)), pl.BlockSpec((M, 2 * ECA), lambda i: (0, 0))],
        out_shape=[jax.ShapeDtypeStruct(dbuf.shape, dbuf.dtype), jax.ShapeDtypeStruct((M, 2 * ECA), F32)],
        input_output_aliases={3: 0}, compiler_params=_params("arbitrary"),
    )(dpremix, proj, kv, dbuf)


def _gate_fwd(name, premix, proj, gblk, colscale, S, EB, ECA):
    ts = _tile(S, 512)
    nj = EB // ECA
    nsteps = (S // ts) * nj

    def body(p_hbm, g_hbm, c_ref, o_ref, pbuf, gbuf, sems):
        s = pl.program_id(0) * nj + pl.program_id(1)

        def tiles(step):
            slot = step % RING_SLOTS
            rows = pl.ds(pl.multiple_of((step // nj) * ts, ts), ts)
            col = pl.multiple_of((step % nj) * ECA, 128)
            return (pltpu.make_async_copy(p_hbm.at[rows, pl.ds(col, ECA)], pbuf.at[slot], sems.at[0, slot]),
                    pltpu.make_async_copy(g_hbm.at[rows, pl.ds(gblk * ECA + col, ECA)], gbuf.at[slot], sems.at[1, slot]))

        @pl.when(s == 0)
        def _():
            for first in range(min(RING_SLOTS - 1, nsteps)):
                for cp in tiles(s + first):
                    cp.start()

        @pl.when(s + RING_SLOTS - 1 < nsteps)
        def _():
            for cp in tiles(s + RING_SLOTS - 1):
                cp.start()

        for cp in tiles(s):
            cp.wait()
        slot = s % RING_SLOTS
        g = gbuf[slot].astype(F32)
        o_ref[...] = (pbuf[slot].astype(F32) * c_ref[...] * (g * _sigmoid(g))).astype(BF16)

    return pl.pallas_call(
        body, name=name, grid=(S // ts, nj),
        in_specs=[ANY, ANY, pl.BlockSpec((1, ECA), lambda i, j: (0, j))],
        out_specs=pl.BlockSpec((ts, ECA), lambda i, j: (i, j)),
        out_shape=jax.ShapeDtypeStruct((S, EB), BF16),
        scratch_shapes=[pltpu.VMEM((RING_SLOTS, ts, ECA), BF16), pltpu.VMEM((RING_SLOTS, ts, ECA), BF16),
                        pltpu.SemaphoreType.DMA((2, RING_SLOTS))],
        compiler_params=_params("arbitrary", "arbitrary"),
    )(premix, proj, colscale)


def _gate_bwd_epilogue(db, extra_refs, out_refs):
    p_ref, g_ref, c_ref = extra_refs
    dp_ref, dg_ref, dc_ref = out_refs
    g = g_ref[...].astype(F32)
    sg = _sigmoid(g)
    si = g * sg
    c = c_ref[...]
    t = db * p_ref[...].astype(F32)
    dp_ref[...] = (db * si * c).astype(BF16)
    dg_ref[...] = (t * c * (sg * (1.0 + g * (1.0 - sg)))).astype(BF16)
    dc_ref[...] = jnp.sum(t * si, axis=0, keepdims=True)


def _hgrn_lb(lb_ref):
    l0 = lb_ref[0:1, :]
    l1 = lb_ref[1:2, :]
    mx = jnp.maximum(l0, l1)
    e0 = jnp.exp(l0 - mx)
    e1 = jnp.exp(l1 - mx)
    return e1 / (e0 + e1)


def _bdot(a, b, ca, cb):
    return lax.dot_general(a, b, (((ca,), (cb,)), ((0,), (0,))), preferred_element_type=F32)


def _tri_sum(tri, x):
    hi = x.astype(BF16)
    lo = (x - hi.astype(F32)).astype(BF16)
    tri = tri.astype(BF16)
    return _bdot(tri, hi, 2, 1) + _bdot(tri, lo, 2, 1)


def _hgrn_chunks(qin, fin, lbh, n):
    C = HG_CHUNK
    row = lax.broadcasted_iota(jnp.int32, (n, C, C), 1)
    col = lax.broadcasted_iota(jnp.int32, (n, C, C), 2)
    causal = row >= col
    sg = _sigmoid(fin)
    f = lbh + (1.0 - lbh) * sg
    k = 1.0 - f
    g = jnp.log(f)
    b = _tri_sum(jnp.where(causal, 1.0, 0.0), g)
    b_last = jnp.sum(g, axis=1, keepdims=True)
    eb = jnp.exp(b)
    einv = jnp.exp(-b)
    eend = jnp.exp(b_last - b)
    sq = _sigmoid(qin)
    a = qin * sq * (HG_HEAD_DIM ** -0.5) * eb
    bm = k * einv
    e = k * eend
    d = jnp.exp(b_last)
    p = jnp.where(causal, _bdot(a.astype(BF16), bm.astype(BF16), 2, 2), 0.0)
    return dict(causal=causal, sg=sg, f=f, eb=eb, einv=einv, eend=eend, sq=sq, a=a, bm=bm, e=e, d=d, p=p)


def _hgrn_fwd(proj, fgate, hgrn_lb, S, EMIX, EB):
    HD, C = HG_HEAD_DIM, HG_CHUNK
    HH = EMIX // HD
    hb = 6 if HH % 6 == 0 else 1
    W = hb * HD
    tr = _tile(S, 512)
    n = tr // C

    def body(q_ref, f_ref, i_ref, lb_ref, o_ref, rstd_ref, st_ref, state):
        @pl.when(pl.program_id(1) == 0)
        def _():
            state[...] = jnp.zeros_like(state)

        lb = _hgrn_lb(lb_ref)
        for h in range(hb):
            cs = slice(h * HD, (h + 1) * HD)
            qin = q_ref[:, cs].astype(F32).reshape(n, C, HD)
            fin = f_ref[:, cs].reshape(n, C, HD)
            v = i_ref[:, cs].reshape(n, C, HD)
            t = _hgrn_chunks(qin, fin, lb[:, cs], n)
            upd = _bdot(v, t["e"].astype(BF16), 1, 1)
            st = state[h]
            for c in range(n):
                st_ref[h, c] = st
                st = st * t["d"][c] + upd[c]
            state[h] = st
            o = _bdot(t["p"].astype(BF16), v, 2, 1) + _bdot(t["a"].astype(BF16), st_ref[h].astype(BF16), 2, 2)
            rstd = lax.rsqrt(jnp.mean(o * o, axis=-1, keepdims=True) + EPS)
            o_ref[:, cs] = (o * rstd).reshape(tr, HD).astype(BF16)
            rstd_ref[:, cs] = jnp.broadcast_to(rstd, (n, C, HD)).reshape(tr, HD)

    blk = lambda off: pl.BlockSpec((tr, W), lambda g, i: (i, off + g))
    return pl.pallas_call(
        body, name="hgrn_fwd", grid=(HH // hb, S // tr),
        in_specs=[blk(0), blk(0), blk(2 * EMIX // W), pl.BlockSpec((2, W), lambda g, i: (0, g))],
        out_specs=[blk(0), blk(0), pl.BlockSpec((hb, n, HD, HD), lambda g, i: (g, i, 0, 0))],
        out_shape=[jax.ShapeDtypeStruct((S, EB), BF16), jax.ShapeDtypeStruct((S, EMIX), F32),
                   jax.ShapeDtypeStruct((HH, S // C, HD, HD), F32)],
        scratch_shapes=[pltpu.VMEM((hb, HD, HD), F32)],
        compiler_params=_params("parallel", "arbitrary"),
    )(proj, fgate, proj, hgrn_lb)


def _hgrn_bwd(dpremix, premix, rstd, states, proj, fgate, hgrn_lb, S, EMIX):
    HD, C = HG_HEAD_DIM, HG_CHUNK
    HH = EMIX // HD
    hb = 6 if HH % 6 == 0 else 1
    W = hb * HD
    tr = _tile(S, 512)
    n = tr // C
    nrt = S // tr

    def body(do_ref, on_ref, rstd_ref, st_ref, q_ref, f_ref, i_ref, lb_ref, d_ref, dlb_ref, dstate, dsbuf):
        @pl.when(pl.program_id(1) == 0)
        def _():
            dstate[...] = jnp.zeros_like(dstate)
            dlb_ref[...] = jnp.zeros_like(dlb_ref)

        lb = _hgrn_lb(lb_ref)
        for h in range(hb):
            cs = slice(h * HD, (h + 1) * HD)
            qin = q_ref[:, cs].astype(F32).reshape(n, C, HD)
            fin = f_ref[:, cs].reshape(n, C, HD)
            v = i_ref[:, cs].reshape(n, C, HD)
            lbh = lb[:, cs]
            t = _hgrn_chunks(qin, fin, lbh, n)
            a, bm, e, d, p = t["a"], t["bm"], t["e"], t["d"], t["p"]
            ab, bmb, eb16 = a.astype(BF16), bm.astype(BF16), e.astype(BF16)
            on = on_ref[:, cs].astype(F32).reshape(n, C, HD)
            dn = do_ref[:, cs].astype(F32).reshape(n, C, HD)
            do = rstd_ref[:, cs].reshape(n, C, HD) * (dn - on * jnp.mean(dn * on, axis=-1, keepdims=True))
            dob = do.astype(BF16)
            grow = _bdot(dob, ab, 1, 1)
            ds = dstate[h]
            for c in reversed(range(n)):
                dsbuf[h, c] = ds
                ds = ds * d[c] + grow[c]
            dstate[h] = ds
            dst = dsbuf[h]
            st = st_ref[h]
            dstb = dst.astype(BF16)
            dp = jnp.where(t["causal"], _bdot(dob, v, 2, 2), 0.0).astype(BF16)
            dv = _bdot(p.astype(BF16), dob, 1, 1) + _bdot(eb16, dstb, 2, 2)
            da = _bdot(dp, bmb, 2, 1) + _bdot(dob, st.astype(BF16), 2, 1)
            dbm = _bdot(dp, ab, 1, 1)
            de = _bdot(v, dstb, 2, 1)
            dd = jnp.sum(dst * st, axis=1, keepdims=True)
            dk = dbm * t["einv"] + de * t["eend"]
            dee = de * e
            db = da * a - dbm * bm - dee
            extra = jnp.sum(dee, axis=1, keepdims=True) + dd * d
            upper = jnp.where(lax.broadcasted_iota(jnp.int32, (n, C, C), 2)
                              >= lax.broadcasted_iota(jnp.int32, (n, C, C), 1), 1.0, 0.0)
            dg = _tri_sum(upper, db) + extra
            df = dg / t["f"] - dk
            sg, sq = t["sg"], t["sq"]
            dq = da * t["eb"] * (HD ** -0.5) * (sq * (1.0 + qin * (1.0 - sq)))
            d_ref[0, :, cs] = dq.reshape(tr, HD).astype(BF16)
            d_ref[1, :, cs] = (df * (1.0 - lbh) * sg * (1.0 - sg)).reshape(tr, HD).astype(BF16)
            d_ref[2, :, cs] = dv.reshape(tr, HD).astype(BF16)
            dlb_ref[:, cs] += jnp.sum((df * (1.0 - sg)).reshape(tr, HD), axis=0, keepdims=True)

    rev = lambda off: pl.BlockSpec((tr, W), lambda g, s: (nrt - 1 - s, off + g))
    return pl.pallas_call(
        body, name="hgrn_bwd", grid=(HH // hb, nrt),
        in_specs=[rev(0), rev(0), rev(0), pl.BlockSpec((hb, n, HD, HD), lambda g, s: (g, nrt - 1 - s, 0, 0)),
                  rev(0), rev(0), rev(2 * EMIX // W), pl.BlockSpec((2, W), lambda g, s: (0, g))],
        out_specs=[pl.BlockSpec((3, tr, W), lambda g, s: (0, nrt - 1 - s, g)), pl.BlockSpec((1, W), lambda g, s: (0, g))],
        out_shape=[jax.ShapeDtypeStruct((3, S, EMIX), BF16), jax.ShapeDtypeStruct((1, EMIX), F32)],
        scratch_shapes=[pltpu.VMEM((hb, HD, HD), F32), pltpu.VMEM((hb, n, HD, HD), F32)],
        compiler_params=_params("parallel", "arbitrary"),
    )(dpremix, premix, rstd, states, proj, fgate, proj, hgrn_lb)


EW_BLOCK_ELEMS = 512 * 1024


def _ew_tiles(R, C):
    tc = C if C <= 4096 else _tile(C, 2048)
    tr = _tile(R, 512)
    while tr * tc > EW_BLOCK_ELEMS and tr % 16 == 0:
        tr //= 2
    return tr, tc


def _add_halves(name, core_chip, grad, got):
    _, _, R, C = grad.shape
    tr, tc = _ew_tiles(R, C)
    nc = C // tc
    nsteps = (R // tr) * nc * N_CHIPS

    def body(c_ref, a_hbm, b_hbm, o_ref, own_ref, abuf, bbuf, sems):
        step = (pl.program_id(0) * nc + pl.program_id(1)) * N_CHIPS + pl.program_id(2)

        def tiles(k):
            slot = k % RING_SLOTS
            chip_slot, tile = k % N_CHIPS, k // N_CHIPS
            rows = pl.ds(pl.multiple_of((tile // nc) * tr, tr), tr)
            cols = pl.ds(pl.multiple_of((tile % nc) * tc, 128), tc)
            return (pltpu.make_async_copy(a_hbm.at[chip_slot, c_ref[0], rows, cols], abuf.at[slot], sems.at[0, slot]),
                    pltpu.make_async_copy(b_hbm.at[chip_slot, rows, cols], bbuf.at[slot], sems.at[1, slot]))

        @pl.when(step == 0)
        def _():
            for first in range(min(RING_SLOTS - 1, nsteps)):
                for cp in tiles(step + first):
                    cp.start()

        @pl.when(step + RING_SLOTS - 1 < nsteps)
        def _():
            for cp in tiles(step + RING_SLOTS - 1):
                cp.start()

        for cp in tiles(step):
            cp.wait()
        slot = step % RING_SLOTS
        r = (abuf[slot].astype(F32) + bbuf[slot].astype(F32)).astype(BF16)
        o_ref[...] = r

        @pl.when(pl.program_id(2) == c_ref[1])
        def _():
            own_ref[...] = r

    blk = pl.BlockSpec((None, tr, tc), lambda i, j, s, c: (s, i, j))
    sds = jax.ShapeDtypeStruct(got.shape, BF16)
    return pl.pallas_call(
        body, name=name, out_shape=[sds, sds],
        grid_spec=pltpu.PrefetchScalarGridSpec(
            num_scalar_prefetch=1, grid=(R // tr, nc, N_CHIPS), in_specs=[ANY, ANY],
            out_specs=[blk, pl.BlockSpec((None, tr, tc), lambda i, j, s, c: (c[1], i, j))],
            scratch_shapes=[pltpu.VMEM((RING_SLOTS, tr, tc), BF16), pltpu.VMEM((RING_SLOTS, tr, tc), BF16),
                            pltpu.SemaphoreType.DMA((2, RING_SLOTS))]),
        compiler_params=_params("arbitrary", "arbitrary", "arbitrary"),
    )(core_chip, grad, got)


def _adam_step(w, g, m, v):
    mn = ADAM_B1 * m + (1.0 - ADAM_B1) * g
    vn = ADAM_B2 * v + (1.0 - ADAM_B2) * (g * g)
    m_hat = mn / (1.0 - ADAM_B1 ** ADAM_STEP)
    v_hat = vn / (1.0 - ADAM_B2 ** ADAM_STEP)
    return -ADAM_LR * (m_hat / (jnp.sqrt(v_hat) + ADAM_EPS) + ADAM_WD * w), mn, vn


def _adamw(name, w, g, m, v):
    R, C = w.shape
    tr, tc = _ew_tiles(R, C)

    def body(w_ref, g_ref, m_ref, v_ref, d_ref, mo_ref, vo_ref):
        d_ref[...], mo_ref[...], vo_ref[...] = _adam_step(w_ref[...], g_ref[...], m_ref[...], v_ref[...])

    blk = pl.BlockSpec((tr, tc), lambda i, j: (i, j))
    sds = jax.ShapeDtypeStruct((R, C), F32)
    return pl.pallas_call(
        body, name=name, grid=(R // tr, C // tc), in_specs=[blk] * 4, out_specs=[blk] * 3, out_shape=[sds] * 3,
        compiler_params=_params("parallel", "parallel"),
    )(w, g, m, v)


def _adamw_layers(name, w, gs, m, v):
    L, R, C = w.shape
    tr, tc = _ew_tiles(R, C)

    def body(*refs):
        w_ref, m_ref, v_ref = refs[:3]
        g_refs = refs[3:3 + L]
        go_ref, d_ref, mo_ref, vo_ref = refs[3 + L:]
        layer = pl.program_id(0)
        g = g_refs[0][...]
        for n in range(1, L):
            g = jnp.where(layer == n, g_refs[n][...], g)
        go_ref[...] = g
        d_ref[...], mo_ref[...], vo_ref[...] = _adam_step(w_ref[...], g, m_ref[...], v_ref[...])

    blk = pl.BlockSpec((None, tr, tc), lambda l, i, j: (l, i, j))
    of_layer = lambda n: pl.BlockSpec((tr, tc), lambda l, i, j: (jnp.where(l == n, i, 0), jnp.where(l == n, j, 0)))
    sds = jax.ShapeDtypeStruct((L, R, C), F32)
    return pl.pallas_call(
        body, name=name, grid=(L, R // tr, C // tc), in_specs=[blk] * 3 + [of_layer(n) for n in range(L)],
        out_specs=[blk] * 4, out_shape=[sds] * 4, compiler_params=_params("parallel", "parallel", "parallel"),
    )(w, m, v, *gs)


def _pack_rows(name, items, W):
    nv = len(items)
    first, row = [], 0
    for a, add, _ in items:
        first.append(row)
        row += 1 if add else a.shape[0]
    assert row <= SMALL_ROWS

    def body(*refs):
        o_ref = refs[nv]
        o_ref[...] = jnp.zeros_like(o_ref)
        for i, (a, add, width) in enumerate(items):
            val = refs[i][:, 0:width]
            if add:
                val = jnp.sum(val, axis=0, keepdims=True)
            o_ref[first[i]:first[i] + val.shape[0], 0:width] = val

    vm = pl.BlockSpec(memory_space=pltpu.VMEM)
    return pl.pallas_call(
        body, name=name, in_specs=[vm] * nv, out_specs=vm, out_shape=jax.ShapeDtypeStruct((SMALL_ROWS, W), F32),
    )(*[a for a, _, _ in items])


def _unpack_rows(name, pack, layout):
    def body(p_ref, *o_refs):
        row = 0
        for o_ref, (k, n) in zip(o_refs, layout):
            o_ref[...] = p_ref[row:row + k, 0:n]
            row += k

    vm = pl.BlockSpec(memory_space=pltpu.VMEM)
    return pl.pallas_call(
        body, name=name, in_specs=[vm], out_specs=[vm] * len(layout),
        out_shape=[jax.ShapeDtypeStruct(s, F32) for s in layout],
    )(pack)


def _small_sum(gathered, hgrn_lb, chip, nshard):
    _, T, W = gathered.shape

    def body(c_ref, g_ref, lb_ref, o_ref, loss_ref, tmp):
        acc = g_ref[0]
        for dev in range(1, N_DEV):
            acc = acc + g_ref[dev]
        tmp[...] = acc
        lb = _hgrn_lb(lb_ref)
        d1 = tmp[4:5, :] * (lb * (1.0 - lb))
        o_ref[...] = jnp.zeros_like(o_ref)
        o_ref[0:4, :] = tmp[0:4, :]
        o_ref[4:5, :] = -d1
        o_ref[5:6, :] = d1
        mine = tmp[5:6, 0:nshard]
        for b in range(1, N_CHIPS):
            mine = jnp.where(c_ref[0] == b, tmp[5:6, b * nshard:(b + 1) * nshard], mine)
        o_ref[6:7, 0:nshard] = mine
        o_ref[7:8, :] = tmp[6:7, :]
        loss_ref[...] = tmp[7:8, 0:128]

    vm = pl.BlockSpec(memory_space=pltpu.VMEM)
    return pl.pallas_call(
        body, name="small_sum", in_specs=[pl.BlockSpec(memory_space=pltpu.SMEM), vm, vm], out_specs=[vm, vm],
        out_shape=[jax.ShapeDtypeStruct((T, W), F32), jax.ShapeDtypeStruct((1, 128), F32)],
        scratch_shapes=[pltpu.VMEM((T, W), F32)],
    )(chip, gathered, hgrn_lb)


def _place():
    return lax.axis_index("x"), lax.axis_index("y"), lax.axis_index("c")


def _other_chips(x, y):
    return [(1 - x, y), (x, 1 - y), (1 - x, 1 - y)]


def _chunk_rows(rows, row_bytes):
    cr = rows
    while cr * row_bytes > STREAM_CHUNK_BYTES and cr % 32 == 0:
        cr //= 2
    return cr


def _stream(pairs, buf, sems, t, peer):
    lsem, ssem, rsem = sems
    n = len(pairs)
    loads, sent = [None] * n, [None] * n

    def load(k):
        slot = k % STREAM_SLOTS
        if k >= STREAM_SLOTS:
            sent[k - STREAM_SLOTS]()
        loads[k] = pltpu.make_async_copy(pairs[k][0], buf.at[slot], lsem.at[t, slot])
        loads[k].start()

    load(0)
    for k in range(n):
        slot = k % STREAM_SLOTS
        if k + 1 < n:
            load(k + 1)
        loads[k].wait()
        cp = pltpu.make_async_remote_copy(src_ref=buf.at[slot], dst_ref=pairs[k][1], send_sem=ssem.at[t, slot],
                                          recv_sem=rsem.at[t], device_id=peer, device_id_type=MESH)
        cp.start()
        sent[k] = cp.wait_send
    for k in range(max(0, n - STREAM_SLOTS), n):
        sent[k]()


def _stream_scratch(shapes):
    nt = len(shapes)
    return ([pltpu.VMEM((STREAM_SLOTS,) + s, d) for s, d in shapes]
            + [pltpu.SemaphoreType.DMA((nt, STREAM_SLOTS)), pltpu.SemaphoreType.DMA((nt, STREAM_SLOTS)),
               pltpu.SemaphoreType.DMA((nt,))])


def _exchange_halves(name, grads):
    nt = len(grads)
    hs = [g.shape[1] // 2 for g in grads]
    crs = [_chunk_rows(h, g.shape[2] * g.dtype.itemsize) for h, g in zip(hs, grads)]

    def body(*refs):
        ins, gots, bufs, sems = refs[:nt], refs[nt:2 * nt], refs[2 * nt:3 * nt], refs[3 * nt:]
        x, y, c = _place()
        sib = (x, y, 1 - c)
        for t in range(nt):
            h, cr = hs[t], crs[t]
            pairs = [(ins[t].at[b, pl.ds((1 - c) * h + r0, cr)], gots[t].at[b, pl.ds(r0, cr)])
                     for b in range(N_CHIPS) for r0 in range(0, h, cr)]
            _stream(pairs, bufs[t], sems, t, sib)
        for t in range(nt):
            pltpu.make_async_remote_copy(src_ref=gots[t], dst_ref=gots[t], send_sem=sems[1].at[t, 0],
                                         recv_sem=sems[2].at[t], device_id=sib, device_id_type=MESH).wait_recv()

    return pl.pallas_call(
        body, name=name, in_specs=[ANY] * nt, out_specs=[ANY] * nt,
        out_shape=[jax.ShapeDtypeStruct((N_CHIPS, h, g.shape[2]), g.dtype) for h, g in zip(hs, grads)],
        scratch_shapes=_stream_scratch([((cr, g.shape[2]), g.dtype) for cr, g in zip(crs, grads)]),
        compiler_params=pltpu.CompilerParams(vmem_limit_bytes=VMEM_LIMIT_BYTES),
    )(*grads)


def _scatter_plan(srcs, dsts):
    x, y, c = _place()
    me = 2 * x + y
    return [(srcs[t].at[2 * px + py], dsts[t].at[me], (px, py, c))
            for t in range(len(srcs)) for px, py in _other_chips(x, y)]


def _slot(dst, chip, r0, rows, cols):
    if len(dst.shape) == 3:
        return dst.at[chip, pl.ds(r0, rows)]
    return dst.at[pl.ds(r0, rows), pl.ds(pl.multiple_of(chip * cols, 128), cols)]


def _shard_dims(gathered):
    s = gathered.shape
    return (s[1], s[2]) if len(s) == 3 else (s[0], s[1] // N_CHIPS)


def _gather_plan(_, bufs):
    x, y, c = _place()
    me = 2 * x + y
    plan = []
    for buf in bufs:
        rows, cols = _shard_dims(buf)
        mine = _slot(buf, me, c * (rows // 2), rows // 2, cols)
        plan += [(mine, mine, (px, py, c)) for px, py in _other_chips(x, y)]
    return plan


def _cast_to_slot(name, chip, w, layer, gathered_shape, after):
    _, R, C = w.shape
    tr, tc = _ew_tiles(R, C)

    def body(c_ref, w_ref, *rest):
        rest[-1][...] = w_ref[...].astype(BF16)

    if len(gathered_shape) == 3:
        out_spec = pl.BlockSpec((None, tr, tc), lambda i, j, c: (c[0], i, j))
    else:
        out_spec = pl.BlockSpec((tr, tc), lambda i, j, c: (i, c[0] * (C // tc) + j))
    extra = [] if after is None else [after]
    return pl.pallas_call(
        body, name=name, out_shape=jax.ShapeDtypeStruct(gathered_shape, BF16),
        grid_spec=pltpu.PrefetchScalarGridSpec(
            num_scalar_prefetch=1, grid=(R // tr, C // tc),
            in_specs=[pl.BlockSpec((None, tr, tc), lambda i, j, c: (layer, i, j))] + [ANY] * len(extra),
            out_specs=out_spec),
        compiler_params=_params("parallel", "parallel"),
    )(chip, w, *extra)


HBM_SPEC = pl.BlockSpec(memory_space=pltpu.HBM)
SEM_SPEC = pl.BlockSpec(memory_space=pltpu.SEMAPHORE)


def _split_start(name, srcs, dsts, plan, ncopies, after):
    bufs = [pltpu.with_memory_space_constraint(a, pltpu.HBM) for a in list(srcs) + list(dsts)]
    nb, ns = len(bufs), len(srcs)
    operands = bufs + ([after] if after is not None else [])

    def body(*refs):
        outs = refs[len(operands):]
        send, recv, token = outs[0], outs[1], outs[-1]
        for i, (src, dst, dev) in enumerate(plan(refs[:ns], refs[ns:nb])):
            pltpu.make_async_remote_copy(src_ref=src, dst_ref=dst, send_sem=send.at[i], recv_sem=recv.at[i],
                                         device_id=dev, device_id_type=MESH).start()
        token[...] = jnp.zeros_like(token)

    res = pl.pallas_call(
        body, name=name,
        out_shape=[pltpu.SemaphoreType.DMA((ncopies,)), pltpu.SemaphoreType.DMA((ncopies,))]
        + [pltpu.HBM(a.shape, a.dtype) for a in bufs] + [jax.ShapeDtypeStruct((8, 128), F32)],
        in_specs=[HBM_SPEC] * nb + [ANY] * (len(operands) - nb),
        out_specs=[SEM_SPEC, SEM_SPEC] + [HBM_SPEC] * nb + [pl.BlockSpec(memory_space=pltpu.VMEM)],
        input_output_aliases={i: 2 + i for i in range(nb)},
        compiler_params=pltpu.CompilerParams(has_side_effects=pltpu.SideEffectType.DATAFLOW_SIDE_EFFECTING),
    )(*operands)
    return res[:-1], res[-1]


def _split_wait(name, started, plan, ns, after):
    send, recv, bufs = started[0], started[1], list(started[2:])
    nb = len(bufs)

    def body(*refs):
        send_ref, recv_ref = refs[nb], refs[nb + 1]
        for i, (src, dst, dev) in enumerate(plan(refs[:ns], refs[ns:nb])):
            cp = pltpu.make_async_remote_copy(src_ref=src, dst_ref=dst, send_sem=send_ref.at[i], recv_sem=recv_ref.at[i],
                                              device_id=dev, device_id_type=MESH)
            cp.wait_send()
            cp.wait_recv()

    res = pl.pallas_call(
        body, name=name, out_shape=[pltpu.HBM(a.shape, a.dtype) for a in bufs],
        in_specs=[HBM_SPEC] * nb + [SEM_SPEC, SEM_SPEC, ANY], out_specs=[HBM_SPEC] * nb,
        input_output_aliases={i: i for i in range(nb)},
        compiler_params=pltpu.CompilerParams(has_side_effects=pltpu.SideEffectType.DATAFLOW_SIDE_EFFECTING),
    )(*bufs, send, recv, after)
    return res[:ns], res[ns:]


def _gather_finish(name, _, gathered):
    nt = len(gathered)
    dims = [_shard_dims(g) for g in gathered]
    hs = [rows // 2 for rows, _ in dims]
    crs = [_chunk_rows(h, cols * 2) for h, (_, cols) in zip(hs, dims)]

    def body(*refs):
        outs, bufs, sems = refs[nt:2 * nt], refs[2 * nt:3 * nt], refs[3 * nt:]
        x, y, c = _place()
        sib = (x, y, 1 - c)
        for t in range(nt):
            h, cr, cols = hs[t], crs[t], dims[t][1]
            passed = [_slot(outs[t], 2 * px + py, c * h + r0, cr, cols)
                      for px, py in _other_chips(x, y) for r0 in range(0, h, cr)]
            _stream([(r, r) for r in passed], bufs[t], sems, t, sib)
        for t in range(nt):
            if len(gathered[t].shape) == 3:
                three = outs[t].at[pl.ds(0, 3), pl.ds(0, hs[t])]
            else:
                three = outs[t].at[pl.ds(0, hs[t]), pl.ds(0, 3 * dims[t][1])]
            pltpu.make_async_remote_copy(src_ref=three, dst_ref=three, send_sem=sems[1].at[t, 0],
                                         recv_sem=sems[2].at[t], device_id=sib, device_id_type=MESH).wait_recv()

    return pl.pallas_call(
        body, name=name, in_specs=[ANY] * nt, out_specs=[ANY] * nt,
        out_shape=[jax.ShapeDtypeStruct(g.shape, g.dtype) for g in gathered],
        scratch_shapes=_stream_scratch([((cr, cols), BF16) for cr, (_, cols) in zip(crs, dims)]),
        input_output_aliases={t: t for t in range(nt)},
        compiler_params=pltpu.CompilerParams(vmem_limit_bytes=VMEM_LIMIT_BYTES),
    )(*gathered)


def _sum_share(landed):
    nt = len(landed)
    hs = [a.shape[1] for a in landed]
    cs = [a.shape[2] for a in landed]
    crs = [_chunk_rows(h, 2 * c * 4) for h, c in zip(hs, cs)]
    shapes = sorted(set(zip(crs, cs)))
    which = [shapes.index(s) for s in zip(crs, cs)]

    def body(*refs):
        ins, outs = refs[:nt], refs[nt:2 * nt]
        inbufs, outbufs = refs[2 * nt:2 * nt + len(shapes)], refs[2 * nt + len(shapes):2 * nt + 2 * len(shapes)]
        lsem, ssem, osem, rsem = refs[2 * nt + 2 * len(shapes):]
        x, y, c = _place()
        sib = (x, y, 1 - c)
        for t in range(nt):
            cr, n, ib, ob = crs[t], hs[t] // crs[t], inbufs[which[t]], outbufs[which[t]]
            loads, gone = [None] * n, [None] * n

            def load(k):
                slot = k % SUM_SLOTS
                if k >= SUM_SLOTS:
                    for cp_wait in gone[k - SUM_SLOTS]:
                        cp_wait()
                loads[k] = pltpu.make_async_copy(ins[t].at[:, pl.ds(k * cr, cr)], ib.at[slot], lsem.at[t, slot])
                loads[k].start()

            load(0)
            for k in range(n):
                slot = k % SUM_SLOTS
                if k + 1 < n:
                    load(k + 1)
                loads[k].wait()
                acc = ib[slot, 0].astype(F32)
                for s in range(1, N_CHIPS):
                    acc = acc + ib[slot, s].astype(F32)
                ob[slot] = acc
                rows = outs[t].at[c, pl.ds(k * cr, cr)]
                away = pltpu.make_async_remote_copy(src_ref=ob.at[slot], dst_ref=rows, send_sem=ssem.at[t, slot],
                                                    recv_sem=rsem.at[t], device_id=sib, device_id_type=MESH)
                away.start()
                home = pltpu.make_async_copy(ob.at[slot], rows, osem.at[t, slot])
                home.start()
                gone[k] = (away.wait_send, home.wait)
            for k in range(max(0, n - SUM_SLOTS), n):
                for cp_wait in gone[k]:
                    cp_wait()
        for t in range(nt):
            other = outs[t].at[1 - c]
            pltpu.make_async_remote_copy(src_ref=other, dst_ref=other, send_sem=ssem.at[t, 0], recv_sem=rsem.at[t],
                                         device_id=sib, device_id_type=MESH).wait_recv()

    slot_sems = pltpu.SemaphoreType.DMA((nt, SUM_SLOTS))
    return pl.pallas_call(
        body, name="sum_share", in_specs=[ANY] * nt, out_specs=[ANY] * nt,
        out_shape=[jax.ShapeDtypeStruct((2, h, c), F32) for h, c in zip(hs, cs)],
        scratch_shapes=[pltpu.VMEM((SUM_SLOTS, N_CHIPS, cr, c), BF16) for cr, c in shapes]
        + [pltpu.VMEM((SUM_SLOTS, cr, c), F32) for cr, c in shapes]
        + [slot_sems, slot_sems, slot_sems, pltpu.SemaphoreType.DMA((nt,))],
        compiler_params=pltpu.CompilerParams(vmem_limit_bytes=VMEM_LIMIT_BYTES),
    )(*landed)


def _allgather_small(name, v):
    def body(v_ref, o_ref, send, recv, lsem):
        x, y, c = _place()
        me = 4 * x + 2 * y + c
        loc = pltpu.make_async_copy(v_ref, o_ref.at[me], lsem)
        loc.start()
        copies = []
        for k in range(1, N_DEV):
            px = 1 - x if k & 4 else x
            py = 1 - y if k & 2 else y
            pc = 1 - c if k & 1 else c
            cp = pltpu.make_async_remote_copy(
                src_ref=v_ref, dst_ref=o_ref.at[me], send_sem=send.at[k - 1], recv_sem=recv.at[k - 1],
                device_id=(px, py, pc), device_id_type=MESH)
            cp.start()
            copies.append(cp)
        for cp in copies:
            cp.wait()
        loc.wait()

    vm = pl.BlockSpec(memory_space=pltpu.VMEM)
    return pl.pallas_call(
        body, name=name, in_specs=[vm], out_specs=vm,
        out_shape=jax.ShapeDtypeStruct((N_DEV,) + v.shape, v.dtype),
        scratch_shapes=[pltpu.SemaphoreType.DMA((N_DEV - 1,))] * 2 + [pltpu.SemaphoreType.DMA],
    )(v)


def kernel(x, mem, norm_g, mem_norm_g, w_kv, w_out, pool_w_in, pool_w_grp, pool_scale, hgrn_w_in, hgrn_lb, hgrn_norm_g, final_g, loss_target, m_norm_g, m_mem_norm_g, m_w_kv, m_w_out, m_pool_w_in, m_pool_w_grp, m_pool_scale, m_hgrn_w_in, m_hgrn_lb, m_hgrn_norm_g, m_final_g, v_norm_g, v_mem_norm_g, v_w_kv, v_w_out, v_pool_w_in, v_pool_w_grp, v_pool_scale, v_hgrn_w_in, v_hgrn_lb, v_hgrn_norm_g, v_final_g):
    _, S, D = x.shape
    M = mem.shape[1]
    EB = 2 * D
    ECA = EB // 4
    EMIX = EB - ECA
    PG = EMIX // N_POOL_GROUPS
    NP0 = EMIX + ECA + EB
    NP1 = 3 * EMIX + ECA + EB
    SH0, SH1 = NP0 // N_CHIPS, NP1 // N_CHIPS
    DK, EK = D // N_CHIPS, EB // N_CHIPS
    TNP = 512 if all(v % 512 == 0 for v in (SH0, SH1, ECA, EMIX)) else 256
    TM = _tile(S, 1024)
    TMF = _tile(S, 2048)
    TD = _tile(D, 512)
    TDW = _tile(D, 1024)
    c0, c1 = SH0 // TNP, SH1 // TNP
    qt = EMIX // TNP
    chip = 2 * lax.axis_index("x") + lax.axis_index("y")

    xs, ms, tgt = x[0], mem[0], loss_target[0]

    sds = jax.ShapeDtypeStruct
    chip1 = chip.astype(jnp.int32).reshape(1)

    def start_gather(tag, layers, after):
        bufs = []
        for t, (w, layer) in enumerate(layers):
            w3 = w.reshape((w.shape[0], -1, w.shape[-1]))
            shape = (D, NP1) if w is hgrn_w_in else (N_CHIPS,) + w3.shape[1:]
            bufs.append(_cast_to_slot(f"cast_{tag}{t}", chip1, w3, layer, shape, after))
        return _split_start(f"gather_{tag}_start", [], bufs, _gather_plan, 3 * len(bufs), after)

    gather_a, token = start_gather("a", [(pool_w_in, 0)], None)
    gather_b, token = start_gather("b", [(w_kv, 0), (w_out, 0), (pool_w_grp, 0)], token)
    gather_c, token = start_gather("c", [(hgrn_w_in, 0)], token)
    gather_d, token = start_gather("d", [(w_kv, 1), (w_out, 1)], token)

    tek, tew = _tile(EK, 512), _tile(EK, 1024)

    mem_n = _rms_fwd("rms_mem", ms, mem_norm_g.reshape(1, D), token)
    h0 = _rms_fwd("rms0", xs, norm_g[0:1], token)
    wpin, = _gather_finish("gather_a_finish", *_split_wait("gather_a_wait", gather_a, _gather_plan, 0, h0))

    tkw = _tile(2 * ECA, 1024)

    def kv_of(layer, wkv):
        return _matmul(
            f"kv{layer}", mem_n, wkv.reshape(D, 2 * ECA), grid=(1, 2 * ECA // tkw, 1),
            a_spec=pl.BlockSpec((M, D), lambda i, j, k: (0, 0)), b_spec=pl.BlockSpec((D, tkw), lambda i, j, k: (0, j)),
            out_shape=sds((M, 2 * ECA), BF16), out_spec=pl.BlockSpec((M, tkw), lambda i, j, k: (0, j)),
            acc_shape=(M, tkw), dims=NN)

    tko = _tile(EB, 2048)

    def out_proj(layer, branch, wout, resid):
        return _matmul(
            f"out_proj{layer}", branch, wout.reshape(EB, D), grid=(S // TM, D // TDW, EB // tko),
            a_spec=pl.BlockSpec((TM, tko), IK), b_spec=pl.BlockSpec((tko, TDW), KJ),
            out_shape=sds((S, D), F32), out_spec=pl.BlockSpec((TM, TDW), IJ),
            acc_shape=(TM, TDW), dims=NN, add=resid, add_spec=pl.BlockSpec((TM, TDW), IJ))

    ones_ca = jnp.ones((1, ECA), F32)

    proj0 = _matmul(
        "proj0", h0, wpin, grid=(S // TMF, NP0 // TNP, 1),
        a_spec=pl.BlockSpec((TMF, D), lambda i, j, k: (i, 0)),
        b_spec=pl.BlockSpec((None, D, TNP), lambda i, j, k: (j // c0, 0, j % c0)),
        out_shape=sds((S, NP0), BF16), out_spec=pl.BlockSpec((TMF, TNP), IJ),
        acc_shape=(TMF, TNP), dims=NN)
    pooled = _pool_fwd(proj0, S, EMIX)
    wkv0, wout0, g_grp = _gather_finish("gather_b_finish", *_split_wait("gather_b_wait", gather_b, _gather_plan, 0, pooled))
    wgrp = g_grp.reshape(N_CHIPS, N_POOL_GROUPS, PG // N_CHIPS, PG).transpose(1, 0, 2, 3).reshape(N_POOL_GROUPS, PG, PG)
    kv = [kv_of(0, wkv0), None]
    premix0 = _matmul(
        "pool_grp", pooled, wgrp, grid=(S // TM, N_POOL_GROUPS, 1),
        a_spec=pl.BlockSpec((TM, PG), lambda i, j, k: (i, j)),
        b_spec=pl.BlockSpec((None, PG, PG), lambda i, j, k: (j, 0, 0)),
        out_shape=sds((S, EB), BF16), out_spec=pl.BlockSpec((TM, PG), lambda i, j, k: (i, j)),
        acc_shape=(TM, PG), dims=NN)
    premix0 = _ca_fwd("ca_fwd0", proj0, EMIX // ECA, kv[0], premix0, S, ECA, EMIX)
    colscale0 = jnp.concatenate([pool_scale.reshape(1, EMIX), ones_ca], axis=1)
    gblk0 = (EMIX + ECA) // ECA
    branch0 = _gate_fwd("gate_fwd0", premix0, proj0, gblk0, colscale0, S, EB, ECA)
    x1 = out_proj(0, branch0, wout0, xs)

    whin, = _gather_finish("gather_c_finish", *_split_wait("gather_c_wait", gather_c, _gather_plan, 0, x1))
    h1 = _rms_fwd("rms1", x1, norm_g[1:2])

    def proj1_cols(name, ncols, col_of, out_cols, out_dtype, out_col_of):
        return _matmul(
            name, h1, whin, grid=(S // TMF, ncols, 1),
            a_spec=pl.BlockSpec((TMF, D), lambda i, j, k: (i, 0)),
            b_spec=pl.BlockSpec((D, TNP), lambda i, j, k: (0, col_of(j))),
            out_shape=sds((S, out_cols), out_dtype), out_spec=pl.BlockSpec((TMF, TNP), lambda i, j, k: (i, out_col_of(j))),
            acc_shape=(TMF, TNP), dims=NN)

    skip_f = lambda j: jnp.where(j < qt, j, j + qt)
    proj1 = proj1_cols("proj1", NP1 // TNP - qt, skip_f, NP1, BF16, skip_f)
    fgate = proj1_cols("proj1_f", qt, lambda j: j + qt, EMIX, F32, lambda j: j)
    premix1, rstd1, states = _hgrn_fwd(proj1, fgate, hgrn_lb, S, EMIX, EB)
    wkv1, wout1 = _gather_finish("gather_d_finish", *_split_wait("gather_d_wait", gather_d, _gather_plan, 0, rstd1))
    kv[1] = kv_of(1, wkv1)
    premix1 = _ca_fwd("ca_fwd1", proj1, 3 * EMIX // ECA, kv[1], premix1, S, ECA, EMIX)
    norm_tiles = _allgather_small("allgather_norm_g", jnp.pad(hgrn_norm_g, ((0, SMALL_ROWS - 1), (0, 0))))
    hg_norm = norm_tiles[0::2, 0, :].reshape(1, EMIX)
    colscale1 = jnp.concatenate([hg_norm, ones_ca], axis=1)
    gblk1 = (3 * EMIX + ECA) // ECA
    branch1 = _gate_fwd("gate_fwd1", premix1, proj1, gblk1, colscale1, S, EB, ECA)
    x2 = out_proj(1, branch1, wout1, x1)

    dx2, dx2b, d_final_g, loss_part = _loss_head(x2, final_g.reshape(1, D), tgt)

    def out_proj_bwd(layer, dxb, branch, wout, premix, proj, gblk, colscale, dshape, dblk):
        goff, doff = gblk * ECA // tek, dblk * ECA // tek
        dpremix, dgate, dcol = _matmul(
            f"dbranch{layer}", dxb, wout, grid=(S // TMF, EB // tek, 1),
            a_spec=pl.BlockSpec((TMF, D), lambda i, j, k: (i, 0)),
            b_spec=pl.BlockSpec((None, tek, D), lambda i, j, k: (j // (EK // tek), j % (EK // tek), 0)),
            extras=[(premix, pl.BlockSpec((TMF, tek), IJ)), (proj, pl.BlockSpec((TMF, tek), lambda i, j, k: (i, goff + j))),
                    (colscale, pl.BlockSpec((1, tek), lambda i, j, k: (0, j)))],
            epilogue=_gate_bwd_epilogue,
            out_shape=[sds((S, EB), BF16), sds(dshape, BF16), sds((S // TMF, 1, EB), F32)],
            out_spec=[pl.BlockSpec((TMF, tek), IJ), pl.BlockSpec((TMF, tek), lambda i, j, k: (i, doff + j)),
                      pl.BlockSpec((None, 1, tek), lambda i, j, k: (i, 0, j))],
            acc_shape=(TMF, tek), dims=NT)
        dw = _matmul(
            f"dwout{layer}", branch, dxb, grid=(EB // tew, D // TD, 1),
            a_spec=pl.BlockSpec((S, tew), lambda i, j, k: (0, i)), b_spec=pl.BlockSpec((S, TD), lambda i, j, k: (0, j)),
            out_shape=sds((N_CHIPS, EK, D), BF16),
            out_spec=pl.BlockSpec((None, tew, TD), lambda i, j, k: (i // (EK // tew), i % (EK // tew), j)),
            acc_shape=(tew, TD), dims=TN)
        return dpremix, dgate, dcol.reshape(S // TMF, EB), dw

    def kv_bwd(layer, dkv, wkv, dmem_add):
        dkvb = dkv.astype(BF16)
        dmem = _matmul(
            f"dmem{layer}", dkvb, wkv.reshape(D, 2 * ECA), grid=(1, D // TDW, 1),
            a_spec=pl.BlockSpec((M, 2 * ECA), lambda i, j, k: (0, 0)),
            b_spec=pl.BlockSpec((TDW, 2 * ECA), lambda i, j, k: (j, 0)),
            out_shape=sds((M, D), F32), out_spec=pl.BlockSpec((M, TDW), lambda i, j, k: (0, j)), acc_shape=(M, TDW),
            dims=NT, add=dmem_add, add_spec=pl.BlockSpec((M, TDW), lambda i, j, k: (0, j)))
        dw = _matmul(
            f"dwkv{layer}", mem_n, dkvb, grid=(D // TDW, 2 * ECA // tkw, 1),
            a_spec=pl.BlockSpec((M, TDW), lambda i, j, k: (0, i)), b_spec=pl.BlockSpec((M, tkw), lambda i, j, k: (0, j)),
            out_shape=sds((D, 2 * ECA), BF16), out_spec=pl.BlockSpec((TDW, tkw), IJ), acc_shape=(TDW, tkw), dims=TN)
        return dmem, dw.reshape(N_CHIPS, DK, 2 * ECA)

    dpremix1, drest1, dcol1, gw_out1 = out_proj_bwd(1, dx2b, branch1, wout1, premix1, proj1, gblk1, colscale1,
                                                    (S, ECA + EB), 1)
    drest1, dkv1 = _ca_bwd("ca_bwd1", dpremix1, proj1, 3 * EMIX // ECA, kv[1], drest1, 0, S, ECA, EMIX)
    dqfi, dlb = _hgrn_bwd(dpremix1, premix1, rstd1, states, proj1, fgate, hgrn_lb, S, EMIX)
    nq, nr = 3 * qt, (ECA + EB) // TNP
    tkh = _tile(EMIX, 1024) if (ECA + EB) % _tile(EMIX, 1024) == 0 else TNP
    kq = EMIX // tkh
    dh1 = _matmul(
        "dh1_qfi", dqfi, whin, grid=(S // TM, D // TDW, 3),
        a_spec=pl.BlockSpec((None, TM, EMIX), lambda i, j, k: (k, i, 0)),
        b_spec=pl.BlockSpec((TDW, EMIX), lambda i, j, k: (j, k)),
        out_shape=sds((S, D), F32), out_spec=pl.BlockSpec((TM, TDW), IJ), acc_shape=(TM, TDW), dims=NT)
    dh1 = _matmul(
        "dh1_rest", drest1, whin, grid=(S // TM, D // TDW, (ECA + EB) // tkh), a_spec=pl.BlockSpec((TM, tkh), IK),
        b_spec=pl.BlockSpec((TDW, tkh), lambda i, j, k: (j, k + 3 * kq)),
        out_shape=sds((S, D), F32), out_spec=pl.BlockSpec((TM, TDW), IJ), acc_shape=(TM, TDW), dims=NT,
        add=dh1, add_spec=pl.BlockSpec((TM, TDW), IJ))
    gw_hin = _matmul(
        "dwhin_qfi", h1, dqfi, grid=(D // TDW, nq, 1), a_spec=pl.BlockSpec((S, TDW), lambda i, j, k: (0, i)),
        b_spec=pl.BlockSpec((None, S, TNP), lambda i, j, k: (j // qt, 0, j % qt)),
        out_shape=sds((N_CHIPS, D, SH1), BF16), out_spec=pl.BlockSpec((None, TDW, TNP), lambda i, j, k: (j // c1, i, j % c1)),
        acc_shape=(TDW, TNP), dims=TN)
    gw_hin = _matmul(
        "dwhin_rest", h1, drest1, grid=(D // TDW, nr, 1), a_spec=pl.BlockSpec((S, TDW), lambda i, j, k: (0, i)),
        b_spec=pl.BlockSpec((S, TNP), lambda i, j, k: (0, j)), out_shape=sds((N_CHIPS, D, SH1), BF16),
        out_spec=pl.BlockSpec((None, TDW, TNP), lambda i, j, k: ((j + nq) // c1, i, (j + nq) % c1)),
        acc_shape=(TDW, TNP), dims=TN, alias=gw_hin)
    dmem, gw_kv1 = kv_bwd(1, dkv1, wkv1, None)

    core_chip = jnp.stack([lax.axis_index("c"), chip]).astype(jnp.int32)

    def reduce_in_chip(tag, stacks):
        got = _exchange_halves(f"exchange_halves{tag}", stacks)
        pairs = [_add_halves(f"add_halves{tag}_{t}", core_chip, g.reshape(N_CHIPS, 2, g.shape[1] // 2, g.shape[2]), r)
                 for t, (g, r) in enumerate(zip(stacks, got))]
        return [p for p, _ in pairs], [own for _, own in pairs]

    parts1, landed1 = reduce_in_chip(1, [gw_kv1, gw_out1, gw_hin])
    scatter1, token1 = _split_start("scatter1_start", parts1, landed1, _scatter_plan, 3 * len(parts1), None)
    dx1, dx1b, d_ng1 = _rms_bwd("rms_bwd1", dh1, x1, norm_g[1:2], dx2, token1)

    dpremix0, dproj0, dcol0, gw_out0 = out_proj_bwd(0, dx1b, branch0, wout0, premix0, proj0, gblk0, colscale0,
                                                    (S, NP0), gblk0)
    dproj0, dkv0 = _ca_bwd("ca_bwd0", dpremix0, proj0, EMIX // ECA, kv[0], dproj0, EMIX // ECA, S, ECA, EMIX)
    dmem, gw_kv0 = kv_bwd(0, dkv0, wkv0, dmem)
    parts_a, landed_a = reduce_in_chip("0a", [gw_kv0, gw_out0])
    scatter_a, token_a = _split_start("scatter0a_start", parts_a, landed_a, _scatter_plan, 3 * len(parts_a), None)
    dpooled = _matmul(
        "dpooled", dpremix0, wgrp, grid=(S // TM, N_POOL_GROUPS, 1), a_spec=pl.BlockSpec((TM, PG), IJ),
        b_spec=pl.BlockSpec((None, PG, PG), lambda i, j, k: (j, 0, 0)),
        out_shape=sds((S, EMIX), F32), out_spec=pl.BlockSpec((TM, PG), IJ), acc_shape=(TM, PG), dims=NT, after=token_a)
    def rows_by_chip(r, _, outs):
        outs[0][...] = r.reshape(N_CHIPS, PG // N_CHIPS, PG).astype(BF16)

    gw_grp, = _matmul(
        "dwgrp", pooled, dpremix0, grid=(N_POOL_GROUPS, 1, 1), a_spec=pl.BlockSpec((S, PG), lambda i, j, k: (0, i)),
        b_spec=pl.BlockSpec((S, PG), lambda i, j, k: (0, i)), epilogue=rows_by_chip,
        out_shape=[sds((N_CHIPS, N_POOL_GROUPS, PG // N_CHIPS, PG), BF16)],
        out_spec=[pl.BlockSpec((N_CHIPS, None, PG // N_CHIPS, PG), lambda i, j, k: (0, i, 0, 0))],
        acc_shape=(PG, PG), dims=TN)
    dproj0 = _pool_bwd(dpooled, dproj0, S, EMIX)
    gw_pin = _matmul(
        "dwpin", h0, dproj0, grid=(D // TDW, NP0 // TNP, 1), a_spec=pl.BlockSpec((S, TDW), lambda i, j, k: (0, i)),
        b_spec=pl.BlockSpec((S, TNP), lambda i, j, k: (0, j)), out_shape=sds((N_CHIPS, D, SH0), BF16),
        out_spec=pl.BlockSpec((None, TDW, TNP), lambda i, j, k: (j // c0, i, j % c0)), acc_shape=(TDW, TNP), dims=TN)
    parts_b, landed_b = reduce_in_chip("0b", [gw_pin, gw_grp.reshape(N_CHIPS, PG, PG)])
    scatter_b, token_b = _split_start("scatter0b_start", parts_b, landed_b, _scatter_plan, 3 * len(parts_b), None)
    dh0 = _matmul(
        "dh0", dproj0, wpin, grid=(S // TM, D // TDW, N_CHIPS), a_spec=pl.BlockSpec((TM, SH0), IK),
        b_spec=pl.BlockSpec((None, TDW, SH0), lambda i, j, k: (k, j, 0)),
        out_shape=sds((S, D), F32), out_spec=pl.BlockSpec((TM, TDW), IJ), acc_shape=(TM, TDW), dims=NT, after=token_b)
    grad_x, _, d_ng0 = _rms_bwd("rms_bwd0", dh0, xs, norm_g[0:1], dx1)
    _, _, d_mng = _rms_bwd("rms_bwd_mem", dmem, ms, mem_norm_g.reshape(1, D), jnp.zeros_like(ms))

    _, landed1 = _split_wait("scatter1_wait", scatter1, _scatter_plan, len(parts1), grad_x)
    _, landed_a = _split_wait("scatter0a_wait", scatter_a, _scatter_plan, len(parts_a), grad_x)
    _, landed_b = _split_wait("scatter0b_wait", scatter_b, _scatter_plan, len(parts_b), grad_x)
    landed = [landed_a[0], landed1[0], landed_a[1], landed1[1], landed_b[0], landed_b[1], landed1[2]]
    fulls = _sum_share(landed)
    f2 = [f.reshape(-1, f.shape[-1]) for f in fulls]
    grads, deltas, new_m, new_v = {}, {}, {}, {}
    for n, w, mm, vv, gs in (("w_kv", w_kv, m_w_kv, v_w_kv, f2[0:2]), ("w_out", w_out, m_w_out, v_w_out, f2[2:4]),
                             ("pool_w_in", pool_w_in, m_pool_w_in, v_pool_w_in, f2[4:5]),
                             ("pool_w_grp", pool_w_grp, m_pool_w_grp, v_pool_w_grp, f2[5:6]),
                             ("hgrn_w_in", hgrn_w_in, m_hgrn_w_in, v_hgrn_w_in, f2[6:7])):
        as3d = lambda a: a.reshape((a.shape[0], -1, a.shape[-1]))
        outs = _adamw_layers(f"adamw_{n}", as3d(w), gs, as3d(mm), as3d(vv))
        grads[n], deltas[n], new_m[n], new_v[n] = [o.reshape(w.shape) for o in outs]

    Wd = EMIX
    nshard = EMIX // N_CHIPS
    summed_rows = [(v, True, v.shape[1]) for v in (d_ng0, d_ng1, d_mng)] + [
        (dcol0, True, EMIX), (dlb, True, EMIX), (dcol1, True, EMIX), (d_final_g, True, D), (loss_part, True, 128)]
    partial = _pack_rows("pack_partials", summed_rows, Wd)
    g_pack, loss = _small_sum(_allgather_small("allgather_grads", partial), hgrn_lb, chip1, nshard)

    def pack_small(name, ng, mng, ps, lb_, hn, fg):
        return _pack_rows(name, [(ng, False, D), (mng.reshape(1, D), False, D), (ps, False, EMIX), (lb_, False, EMIX),
                                 (hn, False, nshard), (fg.reshape(1, D), False, D)], Wd)

    d_pack, m_pack, v_pack = _adamw(
        "adamw_small", pack_small("pack_small_w", norm_g, mem_norm_g, pool_scale, hgrn_lb, hgrn_norm_g, final_g), g_pack,
        pack_small("pack_small_m", m_norm_g, m_mem_norm_g, m_pool_scale, m_hgrn_lb, m_hgrn_norm_g, m_final_g),
        pack_small("pack_small_v", v_norm_g, v_mem_norm_g, v_pool_scale, v_hgrn_lb, v_hgrn_norm_g, v_final_g))
    layout = [(2, D), (1, D), (1, EMIX), (2, EMIX), (1, nshard), (1, D)]
    for tag, pack, out in (("g", g_pack, grads), ("d", d_pack, deltas), ("m", m_pack, new_m), ("v", v_pack, new_v)):
        ng, mng, ps, lb_, hn, fg = _unpack_rows(f"unpack_small_{tag}", pack, layout)
        out.update(norm_g=ng, mem_norm_g=mng.reshape(D), pool_scale=ps, hgrn_lb=lb_, hgrn_norm_g=hn, final_g=fg.reshape(D))
    loss = loss[0, 0]

    order = ["norm_g", "mem_norm_g", "w_kv", "w_out", "pool_w_in", "pool_w_grp", "pool_scale", "hgrn_w_in", "hgrn_lb",
             "hgrn_norm_g", "final_g"]
    return (loss, grad_x.reshape(1, S, D), *[grads[n] for n in order], *[deltas[n] for n in order],
            *[new_m[n] for n in order], *[new_v[n] for n in order])
```

```python
import functools

import jax
import jax.numpy as jnp
from jax import lax
from jax.experimental import pallas as pl
from jax.experimental.pallas import tpu as pltpu

F32 = jnp.float32
BF16 = jnp.bfloat16
MESH = pl.DeviceIdType.MESH
ANY = pl.BlockSpec(memory_space=pl.ANY)

EPS = 1e-6
HG_HEAD_DIM = 128
HG_CHUNK = 64
CA_HEADS = 4
N_POOL_GROUPS = 4
POOL_HALO = 128
ADAM_LR = 0.001
ADAM_B1 = 0.9
ADAM_B2 = 0.999
ADAM_EPS = 1e-08
ADAM_WD = 0.01
ADAM_STEP = 10
N_CHIPS = 4
N_DEV = 8
VMEM_LIMIT_BYTES = 56 * 1024 * 1024
SMALL_ROWS = 8
STREAM_CHUNK_BYTES = 4 * 1024 * 1024
STREAM_SLOTS = 3
SUM_SLOTS = 2
RING_SLOTS = 3


def _params(*sem):
    return pltpu.CompilerParams(dimension_semantics=sem, vmem_limit_bytes=VMEM_LIMIT_BYTES)


def _tile(n, pref):
    t = pref
    while n % t:
        t //= 2
    return t


def _sigmoid(x):
    return 1.0 / (1.0 + jnp.exp(-x))


def _matmul(name, a, b, *, grid, a_spec, b_spec, out_shape, out_spec, acc_shape, dims,
            add=None, add_spec=None, alias=None, after=None, extras=(), epilogue=None):
    nk = grid[2]
    has_add = add is not None
    has_alias = alias is not None
    has_after = after is not None
    n_out = len(out_shape) if epilogue is not None else 1

    def body(*refs):
        a_ref, b_ref = refs[0], refs[1]
        pos = 2
        add_ref = None
        if has_add:
            add_ref = refs[pos]
            pos += 1
        extra_refs = refs[pos:pos + len(extras)]
        pos += len(extras) + has_alias + has_after
        o_refs = refs[pos:pos + n_out]
        prod = lax.dot_general(a_ref[...], b_ref[...], (dims, ((), ())), preferred_element_type=F32)

        def finish(r):
            if epilogue is not None:
                epilogue(r, extra_refs, o_refs)
                return
            if has_add:
                r = r + add_ref[...].astype(F32)
            o_refs[0][...] = r.astype(o_refs[0].dtype)

        if nk == 1:
            finish(prod)
            return
        acc_ref = refs[pos + n_out]
        k = pl.program_id(2)

        @pl.when(k == 0)
        def _():
            acc_ref[...] = prod

        @pl.when(k > 0)
        def _():
            acc_ref[...] += prod

        @pl.when(k == nk - 1)
        def _():
            finish(acc_ref[...])

    operands = [a, b]
    in_specs = [a_spec, b_spec]
    if has_add:
        operands.append(add)
        in_specs.append(add_spec)
    for arr, spec in extras:
        operands.append(arr)
        in_specs.append(spec)
    aliases = {}
    if has_alias:
        aliases = {len(operands): 0}
        operands.append(alias)
        in_specs.append(ANY)
    if has_after:
        operands.append(after)
        in_specs.append(ANY)
    return pl.pallas_call(
        body, name=name, grid=grid, in_specs=in_specs, out_specs=out_spec, out_shape=out_shape,
        scratch_shapes=[pltpu.VMEM(acc_shape, F32)] if nk > 1 else [], input_output_aliases=aliases,
        compiler_params=_params("parallel", "parallel", "arbitrary"),
    )(*operands)


IJ = lambda i, j, k: (i, j)
IK = lambda i, j, k: (i, k)
KJ = lambda i, j, k: (k, j)
KI = lambda i, j, k: (k, i)
NN = ((1,), (0,))
NT = ((1,), (1,))
TN = ((0,), (0,))


def _rms_fwd(name, x, g, after=None):
    R, D = x.shape
    tr = _tile(R, 256)
    extra = [] if after is None else [after]

    def body(x_ref, g_ref, *rest):
        xf = x_ref[...]
        r = lax.rsqrt(jnp.mean(xf * xf, axis=-1, keepdims=True) + EPS)
        rest[-1][...] = (xf * r * g_ref[...]).astype(BF16)

    return pl.pallas_call(
        body, name=name, grid=(R // tr,),
        in_specs=[pl.BlockSpec((tr, D), lambda i: (i, 0)), pl.BlockSpec((1, D), lambda i: (0, 0))] + [ANY] * len(extra),
        out_specs=pl.BlockSpec((tr, D), lambda i: (i, 0)),
        out_shape=jax.ShapeDtypeStruct((R, D), BF16), compiler_params=_params("parallel"),
    )(x, g, *extra)


def _rms_bwd(name, dh, x, g, dres, after=None):
    R, D = x.shape
    tr = _tile(R, 256)
    extra = [] if after is None else [after]

    def body(dh_ref, x_ref, g_ref, dres_ref, *rest):
        dx_ref, dxb_ref, dg_ref = rest[len(extra):]
        xf = x_ref[...]
        r = lax.rsqrt(jnp.mean(xf * xf, axis=-1, keepdims=True) + EPS)
        xn = xf * r
        d = dh_ref[...]
        dyg = d * g_ref[...]
        dx = r * (dyg - xn * jnp.mean(dyg * xn, axis=-1, keepdims=True)) + dres_ref[...]
        dx_ref[...] = dx
        dxb_ref[...] = dx.astype(BF16)

        @pl.when(pl.program_id(0) == 0)
        def _():
            dg_ref[...] = jnp.zeros_like(dg_ref)

        dg_ref[...] += jnp.sum(d * xn, axis=0, keepdims=True)

    row = pl.BlockSpec((tr, D), lambda i: (i, 0))
    vec = pl.BlockSpec((1, D), lambda i: (0, 0))
    return pl.pallas_call(
        body, name=name, grid=(R // tr,), in_specs=[row, row, vec, row] + [ANY] * len(extra), out_specs=[row, row, vec],
        out_shape=[jax.ShapeDtypeStruct((R, D), F32), jax.ShapeDtypeStruct((R, D), BF16),
                   jax.ShapeDtypeStruct((1, D), F32)],
        compiler_params=_params("arbitrary"),
    )(dh, x, g, dres, *extra)


def _loss_head(x2, g, target):
    R, D = x2.shape
    tr = _tile(R, 256)
    nsteps = R // tr

    def body(x_hbm, g_ref, t_hbm, dx_ref, dxb_ref, dg_ref, loss_ref, xbuf, tbuf, sems):
        step = pl.program_id(0)

        def tiles(k):
            slot = k % RING_SLOTS
            rows = pl.ds(pl.multiple_of(k * tr, tr), tr)
            return (pltpu.make_async_copy(x_hbm.at[rows], xbuf.at[slot], sems.at[0, slot]),
                    pltpu.make_async_copy(t_hbm.at[rows], tbuf.at[slot], sems.at[1, slot]))

        @pl.when(step == 0)
        def _():
            for first in range(min(RING_SLOTS - 1, nsteps)):
                for cp in tiles(step + first):
                    cp.start()

        @pl.when(step + RING_SLOTS - 1 < nsteps)
        def _():
            for cp in tiles(step + RING_SLOTS - 1):
                cp.start()

        for cp in tiles(step):
            cp.wait()
        x_ref, t_ref = xbuf.at[step % RING_SLOTS], tbuf.at[step % RING_SLOTS]
        xf = x_ref[...]
        gg = g_ref[...]
        r = lax.rsqrt(jnp.mean(xf * xf, axis=-1, keepdims=True) + EPS)
        xn = xf * r
        e = xn * gg - t_ref[...]
        part = 0.5 * jnp.sum(jnp.mean(e * e, axis=-1, keepdims=True), axis=0, keepdims=True)
        dy = e * (1.0 / D)
        dyg = dy * gg
        dx = r * (dyg - xn * jnp.mean(dyg * xn, axis=-1, keepdims=True))
        dx_ref[...] = dx
        dxb_ref[...] = dx.astype(BF16)

        @pl.when(pl.program_id(0) == 0)
        def _():
            dg_ref[...] = jnp.zeros_like(dg_ref)
            loss_ref[...] = jnp.zeros_like(loss_ref)

        dg_ref[...] += jnp.sum(dy * xn, axis=0, keepdims=True)
        loss_ref[...] += jnp.broadcast_to(part, loss_ref.shape)

    row = pl.BlockSpec((tr, D), lambda i: (i, 0))
    vec = pl.BlockSpec((1, D), lambda i: (0, 0))
    return pl.pallas_call(
        body, name="loss_head", grid=(R // tr,), in_specs=[ANY, vec, ANY],
        out_specs=[row, row, vec, pl.BlockSpec((1, 128), lambda i: (0, 0))],
        out_shape=[jax.ShapeDtypeStruct((R, D), F32), jax.ShapeDtypeStruct((R, D), BF16),
                   jax.ShapeDtypeStruct((1, D), F32), jax.ShapeDtypeStruct((1, 128), F32)],
        scratch_shapes=[pltpu.VMEM((RING_SLOTS, tr, D), F32), pltpu.VMEM((RING_SLOTS, tr, D), F32),
                        pltpu.SemaphoreType.DMA((2, RING_SLOTS))],
        compiler_params=_params("arbitrary"),
    )(x2, g, target)


def _pool_band(tr, reverse, w):
    r = lax.broadcasted_iota(jnp.int32, (tr, tr + POOL_HALO), 0)
    c = lax.broadcasted_iota(jnp.int32, (tr, tr + POOL_HALO), 1)
    if reverse:
        inside = (c >= r) & (c < r + w)
    else:
        cc = c - POOL_HALO
        inside = (cc <= r) & (cc > r - w)
    return jnp.where(inside, 1.0, 0.0).astype(BF16)


def _pool_fwd(proj, S, EMIX):
    PG = EMIX // N_POOL_GROUPS
    cb = PG
    tr = _tile(S, 512)
    per_group = PG // cb

    def body(u_ref, o_ref, ext):
        i = pl.program_id(1)
        w = jnp.left_shift(2, pl.program_id(0) // per_group)

        @pl.when(i == 0)
        def _():
            ext[0:POOL_HALO, :] = jnp.zeros((POOL_HALO, cb), BF16)

        u = u_ref[...]
        ext[POOL_HALO:, :] = u
        win = jnp.dot(_pool_band(tr, False, w), ext[...], preferred_element_type=F32)
        pos = i * tr + lax.broadcasted_iota(jnp.int32, (tr, 1), 0)
        cnt = jnp.minimum(pos + 1, w).astype(F32)
        o_ref[...] = (win / cnt - u.astype(F32)).astype(BF16)
        ext[0:POOL_HALO, :] = u[tr - POOL_HALO:, :]

    return pl.pallas_call(
        body, name="pool_fwd", grid=(EMIX // cb, S // tr),
        in_specs=[pl.BlockSpec((tr, cb), lambda j, i: (i, j))],
        out_specs=pl.BlockSpec((tr, cb), lambda j, i: (i, j)),
        out_shape=jax.ShapeDtypeStruct((S, EMIX), BF16),
        scratch_shapes=[pltpu.VMEM((tr + POOL_HALO, cb), BF16)],
        compiler_params=_params("parallel", "arbitrary"),
    )(proj)


def _pool_bwd(dpooled, dproj, S, EMIX):
    PG = EMIX // N_POOL_GROUPS
    cb = PG
    tr = _tile(S, 512)
    per_group = PG // cb
    nrt = S // tr

    def body(d_ref, _, o_ref, ext):
        step = pl.program_id(1)
        i = nrt - 1 - step
        w = jnp.left_shift(2, pl.program_id(0) // per_group)

        @pl.when(step == 0)
        def _():
            ext[tr:, :] = jnp.zeros((POOL_HALO, cb), BF16)

        d = d_ref[...]
        pos = i * tr + lax.broadcasted_iota(jnp.int32, (tr, 1), 0)
        cnt = jnp.minimum(pos + 1, w).astype(F32)
        z = (d / cnt).astype(BF16)
        ext[0:tr, :] = z
        win = jnp.dot(_pool_band(tr, True, w), ext[...], preferred_element_type=F32)
        o_ref[...] = (win - d).astype(BF16)
        ext[tr:, :] = z[0:POOL_HALO, :]

    return pl.pallas_call(
        body, name="pool_bwd", grid=(EMIX // cb, nrt),
        in_specs=[pl.BlockSpec((tr, cb), lambda j, s: (nrt - 1 - s, j)), ANY],
        out_specs=pl.BlockSpec((tr, cb), lambda j, s: (nrt - 1 - s, j)),
        out_shape=jax.ShapeDtypeStruct(dproj.shape, dproj.dtype),
        scratch_shapes=[pltpu.VMEM((tr + POOL_HALO, cb), BF16)],
        input_output_aliases={1: 0},
        compiler_params=_params("parallel", "arbitrary"),
    )(dpooled, dproj)


def _ca_fwd(name, proj, qblk, kv, premix, S, ECA, EMIX):
    M = kv.shape[0]
    hd = ECA // CA_HEADS
    ts = _tile(S, 512)
    scale = hd ** -0.5

    def body(q_ref, kv_ref, _, o_ref):
        for h in range(CA_HEADS):
            q = q_ref[:, h * hd:(h + 1) * hd]
            k = kv_ref[:, h * hd:(h + 1) * hd]
            v = kv_ref[:, ECA + h * hd:ECA + (h + 1) * hd]
            s = lax.dot_general(q, k, (NT, ((), ())), preferred_element_type=F32) * scale
            s = s - jnp.max(s, axis=-1, keepdims=True)
            p = jnp.exp(s)
            p = p / jnp.sum(p, axis=-1, keepdims=True)
            o = jnp.dot(p.astype(BF16), v, preferred_element_type=F32)
            o_ref[:, h * hd:(h + 1) * hd] = o.astype(BF16)

    return pl.pallas_call(
        body, name=name, grid=(S // ts,),
        in_specs=[pl.BlockSpec((ts, ECA), lambda i: (i, qblk)), pl.BlockSpec((M, 2 * ECA), lambda i: (0, 0)), ANY],
        out_specs=pl.BlockSpec((ts, ECA), lambda i: (i, EMIX // ECA)),
        out_shape=jax.ShapeDtypeStruct(premix.shape, premix.dtype),
        input_output_aliases={2: 0}, compiler_params=_params("parallel"),
    )(proj, kv, premix)


def _ca_bwd(name, dpremix, proj, qblk, kv, dbuf, dblk, S, ECA, EMIX):
    M = kv.shape[0]
    hd = ECA // CA_HEADS
    ts = _tile(S, 512)
    scale = hd ** -0.5

    def body(do_ref, q_ref, kv_ref, _, dq_ref, dkv_ref):
        @pl.when(pl.program_id(0) == 0)
        def _():
            dkv_ref[...] = jnp.zeros_like(dkv_ref)

        for h in range(CA_HEADS):
            lo, hi = h * hd, (h + 1) * hd
            q = q_ref[:, lo:hi]
            k = kv_ref[:, lo:hi]
            v = kv_ref[:, ECA + lo:ECA + hi]
            do = do_ref[:, lo:hi]
            s = lax.dot_general(q, k, (NT, ((), ())), preferred_element_type=F32) * scale
            s = s - jnp.max(s, axis=-1, keepdims=True)
            p = jnp.exp(s)
            p = p / jnp.sum(p, axis=-1, keepdims=True)
            pb = p.astype(BF16)
            dkv_ref[:, ECA + lo:ECA + hi] += lax.dot_general(pb, do, (TN, ((), ())), preferred_element_type=F32)
            dp = lax.dot_general(do, v, (NT, ((), ())), preferred_element_type=F32)
            ds = (p * (dp - jnp.sum(p * dp, axis=-1, keepdims=True)) * scale).astype(BF16)
            dq_ref[:, lo:hi] = jnp.dot(ds, k, preferred_element_type=F32).astype(BF16)
            dkv_ref[:, lo:hi] += lax.dot_general(ds, q, (TN, ((), ())), preferred_element_type=F32)

    return pl.pallas_call(
        body, name=name, grid=(S // ts,),
        in_specs=[pl.BlockSpec((ts, ECA), lambda i: (i, EMIX // ECA)), pl.BlockSpec((ts, ECA), lambda i: (i, qblk)),
                  pl.BlockSpec((M, 2 * ECA), lambda i: (0, 0)), ANY],
        out_specs=[pl.BlockSpec((ts, ECA), lambda i: (i, dblk)), pl.BlockSpec((M, 2 * ECA), lambda i: (0, 0))],
        out_shape=[jax.ShapeDtypeStruct(dbuf.shape, dbuf.dtype), jax.ShapeDtypeStruct((M, 2 * ECA), F32)],
        input_output_aliases={3: 0}, compiler_params=_params("arbitrary"),
    )(dpremix, proj, kv, dbuf)


def _gate_fwd(name, premix, proj, gblk, colscale, S, EB, ECA):
    ts = _tile(S, 512)
    nj = EB // ECA
    nsteps = (S // ts) * nj

    def body(p_hbm, g_hbm, c_ref, o_ref, pbuf, gbuf, sems):
        s = pl.program_id(0) * nj + pl.program_id(1)

        def tiles(step):
            slot = step % RING_SLOTS
            rows = pl.ds(pl.multiple_of((step // nj) * ts, ts), ts)
            col = pl.multiple_of((step % nj) * ECA, 128)
            return (pltpu.make_async_copy(p_hbm.at[rows, pl.ds(col, ECA)], pbuf.at[slot], sems.at[0, slot]),
                    pltpu.make_async_copy(g_hbm.at[rows, pl.ds(gblk * ECA + col, ECA)], gbuf.at[slot], sems.at[1, slot]))

        @pl.when(s == 0)
        def _():
            for first in range(min(RING_SLOTS - 1, nsteps)):
                for cp in tiles(s + first):
                    cp.start()

        @pl.when(s + RING_SLOTS - 1 < nsteps)
        def _():
            for cp in tiles(s + RING_SLOTS - 1):
                cp.start()

        for cp in tiles(s):
            cp.wait()
        slot = s % RING_SLOTS
        g = gbuf[slot].astype(F32)
        o_ref[...] = (pbuf[slot].astype(F32) * c_ref[...] * (g * _sigmoid(g))).astype(BF16)

    return pl.pallas_call(
        body, name=name, grid=(S // ts, nj),
        in_specs=[ANY, ANY, pl.BlockSpec((1, ECA), lambda i, j: (0, j))],
        out_specs=pl.BlockSpec((ts, ECA), lambda i, j: (i, j)),
        out_shape=jax.ShapeDtypeStruct((S, EB), BF16),
        scratch_shapes=[pltpu.VMEM((RING_SLOTS, ts, ECA), BF16), pltpu.VMEM((RING_SLOTS, ts, ECA), BF16),
                        pltpu.SemaphoreType.DMA((2, RING_SLOTS))],
        compiler_params=_params("arbitrary", "arbitrary"),
    )(premix, proj, colscale)


def _gate_bwd_epilogue(db, extra_refs, out_refs):
    p_ref, g_ref, c_ref = extra_refs
    dp_ref, dg_ref, dc_ref = out_refs
    g = g_ref[...].astype(F32)
    sg = _sigmoid(g)
    si = g * sg
    c = c_ref[...]
    t = db * p_ref[...].astype(F32)
    dp_ref[...] = (db * si * c).astype(BF16)
    dg_ref[...] = (t * c * (sg * (1.0 + g * (1.0 - sg)))).astype(BF16)
    dc_ref[...] = jnp.sum(t * si, axis=0, keepdims=True)


def _hgrn_lb(lb_ref):
    l0 = lb_ref[0:1, :]
    l1 = lb_ref[1:2, :]
    mx = jnp.maximum(l0, l1)
    e0 = jnp.exp(l0 - mx)
    e1 = jnp.exp(l1 - mx)
    return e1 / (e0 + e1)


def _bdot(a, b, ca, cb):
    return lax.dot_general(a, b, (((ca,), (cb,)), ((0,), (0,))), preferred_element_type=F32)


def _tri_sum(tri, x):
    hi = x.astype(BF16)
    lo = (x - hi.astype(F32)).astype(BF16)
    tri = tri.astype(BF16)
    return _bdot(tri, hi, 2, 1) + _bdot(tri, lo, 2, 1)


def _hgrn_chunks(qin, fin, lbh, n):
    C = HG_CHUNK
    row = lax.broadcasted_iota(jnp.int32, (n, C, C), 1)
    col = lax.broadcasted_iota(jnp.int32, (n, C, C), 2)
    causal = row >= col
    sg = _sigmoid(fin)
    f = lbh + (1.0 - lbh) * sg
    k = 1.0 - f
    g = jnp.log(f)
    b = _tri_sum(jnp.where(causal, 1.0, 0.0), g)
    b_last = jnp.sum(g, axis=1, keepdims=True)
    eb = jnp.exp(b)
    einv = jnp.exp(-b)
    eend = jnp.exp(b_last - b)
    sq = _sigmoid(qin)
    a = qin * sq * (HG_HEAD_DIM ** -0.5) * eb
    bm = k * einv
    e = k * eend
    d = jnp.exp(b_last)
    p = jnp.where(causal, _bdot(a.astype(BF16), bm.astype(BF16), 2, 2), 0.0)
    return dict(causal=causal, sg=sg, f=f, eb=eb, einv=einv, eend=eend, sq=sq, a=a, bm=bm, e=e, d=d, p=p)


def _hgrn_fwd(proj, fgate, hgrn_lb, S, EMIX, EB):
    HD, C = HG_HEAD_DIM, HG_CHUNK
    HH = EMIX // HD
    hb = 6 if HH % 6 == 0 else 1
    W = hb * HD
    tr = _tile(S, 512)
    n = tr // C

    def body(q_ref, f_ref, i_ref, lb_ref, o_ref, rstd_ref, st_ref, state):
        @pl.when(pl.program_id(1) == 0)
        def _():
            state[...] = jnp.zeros_like(state)

        lb = _hgrn_lb(lb_ref)
        for h in range(hb):
            cs = slice(h * HD, (h + 1) * HD)
            qin = q_ref[:, cs].astype(F32).reshape(n, C, HD)
            fin = f_ref[:, cs].reshape(n, C, HD)
            v = i_ref[:, cs].reshape(n, C, HD)
            t = _hgrn_chunks(qin, fin, lb[:, cs], n)
            upd = _bdot(v, t["e"].astype(BF16), 1, 1)
            st = state[h]
            for c in range(n):
                st_ref[h, c] = st
                st = st * t["d"][c] + upd[c]
            state[h] = st
            o = _bdot(t["p"].astype(BF16), v, 2, 1) + _bdot(t["a"].astype(BF16), st_ref[h].astype(BF16), 2, 2)
            rstd = lax.rsqrt(jnp.mean(o * o, axis=-1, keepdims=True) + EPS)
            o_ref[:, cs] = (o * rstd).reshape(tr, HD).astype(BF16)
            rstd_ref[:, cs] = jnp.broadcast_to(rstd, (n, C, HD)).reshape(tr, HD)

    blk = lambda off: pl.BlockSpec((tr, W), lambda g, i: (i, off + g))
    return pl.pallas_call(
        body, name="hgrn_fwd", grid=(HH // hb, S // tr),
        in_specs=[blk(0), blk(0), blk(2 * EMIX // W), pl.BlockSpec((2, W), lambda g, i: (0, g))],
        out_specs=[blk(0), blk(0), pl.BlockSpec((hb, n, HD, HD), lambda g, i: (g, i, 0, 0))],
        out_shape=[jax.ShapeDtypeStruct((S, EB), BF16), jax.ShapeDtypeStruct((S, EMIX), F32),
                   jax.ShapeDtypeStruct((HH, S // C, HD, HD), F32)],
        scratch_shapes=[pltpu.VMEM((hb, HD, HD), F32)],
        compiler_params=_params("parallel", "arbitrary"),
    )(proj, fgate, proj, hgrn_lb)


def _hgrn_bwd(dpremix, premix, rstd, states, proj, fgate, hgrn_lb, S, EMIX):
    HD, C = HG_HEAD_DIM, HG_CHUNK
    HH = EMIX // HD
    hb = 6 if HH % 6 == 0 else 1
    W = hb * HD
    tr = _tile(S, 512)
    n = tr // C
    nrt = S // tr

    def body(do_ref, on_ref, rstd_ref, st_ref, q_ref, f_ref, i_ref, lb_ref, d_ref, dlb_ref, dstate, dsbuf):
        @pl.when(pl.program_id(1) == 0)
        def _():
            dstate[...] = jnp.zeros_like(dstate)
            dlb_ref[...] = jnp.zeros_like(dlb_ref)

        lb = _hgrn_lb(lb_ref)
        for h in range(hb):
            cs = slice(h * HD, (h + 1) * HD)
            qin = q_ref[:, cs].astype(F32).reshape(n, C, HD)
            fin = f_ref[:, cs].reshape(n, C, HD)
            v = i_ref[:, cs].reshape(n, C, HD)
            lbh = lb[:, cs]
            t = _hgrn_chunks(qin, fin, lbh, n)
            a, bm, e, d, p = t["a"], t["bm"], t["e"], t["d"], t["p"]
            ab, bmb, eb16 = a.astype(BF16), bm.astype(BF16), e.astype(BF16)
            on = on_ref[:, cs].astype(F32).reshape(n, C, HD)
            dn = do_ref[:, cs].astype(F32).reshape(n, C, HD)
            do = rstd_ref[:, cs].reshape(n, C, HD) * (dn - on * jnp.mean(dn * on, axis=-1, keepdims=True))
            dob = do.astype(BF16)
            grow = _bdot(dob, ab, 1, 1)
            ds = dstate[h]
            for c in reversed(range(n)):
                dsbuf[h, c] = ds
                ds = ds * d[c] + grow[c]
            dstate[h] = ds
            dst = dsbuf[h]
            st = st_ref[h]
            dstb = dst.astype(BF16)
            dp = jnp.where(t["causal"], _bdot(dob, v, 2, 2), 0.0).astype(BF16)
            dv = _bdot(p.astype(BF16), dob, 1, 1) + _bdot(eb16, dstb, 2, 2)
            da = _bdot(dp, bmb, 2, 1) + _bdot(dob, st.astype(BF16), 2, 1)
            dbm = _bdot(dp, ab, 1, 1)
            de = _bdot(v, dstb, 2, 1)
            dd = jnp.sum(dst * st, axis=1, keepdims=True)
            dk = dbm * t["einv"] + de * t["eend"]
            dee = de * e
            db = da * a - dbm * bm - dee
            extra = jnp.sum(dee, axis=1, keepdims=True) + dd * d
            upper = jnp.where(lax.broadcasted_iota(jnp.int32, (n, C, C), 2)
                              >= lax.broadcasted_iota(jnp.int32, (n, C, C), 1), 1.0, 0.0)
            dg = _tri_sum(upper, db) + extra
            df = dg / t["f"] - dk
            sg, sq = t["sg"], t["sq"]
            dq = da * t["eb"] * (HD ** -0.5) * (sq * (1.0 + qin * (1.0 - sq)))
            d_ref[0, :, cs] = dq.reshape(tr, HD).astype(BF16)
            d_ref[1, :, cs] = (df * (1.0 - lbh) * sg * (1.0 - sg)).reshape(tr, HD).astype(BF16)
            d_ref[2, :, cs] = dv.reshape(tr, HD).astype(BF16)
            dlb_ref[:, cs] += jnp.sum((df * (1.0 - sg)).reshape(tr, HD), axis=0, keepdims=True)

    rev = lambda off: pl.BlockSpec((tr, W), lambda g, s: (nrt - 1 - s, off + g))
    return pl.pallas_call(
        body, name="hgrn_bwd", grid=(HH // hb, nrt),
        in_specs=[rev(0), rev(0), rev(0), pl.BlockSpec((hb, n, HD, HD), lambda g, s: (g, nrt - 1 - s, 0, 0)),
                  rev(0), rev(0), rev(2 * EMIX // W), pl.BlockSpec((2, W), lambda g, s: (0, g))],
        out_specs=[pl.BlockSpec((3, tr, W), lambda g, s: (0, nrt - 1 - s, g)), pl.BlockSpec((1, W), lambda g, s: (0, g))],
        out_shape=[jax.ShapeDtypeStruct((3, S, EMIX), BF16), jax.ShapeDtypeStruct((1, EMIX), F32)],
        scratch_shapes=[pltpu.VMEM((hb, HD, HD), F32), pltpu.VMEM((hb, n, HD, HD), F32)],
        compiler_params=_params("parallel", "arbitrary"),
    )(dpremix, premix, rstd, states, proj, fgate, proj, hgrn_lb)


EW_BLOCK_ELEMS = 512 * 1024


def _ew_tiles(R, C):
    tc = C if C <= 4096 else _tile(C, 2048)
    tr = _tile(R, 512)
    while tr * tc > EW_BLOCK_ELEMS and tr % 16 == 0:
        tr //= 2
    return tr, tc


def _add_halves(name, core_chip, grad, got):
    _, _, R, C = grad.shape
    tr, tc = _ew_tiles(R, C)
    nc = C // tc
    nsteps = (R // tr) * nc * N_CHIPS

    def body(c_ref, a_hbm, b_hbm, o_ref, own_ref, abuf, bbuf, sems):
        step = (pl.program_id(0) * nc + pl.program_id(1)) * N_CHIPS + pl.program_id(2)

        def tiles(k):
            slot = k % RING_SLOTS
            chip_slot, tile = k % N_CHIPS, k // N_CHIPS
            rows = pl.ds(pl.multiple_of((tile // nc) * tr, tr), tr)
            cols = pl.ds(pl.multiple_of((tile % nc) * tc, 128), tc)
            return (pltpu.make_async_copy(a_hbm.at[chip_slot, c_ref[0], rows, cols], abuf.at[slot], sems.at[0, slot]),
                    pltpu.make_async_copy(b_hbm.at[chip_slot, rows, cols], bbuf.at[slot], sems.at[1, slot]))

        @pl.when(step == 0)
        def _():
            for first in range(min(RING_SLOTS - 1, nsteps)):
                for cp in tiles(step + first):
                    cp.start()

        @pl.when(step + RING_SLOTS - 1 < nsteps)
        def _():
            for cp in tiles(step + RING_SLOTS - 1):
                cp.start()

        for cp in tiles(step):
            cp.wait()
        slot = step % RING_SLOTS
        r = (abuf[slot].astype(F32) + bbuf[slot].astype(F32)).astype(BF16)
        o_ref[...] = r

        @pl.when(pl.program_id(2) == c_ref[1])
        def _():
            own_ref[...] = r

    blk = pl.BlockSpec((None, tr, tc), lambda i, j, s, c: (s, i, j))
    sds = jax.ShapeDtypeStruct(got.shape, BF16)
    return pl.pallas_call(
        body, name=name, out_shape=[sds, sds],
        grid_spec=pltpu.PrefetchScalarGridSpec(
            num_scalar_prefetch=1, grid=(R // tr, nc, N_CHIPS), in_specs=[ANY, ANY],
            out_specs=[blk, pl.BlockSpec((None, tr, tc), lambda i, j, s, c: (c[1], i, j))],
            scratch_shapes=[pltpu.VMEM((RING_SLOTS, tr, tc), BF16), pltpu.VMEM((RING_SLOTS, tr, tc), BF16),
                            pltpu.SemaphoreType.DMA((2, RING_SLOTS))]),
        compiler_params=_params("arbitrary", "arbitrary", "arbitrary"),
    )(core_chip, grad, got)


def _adam_step(w, g, m, v):
    mn = ADAM_B1 * m + (1.0 - ADAM_B1) * g
    vn = ADAM_B2 * v + (1.0 - ADAM_B2) * (g * g)
    m_hat = mn / (1.0 - ADAM_B1 ** ADAM_STEP)
    v_hat = vn / (1.0 - ADAM_B2 ** ADAM_STEP)
    return -ADAM_LR * (m_hat / (jnp.sqrt(v_hat) + ADAM_EPS) + ADAM_WD * w), mn, vn


def _adamw(name, w, g, m, v):
    R, C = w.shape
    tr, tc = _ew_tiles(R, C)

    def body(w_ref, g_ref, m_ref, v_ref, d_ref, mo_ref, vo_ref):
        d_ref[...], mo_ref[...], vo_ref[...] = _adam_step(w_ref[...], g_ref[...], m_ref[...], v_ref[...])

    blk = pl.BlockSpec((tr, tc), lambda i, j: (i, j))
    sds = jax.ShapeDtypeStruct((R, C), F32)
    return pl.pallas_call(
        body, name=name, grid=(R // tr, C // tc), in_specs=[blk] * 4, out_specs=[blk] * 3, out_shape=[sds] * 3,
        compiler_params=_params("parallel", "parallel"),
    )(w, g, m, v)


def _adamw_layers(name, w, gs, m, v):
    L, R, C = w.shape
    tr, tc = _ew_tiles(R, C)

    def body(*refs):
        w_ref, m_ref, v_ref = refs[:3]
        g_refs = refs[3:3 + L]
        go_ref, d_ref, mo_ref, vo_ref = refs[3 + L:]
        layer = pl.program_id(0)
        g = g_refs[0][...]
        for n in range(1, L):
            g = jnp.where(layer == n, g_refs[n][...], g)
        go_ref[...] = g
        d_ref[...], mo_ref[...], vo_ref[...] = _adam_step(w_ref[...], g, m_ref[...], v_ref[...])

    blk = pl.BlockSpec((None, tr, tc), lambda l, i, j: (l, i, j))
    of_layer = lambda n: pl.BlockSpec((tr, tc), lambda l, i, j: (jnp.where(l == n, i, 0), jnp.where(l == n, j, 0)))
    sds = jax.ShapeDtypeStruct((L, R, C), F32)
    return pl.pallas_call(
        body, name=name, grid=(L, R // tr, C // tc), in_specs=[blk] * 3 + [of_layer(n) for n in range(L)],
        out_specs=[blk] * 4, out_shape=[sds] * 4, compiler_params=_params("parallel", "parallel", "parallel"),
    )(w, m, v, *gs)


def _pack_rows(name, items, W):
    nv = len(items)
    first, row = [], 0
    for a, add, _ in items:
        first.append(row)
        row += 1 if add else a.shape[0]
    assert row <= SMALL_ROWS

    def body(*refs):
        o_ref = refs[nv]
        o_ref[...] = jnp.zeros_like(o_ref)
        for i, (a, add, width) in enumerate(items):
            val = refs[i][:, 0:width]
            if add:
                val = jnp.sum(val, axis=0, keepdims=True)
            o_ref[first[i]:first[i] + val.shape[0], 0:width] = val

    vm = pl.BlockSpec(memory_space=pltpu.VMEM)
    return pl.pallas_call(
        body, name=name, in_specs=[vm] * nv, out_specs=vm, out_shape=jax.ShapeDtypeStruct((SMALL_ROWS, W), F32),
    )(*[a for a, _, _ in items])


def _unpack_rows(name, pack, layout):
    def body(p_ref, *o_refs):
        row = 0
        for o_ref, (k, n) in zip(o_refs, layout):
            o_ref[...] = p_ref[row:row + k, 0:n]
            row += k

    vm = pl.BlockSpec(memory_space=pltpu.VMEM)
    return pl.pallas_call(
        body, name=name, in_specs=[vm], out_specs=[vm] * len(layout),
        out_shape=[jax.ShapeDtypeStruct(s, F32) for s in layout],
    )(pack)


def _small_sum(gathered, hgrn_lb, chip, nshard):
    _, T, W = gathered.shape

    def body(c_ref, g_ref, lb_ref, o_ref, loss_ref, tmp):
        acc = g_ref[0]
        for dev in range(1, N_DEV):
            acc = acc + g_ref[dev]
        tmp[...] = acc
        lb = _hgrn_lb(lb_ref)
        d1 = tmp[4:5, :] * (lb * (1.0 - lb))
        o_ref[...] = jnp.zeros_like(o_ref)
        o_ref[0:4, :] = tmp[0:4, :]
        o_ref[4:5, :] = -d1
        o_ref[5:6, :] = d1
        mine = tmp[5:6, 0:nshard]
        for b in range(1, N_CHIPS):
            mine = jnp.where(c_ref[0] == b, tmp[5:6, b * nshard:(b + 1) * nshard], mine)
        o_ref[6:7, 0:nshard] = mine
        o_ref[7:8, :] = tmp[6:7, :]
        loss_ref[...] = tmp[7:8, 0:128]

    vm = pl.BlockSpec(memory_space=pltpu.VMEM)
    return pl.pallas_call(
        body, name="small_sum", in_specs=[pl.BlockSpec(memory_space=pltpu.SMEM), vm, vm], out_specs=[vm, vm],
        out_shape=[jax.ShapeDtypeStruct((T, W), F32), jax.ShapeDtypeStruct((1, 128), F32)],
        scratch_shapes=[pltpu.VMEM((T, W), F32)],
    )(chip, gathered, hgrn_lb)


def _place():
    return lax.axis_index("x"), lax.axis_index("y"), lax.axis_index("c")


def _other_chips(x, y):
    return [(1 - x, y), (x, 1 - y), (1 - x, 1 - y)]


def _chunk_rows(rows, row_bytes):
    cr = rows
    while cr * row_bytes > STREAM_CHUNK_BYTES and cr % 32 == 0:
        cr //= 2
    return cr


def _stream(pairs, buf, sems, t, peer):
    lsem, ssem, rsem = sems
    n = len(pairs)
    loads, sent = [None] * n, [None] * n

    def load(k):
        slot = k % STREAM_SLOTS
        if k >= STREAM_SLOTS:
            sent[k - STREAM_SLOTS]()
        loads[k] = pltpu.make_async_copy(pairs[k][0], buf.at[slot], lsem.at[t, slot])
        loads[k].start()

    load(0)
    for k in range(n):
        slot = k % STREAM_SLOTS
        if k + 1 < n:
            load(k + 1)
        loads[k].wait()
        cp = pltpu.make_async_remote_copy(src_ref=buf.at[slot], dst_ref=pairs[k][1], send_sem=ssem.at[t, slot],
                                          recv_sem=rsem.at[t], device_id=peer, device_id_type=MESH)
        cp.start()
        sent[k] = cp.wait_send
    for k in range(max(0, n - STREAM_SLOTS), n):
        sent[k]()


def _stream_scratch(shapes):
    nt = len(shapes)
    return ([pltpu.VMEM((STREAM_SLOTS,) + s, d) for s, d in shapes]
            + [pltpu.SemaphoreType.DMA((nt, STREAM_SLOTS)), pltpu.SemaphoreType.DMA((nt, STREAM_SLOTS)),
               pltpu.SemaphoreType.DMA((nt,))])


def _exchange_halves(name, grads):
    nt = len(grads)
    hs = [g.shape[1] // 2 for g in grads]
    crs = [_chunk_rows(h, g.shape[2] * g.dtype.itemsize) for h, g in zip(hs, grads)]

    def body(*refs):
        ins, gots, bufs, sems = refs[:nt], refs[nt:2 * nt], refs[2 * nt:3 * nt], refs[3 * nt:]
        x, y, c = _place()
        sib = (x, y, 1 - c)
        for t in range(nt):
            h, cr = hs[t], crs[t]
            pairs = [(ins[t].at[b, pl.ds((1 - c) * h + r0, cr)], gots[t].at[b, pl.ds(r0, cr)])
                     for b in range(N_CHIPS) for r0 in range(0, h, cr)]
            _stream(pairs, bufs[t], sems, t, sib)
        for t in range(nt):
            pltpu.make_async_remote_copy(src_ref=gots[t], dst_ref=gots[t], send_sem=sems[1].at[t, 0],
                                         recv_sem=sems[2].at[t], device_id=sib, device_id_type=MESH).wait_recv()

    return pl.pallas_call(
        body, name=name, in_specs=[ANY] * nt, out_specs=[ANY] * nt,
        out_shape=[jax.ShapeDtypeStruct((N_CHIPS, h, g.shape[2]), g.dtype) for h, g in zip(hs, grads)],
        scratch_shapes=_stream_scratch([((cr, g.shape[2]), g.dtype) for cr, g in zip(crs, grads)]),
        compiler_params=pltpu.CompilerParams(vmem_limit_bytes=VMEM_LIMIT_BYTES),
    )(*grads)


def _scatter_plan(srcs, dsts):
    x, y, c = _place()
    me = 2 * x + y
    return [(srcs[t].at[2 * px + py], dsts[t].at[me], (px, py, c))
            for t in range(len(srcs)) for px, py in _other_chips(x, y)]


def _slot(dst, chip, r0, rows, cols):
    if len(dst.shape) == 3:
        return dst.at[chip, pl.ds(r0, rows)]
    return dst.at[pl.ds(r0, rows), pl.ds(pl.multiple_of(chip * cols, 128), cols)]


def _shard_dims(gathered):
    s = gathered.shape
    return (s[1], s[2]) if len(s) == 3 else (s[0], s[1] // N_CHIPS)


def _gather_plan(_, bufs):
    x, y, c = _place()
    me = 2 * x + y
    plan = []
    for buf in bufs:
        rows, cols = _shard_dims(buf)
        mine = _slot(buf, me, c * (rows // 2), rows // 2, cols)
        plan += [(mine, mine, (px, py, c)) for px, py in _other_chips(x, y)]
    return plan


def _cast_to_slot(name, chip, w, layer, gathered_shape, after):
    _, R, C = w.shape
    tr, tc = _ew_tiles(R, C)

    def body(c_ref, w_ref, *rest):
        rest[-1][...] = w_ref[...].astype(BF16)

    if len(gathered_shape) == 3:
        out_spec = pl.BlockSpec((None, tr, tc), lambda i, j, c: (c[0], i, j))
    else:
        out_spec = pl.BlockSpec((tr, tc), lambda i, j, c: (i, c[0] * (C // tc) + j))
    extra = [] if after is None else [after]
    return pl.pallas_call(
        body, name=name, out_shape=jax.ShapeDtypeStruct(gathered_shape, BF16),
        grid_spec=pltpu.PrefetchScalarGridSpec(
            num_scalar_prefetch=1, grid=(R // tr, C // tc),
            in_specs=[pl.BlockSpec((None, tr, tc), lambda i, j, c: (layer, i, j))] + [ANY] * len(extra),
            out_specs=out_spec),
        compiler_params=_params("parallel", "parallel"),
    )(chip, w, *extra)


HBM_SPEC = pl.BlockSpec(memory_space=pltpu.HBM)
SEM_SPEC = pl.BlockSpec(memory_space=pltpu.SEMAPHORE)


def _split_start(name, srcs, dsts, plan, ncopies, after):
    bufs = [pltpu.with_memory_space_constraint(a, pltpu.HBM) for a in list(srcs) + list(dsts)]
    nb, ns = len(bufs), len(srcs)
    operands = bufs + ([after] if after is not None else [])

    def body(*refs):
        outs = refs[len(operands):]
        send, recv, token = outs[0], outs[1], outs[-1]
        for i, (src, dst, dev) in enumerate(plan(refs[:ns], refs[ns:nb])):
            pltpu.make_async_remote_copy(src_ref=src, dst_ref=dst, send_sem=send.at[i], recv_sem=recv.at[i],
                                         device_id=dev, device_id_type=MESH).start()
        token[...] = jnp.zeros_like(token)

    res = pl.pallas_call(
        body, name=name,
        out_shape=[pltpu.SemaphoreType.DMA((ncopies,)), pltpu.SemaphoreType.DMA((ncopies,))]
        + [pltpu.HBM(a.shape, a.dtype) for a in bufs] + [jax.ShapeDtypeStruct((8, 128), F32)],
        in_specs=[HBM_SPEC] * nb + [ANY] * (len(operands) - nb),
        out_specs=[SEM_SPEC, SEM_SPEC] + [HBM_SPEC] * nb + [pl.BlockSpec(memory_space=pltpu.VMEM)],
        input_output_aliases={i: 2 + i for i in range(nb)},
        compiler_params=pltpu.CompilerParams(has_side_effects=pltpu.SideEffectType.DATAFLOW_SIDE_EFFECTING),
    )(*operands)
    return res[:-1], res[-1]


def _split_wait(name, started, plan, ns, after):
    send, recv, bufs = started[0], started[1], list(started[2:])
    nb = len(bufs)

    def body(*refs):
        send_ref, recv_ref = refs[nb], refs[nb + 1]
        for i, (src, dst, dev) in enumerate(plan(refs[:ns], refs[ns:nb])):
            cp = pltpu.make_async_remote_copy(src_ref=src, dst_ref=dst, send_sem=send_ref.at[i], recv_sem=recv_ref.at[i],
                                              device_id=dev, device_id_type=MESH)
            cp.wait_send()
            cp.wait_recv()

    res = pl.pallas_call(
        body, name=name, out_shape=[pltpu.HBM(a.shape, a.dtype) for a in bufs],
        in_specs=[HBM_SPEC] * nb + [SEM_SPEC, SEM_SPEC, ANY], out_specs=[HBM_SPEC] * nb,
        input_output_aliases={i: i for i in range(nb)},
        compiler_params=pltpu.CompilerParams(has_side_effects=pltpu.SideEffectType.DATAFLOW_SIDE_EFFECTING),
    )(*bufs, send, recv, after)
    return res[:ns], res[ns:]


def _gather_finish(name, _, gathered):
    nt = len(gathered)
    dims = [_shard_dims(g) for g in gathered]
    hs = [rows // 2 for rows, _ in dims]
    crs = [_chunk_rows(h, cols * 2) for h, (_, cols) in zip(hs, dims)]

    def body(*refs):
        outs, bufs, sems = refs[nt:2 * nt], refs[2 * nt:3 * nt], refs[3 * nt:]
        x, y, c = _place()
        sib = (x, y, 1 - c)
        for t in range(nt):
            h, cr, cols = hs[t], crs[t], dims[t][1]
            passed = [_slot(outs[t], 2 * px + py, c * h + r0, cr, cols)
                      for px, py in _other_chips(x, y) for r0 in range(0, h, cr)]
            _stream([(r, r) for r in passed], bufs[t], sems, t, sib)
        for t in range(nt):
            if len(gathered[t].shape) == 3:
                three = outs[t].at[pl.ds(0, 3), pl.ds(0, hs[t])]
            else:
                three = outs[t].at[pl.ds(0, hs[t]), pl.ds(0, 3 * dims[t][1])]
            pltpu.make_async_remote_copy(src_ref=three, dst_ref=three, send_sem=sems[1].at[t, 0],
                                         recv_sem=sems[2].at[t], device_id=sib, device_id_type=MESH).wait_recv()

    return pl.pallas_call(
        body, name=name, in_specs=[ANY] * nt, out_specs=[ANY] * nt,
        out_shape=[jax.ShapeDtypeStruct(g.shape, g.dtype) for g in gathered],
        scratch_shapes=_stream_scratch([((cr, cols), BF16) for cr, (_, cols) in zip(crs, dims)]),
        input_output_aliases={t: t for t in range(nt)},
        compiler_params=pltpu.CompilerParams(vmem_limit_bytes=VMEM_LIMIT_BYTES),
    )(*gathered)


def _sum_share(landed):
    nt = len(landed)
    hs = [a.shape[1] for a in landed]
    cs = [a.shape[2] for a in landed]
    crs = [_chunk_rows(h, 2 * c * 4) for h, c in zip(hs, cs)]
    shapes = sorted(set(zip(crs, cs)))
    which = [shapes.index(s) for s in zip(crs, cs)]

    def body(*refs):
        ins, outs = refs[:nt], refs[nt:2 * nt]
        inbufs, outbufs = refs[2 * nt:2 * nt + len(shapes)], refs[2 * nt + len(shapes):2 * nt + 2 * len(shapes)]
        lsem, ssem, osem, rsem = refs[2 * nt + 2 * len(shapes):]
        x, y, c = _place()
        sib = (x, y, 1 - c)
        for t in range(nt):
            cr, n, ib, ob = crs[t], hs[t] // crs[t], inbufs[which[t]], outbufs[which[t]]
            loads, gone = [None] * n, [None] * n

            def load(k):
                slot = k % SUM_SLOTS
                if k >= SUM_SLOTS:
                    for cp_wait in gone[k - SUM_SLOTS]:
                        cp_wait()
                loads[k] = pltpu.make_async_copy(ins[t].at[:, pl.ds(k * cr, cr)], ib.at[slot], lsem.at[t, slot])
                loads[k].start()

            load(0)
            for k in range(n):
                slot = k % SUM_SLOTS
                if k + 1 < n:
                    load(k + 1)
                loads[k].wait()
                acc = ib[slot, 0].astype(F32)
                for s in range(1, N_CHIPS):
                    acc = acc + ib[slot, s].astype(F32)
                ob[slot] = acc
                rows = outs[t].at[c, pl.ds(k * cr, cr)]
                away = pltpu.make_async_remote_copy(src_ref=ob.at[slot], dst_ref=rows, send_sem=ssem.at[t, slot],
                                                    recv_sem=rsem.at[t], device_id=sib, device_id_type=MESH)
                away.start()
                home = pltpu.make_async_copy(ob.at[slot], rows, osem.at[t, slot])
                home.start()
                gone[k] = (away.wait_send, home.wait)
            for k in range(max(0, n - SUM_SLOTS), n):
                for cp_wait in gone[k]:
                    cp_wait()
        for t in range(nt):
            other = outs[t].at[1 - c]
            pltpu.make_async_remote_copy(src_ref=other, dst_ref=other, send_sem=ssem.at[t, 0], recv_sem=rsem.at[t],
                                         device_id=sib, device_id_type=MESH).wait_recv()

    slot_sems = pltpu.SemaphoreType.DMA((nt, SUM_SLOTS))
    return pl.pallas_call(
        body, name="sum_share", in_specs=[ANY] * nt, out_specs=[ANY] * nt,
        out_shape=[jax.ShapeDtypeStruct((2, h, c), F32) for h, c in zip(hs, cs)],
        scratch_shapes=[pltpu.VMEM((SUM_SLOTS, N_CHIPS, cr, c), BF16) for cr, c in shapes]
        + [pltpu.VMEM((SUM_SLOTS, cr, c), F32) for cr, c in shapes]
        + [slot_sems, slot_sems, slot_sems, pltpu.SemaphoreType.DMA((nt,))],
        compiler_params=pltpu.CompilerParams(vmem_limit_bytes=VMEM_LIMIT_BYTES),
    )(*landed)


def _allgather_small(name, v):
    def body(v_ref, o_ref, send, recv, lsem):
        x, y, c = _place()
        me = 4 * x + 2 * y + c
        loc = pltpu.make_async_copy(v_ref, o_ref.at[me], lsem)
        loc.start()
        copies = []
        for k in range(1, N_DEV):
            px = 1 - x if k & 4 else x
            py = 1 - y if k & 2 else y
            pc = 1 - c if k & 1 else c
            cp = pltpu.make_async_remote_copy(
                src_ref=v_ref, dst_ref=o_ref.at[me], send_sem=send.at[k - 1], recv_sem=recv.at[k - 1],
                device_id=(px, py, pc), device_id_type=MESH)
            cp.start()
            copies.append(cp)
        for cp in copies:
            cp.wait()
        loc.wait()

    vm = pl.BlockSpec(memory_space=pltpu.VMEM)
    return pl.pallas_call(
        body, name=name, in_specs=[vm], out_specs=vm,
        out_shape=jax.ShapeDtypeStruct((N_DEV,) + v.shape, v.dtype),
        scratch_shapes=[pltpu.SemaphoreType.DMA((N_DEV - 1,))] * 2 + [pltpu.SemaphoreType.DMA],
    )(v)


def kernel(x, mem, norm_g, mem_norm_g, w_kv, w_out, pool_w_in, pool_w_grp, pool_scale, hgrn_w_in, hgrn_lb, hgrn_norm_g, final_g, loss_target, m_norm_g, m_mem_norm_g, m_w_kv, m_w_out, m_pool_w_in, m_pool_w_grp, m_pool_scale, m_hgrn_w_in, m_hgrn_lb, m_hgrn_norm_g, m_final_g, v_norm_g, v_mem_norm_g, v_w_kv, v_w_out, v_pool_w_in, v_pool_w_grp, v_pool_scale, v_hgrn_w_in, v_hgrn_lb, v_hgrn_norm_g, v_final_g):
    _, S, D = x.shape
    M = mem.shape[1]
    EB = 2 * D
    ECA = EB // 4
    EMIX = EB - ECA
    PG = EMIX // N_POOL_GROUPS
    NP0 = EMIX + ECA + EB
    NP1 = 3 * EMIX + ECA + EB
    SH0, SH1 = NP0 // N_CHIPS, NP1 // N_CHIPS
    DK, EK = D // N_CHIPS, EB // N_CHIPS
    TNP = 512 if all(v % 512 == 0 for v in (SH0, SH1, ECA, EMIX)) else 256
    TM = _tile(S, 1024)
    TMF = _tile(S, 2048)
    TD = _tile(D, 512)
    TDW = _tile(D, 1024)
    c0, c1 = SH0 // TNP, SH1 // TNP
    qt = EMIX // TNP
    chip = 2 * lax.axis_index("x") + lax.axis_index("y")

    xs, ms, tgt = x[0], mem[0], loss_target[0]

    sds = jax.ShapeDtypeStruct
    chip1 = chip.astype(jnp.int32).reshape(1)

    def start_gather(tag, layers, after):
        bufs = []
        for t, (w, layer) in enumerate(layers):
            w3 = w.reshape((w.shape[0], -1, w.shape[-1]))
            shape = (D, NP1) if w is hgrn_w_in else (N_CHIPS,) + w3.shape[1:]
            bufs.append(_cast_to_slot(f"cast_{tag}{t}", chip1, w3, layer, shape, after))
        return _split_start(f"gather_{tag}_start", [], bufs, _gather_plan, 3 * len(bufs), after)

    gather_a, token = start_gather("a", [(pool_w_in, 0)], None)
    gather_b, token = start_gather("b", [(w_kv, 0), (w_out, 0), (pool_w_grp, 0)], token)
    gather_c, token = start_gather("c", [(hgrn_w_in, 0)], token)
    gather_d, token = start_gather("d", [(w_kv, 1), (w_out, 1)], token)

    tek, tew = _tile(EK, 512), _tile(EK, 1024)

    mem_n = _rms_fwd("rms_mem", ms, mem_norm_g.reshape(1, D), token)
    h0 = _rms_fwd("rms0", xs, norm_g[0:1], token)
    wpin, = _gather_finish("gather_a_finish", *_split_wait("gather_a_wait", gather_a, _gather_plan, 0, h0))

    tkw = _tile(2 * ECA, 1024)

    def kv_of(layer, wkv):
        return _matmul(
            f"kv{layer}", mem_n, wkv.reshape(D, 2 * ECA), grid=(1, 2 * ECA // tkw, 1),
            a_spec=pl.BlockSpec((M, D), lambda i, j, k: (0, 0)), b_spec=pl.BlockSpec((D, tkw), lambda i, j, k: (0, j)),
            out_shape=sds((M, 2 * ECA), BF16), out_spec=pl.BlockSpec((M, tkw), lambda i, j, k: (0, j)),
            acc_shape=(M, tkw), dims=NN)

    tko = _tile(EB, 2048)

    def out_proj(layer, branch, wout, resid):
        return _matmul(
            f"out_proj{layer}", branch, wout.reshape(EB, D), grid=(S // TM, D // TDW, EB // tko),
            a_spec=pl.BlockSpec((TM, tko), IK), b_spec=pl.BlockSpec((tko, TDW), KJ),
            out_shape=sds((S, D), F32), out_spec=pl.BlockSpec((TM, TDW), IJ),
            acc_shape=(TM, TDW), dims=NN, add=resid, add_spec=pl.BlockSpec((TM, TDW), IJ))

    ones_ca = jnp.ones((1, ECA), F32)

    proj0 = _matmul(
        "proj0", h0, wpin, grid=(S // TMF, NP0 // TNP, 1),
        a_spec=pl.BlockSpec((TMF, D), lambda i, j, k: (i, 0)),
        b_spec=pl.BlockSpec((None, D, TNP), lambda i, j, k: (j // c0, 0, j % c0)),
        out_shape=sds((S, NP0), BF16), out_spec=pl.BlockSpec((TMF, TNP), IJ),
        acc_shape=(TMF, TNP), dims=NN)
    pooled = _pool_fwd(proj0, S, EMIX)
    wkv0, wout0, g_grp = _gather_finish("gather_b_finish", *_split_wait("gather_b_wait", gather_b, _gather_plan, 0, pooled))
    wgrp = g_grp.reshape(N_CHIPS, N_POOL_GROUPS, PG // N_CHIPS, PG).transpose(1, 0, 2, 3).reshape(N_POOL_GROUPS, PG, PG)
    kv = [kv_of(0, wkv0), None]
    premix0 = _matmul(
        "pool_grp", pooled, wgrp, grid=(S // TM, N_POOL_GROUPS, 1),
        a_spec=pl.BlockSpec((TM, PG), lambda i, j, k: (i, j)),
        b_spec=pl.BlockSpec((None, PG, PG), lambda i, j, k: (j, 0, 0)),
        out_shape=sds((S, EB), BF16), out_spec=pl.BlockSpec((TM, PG), lambda i, j, k: (i, j)),
        acc_shape=(TM, PG), dims=NN)
    premix0 = _ca_fwd("ca_fwd0", proj0, EMIX // ECA, kv[0], premix0, S, ECA, EMIX)
    colscale0 = jnp.concatenate([pool_scale.reshape(1, EMIX), ones_ca], axis=1)
    gblk0 = (EMIX + ECA) // ECA
    branch0 = _gate_fwd("gate_fwd0", premix0, proj0, gblk0, colscale0, S, EB, ECA)
    x1 = out_proj(0, branch0, wout0, xs)

    whin, = _gather_finish("gather_c_finish", *_split_wait("gather_c_wait", gather_c, _gather_plan, 0, x1))
    h1 = _rms_fwd("rms1", x1, norm_g[1:2])

    def proj1_cols(name, ncols, col_of, out_cols, out_dtype, out_col_of):
        return _matmul(
            name, h1, whin, grid=(S // TMF, ncols, 1),
            a_spec=pl.BlockSpec((TMF, D), lambda i, j, k: (i, 0)),
            b_spec=pl.BlockSpec((D, TNP), lambda i, j, k: (0, col_of(j))),
            out_shape=sds((S, out_cols), out_dtype), out_spec=pl.BlockSpec((TMF, TNP), lambda i, j, k: (i, out_col_of(j))),
            acc_shape=(TMF, TNP), dims=NN)

    skip_f = lambda j: jnp.where(j < qt, j, j + qt)
    proj1 = proj1_cols("proj1", NP1 // TNP - qt, skip_f, NP1, BF16, skip_f)
    fgate = proj1_cols("proj1_f", qt, lambda j: j + qt, EMIX, F32, lambda j: j)
    premix1, rstd1, states = _hgrn_fwd(proj1, fgate, hgrn_lb, S, EMIX, EB)
    wkv1, wout1 = _gather_finish("gather_d_finish", *_split_wait("gather_d_wait", gather_d, _gather_plan, 0, rstd1))
    kv[1] = kv_of(1, wkv1)
    premix1 = _ca_fwd("ca_fwd1", proj1, 3 * EMIX // ECA, kv[1], premix1, S, ECA, EMIX)
    norm_tiles = _allgather_small("allgather_norm_g", jnp.pad(hgrn_norm_g, ((0, SMALL_ROWS - 1), (0, 0))))
    hg_norm = norm_tiles[0::2, 0, :].reshape(1, EMIX)
    colscale1 = jnp.concatenate([hg_norm, ones_ca], axis=1)
    gblk1 = (3 * EMIX + ECA) // ECA
    branch1 = _gate_fwd("gate_fwd1", premix1, proj1, gblk1, colscale1, S, EB, ECA)
    x2 = out_proj(1, branch1, wout1, x1)

    dx2, dx2b, d_final_g, loss_part = _loss_head(x2, final_g.reshape(1, D), tgt)

    def out_proj_bwd(layer, dxb, branch, wout, premix, proj, gblk, colscale, dshape, dblk):
        goff, doff = gblk * ECA // tek, dblk * ECA // tek
        dpremix, dgate, dcol = _matmul(
            f"dbranch{layer}", dxb, wout, grid=(S // TMF, EB // tek, 1),
            a_spec=pl.BlockSpec((TMF, D), lambda i, j, k: (i, 0)),
            b_spec=pl.BlockSpec((None, tek, D), lambda i, j, k: (j // (EK // tek), j % (EK // tek), 0)),
            extras=[(premix, pl.BlockSpec((TMF, tek), IJ)), (proj, pl.BlockSpec((TMF, tek), lambda i, j, k: (i, goff + j))),
                    (colscale, pl.BlockSpec((1, tek), lambda i, j, k: (0, j)))],
            epilogue=_gate_bwd_epilogue,
            out_shape=[sds((S, EB), BF16), sds(dshape, BF16), sds((S // TMF, 1, EB), F32)],
            out_spec=[pl.BlockSpec((TMF, tek), IJ), pl.BlockSpec((TMF, tek), lambda i, j, k: (i, doff + j)),
                      pl.BlockSpec((None, 1, tek), lambda i, j, k: (i, 0, j))],
            acc_shape=(TMF, tek), dims=NT)
        dw = _matmul(
            f"dwout{layer}", branch, dxb, grid=(EB // tew, D // TD, 1),
            a_spec=pl.BlockSpec((S, tew), lambda i, j, k: (0, i)), b_spec=pl.BlockSpec((S, TD), lambda i, j, k: (0, j)),
            out_shape=sds((N_CHIPS, EK, D), BF16),
            out_spec=pl.BlockSpec((None, tew, TD), lambda i, j, k: (i // (EK // tew), i % (EK // tew), j)),
            acc_shape=(tew, TD), dims=TN)
        return dpremix, dgate, dcol.reshape(S // TMF, EB), dw

    def kv_bwd(layer, dkv, wkv, dmem_add):
        dkvb = dkv.astype(BF16)
        dmem = _matmul(
            f"dmem{layer}", dkvb, wkv.reshape(D, 2 * ECA), grid=(1, D // TDW, 1),
            a_spec=pl.BlockSpec((M, 2 * ECA), lambda i, j, k: (0, 0)),
            b_spec=pl.BlockSpec((TDW, 2 * ECA), lambda i, j, k: (j, 0)),
            out_shape=sds((M, D), F32), out_spec=pl.BlockSpec((M, TDW), lambda i, j, k: (0, j)), acc_shape=(M, TDW),
            dims=NT, add=dmem_add, add_spec=pl.BlockSpec((M, TDW), lambda i, j, k: (0, j)))
        dw = _matmul(
            f"dwkv{layer}", mem_n, dkvb, grid=(D // TDW, 2 * ECA // tkw, 1),
            a_spec=pl.BlockSpec((M, TDW), lambda i, j, k: (0, i)), b_spec=pl.BlockSpec((M, tkw), lambda i, j, k: (0, j)),
            out_shape=sds((D, 2 * ECA), BF16), out_spec=pl.BlockSpec((TDW, tkw), IJ), acc_shape=(TDW, tkw), dims=TN)
        return dmem, dw.reshape(N_CHIPS, DK, 2 * ECA)

    dpremix1, drest1, dcol1, gw_out1 = out_proj_bwd(1, dx2b, branch1, wout1, premix1, proj1, gblk1, colscale1,
                                                    (S, ECA + EB), 1)
    drest1, dkv1 = _ca_bwd("ca_bwd1", dpremix1, proj1, 3 * EMIX // ECA, kv[1], drest1, 0, S, ECA, EMIX)
    dqfi, dlb = _hgrn_bwd(dpremix1, premix1, rstd1, states, proj1, fgate, hgrn_lb, S, EMIX)
    nq, nr = 3 * qt, (ECA + EB) // TNP
    tkh = _tile(EMIX, 1024) if (ECA + EB) % _tile(EMIX, 1024) == 0 else TNP
    kq = EMIX // tkh
    dh1 = _matmul(
        "dh1_qfi", dqfi, whin, grid=(S // TM, D // TDW, 3),
        a_spec=pl.BlockSpec((None, TM, EMIX), lambda i, j, k: (k, i, 0)),
        b_spec=pl.BlockSpec((TDW, EMIX), lambda i, j, k: (j, k)),
        out_shape=sds((S, D), F32), out_spec=pl.BlockSpec((TM, TDW), IJ), acc_shape=(TM, TDW), dims=NT)
    dh1 = _matmul(
        "dh1_rest", drest1, whin, grid=(S // TM, D // TDW, (ECA + EB) // tkh), a_spec=pl.BlockSpec((TM, tkh), IK),
        b_spec=pl.BlockSpec((TDW, tkh), lambda i, j, k: (j, k + 3 * kq)),
        out_shape=sds((S, D), F32), out_spec=pl.BlockSpec((TM, TDW), IJ), acc_shape=(TM, TDW), dims=NT,
        add=dh1, add_spec=pl.BlockSpec((TM, TDW), IJ))
    gw_hin = _matmul(
        "dwhin_qfi", h1, dqfi, grid=(D // TDW, nq, 1), a_spec=pl.BlockSpec((S, TDW), lambda i, j, k: (0, i)),
        b_spec=pl.BlockSpec((None, S, TNP), lambda i, j, k: (j // qt, 0, j % qt)),
        out_shape=sds((N_CHIPS, D, SH1), BF16), out_spec=pl.BlockSpec((None, TDW, TNP), lambda i, j, k: (j // c1, i, j % c1)),
        acc_shape=(TDW, TNP), dims=TN)
    gw_hin = _matmul(
        "dwhin_rest", h1, drest1, grid=(D // TDW, nr, 1), a_spec=pl.BlockSpec((S, TDW), lambda i, j, k: (0, i)),
        b_spec=pl.BlockSpec((S, TNP), lambda i, j, k: (0, j)), out_shape=sds((N_CHIPS, D, SH1), BF16),
        out_spec=pl.BlockSpec((None, TDW, TNP), lambda i, j, k: ((j + nq) // c1, i, (j + nq) % c1)),
        acc_shape=(TDW, TNP), dims=TN, alias=gw_hin)
    dmem, gw_kv1 = kv_bwd(1, dkv1, wkv1, None)

    core_chip = jnp.stack([lax.axis_index("c"), chip]).astype(jnp.int32)

    def reduce_in_chip(tag, stacks):
        got = _exchange_halves(f"exchange_halves{tag}", stacks)
        pairs = [_add_halves(f"add_halves{tag}_{t}", core_chip, g.reshape(N_CHIPS, 2, g.shape[1] // 2, g.shape[2]), r)
                 for t, (g, r) in enumerate(zip(stacks, got))]
        return [p for p, _ in pairs], [own for _, own in pairs]

    parts1, landed1 = reduce_in_chip(1, [gw_kv1, gw_out1, gw_hin])
    scatter1, token1 = _split_start("scatter1_start", parts1, landed1, _scatter_plan, 3 * len(parts1), None)
    dx1, dx1b, d_ng1 = _rms_bwd("rms_bwd1", dh1, x1, norm_g[1:2], dx2, token1)

    dpremix0, dproj0, dcol0, gw_out0 = out_proj_bwd(0, dx1b, branch0, wout0, premix0, proj0, gblk0, colscale0,
                                                    (S, NP0), gblk0)
    dproj0, dkv0 = _ca_bwd("ca_bwd0", dpremix0, proj0, EMIX // ECA, kv[0], dproj0, EMIX // ECA, S, ECA, EMIX)
    dmem, gw_kv0 = kv_bwd(0, dkv0, wkv0, dmem)
    parts_a, landed_a = reduce_in_chip("0a", [gw_kv0, gw_out0])
    scatter_a, token_a = _split_start("scatter0a_start", parts_a, landed_a, _scatter_plan, 3 * len(parts_a), None)
    dpooled = _matmul(
        "dpooled", dpremix0, wgrp, grid=(S // TM, N_POOL_GROUPS, 1), a_spec=pl.BlockSpec((TM, PG), IJ),
        b_spec=pl.BlockSpec((None, PG, PG), lambda i, j, k: (j, 0, 0)),
        out_shape=sds((S, EMIX), F32), out_spec=pl.BlockSpec((TM, PG), IJ), acc_shape=(TM, PG), dims=NT, after=token_a)
    def rows_by_chip(r, _, outs):
        outs[0][...] = r.reshape(N_CHIPS, PG // N_CHIPS, PG).astype(BF16)

    gw_grp, = _matmul(
        "dwgrp", pooled, dpremix0, grid=(N_POOL_GROUPS, 1, 1), a_spec=pl.BlockSpec((S, PG), lambda i, j, k: (0, i)),
        b_spec=pl.BlockSpec((S, PG), lambda i, j, k: (0, i)), epilogue=rows_by_chip,
        out_shape=[sds((N_CHIPS, N_POOL_GROUPS, PG // N_CHIPS, PG), BF16)],
        out_spec=[pl.BlockSpec((N_CHIPS, None, PG // N_CHIPS, PG), lambda i, j, k: (0, i, 0, 0))],
        acc_shape=(PG, PG), dims=TN)
    dproj0 = _pool_bwd(dpooled, dproj0, S, EMIX)
    gw_pin = _matmul(
        "dwpin", h0, dproj0, grid=(D // TDW, NP0 // TNP, 1), a_spec=pl.BlockSpec((S, TDW), lambda i, j, k: (0, i)),
        b_spec=pl.BlockSpec((S, TNP), lambda i, j, k: (0, j)), out_shape=sds((N_CHIPS, D, SH0), BF16),
        out_spec=pl.BlockSpec((None, TDW, TNP), lambda i, j, k: (j // c0, i, j % c0)), acc_shape=(TDW, TNP), dims=TN)
    parts_b, landed_b = reduce_in_chip("0b", [gw_pin, gw_grp.reshape(N_CHIPS, PG, PG)])
    scatter_b, token_b = _split_start("scatter0b_start", parts_b, landed_b, _scatter_plan, 3 * len(parts_b), None)
    dh0 = _matmul(
        "dh0", dproj0, wpin, grid=(S // TM, D // TDW, N_CHIPS), a_spec=pl.BlockSpec((TM, SH0), IK),
        b_spec=pl.BlockSpec((None, TDW, SH0), lambda i, j, k: (k, j, 0)),
        out_shape=sds((S, D), F32), out_spec=pl.BlockSpec((TM, TDW), IJ), acc_shape=(TM, TDW), dims=NT, after=token_b)
    grad_x, _, d_ng0 = _rms_bwd("rms_bwd0", dh0, xs, norm_g[0:1], dx1)
    _, _, d_mng = _rms_bwd("rms_bwd_mem", dmem, ms, mem_norm_g.reshape(1, D), jnp.zeros_like(ms))

    _, landed1 = _split_wait("scatter1_wait", scatter1, _scatter_plan, len(parts1), grad_x)
    _, landed_a = _split_wait("scatter0a_wait", scatter_a, _scatter_plan, len(parts_a), grad_x)
    _, landed_b = _split_wait("scatter0b_wait", scatter_b, _scatter_plan, len(parts_b), grad_x)
    landed = [landed_a[0], landed1[0], landed_a[1], landed1[1], landed_b[0], landed_b[1], landed1[2]]
    fulls = _sum_share(landed)
    f2 = [f.reshape(-1, f.shape[-1]) for f in fulls]
    grads, deltas, new_m, new_v = {}, {}, {}, {}
    for n, w, mm, vv, gs in (("w_kv", w_kv, m_w_kv, v_w_kv, f2[0:2]), ("w_out", w_out, m_w_out, v_w_out, f2[2:4]),
                             ("pool_w_in", pool_w_in, m_pool_w_in, v_pool_w_in, f2[4:5]),
                             ("pool_w_grp", pool_w_grp, m_pool_w_grp, v_pool_w_grp, f2[5:6]),
                             ("hgrn_w_in", hgrn_w_in, m_hgrn_w_in, v_hgrn_w_in, f2[6:7])):
        as3d = lambda a: a.reshape((a.shape[0], -1, a.shape[-1]))
        outs = _adamw_layers(f"adamw_{n}", as3d(w), gs, as3d(mm), as3d(vv))
        grads[n], deltas[n], new_m[n], new_v[n] = [o.reshape(w.shape) for o in outs]

    Wd = EMIX
    nshard = EMIX // N_CHIPS
    summed_rows = [(v, True, v.shape[1]) for v in (d_ng0, d_ng1, d_mng)] + [
        (dcol0, True, EMIX), (dlb, True, EMIX), (dcol1, True, EMIX), (d_final_g, True, D), (loss_part, True, 128)]
    partial = _pack_rows("pack_partials", summed_rows, Wd)
    g_pack, loss = _small_sum(_allgather_small("allgather_grads", partial), hgrn_lb, chip1, nshard)

    def pack_small(name, ng, mng, ps, lb_, hn, fg):
        return _pack_rows(name, [(ng, False, D), (mng.reshape(1, D), False, D), (ps, False, EMIX), (lb_, False, EMIX),
                                 (hn, False, nshard), (fg.reshape(1, D), False, D)], Wd)

    d_pack, m_pack, v_pack = _adamw(
        "adamw_small", pack_small("pack_small_w", norm_g, mem_norm_g, pool_scale, hgrn_lb, hgrn_norm_g, final_g), g_pack,
        pack_small("pack_small_m", m_norm_g, m_mem_norm_g, m_pool_scale, m_hgrn_lb, m_hgrn_norm_g, m_final_g),
        pack_small("pack_small_v", v_norm_g, v_mem_norm_g, v_pool_scale, v_hgrn_lb, v_hgrn_norm_g, v_final_g))
    layout = [(2, D), (1, D), (1, EMIX), (2, EMIX), (1, nshard), (1, D)]
    for tag, pack, out in (("g", g_pack, grads), ("d", d_pack, deltas), ("m", m_pack, new_m), ("v", v_pack, new_v)):
        ng, mng, ps, lb_, hn, fg = _unpack_rows(f"unpack_small_{tag}", pack, layout)
        out.update(norm_g=ng, mem_norm_g=mng.reshape(D), pool_scale=ps, hgrn_lb=lb_, hgrn_norm_g=hn, final_g=fg.reshape(D))
    loss = loss[0, 0]

    order = ["norm_g", "mem_norm_g", "w_kv", "w_out", "pool_w_in", "pool_w_grp", "pool_scale", "hgrn_w_in", "hgrn_lb",
             "hgrn_norm_g", "final_g"]
    return (loss, grad_x.reshape(1, S, D), *[grads[n] for n in order], *[deltas[n] for n in order],
            *[new_m[n] for n in order], *[new_v[n] for n in order])
```
